```python
import math
import jax, jax.numpy as jnp
from jax import lax
import numpy as np

D_MODEL = 1024
BATCH = 8
SEQ = 4096
DEPTH = 4

CHUNK = 64
S5_WIDTH = 512
S5_GROUP = 16
S5_GROUPS = S5_WIDTH // S5_GROUP
S5_STATE = 64
DT_MIN = 1e-3
DT_MAX = 1e-1
MLA_HEADS = 8
QK_NOPE = 64
QK_ROPE = 32
V_HEAD = 64
Q_LORA = 384
KV_LORA = 256
ROPE_THETA = 10000.0
Q_BLOCK = 128
SGU_WIDTH = 512
SGU_GROUPS = 4
SGU_CHUNK = 128
N_BRANCH = 3
BRANCH_WIDTH = 512
FF_HIDDEN = -(-8 * D_MODEL // (3 * 256)) * 256
DEEPNORM_ALPHA = (2 * DEPTH) ** 0.25
DEEPNORM_BETA = (8 * DEPTH) ** -0.25
LN_EPS = 1e-5
RMS_EPS = 1e-6
NEG_INF = -1e30
IN_WIDTHS = (S5_WIDTH, Q_LORA, KV_LORA, QK_ROPE, SGU_WIDTH, SGU_WIDTH, N_BRANCH * D_MODEL)
IN_OFFSETS = tuple(int(o) for o in np.cumsum(IN_WIDTHS)[:-1])
IN_WIDTH = sum(IN_WIDTHS)

kernel_name = 'hybrid_s5_mla_sgu_deepnorm_adaln'


def layer_norm(x, g, b):
    xf = x.astype(jnp.float32)
    mu = jnp.mean(xf, axis=-1, keepdims=True)
    var = jnp.mean(jnp.square(xf - mu), axis=-1, keepdims=True)
    return ((xf - mu) * lax.rsqrt(var + LN_EPS)).astype(x.dtype) * g + b


def rms_norm(x, g):
    xf = x.astype(jnp.float32)
    return (xf * lax.rsqrt(jnp.mean(xf * xf, axis=-1, keepdims=True) + RMS_EPS)).astype(x.dtype) * g


def rope(x, cos, sin):
    x1, x2 = jnp.split(x, 2, axis=-1)
    return jnp.concatenate([x1 * cos - x2 * sin, x2 * cos + x1 * sin], axis=-1)


def _complex_affine_combine(left, right):
    ar1, ai1, br1, bi1 = left
    ar2, ai2, br2, bi2 = right
    return (ar2 * ar1 - ai2 * ai1,
            ar2 * ai1 + ai2 * ar1,
            ar2 * br1 - ai2 * bi1 + br2,
            ar2 * bi1 + ai2 * br1 + bi2)


def s5_mixer(u, lam_re, lam_im, log_dt, b_re, b_im, c_re, c_im, d, w_glu, b_glu):
    f32 = jnp.float32
    bsz, seq, _ = u.shape
    uf = u.astype(f32)
    ug = uf.reshape(bsz, seq, S5_GROUPS, S5_GROUP)
    dt = jnp.exp(log_dt.astype(f32))[:, None]
    lr = lam_re.astype(f32)
    li = lam_im.astype(f32)
    mag = jnp.exp(lr * dt)
    a_re = mag * jnp.cos(li * dt)
    a_im = mag * jnp.sin(li * dt)
    den = lr * lr + li * li
    f_re = ((a_re - 1.0) * lr + a_im * li) / den
    f_im = (a_im * lr - (a_re - 1.0) * li) / den
    br = b_re.astype(f32)
    bi = b_im.astype(f32)
    bb_re = f_re[..., None] * br - f_im[..., None] * bi
    bb_im = f_re[..., None] * bi + f_im[..., None] * br
    bu_re = jnp.einsum('bsgc,gpc->bsgp', ug, bb_re)
    bu_im = jnp.einsum('bsgc,gpc->bsgp', ug, bb_im)
    a_re_t = jnp.broadcast_to(a_re, (1, seq) + a_re.shape)
    a_im_t = jnp.broadcast_to(a_im, (1, seq) + a_im.shape)
    _, _, h_re, h_im = lax.associative_scan(
        _complex_affine_combine, (a_re_t, a_im_t, bu_re, bu_im), axis=1)
    y = (jnp.einsum('bsgp,gcp->bsgc', h_re, c_re.astype(f32))
         - jnp.einsum('bsgp,gcp->bsgc', h_im, c_im.astype(f32)))
    y = y.reshape(bsz, seq, S5_WIDTH) + d.astype(f32) * uf
    z = jax.nn.gelu(y)
    out = z * jax.nn.sigmoid(z @ w_glu.astype(f32) + b_glu.astype(f32))
    return out.astype(u.dtype)


def mla_mixer(cq, ckv, k_pe, q_norm, w_q_up, kv_norm, w_kv_up, cos, sin):
    bsz, seq, _ = cq.shape
    q = (rms_norm(cq, q_norm) @ w_q_up).reshape(bsz, seq, MLA_HEADS, QK_NOPE + QK_ROPE)
    q_nope = q[..., :QK_NOPE]
    q_pe = rope(q[..., QK_NOPE:], cos[:, None, :], sin[:, None, :])
    kv = (rms_norm(ckv, kv_norm) @ w_kv_up).reshape(bsz, seq, MLA_HEADS, QK_NOPE + V_HEAD)
    k_nope = kv[..., :QK_NOPE]
    v = kv[..., QK_NOPE:]
    k_pe = rope(k_pe, cos, sin)
    n_blk = seq // Q_BLOCK
    scale = (QK_NOPE + QK_ROPE) ** -0.5
    q_nope_b = q_nope.reshape(bsz, n_blk, Q_BLOCK, MLA_HEADS, QK_NOPE).transpose(1, 0, 2, 3, 4)
    q_pe_b = q_pe.reshape(bsz, n_blk, Q_BLOCK, MLA_HEADS, QK_ROPE).transpose(1, 0, 2, 3, 4)
    key_chunk = jnp.arange(seq) // CHUNK

    def attend_block(args):
        blk, qn, qp = args
        s = (jnp.einsum('bqhd,bkhd->bhqk', qn, k_nope)
             + jnp.einsum('bqhr,bkr->bhqk', qp, k_pe))
        s = s.astype(jnp.float32) * scale
        q_chunk = (blk * Q_BLOCK + jnp.arange(Q_BLOCK)) // CHUNK
        mask = key_chunk[None, :] <= q_chunk[:, None]
        s = jnp.where(mask, s, NEG_INF)
        p = jax.nn.softmax(s, axis=-1).astype(v.dtype)
        return jnp.einsum('bhqk,bkhd->bqhd', p, v)

    o = lax.map(attend_block, (jnp.arange(n_blk), q_nope_b, q_pe_b))
    return o.transpose(1, 0, 2, 3, 4).reshape(bsz, seq, MLA_HEADS * V_HEAD)


def sgu_mixer(u, v, ln_g, ln_b, w_s, b_s):
    bsz, seq, _ = u.shape
    u = jax.nn.gelu(u)
    v = layer_norm(jax.nn.gelu(v), ln_g, ln_b)
    n_chunk = seq // SGU_CHUNK
    vg = v.reshape(bsz, n_chunk, SGU_CHUNK, SGU_GROUPS, SGU_WIDTH // SGU_GROUPS)
    pos_chunk = jnp.arange(SGU_CHUNK) // CHUNK
    mask = pos_chunk[None, :] <= pos_chunk[:, None]
    w = jnp.where(mask[None], w_s, 0.0)
    mixed = jnp.einsum('gij,bnjgc->bnigc', w, vg) + b_s.T[:, :, None]
    return u * mixed.reshape(bsz, seq, SGU_WIDTH)


def hybrid_mixer(h, w_in, b_in, lam_re, lam_im, log_dt, b_re, b_im, c_re, c_im, d, w_glu, b_glu,
                 q_norm, w_q_up, kv_norm, w_kv_up, sgu_g, sgu_b, w_s, b_s, w_branch, w_out, cos, sin):
    bsz, seq, _ = h.shape
    proj = h @ w_in + b_in
    u_s5, cq, ckv, k_pe, u_sgu, v_sgu, gate_logits = jnp.split(proj, IN_OFFSETS, axis=-1)
    y_s5 = s5_mixer(u_s5, lam_re, lam_im, log_dt, b_re, b_im, c_re, c_im, d, w_glu, b_glu)
    y_mla = mla_mixer(cq, ckv, k_pe, q_norm, w_q_up, kv_norm, w_kv_up, cos, sin)
    y_sgu = sgu_mixer(u_sgu, v_sgu, sgu_g, sgu_b, w_s, b_s)
    gates = jax.nn.sigmoid(gate_logits).reshape(bsz, seq, N_BRANCH, D_MODEL)
    merged = (gates[:, :, 0] * (y_s5 @ w_branch[0])
              + gates[:, :, 1] * (y_mla @ w_branch[1])
              + gates[:, :, 2] * (y_sgu @ w_branch[2]))
    return merged @ w_out


def swiglu(h, w_in, w_out):
    a, b = jnp.split(h @ w_in, 2, axis=-1)
    return (jax.nn.silu(a) * b) @ w_out


def _fwd_setup_inputs(seed: int = 0) -> dict:
    key = jax.random.key(seed)
    ks = jax.random.split(key, 32)
    L = DEPTH

    def nrm(k, shape, std):
        return jax.random.normal(k, shape, jnp.float32) * std

    def gain(k, shape):
        return 1.0 + nrm(k, shape, 0.01)

    lam_im0 = jnp.broadcast_to(math.pi * jnp.arange(S5_STATE, dtype=jnp.float32), (L, S5_GROUPS, S5_STATE))
    return {
        'x': nrm(ks[0], (BATCH, SEQ, D_MODEL), 1.0),
        'c': nrm(ks[1], (BATCH, D_MODEL), 1.0),
        'w_ada': nrm(ks[2], (L, D_MODEL, 6 * D_MODEL), 0.02),
        'b_ada': nrm(ks[3], (L, 6 * D_MODEL), 0.01),
        'w_in': nrm(ks[4], (L, D_MODEL, IN_WIDTH), D_MODEL ** -0.5),
        'b_in': nrm(ks[5], (L, IN_WIDTH), 0.01),
        's5_lambda_re': -0.5 + nrm(ks[6], (L, S5_GROUPS, S5_STATE), 0.01),
        's5_lambda_im': lam_im0 + nrm(ks[7], (L, S5_GROUPS, S5_STATE), 0.01),
        's5_log_dt': jax.random.uniform(ks[8], (L, S5_GROUPS), jnp.float32, math.log(DT_MIN), math.log(DT_MAX)),
        's5_b_re': nrm(ks[9], (L, S5_GROUPS, S5_STATE, S5_GROUP), (2 * S5_GROUP) ** -0.5),
        's5_b_im': nrm(ks[10], (L, S5_GROUPS, S5_STATE, S5_GROUP), (2 * S5_GROUP) ** -0.5),
        's5_c_re': nrm(ks[11], (L, S5_GROUPS, S5_GROUP, S5_STATE), S5_STATE ** -0.5),
        's5_c_im': nrm(ks[12], (L, S5_GROUPS, S5_GROUP, S5_STATE), S5_STATE ** -0.5),
        's5_d': nrm(ks[13], (L, S5_WIDTH), 1.0),
        's5_w_glu': nrm(ks[14], (L, S5_WIDTH, S5_WIDTH), S5_WIDTH ** -0.5),
        's5_b_glu': nrm(ks[15], (L, S5_WIDTH), 0.01),
        'mla_q_norm': gain(ks[16], (L, Q_LORA)),
        'mla_w_q_up': nrm(ks[17], (L, Q_LORA, MLA_HEADS * (QK_NOPE + QK_ROPE)), Q_LORA ** -0.5),
        'mla_kv_norm': gain(ks[18], (L, KV_LORA)),
        'mla_w_kv_up': nrm(ks[19], (L, KV_LORA, MLA_HEADS * (QK_NOPE + V_HEAD)), KV_LORA ** -0.5),
        'sgu_ln_g': gain(ks[20], (L, SGU_WIDTH)),
        'sgu_ln_b': nrm(ks[21], (L, SGU_WIDTH), 0.01),
        'sgu_w_s': nrm(ks[22], (L, SGU_GROUPS, SGU_CHUNK, SGU_CHUNK), SGU_CHUNK ** -0.5),
        'sgu_b_s': gain(ks[23], (L, SGU_GROUPS, SGU_CHUNK)),
        'w_branch': nrm(ks[24], (L, N_BRANCH, BRANCH_WIDTH, D_MODEL), BRANCH_WIDTH ** -0.5),
        'w_out': nrm(ks[25], (L, D_MODEL, D_MODEL), DEEPNORM_BETA * D_MODEL ** -0.5),
        'ln1_g': gain(ks[26], (L, D_MODEL)),
        'ln1_b': nrm(ks[27], (L, D_MODEL), 0.01),
        'ffn_w_in': nrm(ks[28], (L, D_MODEL, 2 * FF_HIDDEN), D_MODEL ** -0.5),
        'ffn_w_out': nrm(ks[29], (L, FF_HIDDEN, D_MODEL), DEEPNORM_BETA * FF_HIDDEN ** -0.5),
        'ln2_g': gain(ks[30], (L, D_MODEL)),
        'ln2_b': nrm(ks[31], (L, D_MODEL), 0.01),
    }


def _fwd_reference(x, c, w_ada, b_ada, w_in, b_in, s5_lambda_re, s5_lambda_im, s5_log_dt, s5_b_re, s5_b_im,
              s5_c_re, s5_c_im, s5_d, s5_w_glu, s5_b_glu, mla_q_norm, mla_w_q_up, mla_kv_norm, mla_w_kv_up,
              sgu_ln_g, sgu_ln_b, sgu_w_s, sgu_b_s, w_branch, w_out, ln1_g, ln1_b, ffn_w_in, ffn_w_out,
              ln2_g, ln2_b):
    seq = x.shape[1]
    inv_freq = 1.0 / (ROPE_THETA ** (jnp.arange(0, QK_ROPE, 2, dtype=jnp.float32) / QK_ROPE))
    ang = jnp.arange(seq, dtype=jnp.float32)[:, None] * inv_freq[None, :]
    cos = jnp.cos(ang).astype(x.dtype)
    sin = jnp.sin(ang).astype(x.dtype)
    c_act = jax.nn.silu(c)
    for l in range(DEPTH):
        ada = (c_act @ w_ada[l] + b_ada[l])[:, None, :]
        sh1, sc1, g1, sh2, sc2, g2 = jnp.split(ada, 6, axis=-1)
        h = x * (1.0 + sc1) + sh1
        y = hybrid_mixer(h, w_in[l], b_in[l], s5_lambda_re[l], s5_lambda_im[l], s5_log_dt[l],
                         s5_b_re[l], s5_b_im[l], s5_c_re[l], s5_c_im[l], s5_d[l], s5_w_glu[l], s5_b_glu[l],
                         mla_q_norm[l], mla_w_q_up[l], mla_kv_norm[l], mla_w_kv_up[l],
                         sgu_ln_g[l], sgu_ln_b[l], sgu_w_s[l], sgu_b_s[l], w_branch[l], w_out[l], cos, sin)
        x = layer_norm(DEEPNORM_ALPHA * x + (1.0 + g1) * y, ln1_g[l], ln1_b[l])
        h = x * (1.0 + sc2) + sh2
        f = swiglu(h, ffn_w_in[l], ffn_w_out[l])
        x = layer_norm(DEEPNORM_ALPHA * x + (1.0 + g2) * f, ln2_g[l], ln2_b[l])
    return x


import jax as _jax
import jax.numpy as _jnp

TWIN_FORMAT = 'train_step'
FWD_PARAMS = ['x', 'c', 'w_ada', 'b_ada', 'w_in', 'b_in', 's5_lambda_re', 's5_lambda_im', 's5_log_dt', 's5_b_re', 's5_b_im', 's5_c_re', 's5_c_im', 's5_d', 's5_w_glu', 's5_b_glu', 'mla_q_norm', 'mla_w_q_up', 'mla_kv_norm', 'mla_w_kv_up', 'sgu_ln_g', 'sgu_ln_b', 'sgu_w_s', 'sgu_b_s', 'w_branch', 'w_out', 'ln1_g', 'ln1_b', 'ffn_w_in', 'ffn_w_out', 'ln2_g', 'ln2_b']
TWIN_WEIGHTS = ['w_ada', 'b_ada', 'w_in', 'b_in', 's5_lambda_re', 's5_lambda_im', 's5_log_dt', 's5_b_re', 's5_b_im', 's5_c_re', 's5_c_im', 's5_d', 's5_w_glu', 's5_b_glu', 'mla_q_norm', 'mla_w_q_up', 'mla_kv_norm', 'mla_w_kv_up', 'sgu_ln_g', 'sgu_ln_b', 'sgu_w_s', 'sgu_b_s', 'w_branch', 'w_out', 'ln1_g', 'ln1_b', 'ffn_w_in', 'ffn_w_out', 'ln2_g', 'ln2_b']
TWIN_DIFF_INPUT = 'x'
TWIN_INPUTS = ['x', 'c', 'w_ada', 'b_ada', 'w_in', 'b_in', 's5_lambda_re', 's5_lambda_im', 's5_log_dt', 's5_b_re', 's5_b_im', 's5_c_re', 's5_c_im', 's5_d', 's5_w_glu', 's5_b_glu', 'mla_q_norm', 'mla_w_q_up', 'mla_kv_norm', 'mla_w_kv_up', 'sgu_ln_g', 'sgu_ln_b', 'sgu_w_s', 'sgu_b_s', 'w_branch', 'w_out', 'ln1_g', 'ln1_b', 'ffn_w_in', 'ffn_w_out', 'ln2_g', 'ln2_b', 'loss_target', 'm_w_ada', 'm_b_ada', 'm_w_in', 'm_b_in', 'm_s5_lambda_re', 'm_s5_lambda_im', 'm_s5_log_dt', 'm_s5_b_re', 'm_s5_b_im', 'm_s5_c_re', 'm_s5_c_im', 'm_s5_d', 'm_s5_w_glu', 'm_s5_b_glu', 'm_mla_q_norm', 'm_mla_w_q_up', 'm_mla_kv_norm', 'm_mla_w_kv_up', 'm_sgu_ln_g', 'm_sgu_ln_b', 'm_sgu_w_s', 'm_sgu_b_s', 'm_w_branch', 'm_w_out', 'm_ln1_g', 'm_ln1_b', 'm_ffn_w_in', 'm_ffn_w_out', 'm_ln2_g', 'm_ln2_b', 'v_w_ada', 'v_b_ada', 'v_w_in', 'v_b_in', 'v_s5_lambda_re', 'v_s5_lambda_im', 'v_s5_log_dt', 'v_s5_b_re', 'v_s5_b_im', 'v_s5_c_re', 'v_s5_c_im', 'v_s5_d', 'v_s5_w_glu', 'v_s5_b_glu', 'v_mla_q_norm', 'v_mla_w_q_up', 'v_mla_kv_norm', 'v_mla_w_kv_up', 'v_sgu_ln_g', 'v_sgu_ln_b', 'v_sgu_w_s', 'v_sgu_b_s', 'v_w_branch', 'v_w_out', 'v_ln1_g', 'v_ln1_b', 'v_ffn_w_in', 'v_ffn_w_out', 'v_ln2_g', 'v_ln2_b']
TWIN_OUTPUTS = ['loss', 'grad_x', 'grad_w_ada', 'grad_b_ada', 'grad_w_in', 'grad_b_in', 'grad_s5_lambda_re', 'grad_s5_lambda_im', 'grad_s5_log_dt', 'grad_s5_b_re', 'grad_s5_b_im', 'grad_s5_c_re', 'grad_s5_c_im', 'grad_s5_d', 'grad_s5_w_glu', 'grad_s5_b_glu', 'grad_mla_q_norm', 'grad_mla_w_q_up', 'grad_mla_kv_norm', 'grad_mla_w_kv_up', 'grad_sgu_ln_g', 'grad_sgu_ln_b', 'grad_sgu_w_s', 'grad_sgu_b_s', 'grad_w_branch', 'grad_w_out', 'grad_ln1_g', 'grad_ln1_b', 'grad_ffn_w_in', 'grad_ffn_w_out', 'grad_ln2_g', 'grad_ln2_b', 'delta_w_ada', 'delta_b_ada', 'delta_w_in', 'delta_b_in', 'delta_s5_lambda_re', 'delta_s5_lambda_im', 'delta_s5_log_dt', 'delta_s5_b_re', 'delta_s5_b_im', 'delta_s5_c_re', 'delta_s5_c_im', 'delta_s5_d', 'delta_s5_w_glu', 'delta_s5_b_glu', 'delta_mla_q_norm', 'delta_mla_w_q_up', 'delta_mla_kv_norm', 'delta_mla_w_kv_up', 'delta_sgu_ln_g', 'delta_sgu_ln_b', 'delta_sgu_w_s', 'delta_sgu_b_s', 'delta_w_branch', 'delta_w_out', 'delta_ln1_g', 'delta_ln1_b', 'delta_ffn_w_in', 'delta_ffn_w_out', 'delta_ln2_g', 'delta_ln2_b', 'new_m_w_ada', 'new_m_b_ada', 'new_m_w_in', 'new_m_b_in', 'new_m_s5_lambda_re', 'new_m_s5_lambda_im', 'new_m_s5_log_dt', 'new_m_s5_b_re', 'new_m_s5_b_im', 'new_m_s5_c_re', 'new_m_s5_c_im', 'new_m_s5_d', 'new_m_s5_w_glu', 'new_m_s5_b_glu', 'new_m_mla_q_norm', 'new_m_mla_w_q_up', 'new_m_mla_kv_norm', 'new_m_mla_w_kv_up', 'new_m_sgu_ln_g', 'new_m_sgu_ln_b', 'new_m_sgu_w_s', 'new_m_sgu_b_s', 'new_m_w_branch', 'new_m_w_out', 'new_m_ln1_g', 'new_m_ln1_b', 'new_m_ffn_w_in', 'new_m_ffn_w_out', 'new_m_ln2_g', 'new_m_ln2_b', 'new_v_w_ada', 'new_v_b_ada', 'new_v_w_in', 'new_v_b_in', 'new_v_s5_lambda_re', 'new_v_s5_lambda_im', 'new_v_s5_log_dt', 'new_v_s5_b_re', 'new_v_s5_b_im', 'new_v_s5_c_re', 'new_v_s5_c_im', 'new_v_s5_d', 'new_v_s5_w_glu', 'new_v_s5_b_glu', 'new_v_mla_q_norm', 'new_v_mla_w_q_up', 'new_v_mla_kv_norm', 'new_v_mla_w_kv_up', 'new_v_sgu_ln_g', 'new_v_sgu_ln_b', 'new_v_sgu_w_s', 'new_v_sgu_b_s', 'new_v_w_branch', 'new_v_w_out', 'new_v_ln1_g', 'new_v_ln1_b', 'new_v_ffn_w_in', 'new_v_ffn_w_out', 'new_v_ln2_g', 'new_v_ln2_b']
TWIN_LEAF_KINDS = {'loss': 'loss', 'grad_x': 'grad_x', 'grad_w_ada': 'grad_w', 'grad_b_ada': 'grad_w', 'grad_w_in': 'grad_w', 'grad_b_in': 'grad_w', 'grad_s5_lambda_re': 'grad_w', 'grad_s5_lambda_im': 'grad_w', 'grad_s5_log_dt': 'grad_w', 'grad_s5_b_re': 'grad_w', 'grad_s5_b_im': 'grad_w', 'grad_s5_c_re': 'grad_w', 'grad_s5_c_im': 'grad_w', 'grad_s5_d': 'grad_w', 'grad_s5_w_glu': 'grad_w', 'grad_s5_b_glu': 'grad_w', 'grad_mla_q_norm': 'grad_w', 'grad_mla_w_q_up': 'grad_w', 'grad_mla_kv_norm': 'grad_w', 'grad_mla_w_kv_up': 'grad_w', 'grad_sgu_ln_g': 'grad_w', 'grad_sgu_ln_b': 'grad_w', 'grad_sgu_w_s': 'grad_w', 'grad_sgu_b_s': 'grad_w', 'grad_w_branch': 'grad_w', 'grad_w_out': 'grad_w', 'grad_ln1_g': 'grad_w', 'grad_ln1_b': 'grad_w', 'grad_ffn_w_in': 'grad_w', 'grad_ffn_w_out': 'grad_w', 'grad_ln2_g': 'grad_w', 'grad_ln2_b': 'grad_w', 'delta_w_ada': 'delta_w', 'delta_b_ada': 'delta_w', 'delta_w_in': 'delta_w', 'delta_b_in': 'delta_w', 'delta_s5_lambda_re': 'delta_w', 'delta_s5_lambda_im': 'delta_w', 'delta_s5_log_dt': 'delta_w', 'delta_s5_b_re': 'delta_w', 'delta_s5_b_im': 'delta_w', 'delta_s5_c_re': 'delta_w', 'delta_s5_c_im': 'delta_w', 'delta_s5_d': 'delta_w', 'delta_s5_w_glu': 'delta_w', 'delta_s5_b_glu': 'delta_w', 'delta_mla_q_norm': 'delta_w', 'delta_mla_w_q_up': 'delta_w', 'delta_mla_kv_norm': 'delta_w', 'delta_mla_w_kv_up': 'delta_w', 'delta_sgu_ln_g': 'delta_w', 'delta_sgu_ln_b': 'delta_w', 'delta_sgu_w_s': 'delta_w', 'delta_sgu_b_s': 'delta_w', 'delta_w_branch': 'delta_w', 'delta_w_out': 'delta_w', 'delta_ln1_g': 'delta_w', 'delta_ln1_b': 'delta_w', 'delta_ffn_w_in': 'delta_w', 'delta_ffn_w_out': 'delta_w', 'delta_ln2_g': 'delta_w', 'delta_ln2_b': 'delta_w', 'new_m_w_ada': 'new_m', 'new_m_b_ada': 'new_m', 'new_m_w_in': 'new_m', 'new_m_b_in': 'new_m', 'new_m_s5_lambda_re': 'new_m', 'new_m_s5_lambda_im': 'new_m', 'new_m_s5_log_dt': 'new_m', 'new_m_s5_b_re': 'new_m', 'new_m_s5_b_im': 'new_m', 'new_m_s5_c_re': 'new_m', 'new_m_s5_c_im': 'new_m', 'new_m_s5_d': 'new_m', 'new_m_s5_w_glu': 'new_m', 'new_m_s5_b_glu': 'new_m', 'new_m_mla_q_norm': 'new_m', 'new_m_mla_w_q_up': 'new_m', 'new_m_mla_kv_norm': 'new_m', 'new_m_mla_w_kv_up': 'new_m', 'new_m_sgu_ln_g': 'new_m', 'new_m_sgu_ln_b': 'new_m', 'new_m_sgu_w_s': 'new_m', 'new_m_sgu_b_s': 'new_m', 'new_m_w_branch': 'new_m', 'new_m_w_out': 'new_m', 'new_m_ln1_g': 'new_m', 'new_m_ln1_b': 'new_m', 'new_m_ffn_w_in': 'new_m', 'new_m_ffn_w_out': 'new_m', 'new_m_ln2_g': 'new_m', 'new_m_ln2_b': 'new_m', 'new_v_w_ada': 'new_v', 'new_v_b_ada': 'new_v', 'new_v_w_in': 'new_v', 'new_v_b_in': 'new_v', 'new_v_s5_lambda_re': 'new_v', 'new_v_s5_lambda_im': 'new_v', 'new_v_s5_log_dt': 'new_v', 'new_v_s5_b_re': 'new_v', 'new_v_s5_b_im': 'new_v', 'new_v_s5_c_re': 'new_v', 'new_v_s5_c_im': 'new_v', 'new_v_s5_d': 'new_v', 'new_v_s5_w_glu': 'new_v', 'new_v_s5_b_glu': 'new_v', 'new_v_mla_q_norm': 'new_v', 'new_v_mla_w_q_up': 'new_v', 'new_v_mla_kv_norm': 'new_v', 'new_v_mla_w_kv_up': 'new_v', 'new_v_sgu_ln_g': 'new_v', 'new_v_sgu_ln_b': 'new_v', 'new_v_sgu_w_s': 'new_v', 'new_v_sgu_b_s': 'new_v', 'new_v_w_branch': 'new_v', 'new_v_w_out': 'new_v', 'new_v_ln1_g': 'new_v', 'new_v_ln1_b': 'new_v', 'new_v_ffn_w_in': 'new_v', 'new_v_ffn_w_out': 'new_v', 'new_v_ln2_g': 'new_v', 'new_v_ln2_b': 'new_v'}


def _forward(args):
    return _fwd_reference(*[args[k] for k in FWD_PARAMS])


def _output_shape():
    def fwd():
        inp = _fwd_setup_inputs(0)
        return _fwd_reference(*[inp[k] for k in FWD_PARAMS])
    out = _jax.eval_shape(fwd)
    return out.shape, out.dtype

N_MICROBATCH = 1
ADAM_LR = 0.001
ADAM_B1 = 0.9
ADAM_B2 = 0.999
ADAM_EPS = 1e-08
ADAM_WD = 0.01
ADAM_STEP = 10
PER_EXAMPLE_BATCH_AXIS = {'x': 0, 'c': 0, 'loss_target': 0}
SHARED_INPUTS = []
_WEIGHT_DTYPES = {'w_ada': _jnp.float32, 'b_ada': _jnp.float32, 'w_in': _jnp.float32, 'b_in': _jnp.float32, 's5_lambda_re': _jnp.float32, 's5_lambda_im': _jnp.float32, 's5_log_dt': _jnp.float32, 's5_b_re': _jnp.float32, 's5_b_im': _jnp.float32, 's5_c_re': _jnp.float32, 's5_c_im': _jnp.float32, 's5_d': _jnp.float32, 's5_w_glu': _jnp.float32, 's5_b_glu': _jnp.float32, 'mla_q_norm': _jnp.float32, 'mla_w_q_up': _jnp.float32, 'mla_kv_norm': _jnp.float32, 'mla_w_kv_up': _jnp.float32, 'sgu_ln_g': _jnp.float32, 'sgu_ln_b': _jnp.float32, 'sgu_w_s': _jnp.float32, 'sgu_b_s': _jnp.float32, 'w_branch': _jnp.float32, 'w_out': _jnp.float32, 'ln1_g': _jnp.float32, 'ln1_b': _jnp.float32, 'ffn_w_in': _jnp.float32, 'ffn_w_out': _jnp.float32, 'ln2_g': _jnp.float32, 'ln2_b': _jnp.float32}
MOMENT_SCALE = {'w_ada': 2.486569e-02, 'b_ada': 4.365417e-02, 'w_in': 1.805234e-02, 'b_in': 2.274153e-02, 's5_lambda_re': 2.011764e-03, 's5_lambda_im': 1.363426e-03, 's5_log_dt': 6.320520e-01, 's5_b_re': 1.029745e-03, 's5_b_im': 1.023735e-03, 's5_c_re': 1.448806e-03, 's5_c_im': 1.434789e-03, 's5_d': 2.006488e-02, 's5_w_glu': 5.642171e-03, 's5_b_glu': 7.719888e-03, 'mla_q_norm': 5.994420e-03, 'mla_w_q_up': 4.357073e-03, 'mla_kv_norm': 2.205613e-02, 'mla_w_kv_up': 1.100074e-02, 'sgu_ln_g': 2.513423e-02, 'sgu_ln_b': 2.598212e-02, 'sgu_w_s': 2.526818e-02, 'sgu_b_s': 2.937975e-02, 'w_branch': 1.916917e-02, 'w_out': 7.970096e-02, 'ln1_g': 5.168603e-01, 'ln1_b': 2.971095e-01, 'ffn_w_in': 2.356213e-02, 'ffn_w_out': 9.162715e-02, 'ln2_g': 1.601709e+01, 'ln2_b': 1.702847e+00}


def _to_microbatches(a, axis):
    t = _jnp.moveaxis(a, axis, 0)
    t = t.reshape((N_MICROBATCH, t.shape[0] // N_MICROBATCH) + t.shape[1:])
    return _jnp.moveaxis(t, 1, axis + 1)


def setup_inputs(seed: int = 0) -> dict:
    inp = _fwd_setup_inputs(seed)
    key = _jax.random.fold_in(_jax.random.key(seed), 7919)
    shape, _ = _output_shape()
    out = dict(inp)
    out["loss_target"] = _jax.random.normal(_jax.random.fold_in(key, 0), shape, _jnp.float32)
    for i, name in enumerate(TWIN_WEIGHTS):
        w = inp[name].astype(_jnp.float32)
        if MOMENT_SCALE is None:
            s = _jnp.sqrt(_jnp.mean(_jnp.square(w)) + 1e-30)
        else:
            s = MOMENT_SCALE[name]
        km, kv = _jax.random.split(_jax.random.fold_in(key, i + 1))
        out[name] = w
        out["m_" + name] = s * _jax.random.normal(km, w.shape, _jnp.float32)
        out["v_" + name] = (s * s) * _jax.random.uniform(kv, w.shape, _jnp.float32, 0.5, 1.5)
    if N_MICROBATCH > 1:
        for name, axis in PER_EXAMPLE_BATCH_AXIS.items():
            out[name] = _to_microbatches(out[name], axis)
    return {'x': out['x'], 'c': out['c'], 'w_ada': out['w_ada'], 'b_ada': out['b_ada'], 'w_in': out['w_in'], 'b_in': out['b_in'], 's5_lambda_re': out['s5_lambda_re'], 's5_lambda_im': out['s5_lambda_im'], 's5_log_dt': out['s5_log_dt'], 's5_b_re': out['s5_b_re'], 's5_b_im': out['s5_b_im'], 's5_c_re': out['s5_c_re'], 's5_c_im': out['s5_c_im'], 's5_d': out['s5_d'], 's5_w_glu': out['s5_w_glu'], 's5_b_glu': out['s5_b_glu'], 'mla_q_norm': out['mla_q_norm'], 'mla_w_q_up': out['mla_w_q_up'], 'mla_kv_norm': out['mla_kv_norm'], 'mla_w_kv_up': out['mla_w_kv_up'], 'sgu_ln_g': out['sgu_ln_g'], 'sgu_ln_b': out['sgu_ln_b'], 'sgu_w_s': out['sgu_w_s'], 'sgu_b_s': out['sgu_b_s'], 'w_branch': out['w_branch'], 'w_out': out['w_out'], 'ln1_g': out['ln1_g'], 'ln1_b': out['ln1_b'], 'ffn_w_in': out['ffn_w_in'], 'ffn_w_out': out['ffn_w_out'], 'ln2_g': out['ln2_g'], 'ln2_b': out['ln2_b'], 'loss_target': out['loss_target'], 'm_w_ada': out['m_w_ada'], 'm_b_ada': out['m_b_ada'], 'm_w_in': out['m_w_in'], 'm_b_in': out['m_b_in'], 'm_s5_lambda_re': out['m_s5_lambda_re'], 'm_s5_lambda_im': out['m_s5_lambda_im'], 'm_s5_log_dt': out['m_s5_log_dt'], 'm_s5_b_re': out['m_s5_b_re'], 'm_s5_b_im': out['m_s5_b_im'], 'm_s5_c_re': out['m_s5_c_re'], 'm_s5_c_im': out['m_s5_c_im'], 'm_s5_d': out['m_s5_d'], 'm_s5_w_glu': out['m_s5_w_glu'], 'm_s5_b_glu': out['m_s5_b_glu'], 'm_mla_q_norm': out['m_mla_q_norm'], 'm_mla_w_q_up': out['m_mla_w_q_up'], 'm_mla_kv_norm': out['m_mla_kv_norm'], 'm_mla_w_kv_up': out['m_mla_w_kv_up'], 'm_sgu_ln_g': out['m_sgu_ln_g'], 'm_sgu_ln_b': out['m_sgu_ln_b'], 'm_sgu_w_s': out['m_sgu_w_s'], 'm_sgu_b_s': out['m_sgu_b_s'], 'm_w_branch': out['m_w_branch'], 'm_w_out': out['m_w_out'], 'm_ln1_g': out['m_ln1_g'], 'm_ln1_b': out['m_ln1_b'], 'm_ffn_w_in': out['m_ffn_w_in'], 'm_ffn_w_out': out['m_ffn_w_out'], 'm_ln2_g': out['m_ln2_g'], 'm_ln2_b': out['m_ln2_b'], 'v_w_ada': out['v_w_ada'], 'v_b_ada': out['v_b_ada'], 'v_w_in': out['v_w_in'], 'v_b_in': out['v_b_in'], 'v_s5_lambda_re': out['v_s5_lambda_re'], 'v_s5_lambda_im': out['v_s5_lambda_im'], 'v_s5_log_dt': out['v_s5_log_dt'], 'v_s5_b_re': out['v_s5_b_re'], 'v_s5_b_im': out['v_s5_b_im'], 'v_s5_c_re': out['v_s5_c_re'], 'v_s5_c_im': out['v_s5_c_im'], 'v_s5_d': out['v_s5_d'], 'v_s5_w_glu': out['v_s5_w_glu'], 'v_s5_b_glu': out['v_s5_b_glu'], 'v_mla_q_norm': out['v_mla_q_norm'], 'v_mla_w_q_up': out['v_mla_w_q_up'], 'v_mla_kv_norm': out['v_mla_kv_norm'], 'v_mla_w_kv_up': out['v_mla_w_kv_up'], 'v_sgu_ln_g': out['v_sgu_ln_g'], 'v_sgu_ln_b': out['v_sgu_ln_b'], 'v_sgu_w_s': out['v_sgu_w_s'], 'v_sgu_b_s': out['v_sgu_b_s'], 'v_w_branch': out['v_w_branch'], 'v_w_out': out['v_w_out'], 'v_ln1_g': out['v_ln1_g'], 'v_ln1_b': out['v_ln1_b'], 'v_ffn_w_in': out['v_ffn_w_in'], 'v_ffn_w_out': out['v_ffn_w_out'], 'v_ln2_g': out['v_ln2_g'], 'v_ln2_b': out['v_ln2_b']}


def _loss(weights, diff, rest, loss_target):
    with _jax.named_scope("forward"):
        args = {**rest, TWIN_DIFF_INPUT: diff, **{k: w.astype(_WEIGHT_DTYPES[k]) for k, w in weights.items()}}
        y = _forward(args)
    with _jax.named_scope("loss_head"):
        err = _jnp.square(y.astype(_jnp.float32) - loss_target)
        return 0.5 * _jnp.sum(_jnp.mean(err, axis=-1)) if err.ndim else 0.5 * err


def _adamw(w, g, m, v):
    m = ADAM_B1 * m + (1.0 - ADAM_B1) * g
    v = ADAM_B2 * v + (1.0 - ADAM_B2) * _jnp.square(g)
    m_hat = m / (1.0 - ADAM_B1 ** ADAM_STEP)
    v_hat = v / (1.0 - ADAM_B2 ** ADAM_STEP)
    delta = -ADAM_LR * (m_hat / (_jnp.sqrt(v_hat) + ADAM_EPS) + ADAM_WD * w)
    return delta, m, v


def reference(x, c, w_ada, b_ada, w_in, b_in, s5_lambda_re, s5_lambda_im, s5_log_dt, s5_b_re, s5_b_im, s5_c_re, s5_c_im, s5_d, s5_w_glu, s5_b_glu, mla_q_norm, mla_w_q_up, mla_kv_norm, mla_w_kv_up, sgu_ln_g, sgu_ln_b, sgu_w_s, sgu_b_s, w_branch, w_out, ln1_g, ln1_b, ffn_w_in, ffn_w_out, ln2_g, ln2_b, loss_target, m_w_ada, m_b_ada, m_w_in, m_b_in, m_s5_lambda_re, m_s5_lambda_im, m_s5_log_dt, m_s5_b_re, m_s5_b_im, m_s5_c_re, m_s5_c_im, m_s5_d, m_s5_w_glu, m_s5_b_glu, m_mla_q_norm, m_mla_w_q_up, m_mla_kv_norm, m_mla_w_kv_up, m_sgu_ln_g, m_sgu_ln_b, m_sgu_w_s, m_sgu_b_s, m_w_branch, m_w_out, m_ln1_g, m_ln1_b, m_ffn_w_in, m_ffn_w_out, m_ln2_g, m_ln2_b, v_w_ada, v_b_ada, v_w_in, v_b_in, v_s5_lambda_re, v_s5_lambda_im, v_s5_log_dt, v_s5_b_re, v_s5_b_im, v_s5_c_re, v_s5_c_im, v_s5_d, v_s5_w_glu, v_s5_b_glu, v_mla_q_norm, v_mla_w_q_up, v_mla_kv_norm, v_mla_w_kv_up, v_sgu_ln_g, v_sgu_ln_b, v_sgu_w_s, v_sgu_b_s, v_w_branch, v_w_out, v_ln1_g, v_ln1_b, v_ffn_w_in, v_ffn_w_out, v_ln2_g, v_ln2_b):
    given = dict(x=x, c=c, w_ada=w_ada, b_ada=b_ada, w_in=w_in, b_in=b_in, s5_lambda_re=s5_lambda_re, s5_lambda_im=s5_lambda_im, s5_log_dt=s5_log_dt, s5_b_re=s5_b_re, s5_b_im=s5_b_im, s5_c_re=s5_c_re, s5_c_im=s5_c_im, s5_d=s5_d, s5_w_glu=s5_w_glu, s5_b_glu=s5_b_glu, mla_q_norm=mla_q_norm, mla_w_q_up=mla_w_q_up, mla_kv_norm=mla_kv_norm, mla_w_kv_up=mla_w_kv_up, sgu_ln_g=sgu_ln_g, sgu_ln_b=sgu_ln_b, sgu_w_s=sgu_w_s, sgu_b_s=sgu_b_s, w_branch=w_branch, w_out=w_out, ln1_g=ln1_g, ln1_b=ln1_b, ffn_w_in=ffn_w_in, ffn_w_out=ffn_w_out, ln2_g=ln2_g, ln2_b=ln2_b, loss_target=loss_target, m_w_ada=m_w_ada, m_b_ada=m_b_ada, m_w_in=m_w_in, m_b_in=m_b_in, m_s5_lambda_re=m_s5_lambda_re, m_s5_lambda_im=m_s5_lambda_im, m_s5_log_dt=m_s5_log_dt, m_s5_b_re=m_s5_b_re, m_s5_b_im=m_s5_b_im, m_s5_c_re=m_s5_c_re, m_s5_c_im=m_s5_c_im, m_s5_d=m_s5_d, m_s5_w_glu=m_s5_w_glu, m_s5_b_glu=m_s5_b_glu, m_mla_q_norm=m_mla_q_norm, m_mla_w_q_up=m_mla_w_q_up, m_mla_kv_norm=m_mla_kv_norm, m_mla_w_kv_up=m_mla_w_kv_up, m_sgu_ln_g=m_sgu_ln_g, m_sgu_ln_b=m_sgu_ln_b, m_sgu_w_s=m_sgu_w_s, m_sgu_b_s=m_sgu_b_s, m_w_branch=m_w_branch, m_w_out=m_w_out, m_ln1_g=m_ln1_g, m_ln1_b=m_ln1_b, m_ffn_w_in=m_ffn_w_in, m_ffn_w_out=m_ffn_w_out, m_ln2_g=m_ln2_g, m_ln2_b=m_ln2_b, v_w_ada=v_w_ada, v_b_ada=v_b_ada, v_w_in=v_w_in, v_b_in=v_b_in, v_s5_lambda_re=v_s5_lambda_re, v_s5_lambda_im=v_s5_lambda_im, v_s5_log_dt=v_s5_log_dt, v_s5_b_re=v_s5_b_re, v_s5_b_im=v_s5_b_im, v_s5_c_re=v_s5_c_re, v_s5_c_im=v_s5_c_im, v_s5_d=v_s5_d, v_s5_w_glu=v_s5_w_glu, v_s5_b_glu=v_s5_b_glu, v_mla_q_norm=v_mla_q_norm, v_mla_w_q_up=v_mla_w_q_up, v_mla_kv_norm=v_mla_kv_norm, v_mla_w_kv_up=v_mla_w_kv_up, v_sgu_ln_g=v_sgu_ln_g, v_sgu_ln_b=v_sgu_ln_b, v_sgu_w_s=v_sgu_w_s, v_sgu_b_s=v_sgu_b_s, v_w_branch=v_w_branch, v_w_out=v_w_out, v_ln1_g=v_ln1_g, v_ln1_b=v_ln1_b, v_ffn_w_in=v_ffn_w_in, v_ffn_w_out=v_ffn_w_out, v_ln2_g=v_ln2_g, v_ln2_b=v_ln2_b)
    weights = {n: given[n] for n in TWIN_WEIGHTS}
    shared = {n: given[n] for n in SHARED_INPUTS}
    per_example = {n: given[n] for n in ['x', 'c']}
    grad_fn = _jax.value_and_grad(_loss, argnums=(0, 1))

    def one_microbatch(ex, loss_target):
        ex = dict(ex)
        diff = ex.pop(TWIN_DIFF_INPUT)
        return grad_fn(weights, diff, {**shared, **ex}, loss_target)

    if N_MICROBATCH == 1:
        loss, (grad_w, grad_x) = one_microbatch(per_example, given["loss_target"])
    else:
        def body(carry, xs):
            loss_sum, grad_sum = carry
            l_k, (gw_k, gx_k) = one_microbatch(xs[0], xs[1])
            with _jax.named_scope("update"):
                return (loss_sum + l_k, _jax.tree.map(_jnp.add, grad_sum, gw_k)), gx_k

        init = (_jnp.zeros((), _jnp.float32), _jax.tree.map(_jnp.zeros_like, weights))
        (loss, grad_w), grad_x = _jax.lax.scan(body, init, (per_example, given["loss_target"]))
    with _jax.named_scope("update"):
        delta_w, new_m, new_v = {}, {}, {}
        for n in TWIN_WEIGHTS:
            delta_w[n], new_m[n], new_v[n] = _adamw(weights[n], grad_w[n], given["m_" + n], given["v_" + n])
    return (loss, grad_x, *[grad_w[n] for n in TWIN_WEIGHTS], *[delta_w[n] for n in TWIN_WEIGHTS],
            *[new_m[n] for n in TWIN_WEIGHTS], *[new_v[n] for n in TWIN_WEIGHTS])
```

```python
import functools
import math

import numpy as np
import jax
import jax.numpy as jnp
from jax import lax
from jax.experimental import pallas as pl
from jax.experimental.pallas import tpu as pltpu

F32 = jnp.float32
BF16 = jnp.bfloat16

N_DEV = 8
D_MODEL = 1024
DEPTH = 4
CHUNK = 64
S5_WIDTH = 512
S5_GROUP = 16
S5_GROUPS = 32
S5_STATE = 64
MLA_HEADS = 8
QK_NOPE = 64
QK_ROPE = 32
V_HEAD = 64
Q_LORA = 384
KV_LORA = 256
ROPE_THETA = 10000.0
SGU_WIDTH = 512
SGU_GROUPS = 4
SGU_CHUNK = 128
FF_HIDDEN = 2816
DEEPNORM_ALPHA = (2 * DEPTH) ** 0.25
LN_EPS = 1e-5
RMS_EPS = 1e-6
NEG_INF = -1e30
ADAM_LR = 0.001
ADAM_B1 = 0.9
ADAM_B2 = 0.999
ADAM_EPS = 1e-08
ADAM_WD = 0.01
ADAM_STEP = 10

IN_WIDTH = 5280
IN_PAD = 5376
_O_S5, _O_CQ, _O_CKV, _O_KPE, _O_USGU, _O_VSGU, _O_GATE = 0, 512, 896, 1152, 1184, 1696, 2208
_IN_SEGMENTS = ((_O_GATE, IN_WIDTH), (_O_S5, _O_CQ), (_O_USGU, _O_VSGU), (_O_VSGU, _O_GATE), (_O_CQ, _O_USGU))
P_GATE, P_S5, P_USGU, P_VSGU, P_MLA = 0, 3072, 3584, 4096, 4608
MLA_BLK = 768

V7X_LANES = 128
V7X_SUBLANES = 8
VMEM_LIMIT = 48 * 1024 * 1024
ATT_BLOCK = 512
SCAN_LANES = 128
S5_CH = S5_GROUPS * S5_STATE

WNAMES = ['w_ada', 'b_ada', 'w_in', 'b_in', 's5_lambda_re', 's5_lambda_im', 's5_log_dt', 's5_b_re', 's5_b_im',
          's5_c_re', 's5_c_im', 's5_d', 's5_w_glu', 's5_b_glu', 'mla_q_norm', 'mla_w_q_up', 'mla_kv_norm',
          'mla_w_kv_up', 'sgu_ln_g', 'sgu_ln_b', 'sgu_w_s', 'sgu_b_s', 'w_branch', 'w_out', 'ln1_g', 'ln1_b',
          'ffn_w_in', 'ffn_w_out', 'ln2_g', 'ln2_b']
SHARDED = (('w_in', 'col'), ('s5_w_glu', 'row'), ('mla_w_q_up', 'col'), ('mla_w_kv_up', 'col'),
           ('w_branch', 'col3'), ('w_out', 'row'), ('ffn_w_in', 'col'), ('ffn_w_out', 'row'))
SMALL = [n for n in WNAMES if n != 'w_ada' and n not in dict(SHARDED)]
SMALL_PAD = N_DEV * V7X_SUBLANES * 1024


def _cparams(*sem):
    return pltpu.CompilerParams(dimension_semantics=sem, vmem_limit_bytes=VMEM_LIMIT)


def _pick(n, target):
    if n <= target:
        return n
    best = None
    for d in range(V7X_LANES, target + 1, V7X_LANES):
        if n % d == 0:
            best = d
    assert best is not None, (n, target)
    return best


def _pick_rows(n, target):
    if n <= target:
        return n
    best = None
    for d in range(V7X_SUBLANES, target + 1, V7X_SUBLANES):
        if n % d == 0:
            best = d
    assert best is not None, (n, target)
    return best


@jax.custom_vjp
def _bdot(a, b):
    return jnp.dot(a.astype(BF16), b.astype(BF16), preferred_element_type=F32)


def _bdot_fwd(a, b):
    return _bdot(a, b), (a, b)


def _bdot_bwd(res, g):
    a, b = res
    gb = g.astype(BF16)
    da = lax.dot_general(gb, b.astype(BF16), (((1,), (1,)), ((), ())), preferred_element_type=F32)
    db = lax.dot_general(a.astype(BF16), gb, (((0,), (0,)), ((), ())), preferred_element_type=F32)
    return da.astype(a.dtype), db.astype(b.dtype)


_bdot.defvjp(_bdot_fwd, _bdot_bwd)


@functools.partial(jax.custom_vjp, nondiff_argnums=(1,))
def _lane_roll(x, shift):
    return pltpu.roll(x, shift % x.shape[1], 1)


def _lane_roll_fwd(x, shift):
    return _lane_roll(x, shift), None


def _lane_roll_bwd(shift, _, g):
    return (_lane_roll(g, -shift),)


_lane_roll.defvjp(_lane_roll_fwd, _lane_roll_bwd)


def _sigmoid(x):
    return 1.0 / (1.0 + jnp.exp(-x))


def _gelu(x):
    return 0.5 * x * (1.0 + jnp.tanh(math.sqrt(2.0 / math.pi) * (x + 0.044715 * (x * x * x))))


def _layer_norm(x, g, b):
    mu = jnp.mean(x, axis=-1, keepdims=True)
    var = jnp.mean(jnp.square(x - mu), axis=-1, keepdims=True)
    return (x - mu) * lax.rsqrt(var + LN_EPS) * g + b


def _rms_norm(x, g):
    return x * lax.rsqrt(jnp.mean(x * x, axis=-1, keepdims=True) + RMS_EPS) * g


def _rope(x, c, s1, s2):
    half = QK_ROPE // 2
    return x * c + _lane_roll(x, -half) * s1 + _lane_roll(x, half) * s2


def _modulate_fn(x, scale_row, shift_row):
    return (x * scale_row + shift_row,)


def _ln_res_fn(x, y, gate_row, g, b):
    return (_layer_norm(DEEPNORM_ALPHA * x + gate_row * y, g, b),)


def _s5_post_fn(ylin, u, d, w_glu, b_glu):
    z = _gelu(ylin + d * u)
    return (z * _sigmoid(_bdot(z, w_glu) + b_glu),)


def _mla_pre_fn(blk, cq_t, sq1, sq2, ck_t, sk1, sk2, q_norm, w_q, kv_norm, w_kv):
    cq, ckv, kpe = blk[:, :Q_LORA], blk[:, Q_LORA:Q_LORA + KV_LORA], blk[:, Q_LORA + KV_LORA:]
    q = _rope(_bdot(_rms_norm(cq, q_norm), w_q), cq_t, sq1, sq2)
    kv = _bdot(_rms_norm(ckv, kv_norm), w_kv)
    return q, kv, _rope(kpe, ck_t, sk1, sk2)


def _sgu_fn(u, v, g, b, wm, bias):
    vn = _layer_norm(_gelu(v), g, b)
    w = SGU_CHUNK
    parts = [_bdot(wm[k * w:(k + 1) * w, :], vn[:, k * w:(k + 1) * w]) for k in range(SGU_GROUPS)]
    return (_gelu(u) * (jnp.concatenate(parts, axis=1) + bias),)


def _merge_fn(y0, y1, y2, l0, l1, l2, wb):
    n = S5_WIDTH
    return (_sigmoid(l0) * _bdot(y0, wb[:n]) + _sigmoid(l1) * _bdot(y1, wb[n:2 * n])
            + _sigmoid(l2) * _bdot(y2, wb[2 * n:]),)


def _swiglu_fn(a, b):
    return (a * _sigmoid(a) * b,)


def _rowcall(fn, rows, fulls, out_rows, out_reds=(), *, tm, name):
    rows = [r if isinstance(r, tuple) else (r, r.shape[1], 0) for r in rows]
    t = rows[0][0].shape[0]
    tm = _pick_rows(t, tm)
    n_in, n_or, n_red = len(rows) + len(fulls), len(out_rows), len(out_reds)

    def body(*refs):
        vals = fn(*[r[...] for r in refs[:n_in]])
        assert len(vals) == n_or + n_red, (name, len(vals))
        for ref, v in zip(refs[n_in:n_in + n_or], vals[:n_or]):
            ref[...] = v.astype(ref.dtype)
        if n_red:
            red_refs = refs[n_in + n_or:]

            @pl.when(pl.program_id(0) == 0)
            def _():
                for ref in red_refs:
                    ref[...] = jnp.zeros_like(ref)

            for ref, v in zip(red_refs, vals[n_or:]):
                ref[...] += v.astype(ref.dtype)

    in_specs = [pl.BlockSpec((tm, w), functools.partial(lambda i, blk: (i, blk), blk=blk)) for _, w, blk in rows]
    in_specs += [pl.BlockSpec(f.shape, lambda i: (0, 0)) for f in fulls]
    out_specs = [pl.BlockSpec((tm, c), lambda i: (i, 0)) for c, _ in out_rows]
    out_specs += [pl.BlockSpec(s, lambda i: (0, 0)) for s, _ in out_reds]
    out_shape = [jax.ShapeDtypeStruct((t, c), dt) for c, dt in out_rows]
    out_shape += [jax.ShapeDtypeStruct(s, dt) for s, dt in out_reds]
    return pl.pallas_call(
        body, name=name, grid=(t // tm,), in_specs=in_specs, out_specs=out_specs, out_shape=out_shape,
        compiler_params=_cparams("arbitrary" if n_red else "parallel"),
    )(*[r[0] for r in rows], *fulls)


def _rowcall_vjp(fn, rows, fulls, cots, diff_rows, diff_fulls, *, tm, name, row_dtypes=None):
    rows_n = [r if isinstance(r, tuple) else (r, r.shape[1], 0) for r in rows]
    n_r, n_c = len(rows), len(cots)
    row_dtypes = row_dtypes or [F32] * len(diff_rows)

    def fn2(*vals):
        r = [v.astype(F32) for v in vals[:n_r]]
        ct = vals[n_r:n_r + n_c]
        f = [v.astype(F32) for v in vals[n_r + n_c:]]

        def g(*dargs):
            rr, ff = list(r), list(f)
            for k, idx in enumerate(diff_rows):
                rr[idx] = dargs[k]
            for k, idx in enumerate(diff_fulls):
                ff[idx] = dargs[len(diff_rows) + k]
            return fn(*rr, *ff)

        prim = [r[i] for i in diff_rows] + [f[i] for i in diff_fulls]
        outs, pull = jax.vjp(g, *prim)
        return pull(tuple(c.astype(o.dtype) for c, o in zip(ct, outs)))

    out_rows = [(rows_n[i][1], dt) for i, dt in zip(diff_rows, row_dtypes)]
    out_reds = [(fulls[i].shape, F32) for i in diff_fulls]
    return _rowcall(fn2, list(rows) + list(cots), fulls, out_rows, out_reds, tm=tm, name=name)


def _mm(a, b, *, ta=False, tb=False, bias=None, out_dtype=F32, name, tm=512, tn=1024, tk=1024):
    (k_a, m) = a.shape if ta else a.shape[::-1]
    (n, k_b) = b.shape if tb else b.shape[::-1]
    assert k_a == k_b, (name, a.shape, b.shape)
    tm, tn, tk = _pick(m, tm) if m % V7X_LANES == 0 else m, _pick(n, tn), _pick(k_a, tk) if k_a % V7X_LANES == 0 else k_a
    nk = k_a // tk
    a_spec = pl.BlockSpec((tk, tm), lambda i, j, k: (k, i)) if ta else pl.BlockSpec((tm, tk), lambda i, j, k: (i, k))
    b_spec = pl.BlockSpec((tn, tk), lambda i, j, k: (j, k)) if tb else pl.BlockSpec((tk, tn), lambda i, j, k: (k, j))
    dims = (((0,) if ta else (1,), (1,) if tb else (0,)), ((), ()))
    has_bias = bias is not None

    def body(*refs):
        a_ref, b_ref = refs[0], refs[1]
        o_ref, acc_ref = refs[-2], refs[-1]
        k = pl.program_id(2)

        @pl.when(k == 0)
        def _():
            acc_ref[...] = jnp.zeros_like(acc_ref)

        acc_ref[...] += lax.dot_general(a_ref[...].astype(BF16), b_ref[...].astype(BF16), dims,
                                        preferred_element_type=F32)

        @pl.when(k == nk - 1)
        def _():
            r = acc_ref[...]
            if has_bias:
                r = r + refs[2][...]
            o_ref[...] = r.astype(o_ref.dtype)

    in_specs = [a_spec, b_spec] + ([pl.BlockSpec((1, tn), lambda i, j, k: (0, j))] if has_bias else [])
    return pl.pallas_call(
        body, name=name, grid=(m // tm, n // tn, nk), in_specs=in_specs,
        out_specs=pl.BlockSpec((tm, tn), lambda i, j, k: (i, j)),
        out_shape=jax.ShapeDtypeStruct((m, n), out_dtype),
        scratch_shapes=[pltpu.VMEM((tm, tn), F32)],
        compiler_params=_cparams("parallel", "parallel", "arbitrary"),
    )(a, b, *([bias] if has_bias else []))


def _s5_scan(x, carry_tab, step_tab, *, reverse, hist=None, name):
    t, width = x.shape
    blk = 2 * SCAN_LANES
    ntile = t // V7X_SUBLANES
    with_da = hist is not None
    ln = SCAN_LANES
    rows8 = V7X_SUBLANES

    def body(*refs):
        if with_da:
            x_ref, ct_ref, st_ref, h_ref, o_ref, da_ref = refs
        else:
            x_ref, ct_ref, st_ref, o_ref = refs
        row = lax.broadcasted_iota(jnp.int32, (rows8, ln), 0)
        ctr, cti = ct_ref[:, :ln], ct_ref[:, ln:]
        powers = [(jnp.broadcast_to(st_ref[k:k + 1, :ln], (rows8, ln)),
                   jnp.broadcast_to(st_ref[k:k + 1, ln:], (rows8, ln))) for k in range(3)]
        zero = jnp.zeros((rows8, ln), F32)

        def step(n, carry):
            i = (ntile - 1 - n) if reverse else n
            r0 = pl.multiple_of(i * rows8, rows8)
            xr, xi = x_ref[pl.ds(r0, rows8), :ln], x_ref[pl.ds(r0, rows8), ln:]
            for k, (pr, pi) in zip((1, 2, 4), powers):
                shift, keep = (rows8 - k, row < rows8 - k) if reverse else (k, row >= k)
                sr = jnp.where(keep, pltpu.roll(xr, shift, 0), 0.0)
                si = jnp.where(keep, pltpu.roll(xi, shift, 0), 0.0)
                xr, xi = xr + pr * sr - pi * si, xi + pr * si + pi * sr
            cr, ci = carry[0], carry[1]
            yr = xr + ctr * cr - cti * ci
            yi = xi + ctr * ci + cti * cr
            o_ref[pl.ds(r0, rows8), :ln] = yr
            o_ref[pl.ds(r0, rows8), ln:] = yi
            edge = 0 if reverse else rows8 - 1
            new = (jnp.broadcast_to(yr[edge:edge + 1, :], (rows8, ln)), jnp.broadcast_to(yi[edge:edge + 1, :], (rows8, ln)))
            if not with_da:
                return new
            hr, hi = h_ref[pl.ds(r0, rows8), :ln], h_ref[pl.ds(r0, rows8), ln:]
            rp = pl.multiple_of(jnp.maximum(i - 1, 0) * rows8, rows8)
            last_r = jnp.where(i > 0, jnp.broadcast_to(h_ref[pl.ds(rp, rows8), :ln][rows8 - 1:, :], (rows8, ln)), 0.0)
            last_i = jnp.where(i > 0, jnp.broadcast_to(h_ref[pl.ds(rp, rows8), ln:][rows8 - 1:, :], (rows8, ln)), 0.0)
            hpr = jnp.where(row == 0, last_r, pltpu.roll(hr, 1, 0))
            hpi = jnp.where(row == 0, last_i, pltpu.roll(hi, 1, 0))
            return new + (carry[2] + yr * hpr + yi * hpi, carry[3] + yi * hpr - yr * hpi)

        init = (zero, zero, zero, zero) if with_da else (zero, zero)
        out = lax.fori_loop(0, ntile, step, init, unroll=2)
        if with_da:
            da_ref[:, :ln] = out[2]
            da_ref[:, ln:] = out[3]

    col = pl.BlockSpec((t, blk), lambda j: (0, j))
    tab = pl.BlockSpec((rows8, blk), lambda j: (0, j))
    in_specs = [col, tab, tab] + ([col] if with_da else [])
    out_specs = [col] + ([tab] if with_da else [])
    out_shape = [jax.ShapeDtypeStruct((t, width), F32)] + ([jax.ShapeDtypeStruct((rows8, width), F32)] if with_da else [])
    res = pl.pallas_call(
        body, name=name, grid=(width // blk,), in_specs=in_specs, out_specs=out_specs, out_shape=out_shape,
        compiler_params=_cparams("parallel"),
    )(x, carry_tab, step_tab, *([hist] if with_da else []))
    return res if with_da else res[0]


ATT_SCALE = (QK_NOPE + QK_ROPE) ** -0.5


def _att_mask(qi, kj, tb):
    qc = (qi * tb + lax.broadcasted_iota(jnp.int32, (tb, tb), 0)) // CHUNK
    kc = (kj * tb + lax.broadcasted_iota(jnp.int32, (tb, tb), 1)) // CHUNK
    return kc <= qc


def _attn_fwd(q, k, v, *, name):
    h, t, dq = q.shape
    dv = v.shape[2]
    tb = min(ATT_BLOCK, t)
    nt = (((1,), (1,)), ((), ()))

    def body(q_ref, k_ref, v_ref, o_ref, lse_ref):
        i = pl.program_id(1)
        qb = q_ref[...]

        def kv_step(j, carry):
            m, l, acc = carry
            r0 = pl.multiple_of(j * tb, tb)
            s = lax.dot_general(qb, k_ref[pl.ds(r0, tb), :], nt, preferred_element_type=F32) * ATT_SCALE
            s = jnp.where(_att_mask(i, j, tb), s, NEG_INF)
            m_new = jnp.maximum(m, jnp.max(s, axis=1, keepdims=True))
            alpha = jnp.exp(m - m_new)
            p = jnp.exp(s - m_new)
            l = alpha * l + jnp.sum(p, axis=1, keepdims=True)
            acc = alpha * acc + jnp.dot(p.astype(BF16), v_ref[pl.ds(r0, tb), :], preferred_element_type=F32)
            return m_new, l, acc

        init = (jnp.full((tb, 1), NEG_INF, F32), jnp.zeros((tb, 1), F32), jnp.zeros((tb, dv), F32))
        m, l, acc = lax.fori_loop(0, i + 1, kv_step, init)
        o_ref[...] = acc / l
        lse_ref[...] = m + jnp.log(l)

    return pl.pallas_call(
        body, name=name, grid=(h, t // tb),
        in_specs=[pl.BlockSpec((None, tb, dq), lambda hh, i: (hh, i, 0)),
                  pl.BlockSpec((None, t, dq), lambda hh, i: (hh, 0, 0)),
                  pl.BlockSpec((None, t, dv), lambda hh, i: (hh, 0, 0))],
        out_specs=[pl.BlockSpec((None, tb, dv), lambda hh, i: (hh, i, 0)),
                   pl.BlockSpec((None, tb, 1), lambda hh, i: (hh, i, 0))],
        out_shape=[jax.ShapeDtypeStruct((h, t, dv), F32), jax.ShapeDtypeStruct((h, t, 1), F32)],
        compiler_params=_cparams("parallel", "parallel"),
    )(q, k, v)


def _attn_bwd(q, k, v, o, lse, do, *, name):
    h, t, dq_w = q.shape
    dv_w = v.shape[2]
    tb = min(ATT_BLOCK, t)
    nblk = t // tb
    nt = (((1,), (1,)), ((), ()))
    tn = (((0,), (0,)), ((), ()))

    def body(q_ref, k_ref, v_ref, o_ref, lse_ref, do_ref, dq_ref, dk_ref, dv_ref, delta_ref):
        j = pl.program_id(1)

        @pl.when(j == 0)
        def _():
            dq_ref[...] = jnp.zeros_like(dq_ref)

            def dstep(i, c):
                r0 = pl.multiple_of(i * tb, tb)
                delta_ref[pl.ds(r0, tb), :] = jnp.sum(do_ref[pl.ds(r0, tb), :].astype(F32) * o_ref[pl.ds(r0, tb), :],
                                                      axis=1, keepdims=True)
                return c

            lax.fori_loop(0, nblk, dstep, 0)

        kb, vb = k_ref[...], v_ref[...]

        def q_step(i, carry):
            dk, dv = carry
            r0 = pl.multiple_of(i * tb, tb)
            qb, dob = q_ref[pl.ds(r0, tb), :], do_ref[pl.ds(r0, tb), :]
            s = lax.dot_general(qb, kb, nt, preferred_element_type=F32) * ATT_SCALE
            s = jnp.where(_att_mask(i, j, tb), s, NEG_INF)
            p = jnp.exp(s - lse_ref[pl.ds(r0, tb), :])
            dv = dv + lax.dot_general(p.astype(BF16), dob, tn, preferred_element_type=F32)
            dp = lax.dot_general(dob, vb, nt, preferred_element_type=F32)
            ds = (p * (dp - delta_ref[pl.ds(r0, tb), :]) * ATT_SCALE).astype(BF16)
            dk = dk + lax.dot_general(ds, qb, tn, preferred_element_type=F32)
            dq_ref[pl.ds(r0, tb), :] += jnp.dot(ds, kb, preferred_element_type=F32)
            return dk, dv

        dk, dv = lax.fori_loop(j, nblk, q_step, (jnp.zeros((tb, dq_w), F32), jnp.zeros((tb, dv_w), F32)))
        dk_ref[...] = dk
        dv_ref[...] = dv

    whole = lambda w: pl.BlockSpec((None, t, w), lambda hh, j: (hh, 0, 0))
    blockj = lambda w: pl.BlockSpec((None, tb, w), lambda hh, j: (hh, j, 0))
    return pl.pallas_call(
        body, name=name, grid=(h, nblk),
        in_specs=[whole(dq_w), blockj(dq_w), blockj(dv_w), whole(dv_w), whole(1), whole(dv_w)],
        out_specs=[whole(dq_w), blockj(dq_w), blockj(dv_w)],
        out_shape=[jax.ShapeDtypeStruct((h, t, dq_w), F32), jax.ShapeDtypeStruct((h, t, dq_w), F32),
                   jax.ShapeDtypeStruct((h, t, dv_w), F32)],
        scratch_shapes=[pltpu.VMEM((t, 1), F32)],
        compiler_params=_cparams("parallel", "arbitrary"),
    )(q, k, v, o, lse, do)


def _exchange(x, *, gather, name):
    r, c = x.shape[-2:]

    def body(x_ref, y_ref, send_sems, recv_sems, local_sem):
        mx, my, mc = lax.axis_index("x"), lax.axis_index("y"), lax.axis_index("c")
        me = 4 * mx + 2 * my + mc
        local = pltpu.make_async_copy(x_ref if gather else x_ref.at[me], y_ref.at[me], local_sem)
        local.start()
        copies = []
        for k in range(1, N_DEV):
            px = 1 - mx if k & 4 else mx
            py = 1 - my if k & 2 else my
            pc = 1 - mc if k & 1 else mc
            cp = pltpu.make_async_remote_copy(
                src_ref=x_ref if gather else x_ref.at[4 * px + 2 * py + pc], dst_ref=y_ref.at[me],
                send_sem=send_sems.at[k - 1], recv_sem=recv_sems.at[k - 1],
                device_id=(px, py, pc), device_id_type=pl.DeviceIdType.MESH)
            cp.start()
            copies.append(cp)
        for cp in copies:
            cp.wait()
        local.wait()

    return pl.pallas_call(
        body, name=name, out_shape=jax.ShapeDtypeStruct((N_DEV, r, c), x.dtype),
        in_specs=[pl.BlockSpec(memory_space=pl.ANY)], out_specs=pl.BlockSpec(memory_space=pl.ANY),
        scratch_shapes=[pltpu.SemaphoreType.DMA((N_DEV - 1,)), pltpu.SemaphoreType.DMA((N_DEV - 1,)),
                        pltpu.SemaphoreType.DMA(())],
        compiler_params=pltpu.CompilerParams(has_side_effects=True),
    )(x)


def _adamw(w, g, m, v, *, name, tm=24):
    parts = g.ndim == 3
    r, c = w.shape
    tm = _pick_rows(r, tm)

    def body(w_ref, g_ref, m_ref, v_ref, go_ref, d_ref, mo_ref, vo_ref):
        if parts:
            gv = g_ref[0]
            for k in range(1, N_DEV):
                gv = gv + g_ref[k]
        else:
            gv = g_ref[...]
        mn = ADAM_B1 * m_ref[...] + (1.0 - ADAM_B1) * gv
        vn = ADAM_B2 * v_ref[...] + (1.0 - ADAM_B2) * jnp.square(gv)
        m_hat = mn / (1.0 - ADAM_B1 ** ADAM_STEP)
        v_hat = vn / (1.0 - ADAM_B2 ** ADAM_STEP)
        go_ref[...] = gv
        d_ref[...] = -ADAM_LR * (m_hat / (jnp.sqrt(v_hat) + ADAM_EPS) + ADAM_WD * w_ref[...])
        mo_ref[...] = mn
        vo_ref[...] = vn

    spec = pl.BlockSpec((tm, c), lambda i: (i, 0))
    gspec = pl.BlockSpec((N_DEV, tm, c), lambda i: (0, i, 0)) if parts else spec
    return pl.pallas_call(
        body, name=name, grid=(r // tm,), in_specs=[spec, gspec, spec, spec], out_specs=[spec] * 4,
        out_shape=[jax.ShapeDtypeStruct((r, c), F32)] * 4, compiler_params=_cparams("parallel"),
    )(w, g, m, v)


def _permute_in(a):
    pad = jnp.zeros(a.shape[:-1] + (IN_PAD - IN_WIDTH,), a.dtype)
    return jnp.concatenate([a[..., lo:hi] for lo, hi in _IN_SEGMENTS] + [pad], axis=-1)


def _unpermute_in(a):
    out, pos = {}, 0
    for lo, hi in _IN_SEGMENTS:
        out[lo] = a[..., pos:pos + hi - lo]
        pos += hi - lo
    return jnp.concatenate([out[lo] for lo in sorted(out)], axis=-1)


def _pack_layer(shards, dtype):
    flat = jnp.concatenate([shards[n].astype(dtype).reshape(-1) for n, _ in SHARDED])
    return flat.reshape(-1, 1024)


def _shard_shapes(per_layer):
    return [(n, kind, per_layer[n].shape) for n, kind in SHARDED]


def _unpack_full(gathered, shapes):
    flat = gathered.reshape(N_DEV, -1)
    out, off = {}, 0
    for n, kind, shp in shapes:
        size = int(np.prod(shp))
        piece = flat[:, off:off + size].reshape((N_DEV,) + shp)
        off += size
        if kind == 'row':
            out[n] = piece.reshape(N_DEV * shp[0], shp[1])
        elif kind == 'col':
            out[n] = piece.transpose(1, 0, 2).reshape(shp[0], N_DEV * shp[1])
        else:
            out[n] = piece.transpose(1, 2, 0, 3).reshape(shp[0], shp[1], N_DEV * shp[2])
    return out


def _pack_contrib(full, shapes):
    pieces = []
    for n, kind, shp in shapes:
        g = full[n]
        if kind == 'row':
            pieces.append(g.reshape(N_DEV, -1))
        elif kind == 'col':
            pieces.append(g.reshape(shp[0], N_DEV, shp[1]).transpose(1, 0, 2).reshape(N_DEV, -1))
        else:
            pieces.append(g.reshape(shp[0], shp[1], N_DEV, shp[2]).transpose(2, 0, 1, 3).reshape(N_DEV, -1))
    return jnp.concatenate(pieces, axis=1).reshape(N_DEV, -1, 1024)


def _unpack_shards(flat2d, shapes):
    flat = flat2d.reshape(-1)
    out, off = {}, 0
    for n, _, shp in shapes:
        size = int(np.prod(shp))
        out[n] = flat[off:off + size].reshape(shp)
        off += size
    return out


def _chan_cols(re, im):
    lead = re.shape[:-1]
    nb = S5_CH // SCAN_LANES
    return jnp.stack([re.reshape(lead + (nb, SCAN_LANES)), im.reshape(lead + (nb, SCAN_LANES))],
                     axis=-2).reshape(lead + (2 * S5_CH,))


def _s5_tables(lam_re, lam_im, log_dt, b_re, b_im, c_re, c_im):
    dt = jnp.exp(log_dt)[:, None]
    mag = jnp.exp(lam_re * dt)
    a_re = mag * jnp.cos(lam_im * dt)
    a_im = mag * jnp.sin(lam_im * dt)
    den = lam_re * lam_re + lam_im * lam_im
    f_re = ((a_re - 1.0) * lam_re + a_im * lam_im) / den
    f_im = (a_im * lam_re - (a_re - 1.0) * lam_im) / den
    bb_re = f_re[..., None] * b_re - f_im[..., None] * b_im
    bb_im = f_re[..., None] * b_im + f_im[..., None] * b_re
    eye = jnp.eye(S5_GROUPS, dtype=F32)
    wb_re = jnp.einsum('gpc,gh->gchp', bb_re, eye).reshape(S5_WIDTH, S5_CH)
    wb_im = jnp.einsum('gpc,gh->gchp', bb_im, eye).reshape(S5_WIDTH, S5_CH)
    wb = _chan_cols(wb_re, wb_im)
    wc_re = jnp.einsum('gcp,gh->hcgp', c_re, eye).reshape(S5_WIDTH, S5_CH)
    wc_im = jnp.einsum('gcp,gh->hcgp', c_im, eye).reshape(S5_WIDTH, S5_CH)
    wc = _chan_cols(wc_re, -wc_im).T
    a_row = _chan_cols(a_re.reshape(1, S5_CH), a_im.reshape(1, S5_CH))
    return wb, wc, a_row


def _scan_tables(a_row, conj):
    nb = S5_CH // SCAN_LANES
    a = a_row.reshape(nb, 2, SCAN_LANES)
    ar, ai = a[:, 0], (-a[:, 1] if conj else a[:, 1])
    pr, pi = [ar], [ai]
    for _ in range(V7X_SUBLANES - 1):
        pr, pi = pr + [pr[-1] * ar - pi[-1] * ai], pi + [pr[-1] * ai + pi[-1] * ar]
    order = range(V7X_SUBLANES - 1, -1, -1) if conj else range(V7X_SUBLANES)
    carry = jnp.stack([jnp.stack([pr[r], pi[r]], axis=1).reshape(-1) for r in order])
    zero = jnp.zeros_like(carry[0])
    step = jnp.stack([jnp.stack([pr[r], pi[r]], axis=1).reshape(-1) for r in (0, 1, 3)] + [zero] * 5)
    return carry, step


def _rope_tables(t):
    half = QK_ROPE // 2
    inv_freq = 1.0 / (ROPE_THETA ** (jnp.arange(0, QK_ROPE, 2, dtype=F32) / QK_ROPE))
    ang = jnp.arange(t, dtype=F32)[:, None] * inv_freq[None, :]
    cos, sin = jnp.cos(ang), jnp.sin(ang)
    zero = jnp.zeros_like(sin)

    def lay(nope, width, first, second):
        head = jnp.concatenate([jnp.full((t, nope), 1.0 if first is cos else 0.0, F32), first, second], axis=1)
        reps = width // head.shape[1]
        out = jnp.tile(head, (1, reps))
        return jnp.pad(out, ((0, 0), (0, width - out.shape[1])))

    hq = MLA_HEADS * (QK_NOPE + QK_ROPE)
    q_tabs = (lay(QK_NOPE, hq, cos, cos), lay(QK_NOPE, hq, -sin, zero), lay(QK_NOPE, hq, zero, sin))
    k_tabs = (lay(0, V7X_LANES, cos, cos)[:, :V7X_LANES] * (jnp.arange(V7X_LANES) < QK_ROPE),
              lay(0, V7X_LANES, -sin, zero) * (jnp.arange(V7X_LANES) < QK_ROPE),
              lay(0, V7X_LANES, zero, sin) * (jnp.arange(V7X_LANES) < QK_ROPE))
    return q_tabs, k_tabs


def _sgu_tables(w_s, b_s):
    pos = jnp.arange(SGU_CHUNK) // CHUNK
    mask = pos[None, :] <= pos[:, None]
    wm = jnp.where(mask[None], w_s, 0.0).reshape(SGU_GROUPS * SGU_CHUNK, SGU_CHUNK)
    bias = jnp.repeat(b_s.T, SGU_WIDTH // SGU_GROUPS, axis=1)
    return wm, bias


def _row(v):
    return v.reshape(1, -1)


def _layer_fwd(x, ada, w, rope_tabs, tag):
    s = {'x': x}
    q_tabs, k_tabs = rope_tabs
    sc1, gt1, sc2, gt2 = _row(1.0 + ada[1]), _row(1.0 + ada[2]), _row(1.0 + ada[4]), _row(1.0 + ada[5])
    s.update(sc1=sc1, gt1=gt1, sc2=sc2, gt2=gt2)
    (h,) = _rowcall(_modulate_fn, [x], [sc1, _row(ada[0])], [(D_MODEL, BF16)], tm=512, name=f"mod1_{tag}")
    proj = _mm(h, w['w_in_p'], bias=w['b_in_p'], name=f"proj_{tag}", tn=768)
    s.update(h=h, proj=proj)

    (wb, wc, a_row), s['s5_pull'] = jax.vjp(_s5_tables, *[w[n] for n in ('s5_lambda_re', 's5_lambda_im', 's5_log_dt',
                                                                         's5_b_re', 's5_b_im', 's5_c_re', 's5_c_im')])
    wb, wc = wb.astype(BF16), wc.astype(BF16)
    u_s5 = proj[:, P_S5:P_S5 + S5_WIDTH]
    bu = _mm(u_s5, wb, name=f"s5_bu_{tag}")
    hs = _s5_scan(bu, *_scan_tables(a_row, False), reverse=False, name=f"s5_scan_{tag}")
    ylin = _mm(hs, wc, name=f"s5_c_{tag}")
    s5_full = [_row(w['s5_d']), w['s5_w_glu'], _row(w['s5_b_glu'])]
    (y_s5,) = _rowcall(_s5_post_fn, [ylin, u_s5], s5_full, [(S5_WIDTH, BF16)], tm=256, name=f"s5_post_{tag}")
    s.update(wb=wb, wc=wc, a_row=a_row, u_s5=u_s5, hs=hs, ylin=ylin, y_s5=y_s5)

    mla_rows = [(proj, MLA_BLK, P_MLA // MLA_BLK), *q_tabs, *k_tabs]
    mla_full = [_row(w['mla_q_norm']), w['mla_w_q_up'], _row(w['mla_kv_norm']), w['mla_w_kv_up']]
    hq, hkv = MLA_HEADS * (QK_NOPE + QK_ROPE), MLA_HEADS * (QK_NOPE + V_HEAD)
    q_r, kv, kpe_r = _rowcall(_mla_pre_fn, mla_rows, mla_full, [(hq, BF16), (hkv, BF16), (V7X_LANES, BF16)],
                              tm=256, name=f"mla_pre_{tag}")
    t = x.shape[0]
    qh = q_r.reshape(t, MLA_HEADS, -1).transpose(1, 0, 2)
    kv3 = kv.reshape(t, MLA_HEADS, -1).transpose(1, 0, 2)
    kh = jnp.concatenate([kv3[:, :, :QK_NOPE], jnp.broadcast_to(kpe_r[None, :, :QK_ROPE], (MLA_HEADS, t, QK_ROPE))], axis=2)
    vh = kv3[:, :, QK_NOPE:]
    o, lse = _attn_fwd(qh, kh, vh, name=f"attn_fwd_{tag}")
    y_mla = o.transpose(1, 0, 2).reshape(t, -1).astype(BF16)
    s.update(qh=qh, kh=kh, vh=vh, o=o, lse=lse, y_mla=y_mla)

    (wm, bias), s['sgu_pull'] = jax.vjp(_sgu_tables, w['sgu_w_s'], w['sgu_b_s'])
    sgu_rows = [(proj, SGU_WIDTH, P_USGU // SGU_WIDTH), (proj, SGU_WIDTH, P_VSGU // SGU_WIDTH)]
    sgu_full = [_row(w['sgu_ln_g']), _row(w['sgu_ln_b']), wm, bias]
    (y_sgu,) = _rowcall(_sgu_fn, sgu_rows, sgu_full, [(SGU_WIDTH, BF16)], tm=SGU_CHUNK, name=f"sgu_{tag}")
    s.update(sgu_full=sgu_full, y_sgu=y_sgu)

    wbr = w['w_branch'].reshape(-1, D_MODEL)
    gate_rows = [(proj, D_MODEL, b) for b in range(3)]
    (merged,) = _rowcall(_merge_fn, [y_s5, y_mla, y_sgu] + gate_rows, [wbr], [(D_MODEL, BF16)], tm=256, name=f"merge_{tag}")
    ymix = _mm(merged, w['w_out'], name=f"wout_{tag}")
    (x1,) = _rowcall(_ln_res_fn, [x, ymix], [gt1, _row(w['ln1_g']), _row(w['ln1_b'])], [(D_MODEL, F32)], tm=256,
                     name=f"ln1_{tag}")
    s.update(merged=merged, ymix=ymix, x1=x1)

    (h2,) = _rowcall(_modulate_fn, [x1], [sc2, _row(ada[3])], [(D_MODEL, BF16)], tm=512, name=f"mod2_{tag}")
    ab = _mm(h2, w['ffn_w_in'], name=f"ffn_in_{tag}", tn=512)
    (act,) = _rowcall(_swiglu_fn, [(ab, FF_HIDDEN, 0), (ab, FF_HIDDEN, 1)], [], [(FF_HIDDEN, BF16)], tm=256,
                      name=f"swiglu_{tag}")
    f = _mm(act, w['ffn_w_out'], name=f"ffn_out_{tag}", tk=2816)
    (x2,) = _rowcall(_ln_res_fn, [x1, f], [gt2, _row(w['ln2_g']), _row(w['ln2_b'])], [(D_MODEL, F32)], tm=256,
                     name=f"ln2_{tag}")
    s.update(h2=h2, ab=ab, act=act, f=f)
    return x2, s


def _mod_bwd_fn(x, dh, dxa, scale_row):
    return (dxa + dh * scale_row, jnp.sum(dh * x, axis=0, keepdims=True), jnp.sum(dh, axis=0, keepdims=True))


def _layer_bwd(dx2, s, w, rope_tabs, tag):
    g = {}
    q_tabs, k_tabs = rope_tabs
    t = dx2.shape[0]
    ln_full = lambda gt, a, b: [gt, _row(w[a]), _row(w[b])]

    dx1_a, df, dgt2, g['ln2_g'], g['ln2_b'] = _rowcall_vjp(
        _ln_res_fn, [s['x1'], s['f']], ln_full(s['gt2'], 'ln2_g', 'ln2_b'), [dx2], [0, 1], [0, 1, 2],
        tm=256, name=f"ln2_bwd_{tag}", row_dtypes=[F32, BF16])
    dact = _mm(df, w['ffn_w_out'], tb=True, name=f"ffn_out_dx_{tag}")
    g['ffn_w_out'] = _mm(s['act'], df, ta=True, name=f"ffn_out_dw_{tag}")
    da_, db_ = _rowcall_vjp(_swiglu_fn, [(s['ab'], FF_HIDDEN, 0), (s['ab'], FF_HIDDEN, 1)], [], [dact], [0, 1], [],
                            tm=256, name=f"swiglu_bwd_{tag}", row_dtypes=[BF16, BF16])
    dab = jnp.concatenate([da_, db_], axis=1)
    dh2 = _mm(dab, w['ffn_w_in'], tb=True, name=f"ffn_in_dx_{tag}", tk=512)
    g['ffn_w_in'] = _mm(s['h2'], dab, ta=True, name=f"ffn_in_dw_{tag}", tn=512)
    dx1, dsc2, dsh2 = _rowcall(_mod_bwd_fn, [s['x1'], dh2, dx1_a], [s['sc2']], [(D_MODEL, F32)],
                               [((1, D_MODEL), F32)] * 2, tm=256, name=f"mod2_bwd_{tag}")

    dx_a, dymix, dgt1, g['ln1_g'], g['ln1_b'] = _rowcall_vjp(
        _ln_res_fn, [s['x'], s['ymix']], ln_full(s['gt1'], 'ln1_g', 'ln1_b'), [dx1], [0, 1], [0, 1, 2],
        tm=256, name=f"ln1_bwd_{tag}", row_dtypes=[F32, BF16])
    dmerged = _mm(dymix, w['w_out'], tb=True, name=f"wout_dx_{tag}")
    g['w_out'] = _mm(s['merged'], dymix, ta=True, name=f"wout_dw_{tag}")

    proj = s['proj']
    wbr = w['w_branch'].reshape(-1, D_MODEL)
    gate_rows = [(proj, D_MODEL, b) for b in range(3)]
    dy_s5, dy_mla, dy_sgu, dl0, dl1, dl2, dwbr = _rowcall_vjp(
        _merge_fn, [s['y_s5'], s['y_mla'], s['y_sgu']] + gate_rows, [wbr], [dmerged], [0, 1, 2, 3, 4, 5], [0],
        tm=256, name=f"merge_bwd_{tag}")
    g['w_branch'] = dwbr.reshape(w['w_branch'].shape)

    sgu_rows = [(proj, SGU_WIDTH, P_USGU // SGU_WIDTH), (proj, SGU_WIDTH, P_VSGU // SGU_WIDTH)]
    du_sgu, dv_sgu, dlg, dlb, dwm, dbias = _rowcall_vjp(
        _sgu_fn, sgu_rows, s['sgu_full'], [dy_sgu], [0, 1], [0, 1, 2, 3], tm=SGU_CHUNK, name=f"sgu_bwd_{tag}")
    g['sgu_ln_g'], g['sgu_ln_b'] = dlg.reshape(-1), dlb.reshape(-1)
    g['sgu_w_s'], g['sgu_b_s'] = s['sgu_pull']((dwm, dbias))

    do = dy_mla.reshape(t, MLA_HEADS, V_HEAD).transpose(1, 0, 2).astype(BF16)
    dqh, dkh, dvh = _attn_bwd(s['qh'], s['kh'], s['vh'], s['o'], s['lse'], do, name=f"attn_bwd_{tag}")
    dq_r = dqh.transpose(1, 0, 2).reshape(t, -1)
    dkv = jnp.concatenate([dkh[:, :, :QK_NOPE], dvh], axis=2).transpose(1, 0, 2).reshape(t, -1)
    dkpe = jnp.pad(jnp.sum(dkh[:, :, QK_NOPE:], axis=0), ((0, 0), (0, V7X_LANES - QK_ROPE)))
    mla_rows = [(proj, MLA_BLK, P_MLA // MLA_BLK), *q_tabs, *k_tabs]
    mla_full = [_row(w['mla_q_norm']), w['mla_w_q_up'], _row(w['mla_kv_norm']), w['mla_w_kv_up']]
    dmla, dqn, g['mla_w_q_up'], dkvn, g['mla_w_kv_up'] = _rowcall_vjp(
        _mla_pre_fn, mla_rows, mla_full, [dq_r, dkv, dkpe], [0], [0, 1, 2, 3], tm=256, name=f"mla_pre_bwd_{tag}")
    g['mla_q_norm'], g['mla_kv_norm'] = dqn.reshape(-1), dkvn.reshape(-1)

    s5_full = [_row(w['s5_d']), w['s5_w_glu'], _row(w['s5_b_glu'])]
    dylin, du_a, dd, g['s5_w_glu'], dbg = _rowcall_vjp(
        _s5_post_fn, [s['ylin'], s['u_s5']], s5_full, [dy_s5], [0, 1], [0, 1, 2], tm=256, name=f"s5_post_bwd_{tag}",
        row_dtypes=[BF16, F32])
    g['s5_d'], g['s5_b_glu'] = dd.reshape(-1), dbg.reshape(-1)
    dhs = _mm(dylin, s['wc'], tb=True, name=f"s5_c_dx_{tag}")
    dwc = _mm(s['hs'], dylin, ta=True, name=f"s5_c_dw_{tag}")
    gs, da_part = _s5_scan(dhs, *_scan_tables(s['a_row'], True), reverse=True, hist=s['hs'], name=f"s5_scan_bwd_{tag}")
    du_b = _mm(gs, s['wb'], tb=True, name=f"s5_bu_dx_{tag}")
    dwb = _mm(s['u_s5'], gs, ta=True, name=f"s5_bu_dw_{tag}")
    da_row = jnp.sum(da_part, axis=0, keepdims=True)
    for n, v in zip(('s5_lambda_re', 's5_lambda_im', 's5_log_dt', 's5_b_re', 's5_b_im', 's5_c_re', 's5_c_im'),
                    s['s5_pull']((dwb, dwc, da_row))):
        g[n] = v

    dproj_f32 = jnp.concatenate([dl0, dl1, dl2, du_a + du_b, du_sgu, dv_sgu, dmla], axis=1)
    dproj, db_in = _rowcall(lambda d: (d, jnp.sum(d, axis=0, keepdims=True)), [dproj_f32], [], [(IN_PAD, BF16)],
                            [((1, IN_PAD), F32)], tm=256, name=f"dproj_{tag}")
    dh = _mm(dproj, w['w_in_p'], tb=True, name=f"proj_dx_{tag}", tk=768)
    g['w_in'] = _unpermute_in(_mm(s['h'], dproj, ta=True, name=f"proj_dw_{tag}", tn=768))
    g['b_in'] = _unpermute_in(db_in).reshape(-1)
    dx, dsc1, dsh1 = _rowcall(_mod_bwd_fn, [s['x'], dh, dx_a], [s['sc1']], [(D_MODEL, F32)], [((1, D_MODEL), F32)] * 2,
                              tm=256, name=f"mod1_bwd_{tag}")
    d_ada = jnp.concatenate([dsh1, dsc1, dgt1, dsh2, dsc2, dgt2], axis=0)
    return dx, d_ada, g


def _loss_fn(y, target):
    err = y - target
    return (err / D_MODEL, 0.5 * jnp.sum(jnp.sum(err * err, axis=1, keepdims=True), axis=0, keepdims=True) / D_MODEL)


def _step(p):
    me = 4 * lax.axis_index("x") + 2 * lax.axis_index("y") + lax.axis_index("c")
    x = p['x'][0]
    t = x.shape[0]
    rope_tabs = _rope_tables(t)

    c_all = _exchange(jnp.broadcast_to(p['c'], (V7X_SUBLANES, D_MODEL)), gather=True, name="gather_c")[:, 0, :]
    (c_act,) = _rowcall(lambda cc: (cc * _sigmoid(cc),), [c_all], [], [(D_MODEL, F32)], tm=N_DEV, name="c_silu")
    ncol = p['w_ada'].shape[2]
    b_ada_loc = lax.dynamic_slice_in_dim(p['b_ada'], me * ncol, ncol, axis=1)
    ada_cols = jnp.concatenate([_mm(c_act, p['w_ada'][l], bias=b_ada_loc[l:l + 1], name=f"ada_{l}") for l in range(DEPTH)])
    ada_all = _exchange(ada_cols, gather=True, name="gather_ada").reshape(N_DEV, DEPTH, N_DEV, ncol)
    ada = lax.dynamic_index_in_dim(ada_all, me, axis=2, keepdims=False)
    ada = ada.transpose(1, 0, 2).reshape(DEPTH, 6, D_MODEL)

    shapes = _shard_shapes({n: p[n][0] for n, _ in SHARDED})
    packs_f32 = [_pack_layer({n: p[n][l] for n, _ in SHARDED}, F32) for l in range(DEPTH)]
    layers = []
    for l in range(DEPTH):
        gathered = _exchange(packs_f32[l].astype(BF16), gather=True, name=f"gather_w_{l}")
        w = _unpack_full(gathered, shapes)
        w['w_in_p'] = _permute_in(w.pop('w_in'))
        w['b_in_p'] = _row(_permute_in(p['b_in'][l]))
        for n in SMALL:
            if n != 'b_ada' and n != 'b_in':
                w[n] = p[n][l]
        layers.append(w)

    saved = []
    for l in range(DEPTH):
        x, s = _layer_fwd(x, ada[l], layers[l], rope_tabs, f"l{l}")
        saved.append(s)
    dy, loss_loc = _rowcall(_loss_fn, [x, p['loss_target'][0]], [], [(D_MODEL, F32)], [((1, 1), F32)], tm=256, name="loss")
    loss = lax.psum(loss_loc[0, 0], ("x", "y", "c"))

    d_ada, grads = [None] * DEPTH, [None] * DEPTH
    dx = dy
    for l in reversed(range(DEPTH)):
        dx, d_ada[l], grads[l] = _layer_bwd(dx, saved[l], layers[l], rope_tabs, f"l{l}")
    d_ada = jnp.stack(d_ada).reshape(DEPTH, 6 * D_MODEL)

    d_ada_all = _exchange(d_ada, gather=True, name="gather_dada")
    d_ada_cols = lax.dynamic_slice_in_dim(d_ada_all, me * ncol, ncol, axis=2)
    pad_b = ((0, V7X_LANES - N_DEV), (0, 0))
    c_act_p = jnp.pad(c_act, pad_b)
    g_w_ada = jnp.stack([_mm(c_act_p, jnp.pad(d_ada_cols[:, l], pad_b), ta=True, name=f"ada_dw_{l}") for l in range(DEPTH)])

    out = {}
    for l in range(DEPTH):
        landed = _exchange(_pack_contrib(grads[l], shapes), gather=False, name=f"scatter_g_{l}")
        m_pack = _pack_layer({n: p['m_' + n][l] for n, _ in SHARDED}, F32)
        v_pack = _pack_layer({n: p['v_' + n][l] for n, _ in SHARDED}, F32)
        res = _adamw(packs_f32[l], landed, m_pack, v_pack, name=f"adamw_l{l}")
        for kind, flat in zip(('grad_', 'delta_', 'new_m_', 'new_v_'), res):
            for n, v in _unpack_shards(flat, shapes).items():
                out.setdefault(kind + n, []).append(v)
    out = {k: jnp.stack(v) for k, v in out.items()}

    r_ada = DEPTH * D_MODEL
    res = _adamw(p['w_ada'].reshape(r_ada, ncol), g_w_ada.reshape(r_ada, ncol), p['m_w_ada'].reshape(r_ada, ncol),
                 p['v_w_ada'].reshape(r_ada, ncol), name="adamw_ada", tm=256)
    for kind, flat in zip(('grad_', 'delta_', 'new_m_', 'new_v_'), res):
        out[kind + 'w_ada'] = flat.reshape(p['w_ada'].shape)

    small_g = {n: jnp.stack([grads[l][n] for l in range(DEPTH)]) for n in SMALL if n != 'b_ada'}
    small_g['b_ada'] = d_ada
    n_small = sum(int(np.prod(p[n].shape)) for n in SMALL)
    n_pad = -(-n_small // SMALL_PAD) * SMALL_PAD
    chunk_rows = n_pad // N_DEV // 1024

    def flat_small(get):
        flat = jnp.concatenate([get(n).reshape(-1) for n in SMALL])
        return jnp.pad(flat, (0, n_pad - n_small)).reshape(N_DEV, chunk_rows, 1024)

    landed = _exchange(flat_small(lambda n: small_g[n]), gather=False, name="scatter_small")
    mine = lambda pre: lax.dynamic_index_in_dim(flat_small(lambda n: p[pre + n]), me, axis=0, keepdims=False)
    res = _adamw(mine(''), landed, mine('m_'), mine('v_'), name="adamw_small", tm=56)
    res_all = _exchange(jnp.concatenate(res, axis=0), gather=True, name="gather_small")
    res_all = res_all.reshape(N_DEV, 4, chunk_rows * 1024).transpose(1, 0, 2).reshape(4, n_pad)
    off = 0
    for n in SMALL:
        size = int(np.prod(p[n].shape))
        for k, kind in enumerate(('grad_', 'delta_', 'new_m_', 'new_v_')):
            out[kind + n] = res_all[k, off:off + size].reshape(p[n].shape)
        off += size

    outs = [loss, dx[None]]
    for kind in ('grad_', 'delta_', 'new_m_', 'new_v_'):
        outs += [out[kind + n] for n in WNAMES]
    return tuple(outs)


def kernel(x, c, w_ada, b_ada, w_in, b_in, s5_lambda_re, s5_lambda_im, s5_log_dt, s5_b_re, s5_b_im, s5_c_re, s5_c_im, s5_d, s5_w_glu, s5_b_glu, mla_q_norm, mla_w_q_up, mla_kv_norm, mla_w_kv_up, sgu_ln_g, sgu_ln_b, sgu_w_s, sgu_b_s, w_branch, w_out, ln1_g, ln1_b, ffn_w_in, ffn_w_out, ln2_g, ln2_b, loss_target, m_w_ada, m_b_ada, m_w_in, m_b_in, m_s5_lambda_re, m_s5_lambda_im, m_s5_log_dt, m_s5_b_re, m_s5_b_im, m_s5_c_re, m_s5_c_im, m_s5_d, m_s5_w_glu, m_s5_b_glu, m_mla_q_norm, m_mla_w_q_up, m_mla_kv_norm, m_mla_w_kv_up, m_sgu_ln_g, m_sgu_ln_b, m_sgu_w_s, m_sgu_b_s, m_w_branch, m_w_out, m_ln1_g, m_ln1_b, m_ffn_w_in, m_ffn_w_out, m_ln2_g, m_ln2_b, v_w_ada, v_b_ada, v_w_in, v_b_in, v_s5_lambda_re, v_s5_lambda_im, v_s5_log_dt, v_s5_b_re, v_s5_b_im, v_s5_c_re, v_s5_c_im, v_s5_d, v_s5_w_glu, v_s5_b_glu, v_mla_q_norm, v_mla_w_q_up, v_mla_kv_norm, v_mla_w_kv_up, v_sgu_ln_g, v_sgu_ln_b, v_sgu_w_s, v_sgu_b_s, v_w_branch, v_w_out, v_ln1_g, v_ln1_b, v_ffn_w_in, v_ffn_w_out, v_ln2_g, v_ln2_b):
    return _step(dict(locals()))
```

```python
import functools
import math

import numpy as np
import jax
import jax.numpy as jnp
from jax import lax
from jax.experimental import pallas as pl
from jax.experimental.pallas import tpu as pltpu

F32 = jnp.float32
BF16 = jnp.bfloat16

N_DEV = 8
D_MODEL = 1024
DEPTH = 4
CHUNK = 64
S5_WIDTH = 512
S5_GROUP = 16
S5_GROUPS = 32
S5_STATE = 64
MLA_HEADS = 8
QK_NOPE = 64
QK_ROPE = 32
V_HEAD = 64
Q_LORA = 384
KV_LORA = 256
ROPE_THETA = 10000.0
SGU_WIDTH = 512
SGU_GROUPS = 4
SGU_CHUNK = 128
FF_HIDDEN = 2816
DEEPNORM_ALPHA = (2 * DEPTH) ** 0.25
LN_EPS = 1e-5
RMS_EPS = 1e-6
NEG_INF = -1e30
ADAM_LR = 0.001
ADAM_B1 = 0.9
ADAM_B2 = 0.999
ADAM_EPS = 1e-08
ADAM_WD = 0.01
ADAM_STEP = 10

IN_WIDTH = 5280
IN_PAD = 5376
_O_S5, _O_CQ, _O_CKV, _O_KPE, _O_USGU, _O_VSGU, _O_GATE = 0, 512, 896, 1152, 1184, 1696, 2208
_IN_SEGMENTS = ((_O_GATE, IN_WIDTH), (_O_S5, _O_CQ), (_O_USGU, _O_VSGU), (_O_VSGU, _O_GATE), (_O_CQ, _O_USGU))
P_GATE, P_S5, P_USGU, P_VSGU, P_MLA = 0, 3072, 3584, 4096, 4608
MLA_BLK = 768

V7X_LANES = 128
V7X_SUBLANES = 8
VMEM_LIMIT = 48 * 1024 * 1024
ATT_BLOCK = 512
SCAN_LANES = 128
S5_CH = S5_GROUPS * S5_STATE

WNAMES = ['w_ada', 'b_ada', 'w_in', 'b_in', 's5_lambda_re', 's5_lambda_im', 's5_log_dt', 's5_b_re', 's5_b_im',
          's5_c_re', 's5_c_im', 's5_d', 's5_w_glu', 's5_b_glu', 'mla_q_norm', 'mla_w_q_up', 'mla_kv_norm',
          'mla_w_kv_up', 'sgu_ln_g', 'sgu_ln_b', 'sgu_w_s', 'sgu_b_s', 'w_branch', 'w_out', 'ln1_g', 'ln1_b',
          'ffn_w_in', 'ffn_w_out', 'ln2_g', 'ln2_b']
SHARDED = (('w_in', 'col'), ('s5_w_glu', 'row'), ('mla_w_q_up', 'col'), ('mla_w_kv_up', 'col'),
           ('w_branch', 'col3'), ('w_out', 'row'), ('ffn_w_in', 'col'), ('ffn_w_out', 'row'))
SMALL = [n for n in WNAMES if n != 'w_ada' and n not in dict(SHARDED)]
SMALL_PAD = N_DEV * V7X_SUBLANES * 1024


def _cparams(*sem):
    return pltpu.CompilerParams(dimension_semantics=sem, vmem_limit_bytes=VMEM_LIMIT)


def _pick(n, target):
    if n <= target:
        return n
    best = None
    for d in range(V7X_LANES, target + 1, V7X_LANES):
        if n % d == 0:
            best = d
    assert best is not None, (n, target)
    return best


def _pick_rows(n, target):
    if n <= target:
        return n
    best = None
    for d in range(V7X_SUBLANES, target + 1, V7X_SUBLANES):
        if n % d == 0:
            best = d
    assert best is not None, (n, target)
    return best


@jax.custom_vjp
def _bdot(a, b):
    return jnp.dot(a.astype(BF16), b.astype(BF16), preferred_element_type=F32)


def _bdot_fwd(a, b):
    return _bdot(a, b), (a, b)


def _bdot_bwd(res, g):
    a, b = res
    gb = g.astype(BF16)
    da = lax.dot_general(gb, b.astype(BF16), (((1,), (1,)), ((), ())), preferred_element_type=F32)
    db = lax.dot_general(a.astype(BF16), gb, (((0,), (0,)), ((), ())), preferred_element_type=F32)
    return da.astype(a.dtype), db.astype(b.dtype)


_bdot.defvjp(_bdot_fwd, _bdot_bwd)


@functools.partial(jax.custom_vjp, nondiff_argnums=(1,))
def _lane_roll(x, shift):
    return pltpu.roll(x, shift % x.shape[1], 1)


def _lane_roll_fwd(x, shift):
    return _lane_roll(x, shift), None


def _lane_roll_bwd(shift, _, g):
    return (_lane_roll(g, -shift),)


_lane_roll.defvjp(_lane_roll_fwd, _lane_roll_bwd)


def _sigmoid(x):
    return 1.0 / (1.0 + jnp.exp(-x))


def _gelu(x):
    return 0.5 * x * (1.0 + jnp.tanh(math.sqrt(2.0 / math.pi) * (x + 0.044715 * (x * x * x))))


def _layer_norm(x, g, b):
    mu = jnp.mean(x, axis=-1, keepdims=True)
    var = jnp.mean(jnp.square(x - mu), axis=-1, keepdims=True)
    return (x - mu) * lax.rsqrt(var + LN_EPS) * g + b


def _rms_norm(x, g):
    return x * lax.rsqrt(jnp.mean(x * x, axis=-1, keepdims=True) + RMS_EPS) * g


def _rope(x, c, s1, s2):
    half = QK_ROPE // 2
    return x * c + _lane_roll(x, -half) * s1 + _lane_roll(x, half) * s2


def _modulate_fn(x, scale_row, shift_row):
    return (x * scale_row + shift_row,)


def _ln_res_fn(x, y, gate_row, g, b):
    return (_layer_norm(DEEPNORM_ALPHA * x + gate_row * y, g, b),)


def _s5_post_fn(ylin, u, d, w_glu, b_glu):
    z = _gelu(ylin + d * u)
    return (z * _sigmoid(_bdot(z, w_glu) + b_glu),)


def _mla_pre_fn(blk, cq_t, sq1, sq2, ck_t, sk1, sk2, q_norm, w_q, kv_norm, w_kv):
    cq, ckv, kpe = blk[:, :Q_LORA], blk[:, Q_LORA:Q_LORA + KV_LORA], blk[:, Q_LORA + KV_LORA:]
    q = _rope(_bdot(_rms_norm(cq, q_norm), w_q), cq_t, sq1, sq2)
    kv = _bdot(_rms_norm(ckv, kv_norm), w_kv)
    return q, kv, _rope(kpe, ck_t, sk1, sk2)


def _sgu_fn(u, v, g, b, wm, bias):
    vn = _layer_norm(_gelu(v), g, b)
    w = SGU_CHUNK
    parts = [_bdot(wm[k * w:(k + 1) * w, :], vn[:, k * w:(k + 1) * w]) for k in range(SGU_GROUPS)]
    return (_gelu(u) * (jnp.concatenate(parts, axis=1) + bias),)


def _merge_fn(y0, y1, y2, l0, l1, l2, wb):
    n = S5_WIDTH
    return (_sigmoid(l0) * _bdot(y0, wb[:n]) + _sigmoid(l1) * _bdot(y1, wb[n:2 * n])
            + _sigmoid(l2) * _bdot(y2, wb[2 * n:]),)


def _swiglu_fn(a, b):
    return (a * _sigmoid(a) * b,)


def _rowcall(fn, rows, fulls, out_rows, out_reds=(), *, tm, name):
    rows = [r if isinstance(r, tuple) else (r, r.shape[1], 0) for r in rows]
    t = rows[0][0].shape[0]
    tm = _pick_rows(t, tm)
    n_in, n_or, n_red = len(rows) + len(fulls), len(out_rows), len(out_reds)

    def body(*refs):
        vals = fn(*[r[...] for r in refs[:n_in]])
        assert len(vals) == n_or + n_red, (name, len(vals))
        for ref, v in zip(refs[n_in:n_in + n_or], vals[:n_or]):
            ref[...] = v.astype(ref.dtype)
        if n_red:
            red_refs = refs[n_in + n_or:]

            @pl.when(pl.program_id(0) == 0)
            def _():
                for ref in red_refs:
                    ref[...] = jnp.zeros_like(ref)

            for ref, v in zip(red_refs, vals[n_or:]):
                ref[...] += v.astype(ref.dtype)

    in_specs = [pl.BlockSpec((tm, w), functools.partial(lambda i, blk: (i, blk), blk=blk)) for _, w, blk in rows]
    in_specs += [pl.BlockSpec(f.shape, lambda i: (0, 0)) for f in fulls]
    out_specs = [pl.BlockSpec((tm, c), lambda i: (i, 0)) for c, _ in out_rows]
    out_specs += [pl.BlockSpec(s, lambda i: (0, 0)) for s, _ in out_reds]
    out_shape = [jax.ShapeDtypeStruct((t, c), dt) for c, dt in out_rows]
    out_shape += [jax.ShapeDtypeStruct(s, dt) for s, dt in out_reds]
    return pl.pallas_call(
        body, name=name, grid=(t // tm,), in_specs=in_specs, out_specs=out_specs, out_shape=out_shape,
        compiler_params=_cparams("arbitrary" if n_red else "parallel"),
    )(*[r[0] for r in rows], *fulls)


def _rowcall_vjp(fn, rows, fulls, cots, diff_rows, diff_fulls, *, tm, name, row_dtypes=None):
    rows_n = [r if isinstance(r, tuple) else (r, r.shape[1], 0) for r in rows]
    n_r, n_c = len(rows), len(cots)
    row_dtypes = row_dtypes or [F32] * len(diff_rows)

    def fn2(*vals):
        r = [v.astype(F32) for v in vals[:n_r]]
        ct = vals[n_r:n_r + n_c]
        f = [v.astype(F32) for v in vals[n_r + n_c:]]

        def g(*dargs):
            rr, ff = list(r), list(f)
            for k, idx in enumerate(diff_rows):
                rr[idx] = dargs[k]
            for k, idx in enumerate(diff_fulls):
                ff[idx] = dargs[len(diff_rows) + k]
            return fn(*rr, *ff)

        prim = [r[i] for i in diff_rows] + [f[i] for i in diff_fulls]
        outs, pull = jax.vjp(g, *prim)
        return pull(tuple(c.astype(o.dtype) for c, o in zip(ct, outs)))

    out_rows = [(rows_n[i][1], dt) for i, dt in zip(diff_rows, row_dtypes)]
    out_reds = [(fulls[i].shape, F32) for i in diff_fulls]
    return _rowcall(fn2, list(rows) + list(cots), fulls, out_rows, out_reds, tm=tm, name=name)


def _mm(a, b, *, ta=False, tb=False, bias=None, out_dtype=F32, name, tm=512, tn=1024, tk=1024):
    (k_a, m) = a.shape if ta else a.shape[::-1]
    (n, k_b) = b.shape if tb else b.shape[::-1]
    assert k_a == k_b, (name, a.shape, b.shape)
    tm, tn, tk = _pick(m, tm) if m % V7X_LANES == 0 else m, _pick(n, tn), _pick(k_a, tk) if k_a % V7X_LANES == 0 else k_a
    nk = k_a // tk
    a_spec = pl.BlockSpec((tk, tm), lambda i, j, k: (k, i)) if ta else pl.BlockSpec((tm, tk), lambda i, j, k: (i, k))
    b_spec = pl.BlockSpec((tn, tk), lambda i, j, k: (j, k)) if tb else pl.BlockSpec((tk, tn), lambda i, j, k: (k, j))
    dims = (((0,) if ta else (1,), (1,) if tb else (0,)), ((), ()))
    has_bias = bias is not None

    def body(*refs):
        a_ref, b_ref = refs[0], refs[1]
        o_ref, acc_ref = refs[-2], refs[-1]
        k = pl.program_id(2)

        @pl.when(k == 0)
        def _():
            acc_ref[...] = jnp.zeros_like(acc_ref)

        acc_ref[...] += lax.dot_general(a_ref[...].astype(BF16), b_ref[...].astype(BF16), dims,
                                        preferred_element_type=F32)

        @pl.when(k == nk - 1)
        def _():
            r = acc_ref[...]
            if has_bias:
                r = r + refs[2][...]
            o_ref[...] = r.astype(o_ref.dtype)

    in_specs = [a_spec, b_spec] + ([pl.BlockSpec((1, tn), lambda i, j, k: (0, j))] if has_bias else [])
    return pl.pallas_call(
        body, name=name, grid=(m // tm, n // tn, nk), in_specs=in_specs,
        out_specs=pl.BlockSpec((tm, tn), lambda i, j, k: (i, j)),
        out_shape=jax.ShapeDtypeStruct((m, n), out_dtype),
        scratch_shapes=[pltpu.VMEM((tm, tn), F32)],
        compiler_params=_cparams("parallel", "parallel", "arbitrary"),
    )(a, b, *([bias] if has_bias else []))


def _s5_scan(x, carry_tab, step_tab, *, reverse, hist=None, name):
    t, width = x.shape
    blk = 2 * SCAN_LANES
    ntile = t // V7X_SUBLANES
    with_da = hist is not None
    ln = SCAN_LANES
    rows8 = V7X_SUBLANES

    def body(*refs):
        if with_da:
            x_ref, ct_ref, st_ref, h_ref, o_ref, da_ref = refs
        else:
            x_ref, ct_ref, st_ref, o_ref = refs
        row = lax.broadcasted_iota(jnp.int32, (rows8, ln), 0)
        ctr, cti = ct_ref[:, :ln], ct_ref[:, ln:]
        powers = [(jnp.broadcast_to(st_ref[k:k + 1, :ln], (rows8, ln)),
                   jnp.broadcast_to(st_ref[k:k + 1, ln:], (rows8, ln))) for k in range(3)]
        zero = jnp.zeros((rows8, ln), F32)

        def step(n, carry):
            i = (ntile - 1 - n) if reverse else n
            r0 = pl.multiple_of(i * rows8, rows8)
            xr, xi = x_ref[pl.ds(r0, rows8), :ln], x_ref[pl.ds(r0, rows8), ln:]
            for k, (pr, pi) in zip((1, 2, 4), powers):
                shift, keep = (rows8 - k, row < rows8 - k) if reverse else (k, row >= k)
                sr = jnp.where(keep, pltpu.roll(xr, shift, 0), 0.0)
                si = jnp.where(keep, pltpu.roll(xi, shift, 0), 0.0)
                xr, xi = xr + pr * sr - pi * si, xi + pr * si + pi * sr
            cr, ci = carry[0], carry[1]
            yr = xr + ctr * cr - cti * ci
            yi = xi + ctr * ci + cti * cr
            o_ref[pl.ds(r0, rows8), :ln] = yr
            o_ref[pl.ds(r0, rows8), ln:] = yi
            edge = 0 if reverse else rows8 - 1
            new = (jnp.broadcast_to(yr[edge:edge + 1, :], (rows8, ln)), jnp.broadcast_to(yi[edge:edge + 1, :], (rows8, ln)))
            if not with_da:
                return new
            hr, hi = h_ref[pl.ds(r0, rows8), :ln], h_ref[pl.ds(r0, rows8), ln:]
            rp = pl.multiple_of(jnp.maximum(i - 1, 0) * rows8, rows8)
            last_r = jnp.where(i > 0, jnp.broadcast_to(h_ref[pl.ds(rp, rows8), :ln][rows8 - 1:, :], (rows8, ln)), 0.0)
            last_i = jnp.where(i > 0, jnp.broadcast_to(h_ref[pl.ds(rp, rows8), ln:][rows8 - 1:, :], (rows8, ln)), 0.0)
            hpr = jnp.where(row == 0, last_r, pltpu.roll(hr, 1, 0))
            hpi = jnp.where(row == 0, last_i, pltpu.roll(hi, 1, 0))
            return new + (carry[2] + yr * hpr + yi * hpi, carry[3] + yi * hpr - yr * hpi)

        init = (zero, zero, zero, zero) if with_da else (zero, zero)
        out = lax.fori_loop(0, ntile, step, init, unroll=2)
        if with_da:
            da_ref[:, :ln] = out[2]
            da_ref[:, ln:] = out[3]

    col = pl.BlockSpec((t, blk), lambda j: (0, j))
    tab = pl.BlockSpec((rows8, blk), lambda j: (0, j))
    in_specs = [col, tab, tab] + ([col] if with_da else [])
    out_specs = [col] + ([tab] if with_da else [])
    out_shape = [jax.ShapeDtypeStruct((t, width), F32)] + ([jax.ShapeDtypeStruct((rows8, width), F32)] if with_da else [])
    res = pl.pallas_call(
        body, name=name, grid=(width // blk,), in_specs=in_specs, out_specs=out_specs, out_shape=out_shape,
        compiler_params=_cparams("parallel"),
    )(x, carry_tab, step_tab, *([hist] if with_da else []))
    return res if with_da else res[0]


ATT_SCALE = (QK_NOPE + QK_ROPE) ** -0.5


def _att_mask(qi, kj, tb):
    qc = (qi * tb + lax.broadcasted_iota(jnp.int32, (tb, tb), 0)) // CHUNK
    kc = (kj * tb + lax.broadcasted_iota(jnp.int32, (tb, tb), 1)) // CHUNK
    return kc <= qc


def _with_side(side, n_main_in, n_main_out, refs, grid_first, grid_last, compute):
    if side is None:
        compute(refs)
        return
    n = side.n
    main = refs[:n_main_in] + refs[n_main_in + n:n_main_in + n + n_main_out] + refs[n_main_in + 2 * n + n_main_out + 3:]
    x_refs = refs[n_main_in:n_main_in + n]
    y_refs = refs[n_main_in + n + n_main_out:n_main_in + 2 * n + n_main_out]
    sems = refs[n_main_in + 2 * n + n_main_out:n_main_in + 2 * n + n_main_out + 3]

    @pl.when(grid_first)
    def _():
        for cp in side.copies(x_refs, y_refs, *sems):
            cp.start()

    compute(main)

    @pl.when(grid_last)
    def _():
        for cp in side.copies(x_refs, y_refs, *sems):
            cp.wait()


def _side_call(body, side, *, name, grid, in_specs, out_specs, out_shape, scratch, args, sem):
    n_out = len(out_shape)
    if side is not None:
        in_specs, args = in_specs + side.specs, list(args) + side.xs
        out_specs, out_shape = out_specs + side.specs, out_shape + side.out_shape
        scratch = side.scratch + scratch
        params = pltpu.CompilerParams(dimension_semantics=("arbitrary",) * len(grid), vmem_limit_bytes=VMEM_LIMIT,
                                      has_side_effects=True)
    else:
        params = _cparams(*sem)
    res = pl.pallas_call(body, name=name, grid=grid, in_specs=in_specs, out_specs=out_specs, out_shape=out_shape,
                         scratch_shapes=scratch, compiler_params=params)(*args)
    return res[:n_out], res[n_out:]


def _attn_fwd(q, k, v, *, name, side=None):
    h, t, dq = q.shape
    dv = v.shape[2]
    tb = min(ATT_BLOCK, t)
    nblk = t // tb
    nt = (((1,), (1,)), ((), ()))

    def compute(refs):
        q_ref, k_ref, v_ref, o_ref, lse_ref = refs
        i = pl.program_id(1)
        qb = q_ref[...]

        def kv_step(j, carry, masked):
            m, l, acc = carry
            r0 = pl.multiple_of(j * tb, tb)
            s = lax.dot_general(qb, k_ref[pl.ds(r0, tb), :], nt, preferred_element_type=F32) * ATT_SCALE
            if masked:
                s = jnp.where(_att_mask(i, j, tb), s, NEG_INF)
            m_new = jnp.maximum(m, jnp.max(s, axis=1, keepdims=True))
            alpha = jnp.exp(m - m_new)
            p = jnp.exp(s - m_new)
            l = alpha * l + jnp.sum(p, axis=1, keepdims=True)
            acc = alpha * acc + jnp.dot(p.astype(BF16), v_ref[pl.ds(r0, tb), :], preferred_element_type=F32)
            return m_new, l, acc

        init = (jnp.full((tb, 1), NEG_INF, F32), jnp.zeros((tb, 1), F32), jnp.zeros((tb, dv), F32))
        carry = lax.fori_loop(0, i, functools.partial(kv_step, masked=False), init)
        m, l, acc = kv_step(i, carry, True)
        o_ref[...] = acc / l
        lse_ref[...] = m + jnp.log(l)

    def body(*refs):
        first = (pl.program_id(0) == 0) & (pl.program_id(1) == 0)
        last = (pl.program_id(0) == h - 1) & (pl.program_id(1) == nblk - 1)
        _with_side(side, 3, 2, refs, first, last, compute)

    return _side_call(
        body, side, name=name, grid=(h, nblk),
        in_specs=[pl.BlockSpec((None, tb, dq), lambda hh, i: (hh, i, 0)),
                  pl.BlockSpec((None, t, dq), lambda hh, i: (hh, 0, 0)),
                  pl.BlockSpec((None, t, dv), lambda hh, i: (hh, 0, 0))],
        out_specs=[pl.BlockSpec((None, tb, dv), lambda hh, i: (hh, i, 0)),
                   pl.BlockSpec((None, tb, 1), lambda hh, i: (hh, i, 0))],
        out_shape=[jax.ShapeDtypeStruct((h, t, dv), F32), jax.ShapeDtypeStruct((h, t, 1), F32)],
        scratch=[], args=[q, k, v], sem=("parallel", "parallel"))


def _attn_bwd(q, k, v, o, lse, do, *, name, side=None):
    h, t, dq_w = q.shape
    dv_w = v.shape[2]
    tb = min(ATT_BLOCK, t)
    nblk = t // tb
    nt = (((1,), (1,)), ((), ()))
    tn = (((0,), (0,)), ((), ()))

    def compute(refs):
        q_ref, k_ref, v_ref, o_ref, lse_ref, do_ref, dq_ref, dk_ref, dv_ref, delta_ref = refs
        j = pl.program_id(1)

        @pl.when(j == 0)
        def _():
            dq_ref[...] = jnp.zeros_like(dq_ref)

            def dstep(i, c):
                r0 = pl.multiple_of(i * tb, tb)
                delta_ref[pl.ds(r0, tb), :] = jnp.sum(do_ref[pl.ds(r0, tb), :].astype(F32) * o_ref[pl.ds(r0, tb), :],
                                                      axis=1, keepdims=True)
                return c

            lax.fori_loop(0, nblk, dstep, 0)

        kb, vb = k_ref[...], v_ref[...]

        def q_step(i, carry, masked):
            dk, dv = carry
            r0 = pl.multiple_of(i * tb, tb)
            qb, dob = q_ref[pl.ds(r0, tb), :], do_ref[pl.ds(r0, tb), :]
            s = lax.dot_general(qb, kb, nt, preferred_element_type=F32) * ATT_SCALE
            if masked:
                s = jnp.where(_att_mask(i, j, tb), s, NEG_INF)
            p = jnp.exp(s - lse_ref[pl.ds(r0, tb), :])
            dv = dv + lax.dot_general(p.astype(BF16), dob, tn, preferred_element_type=F32)
            dp = lax.dot_general(dob, vb, nt, preferred_element_type=F32)
            ds = (p * (dp - delta_ref[pl.ds(r0, tb), :]) * ATT_SCALE).astype(BF16)
            dk = dk + lax.dot_general(ds, qb, tn, preferred_element_type=F32)
            dq_ref[pl.ds(r0, tb), :] += jnp.dot(ds, kb, preferred_element_type=F32)
            return dk, dv

        carry = q_step(j, (jnp.zeros((tb, dq_w), F32), jnp.zeros((tb, dv_w), F32)), True)
        dk, dv = lax.fori_loop(j + 1, nblk, functools.partial(q_step, masked=False), carry)
        dk_ref[...] = dk
        dv_ref[...] = dv

    def body(*refs):
        first = (pl.program_id(0) == 0) & (pl.program_id(1) == 0)
        last = (pl.program_id(0) == h - 1) & (pl.program_id(1) == nblk - 1)
        _with_side(side, 6, 3, refs, first, last, compute)

    whole = lambda w: pl.BlockSpec((None, t, w), lambda hh, j: (hh, 0, 0))
    blockj = lambda w: pl.BlockSpec((None, tb, w), lambda hh, j: (hh, j, 0))
    return _side_call(
        body, side, name=name, grid=(h, nblk),
        in_specs=[whole(dq_w), blockj(dq_w), blockj(dv_w), whole(dv_w), whole(1), whole(dv_w)],
        out_specs=[whole(dq_w), blockj(dq_w), blockj(dv_w)],
        out_shape=[jax.ShapeDtypeStruct((h, t, dq_w), F32), jax.ShapeDtypeStruct((h, t, dq_w), F32),
                   jax.ShapeDtypeStruct((h, t, dv_w), F32)],
        scratch=[pltpu.VMEM((t, 1), F32)], args=[q, k, v, o, lse, do], sem=("parallel", "arbitrary"))


class _Exchange:
    def __init__(self, xs, gather):
        self.xs, self.gather, self.n = list(xs), gather, len(xs)
        shapes = [tuple(x.shape) if gather else tuple(x.shape[1:]) for x in xs]
        self.out_shape = [jax.ShapeDtypeStruct((N_DEV,) + shp, x.dtype) for shp, x in zip(shapes, xs)]
        self.specs = [pl.BlockSpec(memory_space=pl.ANY)] * self.n
        self.scratch = [pltpu.SemaphoreType.DMA((self.n, N_DEV - 1)), pltpu.SemaphoreType.DMA((self.n, N_DEV - 1)),
                        pltpu.SemaphoreType.DMA((self.n,))]

    def copies(self, x_refs, y_refs, send_sems, recv_sems, local_sems):
        mx, my, mc = lax.axis_index("x"), lax.axis_index("y"), lax.axis_index("c")
        me = 4 * mx + 2 * my + mc
        out = [pltpu.make_async_copy(x_refs[i] if self.gather else x_refs[i].at[me], y_refs[i].at[me], local_sems.at[i])
               for i in range(self.n)]
        for k in range(1, N_DEV):
            px = 1 - mx if k & 4 else mx
            py = 1 - my if k & 2 else my
            pc = 1 - mc if k & 1 else mc
            for i in range(self.n):
                out.append(pltpu.make_async_remote_copy(
                    src_ref=x_refs[i] if self.gather else x_refs[i].at[4 * px + 2 * py + pc], dst_ref=y_refs[i].at[me],
                    send_sem=send_sems.at[i, k - 1], recv_sem=recv_sems.at[i, k - 1],
                    device_id=(px, py, pc), device_id_type=pl.DeviceIdType.MESH))
        return out


def _exchange(xs, *, gather, name):
    ex = _Exchange(xs, gather)
    n = ex.n

    def body(*refs):
        copies = ex.copies(refs[:n], refs[n:2 * n], *refs[2 * n:])
        for cp in copies:
            cp.start()
        for cp in copies:
            cp.wait()

    return pl.pallas_call(
        body, name=name, out_shape=ex.out_shape, in_specs=ex.specs, out_specs=ex.specs, scratch_shapes=ex.scratch,
        compiler_params=pltpu.CompilerParams(has_side_effects=True),
    )(*ex.xs)


def _adamw(w, g, m, v, *, name, tm=256):
    parts = g.ndim == 3
    r, c = w.shape
    tm = _pick_rows(r, tm)

    def body(w_ref, g_ref, m_ref, v_ref, go_ref, d_ref, mo_ref, vo_ref):
        if parts:
            gv = g_ref[0].astype(F32)
            for k in range(1, N_DEV):
                gv = gv + g_ref[k].astype(F32)
        else:
            gv = g_ref[...]
        mn = ADAM_B1 * m_ref[...] + (1.0 - ADAM_B1) * gv
        vn = ADAM_B2 * v_ref[...] + (1.0 - ADAM_B2) * jnp.square(gv)
        m_hat = mn / (1.0 - ADAM_B1 ** ADAM_STEP)
        v_hat = vn / (1.0 - ADAM_B2 ** ADAM_STEP)
        go_ref[...] = gv
        d_ref[...] = -ADAM_LR * (m_hat / (jnp.sqrt(v_hat) + ADAM_EPS) + ADAM_WD * w_ref[...])
        mo_ref[...] = mn
        vo_ref[...] = vn

    spec = pl.BlockSpec((tm, c), lambda i: (i, 0))
    gspec = pl.BlockSpec((N_DEV, tm, c), lambda i: (0, i, 0)) if parts else spec
    return pl.pallas_call(
        body, name=name, grid=(r // tm,), in_specs=[spec, gspec, spec, spec], out_specs=[spec] * 4,
        out_shape=[jax.ShapeDtypeStruct((r, c), F32)] * 4, compiler_params=_cparams("parallel"),
    )(w, g, m, v)


def _permute_in(a):
    pad = jnp.zeros(a.shape[:-1] + (IN_PAD - IN_WIDTH,), a.dtype)
    return jnp.concatenate([a[..., lo:hi] for lo, hi in _IN_SEGMENTS] + [pad], axis=-1)


def _unpermute_in(a):
    out, pos = {}, 0
    for lo, hi in _IN_SEGMENTS:
        out[lo] = a[..., pos:pos + hi - lo]
        pos += hi - lo
    return jnp.concatenate([out[lo] for lo in sorted(out)], axis=-1)


def _full_from_gathered(g, kind):
    if kind == 'row':
        return g.reshape((-1,) + g.shape[2:])
    if kind == 'col':
        return g.transpose(1, 0, 2).reshape(g.shape[1], -1)
    return g.transpose(1, 2, 0, 3).reshape(g.shape[1], g.shape[2], -1)


def _contrib_from_full(g, kind):
    if kind == 'row':
        return g.reshape((N_DEV, -1) + g.shape[1:])
    if kind == 'col':
        return g.reshape(g.shape[0], N_DEV, -1).transpose(1, 0, 2)
    return g.reshape(g.shape[0], g.shape[1], N_DEV, -1).transpose(2, 0, 1, 3)


def _as2d(a, lead=0):
    return a.reshape(a.shape[:lead] + (-1, a.shape[-1]))


def _chan_cols(re, im):
    lead = re.shape[:-1]
    nb = S5_CH // SCAN_LANES
    return jnp.stack([re.reshape(lead + (nb, SCAN_LANES)), im.reshape(lead + (nb, SCAN_LANES))],
                     axis=-2).reshape(lead + (2 * S5_CH,))


def _s5_tables(lam_re, lam_im, log_dt, b_re, b_im, c_re, c_im):
    dt = jnp.exp(log_dt)[:, None]
    mag = jnp.exp(lam_re * dt)
    a_re = mag * jnp.cos(lam_im * dt)
    a_im = mag * jnp.sin(lam_im * dt)
    den = lam_re * lam_re + lam_im * lam_im
    f_re = ((a_re - 1.0) * lam_re + a_im * lam_im) / den
    f_im = (a_im * lam_re - (a_re - 1.0) * lam_im) / den
    bb_re = f_re[..., None] * b_re - f_im[..., None] * b_im
    bb_im = f_re[..., None] * b_im + f_im[..., None] * b_re
    eye = jnp.eye(S5_GROUPS, dtype=F32)
    wb_re = jnp.einsum('gpc,gh->gchp', bb_re, eye).reshape(S5_WIDTH, S5_CH)
    wb_im = jnp.einsum('gpc,gh->gchp', bb_im, eye).reshape(S5_WIDTH, S5_CH)
    wb = _chan_cols(wb_re, wb_im)
    wc_re = jnp.einsum('gcp,gh->hcgp', c_re, eye).reshape(S5_WIDTH, S5_CH)
    wc_im = jnp.einsum('gcp,gh->hcgp', c_im, eye).reshape(S5_WIDTH, S5_CH)
    wc = _chan_cols(wc_re, -wc_im).T
    a_row = _chan_cols(a_re.reshape(1, S5_CH), a_im.reshape(1, S5_CH))
    return wb, wc, a_row


def _scan_tables(a_row, conj):
    nb = S5_CH // SCAN_LANES
    a = a_row.reshape(nb, 2, SCAN_LANES)
    ar, ai = a[:, 0], (-a[:, 1] if conj else a[:, 1])
    pr, pi = [ar], [ai]
    for _ in range(V7X_SUBLANES - 1):
        pr, pi = pr + [pr[-1] * ar - pi[-1] * ai], pi + [pr[-1] * ai + pi[-1] * ar]
    order = range(V7X_SUBLANES - 1, -1, -1) if conj else range(V7X_SUBLANES)
    carry = jnp.stack([jnp.stack([pr[r], pi[r]], axis=1).reshape(-1) for r in order])
    zero = jnp.zeros_like(carry[0])
    step = jnp.stack([jnp.stack([pr[r], pi[r]], axis=1).reshape(-1) for r in (0, 1, 3)] + [zero] * 5)
    return carry, step


def _rope_tables(t):
    half = QK_ROPE // 2
    inv_freq = 1.0 / (ROPE_THETA ** (jnp.arange(0, QK_ROPE, 2, dtype=F32) / QK_ROPE))
    ang = jnp.arange(t, dtype=F32)[:, None] * inv_freq[None, :]
    cos, sin = jnp.cos(ang), jnp.sin(ang)
    zero = jnp.zeros_like(sin)

    def lay(nope, width, first, second):
        head = jnp.concatenate([jnp.full((t, nope), 1.0 if first is cos else 0.0, F32), first, second], axis=1)
        reps = width // head.shape[1]
        out = jnp.tile(head, (1, reps))
        return jnp.pad(out, ((0, 0), (0, width - out.shape[1])))

    hq = MLA_HEADS * (QK_NOPE + QK_ROPE)
    q_tabs = (lay(QK_NOPE, hq, cos, cos), lay(QK_NOPE, hq, -sin, zero), lay(QK_NOPE, hq, zero, sin))
    k_tabs = (lay(0, V7X_LANES, cos, cos)[:, :V7X_LANES] * (jnp.arange(V7X_LANES) < QK_ROPE),
              lay(0, V7X_LANES, -sin, zero) * (jnp.arange(V7X_LANES) < QK_ROPE),
              lay(0, V7X_LANES, zero, sin) * (jnp.arange(V7X_LANES) < QK_ROPE))
    return q_tabs, k_tabs


def _sgu_tables(w_s, b_s):
    pos = jnp.arange(SGU_CHUNK) // CHUNK
    mask = pos[None, :] <= pos[:, None]
    wm = jnp.where(mask[None], w_s, 0.0).reshape(SGU_GROUPS * SGU_CHUNK, SGU_CHUNK)
    bias = jnp.repeat(b_s.T, SGU_WIDTH // SGU_GROUPS, axis=1)
    return wm, bias


def _row(v):
    return v.reshape(1, -1)


def _layer_fwd(x, ada, w, rope_tabs, tag, side=None):
    s = {'x': x}
    q_tabs, k_tabs = rope_tabs
    sc1, gt1, sc2, gt2 = _row(1.0 + ada[1]), _row(1.0 + ada[2]), _row(1.0 + ada[4]), _row(1.0 + ada[5])
    s.update(sc1=sc1, gt1=gt1, sc2=sc2, gt2=gt2)
    (h,) = _rowcall(_modulate_fn, [x], [sc1, _row(ada[0])], [(D_MODEL, BF16)], tm=512, name=f"mod1_{tag}")
    proj = _mm(h, w['w_in_p'], bias=w['b_in_p'], name=f"proj_{tag}", tn=768)
    s.update(h=h, proj=proj)

    (wb, wc, a_row), s['s5_pull'] = jax.vjp(_s5_tables, *[w[n] for n in ('s5_lambda_re', 's5_lambda_im', 's5_log_dt',
                                                                         's5_b_re', 's5_b_im', 's5_c_re', 's5_c_im')])
    wb, wc = wb.astype(BF16), wc.astype(BF16)
    u_s5 = proj[:, P_S5:P_S5 + S5_WIDTH]
    bu = _mm(u_s5, wb, name=f"s5_bu_{tag}")
    hs = _s5_scan(bu, *_scan_tables(a_row, False), reverse=False, name=f"s5_scan_{tag}")
    ylin = _mm(hs, wc, name=f"s5_c_{tag}")
    s5_full = [_row(w['s5_d']), w['s5_w_glu'], _row(w['s5_b_glu'])]
    (y_s5,) = _rowcall(_s5_post_fn, [ylin, u_s5], s5_full, [(S5_WIDTH, BF16)], tm=256, name=f"s5_post_{tag}")
    s.update(wb=wb, wc=wc, a_row=a_row, u_s5=u_s5, hs=hs, ylin=ylin, y_s5=y_s5)

    mla_rows = [(proj, MLA_BLK, P_MLA // MLA_BLK), *q_tabs, *k_tabs]
    mla_full = [_row(w['mla_q_norm']), w['mla_w_q_up'], _row(w['mla_kv_norm']), w['mla_w_kv_up']]
    hq, hkv = MLA_HEADS * (QK_NOPE + QK_ROPE), MLA_HEADS * (QK_NOPE + V_HEAD)
    q_r, kv, kpe_r = _rowcall(_mla_pre_fn, mla_rows, mla_full, [(hq, BF16), (hkv, BF16), (V7X_LANES, BF16)],
                              tm=256, name=f"mla_pre_{tag}")
    t = x.shape[0]
    qh = q_r.reshape(t, MLA_HEADS, -1).transpose(1, 0, 2)
    kv3 = kv.reshape(t, MLA_HEADS, -1).transpose(1, 0, 2)
    kh = jnp.concatenate([kv3[:, :, :QK_NOPE], jnp.broadcast_to(kpe_r[None, :, :QK_ROPE], (MLA_HEADS, t, QK_ROPE))], axis=2)
    vh = kv3[:, :, QK_NOPE:]
    (o, lse), side_out = _attn_fwd(qh, kh, vh, name=f"attn_fwd_{tag}", side=side)
    y_mla = o.transpose(1, 0, 2).reshape(t, -1).astype(BF16)
    s.update(qh=qh, kh=kh, vh=vh, o=o, lse=lse, y_mla=y_mla)

    (wm, bias), s['sgu_pull'] = jax.vjp(_sgu_tables, w['sgu_w_s'], w['sgu_b_s'])
    sgu_rows = [(proj, SGU_WIDTH, P_USGU // SGU_WIDTH), (proj, SGU_WIDTH, P_VSGU // SGU_WIDTH)]
    sgu_full = [_row(w['sgu_ln_g']), _row(w['sgu_ln_b']), wm, bias]
    (y_sgu,) = _rowcall(_sgu_fn, sgu_rows, sgu_full, [(SGU_WIDTH, BF16)], tm=SGU_CHUNK, name=f"sgu_{tag}")
    s.update(sgu_full=sgu_full, y_sgu=y_sgu)

    wbr = w['w_branch'].reshape(-1, D_MODEL)
    gate_rows = [(proj, D_MODEL, b) for b in range(3)]
    (merged,) = _rowcall(_merge_fn, [y_s5, y_mla, y_sgu] + gate_rows, [wbr], [(D_MODEL, BF16)], tm=256, name=f"merge_{tag}")
    ymix = _mm(merged, w['w_out'], name=f"wout_{tag}")
    (x1,) = _rowcall(_ln_res_fn, [x, ymix], [gt1, _row(w['ln1_g']), _row(w['ln1_b'])], [(D_MODEL, F32)], tm=256,
                     name=f"ln1_{tag}")
    s.update(merged=merged, ymix=ymix, x1=x1)

    (h2,) = _rowcall(_modulate_fn, [x1], [sc2, _row(ada[3])], [(D_MODEL, BF16)], tm=512, name=f"mod2_{tag}")
    ab = _mm(h2, w['ffn_w_in'], name=f"ffn_in_{tag}", tn=512)
    (act,) = _rowcall(_swiglu_fn, [(ab, FF_HIDDEN, 0), (ab, FF_HIDDEN, 1)], [], [(FF_HIDDEN, BF16)], tm=256,
                      name=f"swiglu_{tag}")
    f = _mm(act, w['ffn_w_out'], name=f"ffn_out_{tag}", tk=2816)
    (x2,) = _rowcall(_ln_res_fn, [x1, f], [gt2, _row(w['ln2_g']), _row(w['ln2_b'])], [(D_MODEL, F32)], tm=256,
                     name=f"ln2_{tag}")
    s.update(h2=h2, ab=ab, act=act, f=f)
    return x2, s, side_out


def _mod_bwd_fn(x, dh, dxa, scale_row):
    return (dxa + dh * scale_row, jnp.sum(dh * x, axis=0, keepdims=True), jnp.sum(dh, axis=0, keepdims=True))


def _layer_bwd(dx2, s, w, rope_tabs, tag, side=None):
    g = {}
    q_tabs, k_tabs = rope_tabs
    t = dx2.shape[0]
    ln_full = lambda gt, a, b: [gt, _row(w[a]), _row(w[b])]

    dx1_a, df, dgt2, g['ln2_g'], g['ln2_b'] = _rowcall_vjp(
        _ln_res_fn, [s['x1'], s['f']], ln_full(s['gt2'], 'ln2_g', 'ln2_b'), [dx2], [0, 1], [0, 1, 2],
        tm=256, name=f"ln2_bwd_{tag}", row_dtypes=[F32, BF16])
    dact = _mm(df, w['ffn_w_out'], tb=True, name=f"ffn_out_dx_{tag}")
    g['ffn_w_out'] = _mm(s['act'], df, ta=True, name=f"ffn_out_dw_{tag}")
    da_, db_ = _rowcall_vjp(_swiglu_fn, [(s['ab'], FF_HIDDEN, 0), (s['ab'], FF_HIDDEN, 1)], [], [dact], [0, 1], [],
                            tm=256, name=f"swiglu_bwd_{tag}", row_dtypes=[BF16, BF16])
    dab = jnp.concatenate([da_, db_], axis=1)
    dh2 = _mm(dab, w['ffn_w_in'], tb=True, name=f"ffn_in_dx_{tag}", tk=512)
    g['ffn_w_in'] = _mm(s['h2'], dab, ta=True, name=f"ffn_in_dw_{tag}", tn=512)
    dx1, dsc2, dsh2 = _rowcall(_mod_bwd_fn, [s['x1'], dh2, dx1_a], [s['sc2']], [(D_MODEL, F32)],
                               [((1, D_MODEL), F32)] * 2, tm=256, name=f"mod2_bwd_{tag}")

    dx_a, dymix, dgt1, g['ln1_g'], g['ln1_b'] = _rowcall_vjp(
        _ln_res_fn, [s['x'], s['ymix']], ln_full(s['gt1'], 'ln1_g', 'ln1_b'), [dx1], [0, 1], [0, 1, 2],
        tm=256, name=f"ln1_bwd_{tag}", row_dtypes=[F32, BF16])
    dmerged = _mm(dymix, w['w_out'], tb=True, name=f"wout_dx_{tag}")
    g['w_out'] = _mm(s['merged'], dymix, ta=True, name=f"wout_dw_{tag}")

    proj = s['proj']
    wbr = w['w_branch'].reshape(-1, D_MODEL)
    gate_rows = [(proj, D_MODEL, b) for b in range(3)]
    dy_s5, dy_mla, dy_sgu, dl0, dl1, dl2, dwbr = _rowcall_vjp(
        _merge_fn, [s['y_s5'], s['y_mla'], s['y_sgu']] + gate_rows, [wbr], [dmerged], [0, 1, 2, 3, 4, 5], [0],
        tm=256, name=f"merge_bwd_{tag}")
    g['w_branch'] = dwbr.reshape(w['w_branch'].shape)

    sgu_rows = [(proj, SGU_WIDTH, P_USGU // SGU_WIDTH), (proj, SGU_WIDTH, P_VSGU // SGU_WIDTH)]
    du_sgu, dv_sgu, dlg, dlb, dwm, dbias = _rowcall_vjp(
        _sgu_fn, sgu_rows, s['sgu_full'], [dy_sgu], [0, 1], [0, 1, 2, 3], tm=SGU_CHUNK, name=f"sgu_bwd_{tag}")
    g['sgu_ln_g'], g['sgu_ln_b'] = dlg.reshape(-1), dlb.reshape(-1)
    g['sgu_w_s'], g['sgu_b_s'] = s['sgu_pull']((dwm, dbias))

    do = dy_mla.reshape(t, MLA_HEADS, V_HEAD).transpose(1, 0, 2).astype(BF16)
    (dqh, dkh, dvh), side_out = _attn_bwd(s['qh'], s['kh'], s['vh'], s['o'], s['lse'], do, name=f"attn_bwd_{tag}",
                                          side=side)
    dq_r = dqh.transpose(1, 0, 2).reshape(t, -1)
    dkv = jnp.concatenate([dkh[:, :, :QK_NOPE], dvh], axis=2).transpose(1, 0, 2).reshape(t, -1)
    dkpe = jnp.pad(jnp.sum(dkh[:, :, QK_NOPE:], axis=0), ((0, 0), (0, V7X_LANES - QK_ROPE)))
    mla_rows = [(proj, MLA_BLK, P_MLA // MLA_BLK), *q_tabs, *k_tabs]
    mla_full = [_row(w['mla_q_norm']), w['mla_w_q_up'], _row(w['mla_kv_norm']), w['mla_w_kv_up']]
    dmla, dqn, g['mla_w_q_up'], dkvn, g['mla_w_kv_up'] = _rowcall_vjp(
        _mla_pre_fn, mla_rows, mla_full, [dq_r, dkv, dkpe], [0], [0, 1, 2, 3], tm=256, name=f"mla_pre_bwd_{tag}")
    g['mla_q_norm'], g['mla_kv_norm'] = dqn.reshape(-1), dkvn.reshape(-1)

    s5_full = [_row(w['s5_d']), w['s5_w_glu'], _row(w['s5_b_glu'])]
    dylin, du_a, dd, g['s5_w_glu'], dbg = _rowcall_vjp(
        _s5_post_fn, [s['ylin'], s['u_s5']], s5_full, [dy_s5], [0, 1], [0, 1, 2], tm=256, name=f"s5_post_bwd_{tag}",
        row_dtypes=[BF16, F32])
    g['s5_d'], g['s5_b_glu'] = dd.reshape(-1), dbg.reshape(-1)
    dhs = _mm(dylin, s['wc'], tb=True, name=f"s5_c_dx_{tag}")
    dwc = _mm(s['hs'], dylin, ta=True, name=f"s5_c_dw_{tag}")
    gs, da_part = _s5_scan(dhs, *_scan_tables(s['a_row'], True), reverse=True, hist=s['hs'], name=f"s5_scan_bwd_{tag}")
    du_b = _mm(gs, s['wb'], tb=True, name=f"s5_bu_dx_{tag}")
    dwb = _mm(s['u_s5'], gs, ta=True, name=f"s5_bu_dw_{tag}")
    da_row = jnp.sum(da_part, axis=0, keepdims=True)
    for n, v in zip(('s5_lambda_re', 's5_lambda_im', 's5_log_dt', 's5_b_re', 's5_b_im', 's5_c_re', 's5_c_im'),
                    s['s5_pull']((dwb, dwc, da_row))):
        g[n] = v

    dproj_f32 = jnp.concatenate([dl0, dl1, dl2, du_a + du_b, du_sgu, dv_sgu, dmla], axis=1)
    dproj, db_in = _rowcall(lambda d: (d, jnp.sum(d, axis=0, keepdims=True)), [dproj_f32], [], [(IN_PAD, BF16)],
                            [((1, IN_PAD), F32)], tm=256, name=f"dproj_{tag}")
    dh = _mm(dproj, w['w_in_p'], tb=True, name=f"proj_dx_{tag}", tk=768)
    g['w_in'] = _unpermute_in(_mm(s['h'], dproj, ta=True, name=f"proj_dw_{tag}", tn=768))
    g['b_in'] = _unpermute_in(db_in).reshape(-1)
    dx, dsc1, dsh1 = _rowcall(_mod_bwd_fn, [s['x'], dh, dx_a], [s['sc1']], [(D_MODEL, F32)], [((1, D_MODEL), F32)] * 2,
                              tm=256, name=f"mod1_bwd_{tag}")
    d_ada = jnp.concatenate([dsh1, dsc1, dgt1, dsh2, dsc2, dgt2], axis=0)
    return dx, d_ada, g, side_out


def _loss_fn(y, target):
    err = y - target
    return (err / D_MODEL, 0.5 * jnp.sum(jnp.sum(err * err, axis=1, keepdims=True), axis=0, keepdims=True) / D_MODEL)


def _step(p):
    me = 4 * lax.axis_index("x") + 2 * lax.axis_index("y") + lax.axis_index("c")
    x = p['x'][0]
    t = x.shape[0]
    rope_tabs = _rope_tables(t)

    (c_all,) = _exchange([jnp.broadcast_to(p['c'], (V7X_SUBLANES, D_MODEL))], gather=True, name="gather_c")
    c_all = c_all[:, 0, :]
    (c_act,) = _rowcall(lambda cc: (cc * _sigmoid(cc),), [c_all], [], [(D_MODEL, F32)], tm=N_DEV, name="c_silu")
    ncol = p['w_ada'].shape[2]
    b_ada_loc = lax.dynamic_slice_in_dim(p['b_ada'], me * ncol, ncol, axis=1)
    ada_cols = jnp.concatenate([_mm(c_act, p['w_ada'][l], bias=b_ada_loc[l:l + 1], name=f"ada_{l}") for l in range(DEPTH)])
    (ada_all,) = _exchange([ada_cols], gather=True, name="gather_ada")
    ada_all = ada_all.reshape(N_DEV, DEPTH, N_DEV, ncol)
    ada = lax.dynamic_index_in_dim(ada_all, me, axis=2, keepdims=False)
    ada = ada.transpose(1, 0, 2).reshape(DEPTH, 6, D_MODEL)

    def shards(l):
        return [p[n][l].astype(BF16) for n, _ in SHARDED]

    def full_weights(l, gathered):
        w = {n: _full_from_gathered(g, kind) for (n, kind), g in zip(SHARDED, gathered)}
        w['w_in_p'] = _permute_in(w.pop('w_in'))
        w['b_in_p'] = _row(_permute_in(p['b_in'][l]))
        for n in SMALL:
            if n != 'b_ada' and n != 'b_in':
                w[n] = p[n][l]
        return w

    saved, layers = [], []
    gathered = _exchange(shards(0), gather=True, name="gather_w_0")
    for l in range(DEPTH):
        layers.append(full_weights(l, gathered))
        side = _Exchange(shards(l + 1), True) if l + 1 < DEPTH else None
        x, s, gathered = _layer_fwd(x, ada[l], layers[l], rope_tabs, f"l{l}", side)
        saved.append(s)
    dy, loss_loc = _rowcall(_loss_fn, [x, p['loss_target'][0]], [], [(D_MODEL, F32)], [((1, 1), F32)], tm=256, name="loss")
    loss = lax.psum(loss_loc[0, 0], ("x", "y", "c"))

    d_ada, grads, landed = [None] * DEPTH, [None] * DEPTH, [None] * DEPTH
    dx, pending = dy, None
    for l in reversed(range(DEPTH)):
        side = _Exchange(pending, False) if pending is not None else None
        dx, d_ada[l], grads[l], res = _layer_bwd(dx, saved[l], layers[l], rope_tabs, f"l{l}", side)
        if side is not None:
            landed[l + 1] = res
        pending = [_contrib_from_full(grads[l][n], kind).astype(BF16) for n, kind in SHARDED]
    landed[0] = _exchange(pending, gather=False, name="scatter_g_0")
    d_ada = jnp.stack(d_ada).reshape(DEPTH, 6 * D_MODEL)

    (d_ada_all,) = _exchange([d_ada], gather=True, name="gather_dada")
    d_ada_cols = lax.dynamic_slice_in_dim(d_ada_all, me * ncol, ncol, axis=2)
    pad_b = ((0, V7X_LANES - N_DEV), (0, 0))
    c_act_p = jnp.pad(c_act, pad_b)
    g_w_ada = jnp.stack([_mm(c_act_p, jnp.pad(d_ada_cols[:, l], pad_b), ta=True, name=f"ada_dw_{l}") for l in range(DEPTH)])

    out = {}
    for l in range(DEPTH):
        for (n, _), g_l in zip(SHARDED, landed[l]):
            res = _adamw(_as2d(p[n][l]), _as2d(g_l, lead=1), _as2d(p['m_' + n][l]), _as2d(p['v_' + n][l]),
                         name=f"adamw_{n}_{l}")
            for kind, r in zip(('grad_', 'delta_', 'new_m_', 'new_v_'), res):
                out.setdefault(kind + n, []).append(r.reshape(p[n].shape[1:]))
    out = {k: jnp.stack(v) for k, v in out.items()}

    r_ada = DEPTH * D_MODEL
    res = _adamw(p['w_ada'].reshape(r_ada, ncol), g_w_ada.reshape(r_ada, ncol), p['m_w_ada'].reshape(r_ada, ncol),
                 p['v_w_ada'].reshape(r_ada, ncol), name="adamw_ada", tm=256)
    for kind, flat in zip(('grad_', 'delta_', 'new_m_', 'new_v_'), res):
        out[kind + 'w_ada'] = flat.reshape(p['w_ada'].shape)

    small_g = {n: jnp.stack([grads[l][n] for l in range(DEPTH)]) for n in SMALL if n != 'b_ada'}
    small_g['b_ada'] = d_ada
    n_small = sum(int(np.prod(p[n].shape)) for n in SMALL)
    n_pad = -(-n_small // SMALL_PAD) * SMALL_PAD
    chunk_rows = n_pad // N_DEV // 1024

    def flat_small(get):
        flat = jnp.concatenate([get(n).reshape(-1) for n in SMALL])
        return jnp.pad(flat, (0, n_pad - n_small)).reshape(N_DEV, chunk_rows, 1024)

    (landed_small,) = _exchange([flat_small(lambda n: small_g[n])], gather=False, name="scatter_small")
    mine = lambda pre: lax.dynamic_index_in_dim(flat_small(lambda n: p[pre + n]), me, axis=0, keepdims=False)
    res = _adamw(mine(''), landed_small, mine('m_'), mine('v_'), name="adamw_small", tm=56)
    (res_all,) = _exchange([jnp.concatenate(res, axis=0)], gather=True, name="gather_small")
    res_all = res_all.reshape(N_DEV, 4, chunk_rows * 1024).transpose(1, 0, 2).reshape(4, n_pad)
    off = 0
    for n in SMALL:
        size = int(np.prod(p[n].shape))
        for k, kind in enumerate(('grad_', 'delta_', 'new_m_', 'new_v_')):
            out[kind + n] = res_all[k, off:off + size].reshape(p[n].shape)
        off += size

    outs = [loss, dx[None]]
    for kind in ('grad_', 'delta_', 'new_m_', 'new_v_'):
        outs += [out[kind + n] for n in WNAMES]
    return tuple(outs)


def kernel(x, c, w_ada, b_ada, w_in, b_in, s5_lambda_re, s5_lambda_im, s5_log_dt, s5_b_re, s5_b_im, s5_c_re, s5_c_im, s5_d, s5_w_glu, s5_b_glu, mla_q_norm, mla_w_q_up, mla_kv_norm, mla_w_kv_up, sgu_ln_g, sgu_ln_b, sgu_w_s, sgu_b_s, w_branch, w_out, ln1_g, ln1_b, ffn_w_in, ffn_w_out, ln2_g, ln2_b, loss_target, m_w_ada, m_b_ada, m_w_in, m_b_in, m_s5_lambda_re, m_s5_lambda_im, m_s5_log_dt, m_s5_b_re, m_s5_b_im, m_s5_c_re, m_s5_c_im, m_s5_d, m_s5_w_glu, m_s5_b_glu, m_mla_q_norm, m_mla_w_q_up, m_mla_kv_norm, m_mla_w_kv_up, m_sgu_ln_g, m_sgu_ln_b, m_sgu_w_s, m_sgu_b_s, m_w_branch, m_w_out, m_ln1_g, m_ln1_b, m_ffn_w_in, m_ffn_w_out, m_ln2_g, m_ln2_b, v_w_ada, v_b_ada, v_w_in, v_b_in, v_s5_lambda_re, v_s5_lambda_im, v_s5_log_dt, v_s5_b_re, v_s5_b_im, v_s5_c_re, v_s5_c_im, v_s5_d, v_s5_w_glu, v_s5_b_glu, v_mla_q_norm, v_mla_w_q_up, v_mla_kv_norm, v_mla_w_kv_up, v_sgu_ln_g, v_sgu_ln_b, v_sgu_w_s, v_sgu_b_s, v_w_branch, v_w_out, v_ln1_g, v_ln1_b, v_ffn_w_in, v_ffn_w_out, v_ln2_g, v_ln2_b):
    return _step(dict(locals()))
```

```python
import functools
import math

import numpy as np
import jax
import jax.numpy as jnp
from jax import lax
from jax.experimental import pallas as pl
from jax.experimental.pallas import tpu as pltpu

F32 = jnp.float32
BF16 = jnp.bfloat16

N_DEV = 8
D_MODEL = 1024
DEPTH = 4
CHUNK = 64
S5_WIDTH = 512
S5_GROUP = 16
S5_GROUPS = 32
S5_STATE = 64
MLA_HEADS = 8
QK_NOPE = 64
QK_ROPE = 32
V_HEAD = 64
Q_LORA = 384
KV_LORA = 256
ROPE_THETA = 10000.0
SGU_WIDTH = 512
SGU_GROUPS = 4
SGU_CHUNK = 128
FF_HIDDEN = 2816
DEEPNORM_ALPHA = (2 * DEPTH) ** 0.25
LN_EPS = 1e-5
RMS_EPS = 1e-6
NEG_INF = -1e30
ADAM_LR = 0.001
ADAM_B1 = 0.9
ADAM_B2 = 0.999
ADAM_EPS = 1e-08
ADAM_WD = 0.01
ADAM_STEP = 10

IN_WIDTH = 5280
IN_PAD = 5376
_O_S5, _O_CQ, _O_CKV, _O_KPE, _O_USGU, _O_VSGU, _O_GATE = 0, 512, 896, 1152, 1184, 1696, 2208
_IN_SEGMENTS = ((_O_GATE, IN_WIDTH), (_O_S5, _O_CQ), (_O_USGU, _O_VSGU), (_O_VSGU, _O_GATE), (_O_CQ, _O_USGU))
P_GATE, P_S5, P_USGU, P_VSGU, P_MLA = 0, 3072, 3584, 4096, 4608
MLA_BLK = 768

V7X_LANES = 128
V7X_SUBLANES = 8
VMEM_LIMIT = 48 * 1024 * 1024
ATT_BLOCK = 512
SCAN_LANES = 128
S5_CH = S5_GROUPS * S5_STATE
S5_BD = 4

WNAMES = ['w_ada', 'b_ada', 'w_in', 'b_in', 's5_lambda_re', 's5_lambda_im', 's5_log_dt', 's5_b_re', 's5_b_im',
          's5_c_re', 's5_c_im', 's5_d', 's5_w_glu', 's5_b_glu', 'mla_q_norm', 'mla_w_q_up', 'mla_kv_norm',
          'mla_w_kv_up', 'sgu_ln_g', 'sgu_ln_b', 'sgu_w_s', 'sgu_b_s', 'w_branch', 'w_out', 'ln1_g', 'ln1_b',
          'ffn_w_in', 'ffn_w_out', 'ln2_g', 'ln2_b']
SHARDED = (('w_in', 'col'), ('s5_w_glu', 'row'), ('mla_w_q_up', 'col'), ('mla_w_kv_up', 'col'),
           ('w_branch', 'col3'), ('w_out', 'row'), ('ffn_w_in', 'col'), ('ffn_w_out', 'row'))
SMALL = [n for n in WNAMES if n != 'w_ada' and n not in dict(SHARDED)]
SMALL_PAD = N_DEV * V7X_SUBLANES * 1024


def _cparams(*sem):
    return pltpu.CompilerParams(dimension_semantics=sem, vmem_limit_bytes=VMEM_LIMIT)


def _pick(n, target):
    if n <= target:
        return n
    best = None
    for d in range(V7X_LANES, target + 1, V7X_LANES):
        if n % d == 0:
            best = d
    assert best is not None, (n, target)
    return best


def _pick_rows(n, target):
    if n <= target:
        return n
    best = None
    for d in range(V7X_SUBLANES, target + 1, V7X_SUBLANES):
        if n % d == 0:
            best = d
    assert best is not None, (n, target)
    return best


@jax.custom_vjp
def _bdot(a, b):
    return jnp.dot(a.astype(BF16), b.astype(BF16), preferred_element_type=F32)


def _bdot_fwd(a, b):
    return _bdot(a, b), (a, b)


def _bdot_bwd(res, g):
    a, b = res
    gb = g.astype(BF16)
    da = lax.dot_general(gb, b.astype(BF16), (((1,), (1,)), ((), ())), preferred_element_type=F32)
    db = lax.dot_general(a.astype(BF16), gb, (((0,), (0,)), ((), ())), preferred_element_type=F32)
    return da.astype(a.dtype), db.astype(b.dtype)


_bdot.defvjp(_bdot_fwd, _bdot_bwd)


@functools.partial(jax.custom_vjp, nondiff_argnums=(1,))
def _lane_roll(x, shift):
    return pltpu.roll(x, shift % x.shape[1], 1)


def _lane_roll_fwd(x, shift):
    return _lane_roll(x, shift), None


def _lane_roll_bwd(shift, _, g):
    return (_lane_roll(g, -shift),)


_lane_roll.defvjp(_lane_roll_fwd, _lane_roll_bwd)


def _sigmoid(x):
    return 1.0 / (1.0 + jnp.exp(-x))


def _gelu(x):
    return 0.5 * x * (1.0 + jnp.tanh(math.sqrt(2.0 / math.pi) * (x + 0.044715 * (x * x * x))))


def _layer_norm(x, g, b):
    mu = jnp.mean(x, axis=-1, keepdims=True)
    var = jnp.mean(jnp.square(x - mu), axis=-1, keepdims=True)
    return (x - mu) * lax.rsqrt(var + LN_EPS) * g + b


def _rms_norm(x, g):
    return x * lax.rsqrt(jnp.mean(x * x, axis=-1, keepdims=True) + RMS_EPS) * g


def _rope(x, c, s1, s2):
    half = QK_ROPE // 2
    return x * c + _lane_roll(x, -half) * s1 + _lane_roll(x, half) * s2


def _modulate_fn(x, scale_row, shift_row):
    return (x * scale_row + shift_row,)


def _ln_res_fn(x, y, gate_row, g, b):
    return (_layer_norm(DEEPNORM_ALPHA * x + gate_row * y, g, b),)


def _s5_post_fn(ylin, u, d, w_glu, b_glu):
    z = _gelu(ylin + d * u)
    return (z * _sigmoid(_bdot(z, w_glu) + b_glu),)


def _mla_pre_fn(blk, cq_t, sq1, sq2, ck_t, sk1, sk2, q_norm, w_q, kv_norm, w_kv):
    cq, ckv, kpe = blk[:, :Q_LORA], blk[:, Q_LORA:Q_LORA + KV_LORA], blk[:, Q_LORA + KV_LORA:]
    q = _rope(_bdot(_rms_norm(cq, q_norm), w_q), cq_t, sq1, sq2)
    kv = _bdot(_rms_norm(ckv, kv_norm), w_kv)
    return q, kv, _rope(kpe, ck_t, sk1, sk2)


def _sgu_fn(u, v, g, b, wm, bias):
    vn = _layer_norm(_gelu(v), g, b)
    w = SGU_CHUNK
    parts = [_bdot(wm[k * w:(k + 1) * w, :], vn[:, k * w:(k + 1) * w]) for k in range(SGU_GROUPS)]
    return (_gelu(u) * (jnp.concatenate(parts, axis=1) + bias),)


def _merge_fn(y0, y1, y2, l0, l1, l2, wb):
    n = S5_WIDTH
    return (_sigmoid(l0) * _bdot(y0, wb[:n]) + _sigmoid(l1) * _bdot(y1, wb[n:2 * n])
            + _sigmoid(l2) * _bdot(y2, wb[2 * n:]),)


def _swiglu_fn(a, b):
    return (a * _sigmoid(a) * b,)


def _rowcall(fn, rows, fulls, out_rows, out_reds=(), *, tm, name):
    rows = [r if isinstance(r, tuple) else (r, r.shape[1], 0) for r in rows]
    t = rows[0][0].shape[0]
    tm = _pick_rows(t, tm)
    n_in, n_or, n_red = len(rows) + len(fulls), len(out_rows), len(out_reds)

    def body(*refs):
        vals = fn(*[r[...] for r in refs[:n_in]])
        assert len(vals) == n_or + n_red, (name, len(vals))
        for ref, v in zip(refs[n_in:n_in + n_or], vals[:n_or]):
            ref[...] = v.astype(ref.dtype)
        if n_red:
            red_refs = refs[n_in + n_or:]

            @pl.when(pl.program_id(0) == 0)
            def _():
                for ref in red_refs:
                    ref[...] = jnp.zeros_like(ref)

            for ref, v in zip(red_refs, vals[n_or:]):
                ref[...] += v.astype(ref.dtype)

    in_specs = [pl.BlockSpec((tm, w), functools.partial(lambda i, blk: (i, blk), blk=blk)) for _, w, blk in rows]
    in_specs += [pl.BlockSpec(f.shape, lambda i: (0, 0)) for f in fulls]
    out_specs = [pl.BlockSpec((tm, c), lambda i: (i, 0)) for c, _ in out_rows]
    out_specs += [pl.BlockSpec(s, lambda i: (0, 0)) for s, _ in out_reds]
    out_shape = [jax.ShapeDtypeStruct((t, c), dt) for c, dt in out_rows]
    out_shape += [jax.ShapeDtypeStruct(s, dt) for s, dt in out_reds]
    return pl.pallas_call(
        body, name=name, grid=(t // tm,), in_specs=in_specs, out_specs=out_specs, out_shape=out_shape,
        compiler_params=_cparams("arbitrary" if n_red else "parallel"),
    )(*[r[0] for r in rows], *fulls)


def _rowcall_vjp(fn, rows, fulls, cots, diff_rows, diff_fulls, *, tm, name, row_dtypes=None):
    rows_n = [r if isinstance(r, tuple) else (r, r.shape[1], 0) for r in rows]
    n_r, n_c = len(rows), len(cots)
    row_dtypes = row_dtypes or [F32] * len(diff_rows)

    def fn2(*vals):
        r = [v.astype(F32) for v in vals[:n_r]]
        ct = vals[n_r:n_r + n_c]
        f = [v.astype(F32) for v in vals[n_r + n_c:]]

        def g(*dargs):
            rr, ff = list(r), list(f)
            for k, idx in enumerate(diff_rows):
                rr[idx] = dargs[k]
            for k, idx in enumerate(diff_fulls):
                ff[idx] = dargs[len(diff_rows) + k]
            return fn(*rr, *ff)

        prim = [r[i] for i in diff_rows] + [f[i] for i in diff_fulls]
        outs, pull = jax.vjp(g, *prim)
        return pull(tuple(c.astype(o.dtype) for c, o in zip(ct, outs)))

    out_rows = [(rows_n[i][1], dt) for i, dt in zip(diff_rows, row_dtypes)]
    out_reds = [(fulls[i].shape, F32) for i in diff_fulls]
    return _rowcall(fn2, list(rows) + list(cots), fulls, out_rows, out_reds, tm=tm, name=name)


def _mm(a, b, *, ta=False, tb=False, bias=None, out_dtype=F32, name, tm=1024, tn=1024, tk=1024):
    (k_a, m) = a.shape if ta else a.shape[::-1]
    (n, k_b) = b.shape if tb else b.shape[::-1]
    assert k_a == k_b, (name, a.shape, b.shape)
    tm, tn, tk = _pick(m, tm) if m % V7X_LANES == 0 else m, _pick(n, tn), _pick(k_a, tk) if k_a % V7X_LANES == 0 else k_a
    nk = k_a // tk
    a_spec = pl.BlockSpec((tk, tm), lambda i, j, k: (k, i)) if ta else pl.BlockSpec((tm, tk), lambda i, j, k: (i, k))
    b_spec = pl.BlockSpec((tn, tk), lambda i, j, k: (j, k)) if tb else pl.BlockSpec((tk, tn), lambda i, j, k: (k, j))
    dims = (((0,) if ta else (1,), (1,) if tb else (0,)), ((), ()))
    has_bias = bias is not None

    def body(*refs):
        a_ref, b_ref = refs[0], refs[1]
        part = lax.dot_general(a_ref[...].astype(BF16), b_ref[...].astype(BF16), dims, preferred_element_type=F32)
        if nk == 1:
            o_ref = refs[-1]
            o_ref[...] = (part + refs[2][...] if has_bias else part).astype(o_ref.dtype)
            return
        o_ref, acc_ref = refs[-2], refs[-1]
        k = pl.program_id(2)

        @pl.when(k == 0)
        def _():
            acc_ref[...] = part

        @pl.when(k > 0)
        def _():
            acc_ref[...] += part

        @pl.when(k == nk - 1)
        def _():
            r = acc_ref[...]
            if has_bias:
                r = r + refs[2][...]
            o_ref[...] = r.astype(o_ref.dtype)

    in_specs = [a_spec, b_spec] + ([pl.BlockSpec((1, tn), lambda i, j, k: (0, j))] if has_bias else [])
    return pl.pallas_call(
        body, name=name, grid=(m // tm, n // tn, nk), in_specs=in_specs,
        out_specs=pl.BlockSpec((tm, tn), lambda i, j, k: (i, j)),
        out_shape=jax.ShapeDtypeStruct((m, n), out_dtype),
        scratch_shapes=[pltpu.VMEM((tm, tn), F32)] if nk > 1 else [],
        compiler_params=_cparams("parallel", "parallel", "arbitrary"),
    )(a, b, *([bias] if has_bias else []))


def _mm_bd(a, b, mode, *, name, tm=512, tk=1024):
    if mode == 'tn':
        t = a.shape[0]
        nb = S5_BD
        ka, kb = a.shape[1] // nb, b.shape[1] // nb
        tk = _pick(t, tk)
        nsteps = t // tk

        def body_tn(a_ref, b_ref, o_ref):
            part = lax.dot_general(a_ref[...].astype(BF16), b_ref[...].astype(BF16), (((0,), (0,)), ((), ())),
                                   preferred_element_type=F32)

            @pl.when(pl.program_id(1) == 0)
            def _():
                o_ref[...] = part

            @pl.when(pl.program_id(1) > 0)
            def _():
                o_ref[...] += part

        return pl.pallas_call(
            body_tn, name=name, grid=(nb, nsteps),
            in_specs=[pl.BlockSpec((tk, ka), lambda k, s: (s, k)), pl.BlockSpec((tk, kb), lambda k, s: (s, k))],
            out_specs=pl.BlockSpec((None, ka, kb), lambda k, s: (k, 0, 0)),
            out_shape=jax.ShapeDtypeStruct((nb, ka, kb), F32), compiler_params=_cparams("parallel", "arbitrary"),
        )(a, b)
    nb, ka, kb = b.shape
    m = a.shape[0]
    tm = _pick(m, tm)
    w_in, w_out = (ka, kb) if mode == 'nn' else (kb, ka)
    dims = (((1,), (0,)), ((), ())) if mode == 'nn' else (((1,), (1,)), ((), ()))

    def body(a_ref, b_ref, o_ref):
        o_ref[...] = lax.dot_general(a_ref[...].astype(BF16), b_ref[...].astype(BF16), dims, preferred_element_type=F32)

    return pl.pallas_call(
        body, name=name, grid=(m // tm, nb),
        in_specs=[pl.BlockSpec((tm, w_in), lambda i, k: (i, k)), pl.BlockSpec((None, ka, kb), lambda i, k: (k, 0, 0))],
        out_specs=pl.BlockSpec((tm, w_out), lambda i, k: (i, k)),
        out_shape=jax.ShapeDtypeStruct((m, nb * w_out), F32), compiler_params=_cparams("parallel", "parallel"),
    )(a, b)


def _s5_scan(x, carry_tab, step_tab, *, reverse, hist=None, name):
    t, width = x.shape
    blk = 2 * SCAN_LANES
    ntile = t // V7X_SUBLANES
    with_da = hist is not None
    ln = SCAN_LANES
    rows8 = V7X_SUBLANES

    def body(*refs):
        if with_da:
            x_ref, ct_ref, st_ref, h_ref, o_ref, da_ref = refs
        else:
            x_ref, ct_ref, st_ref, o_ref = refs
        row = lax.broadcasted_iota(jnp.int32, (rows8, ln), 0)
        ctr, cti = ct_ref[:, :ln], ct_ref[:, ln:]
        powers = [(jnp.broadcast_to(st_ref[k:k + 1, :ln], (rows8, ln)),
                   jnp.broadcast_to(st_ref[k:k + 1, ln:], (rows8, ln))) for k in range(3)]
        zero = jnp.zeros((rows8, ln), F32)

        def step(n, carry):
            i = (ntile - 1 - n) if reverse else n
            r0 = pl.multiple_of(i * rows8, rows8)
            xr, xi = x_ref[pl.ds(r0, rows8), :ln], x_ref[pl.ds(r0, rows8), ln:]
            for k, (pr, pi) in zip((1, 2, 4), powers):
                shift, keep = (rows8 - k, row < rows8 - k) if reverse else (k, row >= k)
                sr = jnp.where(keep, pltpu.roll(xr, shift, 0), 0.0)
                si = jnp.where(keep, pltpu.roll(xi, shift, 0), 0.0)
                xr, xi = xr + pr * sr - pi * si, xi + pr * si + pi * sr
            cr, ci = carry[0], carry[1]
            yr = xr + ctr * cr - cti * ci
            yi = xi + ctr * ci + cti * cr
            o_ref[pl.ds(r0, rows8), :ln] = yr
            o_ref[pl.ds(r0, rows8), ln:] = yi
            edge = 0 if reverse else rows8 - 1
            new = (jnp.broadcast_to(yr[edge:edge + 1, :], (rows8, ln)), jnp.broadcast_to(yi[edge:edge + 1, :], (rows8, ln)))
            if not with_da:
                return new
            hr, hi = h_ref[pl.ds(r0, rows8), :ln], h_ref[pl.ds(r0, rows8), ln:]
            rp = pl.multiple_of(jnp.maximum(i - 1, 0) * rows8, rows8)
            last_r = jnp.where(i > 0, jnp.broadcast_to(h_ref[pl.ds(rp, rows8), :ln][rows8 - 1:, :], (rows8, ln)), 0.0)
            last_i = jnp.where(i > 0, jnp.broadcast_to(h_ref[pl.ds(rp, rows8), ln:][rows8 - 1:, :], (rows8, ln)), 0.0)
            hpr = jnp.where(row == 0, last_r, pltpu.roll(hr, 1, 0))
            hpi = jnp.where(row == 0, last_i, pltpu.roll(hi, 1, 0))
            return new + (carry[2] + yr * hpr + yi * hpi, carry[3] + yi * hpr - yr * hpi)

        init = (zero, zero, zero, zero) if with_da else (zero, zero)
        out = lax.fori_loop(0, ntile, step, init, unroll=4)
        if with_da:
            da_ref[:, :ln] = out[2]
            da_ref[:, ln:] = out[3]

    col = pl.BlockSpec((t, blk), lambda j: (0, j))
    tab = pl.BlockSpec((rows8, blk), lambda j: (0, j))
    in_specs = [col, tab, tab] + ([col] if with_da else [])
    out_specs = [col] + ([tab] if with_da else [])
    out_shape = [jax.ShapeDtypeStruct((t, width), F32)] + ([jax.ShapeDtypeStruct((rows8, width), F32)] if with_da else [])
    res = pl.pallas_call(
        body, name=name, grid=(width // blk,), in_specs=in_specs, out_specs=out_specs, out_shape=out_shape,
        compiler_params=_cparams("parallel"),
    )(x, carry_tab, step_tab, *([hist] if with_da else []))
    return res if with_da else res[0]


ATT_SCALE = (QK_NOPE + QK_ROPE) ** -0.5


def _att_mask(qi, kj, tb):
    qc = (qi * tb + lax.broadcasted_iota(jnp.int32, (tb, tb), 0)) // CHUNK
    kc = (kj * tb + lax.broadcasted_iota(jnp.int32, (tb, tb), 1)) // CHUNK
    return kc <= qc


def _with_side(side, n_main_in, n_main_out, refs, grid_first, grid_last, compute):
    if side is None:
        compute(refs)
        return
    n = side.n
    main = refs[:n_main_in] + refs[n_main_in + n:n_main_in + n + n_main_out] + refs[n_main_in + 2 * n + n_main_out + 3:]
    x_refs = refs[n_main_in:n_main_in + n]
    y_refs = refs[n_main_in + n + n_main_out:n_main_in + 2 * n + n_main_out]
    sems = refs[n_main_in + 2 * n + n_main_out:n_main_in + 2 * n + n_main_out + 3]

    @pl.when(grid_first)
    def _():
        for cp in side.copies(x_refs, y_refs, *sems):
            cp.start()

    compute(main)

    @pl.when(grid_last)
    def _():
        for cp in side.copies(x_refs, y_refs, *sems):
            cp.wait()


def _side_call(body, side, *, name, grid, in_specs, out_specs, out_shape, scratch, args, sem):
    n_out = len(out_shape)
    if side is not None:
        in_specs, args = in_specs + side.specs, list(args) + side.xs
        out_specs, out_shape = out_specs + side.specs, out_shape + side.out_shape
        scratch = side.scratch + scratch
        params = pltpu.CompilerParams(dimension_semantics=("arbitrary",) * len(grid), vmem_limit_bytes=VMEM_LIMIT,
                                      has_side_effects=True)
    else:
        params = _cparams(*sem)
    res = pl.pallas_call(body, name=name, grid=grid, in_specs=in_specs, out_specs=out_specs, out_shape=out_shape,
                         scratch_shapes=scratch, compiler_params=params)(*args)
    return res[:n_out], res[n_out:]


def _attn_fwd(q, k, v, *, name, side=None):
    h, t, dq = q.shape
    dv = v.shape[2]
    tb = min(ATT_BLOCK, t)
    nblk = t // tb
    nt = (((1,), (1,)), ((), ()))

    def compute(refs):
        q_ref, k_ref, v_ref, o_ref, lse_ref = refs
        i = pl.program_id(1)
        qb = q_ref[...]

        def kv_step(j, carry, masked):
            m, l, acc = carry
            r0 = pl.multiple_of(j * tb, tb)
            s = lax.dot_general(qb, k_ref[pl.ds(r0, tb), :], nt, preferred_element_type=F32) * ATT_SCALE
            if masked:
                s = jnp.where(_att_mask(i, j, tb), s, NEG_INF)
            m_new = jnp.maximum(m, jnp.max(s, axis=1, keepdims=True))
            alpha = jnp.exp(m - m_new)
            p = jnp.exp(s - m_new)
            l = alpha * l + jnp.sum(p, axis=1, keepdims=True)
            acc = alpha * acc + jnp.dot(p.astype(BF16), v_ref[pl.ds(r0, tb), :], preferred_element_type=F32)
            return m_new, l, acc

        init = (jnp.full((tb, 1), NEG_INF, F32), jnp.zeros((tb, 1), F32), jnp.zeros((tb, dv), F32))
        carry = lax.fori_loop(0, i, functools.partial(kv_step, masked=False), init)
        m, l, acc = kv_step(i, carry, True)
        o_ref[...] = acc / l
        lse_ref[...] = m + jnp.log(l)

    def body(*refs):
        first = (pl.program_id(0) == 0) & (pl.program_id(1) == 0)
        last = (pl.program_id(0) == h - 1) & (pl.program_id(1) == nblk - 1)
        _with_side(side, 3, 2, refs, first, last, compute)

    return _side_call(
        body, side, name=name, grid=(h, nblk),
        in_specs=[pl.BlockSpec((None, tb, dq), lambda hh, i: (hh, i, 0)),
                  pl.BlockSpec((None, t, dq), lambda hh, i: (hh, 0, 0)),
                  pl.BlockSpec((None, t, dv), lambda hh, i: (hh, 0, 0))],
        out_specs=[pl.BlockSpec((None, tb, dv), lambda hh, i: (hh, i, 0)),
                   pl.BlockSpec((None, tb, 1), lambda hh, i: (hh, i, 0))],
        out_shape=[jax.ShapeDtypeStruct((h, t, dv), F32), jax.ShapeDtypeStruct((h, t, 1), F32)],
        scratch=[], args=[q, k, v], sem=("parallel", "parallel"))


def _attn_bwd(q, k, v, o, lse, do, *, name, side=None):
    h, t, dq_w = q.shape
    dv_w = v.shape[2]
    tb = min(ATT_BLOCK, t)
    nblk = t // tb
    nt = (((1,), (1,)), ((), ()))
    tn = (((0,), (0,)), ((), ()))

    def compute(refs):
        q_ref, k_ref, v_ref, o_ref, lse_ref, do_ref, dq_ref, dk_ref, dv_ref, delta_ref = refs
        j = pl.program_id(1)

        @pl.when(j == 0)
        def _():
            dq_ref[...] = jnp.zeros_like(dq_ref)

            def dstep(i, c):
                r0 = pl.multiple_of(i * tb, tb)
                delta_ref[pl.ds(r0, tb), :] = jnp.sum(do_ref[pl.ds(r0, tb), :].astype(F32) * o_ref[pl.ds(r0, tb), :],
                                                      axis=1, keepdims=True)
                return c

            lax.fori_loop(0, nblk, dstep, 0)

        kb, vb = k_ref[...], v_ref[...]

        def q_step(i, carry, masked):
            dk, dv = carry
            r0 = pl.multiple_of(i * tb, tb)
            qb, dob = q_ref[pl.ds(r0, tb), :], do_ref[pl.ds(r0, tb), :]
            s = lax.dot_general(qb, kb, nt, preferred_element_type=F32) * ATT_SCALE
            if masked:
                s = jnp.where(_att_mask(i, j, tb), s, NEG_INF)
            p = jnp.exp(s - lse_ref[pl.ds(r0, tb), :])
            dv = dv + lax.dot_general(p.astype(BF16), dob, tn, preferred_element_type=F32)
            dp = lax.dot_general(dob, vb, nt, preferred_element_type=F32)
            ds = (p * (dp - delta_ref[pl.ds(r0, tb), :]) * ATT_SCALE).astype(BF16)
            dk = dk + lax.dot_general(ds, qb, tn, preferred_element_type=F32)
            dq_ref[pl.ds(r0, tb), :] += jnp.dot(ds, kb, preferred_element_type=F32)
            return dk, dv

        carry = q_step(j, (jnp.zeros((tb, dq_w), F32), jnp.zeros((tb, dv_w), F32)), True)
        dk, dv = lax.fori_loop(j + 1, nblk, functools.partial(q_step, masked=False), carry)
        dk_ref[...] = dk
        dv_ref[...] = dv

    def body(*refs):
        first = (pl.program_id(0) == 0) & (pl.program_id(1) == 0)
        last = (pl.program_id(0) == h - 1) & (pl.program_id(1) == nblk - 1)
        _with_side(side, 6, 3, refs, first, last, compute)

    whole = lambda w: pl.BlockSpec((None, t, w), lambda hh, j: (hh, 0, 0))
    blockj = lambda w: pl.BlockSpec((None, tb, w), lambda hh, j: (hh, j, 0))
    return _side_call(
        body, side, name=name, grid=(h, nblk),
        in_specs=[whole(dq_w), blockj(dq_w), blockj(dv_w), whole(dv_w), whole(1), whole(dv_w)],
        out_specs=[whole(dq_w), blockj(dq_w), blockj(dv_w)],
        out_shape=[jax.ShapeDtypeStruct((h, t, dq_w), F32), jax.ShapeDtypeStruct((h, t, dq_w), F32),
                   jax.ShapeDtypeStruct((h, t, dv_w), F32)],
        scratch=[pltpu.VMEM((t, 1), F32)], args=[q, k, v, o, lse, do], sem=("parallel", "arbitrary"))


class _Exchange:
    def __init__(self, xs, gather):
        self.xs, self.gather, self.n = list(xs), gather, len(xs)
        shapes = [tuple(x.shape) if gather else tuple(x.shape[1:]) for x in xs]
        self.out_shape = [jax.ShapeDtypeStruct((N_DEV,) + shp, x.dtype) for shp, x in zip(shapes, xs)]
        self.specs = [pl.BlockSpec(memory_space=pl.ANY)] * self.n
        self.scratch = [pltpu.SemaphoreType.DMA((self.n, N_DEV - 1)), pltpu.SemaphoreType.DMA((self.n, N_DEV - 1)),
                        pltpu.SemaphoreType.DMA((self.n,))]

    def copies(self, x_refs, y_refs, send_sems, recv_sems, local_sems):
        mx, my, mc = lax.axis_index("x"), lax.axis_index("y"), lax.axis_index("c")
        me = 4 * mx + 2 * my + mc
        out = [pltpu.make_async_copy(x_refs[i] if self.gather else x_refs[i].at[me], y_refs[i].at[me], local_sems.at[i])
               for i in range(self.n)]
        for k in range(1, N_DEV):
            px = 1 - mx if k & 4 else mx
            py = 1 - my if k & 2 else my
            pc = 1 - mc if k & 1 else mc
            for i in range(self.n):
                out.append(pltpu.make_async_remote_copy(
                    src_ref=x_refs[i] if self.gather else x_refs[i].at[4 * px + 2 * py + pc], dst_ref=y_refs[i].at[me],
                    send_sem=send_sems.at[i, k - 1], recv_sem=recv_sems.at[i, k - 1],
                    device_id=(px, py, pc), device_id_type=pl.DeviceIdType.MESH))
        return out


def _exchange(xs, *, gather, name):
    ex = _Exchange(xs, gather)
    n = ex.n

    def body(*refs):
        copies = ex.copies(refs[:n], refs[n:2 * n], *refs[2 * n:])
        for cp in copies:
            cp.start()
        for cp in copies:
            cp.wait()

    return pl.pallas_call(
        body, name=name, out_shape=ex.out_shape, in_specs=ex.specs, out_specs=ex.specs, scratch_shapes=ex.scratch,
        compiler_params=pltpu.CompilerParams(has_side_effects=True),
    )(*ex.xs)


def _adamw(w, g, m, v, *, name, tm=256):
    parts = g.ndim == 3
    r, c = w.shape
    tm = _pick_rows(r, tm)

    def body(w_ref, g_ref, m_ref, v_ref, go_ref, d_ref, mo_ref, vo_ref):
        if parts:
            gv = g_ref[0].astype(F32)
            for k in range(1, N_DEV):
                gv = gv + g_ref[k].astype(F32)
        else:
            gv = g_ref[...]
        mn = ADAM_B1 * m_ref[...] + (1.0 - ADAM_B1) * gv
        vn = ADAM_B2 * v_ref[...] + (1.0 - ADAM_B2) * jnp.square(gv)
        m_hat = mn / (1.0 - ADAM_B1 ** ADAM_STEP)
        v_hat = vn / (1.0 - ADAM_B2 ** ADAM_STEP)
        go_ref[...] = gv
        d_ref[...] = -ADAM_LR * (m_hat / (jnp.sqrt(v_hat) + ADAM_EPS) + ADAM_WD * w_ref[...])
        mo_ref[...] = mn
        vo_ref[...] = vn

    spec = pl.BlockSpec((tm, c), lambda i: (i, 0))
    gspec = pl.BlockSpec((N_DEV, tm, c), lambda i: (0, i, 0)) if parts else spec
    return pl.pallas_call(
        body, name=name, grid=(r // tm,), in_specs=[spec, gspec, spec, spec], out_specs=[spec] * 4,
        out_shape=[jax.ShapeDtypeStruct((r, c), F32)] * 4, compiler_params=_cparams("parallel"),
    )(w, g, m, v)


def _permute_in(a):
    pad = jnp.zeros(a.shape[:-1] + (IN_PAD - IN_WIDTH,), a.dtype)
    return jnp.concatenate([a[..., lo:hi] for lo, hi in _IN_SEGMENTS] + [pad], axis=-1)


def _unpermute_in(a):
    out, pos = {}, 0
    for lo, hi in _IN_SEGMENTS:
        out[lo] = a[..., pos:pos + hi - lo]
        pos += hi - lo
    return jnp.concatenate([out[lo] for lo in sorted(out)], axis=-1)


def _full_from_gathered(g, kind):
    if kind == 'row':
        return g.reshape((-1,) + g.shape[2:])
    if kind == 'col':
        return g.transpose(1, 0, 2).reshape(g.shape[1], -1)
    return g.transpose(1, 2, 0, 3).reshape(g.shape[1], g.shape[2], -1)


def _contrib_from_full(g, kind):
    if kind == 'row':
        return g.reshape((N_DEV, -1) + g.shape[1:])
    if kind == 'col':
        return g.reshape(g.shape[0], N_DEV, -1).transpose(1, 0, 2)
    return g.reshape(g.shape[0], g.shape[1], N_DEV, -1).transpose(2, 0, 1, 3)


def _as2d(a, lead=0):
    return a.reshape(a.shape[:lead] + (-1, a.shape[-1]))


def _chan_cols(re, im):
    lead = re.shape[:-1]
    nb = S5_CH // SCAN_LANES
    return jnp.stack([re.reshape(lead + (nb, SCAN_LANES)), im.reshape(lead + (nb, SCAN_LANES))],
                     axis=-2).reshape(lead + (2 * S5_CH,))


def _s5_tables(lam_re, lam_im, log_dt, b_re, b_im, c_re, c_im):
    dt = jnp.exp(log_dt)[:, None]
    mag = jnp.exp(lam_re * dt)
    a_re = mag * jnp.cos(lam_im * dt)
    a_im = mag * jnp.sin(lam_im * dt)
    den = lam_re * lam_re + lam_im * lam_im
    f_re = ((a_re - 1.0) * lam_re + a_im * lam_im) / den
    f_im = (a_im * lam_re - (a_re - 1.0) * lam_im) / den
    bb_re = f_re[..., None] * b_re - f_im[..., None] * b_im
    bb_im = f_re[..., None] * b_im + f_im[..., None] * b_re
    eye = jnp.eye(S5_GROUPS, dtype=F32)
    wb_re = jnp.einsum('gpc,gh->gchp', bb_re, eye).reshape(S5_WIDTH, S5_CH)
    wb_im = jnp.einsum('gpc,gh->gchp', bb_im, eye).reshape(S5_WIDTH, S5_CH)
    wb = _chan_cols(wb_re, wb_im)
    wc_re = jnp.einsum('gcp,gh->hcgp', c_re, eye).reshape(S5_WIDTH, S5_CH)
    wc_im = jnp.einsum('gcp,gh->hcgp', c_im, eye).reshape(S5_WIDTH, S5_CH)
    wc = _chan_cols(wc_re, -wc_im).T
    a_row = _chan_cols(a_re.reshape(1, S5_CH), a_im.reshape(1, S5_CH))
    ku, kc = S5_WIDTH // S5_BD, 2 * S5_CH // S5_BD
    wb_c = jnp.stack([wb[k * ku:(k + 1) * ku, k * kc:(k + 1) * kc] for k in range(S5_BD)])
    wc_c = jnp.stack([wc[k * kc:(k + 1) * kc, k * ku:(k + 1) * ku] for k in range(S5_BD)])
    return wb_c, wc_c, a_row


def _scan_tables(a_row, conj):
    nb = S5_CH // SCAN_LANES
    a = a_row.reshape(nb, 2, SCAN_LANES)
    ar, ai = a[:, 0], (-a[:, 1] if conj else a[:, 1])
    pr, pi = [ar], [ai]
    for _ in range(V7X_SUBLANES - 1):
        pr, pi = pr + [pr[-1] * ar - pi[-1] * ai], pi + [pr[-1] * ai + pi[-1] * ar]
    order = range(V7X_SUBLANES - 1, -1, -1) if conj else range(V7X_SUBLANES)
    carry = jnp.stack([jnp.stack([pr[r], pi[r]], axis=1).reshape(-1) for r in order])
    zero = jnp.zeros_like(carry[0])
    step = jnp.stack([jnp.stack([pr[r], pi[r]], axis=1).reshape(-1) for r in (0, 1, 3)] + [zero] * 5)
    return carry, step


def _rope_tables(t):
    half = QK_ROPE // 2
    inv_freq = 1.0 / (ROPE_THETA ** (jnp.arange(0, QK_ROPE, 2, dtype=F32) / QK_ROPE))
    ang = jnp.arange(t, dtype=F32)[:, None] * inv_freq[None, :]
    cos, sin = jnp.cos(ang), jnp.sin(ang)
    zero = jnp.zeros_like(sin)

    def lay(nope, width, first, second):
        head = jnp.concatenate([jnp.full((t, nope), 1.0 if first is cos else 0.0, F32), first, second], axis=1)
        reps = width // head.shape[1]
        out = jnp.tile(head, (1, reps))
        return jnp.pad(out, ((0, 0), (0, width - out.shape[1])))

    hq = MLA_HEADS * (QK_NOPE + QK_ROPE)
    q_tabs = (lay(QK_NOPE, hq, cos, cos), lay(QK_NOPE, hq, -sin, zero), lay(QK_NOPE, hq, zero, sin))
    k_tabs = (lay(0, V7X_LANES, cos, cos)[:, :V7X_LANES] * (jnp.arange(V7X_LANES) < QK_ROPE),
              lay(0, V7X_LANES, -sin, zero) * (jnp.arange(V7X_LANES) < QK_ROPE),
              lay(0, V7X_LANES, zero, sin) * (jnp.arange(V7X_LANES) < QK_ROPE))
    return q_tabs, k_tabs


def _sgu_tables(w_s, b_s):
    pos = jnp.arange(SGU_CHUNK) // CHUNK
    mask = pos[None, :] <= pos[:, None]
    wm = jnp.where(mask[None], w_s, 0.0).reshape(SGU_GROUPS * SGU_CHUNK, SGU_CHUNK)
    bias = jnp.repeat(b_s.T, SGU_WIDTH // SGU_GROUPS, axis=1)
    return wm, bias


def _row(v):
    return v.reshape(1, -1)


def _layer_fwd(x, ada, w, rope_tabs, tag, side=None, after_attn=None):
    s = {'x': x}
    q_tabs, k_tabs = rope_tabs
    sc1, gt1, sc2, gt2 = _row(1.0 + ada[1]), _row(1.0 + ada[2]), _row(1.0 + ada[4]), _row(1.0 + ada[5])
    s.update(sc1=sc1, gt1=gt1, sc2=sc2, gt2=gt2)
    (h,) = _rowcall(_modulate_fn, [x], [sc1, _row(ada[0])], [(D_MODEL, BF16)], tm=512, name=f"mod1_{tag}")
    proj = _mm(h, w['w_in_p'], bias=w['b_in_p'], name=f"proj_{tag}", tn=1792)
    s.update(h=h, proj=proj)

    (wb, wc, a_row), s['s5_pull'] = jax.vjp(_s5_tables, *[w[n] for n in ('s5_lambda_re', 's5_lambda_im', 's5_log_dt',
                                                                         's5_b_re', 's5_b_im', 's5_c_re', 's5_c_im')])
    wb, wc = wb.astype(BF16), wc.astype(BF16)
    u_s5 = proj[:, P_S5:P_S5 + S5_WIDTH]
    bu = _mm_bd(u_s5, wb, 'nn', name=f"s5_bu_{tag}")
    hs = _s5_scan(bu, *_scan_tables(a_row, False), reverse=False, name=f"s5_scan_{tag}")
    ylin = _mm_bd(hs, wc, 'nn', name=f"s5_c_{tag}")
    s5_full = [_row(w['s5_d']), w['s5_w_glu'], _row(w['s5_b_glu'])]
    (y_s5,) = _rowcall(_s5_post_fn, [ylin, u_s5], s5_full, [(S5_WIDTH, BF16)], tm=256, name=f"s5_post_{tag}")
    s.update(wb=wb, wc=wc, a_row=a_row, u_s5=u_s5, hs=hs, ylin=ylin, y_s5=y_s5)

    mla_rows = [(proj, MLA_BLK, P_MLA // MLA_BLK), *q_tabs, *k_tabs]
    mla_full = [_row(w['mla_q_norm']), w['mla_w_q_up'], _row(w['mla_kv_norm']), w['mla_w_kv_up']]
    hq, hkv = MLA_HEADS * (QK_NOPE + QK_ROPE), MLA_HEADS * (QK_NOPE + V_HEAD)
    q_r, kv, kpe_r = _rowcall(_mla_pre_fn, mla_rows, mla_full, [(hq, BF16), (hkv, BF16), (V7X_LANES, BF16)],
                              tm=256, name=f"mla_pre_{tag}")
    t = x.shape[0]
    qh = q_r.reshape(t, MLA_HEADS, -1).transpose(1, 0, 2)
    kv3 = kv.reshape(t, MLA_HEADS, -1).transpose(1, 0, 2)
    kh = jnp.concatenate([kv3[:, :, :QK_NOPE], jnp.broadcast_to(kpe_r[None, :, :QK_ROPE], (MLA_HEADS, t, QK_ROPE))], axis=2)
    vh = kv3[:, :, QK_NOPE:]
    (o, lse), side_out = _attn_fwd(qh, kh, vh, name=f"attn_fwd_{tag}", side=side)
    if after_attn is not None:
        after_attn(side_out)
    y_mla = o.transpose(1, 0, 2).reshape(t, -1).astype(BF16)
    s.update(qh=qh, kh=kh, vh=vh, o=o, lse=lse, y_mla=y_mla)

    (wm, bias), s['sgu_pull'] = jax.vjp(_sgu_tables, w['sgu_w_s'], w['sgu_b_s'])
    sgu_rows = [(proj, SGU_WIDTH, P_USGU // SGU_WIDTH), (proj, SGU_WIDTH, P_VSGU // SGU_WIDTH)]
    sgu_full = [_row(w['sgu_ln_g']), _row(w['sgu_ln_b']), wm, bias]
    (y_sgu,) = _rowcall(_sgu_fn, sgu_rows, sgu_full, [(SGU_WIDTH, BF16)], tm=SGU_CHUNK, name=f"sgu_{tag}")
    s.update(sgu_full=sgu_full, y_sgu=y_sgu)

    wbr = w['w_branch'].reshape(-1, D_MODEL)
    gate_rows = [(proj, D_MODEL, b) for b in range(3)]
    (merged,) = _rowcall(_merge_fn, [y_s5, y_mla, y_sgu] + gate_rows, [wbr], [(D_MODEL, BF16)], tm=256, name=f"merge_{tag}")
    ymix = _mm(merged, w['w_out'], name=f"wout_{tag}")
    (x1,) = _rowcall(_ln_res_fn, [x, ymix], [gt1, _row(w['ln1_g']), _row(w['ln1_b'])], [(D_MODEL, F32)], tm=256,
                     name=f"ln1_{tag}")
    s.update(merged=merged, ymix=ymix, x1=x1)

    (h2,) = _rowcall(_modulate_fn, [x1], [sc2, _row(ada[3])], [(D_MODEL, BF16)], tm=512, name=f"mod2_{tag}")
    ab = _mm(h2, w['ffn_w_in'], name=f"ffn_in_{tag}", tn=1408)
    (act,) = _rowcall(_swiglu_fn, [(ab, FF_HIDDEN, 0), (ab, FF_HIDDEN, 1)], [], [(FF_HIDDEN, BF16)], tm=256,
                      name=f"swiglu_{tag}")
    f = _mm(act, w['ffn_w_out'], name=f"ffn_out_{tag}", tk=2816)
    (x2,) = _rowcall(_ln_res_fn, [x1, f], [gt2, _row(w['ln2_g']), _row(w['ln2_b'])], [(D_MODEL, F32)], tm=256,
                     name=f"ln2_{tag}")
    s.update(h2=h2, ab=ab, act=act, f=f)
    return x2, s, side_out


def _mod_bwd_fn(x, dh, dxa, scale_row):
    return (dxa + dh * scale_row, jnp.sum(dh * x, axis=0, keepdims=True), jnp.sum(dh, axis=0, keepdims=True))


def _layer_bwd(dx2, s, w, rope_tabs, tag, make_side=None):
    g = {}
    q_tabs, k_tabs = rope_tabs
    t = dx2.shape[0]
    ln_full = lambda gt, a, b: [gt, _row(w[a]), _row(w[b])]

    dx1_a, df, dgt2, g['ln2_g'], g['ln2_b'] = _rowcall_vjp(
        _ln_res_fn, [s['x1'], s['f']], ln_full(s['gt2'], 'ln2_g', 'ln2_b'), [dx2], [0, 1], [0, 1, 2],
        tm=256, name=f"ln2_bwd_{tag}", row_dtypes=[F32, BF16])
    dact = _mm(df, w['ffn_w_out'], tb=True, name=f"ffn_out_dx_{tag}", tn=1408)
    g['ffn_w_out'] = _mm(s['act'], df, ta=True, name=f"ffn_out_dw_{tag}", tm=1408)
    da_, db_ = _rowcall_vjp(_swiglu_fn, [(s['ab'], FF_HIDDEN, 0), (s['ab'], FF_HIDDEN, 1)], [], [dact], [0, 1], [],
                            tm=256, name=f"swiglu_bwd_{tag}", row_dtypes=[BF16, BF16])
    dab = jnp.concatenate([da_, db_], axis=1)
    dh2 = _mm(dab, w['ffn_w_in'], tb=True, name=f"ffn_in_dx_{tag}", tk=1408)
    g['ffn_w_in'] = _mm(s['h2'], dab, ta=True, name=f"ffn_in_dw_{tag}", tn=1408)
    dx1, dsc2, dsh2 = _rowcall(_mod_bwd_fn, [s['x1'], dh2, dx1_a], [s['sc2']], [(D_MODEL, F32)],
                               [((1, D_MODEL), F32)] * 2, tm=256, name=f"mod2_bwd_{tag}")

    dx_a, dymix, dgt1, g['ln1_g'], g['ln1_b'] = _rowcall_vjp(
        _ln_res_fn, [s['x'], s['ymix']], ln_full(s['gt1'], 'ln1_g', 'ln1_b'), [dx1], [0, 1], [0, 1, 2],
        tm=256, name=f"ln1_bwd_{tag}", row_dtypes=[F32, BF16])
    dmerged = _mm(dymix, w['w_out'], tb=True, name=f"wout_dx_{tag}")
    g['w_out'] = _mm(s['merged'], dymix, ta=True, name=f"wout_dw_{tag}")

    proj = s['proj']
    wbr = w['w_branch'].reshape(-1, D_MODEL)
    gate_rows = [(proj, D_MODEL, b) for b in range(3)]
    dy_s5, dy_mla, dy_sgu, dl0, dl1, dl2, dwbr = _rowcall_vjp(
        _merge_fn, [s['y_s5'], s['y_mla'], s['y_sgu']] + gate_rows, [wbr], [dmerged], [0, 1, 2, 3, 4, 5], [0],
        tm=256, name=f"merge_bwd_{tag}")
    g['w_branch'] = dwbr.reshape(w['w_branch'].shape)

    sgu_rows = [(proj, SGU_WIDTH, P_USGU // SGU_WIDTH), (proj, SGU_WIDTH, P_VSGU // SGU_WIDTH)]
    du_sgu, dv_sgu, dlg, dlb, dwm, dbias = _rowcall_vjp(
        _sgu_fn, sgu_rows, s['sgu_full'], [dy_sgu], [0, 1], [0, 1, 2, 3], tm=SGU_CHUNK, name=f"sgu_bwd_{tag}")
    g['sgu_ln_g'], g['sgu_ln_b'] = dlg.reshape(-1), dlb.reshape(-1)
    g['sgu_w_s'], g['sgu_b_s'] = s['sgu_pull']((dwm, dbias))

    do = dy_mla.reshape(t, MLA_HEADS, V_HEAD).transpose(1, 0, 2).astype(BF16)
    (dqh, dkh, dvh), side_out = _attn_bwd(s['qh'], s['kh'], s['vh'], s['o'], s['lse'], do, name=f"attn_bwd_{tag}",
                                          side=make_side(g) if make_side is not None else None)
    dq_r = dqh.transpose(1, 0, 2).reshape(t, -1)
    dkv = jnp.concatenate([dkh[:, :, :QK_NOPE], dvh], axis=2).transpose(1, 0, 2).reshape(t, -1)
    dkpe = jnp.pad(jnp.sum(dkh[:, :, QK_NOPE:], axis=0), ((0, 0), (0, V7X_LANES - QK_ROPE)))
    mla_rows = [(proj, MLA_BLK, P_MLA // MLA_BLK), *q_tabs, *k_tabs]
    mla_full = [_row(w['mla_q_norm']), w['mla_w_q_up'], _row(w['mla_kv_norm']), w['mla_w_kv_up']]
    dmla, dqn, g['mla_w_q_up'], dkvn, g['mla_w_kv_up'] = _rowcall_vjp(
        _mla_pre_fn, mla_rows, mla_full, [dq_r, dkv, dkpe], [0], [0, 1, 2, 3], tm=256, name=f"mla_pre_bwd_{tag}")
    g['mla_q_norm'], g['mla_kv_norm'] = dqn.reshape(-1), dkvn.reshape(-1)

    s5_full = [_row(w['s5_d']), w['s5_w_glu'], _row(w['s5_b_glu'])]
    dylin, du_a, dd, g['s5_w_glu'], dbg = _rowcall_vjp(
        _s5_post_fn, [s['ylin'], s['u_s5']], s5_full, [dy_s5], [0, 1], [0, 1, 2], tm=256, name=f"s5_post_bwd_{tag}",
        row_dtypes=[BF16, F32])
    g['s5_d'], g['s5_b_glu'] = dd.reshape(-1), dbg.reshape(-1)
    dhs = _mm_bd(dylin, s['wc'], 'nt', name=f"s5_c_dx_{tag}")
    dwc = _mm_bd(s['hs'], dylin, 'tn', name=f"s5_c_dw_{tag}")
    gs, da_part = _s5_scan(dhs, *_scan_tables(s['a_row'], True), reverse=True, hist=s['hs'], name=f"s5_scan_bwd_{tag}")
    du_b = _mm_bd(gs, s['wb'], 'nt', name=f"s5_bu_dx_{tag}")
    dwb = _mm_bd(s['u_s5'], gs, 'tn', name=f"s5_bu_dw_{tag}")
    da_row = jnp.sum(da_part, axis=0, keepdims=True)
    for n, v in zip(('s5_lambda_re', 's5_lambda_im', 's5_log_dt', 's5_b_re', 's5_b_im', 's5_c_re', 's5_c_im'),
                    s['s5_pull']((dwb, dwc, da_row))):
        g[n] = v

    dproj_f32 = jnp.concatenate([dl0, dl1, dl2, du_a + du_b, du_sgu, dv_sgu, dmla], axis=1)
    dproj, db_in = _rowcall(lambda d: (d, jnp.sum(d, axis=0, keepdims=True)), [dproj_f32], [], [(IN_PAD, BF16)],
                            [((1, IN_PAD), F32)], tm=256, name=f"dproj_{tag}")
    dh = _mm(dproj, w['w_in_p'], tb=True, name=f"proj_dx_{tag}", tk=1792)
    g['w_in'] = _unpermute_in(_mm(s['h'], dproj, ta=True, name=f"proj_dw_{tag}", tn=896))
    g['b_in'] = _unpermute_in(db_in).reshape(-1)
    dx, dsc1, dsh1 = _rowcall(_mod_bwd_fn, [s['x'], dh, dx_a], [s['sc1']], [(D_MODEL, F32)], [((1, D_MODEL), F32)] * 2,
                              tm=256, name=f"mod1_bwd_{tag}")
    d_ada = jnp.concatenate([dsh1, dsc1, dgt1, dsh2, dsc2, dgt2], axis=0)
    return dx, d_ada, g, side_out


def _loss_fn(y, target):
    err = y - target
    return (err / D_MODEL, 0.5 * jnp.sum(jnp.sum(err * err, axis=1, keepdims=True), axis=0, keepdims=True) / D_MODEL)


def _step(p):
    me = 4 * lax.axis_index("x") + 2 * lax.axis_index("y") + lax.axis_index("c")
    x = p['x'][0]
    t = x.shape[0]
    rope_tabs = _rope_tables(t)

    (c_all,) = _exchange([jnp.broadcast_to(p['c'], (V7X_SUBLANES, D_MODEL))], gather=True, name="gather_c")
    c_all = c_all[:, 0, :]
    (c_act,) = _rowcall(lambda cc: (cc * _sigmoid(cc),), [c_all], [], [(D_MODEL, F32)], tm=N_DEV, name="c_silu")
    ncol = p['w_ada'].shape[2]
    b_ada_loc = lax.dynamic_slice_in_dim(p['b_ada'], me * ncol, ncol, axis=1)
    ada_cols = jnp.concatenate([_mm(c_act, p['w_ada'][l], bias=b_ada_loc[l:l + 1], name=f"ada_{l}") for l in range(DEPTH)])
    (ada_all,) = _exchange([ada_cols], gather=True, name="gather_ada")
    ada_all = ada_all.reshape(N_DEV, DEPTH, N_DEV, ncol)
    ada = lax.dynamic_index_in_dim(ada_all, me, axis=2, keepdims=False)
    ada = ada.transpose(1, 0, 2).reshape(DEPTH, 6, D_MODEL)

    mixer_w, ffn_w = SHARDED[:-2], SHARDED[-2:]

    def shards(l, group):
        return [p[n][l].astype(BF16) for n, _ in group]

    def contribs(g, group):
        return [_contrib_from_full(g[n], kind).astype(BF16) for n, kind in group]

    def mixer_weights(l, gathered):
        w = {n: _full_from_gathered(g, kind) for (n, kind), g in zip(mixer_w, gathered)}
        w['w_in_p'] = _permute_in(w.pop('w_in'))
        w['b_in_p'] = _row(_permute_in(p['b_in'][l]))
        for n in SMALL:
            if n != 'b_ada' and n != 'b_in':
                w[n] = p[n][l]
        return w

    saved, layers = [], []
    gathered = _exchange(shards(0, mixer_w), gather=True, name="gather_w_0")
    for l in range(DEPTH):
        w = mixer_weights(l, gathered)
        side = _Exchange(shards(l, ffn_w) + (shards(l + 1, mixer_w) if l + 1 < DEPTH else []), True)

        def add_ffn(res, w=w):
            for (n, kind), g in zip(ffn_w, res):
                w[n] = _full_from_gathered(g, kind)

        x, s, res = _layer_fwd(x, ada[l], w, rope_tabs, f"l{l}", side, add_ffn)
        gathered = res[len(ffn_w):]
        layers.append(w)
        saved.append(s)
    dy, loss_loc = _rowcall(_loss_fn, [x, p['loss_target'][0]], [], [(D_MODEL, F32)], [((1, 1), F32)], tm=256, name="loss")
    loss = lax.psum(loss_loc[0, 0], ("x", "y", "c"))

    d_ada, grads, landed = [None] * DEPTH, [None] * DEPTH, [[None, None] for _ in range(DEPTH)]
    dx, pending = dy, []
    for l in reversed(range(DEPTH)):
        make_side = lambda g, pending=pending: _Exchange(contribs(g, ffn_w) + pending, False)
        dx, d_ada[l], grads[l], res = _layer_bwd(dx, saved[l], layers[l], rope_tabs, f"l{l}", make_side)
        landed[l][1] = res[:len(ffn_w)]
        if l + 1 < DEPTH:
            landed[l + 1][0] = res[len(ffn_w):]
        pending = contribs(grads[l], mixer_w)
    landed[0][0] = _exchange(pending, gather=False, name="scatter_g_0")
    landed = [list(a) + list(b) for a, b in landed]
    d_ada = jnp.stack(d_ada).reshape(DEPTH, 6 * D_MODEL)

    (d_ada_all,) = _exchange([d_ada], gather=True, name="gather_dada")
    d_ada_cols = lax.dynamic_slice_in_dim(d_ada_all, me * ncol, ncol, axis=2)
    pad_b = ((0, V7X_LANES - N_DEV), (0, 0))
    c_act_p = jnp.pad(c_act, pad_b)
    g_w_ada = jnp.stack([_mm(c_act_p, jnp.pad(d_ada_cols[:, l], pad_b), ta=True, name=f"ada_dw_{l}") for l in range(DEPTH)])

    out = {}
    for l in range(DEPTH):
        for (n, _), g_l in zip(SHARDED, landed[l]):
            res = _adamw(_as2d(p[n][l]), _as2d(g_l, lead=1), _as2d(p['m_' + n][l]), _as2d(p['v_' + n][l]),
                         name=f"adamw_{n}_{l}")
            for kind, r in zip(('grad_', 'delta_', 'new_m_', 'new_v_'), res):
                out.setdefault(kind + n, []).append(r.reshape(p[n].shape[1:]))
    out = {k: jnp.stack(v) for k, v in out.items()}

    r_ada = DEPTH * D_MODEL
    res = _adamw(p['w_ada'].reshape(r_ada, ncol), g_w_ada.reshape(r_ada, ncol), p['m_w_ada'].reshape(r_ada, ncol),
                 p['v_w_ada'].reshape(r_ada, ncol), name="adamw_ada", tm=256)
    for kind, flat in zip(('grad_', 'delta_', 'new_m_', 'new_v_'), res):
        out[kind + 'w_ada'] = flat.reshape(p['w_ada'].shape)

    small_g = {n: jnp.stack([grads[l][n] for l in range(DEPTH)]) for n in SMALL if n != 'b_ada'}
    small_g['b_ada'] = d_ada
    n_small = sum(int(np.prod(p[n].shape)) for n in SMALL)
    n_pad = -(-n_small // SMALL_PAD) * SMALL_PAD
    chunk_rows = n_pad // N_DEV // 1024

    def flat_small(get):
        flat = jnp.concatenate([get(n).reshape(-1) for n in SMALL])
        return jnp.pad(flat, (0, n_pad - n_small)).reshape(N_DEV, chunk_rows, 1024)

    (landed_small,) = _exchange([flat_small(lambda n: small_g[n])], gather=False, name="scatter_small")
    mine = lambda pre: lax.dynamic_index_in_dim(flat_small(lambda n: p[pre + n]), me, axis=0, keepdims=False)
    res = _adamw(mine(''), landed_small, mine('m_'), mine('v_'), name="adamw_small", tm=56)
    (res_all,) = _exchange([jnp.concatenate(res, axis=0)], gather=True, name="gather_small")
    res_all = res_all.reshape(N_DEV, 4, chunk_rows * 1024).transpose(1, 0, 2).reshape(4, n_pad)
    off = 0
    for n in SMALL:
        size = int(np.prod(p[n].shape))
        for k, kind in enumerate(('grad_', 'delta_', 'new_m_', 'new_v_')):
            out[kind + n] = res_all[k, off:off + size].reshape(p[n].shape)
        off += size

    outs = [loss, dx[None]]
    for kind in ('grad_', 'delta_', 'new_m_', 'new_v_'):
        outs += [out[kind + n] for n in WNAMES]
    return tuple(outs)


def kernel(x, c, w_ada, b_ada, w_in, b_in, s5_lambda_re, s5_lambda_im, s5_log_dt, s5_b_re, s5_b_im, s5_c_re, s5_c_im, s5_d, s5_w_glu, s5_b_glu, mla_q_norm, mla_w_q_up, mla_kv_norm, mla_w_kv_up, sgu_ln_g, sgu_ln_b, sgu_w_s, sgu_b_s, w_branch, w_out, ln1_g, ln1_b, ffn_w_in, ffn_w_out, ln2_g, ln2_b, loss_target, m_w_ada, m_b_ada, m_w_in, m_b_in, m_s5_lambda_re, m_s5_lambda_im, m_s5_log_dt, m_s5_b_re, m_s5_b_im, m_s5_c_re, m_s5_c_im, m_s5_d, m_s5_w_glu, m_s5_b_glu, m_mla_q_norm, m_mla_w_q_up, m_mla_kv_norm, m_mla_w_kv_up, m_sgu_ln_g, m_sgu_ln_b, m_sgu_w_s, m_sgu_b_s, m_w_branch, m_w_out, m_ln1_g, m_ln1_b, m_ffn_w_in, m_ffn_w_out, m_ln2_g, m_ln2_b, v_w_ada, v_b_ada, v_w_in, v_b_in, v_s5_lambda_re, v_s5_lambda_im, v_s5_log_dt, v_s5_b_re, v_s5_b_im, v_s5_c_re, v_s5_c_im, v_s5_d, v_s5_w_glu, v_s5_b_glu, v_mla_q_norm, v_mla_w_q_up, v_mla_kv_norm, v_mla_w_kv_up, v_sgu_ln_g, v_sgu_ln_b, v_sgu_w_s, v_sgu_b_s, v_w_branch, v_w_out, v_ln1_g, v_ln1_b, v_ffn_w_in, v_ffn_w_out, v_ln2_g, v_ln2_b):
    return _step(dict(locals()))
```

```python
import functools
import math

import numpy as np
import jax
import jax.numpy as jnp
from jax import lax
from jax.experimental import pallas as pl
from jax.experimental.pallas import tpu as pltpu

F32 = jnp.float32
BF16 = jnp.bfloat16

N_DEV = 8
D_MODEL = 1024
DEPTH = 4
CHUNK = 64
S5_WIDTH = 512
S5_GROUP = 16
S5_GROUPS = 32
S5_STATE = 64
MLA_HEADS = 8
QK_NOPE = 64
QK_ROPE = 32
V_HEAD = 64
Q_LORA = 384
KV_LORA = 256
ROPE_THETA = 10000.0
SGU_WIDTH = 512
SGU_GROUPS = 4
SGU_CHUNK = 128
FF_HIDDEN = 2816
DEEPNORM_ALPHA = (2 * DEPTH) ** 0.25
LN_EPS = 1e-5
RMS_EPS = 1e-6
NEG_INF = -1e30
ADAM_LR = 0.001
ADAM_B1 = 0.9
ADAM_B2 = 0.999
ADAM_EPS = 1e-08
ADAM_WD = 0.01
ADAM_STEP = 10

IN_WIDTH = 5280
IN_PAD = 5376
_O_S5, _O_CQ, _O_CKV, _O_KPE, _O_USGU, _O_VSGU, _O_GATE = 0, 512, 896, 1152, 1184, 1696, 2208
_IN_SEGMENTS = ((_O_GATE, IN_WIDTH), (_O_S5, _O_CQ), (_O_USGU, _O_VSGU), (_O_VSGU, _O_GATE), (_O_CQ, _O_USGU))
P_GATE, P_S5, P_USGU, P_VSGU, P_MLA = 0, 3072, 3584, 4096, 4608
MLA_BLK = 768

V7X_LANES = 128
V7X_SUBLANES = 8
VMEM_LIMIT = 48 * 1024 * 1024
ATT_BLOCK = 512
SCAN_LANES = 128
S5_CH = S5_GROUPS * S5_STATE
S5_BD = 4

WNAMES = ['w_ada', 'b_ada', 'w_in', 'b_in', 's5_lambda_re', 's5_lambda_im', 's5_log_dt', 's5_b_re', 's5_b_im',
          's5_c_re', 's5_c_im', 's5_d', 's5_w_glu', 's5_b_glu', 'mla_q_norm', 'mla_w_q_up', 'mla_kv_norm',
          'mla_w_kv_up', 'sgu_ln_g', 'sgu_ln_b', 'sgu_w_s', 'sgu_b_s', 'w_branch', 'w_out', 'ln1_g', 'ln1_b',
          'ffn_w_in', 'ffn_w_out', 'ln2_g', 'ln2_b']
SHARDED = (('w_in', 'col'), ('s5_w_glu', 'row'), ('mla_w_q_up', 'col'), ('mla_w_kv_up', 'col'),
           ('w_branch', 'col3'), ('w_out', 'row'), ('ffn_w_in', 'col'), ('ffn_w_out', 'row'))
SMALL = [n for n in WNAMES if n != 'w_ada' and n not in dict(SHARDED)]
SMALL_PAD = N_DEV * V7X_SUBLANES * 1024


def _cparams(*sem):
    return pltpu.CompilerParams(dimension_semantics=sem, vmem_limit_bytes=VMEM_LIMIT)


def _pick(n, target):
    if n <= target:
        return n
    best = None
    for d in range(V7X_LANES, target + 1, V7X_LANES):
        if n % d == 0:
            best = d
    assert best is not None, (n, target)
    return best


def _pick_rows(n, target):
    if n <= target:
        return n
    best = None
    for d in range(V7X_SUBLANES, target + 1, V7X_SUBLANES):
        if n % d == 0:
            best = d
    assert best is not None, (n, target)
    return best


@jax.custom_vjp
def _bdot(a, b):
    return jnp.dot(a.astype(BF16), b.astype(BF16), preferred_element_type=F32)


def _bdot_fwd(a, b):
    return _bdot(a, b), (a, b)


def _bdot_bwd(res, g):
    a, b = res
    gb = g.astype(BF16)
    da = lax.dot_general(gb, b.astype(BF16), (((1,), (1,)), ((), ())), preferred_element_type=F32)
    db = lax.dot_general(a.astype(BF16), gb, (((0,), (0,)), ((), ())), preferred_element_type=F32)
    return da.astype(a.dtype), db.astype(b.dtype)


_bdot.defvjp(_bdot_fwd, _bdot_bwd)


@functools.partial(jax.custom_vjp, nondiff_argnums=(1,))
def _lane_roll(x, shift):
    return pltpu.roll(x, shift % x.shape[1], 1)


def _lane_roll_fwd(x, shift):
    return _lane_roll(x, shift), None


def _lane_roll_bwd(shift, _, g):
    return (_lane_roll(g, -shift),)


_lane_roll.defvjp(_lane_roll_fwd, _lane_roll_bwd)


def _sigmoid(x):
    return 1.0 / (1.0 + jnp.exp(-x))


def _gelu(x):
    return 0.5 * x * (1.0 + jnp.tanh(math.sqrt(2.0 / math.pi) * (x + 0.044715 * (x * x * x))))


def _layer_norm(x, g, b):
    mu = jnp.mean(x, axis=-1, keepdims=True)
    var = jnp.mean(jnp.square(x - mu), axis=-1, keepdims=True)
    return (x - mu) * lax.rsqrt(var + LN_EPS) * g + b


def _rms_norm(x, g):
    return x * lax.rsqrt(jnp.mean(x * x, axis=-1, keepdims=True) + RMS_EPS) * g


def _rope(x, c, s1, s2):
    half = QK_ROPE // 2
    return x * c + _lane_roll(x, -half) * s1 + _lane_roll(x, half) * s2


def _modulate_fn(x, scale_row, shift_row):
    return (x * scale_row + shift_row,)


def _ln_res_fn(x, y, gate_row, g, b):
    return (_layer_norm(DEEPNORM_ALPHA * x + gate_row * y, g, b),)


def _s5_post_fn(ylin, u, d, w_glu, b_glu):
    z = _gelu(ylin + d * u)
    return (z * _sigmoid(_bdot(z, w_glu) + b_glu),)


def _mla_pre_fn(blk, cq_t, sq1, sq2, ck_t, sk1, sk2, q_norm, w_q, kv_norm, w_kv):
    cq, ckv, kpe = blk[:, :Q_LORA], blk[:, Q_LORA:Q_LORA + KV_LORA], blk[:, Q_LORA + KV_LORA:]
    q = _rope(_bdot(_rms_norm(cq, q_norm), w_q), cq_t, sq1, sq2)
    kv = _bdot(_rms_norm(ckv, kv_norm), w_kv)
    return q, kv, _rope(kpe, ck_t, sk1, sk2)


def _sgu_fn(u, v, g, b, wm, bias):
    vn = _layer_norm(_gelu(v), g, b)
    w = SGU_CHUNK
    parts = [_bdot(wm[k * w:(k + 1) * w, :], vn[:, k * w:(k + 1) * w]) for k in range(SGU_GROUPS)]
    return (_gelu(u) * (jnp.concatenate(parts, axis=1) + bias),)


def _merge_fn(y0, y1, y2, l0, l1, l2, wb):
    n = S5_WIDTH
    return (_sigmoid(l0) * _bdot(y0, wb[:n]) + _sigmoid(l1) * _bdot(y1, wb[n:2 * n])
            + _sigmoid(l2) * _bdot(y2, wb[2 * n:]),)


def _swiglu_fn(a, b):
    return (a * _sigmoid(a) * b,)


def _rowcall(fn, rows, fulls, out_rows, out_reds=(), *, tm, name):
    rows = [r if isinstance(r, tuple) else (r, r.shape[1], 0) for r in rows]
    t = rows[0][0].shape[0]
    tm = _pick_rows(t, tm)
    n_in, n_or, n_red = len(rows) + len(fulls), len(out_rows), len(out_reds)

    def body(*refs):
        vals = fn(*[r[...] for r in refs[:n_in]])
        assert len(vals) == n_or + n_red, (name, len(vals))
        for ref, v in zip(refs[n_in:n_in + n_or], vals[:n_or]):
            ref[...] = v.astype(ref.dtype)
        if n_red:
            red_refs = refs[n_in + n_or:]

            @pl.when(pl.program_id(0) == 0)
            def _():
                for ref in red_refs:
                    ref[...] = jnp.zeros_like(ref)

            for ref, v in zip(red_refs, vals[n_or:]):
                ref[...] += v.astype(ref.dtype)

    in_specs = [pl.BlockSpec((tm, w), functools.partial(lambda i, blk: (i, blk), blk=blk)) for _, w, blk in rows]
    in_specs += [pl.BlockSpec(f.shape, lambda i: (0, 0)) for f in fulls]
    out_specs = [pl.BlockSpec((tm, c), lambda i: (i, 0)) for c, _ in out_rows]
    out_specs += [pl.BlockSpec(s, lambda i: (0, 0)) for s, _ in out_reds]
    out_shape = [jax.ShapeDtypeStruct((t, c), dt) for c, dt in out_rows]
    out_shape += [jax.ShapeDtypeStruct(s, dt) for s, dt in out_reds]
    return pl.pallas_call(
        body, name=name, grid=(t // tm,), in_specs=in_specs, out_specs=out_specs, out_shape=out_shape,
        compiler_params=_cparams("arbitrary" if n_red else "parallel"),
    )(*[r[0] for r in rows], *fulls)


def _rowcall_vjp(fn, rows, fulls, cots, diff_rows, diff_fulls, *, tm, name, row_dtypes=None):
    rows_n = [r if isinstance(r, tuple) else (r, r.shape[1], 0) for r in rows]
    n_r, n_c = len(rows), len(cots)
    row_dtypes = row_dtypes or [F32] * len(diff_rows)

    def fn2(*vals):
        r = [v.astype(F32) for v in vals[:n_r]]
        ct = vals[n_r:n_r + n_c]
        f = [v.astype(F32) for v in vals[n_r + n_c:]]

        def g(*dargs):
            rr, ff = list(r), list(f)
            for k, idx in enumerate(diff_rows):
                rr[idx] = dargs[k]
            for k, idx in enumerate(diff_fulls):
                ff[idx] = dargs[len(diff_rows) + k]
            return fn(*rr, *ff)

        prim = [r[i] for i in diff_rows] + [f[i] for i in diff_fulls]
        outs, pull = jax.vjp(g, *prim)
        return pull(tuple(c.astype(o.dtype) for c, o in zip(ct, outs)))

    out_rows = [(rows_n[i][1], dt) for i, dt in zip(diff_rows, row_dtypes)]
    out_reds = [(fulls[i].shape, F32) for i in diff_fulls]
    return _rowcall(fn2, list(rows) + list(cots), fulls, out_rows, out_reds, tm=tm, name=name)


def _mm(a, b, *, ta=False, tb=False, bias=None, out_dtype=F32, name, tm=1024, tn=1024, tk=1024):
    (k_a, m) = a.shape if ta else a.shape[::-1]
    (n, k_b) = b.shape if tb else b.shape[::-1]
    assert k_a == k_b, (name, a.shape, b.shape)
    tm, tn, tk = _pick(m, tm) if m % V7X_LANES == 0 else m, _pick(n, tn), _pick(k_a, tk) if k_a % V7X_LANES == 0 else k_a
    nk = k_a // tk
    a_spec = pl.BlockSpec((tk, tm), lambda i, j, k: (k, i)) if ta else pl.BlockSpec((tm, tk), lambda i, j, k: (i, k))
    b_spec = pl.BlockSpec((tn, tk), lambda i, j, k: (j, k)) if tb else pl.BlockSpec((tk, tn), lambda i, j, k: (k, j))
    dims = (((0,) if ta else (1,), (1,) if tb else (0,)), ((), ()))
    has_bias = bias is not None

    def body(*refs):
        a_ref, b_ref = refs[0], refs[1]
        part = lax.dot_general(a_ref[...].astype(BF16), b_ref[...].astype(BF16), dims, preferred_element_type=F32)
        if nk == 1:
            o_ref = refs[-1]
            o_ref[...] = (part + refs[2][...] if has_bias else part).astype(o_ref.dtype)
            return
        o_ref, acc_ref = refs[-2], refs[-1]
        k = pl.program_id(2)

        @pl.when(k == 0)
        def _():
            acc_ref[...] = part

        @pl.when(k > 0)
        def _():
            acc_ref[...] += part

        @pl.when(k == nk - 1)
        def _():
            r = acc_ref[...]
            if has_bias:
                r = r + refs[2][...]
            o_ref[...] = r.astype(o_ref.dtype)

    in_specs = [a_spec, b_spec] + ([pl.BlockSpec((1, tn), lambda i, j, k: (0, j))] if has_bias else [])
    return pl.pallas_call(
        body, name=name, grid=(m // tm, n // tn, nk), in_specs=in_specs,
        out_specs=pl.BlockSpec((tm, tn), lambda i, j, k: (i, j)),
        out_shape=jax.ShapeDtypeStruct((m, n), out_dtype),
        scratch_shapes=[pltpu.VMEM((tm, tn), F32)] if nk > 1 else [],
        compiler_params=_cparams("parallel", "parallel", "arbitrary"),
    )(a, b, *([bias] if has_bias else []))


def _mm_bd(a, b, mode, *, name, tm=512, tk=1024):
    if mode == 'tn':
        t = a.shape[0]
        nb = S5_BD
        ka, kb = a.shape[1] // nb, b.shape[1] // nb
        tk = _pick(t, tk)
        nsteps = t // tk

        def body_tn(a_ref, b_ref, o_ref):
            part = lax.dot_general(a_ref[...].astype(BF16), b_ref[...].astype(BF16), (((0,), (0,)), ((), ())),
                                   preferred_element_type=F32)

            @pl.when(pl.program_id(1) == 0)
            def _():
                o_ref[...] = part

            @pl.when(pl.program_id(1) > 0)
            def _():
                o_ref[...] += part

        return pl.pallas_call(
            body_tn, name=name, grid=(nb, nsteps),
            in_specs=[pl.BlockSpec((tk, ka), lambda k, s: (s, k)), pl.BlockSpec((tk, kb), lambda k, s: (s, k))],
            out_specs=pl.BlockSpec((None, ka, kb), lambda k, s: (k, 0, 0)),
            out_shape=jax.ShapeDtypeStruct((nb, ka, kb), F32), compiler_params=_cparams("parallel", "arbitrary"),
        )(a, b)
    nb, ka, kb = b.shape
    m = a.shape[0]
    tm = _pick(m, tm)
    w_in, w_out = (ka, kb) if mode == 'nn' else (kb, ka)
    dims = (((1,), (0,)), ((), ())) if mode == 'nn' else (((1,), (1,)), ((), ()))

    def body(a_ref, b_ref, o_ref):
        o_ref[...] = lax.dot_general(a_ref[...].astype(BF16), b_ref[...].astype(BF16), dims, preferred_element_type=F32)

    return pl.pallas_call(
        body, name=name, grid=(m // tm, nb),
        in_specs=[pl.BlockSpec((tm, w_in), lambda i, k: (i, k)), pl.BlockSpec((None, ka, kb), lambda i, k: (k, 0, 0))],
        out_specs=pl.BlockSpec((tm, w_out), lambda i, k: (i, k)),
        out_shape=jax.ShapeDtypeStruct((m, nb * w_out), F32), compiler_params=_cparams("parallel", "parallel"),
    )(a, b)


def _s5_scan(x, carry_tab, step_tab, *, reverse, hist=None, name):
    t, width = x.shape
    blk = 2 * SCAN_LANES
    ntile = t // V7X_SUBLANES
    with_da = hist is not None
    ln = SCAN_LANES
    rows8 = V7X_SUBLANES

    def body(*refs):
        if with_da:
            x_ref, ct_ref, st_ref, h_ref, o_ref, da_ref = refs
        else:
            x_ref, ct_ref, st_ref, o_ref = refs
        row = lax.broadcasted_iota(jnp.int32, (rows8, ln), 0)
        ctr, cti = ct_ref[:, :ln], ct_ref[:, ln:]
        powers = [(jnp.broadcast_to(st_ref[k:k + 1, :ln], (rows8, ln)),
                   jnp.broadcast_to(st_ref[k:k + 1, ln:], (rows8, ln))) for k in range(3)]
        zero = jnp.zeros((rows8, ln), F32)

        def step(n, carry):
            i = (ntile - 1 - n) if reverse else n
            r0 = pl.multiple_of(i * rows8, rows8)
            xr, xi = x_ref[pl.ds(r0, rows8), :ln], x_ref[pl.ds(r0, rows8), ln:]
            for k, (pr, pi) in zip((1, 2, 4), powers):
                shift, keep = (rows8 - k, row < rows8 - k) if reverse else (k, row >= k)
                sr = jnp.where(keep, pltpu.roll(xr, shift, 0), 0.0)
                si = jnp.where(keep, pltpu.roll(xi, shift, 0), 0.0)
                xr, xi = xr + pr * sr - pi * si, xi + pr * si + pi * sr
            cr, ci = carry[0], carry[1]
            yr = xr + ctr * cr - cti * ci
            yi = xi + ctr * ci + cti * cr
            o_ref[pl.ds(r0, rows8), :ln] = yr
            o_ref[pl.ds(r0, rows8), ln:] = yi
            edge = 0 if reverse else rows8 - 1
            new = (jnp.broadcast_to(yr[edge:edge + 1, :], (rows8, ln)), jnp.broadcast_to(yi[edge:edge + 1, :], (rows8, ln)))
            if not with_da:
                return new
            hr, hi = h_ref[pl.ds(r0, rows8), :ln], h_ref[pl.ds(r0, rows8), ln:]
            rp = pl.multiple_of(jnp.maximum(i - 1, 0) * rows8, rows8)
            last_r = jnp.where(i > 0, jnp.broadcast_to(h_ref[pl.ds(rp, rows8), :ln][rows8 - 1:, :], (rows8, ln)), 0.0)
            last_i = jnp.where(i > 0, jnp.broadcast_to(h_ref[pl.ds(rp, rows8), ln:][rows8 - 1:, :], (rows8, ln)), 0.0)
            hpr = jnp.where(row == 0, last_r, pltpu.roll(hr, 1, 0))
            hpi = jnp.where(row == 0, last_i, pltpu.roll(hi, 1, 0))
            return new + (carry[2] + yr * hpr + yi * hpi, carry[3] + yi * hpr - yr * hpi)

        init = (zero, zero, zero, zero) if with_da else (zero, zero)
        out = lax.fori_loop(0, ntile, step, init, unroll=4)
        if with_da:
            da_ref[:, :ln] = out[2]
            da_ref[:, ln:] = out[3]

    col = pl.BlockSpec((t, blk), lambda j: (0, j))
    tab = pl.BlockSpec((rows8, blk), lambda j: (0, j))
    in_specs = [col, tab, tab] + ([col] if with_da else [])
    out_specs = [col] + ([tab] if with_da else [])
    out_shape = [jax.ShapeDtypeStruct((t, width), F32)] + ([jax.ShapeDtypeStruct((rows8, width), F32)] if with_da else [])
    res = pl.pallas_call(
        body, name=name, grid=(width // blk,), in_specs=in_specs, out_specs=out_specs, out_shape=out_shape,
        compiler_params=_cparams("parallel"),
    )(x, carry_tab, step_tab, *([hist] if with_da else []))
    return res if with_da else res[0]


ATT_SCALE = (QK_NOPE + QK_ROPE) ** -0.5


def _att_mask(qi, kj, tb):
    qc = (qi * tb + lax.broadcasted_iota(jnp.int32, (tb, tb), 0)) // CHUNK
    kc = (kj * tb + lax.broadcasted_iota(jnp.int32, (tb, tb), 1)) // CHUNK
    return kc <= qc


def _with_side(side, n_main_in, n_main_out, refs, grid_first, grid_last, compute):
    if side is None:
        compute(refs)
        return
    n = side.n
    main = refs[:n_main_in] + refs[n_main_in + n:n_main_in + n + n_main_out] + refs[n_main_in + 2 * n + n_main_out + 3:]
    x_refs = refs[n_main_in:n_main_in + n]
    y_refs = refs[n_main_in + n + n_main_out:n_main_in + 2 * n + n_main_out]
    sems = refs[n_main_in + 2 * n + n_main_out:n_main_in + 2 * n + n_main_out + 3]

    @pl.when(grid_first)
    def _():
        for cp in side.copies(x_refs, y_refs, *sems):
            cp.start()

    compute(main)

    @pl.when(grid_last)
    def _():
        for cp in side.copies(x_refs, y_refs, *sems):
            cp.wait()


def _side_call(body, side, *, name, grid, in_specs, out_specs, out_shape, scratch, args, sem):
    n_out = len(out_shape)
    if side is not None:
        in_specs, args = in_specs + side.specs, list(args) + side.xs
        out_specs, out_shape = out_specs + side.specs, out_shape + side.out_shape
        scratch = side.scratch + scratch
        params = pltpu.CompilerParams(dimension_semantics=("arbitrary",) * len(grid), vmem_limit_bytes=VMEM_LIMIT,
                                      has_side_effects=True)
    else:
        params = _cparams(*sem)
    res = pl.pallas_call(body, name=name, grid=grid, in_specs=in_specs, out_specs=out_specs, out_shape=out_shape,
                         scratch_shapes=scratch, compiler_params=params)(*args)
    return res[:n_out], res[n_out:]


def _attn_fwd(q, k, v, *, name, side=None):
    h, t, dq = q.shape
    dv = v.shape[2]
    tb = min(ATT_BLOCK, t)
    nblk = t // tb
    nt = (((1,), (1,)), ((), ()))

    def compute(refs):
        q_ref, k_ref, v_ref, o_ref, lse_ref = refs
        i = pl.program_id(1)
        qb = q_ref[...]

        def kv_step(j, carry, masked):
            m, l, acc = carry
            r0 = pl.multiple_of(j * tb, tb)
            s = lax.dot_general(qb, k_ref[pl.ds(r0, tb), :], nt, preferred_element_type=F32) * ATT_SCALE
            if masked:
                s = jnp.where(_att_mask(i, j, tb), s, NEG_INF)
            m_new = jnp.maximum(m, jnp.max(s, axis=1, keepdims=True))
            alpha = jnp.exp(m - m_new)
            p = jnp.exp(s - m_new)
            l = alpha * l + jnp.sum(p, axis=1, keepdims=True)
            acc = alpha * acc + jnp.dot(p.astype(BF16), v_ref[pl.ds(r0, tb), :], preferred_element_type=F32)
            return m_new, l, acc

        init = (jnp.full((tb, 1), NEG_INF, F32), jnp.zeros((tb, 1), F32), jnp.zeros((tb, dv), F32))
        carry = lax.fori_loop(0, i, functools.partial(kv_step, masked=False), init)
        m, l, acc = kv_step(i, carry, True)
        o_ref[...] = acc / l
        lse_ref[...] = m + jnp.log(l)

    def body(*refs):
        first = (pl.program_id(0) == 0) & (pl.program_id(1) == 0)
        last = (pl.program_id(0) == h - 1) & (pl.program_id(1) == nblk - 1)
        _with_side(side, 3, 2, refs, first, last, compute)

    return _side_call(
        body, side, name=name, grid=(h, nblk),
        in_specs=[pl.BlockSpec((None, tb, dq), lambda hh, i: (hh, i, 0)),
                  pl.BlockSpec((None, t, dq), lambda hh, i: (hh, 0, 0)),
                  pl.BlockSpec((None, t, dv), lambda hh, i: (hh, 0, 0))],
        out_specs=[pl.BlockSpec((None, tb, dv), lambda hh, i: (hh, i, 0)),
                   pl.BlockSpec((None, tb, 1), lambda hh, i: (hh, i, 0))],
        out_shape=[jax.ShapeDtypeStruct((h, t, dv), F32), jax.ShapeDtypeStruct((h, t, 1), F32)],
        scratch=[], args=[q, k, v], sem=("parallel", "parallel"))


def _attn_bwd(q, k, v, o, lse, do, *, name, side=None):
    h, t, dq_w = q.shape
    dv_w = v.shape[2]
    tb = min(ATT_BLOCK, t)
    nblk = t // tb
    nt = (((1,), (1,)), ((), ()))
    tn = (((0,), (0,)), ((), ()))

    def compute(refs):
        q_ref, k_ref, v_ref, o_ref, lse_ref, do_ref, dq_ref, dk_ref, dv_ref, delta_ref = refs
        j = pl.program_id(1)

        @pl.when(j == 0)
        def _():
            dq_ref[...] = jnp.zeros_like(dq_ref)

            def dstep(i, c):
                r0 = pl.multiple_of(i * tb, tb)
                delta_ref[pl.ds(r0, tb), :] = jnp.sum(do_ref[pl.ds(r0, tb), :].astype(F32) * o_ref[pl.ds(r0, tb), :],
                                                      axis=1, keepdims=True)
                return c

            lax.fori_loop(0, nblk, dstep, 0)

        kb, vb = k_ref[...], v_ref[...]

        def q_step(i, carry, masked):
            dk, dv = carry
            r0 = pl.multiple_of(i * tb, tb)
            qb, dob = q_ref[pl.ds(r0, tb), :], do_ref[pl.ds(r0, tb), :]
            s = lax.dot_general(qb, kb, nt, preferred_element_type=F32) * ATT_SCALE
            if masked:
                s = jnp.where(_att_mask(i, j, tb), s, NEG_INF)
            p = jnp.exp(s - lse_ref[pl.ds(r0, tb), :])
            dv = dv + lax.dot_general(p.astype(BF16), dob, tn, preferred_element_type=F32)
            dp = lax.dot_general(dob, vb, nt, preferred_element_type=F32)
            ds = (p * (dp - delta_ref[pl.ds(r0, tb), :]) * ATT_SCALE).astype(BF16)
            dk = dk + lax.dot_general(ds, qb, tn, preferred_element_type=F32)
            dq_ref[pl.ds(r0, tb), :] += jnp.dot(ds, kb, preferred_element_type=F32)
            return dk, dv

        carry = q_step(j, (jnp.zeros((tb, dq_w), F32), jnp.zeros((tb, dv_w), F32)), True)
        dk, dv = lax.fori_loop(j + 1, nblk, functools.partial(q_step, masked=False), carry)
        dk_ref[...] = dk
        dv_ref[...] = dv

    def body(*refs):
        first = (pl.program_id(0) == 0) & (pl.program_id(1) == 0)
        last = (pl.program_id(0) == h - 1) & (pl.program_id(1) == nblk - 1)
        _with_side(side, 6, 3, refs, first, last, compute)

    whole = lambda w: pl.BlockSpec((None, t, w), lambda hh, j: (hh, 0, 0))
    blockj = lambda w: pl.BlockSpec((None, tb, w), lambda hh, j: (hh, j, 0))
    return _side_call(
        body, side, name=name, grid=(h, nblk),
        in_specs=[whole(dq_w), blockj(dq_w), blockj(dv_w), whole(dv_w), whole(1), whole(dv_w)],
        out_specs=[whole(dq_w), blockj(dq_w), blockj(dv_w)],
        out_shape=[jax.ShapeDtypeStruct((h, t, dq_w), F32), jax.ShapeDtypeStruct((h, t, dq_w), F32),
                   jax.ShapeDtypeStruct((h, t, dv_w), F32)],
        scratch=[pltpu.VMEM((t, 1), F32)], args=[q, k, v, o, lse, do], sem=("parallel", "arbitrary"))


class _Exchange:
    def __init__(self, xs, gather):
        self.xs, self.gather, self.n = list(xs), gather, len(xs)
        shapes = [tuple(x.shape) if gather else tuple(x.shape[1:]) for x in xs]
        self.out_shape = [jax.ShapeDtypeStruct((N_DEV,) + shp, x.dtype) for shp, x in zip(shapes, xs)]
        self.specs = [pl.BlockSpec(memory_space=pl.ANY)] * self.n
        self.scratch = [pltpu.SemaphoreType.DMA((self.n, N_DEV - 1)), pltpu.SemaphoreType.DMA((self.n, N_DEV - 1)),
                        pltpu.SemaphoreType.DMA((self.n,))]

    def copies(self, x_refs, y_refs, send_sems, recv_sems, local_sems):
        mx, my, mc = lax.axis_index("x"), lax.axis_index("y"), lax.axis_index("c")
        me = 4 * mx + 2 * my + mc
        out = [pltpu.make_async_copy(x_refs[i] if self.gather else x_refs[i].at[me], y_refs[i].at[me], local_sems.at[i])
               for i in range(self.n)]
        for k in range(1, N_DEV):
            px = 1 - mx if k & 4 else mx
            py = 1 - my if k & 2 else my
            pc = 1 - mc if k & 1 else mc
            for i in range(self.n):
                out.append(pltpu.make_async_remote_copy(
                    src_ref=x_refs[i] if self.gather else x_refs[i].at[4 * px + 2 * py + pc], dst_ref=y_refs[i].at[me],
                    send_sem=send_sems.at[i, k - 1], recv_sem=recv_sems.at[i, k - 1],
                    device_id=(px, py, pc), device_id_type=pl.DeviceIdType.MESH))
        return out


def _exchange(xs, *, gather, name):
    ex = _Exchange(xs, gather)
    n = ex.n

    def body(*refs):
        copies = ex.copies(refs[:n], refs[n:2 * n], *refs[2 * n:])
        for cp in copies:
            cp.start()
        for cp in copies:
            cp.wait()

    return pl.pallas_call(
        body, name=name, out_shape=ex.out_shape, in_specs=ex.specs, out_specs=ex.specs, scratch_shapes=ex.scratch,
        compiler_params=pltpu.CompilerParams(has_side_effects=True),
    )(*ex.xs)


def _adamw(w, gs, m, v, *, name, tm=256):
    nl = len(gs)
    parts = gs[0].ndim == 3
    c = w.shape[1]
    r = w.shape[0] // nl
    tm = _pick_rows(r, tm)
    nrow = r // tm

    def body(*refs):
        w_ref, g_refs, (m_ref, v_ref, go_ref, d_ref, mo_ref, vo_ref) = refs[0], refs[1:1 + nl], refs[1 + nl:]

        def update(g_ref):
            if parts:
                gv = g_ref[0].astype(F32)
                for k in range(1, N_DEV):
                    gv = gv + g_ref[k].astype(F32)
            else:
                gv = g_ref[...]
            mn = ADAM_B1 * m_ref[...] + (1.0 - ADAM_B1) * gv
            vn = ADAM_B2 * v_ref[...] + (1.0 - ADAM_B2) * jnp.square(gv)
            m_hat = mn / (1.0 - ADAM_B1 ** ADAM_STEP)
            v_hat = vn / (1.0 - ADAM_B2 ** ADAM_STEP)
            go_ref[...] = gv
            d_ref[...] = -ADAM_LR * (m_hat / (jnp.sqrt(v_hat) + ADAM_EPS) + ADAM_WD * w_ref[...])
            mo_ref[...] = mn
            vo_ref[...] = vn

        if nl == 1:
            update(g_refs[0])
        else:
            for layer, g_ref in enumerate(g_refs):
                pl.when(pl.program_id(0) == layer)(functools.partial(update, g_ref))

    spec = pl.BlockSpec((tm, c), lambda l, i: (l * nrow + i, 0))

    def gspec(layer):
        row = lambda l, i: jnp.where(l == layer, i, 0)
        if parts:
            return pl.BlockSpec((N_DEV, tm, c), lambda l, i: (0, row(l, i), 0))
        return pl.BlockSpec((tm, c), lambda l, i: (row(l, i), 0))

    return pl.pallas_call(
        body, name=name, grid=(nl, nrow), in_specs=[spec] + [gspec(k) for k in range(nl)] + [spec, spec],
        out_specs=[spec] * 4, out_shape=[jax.ShapeDtypeStruct(w.shape, F32)] * 4,
        compiler_params=_cparams("arbitrary", "arbitrary"),
    )(w, *gs, m, v)


def _permute_in(a):
    pad = jnp.zeros(a.shape[:-1] + (IN_PAD - IN_WIDTH,), a.dtype)
    return jnp.concatenate([a[..., lo:hi] for lo, hi in _IN_SEGMENTS] + [pad], axis=-1)


def _unpermute_in(a):
    out, pos = {}, 0
    for lo, hi in _IN_SEGMENTS:
        out[lo] = a[..., pos:pos + hi - lo]
        pos += hi - lo
    return jnp.concatenate([out[lo] for lo in sorted(out)], axis=-1)


def _full_from_gathered(g, kind):
    if kind == 'row':
        return g.reshape((-1,) + g.shape[2:])
    return jnp.concatenate([g[d] for d in range(N_DEV)], axis=-1)


def _contrib_from_full(g, kind):
    if kind == 'row':
        return g.reshape((N_DEV, -1) + g.shape[1:])
    ns = g.shape[-1] // N_DEV
    return jnp.stack([g[..., d * ns:(d + 1) * ns] for d in range(N_DEV)])


def _in_runs(ns):
    runs, pos = [], 0
    for lo, hi in _IN_SEGMENTS:
        for d in range(lo // ns, (hi - 1) // ns + 1):
            a, b = max(lo, d * ns), min(hi, (d + 1) * ns)
            runs.append((d, a - d * ns, b - d * ns, pos))
            pos += b - a
    return runs


def _w_in_from_gathered(g):
    pieces = [g[d][:, a:b] for d, a, b, _ in _in_runs(g.shape[2])]
    pad = jnp.zeros((g.shape[1], IN_PAD - IN_WIDTH), g.dtype)
    return jnp.concatenate(pieces + [pad], axis=1)


def _w_in_contrib(gp):
    ns = IN_WIDTH // N_DEV
    per_dev = [[] for _ in range(N_DEV)]
    for d, a, b, pos in sorted(_in_runs(ns), key=lambda r: (r[0], r[1])):
        per_dev[d].append(gp[:, pos:pos + b - a])
    return jnp.stack([jnp.concatenate(p, axis=1) for p in per_dev])


def _as2d(a, lead=0):
    return a.reshape(a.shape[:lead] + (-1, a.shape[-1]))


def _chan_cols(re, im):
    lead = re.shape[:-1]
    nb = S5_CH // SCAN_LANES
    return jnp.stack([re.reshape(lead + (nb, SCAN_LANES)), im.reshape(lead + (nb, SCAN_LANES))],
                     axis=-2).reshape(lead + (2 * S5_CH,))


def _s5_tables(lam_re, lam_im, log_dt, b_re, b_im, c_re, c_im):
    dt = jnp.exp(log_dt)[:, None]
    mag = jnp.exp(lam_re * dt)
    a_re = mag * jnp.cos(lam_im * dt)
    a_im = mag * jnp.sin(lam_im * dt)
    den = lam_re * lam_re + lam_im * lam_im
    f_re = ((a_re - 1.0) * lam_re + a_im * lam_im) / den
    f_im = (a_im * lam_re - (a_re - 1.0) * lam_im) / den
    bb_re = f_re[..., None] * b_re - f_im[..., None] * b_im
    bb_im = f_re[..., None] * b_im + f_im[..., None] * b_re
    eye = jnp.eye(S5_GROUPS, dtype=F32)
    wb_re = jnp.einsum('gpc,gh->gchp', bb_re, eye).reshape(S5_WIDTH, S5_CH)
    wb_im = jnp.einsum('gpc,gh->gchp', bb_im, eye).reshape(S5_WIDTH, S5_CH)
    wb = _chan_cols(wb_re, wb_im)
    wc_re = jnp.einsum('gcp,gh->hcgp', c_re, eye).reshape(S5_WIDTH, S5_CH)
    wc_im = jnp.einsum('gcp,gh->hcgp', c_im, eye).reshape(S5_WIDTH, S5_CH)
    wc = _chan_cols(wc_re, -wc_im).T
    a_row = _chan_cols(a_re.reshape(1, S5_CH), a_im.reshape(1, S5_CH))
    ku, kc = S5_WIDTH // S5_BD, 2 * S5_CH // S5_BD
    wb_c = jnp.stack([wb[k * ku:(k + 1) * ku, k * kc:(k + 1) * kc] for k in range(S5_BD)])
    wc_c = jnp.stack([wc[k * kc:(k + 1) * kc, k * ku:(k + 1) * ku] for k in range(S5_BD)])
    return wb_c, wc_c, a_row


def _scan_tables(a_row, conj):
    nb = S5_CH // SCAN_LANES
    a = a_row.reshape(nb, 2, SCAN_LANES)
    ar, ai = a[:, 0], (-a[:, 1] if conj else a[:, 1])
    pr, pi = [ar], [ai]
    for _ in range(V7X_SUBLANES - 1):
        pr, pi = pr + [pr[-1] * ar - pi[-1] * ai], pi + [pr[-1] * ai + pi[-1] * ar]
    order = range(V7X_SUBLANES - 1, -1, -1) if conj else range(V7X_SUBLANES)
    carry = jnp.stack([jnp.stack([pr[r], pi[r]], axis=1).reshape(-1) for r in order])
    zero = jnp.zeros_like(carry[0])
    step = jnp.stack([jnp.stack([pr[r], pi[r]], axis=1).reshape(-1) for r in (0, 1, 3)] + [zero] * 5)
    return carry, step


def _rope_tables(t):
    half = QK_ROPE // 2
    inv_freq = 1.0 / (ROPE_THETA ** (jnp.arange(0, QK_ROPE, 2, dtype=F32) / QK_ROPE))
    ang = jnp.arange(t, dtype=F32)[:, None] * inv_freq[None, :]
    cos, sin = jnp.cos(ang), jnp.sin(ang)
    zero = jnp.zeros_like(sin)

    def lay(nope, width, first, second):
        head = jnp.concatenate([jnp.full((t, nope), 1.0 if first is cos else 0.0, F32), first, second], axis=1)
        reps = width // head.shape[1]
        out = jnp.tile(head, (1, reps))
        return jnp.pad(out, ((0, 0), (0, width - out.shape[1])))

    hq = MLA_HEADS * (QK_NOPE + QK_ROPE)
    q_tabs = (lay(QK_NOPE, hq, cos, cos), lay(QK_NOPE, hq, -sin, zero), lay(QK_NOPE, hq, zero, sin))
    k_tabs = (lay(0, V7X_LANES, cos, cos)[:, :V7X_LANES] * (jnp.arange(V7X_LANES) < QK_ROPE),
              lay(0, V7X_LANES, -sin, zero) * (jnp.arange(V7X_LANES) < QK_ROPE),
              lay(0, V7X_LANES, zero, sin) * (jnp.arange(V7X_LANES) < QK_ROPE))
    return q_tabs, k_tabs


def _sgu_tables(w_s, b_s):
    pos = jnp.arange(SGU_CHUNK) // CHUNK
    mask = pos[None, :] <= pos[:, None]
    wm = jnp.where(mask[None], w_s, 0.0).reshape(SGU_GROUPS * SGU_CHUNK, SGU_CHUNK)
    bias = jnp.repeat(b_s.T, SGU_WIDTH // SGU_GROUPS, axis=1)
    return wm, bias


def _row(v):
    return v.reshape(1, -1)


_S5_PARAMS = ('s5_lambda_re', 's5_lambda_im', 's5_log_dt', 's5_b_re', 's5_b_im', 's5_c_re', 's5_c_im')


def _derived_tables(p):
    (wb, wc, a_row), s5_pull = jax.vjp(jax.vmap(_s5_tables), *[p[n] for n in _S5_PARAMS])
    (wm, bias), sgu_pull = jax.vjp(jax.vmap(_sgu_tables), p['sgu_w_s'], p['sgu_b_s'])
    tab_f = jax.vmap(lambda a: _scan_tables(a, False))(a_row)
    tab_b = jax.vmap(lambda a: _scan_tables(a, True))(a_row)
    wb, wc = wb.astype(BF16), wc.astype(BF16)
    per_layer = [dict(s5_wb=wb[l], s5_wc=wc[l], s5_tab_fwd=(tab_f[0][l], tab_f[1][l]),
                      s5_tab_bwd=(tab_b[0][l], tab_b[1][l]), sgu_wm=wm[l], sgu_bias=bias[l]) for l in range(len(wm))]

    def pull(grads):
        stacked = lambda k: jnp.stack([g[k] for g in grads])
        out = dict(zip(_S5_PARAMS, s5_pull((stacked('s5_wb'), stacked('s5_wc'), stacked('s5_a')))))
        out['sgu_w_s'], out['sgu_b_s'] = sgu_pull((stacked('sgu_wm'), stacked('sgu_bias')))
        return out

    return per_layer, pull


def _layer_fwd(x, ada, w, rope_tabs, tag, side=None, after_attn=None):
    s = {'x': x}
    q_tabs, k_tabs = rope_tabs
    sc1, gt1, sc2, gt2 = _row(1.0 + ada[1]), _row(1.0 + ada[2]), _row(1.0 + ada[4]), _row(1.0 + ada[5])
    s.update(sc1=sc1, gt1=gt1, sc2=sc2, gt2=gt2)
    (h,) = _rowcall(_modulate_fn, [x], [sc1, _row(ada[0])], [(D_MODEL, BF16)], tm=512, name=f"mod1_{tag}")
    proj = _mm(h, w['w_in_p'], bias=w['b_in_p'], name=f"proj_{tag}", tn=1792)
    s.update(h=h, proj=proj)

    u_s5 = proj[:, P_S5:P_S5 + S5_WIDTH]
    bu = _mm_bd(u_s5, w['s5_wb'], 'nn', name=f"s5_bu_{tag}")
    hs = _s5_scan(bu, *w['s5_tab_fwd'], reverse=False, name=f"s5_scan_{tag}")
    ylin = _mm_bd(hs, w['s5_wc'], 'nn', name=f"s5_c_{tag}")
    s5_full = [_row(w['s5_d']), w['s5_w_glu'], _row(w['s5_b_glu'])]
    (y_s5,) = _rowcall(_s5_post_fn, [ylin, u_s5], s5_full, [(S5_WIDTH, BF16)], tm=256, name=f"s5_post_{tag}")
    s.update(u_s5=u_s5, hs=hs, ylin=ylin, y_s5=y_s5)

    mla_rows = [(proj, MLA_BLK, P_MLA // MLA_BLK), *q_tabs, *k_tabs]
    mla_full = [_row(w['mla_q_norm']), w['mla_w_q_up'], _row(w['mla_kv_norm']), w['mla_w_kv_up']]
    hq, hkv = MLA_HEADS * (QK_NOPE + QK_ROPE), MLA_HEADS * (QK_NOPE + V_HEAD)
    q_r, kv, kpe_r = _rowcall(_mla_pre_fn, mla_rows, mla_full, [(hq, BF16), (hkv, BF16), (V7X_LANES, BF16)],
                              tm=256, name=f"mla_pre_{tag}")
    t = x.shape[0]
    qh = q_r.reshape(t, MLA_HEADS, -1).transpose(1, 0, 2)
    kv3 = kv.reshape(t, MLA_HEADS, -1).transpose(1, 0, 2)
    kh = jnp.concatenate([kv3[:, :, :QK_NOPE], jnp.broadcast_to(kpe_r[None, :, :QK_ROPE], (MLA_HEADS, t, QK_ROPE))], axis=2)
    vh = kv3[:, :, QK_NOPE:]
    (o, lse), side_out = _attn_fwd(qh, kh, vh, name=f"attn_fwd_{tag}", side=side)
    if after_attn is not None:
        after_attn(side_out)
    y_mla = o.transpose(1, 0, 2).reshape(t, -1).astype(BF16)
    s.update(qh=qh, kh=kh, vh=vh, o=o, lse=lse, y_mla=y_mla)

    sgu_rows = [(proj, SGU_WIDTH, P_USGU // SGU_WIDTH), (proj, SGU_WIDTH, P_VSGU // SGU_WIDTH)]
    sgu_full = [_row(w['sgu_ln_g']), _row(w['sgu_ln_b']), w['sgu_wm'], w['sgu_bias']]
    (y_sgu,) = _rowcall(_sgu_fn, sgu_rows, sgu_full, [(SGU_WIDTH, BF16)], tm=SGU_CHUNK, name=f"sgu_{tag}")
    s.update(sgu_full=sgu_full, y_sgu=y_sgu)

    wbr = w['w_branch'].reshape(-1, D_MODEL)
    gate_rows = [(proj, D_MODEL, b) for b in range(3)]
    (merged,) = _rowcall(_merge_fn, [y_s5, y_mla, y_sgu] + gate_rows, [wbr], [(D_MODEL, BF16)], tm=256, name=f"merge_{tag}")
    ymix = _mm(merged, w['w_out'], name=f"wout_{tag}")
    (x1,) = _rowcall(_ln_res_fn, [x, ymix], [gt1, _row(w['ln1_g']), _row(w['ln1_b'])], [(D_MODEL, F32)], tm=256,
                     name=f"ln1_{tag}")
    s.update(merged=merged, ymix=ymix, x1=x1)

    (h2,) = _rowcall(_modulate_fn, [x1], [sc2, _row(ada[3])], [(D_MODEL, BF16)], tm=512, name=f"mod2_{tag}")
    ab = _mm(h2, w['ffn_w_in'], name=f"ffn_in_{tag}", tn=1408)
    (act,) = _rowcall(_swiglu_fn, [(ab, FF_HIDDEN, 0), (ab, FF_HIDDEN, 1)], [], [(FF_HIDDEN, BF16)], tm=256,
                      name=f"swiglu_{tag}")
    f = _mm(act, w['ffn_w_out'], name=f"ffn_out_{tag}", tk=2816)
    (x2,) = _rowcall(_ln_res_fn, [x1, f], [gt2, _row(w['ln2_g']), _row(w['ln2_b'])], [(D_MODEL, F32)], tm=256,
                     name=f"ln2_{tag}")
    s.update(h2=h2, ab=ab, act=act, f=f)
    return x2, s, side_out


def _mod_bwd_fn(x, dh, dxa, scale_row):
    return (dxa + dh * scale_row, jnp.sum(dh * x, axis=0, keepdims=True), jnp.sum(dh, axis=0, keepdims=True))


def _layer_bwd(dx2, s, w, rope_tabs, tag, make_side=None):
    g = {}
    q_tabs, k_tabs = rope_tabs
    t = dx2.shape[0]
    ln_full = lambda gt, a, b: [gt, _row(w[a]), _row(w[b])]

    dx1_a, df, dgt2, g['ln2_g'], g['ln2_b'] = _rowcall_vjp(
        _ln_res_fn, [s['x1'], s['f']], ln_full(s['gt2'], 'ln2_g', 'ln2_b'), [dx2], [0, 1], [0, 1, 2],
        tm=256, name=f"ln2_bwd_{tag}", row_dtypes=[F32, BF16])
    dact = _mm(df, w['ffn_w_out'], tb=True, name=f"ffn_out_dx_{tag}", tn=1408)
    g['ffn_w_out'] = _mm(s['act'], df, ta=True, name=f"ffn_out_dw_{tag}", tm=1408)
    da_, db_ = _rowcall_vjp(_swiglu_fn, [(s['ab'], FF_HIDDEN, 0), (s['ab'], FF_HIDDEN, 1)], [], [dact], [0, 1], [],
                            tm=256, name=f"swiglu_bwd_{tag}", row_dtypes=[BF16, BF16])
    dab = jnp.concatenate([da_, db_], axis=1)
    dh2 = _mm(dab, w['ffn_w_in'], tb=True, name=f"ffn_in_dx_{tag}", tk=1408)
    g['ffn_w_in'] = _mm(s['h2'], dab, ta=True, name=f"ffn_in_dw_{tag}", tn=1408)
    dx1, dsc2, dsh2 = _rowcall(_mod_bwd_fn, [s['x1'], dh2, dx1_a], [s['sc2']], [(D_MODEL, F32)],
                               [((1, D_MODEL), F32)] * 2, tm=256, name=f"mod2_bwd_{tag}")

    dx_a, dymix, dgt1, g['ln1_g'], g['ln1_b'] = _rowcall_vjp(
        _ln_res_fn, [s['x'], s['ymix']], ln_full(s['gt1'], 'ln1_g', 'ln1_b'), [dx1], [0, 1], [0, 1, 2],
        tm=256, name=f"ln1_bwd_{tag}", row_dtypes=[F32, BF16])
    dmerged = _mm(dymix, w['w_out'], tb=True, name=f"wout_dx_{tag}")
    g['w_out'] = _mm(s['merged'], dymix, ta=True, name=f"wout_dw_{tag}")

    proj = s['proj']
    wbr = w['w_branch'].reshape(-1, D_MODEL)
    gate_rows = [(proj, D_MODEL, b) for b in range(3)]
    dy_s5, dy_mla, dy_sgu, dl0, dl1, dl2, dwbr = _rowcall_vjp(
        _merge_fn, [s['y_s5'], s['y_mla'], s['y_sgu']] + gate_rows, [wbr], [dmerged], [0, 1, 2, 3, 4, 5], [0],
        tm=256, name=f"merge_bwd_{tag}")
    g['w_branch'] = dwbr.reshape(w['w_branch'].shape)

    sgu_rows = [(proj, SGU_WIDTH, P_USGU // SGU_WIDTH), (proj, SGU_WIDTH, P_VSGU // SGU_WIDTH)]
    du_sgu, dv_sgu, dlg, dlb, dwm, dbias = _rowcall_vjp(
        _sgu_fn, sgu_rows, s['sgu_full'], [dy_sgu], [0, 1], [0, 1, 2, 3], tm=SGU_CHUNK, name=f"sgu_bwd_{tag}")
    g['sgu_ln_g'], g['sgu_ln_b'] = dlg.reshape(-1), dlb.reshape(-1)
    g['sgu_wm'], g['sgu_bias'] = dwm, dbias

    do = dy_mla.reshape(t, MLA_HEADS, V_HEAD).transpose(1, 0, 2).astype(BF16)
    (dqh, dkh, dvh), side_out = _attn_bwd(s['qh'], s['kh'], s['vh'], s['o'], s['lse'], do, name=f"attn_bwd_{tag}",
                                          side=make_side(g) if make_side is not None else None)
    dq_r = dqh.transpose(1, 0, 2).reshape(t, -1)
    dkv = jnp.concatenate([dkh[:, :, :QK_NOPE], dvh], axis=2).transpose(1, 0, 2).reshape(t, -1)
    dkpe = jnp.pad(jnp.sum(dkh[:, :, QK_NOPE:], axis=0), ((0, 0), (0, V7X_LANES - QK_ROPE)))
    mla_rows = [(proj, MLA_BLK, P_MLA // MLA_BLK), *q_tabs, *k_tabs]
    mla_full = [_row(w['mla_q_norm']), w['mla_w_q_up'], _row(w['mla_kv_norm']), w['mla_w_kv_up']]
    dmla, dqn, g['mla_w_q_up'], dkvn, g['mla_w_kv_up'] = _rowcall_vjp(
        _mla_pre_fn, mla_rows, mla_full, [dq_r, dkv, dkpe], [0], [0, 1, 2, 3], tm=256, name=f"mla_pre_bwd_{tag}")
    g['mla_q_norm'], g['mla_kv_norm'] = dqn.reshape(-1), dkvn.reshape(-1)

    s5_full = [_row(w['s5_d']), w['s5_w_glu'], _row(w['s5_b_glu'])]
    dylin, du_a, dd, g['s5_w_glu'], dbg = _rowcall_vjp(
        _s5_post_fn, [s['ylin'], s['u_s5']], s5_full, [dy_s5], [0, 1], [0, 1, 2], tm=256, name=f"s5_post_bwd_{tag}",
        row_dtypes=[BF16, F32])
    g['s5_d'], g['s5_b_glu'] = dd.reshape(-1), dbg.reshape(-1)
    dhs = _mm_bd(dylin, w['s5_wc'], 'nt', name=f"s5_c_dx_{tag}")
    g['s5_wc'] = _mm_bd(s['hs'], dylin, 'tn', name=f"s5_c_dw_{tag}")
    gs, da_part = _s5_scan(dhs, *w['s5_tab_bwd'], reverse=True, hist=s['hs'], name=f"s5_scan_bwd_{tag}")
    du_b = _mm_bd(gs, w['s5_wb'], 'nt', name=f"s5_bu_dx_{tag}")
    g['s5_wb'] = _mm_bd(s['u_s5'], gs, 'tn', name=f"s5_bu_dw_{tag}")
    g['s5_a'] = jnp.sum(da_part, axis=0, keepdims=True)

    dproj_f32 = jnp.concatenate([dl0, dl1, dl2, du_a + du_b, du_sgu, dv_sgu, dmla], axis=1)
    dproj, db_in = _rowcall(lambda d: (d, jnp.sum(d, axis=0, keepdims=True)), [dproj_f32], [], [(IN_PAD, BF16)],
                            [((1, IN_PAD), F32)], tm=256, name=f"dproj_{tag}")
    dh = _mm(dproj, w['w_in_p'], tb=True, name=f"proj_dx_{tag}", tk=1792)
    g['w_in_p'] = _mm(s['h'], dproj, ta=True, name=f"proj_dw_{tag}", tn=896)
    g['b_in'] = _unpermute_in(db_in).reshape(-1)
    dx, dsc1, dsh1 = _rowcall(_mod_bwd_fn, [s['x'], dh, dx_a], [s['sc1']], [(D_MODEL, F32)], [((1, D_MODEL), F32)] * 2,
                              tm=256, name=f"mod1_bwd_{tag}")
    d_ada = jnp.concatenate([dsh1, dsc1, dgt1, dsh2, dsc2, dgt2], axis=0)
    return dx, d_ada, g, side_out


def _loss_fn(y, target):
    err = y - target
    return (err / D_MODEL, 0.5 * jnp.sum(jnp.sum(err * err, axis=1, keepdims=True), axis=0, keepdims=True) / D_MODEL)


def _step(p):
    me = 4 * lax.axis_index("x") + 2 * lax.axis_index("y") + lax.axis_index("c")
    x = p['x'][0]
    t = x.shape[0]
    rope_tabs = _rope_tables(t)

    (c_all,) = _exchange([jnp.broadcast_to(p['c'], (V7X_SUBLANES, D_MODEL))], gather=True, name="gather_c")
    c_all = c_all[:, 0, :]
    (c_act,) = _rowcall(lambda cc: (cc * _sigmoid(cc),), [c_all], [], [(D_MODEL, F32)], tm=N_DEV, name="c_silu")
    ncol = p['w_ada'].shape[2]
    b_ada_loc = lax.dynamic_slice_in_dim(p['b_ada'], me * ncol, ncol, axis=1)
    ada_cols = jnp.concatenate([_mm(c_act, p['w_ada'][l], bias=b_ada_loc[l:l + 1], name=f"ada_{l}") for l in range(DEPTH)])
    (ada_all,) = _exchange([ada_cols], gather=True, name="gather_ada")
    ada_all = ada_all.reshape(N_DEV, DEPTH, N_DEV, ncol)
    ada = lax.dynamic_index_in_dim(ada_all, me, axis=2, keepdims=False)
    ada = ada.transpose(1, 0, 2).reshape(DEPTH, 6, D_MODEL)

    mixer_w, ffn_w = SHARDED[:-2], SHARDED[-2:]

    def shards(l, group):
        return [p[n][l].astype(BF16) for n, _ in group]

    def contribs(g, group):
        return [(_w_in_contrib(g['w_in_p']) if n == 'w_in' else _contrib_from_full(g[n], kind)).astype(BF16)
                for n, kind in group]

    tables, pull_tables = _derived_tables(p)

    def mixer_weights(l, gathered):
        w = {n: _full_from_gathered(g, kind) for (n, kind), g in zip(mixer_w[1:], gathered[1:])}
        w['w_in_p'] = _w_in_from_gathered(gathered[0])
        w['b_in_p'] = _row(_permute_in(p['b_in'][l]))
        w.update(tables[l])
        for n in SMALL:
            if n != 'b_ada' and n != 'b_in':
                w[n] = p[n][l]
        return w

    saved, layers = [], []
    gathered = _exchange(shards(0, mixer_w), gather=True, name="gather_w_0")
    for l in range(DEPTH):
        w = mixer_weights(l, gathered)
        side = _Exchange(shards(l, ffn_w) + (shards(l + 1, mixer_w) if l + 1 < DEPTH else []), True)

        def add_ffn(res, w=w):
            for (n, kind), g in zip(ffn_w, res):
                w[n] = _full_from_gathered(g, kind)

        x, s, res = _layer_fwd(x, ada[l], w, rope_tabs, f"l{l}", side, add_ffn)
        gathered = res[len(ffn_w):]
        layers.append(w)
        saved.append(s)
    dy, loss_loc = _rowcall(_loss_fn, [x, p['loss_target'][0]], [], [(D_MODEL, F32)], [((1, 1), F32)], tm=256, name="loss")
    loss = lax.psum(loss_loc[0, 0], ("x", "y", "c"))

    d_ada, grads, landed = [None] * DEPTH, [None] * DEPTH, [[None, None] for _ in range(DEPTH)]
    dx, pending = dy, []
    for l in reversed(range(DEPTH)):
        make_side = lambda g, pending=pending: _Exchange(contribs(g, ffn_w) + pending, False)
        dx, d_ada[l], grads[l], res = _layer_bwd(dx, saved[l], layers[l], rope_tabs, f"l{l}", make_side)
        landed[l][1] = res[:len(ffn_w)]
        if l + 1 < DEPTH:
            landed[l + 1][0] = res[len(ffn_w):]
        pending = contribs(grads[l], mixer_w)
    landed[0][0] = _exchange(pending, gather=False, name="scatter_g_0")
    landed = [list(a) + list(b) for a, b in landed]
    d_ada = jnp.stack(d_ada).reshape(DEPTH, 6 * D_MODEL)

    (d_ada_all,) = _exchange([d_ada], gather=True, name="gather_dada")
    d_ada_cols = lax.dynamic_slice_in_dim(d_ada_all, me * ncol, ncol, axis=2)
    pad_b = ((0, V7X_LANES - N_DEV), (0, 0))
    c_act_p = jnp.pad(c_act, pad_b)
    g_w_ada = jnp.stack([_mm(c_act_p, jnp.pad(d_ada_cols[:, l], pad_b), ta=True, name=f"ada_dw_{l}") for l in range(DEPTH)])

    out = {}
    kinds = ('grad_', 'delta_', 'new_m_', 'new_v_')
    for i, (n, _) in enumerate(SHARDED):
        res = _adamw(_as2d(p[n]), [_as2d(landed[l][i], lead=1) for l in range(DEPTH)], _as2d(p['m_' + n]),
                     _as2d(p['v_' + n]), name=f"adamw_{n}")
        for kind, r in zip(kinds, res):
            out[kind + n] = r.reshape(p[n].shape)

    res = _adamw(_as2d(p['w_ada']), [_as2d(g_w_ada)], _as2d(p['m_w_ada']), _as2d(p['v_w_ada']), name="adamw_ada")
    for kind, r in zip(kinds, res):
        out[kind + 'w_ada'] = r.reshape(p['w_ada'].shape)

    small_g = pull_tables(grads)
    small_g.update({n: jnp.stack([grads[l][n] for l in range(DEPTH)]) for n in SMALL if n not in small_g and n != 'b_ada'})
    small_g['b_ada'] = d_ada
    n_small = sum(int(np.prod(p[n].shape)) for n in SMALL)
    n_pad = -(-n_small // SMALL_PAD) * SMALL_PAD
    chunk_rows = n_pad // N_DEV // 1024

    def flat_small(get):
        flat = jnp.concatenate([get(n).reshape(-1) for n in SMALL])
        return jnp.pad(flat, (0, n_pad - n_small)).reshape(N_DEV, chunk_rows, 1024)

    (landed_small,) = _exchange([flat_small(lambda n: small_g[n])], gather=False, name="scatter_small")
    mine = lambda pre: lax.dynamic_index_in_dim(flat_small(lambda n: p[pre + n]), me, axis=0, keepdims=False)
    res = _adamw(mine(''), [landed_small], mine('m_'), mine('v_'), name="adamw_small", tm=56)
    (res_all,) = _exchange([jnp.concatenate(res, axis=0)], gather=True, name="gather_small")
    res_all = res_all.reshape(N_DEV, 4, chunk_rows * 1024).transpose(1, 0, 2).reshape(4, n_pad)
    off = 0
    for n in SMALL:
        size = int(np.prod(p[n].shape))
        for k, kind in enumerate(('grad_', 'delta_', 'new_m_', 'new_v_')):
            out[kind + n] = res_all[k, off:off + size].reshape(p[n].shape)
        off += size

    outs = [loss, dx[None]]
    for kind in ('grad_', 'delta_', 'new_m_', 'new_v_'):
        outs += [out[kind + n] for n in WNAMES]
    return tuple(outs)


def kernel(x, c, w_ada, b_ada, w_in, b_in, s5_lambda_re, s5_lambda_im, s5_log_dt, s5_b_re, s5_b_im, s5_c_re, s5_c_im, s5_d, s5_w_glu, s5_b_glu, mla_q_norm, mla_w_q_up, mla_kv_norm, mla_w_kv_up, sgu_ln_g, sgu_ln_b, sgu_w_s, sgu_b_s, w_branch, w_out, ln1_g, ln1_b, ffn_w_in, ffn_w_out, ln2_g, ln2_b, loss_target, m_w_ada, m_b_ada, m_w_in, m_b_in, m_s5_lambda_re, m_s5_lambda_im, m_s5_log_dt, m_s5_b_re, m_s5_b_im, m_s5_c_re, m_s5_c_im, m_s5_d, m_s5_w_glu, m_s5_b_glu, m_mla_q_norm, m_mla_w_q_up, m_mla_kv_norm, m_mla_w_kv_up, m_sgu_ln_g, m_sgu_ln_b, m_sgu_w_s, m_sgu_b_s, m_w_branch, m_w_out, m_ln1_g, m_ln1_b, m_ffn_w_in, m_ffn_w_out, m_ln2_g, m_ln2_b, v_w_ada, v_b_ada, v_w_in, v_b_in, v_s5_lambda_re, v_s5_lambda_im, v_s5_log_dt, v_s5_b_re, v_s5_b_im, v_s5_c_re, v_s5_c_im, v_s5_d, v_s5_w_glu, v_s5_b_glu, v_mla_q_norm, v_mla_w_q_up, v_mla_kv_norm, v_mla_w_kv_up, v_sgu_ln_g, v_sgu_ln_b, v_sgu_w_s, v_sgu_b_s, v_w_branch, v_w_out, v_ln1_g, v_ln1_b, v_ffn_w_in, v_ffn_w_out, v_ln2_g, v_ln2_b):
    return _step(dict(locals()))
```

```python
import functools
import math

import numpy as np
import jax
import jax.numpy as jnp
from jax import lax
from jax.experimental import pallas as pl
from jax.experimental.pallas import tpu as pltpu

F32 = jnp.float32
BF16 = jnp.bfloat16

N_DEV = 8
D_MODEL = 1024
DEPTH = 4
CHUNK = 64
S5_WIDTH = 512
S5_GROUP = 16
S5_GROUPS = 32
S5_STATE = 64
MLA_HEADS = 8
QK_NOPE = 64
QK_ROPE = 32
V_HEAD = 64
Q_LORA = 384
KV_LORA = 256
ROPE_THETA = 10000.0
SGU_WIDTH = 512
SGU_GROUPS = 4
SGU_CHUNK = 128
FF_HIDDEN = 2816
DEEPNORM_ALPHA = (2 * DEPTH) ** 0.25
LN_EPS = 1e-5
RMS_EPS = 1e-6
NEG_INF = -1e30
ADAM_LR = 0.001
ADAM_B1 = 0.9
ADAM_B2 = 0.999
ADAM_EPS = 1e-08
ADAM_WD = 0.01
ADAM_STEP = 10

IN_WIDTH = 5280
IN_PAD = 5376
_O_S5, _O_CQ, _O_CKV, _O_KPE, _O_USGU, _O_VSGU, _O_GATE = 0, 512, 896, 1152, 1184, 1696, 2208
_IN_SEGMENTS = ((_O_GATE, IN_WIDTH), (_O_S5, _O_CQ), (_O_USGU, _O_VSGU), (_O_VSGU, _O_GATE), (_O_CQ, _O_USGU))
P_GATE, P_S5, P_USGU, P_VSGU, P_MLA = 0, 3072, 3584, 4096, 4608
MLA_BLK = 768

V7X_LANES = 128
V7X_SUBLANES = 8
VMEM_LIMIT = 48 * 1024 * 1024
ATT_BLOCK = 512
SCAN_LANES = 128
S5_CH = S5_GROUPS * S5_STATE
S5_BD = 4

WNAMES = ['w_ada', 'b_ada', 'w_in', 'b_in', 's5_lambda_re', 's5_lambda_im', 's5_log_dt', 's5_b_re', 's5_b_im',
          's5_c_re', 's5_c_im', 's5_d', 's5_w_glu', 's5_b_glu', 'mla_q_norm', 'mla_w_q_up', 'mla_kv_norm',
          'mla_w_kv_up', 'sgu_ln_g', 'sgu_ln_b', 'sgu_w_s', 'sgu_b_s', 'w_branch', 'w_out', 'ln1_g', 'ln1_b',
          'ffn_w_in', 'ffn_w_out', 'ln2_g', 'ln2_b']
SHARDED = (('w_in', 'col'), ('s5_w_glu', 'row'), ('mla_w_q_up', 'col'), ('mla_w_kv_up', 'col'),
           ('w_branch', 'col3'), ('w_out', 'row'), ('ffn_w_in', 'col'), ('ffn_w_out', 'row'))
SMALL = [n for n in WNAMES if n != 'w_ada' and n not in dict(SHARDED)]
SMALL_PAD = N_DEV * V7X_SUBLANES * 1024


def _cparams(*sem):
    return pltpu.CompilerParams(dimension_semantics=sem, vmem_limit_bytes=VMEM_LIMIT)


def _pick(n, target):
    if n <= target:
        return n
    best = None
    for d in range(V7X_LANES, target + 1, V7X_LANES):
        if n % d == 0:
            best = d
    assert best is not None, (n, target)
    return best


def _pick_rows(n, target):
    if n <= target:
        return n
    best = None
    for d in range(V7X_SUBLANES, target + 1, V7X_SUBLANES):
        if n % d == 0:
            best = d
    assert best is not None, (n, target)
    return best


@jax.custom_vjp
def _bdot(a, b):
    return jnp.dot(a.astype(BF16), b.astype(BF16), preferred_element_type=F32)


def _bdot_fwd(a, b):
    return _bdot(a, b), (a, b)


def _bdot_bwd(res, g):
    a, b = res
    gb = g.astype(BF16)
    da = lax.dot_general(gb, b.astype(BF16), (((1,), (1,)), ((), ())), preferred_element_type=F32)
    db = lax.dot_general(a.astype(BF16), gb, (((0,), (0,)), ((), ())), preferred_element_type=F32)
    return da.astype(a.dtype), db.astype(b.dtype)


_bdot.defvjp(_bdot_fwd, _bdot_bwd)


@functools.partial(jax.custom_vjp, nondiff_argnums=(1,))
def _lane_roll(x, shift):
    return pltpu.roll(x, shift % x.shape[1], 1)


def _lane_roll_fwd(x, shift):
    return _lane_roll(x, shift), None


def _lane_roll_bwd(shift, _, g):
    return (_lane_roll(g, -shift),)


_lane_roll.defvjp(_lane_roll_fwd, _lane_roll_bwd)


def _sigmoid(x):
    return 1.0 / (1.0 + jnp.exp(-x))


def _gelu(x):
    return 0.5 * x * (1.0 + jnp.tanh(math.sqrt(2.0 / math.pi) * (x + 0.044715 * (x * x * x))))


def _layer_norm(x, g, b):
    mu = jnp.mean(x, axis=-1, keepdims=True)
    var = jnp.mean(jnp.square(x - mu), axis=-1, keepdims=True)
    return (x - mu) * lax.rsqrt(var + LN_EPS) * g + b


def _rms_norm(x, g):
    return x * lax.rsqrt(jnp.mean(x * x, axis=-1, keepdims=True) + RMS_EPS) * g


def _rope(x, c, s1, s2):
    half = QK_ROPE // 2
    return x * c + _lane_roll(x, -half) * s1 + _lane_roll(x, half) * s2


def _modulate_fn(x, scale_row, shift_row):
    return (x * scale_row + shift_row,)


def _ln_res_fn(x, y, gate_row, g, b):
    return (_layer_norm(DEEPNORM_ALPHA * x + gate_row * y, g, b),)


def _s5_post_fn(ylin, u, d, w_glu, b_glu):
    z = _gelu(ylin + d * u)
    return (z * _sigmoid(_bdot(z, w_glu) + b_glu),)


def _mla_pre_fn(blk, cq_t, sq1, sq2, ck_t, sk1, sk2, q_norm, w_q, kv_norm, w_kv):
    cq, ckv, kpe = blk[:, :Q_LORA], blk[:, Q_LORA:Q_LORA + KV_LORA], blk[:, Q_LORA + KV_LORA:]
    q = _rope(_bdot(_rms_norm(cq, q_norm), w_q), cq_t, sq1, sq2)
    kv = _bdot(_rms_norm(ckv, kv_norm), w_kv)
    return q, kv, _rope(kpe, ck_t, sk1, sk2)


def _sgu_fn(u, v, g, b, wm, bias):
    vn = _layer_norm(_gelu(v), g, b)
    w = SGU_CHUNK
    parts = [_bdot(wm[k * w:(k + 1) * w, :], vn[:, k * w:(k + 1) * w]) for k in range(SGU_GROUPS)]
    return (_gelu(u) * (jnp.concatenate(parts, axis=1) + bias),)


def _merge_fn(y0, y1, y2, l0, l1, l2, wb):
    n = S5_WIDTH
    return (_sigmoid(l0) * _bdot(y0, wb[:n]) + _sigmoid(l1) * _bdot(y1, wb[n:2 * n])
            + _sigmoid(l2) * _bdot(y2, wb[2 * n:]),)


def _swiglu_fn(a, b):
    return (a * _sigmoid(a) * b,)


def _rowcall(fn, rows, fulls, out_rows, out_reds=(), *, tm, name):
    rows = [r if isinstance(r, tuple) else (r, r.shape[1], 0) for r in rows]
    t = rows[0][0].shape[0]
    tm = _pick_rows(t, tm)
    n_in, n_or, n_red = len(rows) + len(fulls), len(out_rows), len(out_reds)

    def body(*refs):
        vals = fn(*[r[...] for r in refs[:n_in]])
        assert len(vals) == n_or + n_red, (name, len(vals))
        for ref, v in zip(refs[n_in:n_in + n_or], vals[:n_or]):
            ref[...] = v.astype(ref.dtype)
        if n_red:
            red_refs = refs[n_in + n_or:]

            @pl.when(pl.program_id(0) == 0)
            def _():
                for ref in red_refs:
                    ref[...] = jnp.zeros_like(ref)

            for ref, v in zip(red_refs, vals[n_or:]):
                ref[...] += v.astype(ref.dtype)

    in_specs = [pl.BlockSpec((tm, w), functools.partial(lambda i, blk: (i, blk), blk=blk)) for _, w, blk in rows]
    in_specs += [pl.BlockSpec(f.shape, lambda i: (0, 0)) for f in fulls]
    out_specs = [pl.BlockSpec((tm, c), lambda i: (i, 0)) for c, _ in out_rows]
    out_specs += [pl.BlockSpec(s, lambda i: (0, 0)) for s, _ in out_reds]
    out_shape = [jax.ShapeDtypeStruct((t, c), dt) for c, dt in out_rows]
    out_shape += [jax.ShapeDtypeStruct(s, dt) for s, dt in out_reds]
    return pl.pallas_call(
        body, name=name, grid=(t // tm,), in_specs=in_specs, out_specs=out_specs, out_shape=out_shape,
        compiler_params=_cparams("arbitrary" if n_red else "parallel"),
    )(*[r[0] for r in rows], *fulls)


def _rowcall_vjp(fn, rows, fulls, cots, diff_rows, diff_fulls, *, tm, name, row_dtypes=None):
    rows_n = [r if isinstance(r, tuple) else (r, r.shape[1], 0) for r in rows]
    n_r, n_c = len(rows), len(cots)
    row_dtypes = row_dtypes or [F32] * len(diff_rows)

    def fn2(*vals):
        r = [v.astype(F32) for v in vals[:n_r]]
        ct = vals[n_r:n_r + n_c]
        f = [v.astype(F32) for v in vals[n_r + n_c:]]

        def g(*dargs):
            rr, ff = list(r), list(f)
            for k, idx in enumerate(diff_rows):
                rr[idx] = dargs[k]
            for k, idx in enumerate(diff_fulls):
                ff[idx] = dargs[len(diff_rows) + k]
            return fn(*rr, *ff)

        prim = [r[i] for i in diff_rows] + [f[i] for i in diff_fulls]
        outs, pull = jax.vjp(g, *prim)
        return pull(tuple(c.astype(o.dtype) for c, o in zip(ct, outs)))

    out_rows = [(rows_n[i][1], dt) for i, dt in zip(diff_rows, row_dtypes)]
    out_reds = [(fulls[i].shape, F32) for i in diff_fulls]
    return _rowcall(fn2, list(rows) + list(cots), fulls, out_rows, out_reds, tm=tm, name=name)


def _mm(a, b, *, ta=False, tb=False, bias=None, out_dtype=F32, name, tm=1024, tn=1024, tk=1024):
    (k_a, m) = a.shape if ta else a.shape[::-1]
    (n, k_b) = b.shape if tb else b.shape[::-1]
    assert k_a == k_b, (name, a.shape, b.shape)
    tm, tn, tk = _pick(m, tm) if m % V7X_LANES == 0 else m, _pick(n, tn), _pick(k_a, tk) if k_a % V7X_LANES == 0 else k_a
    nk = k_a // tk
    a_spec = pl.BlockSpec((tk, tm), lambda i, j, k: (k, i)) if ta else pl.BlockSpec((tm, tk), lambda i, j, k: (i, k))
    b_spec = pl.BlockSpec((tn, tk), lambda i, j, k: (j, k)) if tb else pl.BlockSpec((tk, tn), lambda i, j, k: (k, j))
    dims = (((0,) if ta else (1,), (1,) if tb else (0,)), ((), ()))
    has_bias = bias is not None

    def body(*refs):
        a_ref, b_ref = refs[0], refs[1]
        part = lax.dot_general(a_ref[...].astype(BF16), b_ref[...].astype(BF16), dims, preferred_element_type=F32)
        if nk == 1:
            o_ref = refs[-1]
            o_ref[...] = (part + refs[2][...] if has_bias else part).astype(o_ref.dtype)
            return
        o_ref, acc_ref = refs[-2], refs[-1]
        k = pl.program_id(2)

        @pl.when(k == 0)
        def _():
            acc_ref[...] = part

        @pl.when(k > 0)
        def _():
            acc_ref[...] += part

        @pl.when(k == nk - 1)
        def _():
            r = acc_ref[...]
            if has_bias:
                r = r + refs[2][...]
            o_ref[...] = r.astype(o_ref.dtype)

    in_specs = [a_spec, b_spec] + ([pl.BlockSpec((1, tn), lambda i, j, k: (0, j))] if has_bias else [])
    return pl.pallas_call(
        body, name=name, grid=(m // tm, n // tn, nk), in_specs=in_specs,
        out_specs=pl.BlockSpec((tm, tn), lambda i, j, k: (i, j)),
        out_shape=jax.ShapeDtypeStruct((m, n), out_dtype),
        scratch_shapes=[pltpu.VMEM((tm, tn), F32)] if nk > 1 else [],
        compiler_params=_cparams("parallel", "parallel", "arbitrary"),
    )(a, b, *([bias] if has_bias else []))


def _mm_bd(a, b, mode, *, name, tm=512, tk=1024):
    if mode == 'tn':
        t = a.shape[0]
        nb = S5_BD
        ka, kb = a.shape[1] // nb, b.shape[1] // nb
        tk = _pick(t, tk)
        nsteps = t // tk

        def body_tn(a_ref, b_ref, o_ref):
            part = lax.dot_general(a_ref[...].astype(BF16), b_ref[...].astype(BF16), (((0,), (0,)), ((), ())),
                                   preferred_element_type=F32)

            @pl.when(pl.program_id(1) == 0)
            def _():
                o_ref[...] = part

            @pl.when(pl.program_id(1) > 0)
            def _():
                o_ref[...] += part

        return pl.pallas_call(
            body_tn, name=name, grid=(nb, nsteps),
            in_specs=[pl.BlockSpec((tk, ka), lambda k, s: (s, k)), pl.BlockSpec((tk, kb), lambda k, s: (s, k))],
            out_specs=pl.BlockSpec((None, ka, kb), lambda k, s: (k, 0, 0)),
            out_shape=jax.ShapeDtypeStruct((nb, ka, kb), F32), compiler_params=_cparams("parallel", "arbitrary"),
        )(a, b)
    nb, ka, kb = b.shape
    m = a.shape[0]
    tm = _pick(m, tm)
    w_in, w_out = (ka, kb) if mode == 'nn' else (kb, ka)
    dims = (((1,), (0,)), ((), ())) if mode == 'nn' else (((1,), (1,)), ((), ()))

    def body(a_ref, b_ref, o_ref):
        o_ref[...] = lax.dot_general(a_ref[...].astype(BF16), b_ref[...].astype(BF16), dims, preferred_element_type=F32)

    return pl.pallas_call(
        body, name=name, grid=(m // tm, nb),
        in_specs=[pl.BlockSpec((tm, w_in), lambda i, k: (i, k)), pl.BlockSpec((None, ka, kb), lambda i, k: (k, 0, 0))],
        out_specs=pl.BlockSpec((tm, w_out), lambda i, k: (i, k)),
        out_shape=jax.ShapeDtypeStruct((m, nb * w_out), F32), compiler_params=_cparams("parallel", "parallel"),
    )(a, b)


def _seg_order(a):
    t, c = a.shape
    return a.reshape(V7X_SUBLANES, t // V7X_SUBLANES, c).transpose(1, 0, 2).reshape(t, c)


def _time_order(a):
    t, c = a.shape
    return a.reshape(t // V7X_SUBLANES, V7X_SUBLANES, c).transpose(1, 0, 2).reshape(t, c)


def _s5_scan(x, tab, *, reverse, hist=None, name):
    t, width = x.shape
    blk = 2 * SCAN_LANES
    ntile = t // V7X_SUBLANES
    with_da = hist is not None
    ln = SCAN_LANES
    rows8 = V7X_SUBLANES

    def body(*refs):
        if with_da:
            x_ref, tab_ref, h_ref, o_ref, da_ref = refs
        else:
            x_ref, tab_ref, o_ref = refs
        row = lax.broadcasted_iota(jnp.int32, (rows8, ln), 0)
        bcast = lambda k, lo: jnp.broadcast_to(tab_ref[k:k + 1, lo:lo + ln], (rows8, ln))
        ar, ai = bcast(0, 0), bcast(0, ln)
        zero = jnp.zeros((rows8, ln), F32)

        def tile(ref, i):
            r0 = pl.multiple_of(i * rows8, rows8)
            return ref[pl.ds(r0, rows8), :ln], ref[pl.ds(r0, rows8), ln:]

        def put(i, yr, yi):
            r0 = pl.multiple_of(i * rows8, rows8)
            o_ref[pl.ds(r0, rows8), :ln] = yr
            o_ref[pl.ds(r0, rows8), ln:] = yi

        def pass1(n, carry):
            i = (ntile - 1 - n) if reverse else n
            xr, xi = tile(x_ref, i)
            cr, ci = carry
            yr, yi = xr + ar * cr - ai * ci, xi + ar * ci + ai * cr
            put(i, yr, yi)
            return yr, yi

        fr, fi = lax.fori_loop(0, ntile, pass1, (zero, zero), unroll=8)
        for k, step in zip((1, 2, 3), (1, 2, 4)):
            pr, pi = bcast(k, 0), bcast(k, ln)
            shift, keep = (rows8 - step, row < rows8 - step) if reverse else (step, row >= step)
            sr = jnp.where(keep, pltpu.roll(fr, shift, 0), 0.0)
            si = jnp.where(keep, pltpu.roll(fi, shift, 0), 0.0)
            fr, fi = fr + pr * sr - pi * si, fi + pr * si + pi * sr
        shift, keep = (rows8 - 1, row < rows8 - 1) if reverse else (1, row >= 1)
        cr = jnp.where(keep, pltpu.roll(fr, shift, 0), 0.0)
        ci = jnp.where(keep, pltpu.roll(fi, shift, 0), 0.0)
        if with_da:
            lr, li = tile(h_ref, ntile - 1)
            h0r = jnp.where(row >= 1, pltpu.roll(lr, 1, 0), 0.0)
            h0i = jnp.where(row >= 1, pltpu.roll(li, 1, 0), 0.0)

        def pass2(n, carry):
            i = (ntile - 1 - n) if reverse else n
            pr, pi = carry[0], carry[1]
            yr, yi = tile(o_ref, i)
            yr, yi = yr + pr * cr - pi * ci, yi + pr * ci + pi * cr
            put(i, yr, yi)
            new = (pr * ar - pi * ai, pr * ai + pi * ar)
            if not with_da:
                return new
            pr_, pi_ = tile(h_ref, jnp.maximum(i - 1, 0))
            hpr, hpi = jnp.where(i > 0, pr_, h0r), jnp.where(i > 0, pi_, h0i)
            return new + (carry[2] + yr * hpr + yi * hpi, carry[3] + yi * hpr - yr * hpi)

        out = lax.fori_loop(0, ntile, pass2, (ar, ai) + ((zero, zero) if with_da else ()), unroll=8)
        if with_da:
            da_ref[:, :ln] = out[2]
            da_ref[:, ln:] = out[3]

    col = pl.BlockSpec((t, blk), lambda j: (0, j))
    tabs = pl.BlockSpec((rows8, blk), lambda j: (0, j))
    in_specs = [col, tabs] + ([col] if with_da else [])
    out_specs = [col] + ([tabs] if with_da else [])
    out_shape = [jax.ShapeDtypeStruct((t, width), F32)] + ([jax.ShapeDtypeStruct((rows8, width), F32)] if with_da else [])
    res = pl.pallas_call(
        body, name=name, grid=(width // blk,), in_specs=in_specs, out_specs=out_specs, out_shape=out_shape,
        compiler_params=_cparams("parallel"),
    )(x, tab, *([hist] if with_da else []))
    return res if with_da else res[0]


ATT_SCALE = (QK_NOPE + QK_ROPE) ** -0.5


def _att_mask(qi, kj, tb):
    qc = (qi * tb + lax.broadcasted_iota(jnp.int32, (tb, tb), 0)) // CHUNK
    kc = (kj * tb + lax.broadcasted_iota(jnp.int32, (tb, tb), 1)) // CHUNK
    return kc <= qc


def _with_side(side, n_main_in, n_main_out, refs, grid_first, grid_last, compute):
    if side is None:
        compute(refs)
        return
    n = side.n
    main = refs[:n_main_in] + refs[n_main_in + n:n_main_in + n + n_main_out] + refs[n_main_in + 2 * n + n_main_out + 3:]
    x_refs = refs[n_main_in:n_main_in + n]
    y_refs = refs[n_main_in + n + n_main_out:n_main_in + 2 * n + n_main_out]
    sems = refs[n_main_in + 2 * n + n_main_out:n_main_in + 2 * n + n_main_out + 3]

    @pl.when(grid_first)
    def _():
        for cp in side.copies(x_refs, y_refs, *sems):
            cp.start()

    compute(main)

    @pl.when(grid_last)
    def _():
        for cp in side.copies(x_refs, y_refs, *sems):
            cp.wait()


def _side_call(body, side, *, name, grid, in_specs, out_specs, out_shape, scratch, args, sem):
    n_out = len(out_shape)
    if side is not None:
        in_specs, args = in_specs + side.specs, list(args) + side.xs
        out_specs, out_shape = out_specs + side.specs, out_shape + side.out_shape
        scratch = side.scratch + scratch
        params = pltpu.CompilerParams(dimension_semantics=("arbitrary",) * len(grid), vmem_limit_bytes=VMEM_LIMIT,
                                      has_side_effects=True)
    else:
        params = _cparams(*sem)
    res = pl.pallas_call(body, name=name, grid=grid, in_specs=in_specs, out_specs=out_specs, out_shape=out_shape,
                         scratch_shapes=scratch, compiler_params=params)(*args)
    return res[:n_out], res[n_out:]


def _attn_fwd(q, k, v, *, name, side=None):
    h, t, dq = q.shape
    dv = v.shape[2]
    tb = min(ATT_BLOCK, t)
    nblk = t // tb
    nt = (((1,), (1,)), ((), ()))

    def compute(refs):
        q_ref, k_ref, v_ref, o_ref, lse_ref = refs
        i = pl.program_id(1)
        qb = q_ref[...]

        def kv_step(j, carry, masked):
            m, l, acc = carry
            r0 = pl.multiple_of(j * tb, tb)
            s = lax.dot_general(qb, k_ref[pl.ds(r0, tb), :], nt, preferred_element_type=F32) * ATT_SCALE
            if masked:
                s = jnp.where(_att_mask(i, j, tb), s, NEG_INF)
            m_new = jnp.maximum(m, jnp.max(s, axis=1, keepdims=True))
            alpha = jnp.exp(m - m_new)
            p = jnp.exp(s - m_new)
            l = alpha * l + jnp.sum(p, axis=1, keepdims=True)
            acc = alpha * acc + jnp.dot(p.astype(BF16), v_ref[pl.ds(r0, tb), :], preferred_element_type=F32)
            return m_new, l, acc

        init = (jnp.full((tb, 1), NEG_INF, F32), jnp.zeros((tb, 1), F32), jnp.zeros((tb, dv), F32))
        carry = lax.fori_loop(0, i, functools.partial(kv_step, masked=False), init)
        m, l, acc = kv_step(i, carry, True)
        o_ref[...] = acc / l
        lse_ref[...] = m + jnp.log(l)

    def body(*refs):
        first = (pl.program_id(0) == 0) & (pl.program_id(1) == 0)
        last = (pl.program_id(0) == h - 1) & (pl.program_id(1) == nblk - 1)
        _with_side(side, 3, 2, refs, first, last, compute)

    return _side_call(
        body, side, name=name, grid=(h, nblk),
        in_specs=[pl.BlockSpec((None, tb, dq), lambda hh, i: (hh, i, 0)),
                  pl.BlockSpec((None, t, dq), lambda hh, i: (hh, 0, 0)),
                  pl.BlockSpec((None, t, dv), lambda hh, i: (hh, 0, 0))],
        out_specs=[pl.BlockSpec((None, tb, dv), lambda hh, i: (hh, i, 0)),
                   pl.BlockSpec((None, tb, 1), lambda hh, i: (hh, i, 0))],
        out_shape=[jax.ShapeDtypeStruct((h, t, dv), F32), jax.ShapeDtypeStruct((h, t, 1), F32)],
        scratch=[], args=[q, k, v], sem=("parallel", "parallel"))


def _attn_bwd(q, k, v, o, lse, do, *, name, side=None):
    h, t, dq_w = q.shape
    dv_w = v.shape[2]
    tb = min(ATT_BLOCK, t)
    nblk = t // tb
    nt = (((1,), (1,)), ((), ()))
    tn = (((0,), (0,)), ((), ()))

    def compute(refs):
        q_ref, k_ref, v_ref, o_ref, lse_ref, do_ref, dq_ref, dk_ref, dv_ref, delta_ref = refs
        j = pl.program_id(1)

        @pl.when(j == 0)
        def _():
            dq_ref[...] = jnp.zeros_like(dq_ref)

            def dstep(i, c):
                r0 = pl.multiple_of(i * tb, tb)
                delta_ref[pl.ds(r0, tb), :] = jnp.sum(do_ref[pl.ds(r0, tb), :].astype(F32) * o_ref[pl.ds(r0, tb), :],
                                                      axis=1, keepdims=True)
                return c

            lax.fori_loop(0, nblk, dstep, 0)

        kb, vb = k_ref[...], v_ref[...]

        def q_step(i, carry, masked):
            dk, dv = carry
            r0 = pl.multiple_of(i * tb, tb)
            qb, dob = q_ref[pl.ds(r0, tb), :], do_ref[pl.ds(r0, tb), :]
            s = lax.dot_general(qb, kb, nt, preferred_element_type=F32) * ATT_SCALE
            if masked:
                s = jnp.where(_att_mask(i, j, tb), s, NEG_INF)
            p = jnp.exp(s - lse_ref[pl.ds(r0, tb), :])
            dv = dv + lax.dot_general(p.astype(BF16), dob, tn, preferred_element_type=F32)
            dp = lax.dot_general(dob, vb, nt, preferred_element_type=F32)
            ds = (p * (dp - delta_ref[pl.ds(r0, tb), :]) * ATT_SCALE).astype(BF16)
            dk = dk + lax.dot_general(ds, qb, tn, preferred_element_type=F32)
            dq_ref[pl.ds(r0, tb), :] += jnp.dot(ds, kb, preferred_element_type=F32)
            return dk, dv

        carry = q_step(j, (jnp.zeros((tb, dq_w), F32), jnp.zeros((tb, dv_w), F32)), True)
        dk, dv = lax.fori_loop(j + 1, nblk, functools.partial(q_step, masked=False), carry)
        dk_ref[...] = dk
        dv_ref[...] = dv

    def body(*refs):
        first = (pl.program_id(0) == 0) & (pl.program_id(1) == 0)
        last = (pl.program_id(0) == h - 1) & (pl.program_id(1) == nblk - 1)
        _with_side(side, 6, 3, refs, first, last, compute)

    whole = lambda w: pl.BlockSpec((None, t, w), lambda hh, j: (hh, 0, 0))
    blockj = lambda w: pl.BlockSpec((None, tb, w), lambda hh, j: (hh, j, 0))
    return _side_call(
        body, side, name=name, grid=(h, nblk),
        in_specs=[whole(dq_w), blockj(dq_w), blockj(dv_w), whole(dv_w), whole(1), whole(dv_w)],
        out_specs=[whole(dq_w), blockj(dq_w), blockj(dv_w)],
        out_shape=[jax.ShapeDtypeStruct((h, t, dq_w), F32), jax.ShapeDtypeStruct((h, t, dq_w), F32),
                   jax.ShapeDtypeStruct((h, t, dv_w), F32)],
        scratch=[pltpu.VMEM((t, 1), F32)], args=[q, k, v, o, lse, do], sem=("parallel", "arbitrary"))


class _Exchange:
    def __init__(self, xs, gather):
        self.xs, self.gather, self.n = list(xs), gather, len(xs)
        shapes = [tuple(x.shape) if gather else tuple(x.shape[1:]) for x in xs]
        self.out_shape = [jax.ShapeDtypeStruct((N_DEV,) + shp, x.dtype) for shp, x in zip(shapes, xs)]
        self.specs = [pl.BlockSpec(memory_space=pl.ANY)] * self.n
        self.scratch = [pltpu.SemaphoreType.DMA((self.n, N_DEV - 1)), pltpu.SemaphoreType.DMA((self.n, N_DEV - 1)),
                        pltpu.SemaphoreType.DMA((self.n,))]

    def copies(self, x_refs, y_refs, send_sems, recv_sems, local_sems):
        mx, my, mc = lax.axis_index("x"), lax.axis_index("y"), lax.axis_index("c")
        me = 4 * mx + 2 * my + mc
        out = [pltpu.make_async_copy(x_refs[i] if self.gather else x_refs[i].at[me], y_refs[i].at[me], local_sems.at[i])
               for i in range(self.n)]
        for k in range(1, N_DEV):
            px = 1 - mx if k & 4 else mx
            py = 1 - my if k & 2 else my
            pc = 1 - mc if k & 1 else mc
            for i in range(self.n):
                out.append(pltpu.make_async_remote_copy(
                    src_ref=x_refs[i] if self.gather else x_refs[i].at[4 * px + 2 * py + pc], dst_ref=y_refs[i].at[me],
                    send_sem=send_sems.at[i, k - 1], recv_sem=recv_sems.at[i, k - 1],
                    device_id=(px, py, pc), device_id_type=pl.DeviceIdType.MESH))
        return out


def _exchange(xs, *, gather, name):
    ex = _Exchange(xs, gather)
    n = ex.n

    def body(*refs):
        copies = ex.copies(refs[:n], refs[n:2 * n], *refs[2 * n:])
        for cp in copies:
            cp.start()
        for cp in copies:
            cp.wait()

    return pl.pallas_call(
        body, name=name, out_shape=ex.out_shape, in_specs=ex.specs, out_specs=ex.specs, scratch_shapes=ex.scratch,
        compiler_params=pltpu.CompilerParams(has_side_effects=True),
    )(*ex.xs)


def _adamw(w, gs, m, v, *, name, tm=256):
    nl = len(gs)
    parts = gs[0].ndim == 3
    c = w.shape[1]
    r = w.shape[0] // nl
    tm = _pick_rows(r, tm)
    nrow = r // tm

    def body(*refs):
        w_ref, g_refs, (m_ref, v_ref, go_ref, d_ref, mo_ref, vo_ref) = refs[0], refs[1:1 + nl], refs[1 + nl:]

        def update(g_ref):
            if parts:
                gv = g_ref[0].astype(F32)
                for k in range(1, N_DEV):
                    gv = gv + g_ref[k].astype(F32)
            else:
                gv = g_ref[...]
            mn = ADAM_B1 * m_ref[...] + (1.0 - ADAM_B1) * gv
            vn = ADAM_B2 * v_ref[...] + (1.0 - ADAM_B2) * jnp.square(gv)
            m_hat = mn / (1.0 - ADAM_B1 ** ADAM_STEP)
            v_hat = vn / (1.0 - ADAM_B2 ** ADAM_STEP)
            go_ref[...] = gv
            d_ref[...] = -ADAM_LR * (m_hat / (jnp.sqrt(v_hat) + ADAM_EPS) + ADAM_WD * w_ref[...])
            mo_ref[...] = mn
            vo_ref[...] = vn

        if nl == 1:
            update(g_refs[0])
        else:
            for layer, g_ref in enumerate(g_refs):
                pl.when(pl.program_id(0) == layer)(functools.partial(update, g_ref))

    spec = pl.BlockSpec((tm, c), lambda l, i: (l * nrow + i, 0))

    def gspec(layer):
        row = lambda l, i: jnp.where(l == layer, i, 0)
        if parts:
            return pl.BlockSpec((N_DEV, tm, c), lambda l, i: (0, row(l, i), 0))
        return pl.BlockSpec((tm, c), lambda l, i: (row(l, i), 0))

    return pl.pallas_call(
        body, name=name, grid=(nl, nrow), in_specs=[spec] + [gspec(k) for k in range(nl)] + [spec, spec],
        out_specs=[spec] * 4, out_shape=[jax.ShapeDtypeStruct(w.shape, F32)] * 4,
        compiler_params=_cparams("arbitrary", "arbitrary"),
    )(w, *gs, m, v)


def _sum_parts(x, *, name):
    def body(x_ref, o_ref):
        acc = x_ref[0]
        for k in range(1, N_DEV):
            acc = acc + x_ref[k]
        o_ref[...] = acc

    return pl.pallas_call(body, name=name, out_shape=jax.ShapeDtypeStruct(x.shape[1:], x.dtype))(x)


def _permute_in(a):
    pad = jnp.zeros(a.shape[:-1] + (IN_PAD - IN_WIDTH,), a.dtype)
    return jnp.concatenate([a[..., lo:hi] for lo, hi in _IN_SEGMENTS] + [pad], axis=-1)


def _unpermute_in(a):
    out, pos = {}, 0
    for lo, hi in _IN_SEGMENTS:
        out[lo] = a[..., pos:pos + hi - lo]
        pos += hi - lo
    return jnp.concatenate([out[lo] for lo in sorted(out)], axis=-1)


def _full_from_gathered(g, kind):
    if kind == 'row':
        return g.reshape((-1,) + g.shape[2:])
    return jnp.concatenate([g[d] for d in range(N_DEV)], axis=-1)


def _contrib_from_full(g, kind):
    if kind == 'row':
        return g.reshape((N_DEV, -1) + g.shape[1:])
    ns = g.shape[-1] // N_DEV
    return jnp.stack([g[..., d * ns:(d + 1) * ns] for d in range(N_DEV)])


def _in_runs(ns):
    runs, pos = [], 0
    for lo, hi in _IN_SEGMENTS:
        for d in range(lo // ns, (hi - 1) // ns + 1):
            a, b = max(lo, d * ns), min(hi, (d + 1) * ns)
            runs.append((d, a - d * ns, b - d * ns, pos))
            pos += b - a
    return runs


def _w_in_from_gathered(g):
    pieces = [g[d][:, a:b] for d, a, b, _ in _in_runs(g.shape[2])]
    pad = jnp.zeros((g.shape[1], IN_PAD - IN_WIDTH), g.dtype)
    return jnp.concatenate(pieces + [pad], axis=1)


def _w_in_contrib(gp):
    ns = IN_WIDTH // N_DEV
    per_dev = [[] for _ in range(N_DEV)]
    for d, a, b, pos in sorted(_in_runs(ns), key=lambda r: (r[0], r[1])):
        per_dev[d].append(gp[:, pos:pos + b - a])
    return jnp.stack([jnp.concatenate(p, axis=1) for p in per_dev])


def _as2d(a, lead=0):
    return a.reshape(a.shape[:lead] + (-1, a.shape[-1]))


def _chan_cols(re, im):
    lead = re.shape[:-1]
    nb = S5_CH // SCAN_LANES
    return jnp.stack([re.reshape(lead + (nb, SCAN_LANES)), im.reshape(lead + (nb, SCAN_LANES))],
                     axis=-2).reshape(lead + (2 * S5_CH,))


def _s5_tables(lam_re, lam_im, log_dt, b_re, b_im, c_re, c_im):
    dt = jnp.exp(log_dt)[:, None]
    mag = jnp.exp(lam_re * dt)
    a_re = mag * jnp.cos(lam_im * dt)
    a_im = mag * jnp.sin(lam_im * dt)
    den = lam_re * lam_re + lam_im * lam_im
    f_re = ((a_re - 1.0) * lam_re + a_im * lam_im) / den
    f_im = (a_im * lam_re - (a_re - 1.0) * lam_im) / den
    bb_re = f_re[..., None] * b_re - f_im[..., None] * b_im
    bb_im = f_re[..., None] * b_im + f_im[..., None] * b_re
    gb = S5_GROUPS // S5_BD
    eye = jnp.eye(gb, dtype=F32)
    blocks = lambda a: a.reshape((S5_BD, gb) + a.shape[1:])

    def cols(re, im):
        shp = (S5_BD, S5_WIDTH // S5_BD, -1, SCAN_LANES)
        return jnp.stack([re.reshape(shp), im.reshape(shp)], axis=-2).reshape(S5_BD, S5_WIDTH // S5_BD, -1)

    flat = lambda a: a.reshape(S5_BD, S5_WIDTH // S5_BD, -1)
    wb_c = cols(flat(jnp.einsum('kgpc,gh->kgchp', blocks(bb_re), eye)), flat(jnp.einsum('kgpc,gh->kgchp', blocks(bb_im), eye)))
    wc_c = cols(flat(jnp.einsum('kgcp,gh->khcgp', blocks(c_re), eye)), -flat(jnp.einsum('kgcp,gh->khcgp', blocks(c_im), eye)))
    a_row = _chan_cols(a_re.reshape(1, S5_CH), a_im.reshape(1, S5_CH))
    return wb_c, wc_c.transpose(0, 2, 1), a_row


def _scan_tables(a_row, conj, seg_len):
    nb = S5_CH // SCAN_LANES
    a = a_row.reshape(nb, 2, SCAN_LANES)
    base = (a[:, 0], -a[:, 1] if conj else a[:, 1])
    mul = lambda x, y: (x[0] * y[0] - x[1] * y[1], x[0] * y[1] + x[1] * y[0])
    seg, sq, e = None, base, seg_len
    while e:
        if e & 1:
            seg = sq if seg is None else mul(seg, sq)
        sq, e = mul(sq, sq), e >> 1
    seg2 = mul(seg, seg)
    rows = [base, seg, seg2, mul(seg2, seg2)]
    lay = lambda z: jnp.stack([z[0], z[1]], axis=1).reshape(-1)
    return jnp.stack([lay(z) for z in rows] + [jnp.zeros((2 * S5_CH,), F32)] * (V7X_SUBLANES - len(rows)))


def _rope_tables(t):
    half = QK_ROPE // 2
    inv_freq = 1.0 / (ROPE_THETA ** (jnp.arange(0, QK_ROPE, 2, dtype=F32) / QK_ROPE))
    ang = jnp.arange(t, dtype=F32)[:, None] * inv_freq[None, :]
    cos, sin = jnp.cos(ang), jnp.sin(ang)
    zero = jnp.zeros_like(sin)

    def lay(nope, width, first, second):
        head = jnp.concatenate([jnp.full((t, nope), 1.0 if first is cos else 0.0, F32), first, second], axis=1)
        reps = width // head.shape[1]
        out = jnp.tile(head, (1, reps))
        return jnp.pad(out, ((0, 0), (0, width - out.shape[1])))

    hq = MLA_HEADS * (QK_NOPE + QK_ROPE)
    q_tabs = (lay(QK_NOPE, hq, cos, cos), lay(QK_NOPE, hq, -sin, zero), lay(QK_NOPE, hq, zero, sin))
    k_tabs = (lay(0, V7X_LANES, cos, cos)[:, :V7X_LANES] * (jnp.arange(V7X_LANES) < QK_ROPE),
              lay(0, V7X_LANES, -sin, zero) * (jnp.arange(V7X_LANES) < QK_ROPE),
              lay(0, V7X_LANES, zero, sin) * (jnp.arange(V7X_LANES) < QK_ROPE))
    return q_tabs, k_tabs


def _sgu_tables(w_s, b_s):
    pos = jnp.arange(SGU_CHUNK) // CHUNK
    mask = pos[None, :] <= pos[:, None]
    wm = jnp.where(mask[None], w_s, 0.0).reshape(SGU_GROUPS * SGU_CHUNK, SGU_CHUNK)
    bias = jnp.repeat(b_s.T, SGU_WIDTH // SGU_GROUPS, axis=1)
    return wm, bias


def _row(v):
    return v.reshape(1, -1)


_S5_PARAMS = ('s5_lambda_re', 's5_lambda_im', 's5_log_dt', 's5_b_re', 's5_b_im', 's5_c_re', 's5_c_im')


def _derived_tables(p, t):
    (wb, wc, a_row), s5_pull = jax.vjp(jax.vmap(_s5_tables), *[p[n] for n in _S5_PARAMS])
    (wm, bias), sgu_pull = jax.vjp(jax.vmap(_sgu_tables), p['sgu_w_s'], p['sgu_b_s'])
    tab_f = jax.vmap(lambda a: _scan_tables(a, False, t // V7X_SUBLANES))(a_row)
    tab_b = jax.vmap(lambda a: _scan_tables(a, True, t // V7X_SUBLANES))(a_row)
    wb, wc = wb.astype(BF16), wc.astype(BF16)
    per_layer = [dict(s5_wb=wb[l], s5_wc=wc[l], s5_tab_fwd=tab_f[l], s5_tab_bwd=tab_b[l], sgu_wm=wm[l], sgu_bias=bias[l])
                 for l in range(len(wm))]

    def pull(grads):
        stacked = lambda k: jnp.stack([g[k] for g in grads])
        out = dict(zip(_S5_PARAMS, s5_pull((stacked('s5_wb'), stacked('s5_wc'), stacked('s5_a')))))
        out['sgu_w_s'], out['sgu_b_s'] = sgu_pull((stacked('sgu_wm'), stacked('sgu_bias')))
        return out

    return per_layer, pull


def _layer_fwd(x, ada, w, rope_tabs, tag, side=None, after_attn=None):
    s = {'x': x}
    q_tabs, k_tabs = rope_tabs
    sc1, gt1, sc2, gt2 = _row(1.0 + ada[1]), _row(1.0 + ada[2]), _row(1.0 + ada[4]), _row(1.0 + ada[5])
    s.update(sc1=sc1, gt1=gt1, sc2=sc2, gt2=gt2)
    (h,) = _rowcall(_modulate_fn, [x], [sc1, _row(ada[0])], [(D_MODEL, BF16)], tm=512, name=f"mod1_{tag}")
    proj = _mm(h, w['w_in_p'], bias=w['b_in_p'], name=f"proj_{tag}", tn=1792)
    s.update(h=h, proj=proj)

    u_view = (proj, S5_WIDTH, P_S5 // S5_WIDTH)
    u_seg = _seg_order(proj[:, P_S5:P_S5 + S5_WIDTH])
    bu = _mm_bd(u_seg, w['s5_wb'], 'nn', name=f"s5_bu_{tag}")
    hs = _s5_scan(bu, w['s5_tab_fwd'], reverse=False, name=f"s5_scan_{tag}")
    ylin = _time_order(_mm_bd(hs, w['s5_wc'], 'nn', name=f"s5_c_{tag}"))
    s5_full = [_row(w['s5_d']), w['s5_w_glu'], _row(w['s5_b_glu'])]
    (y_s5,) = _rowcall(_s5_post_fn, [ylin, u_view], s5_full, [(S5_WIDTH, BF16)], tm=256, name=f"s5_post_{tag}")
    s.update(u_seg=u_seg, hs=hs, ylin=ylin, y_s5=y_s5)

    mla_rows = [(proj, MLA_BLK, P_MLA // MLA_BLK), *q_tabs, *k_tabs]
    mla_full = [_row(w['mla_q_norm']), w['mla_w_q_up'], _row(w['mla_kv_norm']), w['mla_w_kv_up']]
    hq, hkv = MLA_HEADS * (QK_NOPE + QK_ROPE), MLA_HEADS * (QK_NOPE + V_HEAD)
    q_r, kv, kpe_r = _rowcall(_mla_pre_fn, mla_rows, mla_full, [(hq, BF16), (hkv, BF16), (V7X_LANES, BF16)],
                              tm=256, name=f"mla_pre_{tag}")
    t = x.shape[0]
    qh = q_r.reshape(t, MLA_HEADS, -1).transpose(1, 0, 2)
    kv3 = kv.reshape(t, MLA_HEADS, -1).transpose(1, 0, 2)
    kh = jnp.concatenate([kv3[:, :, :QK_NOPE], jnp.broadcast_to(kpe_r[None, :, :QK_ROPE], (MLA_HEADS, t, QK_ROPE))], axis=2)
    vh = kv3[:, :, QK_NOPE:]
    (o, lse), side_out = _attn_fwd(qh, kh, vh, name=f"attn_fwd_{tag}", side=side)
    if after_attn is not None:
        after_attn(side_out)
    y_mla = o.transpose(1, 0, 2).reshape(t, -1).astype(BF16)
    s.update(qh=qh, kh=kh, vh=vh, o=o, lse=lse, y_mla=y_mla)

    sgu_rows = [(proj, SGU_WIDTH, P_USGU // SGU_WIDTH), (proj, SGU_WIDTH, P_VSGU // SGU_WIDTH)]
    sgu_full = [_row(w['sgu_ln_g']), _row(w['sgu_ln_b']), w['sgu_wm'], w['sgu_bias']]
    (y_sgu,) = _rowcall(_sgu_fn, sgu_rows, sgu_full, [(SGU_WIDTH, BF16)], tm=SGU_CHUNK, name=f"sgu_{tag}")
    s.update(sgu_full=sgu_full, y_sgu=y_sgu)

    wbr = w['w_branch'].reshape(-1, D_MODEL)
    gate_rows = [(proj, D_MODEL, b) for b in range(3)]
    (merged,) = _rowcall(_merge_fn, [y_s5, y_mla, y_sgu] + gate_rows, [wbr], [(D_MODEL, BF16)], tm=256, name=f"merge_{tag}")
    ymix = _mm(merged, w['w_out'], name=f"wout_{tag}")
    (x1,) = _rowcall(_ln_res_fn, [x, ymix], [gt1, _row(w['ln1_g']), _row(w['ln1_b'])], [(D_MODEL, F32)], tm=256,
                     name=f"ln1_{tag}")
    s.update(merged=merged, ymix=ymix, x1=x1)

    (h2,) = _rowcall(_modulate_fn, [x1], [sc2, _row(ada[3])], [(D_MODEL, BF16)], tm=512, name=f"mod2_{tag}")
    ab = _mm(h2, w['ffn_w_in'], name=f"ffn_in_{tag}", tn=1408)
    (act,) = _rowcall(_swiglu_fn, [(ab, FF_HIDDEN, 0), (ab, FF_HIDDEN, 1)], [], [(FF_HIDDEN, BF16)], tm=256,
                      name=f"swiglu_{tag}")
    f = _mm(act, w['ffn_w_out'], name=f"ffn_out_{tag}", tk=2816)
    (x2,) = _rowcall(_ln_res_fn, [x1, f], [gt2, _row(w['ln2_g']), _row(w['ln2_b'])], [(D_MODEL, F32)], tm=256,
                     name=f"ln2_{tag}")
    s.update(h2=h2, ab=ab, act=act, f=f)
    return x2, s, side_out


def _mod_bwd_fn(x, dh, dxa, scale_row):
    return (dxa + dh * scale_row, jnp.sum(dh * x, axis=0, keepdims=True), jnp.sum(dh, axis=0, keepdims=True))


def _layer_bwd(dx2, s, w, rope_tabs, tag, make_side=None):
    g = {}
    q_tabs, k_tabs = rope_tabs
    t = dx2.shape[0]
    ln_full = lambda gt, a, b: [gt, _row(w[a]), _row(w[b])]

    dx1_a, df, dgt2, g['ln2_g'], g['ln2_b'] = _rowcall_vjp(
        _ln_res_fn, [s['x1'], s['f']], ln_full(s['gt2'], 'ln2_g', 'ln2_b'), [dx2], [0, 1], [0, 1, 2],
        tm=256, name=f"ln2_bwd_{tag}", row_dtypes=[F32, BF16])
    dact = _mm(df, w['ffn_w_out'], tb=True, name=f"ffn_out_dx_{tag}", tn=1408)
    g['ffn_w_out'] = _mm(s['act'], df, ta=True, name=f"ffn_out_dw_{tag}", tm=1408)
    da_, db_ = _rowcall_vjp(_swiglu_fn, [(s['ab'], FF_HIDDEN, 0), (s['ab'], FF_HIDDEN, 1)], [], [dact], [0, 1], [],
                            tm=256, name=f"swiglu_bwd_{tag}", row_dtypes=[BF16, BF16])
    dab = jnp.concatenate([da_, db_], axis=1)
    dh2 = _mm(dab, w['ffn_w_in'], tb=True, name=f"ffn_in_dx_{tag}", tk=1408)
    g['ffn_w_in'] = _mm(s['h2'], dab, ta=True, name=f"ffn_in_dw_{tag}", tn=1408)
    dx1, dsc2, dsh2 = _rowcall(_mod_bwd_fn, [s['x1'], dh2, dx1_a], [s['sc2']], [(D_MODEL, F32)],
                               [((1, D_MODEL), F32)] * 2, tm=256, name=f"mod2_bwd_{tag}")

    dx_a, dymix, dgt1, g['ln1_g'], g['ln1_b'] = _rowcall_vjp(
        _ln_res_fn, [s['x'], s['ymix']], ln_full(s['gt1'], 'ln1_g', 'ln1_b'), [dx1], [0, 1], [0, 1, 2],
        tm=256, name=f"ln1_bwd_{tag}", row_dtypes=[F32, BF16])
    dmerged = _mm(dymix, w['w_out'], tb=True, name=f"wout_dx_{tag}")
    g['w_out'] = _mm(s['merged'], dymix, ta=True, name=f"wout_dw_{tag}")

    proj = s['proj']
    wbr = w['w_branch'].reshape(-1, D_MODEL)
    gate_rows = [(proj, D_MODEL, b) for b in range(3)]
    dy_s5, dy_mla, dy_sgu, dl0, dl1, dl2, dwbr = _rowcall_vjp(
        _merge_fn, [s['y_s5'], s['y_mla'], s['y_sgu']] + gate_rows, [wbr], [dmerged], [0, 1, 2, 3, 4, 5], [0],
        tm=256, name=f"merge_bwd_{tag}")
    g['w_branch'] = dwbr.reshape(w['w_branch'].shape)

    sgu_rows = [(proj, SGU_WIDTH, P_USGU // SGU_WIDTH), (proj, SGU_WIDTH, P_VSGU // SGU_WIDTH)]
    du_sgu, dv_sgu, dlg, dlb, dwm, dbias = _rowcall_vjp(
        _sgu_fn, sgu_rows, s['sgu_full'], [dy_sgu], [0, 1], [0, 1, 2, 3], tm=SGU_CHUNK, name=f"sgu_bwd_{tag}")
    g['sgu_ln_g'], g['sgu_ln_b'] = dlg.reshape(-1), dlb.reshape(-1)
    g['sgu_wm'], g['sgu_bias'] = dwm, dbias

    do = dy_mla.reshape(t, MLA_HEADS, V_HEAD).transpose(1, 0, 2).astype(BF16)
    (dqh, dkh, dvh), side_out = _attn_bwd(s['qh'], s['kh'], s['vh'], s['o'], s['lse'], do, name=f"attn_bwd_{tag}",
                                          side=make_side(g) if make_side is not None else None)
    dq_r = dqh.transpose(1, 0, 2).reshape(t, -1)
    dkv = jnp.concatenate([dkh[:, :, :QK_NOPE], dvh], axis=2).transpose(1, 0, 2).reshape(t, -1)
    dkpe = jnp.pad(jnp.sum(dkh[:, :, QK_NOPE:], axis=0), ((0, 0), (0, V7X_LANES - QK_ROPE)))
    mla_rows = [(proj, MLA_BLK, P_MLA // MLA_BLK), *q_tabs, *k_tabs]
    mla_full = [_row(w['mla_q_norm']), w['mla_w_q_up'], _row(w['mla_kv_norm']), w['mla_w_kv_up']]
    dmla, dqn, g['mla_w_q_up'], dkvn, g['mla_w_kv_up'] = _rowcall_vjp(
        _mla_pre_fn, mla_rows, mla_full, [dq_r, dkv, dkpe], [0], [0, 1, 2, 3], tm=256, name=f"mla_pre_bwd_{tag}")
    g['mla_q_norm'], g['mla_kv_norm'] = dqn.reshape(-1), dkvn.reshape(-1)

    s5_full = [_row(w['s5_d']), w['s5_w_glu'], _row(w['s5_b_glu'])]
    dylin, du_a, dd, g['s5_w_glu'], dbg = _rowcall_vjp(
        _s5_post_fn, [s['ylin'], (proj, S5_WIDTH, P_S5 // S5_WIDTH)], s5_full, [dy_s5], [0, 1], [0, 1, 2], tm=256,
        name=f"s5_post_bwd_{tag}", row_dtypes=[BF16, F32])
    g['s5_d'], g['s5_b_glu'] = dd.reshape(-1), dbg.reshape(-1)
    dylin = _seg_order(dylin)
    dhs = _mm_bd(dylin, w['s5_wc'], 'nt', name=f"s5_c_dx_{tag}")
    g['s5_wc'] = _mm_bd(s['hs'], dylin, 'tn', name=f"s5_c_dw_{tag}")
    gs, da_part = _s5_scan(dhs, w['s5_tab_bwd'], reverse=True, hist=s['hs'], name=f"s5_scan_bwd_{tag}")
    du_b = _time_order(_mm_bd(gs, w['s5_wb'], 'nt', name=f"s5_bu_dx_{tag}"))
    g['s5_wb'] = _mm_bd(s['u_seg'], gs, 'tn', name=f"s5_bu_dw_{tag}")
    g['s5_a'] = jnp.sum(da_part, axis=0, keepdims=True)

    dproj_f32 = jnp.concatenate([dl0, dl1, dl2, du_a + du_b, du_sgu, dv_sgu, dmla], axis=1)
    dproj, db_in = _rowcall(lambda d: (d, jnp.sum(d, axis=0, keepdims=True)), [dproj_f32], [], [(IN_PAD, BF16)],
                            [((1, IN_PAD), F32)], tm=256, name=f"dproj_{tag}")
    dh = _mm(dproj, w['w_in_p'], tb=True, name=f"proj_dx_{tag}", tk=1792)
    g['w_in_p'] = _mm(s['h'], dproj, ta=True, name=f"proj_dw_{tag}", tn=896)
    g['b_in'] = _unpermute_in(db_in).reshape(-1)
    dx, dsc1, dsh1 = _rowcall(_mod_bwd_fn, [s['x'], dh, dx_a], [s['sc1']], [(D_MODEL, F32)], [((1, D_MODEL), F32)] * 2,
                              tm=256, name=f"mod1_bwd_{tag}")
    d_ada = jnp.concatenate([dsh1, dsc1, dgt1, dsh2, dsc2, dgt2], axis=0)
    return dx, d_ada, g, side_out


def _loss_fn(y, target):
    err = y - target
    return (err / D_MODEL, 0.5 * jnp.sum(jnp.sum(err * err, axis=1, keepdims=True), axis=0, keepdims=True) / D_MODEL)


def _step(p):
    me = 4 * lax.axis_index("x") + 2 * lax.axis_index("y") + lax.axis_index("c")
    x = p['x'][0]
    t = x.shape[0]
    rope_tabs = _rope_tables(t)

    (c_all,) = _exchange([jnp.broadcast_to(p['c'], (V7X_SUBLANES, D_MODEL))], gather=True, name="gather_c")
    c_all = c_all[:, 0, :]
    (c_act,) = _rowcall(lambda cc: (cc * _sigmoid(cc),), [c_all], [], [(D_MODEL, F32)], tm=N_DEV, name="c_silu")
    ncol = p['w_ada'].shape[2]
    b_ada_loc = lax.dynamic_slice_in_dim(p['b_ada'], me * ncol, ncol, axis=1)
    ada_cols = jnp.concatenate([_mm(c_act, p['w_ada'][l], bias=b_ada_loc[l:l + 1], name=f"ada_{l}") for l in range(DEPTH)])
    (ada_all,) = _exchange([ada_cols], gather=True, name="gather_ada")
    ada_all = ada_all.reshape(N_DEV, DEPTH, N_DEV, ncol)
    ada = lax.dynamic_index_in_dim(ada_all, me, axis=2, keepdims=False)
    ada = ada.transpose(1, 0, 2).reshape(DEPTH, 6, D_MODEL)

    mixer_w, ffn_w = SHARDED[:-2], SHARDED[-2:]

    def shards(l, group):
        return [p[n][l].astype(BF16) for n, _ in group]

    def contribs(g, group):
        return [(_w_in_contrib(g['w_in_p']) if n == 'w_in' else _contrib_from_full(g[n], kind)).astype(BF16)
                for n, kind in group]

    tables, pull_tables = _derived_tables(p, t)

    def mixer_weights(l, gathered):
        w = {n: _full_from_gathered(g, kind) for (n, kind), g in zip(mixer_w[1:], gathered[1:])}
        w['w_in_p'] = _w_in_from_gathered(gathered[0])
        w['b_in_p'] = _row(_permute_in(p['b_in'][l]))
        w.update(tables[l])
        for n in SMALL:
            if n != 'b_ada' and n != 'b_in':
                w[n] = p[n][l]
        return w

    saved, layers = [], []
    gathered = _exchange(shards(0, mixer_w), gather=True, name="gather_w_0")
    for l in range(DEPTH):
        w = mixer_weights(l, gathered)
        side = _Exchange(shards(l, ffn_w) + (shards(l + 1, mixer_w) if l + 1 < DEPTH else []), True)

        def add_ffn(res, w=w):
            for (n, kind), g in zip(ffn_w, res):
                w[n] = _full_from_gathered(g, kind)

        x, s, res = _layer_fwd(x, ada[l], w, rope_tabs, f"l{l}", side, add_ffn)
        gathered = res[len(ffn_w):]
        layers.append(w)
        saved.append(s)
    dy, loss_loc = _rowcall(_loss_fn, [x, p['loss_target'][0]], [], [(D_MODEL, F32)], [((1, 1), F32)], tm=256, name="loss")
    loss = lax.psum(loss_loc[0, 0], ("x", "y", "c"))

    d_ada, grads, landed = [None] * DEPTH, [None] * DEPTH, [[None, None] for _ in range(DEPTH)]
    dx, pending = dy, []
    for l in reversed(range(DEPTH)):
        make_side = lambda g, pending=pending: _Exchange(contribs(g, ffn_w) + pending, False)
        dx, d_ada[l], grads[l], res = _layer_bwd(dx, saved[l], layers[l], rope_tabs, f"l{l}", make_side)
        landed[l][1] = res[:len(ffn_w)]
        if l + 1 < DEPTH:
            landed[l + 1][0] = res[len(ffn_w):]
        pending = contribs(grads[l], mixer_w)
    landed[0][0] = _exchange(pending, gather=False, name="scatter_g_0")
    landed = [list(a) + list(b) for a, b in landed]
    d_ada = jnp.stack(d_ada).reshape(DEPTH, 6 * D_MODEL)

    (d_ada_all,) = _exchange([d_ada], gather=True, name="gather_dada")
    d_ada_cols = lax.dynamic_slice_in_dim(d_ada_all, me * ncol, ncol, axis=2)
    pad_b = ((0, V7X_LANES - N_DEV), (0, 0))
    c_act_p = jnp.pad(c_act, pad_b)
    g_w_ada = jnp.stack([_mm(c_act_p, jnp.pad(d_ada_cols[:, l], pad_b), ta=True, name=f"ada_dw_{l}") for l in range(DEPTH)])

    out = {}
    kinds = ('grad_', 'delta_', 'new_m_', 'new_v_')
    for i, (n, _) in enumerate(SHARDED):
        res = _adamw(_as2d(p[n]), [_as2d(landed[l][i], lead=1) for l in range(DEPTH)], _as2d(p['m_' + n]),
                     _as2d(p['v_' + n]), name=f"adamw_{n}")
        for kind, r in zip(kinds, res):
            out[kind + n] = r.reshape(p[n].shape)

    res = _adamw(_as2d(p['w_ada']), [_as2d(g_w_ada)], _as2d(p['m_w_ada']), _as2d(p['v_w_ada']), name="adamw_ada")
    for kind, r in zip(kinds, res):
        out[kind + 'w_ada'] = r.reshape(p['w_ada'].shape)

    small_g = pull_tables(grads)
    small_g.update({n: jnp.stack([grads[l][n] for l in range(DEPTH)]) for n in SMALL if n not in small_g and n != 'b_ada'})
    small_g['b_ada'] = d_ada
    n_small = sum(int(np.prod(p[n].shape)) for n in SMALL)
    n_pad = -(-n_small // SMALL_PAD) * SMALL_PAD
    flat = jnp.concatenate([small_g[n].reshape(-1) for n in SMALL])
    flat = jnp.pad(flat, (0, n_pad - n_small)).reshape(N_DEV, n_pad // N_DEV // 1024, 1024)
    (landed_small,) = _exchange([flat], gather=False, name="scatter_small")
    (g_all,) = _exchange([_sum_parts(landed_small, name="sum_small")], gather=True, name="gather_small")
    g_all = g_all.reshape(-1)
    off = 0
    for n in SMALL:
        size = int(np.prod(p[n].shape))
        g_n = g_all[off:off + size].reshape(p[n].shape)
        off += size
        res = _adamw(_as2d(p[n]), [_as2d(g_n)], _as2d(p['m_' + n]), _as2d(p['v_' + n]), name=f"adamw_{n}")
        for kind, r in zip(kinds, res):
            out[kind + n] = r.reshape(p[n].shape)

    outs = [loss, dx[None]]
    for kind in ('grad_', 'delta_', 'new_m_', 'new_v_'):
        outs += [out[kind + n] for n in WNAMES]
    return tuple(outs)


def kernel(x, c, w_ada, b_ada, w_in, b_in, s5_lambda_re, s5_lambda_im, s5_log_dt, s5_b_re, s5_b_im, s5_c_re, s5_c_im, s5_d, s5_w_glu, s5_b_glu, mla_q_norm, mla_w_q_up, mla_kv_norm, mla_w_kv_up, sgu_ln_g, sgu_ln_b, sgu_w_s, sgu_b_s, w_branch, w_out, ln1_g, ln1_b, ffn_w_in, ffn_w_out, ln2_g, ln2_b, loss_target, m_w_ada, m_b_ada, m_w_in, m_b_in, m_s5_lambda_re, m_s5_lambda_im, m_s5_log_dt, m_s5_b_re, m_s5_b_im, m_s5_c_re, m_s5_c_im, m_s5_d, m_s5_w_glu, m_s5_b_glu, m_mla_q_norm, m_mla_w_q_up, m_mla_kv_norm, m_mla_w_kv_up, m_sgu_ln_g, m_sgu_ln_b, m_sgu_w_s, m_sgu_b_s, m_w_branch, m_w_out, m_ln1_g, m_ln1_b, m_ffn_w_in, m_ffn_w_out, m_ln2_g, m_ln2_b, v_w_ada, v_b_ada, v_w_in, v_b_in, v_s5_lambda_re, v_s5_lambda_im, v_s5_log_dt, v_s5_b_re, v_s5_b_im, v_s5_c_re, v_s5_c_im, v_s5_d, v_s5_w_glu, v_s5_b_glu, v_mla_q_norm, v_mla_w_q_up, v_mla_kv_norm, v_mla_w_kv_up, v_sgu_ln_g, v_sgu_ln_b, v_sgu_w_s, v_sgu_b_s, v_w_branch, v_w_out, v_ln1_g, v_ln1_b, v_ffn_w_in, v_ffn_w_out, v_ln2_g, v_ln2_b):
    return _step(dict(locals()))
```

```python
import functools
import math

import numpy as np
import jax
import jax.numpy as jnp
from jax import lax
from jax.experimental import pallas as pl
from jax.experimental.pallas import tpu as pltpu

F32 = jnp.float32
BF16 = jnp.bfloat16

N_DEV = 8
D_MODEL = 1024
DEPTH = 4
CHUNK = 64
S5_WIDTH = 512
S5_GROUP = 16
S5_GROUPS = 32
S5_STATE = 64
MLA_HEADS = 8
QK_NOPE = 64
QK_ROPE = 32
V_HEAD = 64
Q_LORA = 384
KV_LORA = 256
ROPE_THETA = 10000.0
SGU_WIDTH = 512
SGU_GROUPS = 4
SGU_CHUNK = 128
FF_HIDDEN = 2816
DEEPNORM_ALPHA = (2 * DEPTH) ** 0.25
LN_EPS = 1e-5
RMS_EPS = 1e-6
NEG_INF = -1e30
ADAM_LR = 0.001
ADAM_B1 = 0.9
ADAM_B2 = 0.999
ADAM_EPS = 1e-08
ADAM_WD = 0.01
ADAM_STEP = 10

IN_WIDTH = 5280
IN_PAD = 5376
_O_S5, _O_CQ, _O_CKV, _O_KPE, _O_USGU, _O_VSGU, _O_GATE = 0, 512, 896, 1152, 1184, 1696, 2208
_IN_SEGMENTS = ((_O_GATE, IN_WIDTH), (_O_S5, _O_CQ), (_O_USGU, _O_VSGU), (_O_VSGU, _O_GATE), (_O_CQ, _O_USGU))
P_GATE, P_S5, P_USGU, P_VSGU, P_MLA = 0, 3072, 3584, 4096, 4608
MLA_BLK = 768

V7X_LANES = 128
V7X_SUBLANES = 8
VMEM_LIMIT = 48 * 1024 * 1024
ATT_BLOCK = 512
SCAN_LANES = 128
S5_CH = S5_GROUPS * S5_STATE
S5_BD = 4

WNAMES = ['w_ada', 'b_ada', 'w_in', 'b_in', 's5_lambda_re', 's5_lambda_im', 's5_log_dt', 's5_b_re', 's5_b_im',
          's5_c_re', 's5_c_im', 's5_d', 's5_w_glu', 's5_b_glu', 'mla_q_norm', 'mla_w_q_up', 'mla_kv_norm',
          'mla_w_kv_up', 'sgu_ln_g', 'sgu_ln_b', 'sgu_w_s', 'sgu_b_s', 'w_branch', 'w_out', 'ln1_g', 'ln1_b',
          'ffn_w_in', 'ffn_w_out', 'ln2_g', 'ln2_b']
SHARDED = (('w_in', 'col'), ('s5_w_glu', 'row'), ('mla_w_q_up', 'col'), ('mla_w_kv_up', 'col'),
           ('w_branch', 'col3'), ('w_out', 'row'), ('ffn_w_in', 'col'), ('ffn_w_out', 'row'))
SMALL = [n for n in WNAMES if n != 'w_ada' and n not in dict(SHARDED)]
SMALL_PAD = N_DEV * V7X_SUBLANES * 1024


def _cparams(*sem):
    return pltpu.CompilerParams(dimension_semantics=sem, vmem_limit_bytes=VMEM_LIMIT)


def _pick(n, target):
    if n <= target:
        return n
    best = None
    for d in range(V7X_LANES, target + 1, V7X_LANES):
        if n % d == 0:
            best = d
    assert best is not None, (n, target)
    return best


def _pick_rows(n, target):
    if n <= target:
        return n
    best = None
    for d in range(V7X_SUBLANES, target + 1, V7X_SUBLANES):
        if n % d == 0:
            best = d
    assert best is not None, (n, target)
    return best


@jax.custom_vjp
def _bdot(a, b):
    return jnp.dot(a.astype(BF16), b.astype(BF16), preferred_element_type=F32)


def _bdot_fwd(a, b):
    return _bdot(a, b), (a, b)


def _bdot_bwd(res, g):
    a, b = res
    gb = g.astype(BF16)
    da = lax.dot_general(gb, b.astype(BF16), (((1,), (1,)), ((), ())), preferred_element_type=F32)
    db = lax.dot_general(a.astype(BF16), gb, (((0,), (0,)), ((), ())), preferred_element_type=F32)
    return da.astype(a.dtype), db.astype(b.dtype)


_bdot.defvjp(_bdot_fwd, _bdot_bwd)


@functools.partial(jax.custom_vjp, nondiff_argnums=(1,))
def _lane_roll(x, shift):
    return pltpu.roll(x, shift % x.shape[1], 1)


def _lane_roll_fwd(x, shift):
    return _lane_roll(x, shift), None


def _lane_roll_bwd(shift, _, g):
    return (_lane_roll(g, -shift),)


_lane_roll.defvjp(_lane_roll_fwd, _lane_roll_bwd)


def _sigmoid(x):
    return 1.0 / (1.0 + jnp.exp(-x))


def _gelu(x):
    return 0.5 * x * (1.0 + jnp.tanh(math.sqrt(2.0 / math.pi) * (x + 0.044715 * (x * x * x))))


def _layer_norm(x, g, b):
    mu = jnp.mean(x, axis=-1, keepdims=True)
    var = jnp.mean(jnp.square(x - mu), axis=-1, keepdims=True)
    return (x - mu) * lax.rsqrt(var + LN_EPS) * g + b


def _rms_norm(x, g):
    return x * lax.rsqrt(jnp.mean(x * x, axis=-1, keepdims=True) + RMS_EPS) * g


def _rope(x, c, s1, s2):
    half = QK_ROPE // 2
    return x * c + _lane_roll(x, -half) * s1 + _lane_roll(x, half) * s2


def _modulate_fn(x, scale_row, shift_row):
    return (x * scale_row + shift_row,)


def _ln_res_fn(x, y, gate_row, g, b):
    return (_layer_norm(DEEPNORM_ALPHA * x + gate_row * y, g, b),)


def _s5_post_fn(ylin, u, d, w_glu, b_glu):
    z = _gelu(ylin + d * u)
    return (z * _sigmoid(_bdot(z, w_glu) + b_glu),)


def _mla_pre_fn(blk, cq_t, sq1, sq2, ck_t, sk1, sk2, q_norm, w_q, kv_norm, w_kv):
    cq, ckv, kpe = blk[:, :Q_LORA], blk[:, Q_LORA:Q_LORA + KV_LORA], blk[:, Q_LORA + KV_LORA:]
    q = _rope(_bdot(_rms_norm(cq, q_norm), w_q), cq_t, sq1, sq2) * Q_SCALE
    kv = _bdot(_rms_norm(ckv, kv_norm), w_kv)
    return q, kv, _rope(kpe, ck_t, sk1, sk2)


def _sgu_fn(u, v, g, b, wm, bias):
    vn = _layer_norm(_gelu(v), g, b)
    w = SGU_CHUNK
    parts = [_bdot(wm[k * w:(k + 1) * w, :], vn[:, k * w:(k + 1) * w]) for k in range(SGU_GROUPS)]
    return (_gelu(u) * (jnp.concatenate(parts, axis=1) + bias),)


def _merge_fn(y0, y1, y2, l0, l1, l2, wb):
    n = S5_WIDTH
    return (_sigmoid(l0) * _bdot(y0, wb[:n]) + _sigmoid(l1) * _bdot(y1, wb[n:2 * n])
            + _sigmoid(l2) * _bdot(y2, wb[2 * n:]),)


def _swiglu_fn(a, b):
    return (a * _sigmoid(a) * b,)


def _rowcall(fn, rows, fulls, out_rows, out_reds=(), *, tm, name):
    rows = [r if isinstance(r, tuple) else (r, r.shape[1], 0) for r in rows]
    t = rows[0][0].shape[0]
    tm = _pick_rows(t, tm)
    n_in, n_or, n_red = len(rows) + len(fulls), len(out_rows), len(out_reds)

    def body(*refs):
        vals = fn(*[r[...] for r in refs[:n_in]])
        assert len(vals) == n_or + n_red, (name, len(vals))
        for ref, v in zip(refs[n_in:n_in + n_or], vals[:n_or]):
            ref[...] = v.astype(ref.dtype)
        if n_red:
            red_refs = refs[n_in + n_or:]

            @pl.when(pl.program_id(0) == 0)
            def _():
                for ref in red_refs:
                    ref[...] = jnp.zeros_like(ref)

            for ref, v in zip(red_refs, vals[n_or:]):
                ref[...] += v.astype(ref.dtype)

    in_specs = [pl.BlockSpec((tm, w), functools.partial(lambda i, blk: (i, blk), blk=blk)) for _, w, blk in rows]
    in_specs += [pl.BlockSpec(f.shape, lambda i: (0, 0)) for f in fulls]
    out_specs = [pl.BlockSpec((tm, c), lambda i: (i, 0)) for c, _ in out_rows]
    out_specs += [pl.BlockSpec(s, lambda i: (0, 0)) for s, _ in out_reds]
    out_shape = [jax.ShapeDtypeStruct((t, c), dt) for c, dt in out_rows]
    out_shape += [jax.ShapeDtypeStruct(s, dt) for s, dt in out_reds]
    return pl.pallas_call(
        body, name=name, grid=(t // tm,), in_specs=in_specs, out_specs=out_specs, out_shape=out_shape,
        compiler_params=_cparams("arbitrary" if n_red else "parallel"),
    )(*[r[0] for r in rows], *fulls)


def _rowcall_vjp(fn, rows, fulls, cots, diff_rows, diff_fulls, *, tm, name, row_dtypes=None):
    rows_n = [r if isinstance(r, tuple) else (r, r.shape[1], 0) for r in rows]
    n_r, n_c = len(rows), len(cots)
    row_dtypes = row_dtypes or [F32] * len(diff_rows)

    def fn2(*vals):
        r = [v.astype(F32) for v in vals[:n_r]]
        ct = vals[n_r:n_r + n_c]
        f = [v.astype(F32) for v in vals[n_r + n_c:]]

        def g(*dargs):
            rr, ff = list(r), list(f)
            for k, idx in enumerate(diff_rows):
                rr[idx] = dargs[k]
            for k, idx in enumerate(diff_fulls):
                ff[idx] = dargs[len(diff_rows) + k]
            return fn(*rr, *ff)

        prim = [r[i] for i in diff_rows] + [f[i] for i in diff_fulls]
        outs, pull = jax.vjp(g, *prim)
        return pull(tuple(c.astype(o.dtype) for c, o in zip(ct, outs)))

    out_rows = [(rows_n[i][1], dt) for i, dt in zip(diff_rows, row_dtypes)]
    out_reds = [(fulls[i].shape, F32) for i in diff_fulls]
    return _rowcall(fn2, list(rows) + list(cots), fulls, out_rows, out_reds, tm=tm, name=name)


def _mm(a, b, *, ta=False, tb=False, bias=None, out_dtype=F32, name, tm=1024, tn=1024, tk=1024):
    (k_a, m) = a.shape if ta else a.shape[::-1]
    (n, k_b) = b.shape if tb else b.shape[::-1]
    assert k_a == k_b, (name, a.shape, b.shape)
    tm, tn, tk = _pick(m, tm) if m % V7X_LANES == 0 else m, _pick(n, tn), _pick(k_a, tk) if k_a % V7X_LANES == 0 else k_a
    nk = k_a // tk
    a_spec = pl.BlockSpec((tk, tm), lambda i, j, k: (k, i)) if ta else pl.BlockSpec((tm, tk), lambda i, j, k: (i, k))
    b_spec = pl.BlockSpec((tn, tk), lambda i, j, k: (j, k)) if tb else pl.BlockSpec((tk, tn), lambda i, j, k: (k, j))
    dims = (((0,) if ta else (1,), (1,) if tb else (0,)), ((), ()))
    has_bias = bias is not None

    def body(*refs):
        a_ref, b_ref = refs[0], refs[1]
        part = lax.dot_general(a_ref[...].astype(BF16), b_ref[...].astype(BF16), dims, preferred_element_type=F32)
        if nk == 1:
            o_ref = refs[-1]
            o_ref[...] = (part + refs[2][...] if has_bias else part).astype(o_ref.dtype)
            return
        o_ref, acc_ref = refs[-2], refs[-1]
        k = pl.program_id(2)

        @pl.when(k == 0)
        def _():
            acc_ref[...] = part

        @pl.when(k > 0)
        def _():
            acc_ref[...] += part

        @pl.when(k == nk - 1)
        def _():
            r = acc_ref[...]
            if has_bias:
                r = r + refs[2][...]
            o_ref[...] = r.astype(o_ref.dtype)

    in_specs = [a_spec, b_spec] + ([pl.BlockSpec((1, tn), lambda i, j, k: (0, j))] if has_bias else [])
    return pl.pallas_call(
        body, name=name, grid=(m // tm, n // tn, nk), in_specs=in_specs,
        out_specs=pl.BlockSpec((tm, tn), lambda i, j, k: (i, j)),
        out_shape=jax.ShapeDtypeStruct((m, n), out_dtype),
        scratch_shapes=[pltpu.VMEM((tm, tn), F32)] if nk > 1 else [],
        compiler_params=_cparams("parallel", "parallel", "arbitrary"),
    )(a, b, *([bias] if has_bias else []))


def _mm_bd(a, b, mode, *, name, tm=512, tk=1024):
    if mode == 'tn':
        t = a.shape[0]
        nb = S5_BD
        ka, kb = a.shape[1] // nb, b.shape[1] // nb
        tk = _pick(t, tk)
        nsteps = t // tk

        def body_tn(a_ref, b_ref, o_ref):
            part = lax.dot_general(a_ref[...].astype(BF16), b_ref[...].astype(BF16), (((0,), (0,)), ((), ())),
                                   preferred_element_type=F32)

            @pl.when(pl.program_id(1) == 0)
            def _():
                o_ref[...] = part

            @pl.when(pl.program_id(1) > 0)
            def _():
                o_ref[...] += part

        return pl.pallas_call(
            body_tn, name=name, grid=(nb, nsteps),
            in_specs=[pl.BlockSpec((tk, ka), lambda k, s: (s, k)), pl.BlockSpec((tk, kb), lambda k, s: (s, k))],
            out_specs=pl.BlockSpec((None, ka, kb), lambda k, s: (k, 0, 0)),
            out_shape=jax.ShapeDtypeStruct((nb, ka, kb), F32), compiler_params=_cparams("parallel", "arbitrary"),
        )(a, b)
    nb, ka, kb = b.shape
    m = a.shape[0]
    tm = _pick(m, tm)
    w_in, w_out = (ka, kb) if mode == 'nn' else (kb, ka)
    dims = (((1,), (0,)), ((), ())) if mode == 'nn' else (((1,), (1,)), ((), ()))

    def body(a_ref, b_ref, o_ref):
        o_ref[...] = lax.dot_general(a_ref[...].astype(BF16), b_ref[...].astype(BF16), dims, preferred_element_type=F32)

    return pl.pallas_call(
        body, name=name, grid=(m // tm, nb),
        in_specs=[pl.BlockSpec((tm, w_in), lambda i, k: (i, k)), pl.BlockSpec((None, ka, kb), lambda i, k: (k, 0, 0))],
        out_specs=pl.BlockSpec((tm, w_out), lambda i, k: (i, k)),
        out_shape=jax.ShapeDtypeStruct((m, nb * w_out), F32), compiler_params=_cparams("parallel", "parallel"),
    )(a, b)


def _seg_order(a):
    t, c = a.shape
    return a.reshape(V7X_SUBLANES, t // V7X_SUBLANES, c).transpose(1, 0, 2).reshape(t, c)


def _time_order(a):
    t, c = a.shape
    return a.reshape(t // V7X_SUBLANES, V7X_SUBLANES, c).transpose(1, 0, 2).reshape(t, c)


def _s5_scan(x, tab, *, reverse, hist=None, name):
    t, width = x.shape
    blk = 2 * SCAN_LANES
    ntile = t // V7X_SUBLANES
    with_da = hist is not None
    ln = SCAN_LANES
    rows8 = V7X_SUBLANES

    def body(*refs):
        if with_da:
            x_ref, tab_ref, h_ref, o_ref, da_ref = refs
        else:
            x_ref, tab_ref, o_ref = refs
        row = lax.broadcasted_iota(jnp.int32, (rows8, ln), 0)
        bcast = lambda k, lo: jnp.broadcast_to(tab_ref[k:k + 1, lo:lo + ln], (rows8, ln))
        ar, ai = bcast(0, 0), bcast(0, ln)
        zero = jnp.zeros((rows8, ln), F32)

        def tile(ref, i):
            r0 = pl.multiple_of(i * rows8, rows8)
            return ref[pl.ds(r0, rows8), :ln], ref[pl.ds(r0, rows8), ln:]

        def put(i, yr, yi):
            r0 = pl.multiple_of(i * rows8, rows8)
            o_ref[pl.ds(r0, rows8), :ln] = yr
            o_ref[pl.ds(r0, rows8), ln:] = yi

        def pass1(n, carry):
            i = (ntile - 1 - n) if reverse else n
            xr, xi = tile(x_ref, i)
            cr, ci = carry
            yr, yi = xr + ar * cr - ai * ci, xi + ar * ci + ai * cr
            put(i, yr, yi)
            return yr, yi

        fr, fi = lax.fori_loop(0, ntile, pass1, (zero, zero), unroll=8)
        for k, step in zip((1, 2, 3), (1, 2, 4)):
            pr, pi = bcast(k, 0), bcast(k, ln)
            shift, keep = (rows8 - step, row < rows8 - step) if reverse else (step, row >= step)
            sr = jnp.where(keep, pltpu.roll(fr, shift, 0), 0.0)
            si = jnp.where(keep, pltpu.roll(fi, shift, 0), 0.0)
            fr, fi = fr + pr * sr - pi * si, fi + pr * si + pi * sr
        shift, keep = (rows8 - 1, row < rows8 - 1) if reverse else (1, row >= 1)
        cr = jnp.where(keep, pltpu.roll(fr, shift, 0), 0.0)
        ci = jnp.where(keep, pltpu.roll(fi, shift, 0), 0.0)
        if with_da:
            lr, li = tile(h_ref, ntile - 1)
            h0r = jnp.where(row >= 1, pltpu.roll(lr, 1, 0), 0.0)
            h0i = jnp.where(row >= 1, pltpu.roll(li, 1, 0), 0.0)

        def pass2(n, carry):
            i = (ntile - 1 - n) if reverse else n
            pr, pi = carry[0], carry[1]
            yr, yi = tile(o_ref, i)
            yr, yi = yr + pr * cr - pi * ci, yi + pr * ci + pi * cr
            put(i, yr, yi)
            new = (pr * ar - pi * ai, pr * ai + pi * ar)
            if not with_da:
                return new
            pr_, pi_ = tile(h_ref, jnp.maximum(i - 1, 0))
            hpr, hpi = jnp.where(i > 0, pr_, h0r), jnp.where(i > 0, pi_, h0i)
            return new + (carry[2] + yr * hpr + yi * hpi, carry[3] + yi * hpr - yr * hpi)

        out = lax.fori_loop(0, ntile, pass2, (ar, ai) + ((zero, zero) if with_da else ()), unroll=8)
        if with_da:
            da_ref[:, :ln] = out[2]
            da_ref[:, ln:] = out[3]

    col = pl.BlockSpec((t, blk), lambda j: (0, j))
    tabs = pl.BlockSpec((rows8, blk), lambda j: (0, j))
    in_specs = [col, tabs] + ([col] if with_da else [])
    out_specs = [col] + ([tabs] if with_da else [])
    out_shape = [jax.ShapeDtypeStruct((t, width), F32)] + ([jax.ShapeDtypeStruct((rows8, width), F32)] if with_da else [])
    res = pl.pallas_call(
        body, name=name, grid=(width // blk,), in_specs=in_specs, out_specs=out_specs, out_shape=out_shape,
        compiler_params=_cparams("parallel"),
    )(x, tab, *([hist] if with_da else []))
    return res if with_da else res[0]


ATT_SCALE = (QK_NOPE + QK_ROPE) ** -0.5
Q_SCALE = ATT_SCALE * math.log2(math.e)
LN2 = math.log(2.0)


def _att_mask(qi, kj, tb):
    qc = (qi * tb + lax.broadcasted_iota(jnp.int32, (tb, tb), 0)) // CHUNK
    kc = (kj * tb + lax.broadcasted_iota(jnp.int32, (tb, tb), 1)) // CHUNK
    return kc <= qc


def _with_side(side, n_main_in, n_main_out, refs, grid_first, grid_last, compute):
    if side is None:
        compute(refs)
        return
    n = side.n
    main = refs[:n_main_in] + refs[n_main_in + n:n_main_in + n + n_main_out] + refs[n_main_in + 2 * n + n_main_out + 3:]
    x_refs = refs[n_main_in:n_main_in + n]
    y_refs = refs[n_main_in + n + n_main_out:n_main_in + 2 * n + n_main_out]
    sems = refs[n_main_in + 2 * n + n_main_out:n_main_in + 2 * n + n_main_out + 3]

    @pl.when(grid_first)
    def _():
        for cp in side.copies(x_refs, y_refs, *sems):
            cp.start()

    compute(main)

    @pl.when(grid_last)
    def _():
        for cp in side.copies(x_refs, y_refs, *sems):
            cp.wait()


def _side_call(body, side, *, name, grid, in_specs, out_specs, out_shape, scratch, args, sem):
    n_out = len(out_shape)
    if side is not None:
        in_specs, args = in_specs + side.specs, list(args) + side.xs
        out_specs, out_shape = out_specs + side.specs, out_shape + side.out_shape
        scratch = side.scratch + scratch
        params = pltpu.CompilerParams(dimension_semantics=("arbitrary",) * len(grid), vmem_limit_bytes=VMEM_LIMIT,
                                      has_side_effects=True)
    else:
        params = _cparams(*sem)
    res = pl.pallas_call(body, name=name, grid=grid, in_specs=in_specs, out_specs=out_specs, out_shape=out_shape,
                         scratch_shapes=scratch, compiler_params=params)(*args)
    return res[:n_out], res[n_out:]


def _attn_fwd(q, k, v, *, name, side=None):
    h, t, dq = q.shape
    dv = v.shape[2]
    tb = min(ATT_BLOCK, t)
    nblk = t // tb
    nt = (((1,), (1,)), ((), ()))

    def compute(refs):
        q_ref, k_ref, v_ref, o_ref, lse_ref = refs
        i = pl.program_id(1)
        qb = q_ref[...]

        def kv_step(j, carry, masked):
            m, l, acc = carry
            r0 = pl.multiple_of(j * tb, tb)
            s = lax.dot_general(qb, k_ref[pl.ds(r0, tb), :], nt, preferred_element_type=F32)
            if masked:
                s = jnp.where(_att_mask(i, j, tb), s, NEG_INF)
            m_new = jnp.maximum(m, jnp.max(s, axis=1, keepdims=True))
            alpha = jnp.exp2(m - m_new)
            p = jnp.exp2(s - m_new)
            l = alpha * l + jnp.sum(p, axis=1, keepdims=True)
            acc = alpha * acc + jnp.dot(p.astype(BF16), v_ref[pl.ds(r0, tb), :], preferred_element_type=F32)
            return m_new, l, acc

        init = (jnp.full((tb, 1), NEG_INF, F32), jnp.zeros((tb, 1), F32), jnp.zeros((tb, dv), F32))
        carry = lax.fori_loop(0, i, functools.partial(kv_step, masked=False), init)
        m, l, acc = kv_step(i, carry, True)
        o_ref[...] = acc / l
        lse_ref[...] = m + jnp.log2(l)

    def body(*refs):
        first = (pl.program_id(0) == 0) & (pl.program_id(1) == 0)
        last = (pl.program_id(0) == h - 1) & (pl.program_id(1) == nblk - 1)
        _with_side(side, 3, 2, refs, first, last, compute)

    return _side_call(
        body, side, name=name, grid=(h, nblk),
        in_specs=[pl.BlockSpec((None, tb, dq), lambda hh, i: (hh, i, 0)),
                  pl.BlockSpec((None, t, dq), lambda hh, i: (hh, 0, 0)),
                  pl.BlockSpec((None, t, dv), lambda hh, i: (hh, 0, 0))],
        out_specs=[pl.BlockSpec((None, tb, dv), lambda hh, i: (hh, i, 0)),
                   pl.BlockSpec((None, tb, 1), lambda hh, i: (hh, i, 0))],
        out_shape=[jax.ShapeDtypeStruct((h, t, dv), F32), jax.ShapeDtypeStruct((h, t, 1), F32)],
        scratch=[], args=[q, k, v], sem=("parallel", "parallel"))


def _attn_bwd(q, k, v, o, lse, do, *, name, side=None):
    h, t, dq_w = q.shape
    dv_w = v.shape[2]
    tb = min(ATT_BLOCK, t)
    nblk = t // tb
    nt = (((1,), (1,)), ((), ()))
    tn = (((0,), (0,)), ((), ()))

    def compute(refs):
        q_ref, k_ref, v_ref, o_ref, lse_ref, do_ref, dq_ref, dk_ref, dv_ref, delta_ref = refs
        j = pl.program_id(1)

        @pl.when(j == 0)
        def _():
            dq_ref[...] = jnp.zeros_like(dq_ref)

            def dstep(i, c):
                r0 = pl.multiple_of(i * tb, tb)
                delta_ref[pl.ds(r0, tb), :] = jnp.sum(do_ref[pl.ds(r0, tb), :].astype(F32) * o_ref[pl.ds(r0, tb), :],
                                                      axis=1, keepdims=True)
                return c

            lax.fori_loop(0, nblk, dstep, 0)

        kb, vb = k_ref[...], v_ref[...]

        def q_step(i, carry, masked):
            dk, dv = carry
            r0 = pl.multiple_of(i * tb, tb)
            qb, dob = q_ref[pl.ds(r0, tb), :], do_ref[pl.ds(r0, tb), :]
            s = lax.dot_general(qb, kb, nt, preferred_element_type=F32)
            if masked:
                s = jnp.where(_att_mask(i, j, tb), s, NEG_INF)
            p = jnp.exp2(s - lse_ref[pl.ds(r0, tb), :])
            dv = dv + lax.dot_general(p.astype(BF16), dob, tn, preferred_element_type=F32)
            dp = lax.dot_general(dob, vb, nt, preferred_element_type=F32)
            ds = (p * (dp - delta_ref[pl.ds(r0, tb), :]) * LN2).astype(BF16)
            dk = dk + lax.dot_general(ds, qb, tn, preferred_element_type=F32)
            dq_ref[pl.ds(r0, tb), :] += jnp.dot(ds, kb, preferred_element_type=F32)
            return dk, dv

        carry = q_step(j, (jnp.zeros((tb, dq_w), F32), jnp.zeros((tb, dv_w), F32)), True)
        dk, dv = lax.fori_loop(j + 1, nblk, functools.partial(q_step, masked=False), carry)
        dk_ref[...] = dk
        dv_ref[...] = dv

    def body(*refs):
        first = (pl.program_id(0) == 0) & (pl.program_id(1) == 0)
        last = (pl.program_id(0) == h - 1) & (pl.program_id(1) == nblk - 1)
        _with_side(side, 6, 3, refs, first, last, compute)

    whole = lambda w: pl.BlockSpec((None, t, w), lambda hh, j: (hh, 0, 0))
    blockj = lambda w: pl.BlockSpec((None, tb, w), lambda hh, j: (hh, j, 0))
    return _side_call(
        body, side, name=name, grid=(h, nblk),
        in_specs=[whole(dq_w), blockj(dq_w), blockj(dv_w), whole(dv_w), whole(1), whole(dv_w)],
        out_specs=[whole(dq_w), blockj(dq_w), blockj(dv_w)],
        out_shape=[jax.ShapeDtypeStruct((h, t, dq_w), F32), jax.ShapeDtypeStruct((h, t, dq_w), F32),
                   jax.ShapeDtypeStruct((h, t, dv_w), F32)],
        scratch=[pltpu.VMEM((t, 1), F32)], args=[q, k, v, o, lse, do], sem=("parallel", "arbitrary"))


class _Exchange:
    def __init__(self, xs, gather):
        self.xs, self.gather, self.n = list(xs), gather, len(xs)
        shapes = [tuple(x.shape) if gather else tuple(x.shape[1:]) for x in xs]
        self.out_shape = [jax.ShapeDtypeStruct((N_DEV,) + shp, x.dtype) for shp, x in zip(shapes, xs)]
        self.specs = [pl.BlockSpec(memory_space=pl.ANY)] * self.n
        self.scratch = [pltpu.SemaphoreType.DMA((self.n, N_DEV - 1)), pltpu.SemaphoreType.DMA((self.n, N_DEV - 1)),
                        pltpu.SemaphoreType.DMA((self.n,))]

    def copies(self, x_refs, y_refs, send_sems, recv_sems, local_sems):
        mx, my, mc = lax.axis_index("x"), lax.axis_index("y"), lax.axis_index("c")
        me = 4 * mx + 2 * my + mc
        out = [pltpu.make_async_copy(x_refs[i] if self.gather else x_refs[i].at[me], y_refs[i].at[me], local_sems.at[i])
               for i in range(self.n)]
        for k in range(1, N_DEV):
            px = 1 - mx if k & 4 else mx
            py = 1 - my if k & 2 else my
            pc = 1 - mc if k & 1 else mc
            for i in range(self.n):
                out.append(pltpu.make_async_remote_copy(
                    src_ref=x_refs[i] if self.gather else x_refs[i].at[4 * px + 2 * py + pc], dst_ref=y_refs[i].at[me],
                    send_sem=send_sems.at[i, k - 1], recv_sem=recv_sems.at[i, k - 1],
                    device_id=(px, py, pc), device_id_type=pl.DeviceIdType.MESH))
        return out


def _exchange(xs, *, gather, name):
    ex = _Exchange(xs, gather)
    n = ex.n

    def body(*refs):
        copies = ex.copies(refs[:n], refs[n:2 * n], *refs[2 * n:])
        for cp in copies:
            cp.start()
        for cp in copies:
            cp.wait()

    return pl.pallas_call(
        body, name=name, out_shape=ex.out_shape, in_specs=ex.specs, out_specs=ex.specs, scratch_shapes=ex.scratch,
        compiler_params=pltpu.CompilerParams(has_side_effects=True),
    )(*ex.xs)


def _adamw(w, gs, m, v, *, name, tm=256):
    nl = len(gs)
    parts = gs[0].ndim == 3
    c = w.shape[1]
    r = w.shape[0] // nl
    tm = _pick_rows(r, tm)
    nrow = r // tm

    def body(*refs):
        w_ref, g_refs, (m_ref, v_ref, go_ref, d_ref, mo_ref, vo_ref) = refs[0], refs[1:1 + nl], refs[1 + nl:]

        def update(g_ref):
            if parts:
                gv = g_ref[0].astype(F32)
                for k in range(1, N_DEV):
                    gv = gv + g_ref[k].astype(F32)
            else:
                gv = g_ref[...]
            mn = ADAM_B1 * m_ref[...] + (1.0 - ADAM_B1) * gv
            vn = ADAM_B2 * v_ref[...] + (1.0 - ADAM_B2) * jnp.square(gv)
            m_hat = mn / (1.0 - ADAM_B1 ** ADAM_STEP)
            v_hat = vn / (1.0 - ADAM_B2 ** ADAM_STEP)
            go_ref[...] = gv
            d_ref[...] = -ADAM_LR * (m_hat / (jnp.sqrt(v_hat) + ADAM_EPS) + ADAM_WD * w_ref[...])
            mo_ref[...] = mn
            vo_ref[...] = vn

        if nl == 1:
            update(g_refs[0])
        else:
            for layer, g_ref in enumerate(g_refs):
                pl.when(pl.program_id(0) == layer)(functools.partial(update, g_ref))

    spec = pl.BlockSpec((tm, c), lambda l, i: (l * nrow + i, 0))

    def gspec(layer):
        row = lambda l, i: jnp.where(l == layer, i, 0)
        if parts:
            return pl.BlockSpec((N_DEV, tm, c), lambda l, i: (0, row(l, i), 0))
        return pl.BlockSpec((tm, c), lambda l, i: (row(l, i), 0))

    return pl.pallas_call(
        body, name=name, grid=(nl, nrow), in_specs=[spec] + [gspec(k) for k in range(nl)] + [spec, spec],
        out_specs=[spec] * 4, out_shape=[jax.ShapeDtypeStruct(w.shape, F32)] * 4,
        compiler_params=_cparams("arbitrary", "arbitrary"),
    )(w, *gs, m, v)


def _sum_parts(x, *, name):
    def body(x_ref, o_ref):
        acc = x_ref[0]
        for k in range(1, N_DEV):
            acc = acc + x_ref[k]
        o_ref[...] = acc

    return pl.pallas_call(body, name=name, out_shape=jax.ShapeDtypeStruct(x.shape[1:], x.dtype))(x)


def _permute_in(a):
    pad = jnp.zeros(a.shape[:-1] + (IN_PAD - IN_WIDTH,), a.dtype)
    return jnp.concatenate([a[..., lo:hi] for lo, hi in _IN_SEGMENTS] + [pad], axis=-1)


def _unpermute_in(a):
    out, pos = {}, 0
    for lo, hi in _IN_SEGMENTS:
        out[lo] = a[..., pos:pos + hi - lo]
        pos += hi - lo
    return jnp.concatenate([out[lo] for lo in sorted(out)], axis=-1)


def _full_from_gathered(g, kind):
    if kind == 'row':
        return g.reshape((-1,) + g.shape[2:])
    if g.shape[-1] % V7X_LANES == 0:
        return jnp.moveaxis(g, 0, -2).reshape(g.shape[1:-1] + (-1,))
    return jnp.concatenate([g[d] for d in range(N_DEV)], axis=-1)


def _contrib_from_full(g, kind):
    if kind == 'row':
        return g.reshape((N_DEV, -1) + g.shape[1:])
    ns = g.shape[-1] // N_DEV
    if ns % V7X_LANES == 0:
        return jnp.moveaxis(g.reshape(g.shape[:-1] + (N_DEV, ns)), -2, 0)
    return jnp.stack([g[..., d * ns:(d + 1) * ns] for d in range(N_DEV)])


def _in_runs(ns):
    runs, pos = [], 0
    for lo, hi in _IN_SEGMENTS:
        for d in range(lo // ns, (hi - 1) // ns + 1):
            a, b = max(lo, d * ns), min(hi, (d + 1) * ns)
            runs.append((d, a - d * ns, b - d * ns, pos))
            pos += b - a
    return runs


def _w_in_from_gathered(g):
    pieces = [g[d][:, a:b] for d, a, b, _ in _in_runs(g.shape[2])]
    pad = jnp.zeros((g.shape[1], IN_PAD - IN_WIDTH), g.dtype)
    return jnp.concatenate(pieces + [pad], axis=1)


def _w_in_contrib(gp):
    ns = IN_WIDTH // N_DEV
    per_dev = [[] for _ in range(N_DEV)]
    for d, a, b, pos in sorted(_in_runs(ns), key=lambda r: (r[0], r[1])):
        per_dev[d].append(gp[:, pos:pos + b - a])
    return jnp.stack([jnp.concatenate(p, axis=1) for p in per_dev])


def _as2d(a, lead=0):
    return a.reshape(a.shape[:lead] + (-1, a.shape[-1]))


def _chan_cols(re, im):
    lead = re.shape[:-1]
    nb = S5_CH // SCAN_LANES
    return jnp.stack([re.reshape(lead + (nb, SCAN_LANES)), im.reshape(lead + (nb, SCAN_LANES))],
                     axis=-2).reshape(lead + (2 * S5_CH,))


def _s5_tables(lam_re, lam_im, log_dt, b_re, b_im, c_re, c_im):
    dt = jnp.exp(log_dt)[:, None]
    mag = jnp.exp(lam_re * dt)
    a_re = mag * jnp.cos(lam_im * dt)
    a_im = mag * jnp.sin(lam_im * dt)
    den = lam_re * lam_re + lam_im * lam_im
    f_re = ((a_re - 1.0) * lam_re + a_im * lam_im) / den
    f_im = (a_im * lam_re - (a_re - 1.0) * lam_im) / den
    bb_re = f_re[..., None] * b_re - f_im[..., None] * b_im
    bb_im = f_re[..., None] * b_im + f_im[..., None] * b_re
    gb = S5_GROUPS // S5_BD
    eye = jnp.eye(gb, dtype=F32)
    blocks = lambda a: a.reshape((S5_BD, gb) + a.shape[1:])

    def cols(re, im):
        shp = (S5_BD, S5_WIDTH // S5_BD, -1, SCAN_LANES)
        return jnp.stack([re.reshape(shp), im.reshape(shp)], axis=-2).reshape(S5_BD, S5_WIDTH // S5_BD, -1)

    flat = lambda a: a.reshape(S5_BD, S5_WIDTH // S5_BD, -1)
    wb_c = cols(flat(jnp.einsum('kgpc,gh->kgchp', blocks(bb_re), eye)), flat(jnp.einsum('kgpc,gh->kgchp', blocks(bb_im), eye)))
    wc_c = cols(flat(jnp.einsum('kgcp,gh->khcgp', blocks(c_re), eye)), -flat(jnp.einsum('kgcp,gh->khcgp', blocks(c_im), eye)))
    a_row = _chan_cols(a_re.reshape(1, S5_CH), a_im.reshape(1, S5_CH))
    return wb_c, wc_c.transpose(0, 2, 1), a_row


def _scan_tables(a_row, conj, seg_len):
    nb = S5_CH // SCAN_LANES
    a = a_row.reshape(nb, 2, SCAN_LANES)
    base = (a[:, 0], -a[:, 1] if conj else a[:, 1])
    mul = lambda x, y: (x[0] * y[0] - x[1] * y[1], x[0] * y[1] + x[1] * y[0])
    seg, sq, e = None, base, seg_len
    while e:
        if e & 1:
            seg = sq if seg is None else mul(seg, sq)
        sq, e = mul(sq, sq), e >> 1
    seg2 = mul(seg, seg)
    rows = [base, seg, seg2, mul(seg2, seg2)]
    lay = lambda z: jnp.stack([z[0], z[1]], axis=1).reshape(-1)
    return jnp.stack([lay(z) for z in rows] + [jnp.zeros((2 * S5_CH,), F32)] * (V7X_SUBLANES - len(rows)))


def _rope_tables(t):
    half = QK_ROPE // 2
    inv_freq = 1.0 / (ROPE_THETA ** (jnp.arange(0, QK_ROPE, 2, dtype=F32) / QK_ROPE))
    ang = jnp.arange(t, dtype=F32)[:, None] * inv_freq[None, :]
    cos, sin = jnp.cos(ang), jnp.sin(ang)
    zero = jnp.zeros_like(sin)

    def lay(nope, width, first, second):
        head = jnp.concatenate([jnp.full((t, nope), 1.0 if first is cos else 0.0, F32), first, second], axis=1)
        reps = width // head.shape[1]
        out = jnp.tile(head, (1, reps))
        return jnp.pad(out, ((0, 0), (0, width - out.shape[1])))

    hq = MLA_HEADS * (QK_NOPE + QK_ROPE)
    q_tabs = (lay(QK_NOPE, hq, cos, cos), lay(QK_NOPE, hq, -sin, zero), lay(QK_NOPE, hq, zero, sin))
    k_tabs = (lay(0, V7X_LANES, cos, cos)[:, :V7X_LANES] * (jnp.arange(V7X_LANES) < QK_ROPE),
              lay(0, V7X_LANES, -sin, zero) * (jnp.arange(V7X_LANES) < QK_ROPE),
              lay(0, V7X_LANES, zero, sin) * (jnp.arange(V7X_LANES) < QK_ROPE))
    return q_tabs, k_tabs


def _sgu_tables(w_s, b_s):
    pos = jnp.arange(SGU_CHUNK) // CHUNK
    mask = pos[None, :] <= pos[:, None]
    wm = jnp.where(mask[None], w_s, 0.0).reshape(SGU_GROUPS * SGU_CHUNK, SGU_CHUNK)
    bias = jnp.repeat(b_s.T, SGU_WIDTH // SGU_GROUPS, axis=1)
    return wm, bias


def _row(v):
    return v.reshape(1, -1)


_S5_PARAMS = ('s5_lambda_re', 's5_lambda_im', 's5_log_dt', 's5_b_re', 's5_b_im', 's5_c_re', 's5_c_im')


def _derived_tables(p, t):
    (wb, wc, a_row), s5_pull = jax.vjp(jax.vmap(_s5_tables), *[p[n] for n in _S5_PARAMS])
    (wm, bias), sgu_pull = jax.vjp(jax.vmap(_sgu_tables), p['sgu_w_s'], p['sgu_b_s'])
    tab_f = jax.vmap(lambda a: _scan_tables(a, False, t // V7X_SUBLANES))(a_row)
    tab_b = jax.vmap(lambda a: _scan_tables(a, True, t // V7X_SUBLANES))(a_row)
    wb, wc = wb.astype(BF16), wc.astype(BF16)
    per_layer = [dict(s5_wb=wb[l], s5_wc=wc[l], s5_tab_fwd=tab_f[l], s5_tab_bwd=tab_b[l], sgu_wm=wm[l], sgu_bias=bias[l])
                 for l in range(len(wm))]

    def pull(grads):
        stacked = lambda k: jnp.stack([g[k] for g in grads])
        out = dict(zip(_S5_PARAMS, s5_pull((stacked('s5_wb'), stacked('s5_wc'), stacked('s5_a')))))
        out['sgu_w_s'], out['sgu_b_s'] = sgu_pull((stacked('sgu_wm'), stacked('sgu_bias')))
        return out

    return per_layer, pull


def _layer_fwd(x, ada, w, rope_tabs, tag, side=None, after_attn=None):
    s = {'x': x}
    q_tabs, k_tabs = rope_tabs
    sc1, gt1, sc2, gt2 = _row(1.0 + ada[1]), _row(1.0 + ada[2]), _row(1.0 + ada[4]), _row(1.0 + ada[5])
    s.update(sc1=sc1, gt1=gt1, sc2=sc2, gt2=gt2)
    (h,) = _rowcall(_modulate_fn, [x], [sc1, _row(ada[0])], [(D_MODEL, BF16)], tm=512, name=f"mod1_{tag}")
    proj = _mm(h, w['w_in_p'], bias=w['b_in_p'], name=f"proj_{tag}", tn=1792)
    s.update(h=h, proj=proj)

    u_view = (proj, S5_WIDTH, P_S5 // S5_WIDTH)
    u_seg = _seg_order(proj[:, P_S5:P_S5 + S5_WIDTH])
    bu = _mm_bd(u_seg, w['s5_wb'], 'nn', name=f"s5_bu_{tag}")
    hs = _s5_scan(bu, w['s5_tab_fwd'], reverse=False, name=f"s5_scan_{tag}")
    ylin = _time_order(_mm_bd(hs, w['s5_wc'], 'nn', name=f"s5_c_{tag}"))
    s5_full = [_row(w['s5_d']), w['s5_w_glu'], _row(w['s5_b_glu'])]
    (y_s5,) = _rowcall(_s5_post_fn, [ylin, u_view], s5_full, [(S5_WIDTH, BF16)], tm=256, name=f"s5_post_{tag}")
    s.update(u_seg=u_seg, hs=hs, ylin=ylin, y_s5=y_s5)

    mla_rows = [(proj, MLA_BLK, P_MLA // MLA_BLK), *q_tabs, *k_tabs]
    mla_full = [_row(w['mla_q_norm']), w['mla_w_q_up'], _row(w['mla_kv_norm']), w['mla_w_kv_up']]
    hq, hkv = MLA_HEADS * (QK_NOPE + QK_ROPE), MLA_HEADS * (QK_NOPE + V_HEAD)
    q_r, kv, kpe_r = _rowcall(_mla_pre_fn, mla_rows, mla_full, [(hq, BF16), (hkv, BF16), (V7X_LANES, BF16)],
                              tm=256, name=f"mla_pre_{tag}")
    t = x.shape[0]
    qh = q_r.reshape(t, MLA_HEADS, -1).transpose(1, 0, 2)
    kv3 = kv.reshape(t, MLA_HEADS, -1).transpose(1, 0, 2)
    kh = jnp.concatenate([kv3[:, :, :QK_NOPE], jnp.broadcast_to(kpe_r[None, :, :QK_ROPE], (MLA_HEADS, t, QK_ROPE))], axis=2)
    vh = kv3[:, :, QK_NOPE:]
    (o, lse), side_out = _attn_fwd(qh, kh, vh, name=f"attn_fwd_{tag}", side=side)
    if after_attn is not None:
        after_attn(side_out)
    y_mla = o.transpose(1, 0, 2).reshape(t, -1).astype(BF16)
    s.update(qh=qh, kh=kh, vh=vh, o=o, lse=lse, y_mla=y_mla)

    sgu_rows = [(proj, SGU_WIDTH, P_USGU // SGU_WIDTH), (proj, SGU_WIDTH, P_VSGU // SGU_WIDTH)]
    sgu_full = [_row(w['sgu_ln_g']), _row(w['sgu_ln_b']), w['sgu_wm'], w['sgu_bias']]
    (y_sgu,) = _rowcall(_sgu_fn, sgu_rows, sgu_full, [(SGU_WIDTH, BF16)], tm=SGU_CHUNK, name=f"sgu_{tag}")
    s.update(sgu_full=sgu_full, y_sgu=y_sgu)

    wbr = w['w_branch'].reshape(-1, D_MODEL)
    gate_rows = [(proj, D_MODEL, b) for b in range(3)]
    (merged,) = _rowcall(_merge_fn, [y_s5, y_mla, y_sgu] + gate_rows, [wbr], [(D_MODEL, BF16)], tm=256, name=f"merge_{tag}")
    ymix = _mm(merged, w['w_out'], name=f"wout_{tag}")
    (x1,) = _rowcall(_ln_res_fn, [x, ymix], [gt1, _row(w['ln1_g']), _row(w['ln1_b'])], [(D_MODEL, F32)], tm=256,
                     name=f"ln1_{tag}")
    s.update(merged=merged, ymix=ymix, x1=x1)

    (h2,) = _rowcall(_modulate_fn, [x1], [sc2, _row(ada[3])], [(D_MODEL, BF16)], tm=512, name=f"mod2_{tag}")
    ab = _mm(h2, w['ffn_w_in'], name=f"ffn_in_{tag}", tn=1408)
    (act,) = _rowcall(_swiglu_fn, [(ab, FF_HIDDEN, 0), (ab, FF_HIDDEN, 1)], [], [(FF_HIDDEN, BF16)], tm=256,
                      name=f"swiglu_{tag}")
    f = _mm(act, w['ffn_w_out'], name=f"ffn_out_{tag}", tk=2816)
    (x2,) = _rowcall(_ln_res_fn, [x1, f], [gt2, _row(w['ln2_g']), _row(w['ln2_b'])], [(D_MODEL, F32)], tm=256,
                     name=f"ln2_{tag}")
    s.update(h2=h2, ab=ab, act=act, f=f)
    return x2, s, side_out


def _mod_bwd_fn(x, dh, dxa, scale_row):
    return (dxa + dh * scale_row, jnp.sum(dh * x, axis=0, keepdims=True), jnp.sum(dh, axis=0, keepdims=True))


def _layer_bwd(dx2, s, w, rope_tabs, tag, make_side=None):
    g = {}
    q_tabs, k_tabs = rope_tabs
    t = dx2.shape[0]
    ln_full = lambda gt, a, b: [gt, _row(w[a]), _row(w[b])]

    dx1_a, df, dgt2, g['ln2_g'], g['ln2_b'] = _rowcall_vjp(
        _ln_res_fn, [s['x1'], s['f']], ln_full(s['gt2'], 'ln2_g', 'ln2_b'), [dx2], [0, 1], [0, 1, 2],
        tm=256, name=f"ln2_bwd_{tag}", row_dtypes=[F32, BF16])
    dact = _mm(df, w['ffn_w_out'], tb=True, name=f"ffn_out_dx_{tag}", tn=1408)
    g['ffn_w_out'] = _mm(s['act'], df, ta=True, name=f"ffn_out_dw_{tag}", tm=1408)
    da_, db_ = _rowcall_vjp(_swiglu_fn, [(s['ab'], FF_HIDDEN, 0), (s['ab'], FF_HIDDEN, 1)], [], [dact], [0, 1], [],
                            tm=256, name=f"swiglu_bwd_{tag}", row_dtypes=[BF16, BF16])
    dab = jnp.concatenate([da_, db_], axis=1)
    dh2 = _mm(dab, w['ffn_w_in'], tb=True, name=f"ffn_in_dx_{tag}", tk=1408)
    g['ffn_w_in'] = _mm(s['h2'], dab, ta=True, name=f"ffn_in_dw_{tag}", tn=1408)
    dx1, dsc2, dsh2 = _rowcall(_mod_bwd_fn, [s['x1'], dh2, dx1_a], [s['sc2']], [(D_MODEL, F32)],
                               [((1, D_MODEL), F32)] * 2, tm=256, name=f"mod2_bwd_{tag}")

    dx_a, dymix, dgt1, g['ln1_g'], g['ln1_b'] = _rowcall_vjp(
        _ln_res_fn, [s['x'], s['ymix']], ln_full(s['gt1'], 'ln1_g', 'ln1_b'), [dx1], [0, 1], [0, 1, 2],
        tm=256, name=f"ln1_bwd_{tag}", row_dtypes=[F32, BF16])
    dmerged = _mm(dymix, w['w_out'], tb=True, name=f"wout_dx_{tag}")
    g['w_out'] = _mm(s['merged'], dymix, ta=True, name=f"wout_dw_{tag}")

    proj = s['proj']
    wbr = w['w_branch'].reshape(-1, D_MODEL)
    gate_rows = [(proj, D_MODEL, b) for b in range(3)]
    dy_s5, dy_mla, dy_sgu, dl0, dl1, dl2, dwbr = _rowcall_vjp(
        _merge_fn, [s['y_s5'], s['y_mla'], s['y_sgu']] + gate_rows, [wbr], [dmerged], [0, 1, 2, 3, 4, 5], [0],
        tm=256, name=f"merge_bwd_{tag}")
    g['w_branch'] = dwbr.reshape(w['w_branch'].shape)

    sgu_rows = [(proj, SGU_WIDTH, P_USGU // SGU_WIDTH), (proj, SGU_WIDTH, P_VSGU // SGU_WIDTH)]
    du_sgu, dv_sgu, dlg, dlb, dwm, dbias = _rowcall_vjp(
        _sgu_fn, sgu_rows, s['sgu_full'], [dy_sgu], [0, 1], [0, 1, 2, 3], tm=SGU_CHUNK, name=f"sgu_bwd_{tag}")
    g['sgu_ln_g'], g['sgu_ln_b'] = dlg.reshape(-1), dlb.reshape(-1)
    g['sgu_wm'], g['sgu_bias'] = dwm, dbias

    do = dy_mla.reshape(t, MLA_HEADS, V_HEAD).transpose(1, 0, 2).astype(BF16)
    (dqh, dkh, dvh), side_out = _attn_bwd(s['qh'], s['kh'], s['vh'], s['o'], s['lse'], do, name=f"attn_bwd_{tag}",
                                          side=make_side(g) if make_side is not None else None)
    dq_r = dqh.transpose(1, 0, 2).reshape(t, -1)
    dkv = jnp.concatenate([dkh[:, :, :QK_NOPE], dvh], axis=2).transpose(1, 0, 2).reshape(t, -1)
    dkpe = jnp.pad(jnp.sum(dkh[:, :, QK_NOPE:], axis=0), ((0, 0), (0, V7X_LANES - QK_ROPE)))
    mla_rows = [(proj, MLA_BLK, P_MLA // MLA_BLK), *q_tabs, *k_tabs]
    mla_full = [_row(w['mla_q_norm']), w['mla_w_q_up'], _row(w['mla_kv_norm']), w['mla_w_kv_up']]
    dmla, dqn, g['mla_w_q_up'], dkvn, g['mla_w_kv_up'] = _rowcall_vjp(
        _mla_pre_fn, mla_rows, mla_full, [dq_r, dkv, dkpe], [0], [0, 1, 2, 3], tm=256, name=f"mla_pre_bwd_{tag}")
    g['mla_q_norm'], g['mla_kv_norm'] = dqn.reshape(-1), dkvn.reshape(-1)

    s5_full = [_row(w['s5_d']), w['s5_w_glu'], _row(w['s5_b_glu'])]
    dylin, du_a, dd, g['s5_w_glu'], dbg = _rowcall_vjp(
        _s5_post_fn, [s['ylin'], (proj, S5_WIDTH, P_S5 // S5_WIDTH)], s5_full, [dy_s5], [0, 1], [0, 1, 2], tm=256,
        name=f"s5_post_bwd_{tag}", row_dtypes=[BF16, F32])
    g['s5_d'], g['s5_b_glu'] = dd.reshape(-1), dbg.reshape(-1)
    dylin = _seg_order(dylin)
    dhs = _mm_bd(dylin, w['s5_wc'], 'nt', name=f"s5_c_dx_{tag}")
    g['s5_wc'] = _mm_bd(s['hs'], dylin, 'tn', name=f"s5_c_dw_{tag}")
    gs, da_part = _s5_scan(dhs, w['s5_tab_bwd'], reverse=True, hist=s['hs'], name=f"s5_scan_bwd_{tag}")
    du_b = _time_order(_mm_bd(gs, w['s5_wb'], 'nt', name=f"s5_bu_dx_{tag}"))
    g['s5_wb'] = _mm_bd(s['u_seg'], gs, 'tn', name=f"s5_bu_dw_{tag}")
    g['s5_a'] = jnp.sum(da_part, axis=0, keepdims=True)

    dproj_f32 = jnp.concatenate([dl0, dl1, dl2, du_a + du_b, du_sgu, dv_sgu, dmla], axis=1)
    dproj, db_in = _rowcall(lambda d: (d, jnp.sum(d, axis=0, keepdims=True)), [dproj_f32], [], [(IN_PAD, BF16)],
                            [((1, IN_PAD), F32)], tm=256, name=f"dproj_{tag}")
    dh = _mm(dproj, w['w_in_p'], tb=True, name=f"proj_dx_{tag}", tk=1792)
    g['w_in_p'] = _mm(s['h'], dproj, ta=True, name=f"proj_dw_{tag}", tn=896)
    g['b_in'] = _unpermute_in(db_in).reshape(-1)
    dx, dsc1, dsh1 = _rowcall(_mod_bwd_fn, [s['x'], dh, dx_a], [s['sc1']], [(D_MODEL, F32)], [((1, D_MODEL), F32)] * 2,
                              tm=256, name=f"mod1_bwd_{tag}")
    d_ada = jnp.concatenate([dsh1, dsc1, dgt1, dsh2, dsc2, dgt2], axis=0)
    return dx, d_ada, g, side_out


def _loss_fn(y, target):
    err = y - target
    return (err / D_MODEL, 0.5 * jnp.sum(jnp.sum(err * err, axis=1, keepdims=True), axis=0, keepdims=True) / D_MODEL)


def _step(p):
    me = 4 * lax.axis_index("x") + 2 * lax.axis_index("y") + lax.axis_index("c")
    x = p['x'][0]
    t = x.shape[0]
    rope_tabs = _rope_tables(t)

    (c_all,) = _exchange([jnp.broadcast_to(p['c'], (V7X_SUBLANES, D_MODEL))], gather=True, name="gather_c")
    c_all = c_all[:, 0, :]
    (c_act,) = _rowcall(lambda cc: (cc * _sigmoid(cc),), [c_all], [], [(D_MODEL, F32)], tm=N_DEV, name="c_silu")
    ncol = p['w_ada'].shape[2]
    b_ada_loc = lax.dynamic_slice_in_dim(p['b_ada'], me * ncol, ncol, axis=1)
    ada_cols = jnp.concatenate([_mm(c_act, p['w_ada'][l], bias=b_ada_loc[l:l + 1], name=f"ada_{l}") for l in range(DEPTH)])
    (ada_all,) = _exchange([ada_cols], gather=True, name="gather_ada")
    ada_all = ada_all.reshape(N_DEV, DEPTH, N_DEV, ncol)
    ada = lax.dynamic_index_in_dim(ada_all, me, axis=2, keepdims=False)
    ada = ada.transpose(1, 0, 2).reshape(DEPTH, 6, D_MODEL)

    mixer_w, ffn_w = SHARDED[:-2], SHARDED[-2:]

    def shards(l, group):
        return [p[n][l].astype(BF16) for n, _ in group]

    def contribs(g, group):
        return [(_w_in_contrib(g['w_in_p']) if n == 'w_in' else _contrib_from_full(g[n], kind)).astype(BF16)
                for n, kind in group]

    tables, pull_tables = _derived_tables(p, t)

    def mixer_weights(l, gathered):
        w = {n: _full_from_gathered(g, kind) for (n, kind), g in zip(mixer_w[1:], gathered[1:])}
        w['w_in_p'] = _w_in_from_gathered(gathered[0])
        w['b_in_p'] = _row(_permute_in(p['b_in'][l]))
        w.update(tables[l])
        for n in SMALL:
            if n != 'b_ada' and n != 'b_in':
                w[n] = p[n][l]
        return w

    saved, layers = [], []
    gathered = _exchange(shards(0, mixer_w), gather=True, name="gather_w_0")
    for l in range(DEPTH):
        w = mixer_weights(l, gathered)
        side = _Exchange(shards(l, ffn_w) + (shards(l + 1, mixer_w) if l + 1 < DEPTH else []), True)

        def add_ffn(res, w=w):
            for (n, kind), g in zip(ffn_w, res):
                w[n] = _full_from_gathered(g, kind)

        x, s, res = _layer_fwd(x, ada[l], w, rope_tabs, f"l{l}", side, add_ffn)
        gathered = res[len(ffn_w):]
        layers.append(w)
        saved.append(s)
    dy, loss_loc = _rowcall(_loss_fn, [x, p['loss_target'][0]], [], [(D_MODEL, F32)], [((1, 1), F32)], tm=256, name="loss")
    loss = lax.psum(loss_loc[0, 0], ("x", "y", "c"))

    d_ada, grads, landed = [None] * DEPTH, [None] * DEPTH, [[None, None] for _ in range(DEPTH)]
    dx, pending = dy, []
    for l in reversed(range(DEPTH)):
        make_side = lambda g, pending=pending: _Exchange(contribs(g, ffn_w) + pending, False)
        dx, d_ada[l], grads[l], res = _layer_bwd(dx, saved[l], layers[l], rope_tabs, f"l{l}", make_side)
        landed[l][1] = res[:len(ffn_w)]
        if l + 1 < DEPTH:
            landed[l + 1][0] = res[len(ffn_w):]
        pending = contribs(grads[l], mixer_w)
    landed[0][0] = _exchange(pending, gather=False, name="scatter_g_0")
    landed = [list(a) + list(b) for a, b in landed]
    d_ada = jnp.stack(d_ada).reshape(DEPTH, 6 * D_MODEL)

    (d_ada_all,) = _exchange([d_ada], gather=True, name="gather_dada")
    d_ada_cols = lax.dynamic_slice_in_dim(d_ada_all, me * ncol, ncol, axis=2)
    pad_b = ((0, V7X_LANES - N_DEV), (0, 0))
    c_act_p = jnp.pad(c_act, pad_b)
    g_w_ada = jnp.stack([_mm(c_act_p, jnp.pad(d_ada_cols[:, l], pad_b), ta=True, name=f"ada_dw_{l}") for l in range(DEPTH)])

    out = {}
    kinds = ('grad_', 'delta_', 'new_m_', 'new_v_')
    for i, (n, _) in enumerate(SHARDED):
        res = _adamw(_as2d(p[n]), [_as2d(landed[l][i], lead=1) for l in range(DEPTH)], _as2d(p['m_' + n]),
                     _as2d(p['v_' + n]), name=f"adamw_{n}")
        for kind, r in zip(kinds, res):
            out[kind + n] = r.reshape(p[n].shape)

    res = _adamw(_as2d(p['w_ada']), [_as2d(g_w_ada)], _as2d(p['m_w_ada']), _as2d(p['v_w_ada']), name="adamw_ada")
    for kind, r in zip(kinds, res):
        out[kind + 'w_ada'] = r.reshape(p['w_ada'].shape)

    small_g = pull_tables(grads)
    small_g.update({n: jnp.stack([grads[l][n] for l in range(DEPTH)]) for n in SMALL if n not in small_g and n != 'b_ada'})
    small_g['b_ada'] = d_ada
    n_small = sum(int(np.prod(p[n].shape)) for n in SMALL)
    n_pad = -(-n_small // SMALL_PAD) * SMALL_PAD
    flat = jnp.concatenate([small_g[n].reshape(-1) for n in SMALL])
    flat = jnp.pad(flat, (0, n_pad - n_small)).reshape(N_DEV, n_pad // N_DEV // 1024, 1024)
    (landed_small,) = _exchange([flat], gather=False, name="scatter_small")
    (g_all,) = _exchange([_sum_parts(landed_small, name="sum_small")], gather=True, name="gather_small")
    g_all = g_all.reshape(-1)
    off = 0
    for n in SMALL:
        size = int(np.prod(p[n].shape))
        g_n = g_all[off:off + size].reshape(p[n].shape)
        off += size
        res = _adamw(_as2d(p[n]), [_as2d(g_n)], _as2d(p['m_' + n]), _as2d(p['v_' + n]), name=f"adamw_{n}")
        for kind, r in zip(kinds, res):
            out[kind + n] = r.reshape(p[n].shape)

    outs = [loss, dx[None]]
    for kind in ('grad_', 'delta_', 'new_m_', 'new_v_'):
        outs += [out[kind + n] for n in WNAMES]
    return tuple(outs)


def kernel(x, c, w_ada, b_ada, w_in, b_in, s5_lambda_re, s5_lambda_im, s5_log_dt, s5_b_re, s5_b_im, s5_c_re, s5_c_im, s5_d, s5_w_glu, s5_b_glu, mla_q_norm, mla_w_q_up, mla_kv_norm, mla_w_kv_up, sgu_ln_g, sgu_ln_b, sgu_w_s, sgu_b_s, w_branch, w_out, ln1_g, ln1_b, ffn_w_in, ffn_w_out, ln2_g, ln2_b, loss_target, m_w_ada, m_b_ada, m_w_in, m_b_in, m_s5_lambda_re, m_s5_lambda_im, m_s5_log_dt, m_s5_b_re, m_s5_b_im, m_s5_c_re, m_s5_c_im, m_s5_d, m_s5_w_glu, m_s5_b_glu, m_mla_q_norm, m_mla_w_q_up, m_mla_kv_norm, m_mla_w_kv_up, m_sgu_ln_g, m_sgu_ln_b, m_sgu_w_s, m_sgu_b_s, m_w_branch, m_w_out, m_ln1_g, m_ln1_b, m_ffn_w_in, m_ffn_w_out, m_ln2_g, m_ln2_b, v_w_ada, v_b_ada, v_w_in, v_b_in, v_s5_lambda_re, v_s5_lambda_im, v_s5_log_dt, v_s5_b_re, v_s5_b_im, v_s5_c_re, v_s5_c_im, v_s5_d, v_s5_w_glu, v_s5_b_glu, v_mla_q_norm, v_mla_w_q_up, v_mla_kv_norm, v_mla_w_kv_up, v_sgu_ln_g, v_sgu_ln_b, v_sgu_w_s, v_sgu_b_s, v_w_branch, v_w_out, v_ln1_g, v_ln1_b, v_ffn_w_in, v_ffn_w_out, v_ln2_g, v_ln2_b):
    return _step(dict(locals()))
```

```python
import functools
import math

import numpy as np
import jax
import jax.numpy as jnp
from jax import lax
from jax.experimental import pallas as pl
from jax.experimental.pallas import tpu as pltpu

F32 = jnp.float32
BF16 = jnp.bfloat16

N_DEV = 8
D_MODEL = 1024
DEPTH = 4
CHUNK = 64
S5_WIDTH = 512
S5_GROUP = 16
S5_GROUPS = 32
S5_STATE = 64
MLA_HEADS = 8
QK_NOPE = 64
QK_ROPE = 32
V_HEAD = 64
Q_LORA = 384
KV_LORA = 256
ROPE_THETA = 10000.0
SGU_WIDTH = 512
SGU_GROUPS = 4
SGU_CHUNK = 128
FF_HIDDEN = 2816
DEEPNORM_ALPHA = (2 * DEPTH) ** 0.25
LN_EPS = 1e-5
RMS_EPS = 1e-6
NEG_INF = -1e30
ADAM_LR = 0.001
ADAM_B1 = 0.9
ADAM_B2 = 0.999
ADAM_EPS = 1e-08
ADAM_WD = 0.01
ADAM_STEP = 10

IN_WIDTH = 5280
IN_PAD = 5376
_O_S5, _O_CQ, _O_CKV, _O_KPE, _O_USGU, _O_VSGU, _O_GATE = 0, 512, 896, 1152, 1184, 1696, 2208
_IN_SEGMENTS = ((_O_GATE, IN_WIDTH), (_O_S5, _O_CQ), (_O_USGU, _O_VSGU), (_O_VSGU, _O_GATE), (_O_CQ, _O_USGU))
P_GATE, P_S5, P_USGU, P_VSGU, P_MLA = 0, 3072, 3584, 4096, 4608
MLA_BLK = 768

V7X_LANES = 128
V7X_SUBLANES = 8
VMEM_LIMIT = 48 * 1024 * 1024
ATT_BLOCK = 512
SCAN_LANES = 128
S5_CH = S5_GROUPS * S5_STATE
S5_BD = 4

WNAMES = ['w_ada', 'b_ada', 'w_in', 'b_in', 's5_lambda_re', 's5_lambda_im', 's5_log_dt', 's5_b_re', 's5_b_im',
          's5_c_re', 's5_c_im', 's5_d', 's5_w_glu', 's5_b_glu', 'mla_q_norm', 'mla_w_q_up', 'mla_kv_norm',
          'mla_w_kv_up', 'sgu_ln_g', 'sgu_ln_b', 'sgu_w_s', 'sgu_b_s', 'w_branch', 'w_out', 'ln1_g', 'ln1_b',
          'ffn_w_in', 'ffn_w_out', 'ln2_g', 'ln2_b']
SHARDED = (('w_in', 'col'), ('s5_w_glu', 'row'), ('mla_w_q_up', 'col'), ('mla_w_kv_up', 'col'),
           ('w_branch', 'col3'), ('w_out', 'row'), ('ffn_w_in', 'col'), ('ffn_w_out', 'row'))
SMALL = [n for n in WNAMES if n != 'w_ada' and n not in dict(SHARDED)]
SMALL_PAD = N_DEV * V7X_SUBLANES * 1024


def _cparams(*sem):
    return pltpu.CompilerParams(dimension_semantics=sem, vmem_limit_bytes=VMEM_LIMIT)


def _pick(n, target):
    if n <= target:
        return n
    best = None
    for d in range(V7X_LANES, target + 1, V7X_LANES):
        if n % d == 0:
            best = d
    assert best is not None, (n, target)
    return best


def _pick_rows(n, target):
    if n <= target:
        return n
    best = None
    for d in range(V7X_SUBLANES, target + 1, V7X_SUBLANES):
        if n % d == 0:
            best = d
    assert best is not None, (n, target)
    return best


@jax.custom_vjp
def _bdot(a, b):
    return jnp.dot(a.astype(BF16), b.astype(BF16), preferred_element_type=F32)


def _bdot_fwd(a, b):
    return _bdot(a, b), (a, b)


def _bdot_bwd(res, g):
    a, b = res
    gb = g.astype(BF16)
    da = lax.dot_general(gb, b.astype(BF16), (((1,), (1,)), ((), ())), preferred_element_type=F32)
    db = lax.dot_general(a.astype(BF16), gb, (((0,), (0,)), ((), ())), preferred_element_type=F32)
    return da.astype(a.dtype), db.astype(b.dtype)


_bdot.defvjp(_bdot_fwd, _bdot_bwd)


@functools.partial(jax.custom_vjp, nondiff_argnums=(1,))
def _lane_roll(x, shift):
    return pltpu.roll(x, shift % x.shape[1], 1)


def _lane_roll_fwd(x, shift):
    return _lane_roll(x, shift), None


def _lane_roll_bwd(shift, _, g):
    return (_lane_roll(g, -shift),)


_lane_roll.defvjp(_lane_roll_fwd, _lane_roll_bwd)


def _sigmoid(x):
    return 1.0 / (1.0 + jnp.exp(-x))


def _gelu(x):
    return 0.5 * x * (1.0 + jnp.tanh(math.sqrt(2.0 / math.pi) * (x + 0.044715 * (x * x * x))))


def _layer_norm(x, g, b):
    mu = jnp.mean(x, axis=-1, keepdims=True)
    var = jnp.mean(jnp.square(x - mu), axis=-1, keepdims=True)
    return (x - mu) * lax.rsqrt(var + LN_EPS) * g + b


def _rms_norm(x, g):
    return x * lax.rsqrt(jnp.mean(x * x, axis=-1, keepdims=True) + RMS_EPS) * g


def _rope(x, c, s1, s2):
    half = QK_ROPE // 2
    return x * c + _lane_roll(x, -half) * s1 + _lane_roll(x, half) * s2


def _modulate_fn(x, scale_row, shift_row):
    return (x * scale_row + shift_row,)


def _ln_res_fn(x, y, gate_row, g, b):
    return (_layer_norm(DEEPNORM_ALPHA * x + gate_row * y, g, b),)


def _s5_post_fn(ylin, u, d, w_glu, b_glu):
    z = _gelu(ylin + d * u)
    return (z * _sigmoid(_bdot(z, w_glu) + b_glu),)


def _mla_pre_fn(blk, cq_t, sq1, sq2, ck_t, sk1, sk2, q_norm, w_q, kv_norm, w_kv):
    cq, ckv, kpe = blk[:, :Q_LORA], blk[:, Q_LORA:Q_LORA + KV_LORA], blk[:, Q_LORA + KV_LORA:]
    q = _rope(_bdot(_rms_norm(cq, q_norm), w_q), cq_t, sq1, sq2) * Q_SCALE
    kv = _bdot(_rms_norm(ckv, kv_norm), w_kv)
    return q, kv, _rope(kpe, ck_t, sk1, sk2)


def _sgu_fn(u, v, g, b, wm, bias):
    vn = _layer_norm(_gelu(v), g, b)
    w = SGU_CHUNK
    parts = [_bdot(wm[k * w:(k + 1) * w, :], vn[:, k * w:(k + 1) * w]) for k in range(SGU_GROUPS)]
    return (_gelu(u) * (jnp.concatenate(parts, axis=1) + bias),)


def _merge_fn(y0, y1, y2, l0, l1, l2, wb):
    n = S5_WIDTH
    return (_sigmoid(l0) * _bdot(y0, wb[:n]) + _sigmoid(l1) * _bdot(y1, wb[n:2 * n])
            + _sigmoid(l2) * _bdot(y2, wb[2 * n:]),)


def _swiglu_fn(a, b):
    return (a * _sigmoid(a) * b,)


def _rowcall(fn, rows, fulls, out_rows, out_reds=(), *, tm, name):
    rows = [r if isinstance(r, tuple) else (r, r.shape[1], 0) for r in rows]
    t = rows[0][0].shape[0]
    tm = _pick_rows(t, tm)
    n_in, n_or, n_red = len(rows) + len(fulls), len(out_rows), len(out_reds)

    def body(*refs):
        vals = fn(*[r[...] for r in refs[:n_in]])
        assert len(vals) == n_or + n_red, (name, len(vals))
        for ref, v in zip(refs[n_in:n_in + n_or], vals[:n_or]):
            ref[...] = v.astype(ref.dtype)
        if n_red:
            red_refs = refs[n_in + n_or:]

            @pl.when(pl.program_id(0) == 0)
            def _():
                for ref in red_refs:
                    ref[...] = jnp.zeros_like(ref)

            for ref, v in zip(red_refs, vals[n_or:]):
                ref[...] += v.astype(ref.dtype)

    in_specs = [pl.BlockSpec((tm, w), functools.partial(lambda i, blk: (i, blk), blk=blk)) for _, w, blk in rows]
    in_specs += [pl.BlockSpec(f.shape, lambda i: (0, 0)) for f in fulls]
    out_specs = [pl.BlockSpec((tm, c), lambda i: (i, 0)) for c, _ in out_rows]
    out_specs += [pl.BlockSpec(s, lambda i: (0, 0)) for s, _ in out_reds]
    out_shape = [jax.ShapeDtypeStruct((t, c), dt) for c, dt in out_rows]
    out_shape += [jax.ShapeDtypeStruct(s, dt) for s, dt in out_reds]
    return pl.pallas_call(
        body, name=name, grid=(t // tm,), in_specs=in_specs, out_specs=out_specs, out_shape=out_shape,
        compiler_params=_cparams("arbitrary" if n_red else "parallel"),
    )(*[r[0] for r in rows], *fulls)


def _rowcall_vjp(fn, rows, fulls, cots, diff_rows, diff_fulls, *, tm, name, row_dtypes=None):
    rows_n = [r if isinstance(r, tuple) else (r, r.shape[1], 0) for r in rows]
    n_r, n_c = len(rows), len(cots)
    row_dtypes = row_dtypes or [F32] * len(diff_rows)

    def fn2(*vals):
        r = [v.astype(F32) for v in vals[:n_r]]
        ct = vals[n_r:n_r + n_c]
        f = [v.astype(F32) for v in vals[n_r + n_c:]]

        def g(*dargs):
            rr, ff = list(r), list(f)
            for k, idx in enumerate(diff_rows):
                rr[idx] = dargs[k]
            for k, idx in enumerate(diff_fulls):
                ff[idx] = dargs[len(diff_rows) + k]
            return fn(*rr, *ff)

        prim = [r[i] for i in diff_rows] + [f[i] for i in diff_fulls]
        outs, pull = jax.vjp(g, *prim)
        return pull(tuple(c.astype(o.dtype) for c, o in zip(ct, outs)))

    out_rows = [(rows_n[i][1], dt) for i, dt in zip(diff_rows, row_dtypes)]
    out_reds = [(fulls[i].shape, F32) for i in diff_fulls]
    return _rowcall(fn2, list(rows) + list(cots), fulls, out_rows, out_reds, tm=tm, name=name)


def _mm(a, b, *, ta=False, tb=False, bias=None, out_dtype=F32, name, tm=1024, tn=1024, tk=1024):
    (k_a, m) = a.shape if ta else a.shape[::-1]
    (n, k_b) = b.shape if tb else b.shape[::-1]
    assert k_a == k_b, (name, a.shape, b.shape)
    tm, tn, tk = _pick(m, tm) if m % V7X_LANES == 0 else m, _pick(n, tn), _pick(k_a, tk) if k_a % V7X_LANES == 0 else k_a
    nk = k_a // tk
    a_spec = pl.BlockSpec((tk, tm), lambda i, j, k: (k, i)) if ta else pl.BlockSpec((tm, tk), lambda i, j, k: (i, k))
    b_spec = pl.BlockSpec((tn, tk), lambda i, j, k: (j, k)) if tb else pl.BlockSpec((tk, tn), lambda i, j, k: (k, j))
    dims = (((0,) if ta else (1,), (1,) if tb else (0,)), ((), ()))
    has_bias = bias is not None

    def body(*refs):
        a_ref, b_ref = refs[0], refs[1]
        part = lax.dot_general(a_ref[...].astype(BF16), b_ref[...].astype(BF16), dims, preferred_element_type=F32)
        if nk == 1:
            o_ref = refs[-1]
            o_ref[...] = (part + refs[2][...] if has_bias else part).astype(o_ref.dtype)
            return
        o_ref, acc_ref = refs[-2], refs[-1]
        k = pl.program_id(2)

        @pl.when(k == 0)
        def _():
            acc_ref[...] = part

        @pl.when(k > 0)
        def _():
            acc_ref[...] += part

        @pl.when(k == nk - 1)
        def _():
            r = acc_ref[...]
            if has_bias:
                r = r + refs[2][...]
            o_ref[...] = r.astype(o_ref.dtype)

    in_specs = [a_spec, b_spec] + ([pl.BlockSpec((1, tn), lambda i, j, k: (0, j))] if has_bias else [])
    return pl.pallas_call(
        body, name=name, grid=(m // tm, n // tn, nk), in_specs=in_specs,
        out_specs=pl.BlockSpec((tm, tn), lambda i, j, k: (i, j)),
        out_shape=jax.ShapeDtypeStruct((m, n), out_dtype),
        scratch_shapes=[pltpu.VMEM((tm, tn), F32)] if nk > 1 else [],
        compiler_params=_cparams("parallel", "parallel", "arbitrary"),
    )(a, b, *([bias] if has_bias else []))


def _mm_bd(a, b, mode, *, name, tm=512, tk=1024):
    if mode == 'tn':
        t = a.shape[0]
        nb = S5_BD
        ka, kb = a.shape[1] // nb, b.shape[1] // nb
        tk = _pick(t, tk)
        nsteps = t // tk

        def body_tn(a_ref, b_ref, o_ref):
            part = lax.dot_general(a_ref[...].astype(BF16), b_ref[...].astype(BF16), (((0,), (0,)), ((), ())),
                                   preferred_element_type=F32)

            @pl.when(pl.program_id(1) == 0)
            def _():
                o_ref[...] = part

            @pl.when(pl.program_id(1) > 0)
            def _():
                o_ref[...] += part

        return pl.pallas_call(
            body_tn, name=name, grid=(nb, nsteps),
            in_specs=[pl.BlockSpec((tk, ka), lambda k, s: (s, k)), pl.BlockSpec((tk, kb), lambda k, s: (s, k))],
            out_specs=pl.BlockSpec((None, ka, kb), lambda k, s: (k, 0, 0)),
            out_shape=jax.ShapeDtypeStruct((nb, ka, kb), F32), compiler_params=_cparams("parallel", "arbitrary"),
        )(a, b)
    nb, ka, kb = b.shape
    m = a.shape[0]
    tm = _pick(m, tm)
    w_in, w_out = (ka, kb) if mode == 'nn' else (kb, ka)
    dims = (((1,), (0,)), ((), ())) if mode == 'nn' else (((1,), (1,)), ((), ()))

    def body(a_ref, b_ref, o_ref):
        o_ref[...] = lax.dot_general(a_ref[...].astype(BF16), b_ref[...].astype(BF16), dims, preferred_element_type=F32)

    return pl.pallas_call(
        body, name=name, grid=(m // tm, nb),
        in_specs=[pl.BlockSpec((tm, w_in), lambda i, k: (i, k)), pl.BlockSpec((None, ka, kb), lambda i, k: (k, 0, 0))],
        out_specs=pl.BlockSpec((tm, w_out), lambda i, k: (i, k)),
        out_shape=jax.ShapeDtypeStruct((m, nb * w_out), F32), compiler_params=_cparams("parallel", "parallel"),
    )(a, b)


def _seg_order(a):
    t, c = a.shape
    return a.reshape(V7X_SUBLANES, t // V7X_SUBLANES, c).transpose(1, 0, 2).reshape(t, c)


def _time_order(a):
    t, c = a.shape
    return a.reshape(t // V7X_SUBLANES, V7X_SUBLANES, c).transpose(1, 0, 2).reshape(t, c)


def _s5_scan(x, tab, *, reverse, hist=None, name):
    t, width = x.shape
    blk = 2 * SCAN_LANES
    ntile = t // V7X_SUBLANES
    with_da = hist is not None
    ln = SCAN_LANES
    rows8 = V7X_SUBLANES

    def body(*refs):
        if with_da:
            x_ref, tab_ref, h_ref, o_ref, da_ref = refs
        else:
            x_ref, tab_ref, o_ref = refs
        row = lax.broadcasted_iota(jnp.int32, (rows8, ln), 0)
        bcast = lambda k, lo: jnp.broadcast_to(tab_ref[k:k + 1, lo:lo + ln], (rows8, ln))
        ar, ai = bcast(0, 0), bcast(0, ln)
        zero = jnp.zeros((rows8, ln), F32)

        def tile(ref, i):
            r0 = pl.multiple_of(i * rows8, rows8)
            return ref[pl.ds(r0, rows8), :ln], ref[pl.ds(r0, rows8), ln:]

        def put(i, yr, yi):
            r0 = pl.multiple_of(i * rows8, rows8)
            o_ref[pl.ds(r0, rows8), :ln] = yr
            o_ref[pl.ds(r0, rows8), ln:] = yi

        def pass1(n, carry):
            i = (ntile - 1 - n) if reverse else n
            xr, xi = tile(x_ref, i)
            cr, ci = carry
            yr, yi = xr + ar * cr - ai * ci, xi + ar * ci + ai * cr
            put(i, yr, yi)
            return yr, yi

        fr, fi = lax.fori_loop(0, ntile, pass1, (zero, zero), unroll=8)
        for k, step in zip((1, 2, 3), (1, 2, 4)):
            pr, pi = bcast(k, 0), bcast(k, ln)
            shift, keep = (rows8 - step, row < rows8 - step) if reverse else (step, row >= step)
            sr = jnp.where(keep, pltpu.roll(fr, shift, 0), 0.0)
            si = jnp.where(keep, pltpu.roll(fi, shift, 0), 0.0)
            fr, fi = fr + pr * sr - pi * si, fi + pr * si + pi * sr
        shift, keep = (rows8 - 1, row < rows8 - 1) if reverse else (1, row >= 1)
        cr = jnp.where(keep, pltpu.roll(fr, shift, 0), 0.0)
        ci = jnp.where(keep, pltpu.roll(fi, shift, 0), 0.0)
        if with_da:
            lr, li = tile(h_ref, ntile - 1)
            h0r = jnp.where(row >= 1, pltpu.roll(lr, 1, 0), 0.0)
            h0i = jnp.where(row >= 1, pltpu.roll(li, 1, 0), 0.0)

        def pass2(n, carry):
            i = (ntile - 1 - n) if reverse else n
            pr, pi = carry[0], carry[1]
            yr, yi = tile(o_ref, i)
            yr, yi = yr + pr * cr - pi * ci, yi + pr * ci + pi * cr
            put(i, yr, yi)
            new = (pr * ar - pi * ai, pr * ai + pi * ar)
            if not with_da:
                return new
            pr_, pi_ = tile(h_ref, jnp.maximum(i - 1, 0))
            hpr, hpi = jnp.where(i > 0, pr_, h0r), jnp.where(i > 0, pi_, h0i)
            return new + (carry[2] + yr * hpr + yi * hpi, carry[3] + yi * hpr - yr * hpi)

        out = lax.fori_loop(0, ntile, pass2, (ar, ai) + ((zero, zero) if with_da else ()), unroll=8)
        if with_da:
            da_ref[:, :ln] = out[2]
            da_ref[:, ln:] = out[3]

    col = pl.BlockSpec((t, blk), lambda j: (0, j))
    tabs = pl.BlockSpec((rows8, blk), lambda j: (0, j))
    in_specs = [col, tabs] + ([col] if with_da else [])
    out_specs = [col] + ([tabs] if with_da else [])
    out_shape = [jax.ShapeDtypeStruct((t, width), F32)] + ([jax.ShapeDtypeStruct((rows8, width), F32)] if with_da else [])
    res = pl.pallas_call(
        body, name=name, grid=(width // blk,), in_specs=in_specs, out_specs=out_specs, out_shape=out_shape,
        compiler_params=_cparams("parallel"),
    )(x, tab, *([hist] if with_da else []))
    return res if with_da else res[0]


ATT_SCALE = (QK_NOPE + QK_ROPE) ** -0.5
Q_SCALE = ATT_SCALE * math.log2(math.e)
LN2 = math.log(2.0)


def _att_mask(qi, kj, tb):
    qc = (qi * tb + lax.broadcasted_iota(jnp.int32, (tb, tb), 0)) // CHUNK
    kc = (kj * tb + lax.broadcasted_iota(jnp.int32, (tb, tb), 1)) // CHUNK
    return kc <= qc


def _with_side(side, n_main_in, n_main_out, refs, step, nsteps, compute):
    if side is None:
        compute(refs)
        return
    n = side.n
    main = refs[:n_main_in] + refs[n_main_in + n:n_main_in + n + n_main_out] + refs[n_main_in + 2 * n + n_main_out + 3:]
    x_refs = refs[n_main_in:n_main_in + n]
    y_refs = refs[n_main_in + n + n_main_out:n_main_in + 2 * n + n_main_out]
    side_refs = (x_refs, y_refs) + tuple(refs[n_main_in + 2 * n + n_main_out:n_main_in + 2 * n + n_main_out + 3])
    pl.when(step == 0)(functools.partial(side.start, *side_refs))
    compute(main)
    pl.when(step == (3 * nsteps) // 4)(functools.partial(side.relay, *side_refs))
    pl.when(step == nsteps - 1)(functools.partial(side.finish, *side_refs))


def _side_call(body, side, *, name, grid, in_specs, out_specs, out_shape, scratch, args, sem):
    n_out = len(out_shape)
    if side is not None:
        in_specs, args = in_specs + side.specs, list(args) + side.xs
        out_specs, out_shape = out_specs + side.specs, out_shape + side.out_shape
        scratch = side.scratch + scratch
        params = pltpu.CompilerParams(dimension_semantics=("arbitrary",) * len(grid), vmem_limit_bytes=VMEM_LIMIT,
                                      has_side_effects=True)
    else:
        params = _cparams(*sem)
    res = pl.pallas_call(body, name=name, grid=grid, in_specs=in_specs, out_specs=out_specs, out_shape=out_shape,
                         scratch_shapes=scratch, compiler_params=params)(*args)
    return res[:n_out], res[n_out:]


def _attn_fwd(q, k, v, *, name, side=None):
    h, t, dq = q.shape
    dv = v.shape[2]
    tb = min(ATT_BLOCK, t)
    nblk = t // tb
    nt = (((1,), (1,)), ((), ()))

    def compute(refs):
        q_ref, k_ref, v_ref, o_ref, lse_ref = refs
        i = pl.program_id(1)
        qb = q_ref[...]

        def kv_step(j, carry, masked):
            m, l, acc = carry
            r0 = pl.multiple_of(j * tb, tb)
            s = lax.dot_general(qb, k_ref[pl.ds(r0, tb), :], nt, preferred_element_type=F32)
            if masked:
                s = jnp.where(_att_mask(i, j, tb), s, NEG_INF)
            m_new = jnp.maximum(m, jnp.max(s, axis=1, keepdims=True))
            alpha = jnp.exp2(m - m_new)
            p = jnp.exp2(s - m_new)
            l = alpha * l + jnp.sum(p, axis=1, keepdims=True)
            acc = alpha * acc + jnp.dot(p.astype(BF16), v_ref[pl.ds(r0, tb), :], preferred_element_type=F32)
            return m_new, l, acc

        init = (jnp.full((tb, 1), NEG_INF, F32), jnp.zeros((tb, 1), F32), jnp.zeros((tb, dv), F32))
        carry = lax.fori_loop(0, i, functools.partial(kv_step, masked=False), init)
        m, l, acc = kv_step(i, carry, True)
        o_ref[...] = acc / l
        lse_ref[...] = m + jnp.log2(l)

    def body(*refs):
        _with_side(side, 3, 2, refs, pl.program_id(0) * nblk + pl.program_id(1), h * nblk, compute)

    return _side_call(
        body, side, name=name, grid=(h, nblk),
        in_specs=[pl.BlockSpec((None, tb, dq), lambda hh, i: (hh, i, 0)),
                  pl.BlockSpec((None, t, dq), lambda hh, i: (hh, 0, 0)),
                  pl.BlockSpec((None, t, dv), lambda hh, i: (hh, 0, 0))],
        out_specs=[pl.BlockSpec((None, tb, dv), lambda hh, i: (hh, i, 0)),
                   pl.BlockSpec((None, tb, 1), lambda hh, i: (hh, i, 0))],
        out_shape=[jax.ShapeDtypeStruct((h, t, dv), F32), jax.ShapeDtypeStruct((h, t, 1), F32)],
        scratch=[], args=[q, k, v], sem=("parallel", "parallel"))


def _attn_bwd(q, k, v, o, lse, do, *, name, side=None):
    h, t, dq_w = q.shape
    dv_w = v.shape[2]
    tb = min(ATT_BLOCK, t)
    nblk = t // tb
    nt = (((1,), (1,)), ((), ()))
    tn = (((0,), (0,)), ((), ()))

    def compute(refs):
        q_ref, k_ref, v_ref, o_ref, lse_ref, do_ref, dq_ref, dk_ref, dv_ref, delta_ref = refs
        j = pl.program_id(1)

        @pl.when(j == 0)
        def _():
            dq_ref[...] = jnp.zeros_like(dq_ref)

            def dstep(i, c):
                r0 = pl.multiple_of(i * tb, tb)
                delta_ref[pl.ds(r0, tb), :] = jnp.sum(do_ref[pl.ds(r0, tb), :].astype(F32) * o_ref[pl.ds(r0, tb), :],
                                                      axis=1, keepdims=True)
                return c

            lax.fori_loop(0, nblk, dstep, 0)

        kb, vb = k_ref[...], v_ref[...]

        def q_step(i, carry, masked):
            dk, dv = carry
            r0 = pl.multiple_of(i * tb, tb)
            qb, dob = q_ref[pl.ds(r0, tb), :], do_ref[pl.ds(r0, tb), :]
            s = lax.dot_general(qb, kb, nt, preferred_element_type=F32)
            if masked:
                s = jnp.where(_att_mask(i, j, tb), s, NEG_INF)
            p = jnp.exp2(s - lse_ref[pl.ds(r0, tb), :])
            dv = dv + lax.dot_general(p.astype(BF16), dob, tn, preferred_element_type=F32)
            dp = lax.dot_general(dob, vb, nt, preferred_element_type=F32)
            ds = (p * (dp - delta_ref[pl.ds(r0, tb), :]) * LN2).astype(BF16)
            dk = dk + lax.dot_general(ds, qb, tn, preferred_element_type=F32)
            dq_ref[pl.ds(r0, tb), :] += jnp.dot(ds, kb, preferred_element_type=F32)
            return dk, dv

        carry = q_step(j, (jnp.zeros((tb, dq_w), F32), jnp.zeros((tb, dv_w), F32)), True)
        dk, dv = lax.fori_loop(j + 1, nblk, functools.partial(q_step, masked=False), carry)
        dk_ref[...] = dk
        dv_ref[...] = dv

    def body(*refs):
        _with_side(side, 6, 3, refs, pl.program_id(0) * nblk + pl.program_id(1), h * nblk, compute)

    whole = lambda w: pl.BlockSpec((None, t, w), lambda hh, j: (hh, 0, 0))
    blockj = lambda w: pl.BlockSpec((None, tb, w), lambda hh, j: (hh, j, 0))
    return _side_call(
        body, side, name=name, grid=(h, nblk),
        in_specs=[whole(dq_w), blockj(dq_w), blockj(dv_w), whole(dv_w), whole(1), whole(dv_w)],
        out_specs=[whole(dq_w), blockj(dq_w), blockj(dv_w)],
        out_shape=[jax.ShapeDtypeStruct((h, t, dq_w), F32), jax.ShapeDtypeStruct((h, t, dq_w), F32),
                   jax.ShapeDtypeStruct((h, t, dv_w), F32)],
        scratch=[pltpu.VMEM((t, 1), F32)], args=[q, k, v, o, lse, do], sem=("parallel", "arbitrary"))


class _Exchange:
    def __init__(self, xs, gather, two_level=False):
        assert gather or not two_level
        self.xs, self.gather, self.n, self.two_level = list(xs), gather, len(xs), two_level
        shapes = [tuple(x.shape) if gather else tuple(x.shape[1:]) for x in xs]
        self.out_shape = [jax.ShapeDtypeStruct((N_DEV,) + shp, x.dtype) for shp, x in zip(shapes, xs)]
        self.specs = [pl.BlockSpec(memory_space=pl.ANY)] * self.n
        self.scratch = [pltpu.SemaphoreType.DMA((self.n, N_DEV - 1)), pltpu.SemaphoreType.DMA((self.n, N_DEV - 1)),
                        pltpu.SemaphoreType.DMA((self.n,))]

    def copies(self, x_refs, y_refs, send_sems, recv_sems, local_sems):
        mx, my, mc = lax.axis_index("x"), lax.axis_index("y"), lax.axis_index("c")
        me = 4 * mx + 2 * my + mc
        out = [pltpu.make_async_copy(x_refs[i] if self.gather else x_refs[i].at[me], y_refs[i].at[me], local_sems.at[i])
               for i in range(self.n)]
        for k in range(1, N_DEV):
            px = 1 - mx if k & 4 else mx
            py = 1 - my if k & 2 else my
            pc = 1 - mc if k & 1 else mc
            for i in range(self.n):
                out.append(pltpu.make_async_remote_copy(
                    src_ref=x_refs[i] if self.gather else x_refs[i].at[4 * px + 2 * py + pc], dst_ref=y_refs[i].at[me],
                    send_sem=send_sems.at[i, k - 1], recv_sem=recv_sems.at[i, k - 1],
                    device_id=(px, py, pc), device_id_type=pl.DeviceIdType.MESH))
        return out


    def _two_level(self, x_refs, y_refs, send_sems, recv_sems, local_sems):
        mx, my, mc = lax.axis_index("x"), lax.axis_index("y"), lax.axis_index("c")
        sib = (mx, my, 1 - mc)
        chips = [(1 - mx, my), (mx, 1 - my), (1 - mx, 1 - my)]
        idx = lambda px, py, pc: 4 * px + 2 * py + pc
        me = idx(mx, my, mc)

        def rc(i, k, src, block, to):
            return pltpu.make_async_remote_copy(
                src_ref=src, dst_ref=y_refs[i].at[block], send_sem=send_sems.at[i, k], recv_sem=recv_sems.at[i, k],
                device_id=to, device_id_type=pl.DeviceIdType.MESH)

        rng = range(self.n)
        over_ici = [(j, chip, i) for j, chip in enumerate(chips) for i in rng]
        return dict(
            local=lambda: [pltpu.make_async_copy(x_refs[i], y_refs[i].at[me], local_sems.at[i]) for i in rng],
            own=lambda: [rc(i, 0, x_refs[i], me, sib) for i in rng]
            + [rc(i, 1 + j, x_refs[i], me, (*chip, mc)) for j, chip, i in over_ici],
            relay=lambda: [rc(i, 4 + j, y_refs[i].at[idx(*chip, mc)], idx(*chip, mc), sib) for j, chip, i in over_ici],
            landed=lambda: [rc(i, 1 + j, x_refs[i], idx(*chip, mc), sib) for j, chip, i in over_ici],
            last=lambda: [rc(i, 0, x_refs[i], idx(*sib), sib) for i in rng]
            + [rc(i, 4 + j, x_refs[i], idx(*chip, 1 - mc), sib) for j, chip, i in over_ici])

    def start(self, *refs):
        if not self.two_level:
            for cp in self.copies(*refs):
                cp.start()
            return
        plan = self._two_level(*refs)
        for cp in plan['local']() + plan['own']():
            cp.start()

    def relay(self, *refs):
        if not self.two_level:
            return
        plan = self._two_level(*refs)
        for arrived, cp in zip(plan['landed'](), plan['relay']()):
            arrived.wait_recv()
            cp.start()

    def finish(self, *refs):
        if not self.two_level:
            for cp in self.copies(*refs):
                cp.wait()
            return
        plan = self._two_level(*refs)
        for cp in plan['last']():
            cp.wait_recv()
        for cp in plan['own']() + plan['relay']():
            cp.wait_send()
        for cp in plan['local']():
            cp.wait()


def _exchange(xs, *, gather, name, two_level=False):
    ex = _Exchange(xs, gather, two_level)
    n = ex.n

    def body(*refs):
        refs = (refs[:n], refs[n:2 * n]) + tuple(refs[2 * n:])
        ex.start(*refs)
        ex.relay(*refs)
        ex.finish(*refs)

    return pl.pallas_call(
        body, name=name, out_shape=ex.out_shape, in_specs=ex.specs, out_specs=ex.specs, scratch_shapes=ex.scratch,
        compiler_params=pltpu.CompilerParams(has_side_effects=True),
    )(*ex.xs)


def _adamw(w, gs, m, v, *, name, tm=256):
    nl = len(gs)
    parts = gs[0].ndim == 3
    c = w.shape[1]
    r = w.shape[0] // nl
    tm = _pick_rows(r, tm)
    nrow = r // tm

    def body(*refs):
        w_ref, g_refs, (m_ref, v_ref, go_ref, d_ref, mo_ref, vo_ref) = refs[0], refs[1:1 + nl], refs[1 + nl:]

        def update(g_ref):
            if parts:
                gv = g_ref[0].astype(F32)
                for k in range(1, N_DEV):
                    gv = gv + g_ref[k].astype(F32)
            else:
                gv = g_ref[...]
            mn = ADAM_B1 * m_ref[...] + (1.0 - ADAM_B1) * gv
            vn = ADAM_B2 * v_ref[...] + (1.0 - ADAM_B2) * jnp.square(gv)
            m_hat = mn / (1.0 - ADAM_B1 ** ADAM_STEP)
            v_hat = vn / (1.0 - ADAM_B2 ** ADAM_STEP)
            go_ref[...] = gv
            d_ref[...] = -ADAM_LR * (m_hat / (jnp.sqrt(v_hat) + ADAM_EPS) + ADAM_WD * w_ref[...])
            mo_ref[...] = mn
            vo_ref[...] = vn

        if nl == 1:
            update(g_refs[0])
        else:
            for layer, g_ref in enumerate(g_refs):
                pl.when(pl.program_id(0) == layer)(functools.partial(update, g_ref))

    spec = pl.BlockSpec((tm, c), lambda l, i: (l * nrow + i, 0))

    def gspec(layer):
        row = lambda l, i: jnp.where(l == layer, i, 0)
        if parts:
            return pl.BlockSpec((N_DEV, tm, c), lambda l, i: (0, row(l, i), 0))
        return pl.BlockSpec((tm, c), lambda l, i: (row(l, i), 0))

    return pl.pallas_call(
        body, name=name, grid=(nl, nrow), in_specs=[spec] + [gspec(k) for k in range(nl)] + [spec, spec],
        out_specs=[spec] * 4, out_shape=[jax.ShapeDtypeStruct(w.shape, F32)] * 4,
        compiler_params=_cparams("arbitrary", "arbitrary"),
    )(w, *gs, m, v)


def _sum_parts(x, *, name):
    def body(x_ref, o_ref):
        acc = x_ref[0]
        for k in range(1, N_DEV):
            acc = acc + x_ref[k]
        o_ref[...] = acc

    return pl.pallas_call(body, name=name, out_shape=jax.ShapeDtypeStruct(x.shape[1:], x.dtype))(x)


def _permute_in(a):
    pad = jnp.zeros(a.shape[:-1] + (IN_PAD - IN_WIDTH,), a.dtype)
    return jnp.concatenate([a[..., lo:hi] for lo, hi in _IN_SEGMENTS] + [pad], axis=-1)


def _unpermute_in(a):
    out, pos = {}, 0
    for lo, hi in _IN_SEGMENTS:
        out[lo] = a[..., pos:pos + hi - lo]
        pos += hi - lo
    return jnp.concatenate([out[lo] for lo in sorted(out)], axis=-1)


def _full_from_gathered(g, kind):
    if kind == 'row':
        return g.reshape((-1,) + g.shape[2:])
    if g.shape[-1] % V7X_LANES == 0:
        return jnp.moveaxis(g, 0, -2).reshape(g.shape[1:-1] + (-1,))
    return jnp.concatenate([g[d] for d in range(N_DEV)], axis=-1)


def _contrib_from_full(g, kind):
    if kind == 'row':
        return g.reshape((N_DEV, -1) + g.shape[1:])
    ns = g.shape[-1] // N_DEV
    if ns % V7X_LANES == 0:
        return jnp.moveaxis(g.reshape(g.shape[:-1] + (N_DEV, ns)), -2, 0)
    return jnp.stack([g[..., d * ns:(d + 1) * ns] for d in range(N_DEV)])


def _in_runs(ns):
    runs, pos = [], 0
    for lo, hi in _IN_SEGMENTS:
        for d in range(lo // ns, (hi - 1) // ns + 1):
            a, b = max(lo, d * ns), min(hi, (d + 1) * ns)
            runs.append((d, a - d * ns, b - d * ns, pos))
            pos += b - a
    return runs


def _w_in_from_gathered(g):
    pieces = [g[d][:, a:b] for d, a, b, _ in _in_runs(g.shape[2])]
    pad = jnp.zeros((g.shape[1], IN_PAD - IN_WIDTH), g.dtype)
    return jnp.concatenate(pieces + [pad], axis=1)


def _w_in_contrib(gp):
    ns = IN_WIDTH // N_DEV
    per_dev = [[] for _ in range(N_DEV)]
    for d, a, b, pos in sorted(_in_runs(ns), key=lambda r: (r[0], r[1])):
        per_dev[d].append(gp[:, pos:pos + b - a])
    return jnp.stack([jnp.concatenate(p, axis=1) for p in per_dev])


def _as2d(a, lead=0):
    return a.reshape(a.shape[:lead] + (-1, a.shape[-1]))


def _chan_cols(re, im):
    lead = re.shape[:-1]
    nb = S5_CH // SCAN_LANES
    return jnp.stack([re.reshape(lead + (nb, SCAN_LANES)), im.reshape(lead + (nb, SCAN_LANES))],
                     axis=-2).reshape(lead + (2 * S5_CH,))


def _s5_tables(lam_re, lam_im, log_dt, b_re, b_im, c_re, c_im):
    dt = jnp.exp(log_dt)[:, None]
    mag = jnp.exp(lam_re * dt)
    a_re = mag * jnp.cos(lam_im * dt)
    a_im = mag * jnp.sin(lam_im * dt)
    den = lam_re * lam_re + lam_im * lam_im
    f_re = ((a_re - 1.0) * lam_re + a_im * lam_im) / den
    f_im = (a_im * lam_re - (a_re - 1.0) * lam_im) / den
    bb_re = f_re[..., None] * b_re - f_im[..., None] * b_im
    bb_im = f_re[..., None] * b_im + f_im[..., None] * b_re
    gb = S5_GROUPS // S5_BD
    eye = jnp.eye(gb, dtype=F32)
    blocks = lambda a: a.reshape((S5_BD, gb) + a.shape[1:])

    def cols(re, im):
        shp = (S5_BD, S5_WIDTH // S5_BD, -1, SCAN_LANES)
        return jnp.stack([re.reshape(shp), im.reshape(shp)], axis=-2).reshape(S5_BD, S5_WIDTH // S5_BD, -1)

    flat = lambda a: a.reshape(S5_BD, S5_WIDTH // S5_BD, -1)
    wb_c = cols(flat(jnp.einsum('kgpc,gh->kgchp', blocks(bb_re), eye)), flat(jnp.einsum('kgpc,gh->kgchp', blocks(bb_im), eye)))
    wc_c = cols(flat(jnp.einsum('kgcp,gh->khcgp', blocks(c_re), eye)), -flat(jnp.einsum('kgcp,gh->khcgp', blocks(c_im), eye)))
    a_row = _chan_cols(a_re.reshape(1, S5_CH), a_im.reshape(1, S5_CH))
    return wb_c, wc_c.transpose(0, 2, 1), a_row


def _scan_tables(a_row, conj, seg_len):
    nb = S5_CH // SCAN_LANES
    a = a_row.reshape(nb, 2, SCAN_LANES)
    base = (a[:, 0], -a[:, 1] if conj else a[:, 1])
    mul = lambda x, y: (x[0] * y[0] - x[1] * y[1], x[0] * y[1] + x[1] * y[0])
    seg, sq, e = None, base, seg_len
    while e:
        if e & 1:
            seg = sq if seg is None else mul(seg, sq)
        sq, e = mul(sq, sq), e >> 1
    seg2 = mul(seg, seg)
    rows = [base, seg, seg2, mul(seg2, seg2)]
    lay = lambda z: jnp.stack([z[0], z[1]], axis=1).reshape(-1)
    return jnp.stack([lay(z) for z in rows] + [jnp.zeros((2 * S5_CH,), F32)] * (V7X_SUBLANES - len(rows)))


def _rope_tables(t):
    half = QK_ROPE // 2
    inv_freq = 1.0 / (ROPE_THETA ** (jnp.arange(0, QK_ROPE, 2, dtype=F32) / QK_ROPE))
    ang = jnp.arange(t, dtype=F32)[:, None] * inv_freq[None, :]
    cos, sin = jnp.cos(ang), jnp.sin(ang)
    zero = jnp.zeros_like(sin)

    def lay(nope, width, first, second):
        head = jnp.concatenate([jnp.full((t, nope), 1.0 if first is cos else 0.0, F32), first, second], axis=1)
        reps = width // head.shape[1]
        out = jnp.tile(head, (1, reps))
        return jnp.pad(out, ((0, 0), (0, width - out.shape[1])))

    hq = MLA_HEADS * (QK_NOPE + QK_ROPE)
    q_tabs = (lay(QK_NOPE, hq, cos, cos), lay(QK_NOPE, hq, -sin, zero), lay(QK_NOPE, hq, zero, sin))
    k_tabs = (lay(0, V7X_LANES, cos, cos)[:, :V7X_LANES] * (jnp.arange(V7X_LANES) < QK_ROPE),
              lay(0, V7X_LANES, -sin, zero) * (jnp.arange(V7X_LANES) < QK_ROPE),
              lay(0, V7X_LANES, zero, sin) * (jnp.arange(V7X_LANES) < QK_ROPE))
    return q_tabs, k_tabs


def _sgu_tables(w_s, b_s):
    pos = jnp.arange(SGU_CHUNK) // CHUNK
    mask = pos[None, :] <= pos[:, None]
    wm = jnp.where(mask[None], w_s, 0.0).reshape(SGU_GROUPS * SGU_CHUNK, SGU_CHUNK)
    bias = jnp.repeat(b_s.T, SGU_WIDTH // SGU_GROUPS, axis=1)
    return wm, bias


def _row(v):
    return v.reshape(1, -1)


_S5_PARAMS = ('s5_lambda_re', 's5_lambda_im', 's5_log_dt', 's5_b_re', 's5_b_im', 's5_c_re', 's5_c_im')


def _derived_tables(p, t):
    (wb, wc, a_row), s5_pull = jax.vjp(jax.vmap(_s5_tables), *[p[n] for n in _S5_PARAMS])
    (wm, bias), sgu_pull = jax.vjp(jax.vmap(_sgu_tables), p['sgu_w_s'], p['sgu_b_s'])
    tab_f = jax.vmap(lambda a: _scan_tables(a, False, t // V7X_SUBLANES))(a_row)
    tab_b = jax.vmap(lambda a: _scan_tables(a, True, t // V7X_SUBLANES))(a_row)
    wb, wc = wb.astype(BF16), wc.astype(BF16)
    per_layer = [dict(s5_wb=wb[l], s5_wc=wc[l], s5_tab_fwd=tab_f[l], s5_tab_bwd=tab_b[l], sgu_wm=wm[l], sgu_bias=bias[l])
                 for l in range(len(wm))]

    def pull(grads):
        stacked = lambda k: jnp.stack([g[k] for g in grads])
        out = dict(zip(_S5_PARAMS, s5_pull((stacked('s5_wb'), stacked('s5_wc'), stacked('s5_a')))))
        out['sgu_w_s'], out['sgu_b_s'] = sgu_pull((stacked('sgu_wm'), stacked('sgu_bias')))
        return out

    return per_layer, pull


def _layer_fwd(x, ada, w, rope_tabs, tag, side=None, after_attn=None):
    s = {'x': x}
    q_tabs, k_tabs = rope_tabs
    sc1, gt1, sc2, gt2 = _row(1.0 + ada[1]), _row(1.0 + ada[2]), _row(1.0 + ada[4]), _row(1.0 + ada[5])
    s.update(sc1=sc1, gt1=gt1, sc2=sc2, gt2=gt2)
    (h,) = _rowcall(_modulate_fn, [x], [sc1, _row(ada[0])], [(D_MODEL, BF16)], tm=512, name=f"mod1_{tag}")
    proj = _mm(h, w['w_in_p'], bias=w['b_in_p'], name=f"proj_{tag}", tn=1792)
    s.update(h=h, proj=proj)

    u_view = (proj, S5_WIDTH, P_S5 // S5_WIDTH)
    u_seg = _seg_order(proj[:, P_S5:P_S5 + S5_WIDTH])
    bu = _mm_bd(u_seg, w['s5_wb'], 'nn', name=f"s5_bu_{tag}")
    hs = _s5_scan(bu, w['s5_tab_fwd'], reverse=False, name=f"s5_scan_{tag}")
    ylin = _time_order(_mm_bd(hs, w['s5_wc'], 'nn', name=f"s5_c_{tag}"))
    s5_full = [_row(w['s5_d']), w['s5_w_glu'], _row(w['s5_b_glu'])]
    (y_s5,) = _rowcall(_s5_post_fn, [ylin, u_view], s5_full, [(S5_WIDTH, BF16)], tm=256, name=f"s5_post_{tag}")
    s.update(u_seg=u_seg, hs=hs, ylin=ylin, y_s5=y_s5)

    mla_rows = [(proj, MLA_BLK, P_MLA // MLA_BLK), *q_tabs, *k_tabs]
    mla_full = [_row(w['mla_q_norm']), w['mla_w_q_up'], _row(w['mla_kv_norm']), w['mla_w_kv_up']]
    hq, hkv = MLA_HEADS * (QK_NOPE + QK_ROPE), MLA_HEADS * (QK_NOPE + V_HEAD)
    q_r, kv, kpe_r = _rowcall(_mla_pre_fn, mla_rows, mla_full, [(hq, BF16), (hkv, BF16), (V7X_LANES, BF16)],
                              tm=256, name=f"mla_pre_{tag}")
    t = x.shape[0]
    qh = q_r.reshape(t, MLA_HEADS, -1).transpose(1, 0, 2)
    kv3 = kv.reshape(t, MLA_HEADS, -1).transpose(1, 0, 2)
    kh = jnp.concatenate([kv3[:, :, :QK_NOPE], jnp.broadcast_to(kpe_r[None, :, :QK_ROPE], (MLA_HEADS, t, QK_ROPE))], axis=2)
    vh = kv3[:, :, QK_NOPE:]
    (o, lse), side_out = _attn_fwd(qh, kh, vh, name=f"attn_fwd_{tag}", side=side)
    if after_attn is not None:
        after_attn(side_out)
    y_mla = o.transpose(1, 0, 2).reshape(t, -1).astype(BF16)
    s.update(qh=qh, kh=kh, vh=vh, o=o, lse=lse, y_mla=y_mla)

    sgu_rows = [(proj, SGU_WIDTH, P_USGU // SGU_WIDTH), (proj, SGU_WIDTH, P_VSGU // SGU_WIDTH)]
    sgu_full = [_row(w['sgu_ln_g']), _row(w['sgu_ln_b']), w['sgu_wm'], w['sgu_bias']]
    (y_sgu,) = _rowcall(_sgu_fn, sgu_rows, sgu_full, [(SGU_WIDTH, BF16)], tm=SGU_CHUNK, name=f"sgu_{tag}")
    s.update(sgu_full=sgu_full, y_sgu=y_sgu)

    wbr = w['w_branch'].reshape(-1, D_MODEL)
    gate_rows = [(proj, D_MODEL, b) for b in range(3)]
    (merged,) = _rowcall(_merge_fn, [y_s5, y_mla, y_sgu] + gate_rows, [wbr], [(D_MODEL, BF16)], tm=256, name=f"merge_{tag}")
    ymix = _mm(merged, w['w_out'], name=f"wout_{tag}")
    (x1,) = _rowcall(_ln_res_fn, [x, ymix], [gt1, _row(w['ln1_g']), _row(w['ln1_b'])], [(D_MODEL, F32)], tm=256,
                     name=f"ln1_{tag}")
    s.update(merged=merged, ymix=ymix, x1=x1)

    (h2,) = _rowcall(_modulate_fn, [x1], [sc2, _row(ada[3])], [(D_MODEL, BF16)], tm=512, name=f"mod2_{tag}")
    ab = _mm(h2, w['ffn_w_in'], name=f"ffn_in_{tag}", tn=1408)
    (act,) = _rowcall(_swiglu_fn, [(ab, FF_HIDDEN, 0), (ab, FF_HIDDEN, 1)], [], [(FF_HIDDEN, BF16)], tm=256,
                      name=f"swiglu_{tag}")
    f = _mm(act, w['ffn_w_out'], name=f"ffn_out_{tag}", tk=2816)
    (x2,) = _rowcall(_ln_res_fn, [x1, f], [gt2, _row(w['ln2_g']), _row(w['ln2_b'])], [(D_MODEL, F32)], tm=256,
                     name=f"ln2_{tag}")
    s.update(h2=h2, ab=ab, act=act, f=f)
    return x2, s, side_out


def _mod_bwd_fn(x, dh, dxa, scale_row):
    return (dxa + dh * scale_row, jnp.sum(dh * x, axis=0, keepdims=True), jnp.sum(dh, axis=0, keepdims=True))


def _layer_bwd(dx2, s, w, rope_tabs, tag, make_side=None):
    g = {}
    q_tabs, k_tabs = rope_tabs
    t = dx2.shape[0]
    ln_full = lambda gt, a, b: [gt, _row(w[a]), _row(w[b])]

    dx1_a, df, dgt2, g['ln2_g'], g['ln2_b'] = _rowcall_vjp(
        _ln_res_fn, [s['x1'], s['f']], ln_full(s['gt2'], 'ln2_g', 'ln2_b'), [dx2], [0, 1], [0, 1, 2],
        tm=256, name=f"ln2_bwd_{tag}", row_dtypes=[F32, BF16])
    dact = _mm(df, w['ffn_w_out'], tb=True, name=f"ffn_out_dx_{tag}", tn=1408)
    g['ffn_w_out'] = _mm(s['act'], df, ta=True, name=f"ffn_out_dw_{tag}", tm=1408)
    def swiglu_bwd_fn(a, b, d):
        _, pull = jax.vjp(_swiglu_fn, a, b)
        return (jnp.concatenate(pull((d,)), axis=1),)

    (dab,) = _rowcall(swiglu_bwd_fn, [(s['ab'], FF_HIDDEN, 0), (s['ab'], FF_HIDDEN, 1), dact], [], [(2 * FF_HIDDEN, BF16)],
                      tm=256, name=f"swiglu_bwd_{tag}")
    dh2 = _mm(dab, w['ffn_w_in'], tb=True, name=f"ffn_in_dx_{tag}", tk=1408)
    g['ffn_w_in'] = _mm(s['h2'], dab, ta=True, name=f"ffn_in_dw_{tag}", tn=1408)
    dx1, dsc2, dsh2 = _rowcall(_mod_bwd_fn, [s['x1'], dh2, dx1_a], [s['sc2']], [(D_MODEL, F32)],
                               [((1, D_MODEL), F32)] * 2, tm=256, name=f"mod2_bwd_{tag}")

    dx_a, dymix, dgt1, g['ln1_g'], g['ln1_b'] = _rowcall_vjp(
        _ln_res_fn, [s['x'], s['ymix']], ln_full(s['gt1'], 'ln1_g', 'ln1_b'), [dx1], [0, 1], [0, 1, 2],
        tm=256, name=f"ln1_bwd_{tag}", row_dtypes=[F32, BF16])
    dmerged = _mm(dymix, w['w_out'], tb=True, name=f"wout_dx_{tag}")
    g['w_out'] = _mm(s['merged'], dymix, ta=True, name=f"wout_dw_{tag}")

    proj = s['proj']
    wbr = w['w_branch'].reshape(-1, D_MODEL)
    gate_rows = [(proj, D_MODEL, b) for b in range(3)]
    dy_s5, dy_mla, dy_sgu, dl0, dl1, dl2, dwbr = _rowcall_vjp(
        _merge_fn, [s['y_s5'], s['y_mla'], s['y_sgu']] + gate_rows, [wbr], [dmerged], [0, 1, 2, 3, 4, 5], [0],
        tm=256, name=f"merge_bwd_{tag}")
    g['w_branch'] = dwbr.reshape(w['w_branch'].shape)

    sgu_rows = [(proj, SGU_WIDTH, P_USGU // SGU_WIDTH), (proj, SGU_WIDTH, P_VSGU // SGU_WIDTH)]
    du_sgu, dv_sgu, dlg, dlb, dwm, dbias = _rowcall_vjp(
        _sgu_fn, sgu_rows, s['sgu_full'], [dy_sgu], [0, 1], [0, 1, 2, 3], tm=SGU_CHUNK, name=f"sgu_bwd_{tag}")
    g['sgu_ln_g'], g['sgu_ln_b'] = dlg.reshape(-1), dlb.reshape(-1)
    g['sgu_wm'], g['sgu_bias'] = dwm, dbias

    do = dy_mla.reshape(t, MLA_HEADS, V_HEAD).transpose(1, 0, 2).astype(BF16)
    (dqh, dkh, dvh), side_out = _attn_bwd(s['qh'], s['kh'], s['vh'], s['o'], s['lse'], do, name=f"attn_bwd_{tag}",
                                          side=make_side(g) if make_side is not None else None)
    dq_r = dqh.transpose(1, 0, 2).reshape(t, -1)
    dkv = jnp.concatenate([dkh[:, :, :QK_NOPE], dvh], axis=2).transpose(1, 0, 2).reshape(t, -1)
    dkpe = jnp.pad(jnp.sum(dkh[:, :, QK_NOPE:], axis=0), ((0, 0), (0, V7X_LANES - QK_ROPE)))
    mla_rows = [(proj, MLA_BLK, P_MLA // MLA_BLK), *q_tabs, *k_tabs]
    mla_full = [_row(w['mla_q_norm']), w['mla_w_q_up'], _row(w['mla_kv_norm']), w['mla_w_kv_up']]
    dmla, dqn, g['mla_w_q_up'], dkvn, g['mla_w_kv_up'] = _rowcall_vjp(
        _mla_pre_fn, mla_rows, mla_full, [dq_r, dkv, dkpe], [0], [0, 1, 2, 3], tm=256, name=f"mla_pre_bwd_{tag}")
    g['mla_q_norm'], g['mla_kv_norm'] = dqn.reshape(-1), dkvn.reshape(-1)

    s5_full = [_row(w['s5_d']), w['s5_w_glu'], _row(w['s5_b_glu'])]
    dylin, du_a, dd, g['s5_w_glu'], dbg = _rowcall_vjp(
        _s5_post_fn, [s['ylin'], (proj, S5_WIDTH, P_S5 // S5_WIDTH)], s5_full, [dy_s5], [0, 1], [0, 1, 2], tm=256,
        name=f"s5_post_bwd_{tag}", row_dtypes=[BF16, F32])
    g['s5_d'], g['s5_b_glu'] = dd.reshape(-1), dbg.reshape(-1)
    dylin = _seg_order(dylin)
    dhs = _mm_bd(dylin, w['s5_wc'], 'nt', name=f"s5_c_dx_{tag}")
    g['s5_wc'] = _mm_bd(s['hs'], dylin, 'tn', name=f"s5_c_dw_{tag}")
    gs, da_part = _s5_scan(dhs, w['s5_tab_bwd'], reverse=True, hist=s['hs'], name=f"s5_scan_bwd_{tag}")
    du_b = _time_order(_mm_bd(gs, w['s5_wb'], 'nt', name=f"s5_bu_dx_{tag}"))
    g['s5_wb'] = _mm_bd(s['u_seg'], gs, 'tn', name=f"s5_bu_dw_{tag}")
    g['s5_a'] = jnp.sum(da_part, axis=0, keepdims=True)

    def dproj_fn(g0, g1, g2, ua, ub, us, vs, ml):
        d = jnp.concatenate([g0, g1, g2, ua + ub, us, vs, ml], axis=1)
        return d, jnp.sum(d, axis=0, keepdims=True)

    dproj, db_in = _rowcall(dproj_fn, [dl0, dl1, dl2, du_a, du_b, du_sgu, dv_sgu, dmla], [], [(IN_PAD, BF16)],
                            [((1, IN_PAD), F32)], tm=256, name=f"dproj_{tag}")
    dh = _mm(dproj, w['w_in_p'], tb=True, name=f"proj_dx_{tag}", tk=1792)
    g['w_in_p'] = _mm(s['h'], dproj, ta=True, name=f"proj_dw_{tag}", tn=896)
    g['b_in'] = _unpermute_in(db_in).reshape(-1)
    dx, dsc1, dsh1 = _rowcall(_mod_bwd_fn, [s['x'], dh, dx_a], [s['sc1']], [(D_MODEL, F32)], [((1, D_MODEL), F32)] * 2,
                              tm=256, name=f"mod1_bwd_{tag}")
    d_ada = jnp.concatenate([dsh1, dsc1, dgt1, dsh2, dsc2, dgt2], axis=0)
    return dx, d_ada, g, side_out


def _loss_fn(y, target):
    err = y - target
    return (err / D_MODEL, 0.5 * jnp.sum(jnp.sum(err * err, axis=1, keepdims=True), axis=0, keepdims=True) / D_MODEL)


def _step(p):
    me = 4 * lax.axis_index("x") + 2 * lax.axis_index("y") + lax.axis_index("c")
    x = p['x'][0]
    t = x.shape[0]
    rope_tabs = _rope_tables(t)

    (c_all,) = _exchange([jnp.broadcast_to(p['c'], (V7X_SUBLANES, D_MODEL))], gather=True, name="gather_c")
    c_all = c_all[:, 0, :]
    (c_act,) = _rowcall(lambda cc: (cc * _sigmoid(cc),), [c_all], [], [(D_MODEL, F32)], tm=N_DEV, name="c_silu")
    ncol = p['w_ada'].shape[2]
    b_ada_loc = lax.dynamic_slice_in_dim(p['b_ada'], me * ncol, ncol, axis=1)
    ada_cols = jnp.concatenate([_mm(c_act, p['w_ada'][l], bias=b_ada_loc[l:l + 1], name=f"ada_{l}") for l in range(DEPTH)])
    (ada_all,) = _exchange([ada_cols], gather=True, name="gather_ada")
    ada_all = ada_all.reshape(N_DEV, DEPTH, N_DEV, ncol)
    ada = lax.dynamic_index_in_dim(ada_all, me, axis=2, keepdims=False)
    ada = ada.transpose(1, 0, 2).reshape(DEPTH, 6, D_MODEL)

    mixer_w, ffn_w = SHARDED[:-2], SHARDED[-2:]

    def shards(l, group):
        return [p[n][l].astype(BF16) for n, _ in group]

    def contribs(g, group):
        return [(_w_in_contrib(g['w_in_p']) if n == 'w_in' else _contrib_from_full(g[n], kind)).astype(BF16)
                for n, kind in group]

    tables, pull_tables = _derived_tables(p, t)

    def mixer_weights(l, gathered):
        w = {n: _full_from_gathered(g, kind) for (n, kind), g in zip(mixer_w[1:], gathered[1:])}
        w['w_in_p'] = _w_in_from_gathered(gathered[0])
        w['b_in_p'] = _row(_permute_in(p['b_in'][l]))
        w.update(tables[l])
        for n in SMALL:
            if n != 'b_ada' and n != 'b_in':
                w[n] = p[n][l]
        return w

    saved, layers = [], []
    gathered = _exchange(shards(0, mixer_w), gather=True, name="gather_w_0", two_level=True)
    for l in range(DEPTH):
        w = mixer_weights(l, gathered)
        side = _Exchange(shards(l, ffn_w) + (shards(l + 1, mixer_w) if l + 1 < DEPTH else []), True, two_level=True)

        def add_ffn(res, w=w):
            for (n, kind), g in zip(ffn_w, res):
                w[n] = _full_from_gathered(g, kind)

        x, s, res = _layer_fwd(x, ada[l], w, rope_tabs, f"l{l}", side, add_ffn)
        gathered = res[len(ffn_w):]
        layers.append(w)
        saved.append(s)
    dy, loss_loc = _rowcall(_loss_fn, [x, p['loss_target'][0]], [], [(D_MODEL, F32)], [((1, 1), F32)], tm=256, name="loss")
    loss = lax.psum(loss_loc[0, 0], ("x", "y", "c"))

    d_ada, grads, landed = [None] * DEPTH, [None] * DEPTH, [[None, None] for _ in range(DEPTH)]
    dx, pending = dy, []
    for l in reversed(range(DEPTH)):
        make_side = lambda g, pending=pending: _Exchange(contribs(g, ffn_w) + pending, False)
        dx, d_ada[l], grads[l], res = _layer_bwd(dx, saved[l], layers[l], rope_tabs, f"l{l}", make_side)
        landed[l][1] = res[:len(ffn_w)]
        if l + 1 < DEPTH:
            landed[l + 1][0] = res[len(ffn_w):]
        pending = contribs(grads[l], mixer_w)
    landed[0][0] = _exchange(pending, gather=False, name="scatter_g_0")
    landed = [list(a) + list(b) for a, b in landed]
    d_ada = jnp.stack(d_ada).reshape(DEPTH, 6 * D_MODEL)

    (d_ada_all,) = _exchange([d_ada], gather=True, name="gather_dada")
    d_ada_cols = lax.dynamic_slice_in_dim(d_ada_all, me * ncol, ncol, axis=2)
    pad_b = ((0, V7X_LANES - N_DEV), (0, 0))
    c_act_p = jnp.pad(c_act, pad_b)
    g_w_ada = jnp.stack([_mm(c_act_p, jnp.pad(d_ada_cols[:, l], pad_b), ta=True, name=f"ada_dw_{l}") for l in range(DEPTH)])

    out = {}
    kinds = ('grad_', 'delta_', 'new_m_', 'new_v_')
    for i, (n, _) in enumerate(SHARDED):
        res = _adamw(_as2d(p[n]), [_as2d(landed[l][i], lead=1) for l in range(DEPTH)], _as2d(p['m_' + n]),
                     _as2d(p['v_' + n]), name=f"adamw_{n}")
        for kind, r in zip(kinds, res):
            out[kind + n] = r.reshape(p[n].shape)

    res = _adamw(_as2d(p['w_ada']), [_as2d(g_w_ada)], _as2d(p['m_w_ada']), _as2d(p['v_w_ada']), name="adamw_ada")
    for kind, r in zip(kinds, res):
        out[kind + 'w_ada'] = r.reshape(p['w_ada'].shape)

    small_g = pull_tables(grads)
    small_g.update({n: jnp.stack([grads[l][n] for l in range(DEPTH)]) for n in SMALL if n not in small_g and n != 'b_ada'})
    small_g['b_ada'] = d_ada
    n_small = sum(int(np.prod(p[n].shape)) for n in SMALL)
    n_pad = -(-n_small // SMALL_PAD) * SMALL_PAD
    flat = jnp.concatenate([small_g[n].reshape(-1) for n in SMALL])
    flat = jnp.pad(flat, (0, n_pad - n_small)).reshape(N_DEV, n_pad // N_DEV // 1024, 1024)
    (landed_small,) = _exchange([flat], gather=False, name="scatter_small")
    (g_all,) = _exchange([_sum_parts(landed_small, name="sum_small")], gather=True, name="gather_small")
    g_all = g_all.reshape(-1)
    off = 0
    for n in SMALL:
        size = int(np.prod(p[n].shape))
        g_n = g_all[off:off + size].reshape(p[n].shape)
        off += size
        res = _adamw(_as2d(p[n]), [_as2d(g_n)], _as2d(p['m_' + n]), _as2d(p['v_' + n]), name=f"adamw_{n}")
        for kind, r in zip(kinds, res):
            out[kind + n] = r.reshape(p[n].shape)

    outs = [loss, dx[None]]
    for kind in ('grad_', 'delta_', 'new_m_', 'new_v_'):
        outs += [out[kind + n] for n in WNAMES]
    return tuple(outs)


def kernel(x, c, w_ada, b_ada, w_in, b_in, s5_lambda_re, s5_lambda_im, s5_log_dt, s5_b_re, s5_b_im, s5_c_re, s5_c_im, s5_d, s5_w_glu, s5_b_glu, mla_q_norm, mla_w_q_up, mla_kv_norm, mla_w_kv_up, sgu_ln_g, sgu_ln_b, sgu_w_s, sgu_b_s, w_branch, w_out, ln1_g, ln1_b, ffn_w_in, ffn_w_out, ln2_g, ln2_b, loss_target, m_w_ada, m_b_ada, m_w_in, m_b_in, m_s5_lambda_re, m_s5_lambda_im, m_s5_log_dt, m_s5_b_re, m_s5_b_im, m_s5_c_re, m_s5_c_im, m_s5_d, m_s5_w_glu, m_s5_b_glu, m_mla_q_norm, m_mla_w_q_up, m_mla_kv_norm, m_mla_w_kv_up, m_sgu_ln_g, m_sgu_ln_b, m_sgu_w_s, m_sgu_b_s, m_w_branch, m_w_out, m_ln1_g, m_ln1_b, m_ffn_w_in, m_ffn_w_out, m_ln2_g, m_ln2_b, v_w_ada, v_b_ada, v_w_in, v_b_in, v_s5_lambda_re, v_s5_lambda_im, v_s5_log_dt, v_s5_b_re, v_s5_b_im, v_s5_c_re, v_s5_c_im, v_s5_d, v_s5_w_glu, v_s5_b_glu, v_mla_q_norm, v_mla_w_q_up, v_mla_kv_norm, v_mla_w_kv_up, v_sgu_ln_g, v_sgu_ln_b, v_sgu_w_s, v_sgu_b_s, v_w_branch, v_w_out, v_ln1_g, v_ln1_b, v_ffn_w_in, v_ffn_w_out, v_ln2_g, v_ln2_b):
    return _step(dict(locals()))
```

```python
import functools
import math

import numpy as np
import jax
import jax.numpy as jnp
from jax import lax
from jax.experimental import pallas as pl
from jax.experimental.pallas import tpu as pltpu

F32 = jnp.float32
BF16 = jnp.bfloat16

N_DEV = 8
D_MODEL = 1024
DEPTH = 4
CHUNK = 64
S5_WIDTH = 512
S5_GROUP = 16
S5_GROUPS = 32
S5_STATE = 64
MLA_HEADS = 8
QK_NOPE = 64
QK_ROPE = 32
V_HEAD = 64
Q_LORA = 384
KV_LORA = 256
ROPE_THETA = 10000.0
SGU_WIDTH = 512
SGU_GROUPS = 4
SGU_CHUNK = 128
FF_HIDDEN = 2816
DEEPNORM_ALPHA = (2 * DEPTH) ** 0.25
LN_EPS = 1e-5
RMS_EPS = 1e-6
NEG_INF = -1e30
ADAM_LR = 0.001
ADAM_B1 = 0.9
ADAM_B2 = 0.999
ADAM_EPS = 1e-08
ADAM_WD = 0.01
ADAM_STEP = 10

IN_WIDTH = 5280
IN_PAD = 5376
_O_S5, _O_CQ, _O_CKV, _O_KPE, _O_USGU, _O_VSGU, _O_GATE = 0, 512, 896, 1152, 1184, 1696, 2208
_IN_SEGMENTS = ((_O_GATE, IN_WIDTH), (_O_S5, _O_CQ), (_O_USGU, _O_VSGU), (_O_VSGU, _O_GATE), (_O_CQ, _O_USGU))
P_GATE, P_S5, P_USGU, P_VSGU, P_MLA = 0, 3072, 3584, 4096, 4608
MLA_BLK = 768

V7X_LANES = 128
V7X_SUBLANES = 8
VMEM_LIMIT = 48 * 1024 * 1024
ATT_BLOCK = 512
SCAN_LANES = 128
S5_CH = S5_GROUPS * S5_STATE
S5_BD = 4

WNAMES = ['w_ada', 'b_ada', 'w_in', 'b_in', 's5_lambda_re', 's5_lambda_im', 's5_log_dt', 's5_b_re', 's5_b_im',
          's5_c_re', 's5_c_im', 's5_d', 's5_w_glu', 's5_b_glu', 'mla_q_norm', 'mla_w_q_up', 'mla_kv_norm',
          'mla_w_kv_up', 'sgu_ln_g', 'sgu_ln_b', 'sgu_w_s', 'sgu_b_s', 'w_branch', 'w_out', 'ln1_g', 'ln1_b',
          'ffn_w_in', 'ffn_w_out', 'ln2_g', 'ln2_b']
SHARDED = (('w_in', 'col'), ('s5_w_glu', 'row'), ('mla_w_q_up', 'col'), ('mla_w_kv_up', 'col'),
           ('w_branch', 'col3'), ('w_out', 'row'), ('ffn_w_in', 'col'), ('ffn_w_out', 'row'))
SMALL = [n for n in WNAMES if n != 'w_ada' and n not in dict(SHARDED)]
SMALL_PAD = N_DEV * V7X_SUBLANES * 1024


def _cparams(*sem):
    return pltpu.CompilerParams(dimension_semantics=sem, vmem_limit_bytes=VMEM_LIMIT)


def _pick(n, target):
    if n <= target:
        return n
    best = None
    for d in range(V7X_LANES, target + 1, V7X_LANES):
        if n % d == 0:
            best = d
    assert best is not None, (n, target)
    return best


def _pick_rows(n, target):
    if n <= target:
        return n
    best = None
    for d in range(V7X_SUBLANES, target + 1, V7X_SUBLANES):
        if n % d == 0:
            best = d
    assert best is not None, (n, target)
    return best


@jax.custom_vjp
def _bdot(a, b):
    return jnp.dot(a.astype(BF16), b.astype(BF16), preferred_element_type=F32)


def _bdot_fwd(a, b):
    return _bdot(a, b), (a, b)


def _bdot_bwd(res, g):
    a, b = res
    gb = g.astype(BF16)
    da = lax.dot_general(gb, b.astype(BF16), (((1,), (1,)), ((), ())), preferred_element_type=F32)
    db = lax.dot_general(a.astype(BF16), gb, (((0,), (0,)), ((), ())), preferred_element_type=F32)
    return da.astype(a.dtype), db.astype(b.dtype)


_bdot.defvjp(_bdot_fwd, _bdot_bwd)


@functools.partial(jax.custom_vjp, nondiff_argnums=(1,))
def _lane_roll(x, shift):
    return pltpu.roll(x, shift % x.shape[1], 1)


def _lane_roll_fwd(x, shift):
    return _lane_roll(x, shift), None


def _lane_roll_bwd(shift, _, g):
    return (_lane_roll(g, -shift),)


_lane_roll.defvjp(_lane_roll_fwd, _lane_roll_bwd)


def _sigmoid(x):
    return 1.0 / (1.0 + jnp.exp(-x))


def _gelu(x):
    return 0.5 * x * (1.0 + jnp.tanh(math.sqrt(2.0 / math.pi) * (x + 0.044715 * (x * x * x))))


def _layer_norm(x, g, b):
    mu = jnp.mean(x, axis=-1, keepdims=True)
    var = jnp.mean(jnp.square(x - mu), axis=-1, keepdims=True)
    return (x - mu) * lax.rsqrt(var + LN_EPS) * g + b


def _rms_norm(x, g):
    return x * lax.rsqrt(jnp.mean(x * x, axis=-1, keepdims=True) + RMS_EPS) * g


def _rope(x, c, s1, s2):
    half = QK_ROPE // 2
    return x * c + _lane_roll(x, -half) * s1 + _lane_roll(x, half) * s2


def _modulate_fn(x, scale_row, shift_row):
    return (x * scale_row + shift_row,)


def _ln_res_fn(x, y, gate_row, g, b):
    return (_layer_norm(DEEPNORM_ALPHA * x + gate_row * y, g, b),)


def _s5_post_fn(ylin, u, d, w_glu, b_glu):
    z = _gelu(ylin + d * u)
    return (z * _sigmoid(_bdot(z, w_glu) + b_glu),)


def _mla_pre_fn(blk, cq_t, sq1, sq2, ck_t, sk1, sk2, q_norm, w_q, kv_norm, w_kv):
    cq, ckv, kpe = blk[:, :Q_LORA], blk[:, Q_LORA:Q_LORA + KV_LORA], blk[:, Q_LORA + KV_LORA:]
    q = _rope(_bdot(_rms_norm(cq, q_norm), w_q), cq_t, sq1, sq2) * Q_SCALE
    kv = _bdot(_rms_norm(ckv, kv_norm), w_kv)
    return q, kv, _rope(kpe, ck_t, sk1, sk2)


def _sgu_fn(u, v, g, b, wm, bias):
    vn = _layer_norm(_gelu(v), g, b)
    w = SGU_CHUNK
    parts = [_bdot(wm[k * w:(k + 1) * w, :], vn[:, k * w:(k + 1) * w]) for k in range(SGU_GROUPS)]
    return (_gelu(u) * (jnp.concatenate(parts, axis=1) + bias),)


def _merge_fn(y0, y1, y2, l0, l1, l2, wb):
    n = S5_WIDTH
    return (_sigmoid(l0) * _bdot(y0, wb[:n]) + _sigmoid(l1) * _bdot(y1, wb[n:2 * n])
            + _sigmoid(l2) * _bdot(y2, wb[2 * n:]),)


def _swiglu_fn(a, b):
    return (a * _sigmoid(a) * b,)


def _rowcall(fn, rows, fulls, out_rows, out_reds=(), *, tm, name):
    rows = [r if isinstance(r, tuple) else (r, r.shape[1], 0) for r in rows]
    t = rows[0][0].shape[0]
    tm = _pick_rows(t, tm)
    n_in, n_or, n_red = len(rows) + len(fulls), len(out_rows), len(out_reds)

    def body(*refs):
        vals = fn(*[r[...] for r in refs[:n_in]])
        assert len(vals) == n_or + n_red, (name, len(vals))
        for ref, v in zip(refs[n_in:n_in + n_or], vals[:n_or]):
            ref[...] = v.astype(ref.dtype)
        if n_red:
            red_refs = refs[n_in + n_or:]

            @pl.when(pl.program_id(0) == 0)
            def _():
                for ref in red_refs:
                    ref[...] = jnp.zeros_like(ref)

            for ref, v in zip(red_refs, vals[n_or:]):
                ref[...] += v.astype(ref.dtype)

    in_specs = [pl.BlockSpec((tm, w), functools.partial(lambda i, blk: (i, blk), blk=blk)) for _, w, blk in rows]
    in_specs += [pl.BlockSpec(f.shape, lambda i: (0, 0)) for f in fulls]
    out_specs = [pl.BlockSpec((tm, c), lambda i: (i, 0)) for c, _ in out_rows]
    out_specs += [pl.BlockSpec(s, lambda i: (0, 0)) for s, _ in out_reds]
    out_shape = [jax.ShapeDtypeStruct((t, c), dt) for c, dt in out_rows]
    out_shape += [jax.ShapeDtypeStruct(s, dt) for s, dt in out_reds]
    return pl.pallas_call(
        body, name=name, grid=(t // tm,), in_specs=in_specs, out_specs=out_specs, out_shape=out_shape,
        compiler_params=_cparams("arbitrary" if n_red else "parallel"),
    )(*[r[0] for r in rows], *fulls)


def _rowcall_vjp(fn, rows, fulls, cots, diff_rows, diff_fulls, *, tm, name, row_dtypes=None):
    rows_n = [r if isinstance(r, tuple) else (r, r.shape[1], 0) for r in rows]
    n_r, n_c = len(rows), len(cots)
    row_dtypes = row_dtypes or [F32] * len(diff_rows)

    def fn2(*vals):
        r = [v.astype(F32) for v in vals[:n_r]]
        ct = vals[n_r:n_r + n_c]
        f = [v.astype(F32) for v in vals[n_r + n_c:]]

        def g(*dargs):
            rr, ff = list(r), list(f)
            for k, idx in enumerate(diff_rows):
                rr[idx] = dargs[k]
            for k, idx in enumerate(diff_fulls):
                ff[idx] = dargs[len(diff_rows) + k]
            return fn(*rr, *ff)

        prim = [r[i] for i in diff_rows] + [f[i] for i in diff_fulls]
        outs, pull = jax.vjp(g, *prim)
        return pull(tuple(c.astype(o.dtype) for c, o in zip(ct, outs)))

    out_rows = [(rows_n[i][1], dt) for i, dt in zip(diff_rows, row_dtypes)]
    out_reds = [(fulls[i].shape, F32) for i in diff_fulls]
    return _rowcall(fn2, list(rows) + list(cots), fulls, out_rows, out_reds, tm=tm, name=name)


def _mm(a, b, *, ta=False, tb=False, bias=None, out_dtype=F32, name, tm=1024, tn=1024, tk=1024):
    (k_a, m) = a.shape if ta else a.shape[::-1]
    (n, k_b) = b.shape if tb else b.shape[::-1]
    assert k_a == k_b, (name, a.shape, b.shape)
    tm, tn, tk = _pick(m, tm) if m % V7X_LANES == 0 else m, _pick(n, tn), _pick(k_a, tk) if k_a % V7X_LANES == 0 else k_a
    nk = k_a // tk
    a_spec = pl.BlockSpec((tk, tm), lambda i, j, k: (k, i)) if ta else pl.BlockSpec((tm, tk), lambda i, j, k: (i, k))
    b_spec = pl.BlockSpec((tn, tk), lambda i, j, k: (j, k)) if tb else pl.BlockSpec((tk, tn), lambda i, j, k: (k, j))
    dims = (((0,) if ta else (1,), (1,) if tb else (0,)), ((), ()))
    has_bias = bias is not None

    def body(*refs):
        a_ref, b_ref = refs[0], refs[1]
        part = lax.dot_general(a_ref[...].astype(BF16), b_ref[...].astype(BF16), dims, preferred_element_type=F32)
        if nk == 1:
            o_ref = refs[-1]
            o_ref[...] = (part + refs[2][...] if has_bias else part).astype(o_ref.dtype)
            return
        o_ref, acc_ref = refs[-2], refs[-1]
        k = pl.program_id(2)

        @pl.when(k == 0)
        def _():
            acc_ref[...] = part

        @pl.when(k > 0)
        def _():
            acc_ref[...] += part

        @pl.when(k == nk - 1)
        def _():
            r = acc_ref[...]
            if has_bias:
                r = r + refs[2][...]
            o_ref[...] = r.astype(o_ref.dtype)

    in_specs = [a_spec, b_spec] + ([pl.BlockSpec((1, tn), lambda i, j, k: (0, j))] if has_bias else [])
    return pl.pallas_call(
        body, name=name, grid=(m // tm, n // tn, nk), in_specs=in_specs,
        out_specs=pl.BlockSpec((tm, tn), lambda i, j, k: (i, j)),
        out_shape=jax.ShapeDtypeStruct((m, n), out_dtype),
        scratch_shapes=[pltpu.VMEM((tm, tn), F32)] if nk > 1 else [],
        compiler_params=_cparams("parallel", "parallel", "arbitrary"),
    )(a, b, *([bias] if has_bias else []))


def _seg_order(a):
    t, c = a.shape
    return a.reshape(V7X_SUBLANES, t // V7X_SUBLANES, c).transpose(1, 0, 2).reshape(t, c)


def _time_order(a):
    t, c = a.shape
    return a.reshape(t // V7X_SUBLANES, V7X_SUBLANES, c).transpose(1, 0, 2).reshape(t, c)


def _scan_in_place(o_ref, tab_ref, reverse, h_ref=None, da_ref=None):
    ntile = o_ref.shape[0] // V7X_SUBLANES
    with_da = h_ref is not None
    x_ref = o_ref
    ln = SCAN_LANES
    rows8 = V7X_SUBLANES

    def run():
        row = lax.broadcasted_iota(jnp.int32, (rows8, ln), 0)
        bcast = lambda k, lo: jnp.broadcast_to(tab_ref[k:k + 1, lo:lo + ln], (rows8, ln))
        ar, ai = bcast(0, 0), bcast(0, ln)
        zero = jnp.zeros((rows8, ln), F32)

        def tile(ref, i):
            r0 = pl.multiple_of(i * rows8, rows8)
            return ref[pl.ds(r0, rows8), :ln], ref[pl.ds(r0, rows8), ln:]

        def put(i, yr, yi):
            r0 = pl.multiple_of(i * rows8, rows8)
            o_ref[pl.ds(r0, rows8), :ln] = yr
            o_ref[pl.ds(r0, rows8), ln:] = yi

        def pass1(n, carry):
            i = (ntile - 1 - n) if reverse else n
            xr, xi = tile(x_ref, i)
            cr, ci = carry
            yr, yi = xr + ar * cr - ai * ci, xi + ar * ci + ai * cr
            put(i, yr, yi)
            return yr, yi

        fr, fi = lax.fori_loop(0, ntile, pass1, (zero, zero), unroll=8)
        for k, step in zip((1, 2, 3), (1, 2, 4)):
            pr, pi = bcast(k, 0), bcast(k, ln)
            shift, keep = (rows8 - step, row < rows8 - step) if reverse else (step, row >= step)
            sr = jnp.where(keep, pltpu.roll(fr, shift, 0), 0.0)
            si = jnp.where(keep, pltpu.roll(fi, shift, 0), 0.0)
            fr, fi = fr + pr * sr - pi * si, fi + pr * si + pi * sr
        shift, keep = (rows8 - 1, row < rows8 - 1) if reverse else (1, row >= 1)
        cr = jnp.where(keep, pltpu.roll(fr, shift, 0), 0.0)
        ci = jnp.where(keep, pltpu.roll(fi, shift, 0), 0.0)
        if with_da:
            lr, li = tile(h_ref, ntile - 1)
            h0r = jnp.where(row >= 1, pltpu.roll(lr, 1, 0), 0.0)
            h0i = jnp.where(row >= 1, pltpu.roll(li, 1, 0), 0.0)

        def pass2(n, carry):
            i = (ntile - 1 - n) if reverse else n
            pr, pi = carry[0], carry[1]
            yr, yi = tile(o_ref, i)
            yr, yi = yr + pr * cr - pi * ci, yi + pr * ci + pi * cr
            put(i, yr, yi)
            new = (pr * ar - pi * ai, pr * ai + pi * ar)
            if not with_da:
                return new
            pr_, pi_ = tile(h_ref, jnp.maximum(i - 1, 0))
            hpr, hpi = jnp.where(i > 0, pr_, h0r), jnp.where(i > 0, pi_, h0i)
            return new + (carry[2] + yr * hpr + yi * hpi, carry[3] + yi * hpr - yr * hpi)

        out = lax.fori_loop(0, ntile, pass2, (ar, ai) + ((zero, zero) if with_da else ()), unroll=8)
        if with_da:
            da_ref[:, :ln] = out[2]
            da_ref[:, ln:] = out[3]

    run()


_S5_SUB = 2 * S5_CH // S5_BD // (2 * SCAN_LANES)


def _s5_specs(t):
    ku, blk = S5_WIDTH // S5_BD, 2 * SCAN_LANES
    return dict(
        u=pl.BlockSpec((t, ku), lambda j: (0, j // _S5_SUB)),
        hs=pl.BlockSpec((t, blk), lambda j: (0, j)),
        wb=pl.BlockSpec((None, ku, blk), lambda j: (j // _S5_SUB, 0, j % _S5_SUB)),
        wc=pl.BlockSpec((None, blk, ku), lambda j: (j // _S5_SUB, j % _S5_SUB, 0)),
        tab=pl.BlockSpec((V7X_SUBLANES, blk), lambda j: (0, j)))


def _s5_fwd(u, wb, wc, tab, *, name):
    t = u.shape[0]
    sp = _s5_specs(t)

    def body(u_ref, wb_ref, wc_ref, tab_ref, hs_ref, y_ref):
        hs_ref[...] = jnp.dot(u_ref[...], wb_ref[...], preferred_element_type=F32)
        _scan_in_place(hs_ref, tab_ref, False)
        y = jnp.dot(hs_ref[...].astype(BF16), wc_ref[...], preferred_element_type=F32)
        first = pl.program_id(0) % _S5_SUB == 0

        @pl.when(first)
        def _():
            y_ref[...] = y

        @pl.when(jnp.logical_not(first))
        def _():
            y_ref[...] += y

    return pl.pallas_call(
        body, name=name, grid=(2 * S5_CH // (2 * SCAN_LANES),),
        in_specs=[sp['u'], sp['wb'], sp['wc'], sp['tab']], out_specs=[sp['hs'], sp['u']],
        out_shape=[jax.ShapeDtypeStruct((t, 2 * S5_CH), F32), jax.ShapeDtypeStruct((t, S5_WIDTH), F32)],
        compiler_params=_cparams("arbitrary"),
    )(u, wb, wc, tab)


def _s5_bwd(dy, u, hs, wb, wc, tab, *, name):
    t = u.shape[0]
    sp = _s5_specs(t)
    nt, tn = (((1,), (1,)), ((), ())), (((0,), (0,)), ((), ()))

    def body(dy_ref, u_ref, hs_ref, wb_ref, wc_ref, tab_ref, du_ref, dwb_ref, dwc_ref, da_ref, g_ref):
        g_ref[...] = lax.dot_general(dy_ref[...], wc_ref[...], nt, preferred_element_type=F32)
        dwc_ref[...] = lax.dot_general(hs_ref[...].astype(BF16), dy_ref[...], tn, preferred_element_type=F32)
        _scan_in_place(g_ref, tab_ref, True, hs_ref, da_ref)
        gb = g_ref[...].astype(BF16)
        dwb_ref[...] = lax.dot_general(u_ref[...], gb, tn, preferred_element_type=F32)
        du = lax.dot_general(gb, wb_ref[...], nt, preferred_element_type=F32)
        first = pl.program_id(0) % _S5_SUB == 0

        @pl.when(first)
        def _():
            du_ref[...] = du

        @pl.when(jnp.logical_not(first))
        def _():
            du_ref[...] += du

    return pl.pallas_call(
        body, name=name, grid=(2 * S5_CH // (2 * SCAN_LANES),),
        in_specs=[sp['u'], sp['u'], sp['hs'], sp['wb'], sp['wc'], sp['tab']],
        out_specs=[sp['u'], sp['wb'], sp['wc'], sp['tab']],
        out_shape=[jax.ShapeDtypeStruct((t, S5_WIDTH), F32), jax.ShapeDtypeStruct(wb.shape, F32),
                   jax.ShapeDtypeStruct(wc.shape, F32), jax.ShapeDtypeStruct((V7X_SUBLANES, 2 * S5_CH), F32)],
        scratch_shapes=[pltpu.VMEM((t, 2 * SCAN_LANES), F32)],
        compiler_params=_cparams("arbitrary"),
    )(dy, u, hs, wb, wc, tab)


ATT_SCALE = (QK_NOPE + QK_ROPE) ** -0.5
Q_SCALE = ATT_SCALE * math.log2(math.e)
LN2 = math.log(2.0)


def _att_mask(qi, kj, tb):
    qc = (qi * tb + lax.broadcasted_iota(jnp.int32, (tb, tb), 0)) // CHUNK
    kc = (kj * tb + lax.broadcasted_iota(jnp.int32, (tb, tb), 1)) // CHUNK
    return kc <= qc


def _with_side(side, n_main_in, n_main_out, refs, step, nsteps, compute):
    if side is None:
        compute(refs)
        return
    n = side.n
    main = refs[:n_main_in] + refs[n_main_in + n:n_main_in + n + n_main_out] + refs[n_main_in + 2 * n + n_main_out + 3:]
    x_refs = refs[n_main_in:n_main_in + n]
    y_refs = refs[n_main_in + n + n_main_out:n_main_in + 2 * n + n_main_out]
    side_refs = (x_refs, y_refs) + tuple(refs[n_main_in + 2 * n + n_main_out:n_main_in + 2 * n + n_main_out + 3])
    pl.when(step == 0)(functools.partial(side.start, *side_refs))
    compute(main)
    pl.when(step == (7 * nsteps) // 8)(functools.partial(side.relay, *side_refs))
    pl.when(step == nsteps - 1)(functools.partial(side.finish, *side_refs))


def _side_call(body, side, *, name, grid, in_specs, out_specs, out_shape, scratch, args, sem):
    n_out = len(out_shape)
    if side is not None:
        in_specs, args = in_specs + side.specs, list(args) + side.xs
        out_specs, out_shape = out_specs + side.specs, out_shape + side.out_shape
        scratch = side.scratch + scratch
        params = pltpu.CompilerParams(dimension_semantics=("arbitrary",) * len(grid), vmem_limit_bytes=VMEM_LIMIT,
                                      has_side_effects=True)
    else:
        params = _cparams(*sem)
    res = pl.pallas_call(body, name=name, grid=grid, in_specs=in_specs, out_specs=out_specs, out_shape=out_shape,
                         scratch_shapes=scratch, compiler_params=params)(*args)
    return res[:n_out], res[n_out:]


def _attn_fwd(q, k, v, *, name, side=None):
    h, t, dq = q.shape
    dv = v.shape[2]
    tb = min(ATT_BLOCK, t)
    nblk = t // tb
    nt = (((1,), (1,)), ((), ()))

    def compute(refs):
        q_ref, k_ref, v_ref, o_ref, lse_ref = refs
        i = pl.program_id(1)
        qb = q_ref[...]

        def kv_step(j, carry, masked):
            m, l, acc = carry
            r0 = pl.multiple_of(j * tb, tb)
            s = lax.dot_general(qb, k_ref[pl.ds(r0, tb), :], nt, preferred_element_type=F32)
            if masked:
                s = jnp.where(_att_mask(i, j, tb), s, NEG_INF)
            m_new = jnp.maximum(m, jnp.max(s, axis=1, keepdims=True))
            alpha = jnp.exp2(m - m_new)
            p = jnp.exp2(s - m_new)
            l = alpha * l + jnp.sum(p, axis=1, keepdims=True)
            acc = alpha * acc + jnp.dot(p.astype(BF16), v_ref[pl.ds(r0, tb), :], preferred_element_type=F32)
            return m_new, l, acc

        init = (jnp.full((tb, 1), NEG_INF, F32), jnp.zeros((tb, 1), F32), jnp.zeros((tb, dv), F32))
        carry = lax.fori_loop(0, i, functools.partial(kv_step, masked=False), init)
        m, l, acc = kv_step(i, carry, True)
        o_ref[...] = acc / l
        lse_ref[...] = m + jnp.log2(l)

    def body(*refs):
        _with_side(side, 3, 2, refs, pl.program_id(0) * nblk + pl.program_id(1), h * nblk, compute)

    return _side_call(
        body, side, name=name, grid=(h, nblk),
        in_specs=[pl.BlockSpec((None, tb, dq), lambda hh, i: (hh, i, 0)),
                  pl.BlockSpec((None, t, dq), lambda hh, i: (hh, 0, 0)),
                  pl.BlockSpec((None, t, dv), lambda hh, i: (hh, 0, 0))],
        out_specs=[pl.BlockSpec((None, tb, dv), lambda hh, i: (hh, i, 0)),
                   pl.BlockSpec((None, tb, 1), lambda hh, i: (hh, i, 0))],
        out_shape=[jax.ShapeDtypeStruct((h, t, dv), F32), jax.ShapeDtypeStruct((h, t, 1), F32)],
        scratch=[], args=[q, k, v], sem=("parallel", "parallel"))


def _attn_bwd(q, k, v, o, lse, do, *, name, side=None):
    h, t, dq_w = q.shape
    dv_w = v.shape[2]
    tb = min(ATT_BLOCK, t)
    nblk = t // tb
    nt = (((1,), (1,)), ((), ()))
    tn = (((0,), (0,)), ((), ()))

    def compute(refs):
        q_ref, k_ref, v_ref, o_ref, lse_ref, do_ref, dq_ref, dk_ref, dv_ref, delta_ref = refs
        j = pl.program_id(1)

        @pl.when(j == 0)
        def _():
            dq_ref[...] = jnp.zeros_like(dq_ref)

            def dstep(i, c):
                r0 = pl.multiple_of(i * tb, tb)
                delta_ref[pl.ds(r0, tb), :] = jnp.sum(do_ref[pl.ds(r0, tb), :].astype(F32) * o_ref[pl.ds(r0, tb), :],
                                                      axis=1, keepdims=True)
                return c

            lax.fori_loop(0, nblk, dstep, 0)

        kb, vb = k_ref[...], v_ref[...]

        def q_step(i, carry, masked):
            dk, dv = carry
            r0 = pl.multiple_of(i * tb, tb)
            qb, dob = q_ref[pl.ds(r0, tb), :], do_ref[pl.ds(r0, tb), :]
            s = lax.dot_general(qb, kb, nt, preferred_element_type=F32)
            if masked:
                s = jnp.where(_att_mask(i, j, tb), s, NEG_INF)
            p = jnp.exp2(s - lse_ref[pl.ds(r0, tb), :])
            dv = dv + lax.dot_general(p.astype(BF16), dob, tn, preferred_element_type=F32)
            dp = lax.dot_general(dob, vb, nt, preferred_element_type=F32)
            ds = (p * (dp - delta_ref[pl.ds(r0, tb), :]) * LN2).astype(BF16)
            dk = dk + lax.dot_general(ds, qb, tn, preferred_element_type=F32)
            dq_ref[pl.ds(r0, tb), :] += jnp.dot(ds, kb, preferred_element_type=F32)
            return dk, dv

        carry = q_step(j, (jnp.zeros((tb, dq_w), F32), jnp.zeros((tb, dv_w), F32)), True)
        dk, dv = lax.fori_loop(j + 1, nblk, functools.partial(q_step, masked=False), carry)
        dk_ref[...] = dk
        dv_ref[...] = dv

    def body(*refs):
        _with_side(side, 6, 3, refs, pl.program_id(0) * nblk + pl.program_id(1), h * nblk, compute)

    whole = lambda w: pl.BlockSpec((None, t, w), lambda hh, j: (hh, 0, 0))
    blockj = lambda w: pl.BlockSpec((None, tb, w), lambda hh, j: (hh, j, 0))
    return _side_call(
        body, side, name=name, grid=(h, nblk),
        in_specs=[whole(dq_w), blockj(dq_w), blockj(dv_w), whole(dv_w), whole(1), whole(dv_w)],
        out_specs=[whole(dq_w), blockj(dq_w), blockj(dv_w)],
        out_shape=[jax.ShapeDtypeStruct((h, t, dq_w), F32), jax.ShapeDtypeStruct((h, t, dq_w), F32),
                   jax.ShapeDtypeStruct((h, t, dv_w), F32)],
        scratch=[pltpu.VMEM((t, 1), F32)], args=[q, k, v, o, lse, do], sem=("parallel", "arbitrary"))


class _Exchange:
    def __init__(self, xs, gather, two_level=False):
        assert gather or not two_level
        self.xs, self.gather, self.n, self.two_level = list(xs), gather, len(xs), two_level
        shapes = [tuple(x.shape) if gather else tuple(x.shape[1:]) for x in xs]
        self.out_shape = [jax.ShapeDtypeStruct((N_DEV,) + shp, x.dtype) for shp, x in zip(shapes, xs)]
        self.specs = [pl.BlockSpec(memory_space=pl.ANY)] * self.n
        self.scratch = [pltpu.SemaphoreType.DMA((self.n, N_DEV - 1)), pltpu.SemaphoreType.DMA((self.n, N_DEV - 1)),
                        pltpu.SemaphoreType.DMA((self.n,))]

    def copies(self, x_refs, y_refs, send_sems, recv_sems, local_sems):
        mx, my, mc = lax.axis_index("x"), lax.axis_index("y"), lax.axis_index("c")
        me = 4 * mx + 2 * my + mc
        out = [pltpu.make_async_copy(x_refs[i] if self.gather else x_refs[i].at[me], y_refs[i].at[me], local_sems.at[i])
               for i in range(self.n)]
        for k in range(1, N_DEV):
            px = 1 - mx if k & 4 else mx
            py = 1 - my if k & 2 else my
            pc = 1 - mc if k & 1 else mc
            for i in range(self.n):
                out.append(pltpu.make_async_remote_copy(
                    src_ref=x_refs[i] if self.gather else x_refs[i].at[4 * px + 2 * py + pc], dst_ref=y_refs[i].at[me],
                    send_sem=send_sems.at[i, k - 1], recv_sem=recv_sems.at[i, k - 1],
                    device_id=(px, py, pc), device_id_type=pl.DeviceIdType.MESH))
        return out


    def _two_level(self, x_refs, y_refs, send_sems, recv_sems, local_sems):
        mx, my, mc = lax.axis_index("x"), lax.axis_index("y"), lax.axis_index("c")
        sib = (mx, my, 1 - mc)
        chips = [(1 - mx, my), (mx, 1 - my), (1 - mx, 1 - my)]
        idx = lambda px, py, pc: 4 * px + 2 * py + pc
        me = idx(mx, my, mc)

        def rc(i, k, src, block, to):
            return pltpu.make_async_remote_copy(
                src_ref=src, dst_ref=y_refs[i].at[block], send_sem=send_sems.at[i, k], recv_sem=recv_sems.at[i, k],
                device_id=to, device_id_type=pl.DeviceIdType.MESH)

        rng = range(self.n)
        over_ici = [(j, chip, i) for j, chip in enumerate(chips) for i in rng]
        return dict(
            local=lambda: [pltpu.make_async_copy(x_refs[i], y_refs[i].at[me], local_sems.at[i]) for i in rng],
            own=lambda: [rc(i, 0, x_refs[i], me, sib) for i in rng]
            + [rc(i, 1 + j, x_refs[i], me, (*chip, mc)) for j, chip, i in over_ici],
            relay=lambda: [rc(i, 4 + j, y_refs[i].at[idx(*chip, mc)], idx(*chip, mc), sib) for j, chip, i in over_ici],
            landed=lambda: [rc(i, 1 + j, x_refs[i], idx(*chip, mc), sib) for j, chip, i in over_ici],
            last=lambda: [rc(i, 0, x_refs[i], idx(*sib), sib) for i in rng]
            + [rc(i, 4 + j, x_refs[i], idx(*chip, 1 - mc), sib) for j, chip, i in over_ici])

    def start(self, *refs):
        if not self.two_level:
            for cp in self.copies(*refs):
                cp.start()
            return
        plan = self._two_level(*refs)
        for cp in plan['local']() + plan['own']():
            cp.start()

    def relay(self, *refs):
        if not self.two_level:
            return
        plan = self._two_level(*refs)
        for arrived, cp in zip(plan['landed'](), plan['relay']()):
            arrived.wait_recv()
            cp.start()

    def finish(self, *refs):
        if not self.two_level:
            for cp in self.copies(*refs):
                cp.wait()
            return
        plan = self._two_level(*refs)
        for cp in plan['last']():
            cp.wait_recv()
        for cp in plan['own']() + plan['relay']():
            cp.wait_send()
        for cp in plan['local']():
            cp.wait()


def _exchange(xs, *, gather, name, two_level=False):
    ex = _Exchange(xs, gather, two_level)
    n = ex.n

    def body(*refs):
        refs = (refs[:n], refs[n:2 * n]) + tuple(refs[2 * n:])
        ex.start(*refs)
        ex.relay(*refs)
        ex.finish(*refs)

    return pl.pallas_call(
        body, name=name, out_shape=ex.out_shape, in_specs=ex.specs, out_specs=ex.specs, scratch_shapes=ex.scratch,
        compiler_params=pltpu.CompilerParams(has_side_effects=True),
    )(*ex.xs)


def _adamw(w, gs, m, v, *, name, tm=256):
    nl = len(gs)
    parts = gs[0].ndim == 3
    c = w.shape[1]
    r = w.shape[0] // nl
    tm = _pick_rows(r, tm)
    nrow = r // tm

    def body(*refs):
        w_ref, g_refs, (m_ref, v_ref, go_ref, d_ref, mo_ref, vo_ref) = refs[0], refs[1:1 + nl], refs[1 + nl:]

        def update(g_ref):
            if parts:
                gv = g_ref[0].astype(F32)
                for k in range(1, N_DEV):
                    gv = gv + g_ref[k].astype(F32)
            else:
                gv = g_ref[...]
            mn = ADAM_B1 * m_ref[...] + (1.0 - ADAM_B1) * gv
            vn = ADAM_B2 * v_ref[...] + (1.0 - ADAM_B2) * jnp.square(gv)
            m_hat = mn / (1.0 - ADAM_B1 ** ADAM_STEP)
            v_hat = vn / (1.0 - ADAM_B2 ** ADAM_STEP)
            go_ref[...] = gv
            d_ref[...] = -ADAM_LR * (m_hat / (jnp.sqrt(v_hat) + ADAM_EPS) + ADAM_WD * w_ref[...])
            mo_ref[...] = mn
            vo_ref[...] = vn

        if nl == 1:
            update(g_refs[0])
        else:
            for layer, g_ref in enumerate(g_refs):
                pl.when(pl.program_id(0) == layer)(functools.partial(update, g_ref))

    spec = pl.BlockSpec((tm, c), lambda l, i: (l * nrow + i, 0))

    def gspec(layer):
        row = lambda l, i: jnp.where(l == layer, i, 0)
        if parts:
            return pl.BlockSpec((N_DEV, tm, c), lambda l, i: (0, row(l, i), 0))
        return pl.BlockSpec((tm, c), lambda l, i: (row(l, i), 0))

    return pl.pallas_call(
        body, name=name, grid=(nl, nrow), in_specs=[spec] + [gspec(k) for k in range(nl)] + [spec, spec],
        out_specs=[spec] * 4, out_shape=[jax.ShapeDtypeStruct(w.shape, F32)] * 4,
        compiler_params=_cparams("arbitrary", "arbitrary"),
    )(w, *gs, m, v)


def _sum_parts(x, *, name):
    def body(x_ref, o_ref):
        acc = x_ref[0]
        for k in range(1, N_DEV):
            acc = acc + x_ref[k]
        o_ref[...] = acc

    return pl.pallas_call(body, name=name, out_shape=jax.ShapeDtypeStruct(x.shape[1:], x.dtype))(x)


def _permute_in(a):
    pad = jnp.zeros(a.shape[:-1] + (IN_PAD - IN_WIDTH,), a.dtype)
    return jnp.concatenate([a[..., lo:hi] for lo, hi in _IN_SEGMENTS] + [pad], axis=-1)


def _unpermute_in(a):
    out, pos = {}, 0
    for lo, hi in _IN_SEGMENTS:
        out[lo] = a[..., pos:pos + hi - lo]
        pos += hi - lo
    return jnp.concatenate([out[lo] for lo in sorted(out)], axis=-1)


def _full_from_gathered(g, kind):
    if kind == 'row':
        return g.reshape((-1,) + g.shape[2:])
    if g.shape[-1] % V7X_LANES == 0:
        return jnp.moveaxis(g, 0, -2).reshape(g.shape[1:-1] + (-1,))
    return jnp.concatenate([g[d] for d in range(N_DEV)], axis=-1)


def _contrib_from_full(g, kind):
    if kind == 'row':
        return g.reshape((N_DEV, -1) + g.shape[1:])
    ns = g.shape[-1] // N_DEV
    if ns % V7X_LANES == 0:
        return jnp.moveaxis(g.reshape(g.shape[:-1] + (N_DEV, ns)), -2, 0)
    return jnp.stack([g[..., d * ns:(d + 1) * ns] for d in range(N_DEV)])


def _in_runs(ns):
    runs, pos = [], 0
    for lo, hi in _IN_SEGMENTS:
        for d in range(lo // ns, (hi - 1) // ns + 1):
            a, b = max(lo, d * ns), min(hi, (d + 1) * ns)
            runs.append((d, a - d * ns, b - d * ns, pos))
            pos += b - a
    return runs


def _w_in_from_gathered(g):
    pieces = [g[d][:, a:b] for d, a, b, _ in _in_runs(g.shape[2])]
    pad = jnp.zeros((g.shape[1], IN_PAD - IN_WIDTH), g.dtype)
    return jnp.concatenate(pieces + [pad], axis=1)


def _w_in_contrib(gp):
    ns = IN_WIDTH // N_DEV
    per_dev = [[] for _ in range(N_DEV)]
    for d, a, b, pos in sorted(_in_runs(ns), key=lambda r: (r[0], r[1])):
        per_dev[d].append(gp[:, pos:pos + b - a])
    return jnp.stack([jnp.concatenate(p, axis=1) for p in per_dev])


def _as2d(a, lead=0):
    return a.reshape(a.shape[:lead] + (-1, a.shape[-1]))


def _chan_cols(re, im):
    lead = re.shape[:-1]
    nb = S5_CH // SCAN_LANES
    return jnp.stack([re.reshape(lead + (nb, SCAN_LANES)), im.reshape(lead + (nb, SCAN_LANES))],
                     axis=-2).reshape(lead + (2 * S5_CH,))


def _s5_tables(lam_re, lam_im, log_dt, b_re, b_im, c_re, c_im):
    dt = jnp.exp(log_dt)[:, None]
    mag = jnp.exp(lam_re * dt)
    a_re = mag * jnp.cos(lam_im * dt)
    a_im = mag * jnp.sin(lam_im * dt)
    den = lam_re * lam_re + lam_im * lam_im
    f_re = ((a_re - 1.0) * lam_re + a_im * lam_im) / den
    f_im = (a_im * lam_re - (a_re - 1.0) * lam_im) / den
    bb_re = f_re[..., None] * b_re - f_im[..., None] * b_im
    bb_im = f_re[..., None] * b_im + f_im[..., None] * b_re
    gb = S5_GROUPS // S5_BD
    eye = jnp.eye(gb, dtype=F32)
    blocks = lambda a: a.reshape((S5_BD, gb) + a.shape[1:])

    def cols(re, im):
        shp = (S5_BD, S5_WIDTH // S5_BD, -1, SCAN_LANES)
        return jnp.stack([re.reshape(shp), im.reshape(shp)], axis=-2).reshape(S5_BD, S5_WIDTH // S5_BD, -1)

    flat = lambda a: a.reshape(S5_BD, S5_WIDTH // S5_BD, -1)
    wb_c = cols(flat(jnp.einsum('kgpc,gh->kgchp', blocks(bb_re), eye)), flat(jnp.einsum('kgpc,gh->kgchp', blocks(bb_im), eye)))
    wc_c = cols(flat(jnp.einsum('kgcp,gh->khcgp', blocks(c_re), eye)), -flat(jnp.einsum('kgcp,gh->khcgp', blocks(c_im), eye)))
    a_row = _chan_cols(a_re.reshape(1, S5_CH), a_im.reshape(1, S5_CH))
    return wb_c, wc_c.transpose(0, 2, 1), a_row


def _scan_tables(a_row, conj, seg_len):
    nb = S5_CH // SCAN_LANES
    a = a_row.reshape(nb, 2, SCAN_LANES)
    base = (a[:, 0], -a[:, 1] if conj else a[:, 1])
    mul = lambda x, y: (x[0] * y[0] - x[1] * y[1], x[0] * y[1] + x[1] * y[0])
    seg, sq, e = None, base, seg_len
    while e:
        if e & 1:
            seg = sq if seg is None else mul(seg, sq)
        sq, e = mul(sq, sq), e >> 1
    seg2 = mul(seg, seg)
    rows = [base, seg, seg2, mul(seg2, seg2)]
    lay = lambda z: jnp.stack([z[0], z[1]], axis=1).reshape(-1)
    return jnp.stack([lay(z) for z in rows] + [jnp.zeros((2 * S5_CH,), F32)] * (V7X_SUBLANES - len(rows)))


def _rope_tables(t):
    half = QK_ROPE // 2
    inv_freq = 1.0 / (ROPE_THETA ** (jnp.arange(0, QK_ROPE, 2, dtype=F32) / QK_ROPE))
    ang = jnp.arange(t, dtype=F32)[:, None] * inv_freq[None, :]
    cos, sin = jnp.cos(ang), jnp.sin(ang)
    zero = jnp.zeros_like(sin)

    def lay(nope, width, first, second):
        head = jnp.concatenate([jnp.full((t, nope), 1.0 if first is cos else 0.0, F32), first, second], axis=1)
        reps = width // head.shape[1]
        out = jnp.tile(head, (1, reps))
        return jnp.pad(out, ((0, 0), (0, width - out.shape[1])))

    hq = MLA_HEADS * (QK_NOPE + QK_ROPE)
    q_tabs = (lay(QK_NOPE, hq, cos, cos), lay(QK_NOPE, hq, -sin, zero), lay(QK_NOPE, hq, zero, sin))
    k_tabs = (lay(0, V7X_LANES, cos, cos)[:, :V7X_LANES] * (jnp.arange(V7X_LANES) < QK_ROPE),
              lay(0, V7X_LANES, -sin, zero) * (jnp.arange(V7X_LANES) < QK_ROPE),
              lay(0, V7X_LANES, zero, sin) * (jnp.arange(V7X_LANES) < QK_ROPE))
    return q_tabs, k_tabs


def _sgu_tables(w_s, b_s):
    pos = jnp.arange(SGU_CHUNK) // CHUNK
    mask = pos[None, :] <= pos[:, None]
    wm = jnp.where(mask[None], w_s, 0.0).reshape(SGU_GROUPS * SGU_CHUNK, SGU_CHUNK)
    bias = jnp.repeat(b_s.T, SGU_WIDTH // SGU_GROUPS, axis=1)
    return wm, bias


def _row(v):
    return v.reshape(1, -1)


_S5_PARAMS = ('s5_lambda_re', 's5_lambda_im', 's5_log_dt', 's5_b_re', 's5_b_im', 's5_c_re', 's5_c_im')


def _derived_tables(p, t):
    (wb, wc, a_row), s5_pull = jax.vjp(jax.vmap(_s5_tables), *[p[n] for n in _S5_PARAMS])
    (wm, bias), sgu_pull = jax.vjp(jax.vmap(_sgu_tables), p['sgu_w_s'], p['sgu_b_s'])
    tab_f = jax.vmap(lambda a: _scan_tables(a, False, t // V7X_SUBLANES))(a_row)
    tab_b = jax.vmap(lambda a: _scan_tables(a, True, t // V7X_SUBLANES))(a_row)
    wb, wc = wb.astype(BF16), wc.astype(BF16)
    per_layer = [dict(s5_wb=wb[l], s5_wc=wc[l], s5_tab_fwd=tab_f[l], s5_tab_bwd=tab_b[l], sgu_wm=wm[l], sgu_bias=bias[l])
                 for l in range(len(wm))]

    def pull(grads):
        stacked = lambda k: jnp.stack([g[k] for g in grads])
        out = dict(zip(_S5_PARAMS, s5_pull((stacked('s5_wb'), stacked('s5_wc'), stacked('s5_a')))))
        out['sgu_w_s'], out['sgu_b_s'] = sgu_pull((stacked('sgu_wm'), stacked('sgu_bias')))
        return out

    return per_layer, pull


def _layer_fwd(x, ada, w, rope_tabs, tag, side=None, after_attn=None):
    s = {'x': x}
    q_tabs, k_tabs = rope_tabs
    sc1, gt1, sc2, gt2 = _row(1.0 + ada[1]), _row(1.0 + ada[2]), _row(1.0 + ada[4]), _row(1.0 + ada[5])
    s.update(sc1=sc1, gt1=gt1, sc2=sc2, gt2=gt2)
    (h,) = _rowcall(_modulate_fn, [x], [sc1, _row(ada[0])], [(D_MODEL, BF16)], tm=512, name=f"mod1_{tag}")
    proj = _mm(h, w['w_in_p'], bias=w['b_in_p'], name=f"proj_{tag}", tn=1792)
    s.update(h=h, proj=proj)

    u_view = (proj, S5_WIDTH, P_S5 // S5_WIDTH)
    u_seg = _seg_order(proj[:, P_S5:P_S5 + S5_WIDTH]).astype(BF16)
    hs, ylin = _s5_fwd(u_seg, w['s5_wb'], w['s5_wc'], w['s5_tab_fwd'], name=f"s5_fwd_{tag}")
    ylin = _time_order(ylin)
    s5_full = [_row(w['s5_d']), w['s5_w_glu'], _row(w['s5_b_glu'])]
    (y_s5,) = _rowcall(_s5_post_fn, [ylin, u_view], s5_full, [(S5_WIDTH, BF16)], tm=256, name=f"s5_post_{tag}")
    s.update(u_seg=u_seg, hs=hs, ylin=ylin, y_s5=y_s5)

    mla_rows = [(proj, MLA_BLK, P_MLA // MLA_BLK), *q_tabs, *k_tabs]
    mla_full = [_row(w['mla_q_norm']), w['mla_w_q_up'], _row(w['mla_kv_norm']), w['mla_w_kv_up']]
    hq, hkv = MLA_HEADS * (QK_NOPE + QK_ROPE), MLA_HEADS * (QK_NOPE + V_HEAD)
    q_r, kv, kpe_r = _rowcall(_mla_pre_fn, mla_rows, mla_full, [(hq, BF16), (hkv, BF16), (V7X_LANES, BF16)],
                              tm=256, name=f"mla_pre_{tag}")
    t = x.shape[0]
    qh = q_r.reshape(t, MLA_HEADS, -1).transpose(1, 0, 2)
    kv3 = kv.reshape(t, MLA_HEADS, -1).transpose(1, 0, 2)
    kh = jnp.concatenate([kv3[:, :, :QK_NOPE], jnp.broadcast_to(kpe_r[None, :, :QK_ROPE], (MLA_HEADS, t, QK_ROPE))], axis=2)
    vh = kv3[:, :, QK_NOPE:]
    (o, lse), side_out = _attn_fwd(qh, kh, vh, name=f"attn_fwd_{tag}", side=side)
    if after_attn is not None:
        after_attn(side_out)
    y_mla = o.transpose(1, 0, 2).reshape(t, -1).astype(BF16)
    s.update(qh=qh, kh=kh, vh=vh, o=o, lse=lse, y_mla=y_mla)

    sgu_rows = [(proj, SGU_WIDTH, P_USGU // SGU_WIDTH), (proj, SGU_WIDTH, P_VSGU // SGU_WIDTH)]
    sgu_full = [_row(w['sgu_ln_g']), _row(w['sgu_ln_b']), w['sgu_wm'], w['sgu_bias']]
    (y_sgu,) = _rowcall(_sgu_fn, sgu_rows, sgu_full, [(SGU_WIDTH, BF16)], tm=SGU_CHUNK, name=f"sgu_{tag}")
    s.update(sgu_full=sgu_full, y_sgu=y_sgu)

    wbr = w['w_branch'].reshape(-1, D_MODEL)
    gate_rows = [(proj, D_MODEL, b) for b in range(3)]
    (merged,) = _rowcall(_merge_fn, [y_s5, y_mla, y_sgu] + gate_rows, [wbr], [(D_MODEL, BF16)], tm=256, name=f"merge_{tag}")
    ymix = _mm(merged, w['w_out'], name=f"wout_{tag}")
    (x1,) = _rowcall(_ln_res_fn, [x, ymix], [gt1, _row(w['ln1_g']), _row(w['ln1_b'])], [(D_MODEL, F32)], tm=256,
                     name=f"ln1_{tag}")
    s.update(merged=merged, ymix=ymix, x1=x1)

    (h2,) = _rowcall(_modulate_fn, [x1], [sc2, _row(ada[3])], [(D_MODEL, BF16)], tm=512, name=f"mod2_{tag}")
    ab = _mm(h2, w['ffn_w_in'], name=f"ffn_in_{tag}", tn=1408)
    (act,) = _rowcall(_swiglu_fn, [(ab, FF_HIDDEN, 0), (ab, FF_HIDDEN, 1)], [], [(FF_HIDDEN, BF16)], tm=256,
                      name=f"swiglu_{tag}")
    f = _mm(act, w['ffn_w_out'], name=f"ffn_out_{tag}", tk=2816)
    (x2,) = _rowcall(_ln_res_fn, [x1, f], [gt2, _row(w['ln2_g']), _row(w['ln2_b'])], [(D_MODEL, F32)], tm=256,
                     name=f"ln2_{tag}")
    s.update(h2=h2, ab=ab, act=act, f=f)
    return x2, s, side_out


def _mod_bwd_fn(x, dh, dxa, scale_row):
    return (dxa + dh * scale_row, jnp.sum(dh * x, axis=0, keepdims=True), jnp.sum(dh, axis=0, keepdims=True))


def _layer_bwd(dx2, s, w, rope_tabs, tag, make_side=None):
    g = {}
    q_tabs, k_tabs = rope_tabs
    t = dx2.shape[0]
    ln_full = lambda gt, a, b: [gt, _row(w[a]), _row(w[b])]

    dx1_a, df, dgt2, g['ln2_g'], g['ln2_b'] = _rowcall_vjp(
        _ln_res_fn, [s['x1'], s['f']], ln_full(s['gt2'], 'ln2_g', 'ln2_b'), [dx2], [0, 1], [0, 1, 2],
        tm=256, name=f"ln2_bwd_{tag}", row_dtypes=[F32, BF16])
    dact = _mm(df, w['ffn_w_out'], tb=True, name=f"ffn_out_dx_{tag}", tn=1408)
    g['ffn_w_out'] = _mm(s['act'], df, ta=True, name=f"ffn_out_dw_{tag}", tm=1408)
    def swiglu_bwd_fn(a, b, d):
        _, pull = jax.vjp(_swiglu_fn, a, b)
        return (jnp.concatenate(pull((d,)), axis=1),)

    (dab,) = _rowcall(swiglu_bwd_fn, [(s['ab'], FF_HIDDEN, 0), (s['ab'], FF_HIDDEN, 1), dact], [], [(2 * FF_HIDDEN, BF16)],
                      tm=256, name=f"swiglu_bwd_{tag}")
    dh2 = _mm(dab, w['ffn_w_in'], tb=True, name=f"ffn_in_dx_{tag}", tk=1408)
    g['ffn_w_in'] = _mm(s['h2'], dab, ta=True, name=f"ffn_in_dw_{tag}", tn=1408)
    dx1, dsc2, dsh2 = _rowcall(_mod_bwd_fn, [s['x1'], dh2, dx1_a], [s['sc2']], [(D_MODEL, F32)],
                               [((1, D_MODEL), F32)] * 2, tm=256, name=f"mod2_bwd_{tag}")

    dx_a, dymix, dgt1, g['ln1_g'], g['ln1_b'] = _rowcall_vjp(
        _ln_res_fn, [s['x'], s['ymix']], ln_full(s['gt1'], 'ln1_g', 'ln1_b'), [dx1], [0, 1], [0, 1, 2],
        tm=256, name=f"ln1_bwd_{tag}", row_dtypes=[F32, BF16])
    dmerged = _mm(dymix, w['w_out'], tb=True, name=f"wout_dx_{tag}")
    g['w_out'] = _mm(s['merged'], dymix, ta=True, name=f"wout_dw_{tag}")

    proj = s['proj']
    wbr = w['w_branch'].reshape(-1, D_MODEL)
    gate_rows = [(proj, D_MODEL, b) for b in range(3)]
    dy_s5, dy_mla, dy_sgu, dl0, dl1, dl2, dwbr = _rowcall_vjp(
        _merge_fn, [s['y_s5'], s['y_mla'], s['y_sgu']] + gate_rows, [wbr], [dmerged], [0, 1, 2, 3, 4, 5], [0],
        tm=256, name=f"merge_bwd_{tag}")
    g['w_branch'] = dwbr.reshape(w['w_branch'].shape)

    sgu_rows = [(proj, SGU_WIDTH, P_USGU // SGU_WIDTH), (proj, SGU_WIDTH, P_VSGU // SGU_WIDTH)]
    du_sgu, dv_sgu, dlg, dlb, dwm, dbias = _rowcall_vjp(
        _sgu_fn, sgu_rows, s['sgu_full'], [dy_sgu], [0, 1], [0, 1, 2, 3], tm=SGU_CHUNK, name=f"sgu_bwd_{tag}")
    g['sgu_ln_g'], g['sgu_ln_b'] = dlg.reshape(-1), dlb.reshape(-1)
    g['sgu_wm'], g['sgu_bias'] = dwm, dbias

    do = dy_mla.reshape(t, MLA_HEADS, V_HEAD).transpose(1, 0, 2).astype(BF16)
    (dqh, dkh, dvh), side_out = _attn_bwd(s['qh'], s['kh'], s['vh'], s['o'], s['lse'], do, name=f"attn_bwd_{tag}",
                                          side=make_side(g) if make_side is not None else None)
    dq_r = dqh.transpose(1, 0, 2).reshape(t, -1)
    dkv = jnp.concatenate([dkh[:, :, :QK_NOPE], dvh], axis=2).transpose(1, 0, 2).reshape(t, -1)
    dkpe = jnp.pad(jnp.sum(dkh[:, :, QK_NOPE:], axis=0), ((0, 0), (0, V7X_LANES - QK_ROPE)))
    mla_rows = [(proj, MLA_BLK, P_MLA // MLA_BLK), *q_tabs, *k_tabs]
    mla_full = [_row(w['mla_q_norm']), w['mla_w_q_up'], _row(w['mla_kv_norm']), w['mla_w_kv_up']]
    dmla, dqn, g['mla_w_q_up'], dkvn, g['mla_w_kv_up'] = _rowcall_vjp(
        _mla_pre_fn, mla_rows, mla_full, [dq_r, dkv, dkpe], [0], [0, 1, 2, 3], tm=256, name=f"mla_pre_bwd_{tag}")
    g['mla_q_norm'], g['mla_kv_norm'] = dqn.reshape(-1), dkvn.reshape(-1)

    s5_full = [_row(w['s5_d']), w['s5_w_glu'], _row(w['s5_b_glu'])]
    dylin, du_a, dd, g['s5_w_glu'], dbg = _rowcall_vjp(
        _s5_post_fn, [s['ylin'], (proj, S5_WIDTH, P_S5 // S5_WIDTH)], s5_full, [dy_s5], [0, 1], [0, 1, 2], tm=256,
        name=f"s5_post_bwd_{tag}", row_dtypes=[BF16, F32])
    g['s5_d'], g['s5_b_glu'] = dd.reshape(-1), dbg.reshape(-1)
    du_b, g['s5_wb'], g['s5_wc'], da_part = _s5_bwd(_seg_order(dylin), s['u_seg'], s['hs'], w['s5_wb'], w['s5_wc'],
                                                     w['s5_tab_bwd'], name=f"s5_bwd_{tag}")
    du_b = _time_order(du_b)
    g['s5_a'] = jnp.sum(da_part, axis=0, keepdims=True)

    def dproj_fn(g0, g1, g2, ua, ub, us, vs, ml):
        d = jnp.concatenate([g0, g1, g2, ua + ub, us, vs, ml], axis=1)
        return d, jnp.sum(d, axis=0, keepdims=True)

    dproj, db_in = _rowcall(dproj_fn, [dl0, dl1, dl2, du_a, du_b, du_sgu, dv_sgu, dmla], [], [(IN_PAD, BF16)],
                            [((1, IN_PAD), F32)], tm=256, name=f"dproj_{tag}")
    dh = _mm(dproj, w['w_in_p'], tb=True, name=f"proj_dx_{tag}", tk=1792)
    g['w_in_p'] = _mm(s['h'], dproj, ta=True, name=f"proj_dw_{tag}", tn=896)
    g['b_in'] = _unpermute_in(db_in).reshape(-1)
    dx, dsc1, dsh1 = _rowcall(_mod_bwd_fn, [s['x'], dh, dx_a], [s['sc1']], [(D_MODEL, F32)], [((1, D_MODEL), F32)] * 2,
                              tm=256, name=f"mod1_bwd_{tag}")
    d_ada = jnp.concatenate([dsh1, dsc1, dgt1, dsh2, dsc2, dgt2], axis=0)
    return dx, d_ada, g, side_out


def _loss_fn(y, target):
    err = y - target
    return (err / D_MODEL, 0.5 * jnp.sum(jnp.sum(err * err, axis=1, keepdims=True), axis=0, keepdims=True) / D_MODEL)


def _step(p):
    me = 4 * lax.axis_index("x") + 2 * lax.axis_index("y") + lax.axis_index("c")
    x = p['x'][0]
    t = x.shape[0]
    rope_tabs = _rope_tables(t)

    (c_all,) = _exchange([jnp.broadcast_to(p['c'], (V7X_SUBLANES, D_MODEL))], gather=True, name="gather_c")
    c_all = c_all[:, 0, :]
    (c_act,) = _rowcall(lambda cc: (cc * _sigmoid(cc),), [c_all], [], [(D_MODEL, F32)], tm=N_DEV, name="c_silu")
    ncol = p['w_ada'].shape[2]
    b_ada_loc = lax.dynamic_slice_in_dim(p['b_ada'], me * ncol, ncol, axis=1)
    ada_cols = jnp.concatenate([_mm(c_act, p['w_ada'][l], bias=b_ada_loc[l:l + 1], name=f"ada_{l}") for l in range(DEPTH)])
    (ada_all,) = _exchange([ada_cols], gather=True, name="gather_ada")
    ada_all = ada_all.reshape(N_DEV, DEPTH, N_DEV, ncol)
    ada = lax.dynamic_index_in_dim(ada_all, me, axis=2, keepdims=False)
    ada = ada.transpose(1, 0, 2).reshape(DEPTH, 6, D_MODEL)

    mixer_w, ffn_w = SHARDED[:-2], SHARDED[-2:]

    def shards(l, group):
        return [p[n][l].astype(BF16) for n, _ in group]

    def contribs(g, group):
        return [(_w_in_contrib(g['w_in_p']) if n == 'w_in' else _contrib_from_full(g[n], kind)).astype(BF16)
                for n, kind in group]

    tables, pull_tables = _derived_tables(p, t)

    def mixer_weights(l, gathered):
        w = {n: _full_from_gathered(g, kind) for (n, kind), g in zip(mixer_w[1:], gathered[1:])}
        w['w_in_p'] = _w_in_from_gathered(gathered[0])
        w['b_in_p'] = _row(_permute_in(p['b_in'][l]))
        w.update(tables[l])
        for n in SMALL:
            if n != 'b_ada' and n != 'b_in':
                w[n] = p[n][l]
        return w

    saved, layers = [], []
    gathered = _exchange(shards(0, mixer_w), gather=True, name="gather_w_0", two_level=True)
    for l in range(DEPTH):
        w = mixer_weights(l, gathered)
        side = _Exchange(shards(l, ffn_w) + (shards(l + 1, mixer_w) if l + 1 < DEPTH else []), True, two_level=True)

        def add_ffn(res, w=w):
            for (n, kind), g in zip(ffn_w, res):
                w[n] = _full_from_gathered(g, kind)

        x, s, res = _layer_fwd(x, ada[l], w, rope_tabs, f"l{l}", side, add_ffn)
        gathered = res[len(ffn_w):]
        layers.append(w)
        saved.append(s)
    dy, loss_loc = _rowcall(_loss_fn, [x, p['loss_target'][0]], [], [(D_MODEL, F32)], [((1, 1), F32)], tm=256, name="loss")
    loss = lax.psum(loss_loc[0, 0], ("x", "y", "c"))

    d_ada, grads, landed = [None] * DEPTH, [None] * DEPTH, [[None, None] for _ in range(DEPTH)]
    dx, pending = dy, []
    for l in reversed(range(DEPTH)):
        make_side = lambda g, pending=pending: _Exchange(contribs(g, ffn_w) + pending, False)
        dx, d_ada[l], grads[l], res = _layer_bwd(dx, saved[l], layers[l], rope_tabs, f"l{l}", make_side)
        landed[l][1] = res[:len(ffn_w)]
        if l + 1 < DEPTH:
            landed[l + 1][0] = res[len(ffn_w):]
        pending = contribs(grads[l], mixer_w)
    landed[0][0] = _exchange(pending, gather=False, name="scatter_g_0")
    landed = [list(a) + list(b) for a, b in landed]
    d_ada = jnp.stack(d_ada).reshape(DEPTH, 6 * D_MODEL)

    (d_ada_all,) = _exchange([d_ada], gather=True, name="gather_dada")
    d_ada_cols = lax.dynamic_slice_in_dim(d_ada_all, me * ncol, ncol, axis=2)
    pad_b = ((0, V7X_LANES - N_DEV), (0, 0))
    c_act_p = jnp.pad(c_act, pad_b)
    g_w_ada = jnp.stack([_mm(c_act_p, jnp.pad(d_ada_cols[:, l], pad_b), ta=True, name=f"ada_dw_{l}") for l in range(DEPTH)])

    out = {}
    kinds = ('grad_', 'delta_', 'new_m_', 'new_v_')
    for i, (n, _) in enumerate(SHARDED):
        res = _adamw(_as2d(p[n]), [_as2d(landed[l][i], lead=1) for l in range(DEPTH)], _as2d(p['m_' + n]),
                     _as2d(p['v_' + n]), name=f"adamw_{n}")
        for kind, r in zip(kinds, res):
            out[kind + n] = r.reshape(p[n].shape)

    res = _adamw(_as2d(p['w_ada']), [_as2d(g_w_ada)], _as2d(p['m_w_ada']), _as2d(p['v_w_ada']), name="adamw_ada")
    for kind, r in zip(kinds, res):
        out[kind + 'w_ada'] = r.reshape(p['w_ada'].shape)

    small_g = pull_tables(grads)
    small_g.update({n: jnp.stack([grads[l][n] for l in range(DEPTH)]) for n in SMALL if n not in small_g and n != 'b_ada'})
    small_g['b_ada'] = d_ada
    n_small = sum(int(np.prod(p[n].shape)) for n in SMALL)
    n_pad = -(-n_small // SMALL_PAD) * SMALL_PAD
    flat = jnp.concatenate([small_g[n].reshape(-1) for n in SMALL])
    flat = jnp.pad(flat, (0, n_pad - n_small)).reshape(N_DEV, n_pad // N_DEV // 1024, 1024)
    (landed_small,) = _exchange([flat], gather=False, name="scatter_small")
    (g_all,) = _exchange([_sum_parts(landed_small, name="sum_small")], gather=True, name="gather_small")
    g_all = g_all.reshape(-1)
    off = 0
    for n in SMALL:
        size = int(np.prod(p[n].shape))
        g_n = g_all[off:off + size].reshape(p[n].shape)
        off += size
        res = _adamw(_as2d(p[n]), [_as2d(g_n)], _as2d(p['m_' + n]), _as2d(p['v_' + n]), name=f"adamw_{n}")
        for kind, r in zip(kinds, res):
            out[kind + n] = r.reshape(p[n].shape)

    outs = [loss, dx[None]]
    for kind in ('grad_', 'delta_', 'new_m_', 'new_v_'):
        outs += [out[kind + n] for n in WNAMES]
    return tuple(outs)


def kernel(x, c, w_ada, b_ada, w_in, b_in, s5_lambda_re, s5_lambda_im, s5_log_dt, s5_b_re, s5_b_im, s5_c_re, s5_c_im, s5_d, s5_w_glu, s5_b_glu, mla_q_norm, mla_w_q_up, mla_kv_norm, mla_w_kv_up, sgu_ln_g, sgu_ln_b, sgu_w_s, sgu_b_s, w_branch, w_out, ln1_g, ln1_b, ffn_w_in, ffn_w_out, ln2_g, ln2_b, loss_target, m_w_ada, m_b_ada, m_w_in, m_b_in, m_s5_lambda_re, m_s5_lambda_im, m_s5_log_dt, m_s5_b_re, m_s5_b_im, m_s5_c_re, m_s5_c_im, m_s5_d, m_s5_w_glu, m_s5_b_glu, m_mla_q_norm, m_mla_w_q_up, m_mla_kv_norm, m_mla_w_kv_up, m_sgu_ln_g, m_sgu_ln_b, m_sgu_w_s, m_sgu_b_s, m_w_branch, m_w_out, m_ln1_g, m_ln1_b, m_ffn_w_in, m_ffn_w_out, m_ln2_g, m_ln2_b, v_w_ada, v_b_ada, v_w_in, v_b_in, v_s5_lambda_re, v_s5_lambda_im, v_s5_log_dt, v_s5_b_re, v_s5_b_im, v_s5_c_re, v_s5_c_im, v_s5_d, v_s5_w_glu, v_s5_b_glu, v_mla_q_norm, v_mla_w_q_up, v_mla_kv_norm, v_mla_w_kv_up, v_sgu_ln_g, v_sgu_ln_b, v_sgu_w_s, v_sgu_b_s, v_w_branch, v_w_out, v_ln1_g, v_ln1_b, v_ffn_w_in, v_ffn_w_out, v_ln2_g, v_ln2_b):
    return _step(dict(locals()))
```

```python
import functools
import math

import numpy as np
import jax
import jax.numpy as jnp
from jax import lax
from jax.experimental import pallas as pl
from jax.experimental.pallas import tpu as pltpu

F32 = jnp.float32
BF16 = jnp.bfloat16

N_DEV = 8
D_MODEL = 1024
DEPTH = 4
CHUNK = 64
S5_WIDTH = 512
S5_GROUP = 16
S5_GROUPS = 32
S5_STATE = 64
MLA_HEADS = 8
QK_NOPE = 64
QK_ROPE = 32
V_HEAD = 64
Q_LORA = 384
KV_LORA = 256
ROPE_THETA = 10000.0
SGU_WIDTH = 512
SGU_GROUPS = 4
SGU_CHUNK = 128
FF_HIDDEN = 2816
DEEPNORM_ALPHA = (2 * DEPTH) ** 0.25
LN_EPS = 1e-5
RMS_EPS = 1e-6
NEG_INF = -1e30
ADAM_LR = 0.001
ADAM_B1 = 0.9
ADAM_B2 = 0.999
ADAM_EPS = 1e-08
ADAM_WD = 0.01
ADAM_STEP = 10

IN_WIDTH = 5280
IN_PAD = 5376
_O_S5, _O_CQ, _O_CKV, _O_KPE, _O_USGU, _O_VSGU, _O_GATE = 0, 512, 896, 1152, 1184, 1696, 2208
_IN_SEGMENTS = ((_O_GATE, IN_WIDTH), (_O_S5, _O_CQ), (_O_USGU, _O_VSGU), (_O_VSGU, _O_GATE), (_O_CQ, _O_USGU))
P_GATE, P_S5, P_USGU, P_VSGU, P_MLA = 0, 3072, 3584, 4096, 4608
MLA_BLK = 768

V7X_LANES = 128
V7X_SUBLANES = 8
VMEM_LIMIT = 56 * 1024 * 1024
ATT_BLOCK = 512
SCAN_LANES = 128
S5_CH = S5_GROUPS * S5_STATE
S5_BD = 4

WNAMES = ['w_ada', 'b_ada', 'w_in', 'b_in', 's5_lambda_re', 's5_lambda_im', 's5_log_dt', 's5_b_re', 's5_b_im',
          's5_c_re', 's5_c_im', 's5_d', 's5_w_glu', 's5_b_glu', 'mla_q_norm', 'mla_w_q_up', 'mla_kv_norm',
          'mla_w_kv_up', 'sgu_ln_g', 'sgu_ln_b', 'sgu_w_s', 'sgu_b_s', 'w_branch', 'w_out', 'ln1_g', 'ln1_b',
          'ffn_w_in', 'ffn_w_out', 'ln2_g', 'ln2_b']
SHARDED = (('w_in', 'col'), ('s5_w_glu', 'row'), ('mla_w_q_up', 'col'), ('mla_w_kv_up', 'col'),
           ('w_branch', 'col3'), ('w_out', 'row'), ('ffn_w_in', 'col'), ('ffn_w_out', 'row'))
SMALL = [n for n in WNAMES if n != 'w_ada' and n not in dict(SHARDED)]
SMALL_PAD = N_DEV * V7X_SUBLANES * 1024


def _cparams(*sem):
    return pltpu.CompilerParams(dimension_semantics=sem, vmem_limit_bytes=VMEM_LIMIT)


def _pick(n, target):
    if n <= target:
        return n
    best = None
    for d in range(V7X_LANES, target + 1, V7X_LANES):
        if n % d == 0:
            best = d
    assert best is not None, (n, target)
    return best


def _pick_rows(n, target):
    if n <= target:
        return n
    best = None
    for d in range(V7X_SUBLANES, target + 1, V7X_SUBLANES):
        if n % d == 0:
            best = d
    assert best is not None, (n, target)
    return best


@jax.custom_vjp
def _bdot(a, b):
    return jnp.dot(a.astype(BF16), b.astype(BF16), preferred_element_type=F32)


def _bdot_fwd(a, b):
    return _bdot(a, b), (a, b)


def _bdot_bwd(res, g):
    a, b = res
    gb = g.astype(BF16)
    da = lax.dot_general(gb, b.astype(BF16), (((1,), (1,)), ((), ())), preferred_element_type=F32)
    db = lax.dot_general(a.astype(BF16), gb, (((0,), (0,)), ((), ())), preferred_element_type=F32)
    return da.astype(a.dtype), db.astype(b.dtype)


_bdot.defvjp(_bdot_fwd, _bdot_bwd)


@functools.partial(jax.custom_vjp, nondiff_argnums=(1,))
def _lane_roll(x, shift):
    return pltpu.roll(x, shift % x.shape[1], 1)


def _lane_roll_fwd(x, shift):
    return _lane_roll(x, shift), None


def _lane_roll_bwd(shift, _, g):
    return (_lane_roll(g, -shift),)


_lane_roll.defvjp(_lane_roll_fwd, _lane_roll_bwd)


def _sigmoid(x):
    return 1.0 / (1.0 + jnp.exp(-x))


def _gelu(x):
    return 0.5 * x * (1.0 + jnp.tanh(math.sqrt(2.0 / math.pi) * (x + 0.044715 * (x * x * x))))


def _layer_norm(x, g, b):
    mu = jnp.mean(x, axis=-1, keepdims=True)
    var = jnp.mean(jnp.square(x - mu), axis=-1, keepdims=True)
    return (x - mu) * lax.rsqrt(var + LN_EPS) * g + b


def _rms_norm(x, g):
    return x * lax.rsqrt(jnp.mean(x * x, axis=-1, keepdims=True) + RMS_EPS) * g


def _rope(x, c, s1, s2):
    half = QK_ROPE // 2
    return x * c + _lane_roll(x, -half) * s1 + _lane_roll(x, half) * s2


def _modulate_fn(x, scale_row, shift_row):
    return (x * scale_row + shift_row,)


def _ln_res_fn(x, y, gate_row, g, b):
    return (_layer_norm(DEEPNORM_ALPHA * x + gate_row * y, g, b),)


def _s5_post_fn(ylin, u, d, w_glu, b_glu):
    z = _gelu(ylin + d * u)
    return (z * _sigmoid(_bdot(z, w_glu) + b_glu),)


def _mla_pre_fn(blk, cq_t, sq1, sq2, ck_t, sk1, sk2, q_norm, w_q, kv_norm, w_kv):
    cq, ckv, kpe = blk[:, :Q_LORA], blk[:, Q_LORA:Q_LORA + KV_LORA], blk[:, Q_LORA + KV_LORA:]
    q = _rope(_bdot(_rms_norm(cq, q_norm), w_q), cq_t, sq1, sq2) * Q_SCALE
    kv = _bdot(_rms_norm(ckv, kv_norm), w_kv)
    return q, kv, _rope(kpe, ck_t, sk1, sk2)


def _sgu_fn(u, v, g, b, wm, bias):
    vn = _layer_norm(_gelu(v), g, b)
    w = SGU_CHUNK
    parts = [_bdot(wm[k * w:(k + 1) * w, :], vn[:, k * w:(k + 1) * w]) for k in range(SGU_GROUPS)]
    return (_gelu(u) * (jnp.concatenate(parts, axis=1) + bias),)


def _merge_fn(y0, y1, y2, l0, l1, l2, wb):
    n = S5_WIDTH
    return (_sigmoid(l0) * _bdot(y0, wb[:n]) + _sigmoid(l1) * _bdot(y1, wb[n:2 * n])
            + _sigmoid(l2) * _bdot(y2, wb[2 * n:]),)


def _swiglu_fn(a, b):
    return (a * _sigmoid(a) * b,)


def _rowcall(fn, rows, fulls, out_rows, out_reds=(), *, tm, name):
    rows = [r if isinstance(r, tuple) else (r, r.shape[1], 0) for r in rows]
    t = rows[0][0].shape[0]
    tm = _pick_rows(t, tm)
    n_in, n_or, n_red = len(rows) + len(fulls), len(out_rows), len(out_reds)

    def body(*refs):
        vals = fn(*[r[...] for r in refs[:n_in]])
        assert len(vals) == n_or + n_red, (name, len(vals))
        for ref, v in zip(refs[n_in:n_in + n_or], vals[:n_or]):
            ref[...] = v.astype(ref.dtype)
        if n_red:
            red_refs = refs[n_in + n_or:]

            @pl.when(pl.program_id(0) == 0)
            def _():
                for ref in red_refs:
                    ref[...] = jnp.zeros_like(ref)

            for ref, v in zip(red_refs, vals[n_or:]):
                ref[...] += v.astype(ref.dtype)

    in_specs = [pl.BlockSpec((tm, w), functools.partial(lambda i, blk: (i, blk), blk=blk)) for _, w, blk in rows]
    in_specs += [pl.BlockSpec(f.shape, lambda i: (0, 0)) for f in fulls]
    out_specs = [pl.BlockSpec((tm, c), lambda i: (i, 0)) for c, _ in out_rows]
    out_specs += [pl.BlockSpec(s, lambda i: (0, 0)) for s, _ in out_reds]
    out_shape = [jax.ShapeDtypeStruct((t, c), dt) for c, dt in out_rows]
    out_shape += [jax.ShapeDtypeStruct(s, dt) for s, dt in out_reds]
    return pl.pallas_call(
        body, name=name, grid=(t // tm,), in_specs=in_specs, out_specs=out_specs, out_shape=out_shape,
        compiler_params=_cparams("arbitrary" if n_red else "parallel"),
    )(*[r[0] for r in rows], *fulls)


def _rowcall_vjp(fn, rows, fulls, cots, diff_rows, diff_fulls, *, tm, name, row_dtypes=None):
    rows_n = [r if isinstance(r, tuple) else (r, r.shape[1], 0) for r in rows]
    n_r, n_c = len(rows), len(cots)
    row_dtypes = row_dtypes or [F32] * len(diff_rows)

    def fn2(*vals):
        r = [v.astype(F32) for v in vals[:n_r]]
        ct = vals[n_r:n_r + n_c]
        f = [v.astype(F32) for v in vals[n_r + n_c:]]

        def g(*dargs):
            rr, ff = list(r), list(f)
            for k, idx in enumerate(diff_rows):
                rr[idx] = dargs[k]
            for k, idx in enumerate(diff_fulls):
                ff[idx] = dargs[len(diff_rows) + k]
            return fn(*rr, *ff)

        prim = [r[i] for i in diff_rows] + [f[i] for i in diff_fulls]
        outs, pull = jax.vjp(g, *prim)
        return pull(tuple(c.astype(o.dtype) for c, o in zip(ct, outs)))

    out_rows = [(rows_n[i][1], dt) for i, dt in zip(diff_rows, row_dtypes)]
    out_reds = [(fulls[i].shape, F32) for i in diff_fulls]
    return _rowcall(fn2, list(rows) + list(cots), fulls, out_rows, out_reds, tm=tm, name=name)


def _mm(a, b, *, ta=False, tb=False, bias=None, out_dtype=F32, name, tm=1024, tn=1024, tk=1024):
    (k_a, m) = a.shape if ta else a.shape[::-1]
    (n, k_b) = b.shape if tb else b.shape[::-1]
    assert k_a == k_b, (name, a.shape, b.shape)
    tm, tn, tk = _pick(m, tm) if m % V7X_LANES == 0 else m, _pick(n, tn), _pick(k_a, tk) if k_a % V7X_LANES == 0 else k_a
    nk = k_a // tk
    a_spec = pl.BlockSpec((tk, tm), lambda i, j, k: (k, i)) if ta else pl.BlockSpec((tm, tk), lambda i, j, k: (i, k))
    b_spec = pl.BlockSpec((tn, tk), lambda i, j, k: (j, k)) if tb else pl.BlockSpec((tk, tn), lambda i, j, k: (k, j))
    dims = (((0,) if ta else (1,), (1,) if tb else (0,)), ((), ()))
    has_bias = bias is not None

    def body(*refs):
        a_ref, b_ref = refs[0], refs[1]
        part = lax.dot_general(a_ref[...].astype(BF16), b_ref[...].astype(BF16), dims, preferred_element_type=F32)
        if nk == 1:
            o_ref = refs[-1]
            o_ref[...] = (part + refs[2][...] if has_bias else part).astype(o_ref.dtype)
            return
        o_ref, acc_ref = refs[-2], refs[-1]
        k = pl.program_id(2)

        @pl.when(k == 0)
        def _():
            acc_ref[...] = part

        @pl.when(k > 0)
        def _():
            acc_ref[...] += part

        @pl.when(k == nk - 1)
        def _():
            r = acc_ref[...]
            if has_bias:
                r = r + refs[2][...]
            o_ref[...] = r.astype(o_ref.dtype)

    in_specs = [a_spec, b_spec] + ([pl.BlockSpec((1, tn), lambda i, j, k: (0, j))] if has_bias else [])
    return pl.pallas_call(
        body, name=name, grid=(m // tm, n // tn, nk), in_specs=in_specs,
        out_specs=pl.BlockSpec((tm, tn), lambda i, j, k: (i, j)),
        out_shape=jax.ShapeDtypeStruct((m, n), out_dtype),
        scratch_shapes=[pltpu.VMEM((tm, tn), F32)] if nk > 1 else [],
        compiler_params=_cparams("parallel", "parallel", "arbitrary"),
    )(a, b, *([bias] if has_bias else []))


def _seg_order(a):
    t, c = a.shape
    return a.reshape(V7X_SUBLANES, t // V7X_SUBLANES, c).transpose(1, 0, 2).reshape(t, c)


def _time_order(a):
    t, c = a.shape
    return a.reshape(t // V7X_SUBLANES, V7X_SUBLANES, c).transpose(1, 0, 2).reshape(t, c)


def _scan_in_place(o_ref, tab_ref, reverse, h_ref=None, da_ref=None):
    ntile = o_ref.shape[0] // V7X_SUBLANES
    with_da = h_ref is not None
    ln = SCAN_LANES
    rows8 = V7X_SUBLANES
    subs = range(o_ref.shape[1] // (2 * ln))
    col = lambda s, part: slice((2 * s + part) * ln, (2 * s + part + 1) * ln)
    cmul = lambda p, q: (p[0] * q[0] - p[1] * q[1], p[0] * q[1] + p[1] * q[0])
    cadd = lambda p, q: (p[0] + q[0], p[1] + q[1])

    def run():
        row = lax.broadcasted_iota(jnp.int32, (rows8, ln), 0)
        bcast = lambda k, s: tuple(jnp.broadcast_to(tab_ref[k:k + 1, col(s, part)], (rows8, ln)) for part in (0, 1))
        a = [bcast(0, s) for s in subs]
        zero = [(jnp.zeros((rows8, ln), F32), jnp.zeros((rows8, ln), F32)) for _ in subs]

        def tile(ref, i):
            r0 = pl.multiple_of(i * rows8, rows8)
            return [(ref[pl.ds(r0, rows8), col(s, 0)], ref[pl.ds(r0, rows8), col(s, 1)]) for s in subs]

        def put(i, ys):
            r0 = pl.multiple_of(i * rows8, rows8)
            for s, (yr, yi) in enumerate(ys):
                o_ref[pl.ds(r0, rows8), col(s, 0)] = yr
                o_ref[pl.ds(r0, rows8), col(s, 1)] = yi

        def shifted(z, step, up):
            shift, keep = (rows8 - step, row < rows8 - step) if up else (step, row >= step)
            return tuple(jnp.where(keep, pltpu.roll(v, shift, 0), 0.0) for v in z)

        def pass1(n, carry):
            i = (ntile - 1 - n) if reverse else n
            ys = [cadd(x, cmul(a[s], carry[s])) for s, x in enumerate(tile(o_ref, i))]
            put(i, ys)
            return ys

        ends = lax.fori_loop(0, ntile, pass1, zero, unroll=4)
        incoming = []
        for s in subs:
            f = ends[s]
            for k, step in zip((1, 2, 3), (1, 2, 4)):
                f = cadd(f, cmul(bcast(k, s), shifted(f, step, reverse)))
            incoming.append(shifted(f, 1, reverse))
        if with_da:
            h0 = [shifted(h, 1, False) for h in tile(h_ref, ntile - 1)]

        def pass2(n, carry):
            i = (ntile - 1 - n) if reverse else n
            power = carry[0]
            ys = [cadd(y, cmul(power[s], incoming[s])) for s, y in enumerate(tile(o_ref, i))]
            put(i, ys)
            new_power = [cmul(power[s], a[s]) for s in subs]
            if not with_da:
                return (new_power,)
            prev = tile(h_ref, jnp.maximum(i - 1, 0))
            acc = []
            for s in subs:
                hpr, hpi = jnp.where(i > 0, prev[s][0], h0[s][0]), jnp.where(i > 0, prev[s][1], h0[s][1])
                (yr, yi), (sr, si) = ys[s], carry[1][s]
                acc.append((sr + yr * hpr + yi * hpi, si + yi * hpr - yr * hpi))
            return (new_power, acc)

        out = lax.fori_loop(0, ntile, pass2, (list(a),) + ((zero,) if with_da else ()), unroll=4)
        if with_da:
            for s in subs:
                da_ref[:, col(s, 0)] = out[1][s][0]
                da_ref[:, col(s, 1)] = out[1][s][1]

    run()


SCAN_BLOCK = 4 * SCAN_LANES
_S5_SUB = 2 * S5_CH // S5_BD // SCAN_BLOCK


def _s5_specs(t):
    ku, blk = S5_WIDTH // S5_BD, SCAN_BLOCK
    return dict(
        u=pl.BlockSpec((t, ku), lambda j: (0, j // _S5_SUB)),
        hs=pl.BlockSpec((t, blk), lambda j: (0, j)),
        wb=pl.BlockSpec((None, ku, blk), lambda j: (j // _S5_SUB, 0, j % _S5_SUB)),
        wc=pl.BlockSpec((None, blk, ku), lambda j: (j // _S5_SUB, j % _S5_SUB, 0)),
        tab=pl.BlockSpec((V7X_SUBLANES, blk), lambda j: (0, j)))


def _s5_fwd(u, wb, wc, tab, *, name):
    t = u.shape[0]
    sp = _s5_specs(t)

    def body(u_ref, wb_ref, wc_ref, tab_ref, hs_ref, y_ref):
        hs_ref[...] = jnp.dot(u_ref[...], wb_ref[...], preferred_element_type=F32)
        _scan_in_place(hs_ref, tab_ref, False)
        y = jnp.dot(hs_ref[...].astype(BF16), wc_ref[...], preferred_element_type=F32)
        first = pl.program_id(0) % _S5_SUB == 0

        @pl.when(first)
        def _():
            y_ref[...] = y

        @pl.when(jnp.logical_not(first))
        def _():
            y_ref[...] += y

    return pl.pallas_call(
        body, name=name, grid=(2 * S5_CH // SCAN_BLOCK,),
        in_specs=[sp['u'], sp['wb'], sp['wc'], sp['tab']], out_specs=[sp['hs'], sp['u']],
        out_shape=[jax.ShapeDtypeStruct((t, 2 * S5_CH), F32), jax.ShapeDtypeStruct((t, S5_WIDTH), F32)],
        compiler_params=_cparams("arbitrary"),
    )(u, wb, wc, tab)


def _s5_bwd(dy, u, hs, wb, wc, tab, *, name):
    t = u.shape[0]
    sp = _s5_specs(t)
    nt, tn = (((1,), (1,)), ((), ())), (((0,), (0,)), ((), ()))

    def body(dy_ref, u_ref, hs_ref, wb_ref, wc_ref, tab_ref, du_ref, dwb_ref, dwc_ref, da_ref, g_ref):
        g_ref[...] = lax.dot_general(dy_ref[...], wc_ref[...], nt, preferred_element_type=F32)
        dwc_ref[...] = lax.dot_general(hs_ref[...].astype(BF16), dy_ref[...], tn, preferred_element_type=F32)
        _scan_in_place(g_ref, tab_ref, True, hs_ref, da_ref)
        gb = g_ref[...].astype(BF16)
        dwb_ref[...] = lax.dot_general(u_ref[...], gb, tn, preferred_element_type=F32)
        du = lax.dot_general(gb, wb_ref[...], nt, preferred_element_type=F32)
        first = pl.program_id(0) % _S5_SUB == 0

        @pl.when(first)
        def _():
            du_ref[...] = du

        @pl.when(jnp.logical_not(first))
        def _():
            du_ref[...] += du

    return pl.pallas_call(
        body, name=name, grid=(2 * S5_CH // SCAN_BLOCK,),
        in_specs=[sp['u'], sp['u'], sp['hs'], sp['wb'], sp['wc'], sp['tab']],
        out_specs=[sp['u'], sp['wb'], sp['wc'], sp['tab']],
        out_shape=[jax.ShapeDtypeStruct((t, S5_WIDTH), F32), jax.ShapeDtypeStruct(wb.shape, F32),
                   jax.ShapeDtypeStruct(wc.shape, F32), jax.ShapeDtypeStruct((V7X_SUBLANES, 2 * S5_CH), F32)],
        scratch_shapes=[pltpu.VMEM((t, SCAN_BLOCK), F32)],
        compiler_params=_cparams("arbitrary"),
    )(dy, u, hs, wb, wc, tab)


ATT_SCALE = (QK_NOPE + QK_ROPE) ** -0.5
Q_SCALE = ATT_SCALE * math.log2(math.e)
LN2 = math.log(2.0)


def _att_mask(qi, kj, tb):
    qc = (qi * tb + lax.broadcasted_iota(jnp.int32, (tb, tb), 0)) // CHUNK
    kc = (kj * tb + lax.broadcasted_iota(jnp.int32, (tb, tb), 1)) // CHUNK
    return kc <= qc


def _with_side(side, n_main_in, n_main_out, refs, step, nsteps, compute):
    if side is None:
        compute(refs)
        return
    n = side.n
    main = refs[:n_main_in] + refs[n_main_in + n:n_main_in + n + n_main_out] + refs[n_main_in + 2 * n + n_main_out + 3:]
    x_refs = refs[n_main_in:n_main_in + n]
    y_refs = refs[n_main_in + n + n_main_out:n_main_in + 2 * n + n_main_out]
    side_refs = (x_refs, y_refs) + tuple(refs[n_main_in + 2 * n + n_main_out:n_main_in + 2 * n + n_main_out + 3])
    pl.when(step == 0)(functools.partial(side.start, *side_refs))
    compute(main)
    pl.when(step == (7 * nsteps) // 8)(functools.partial(side.relay, *side_refs))
    pl.when(step == nsteps - 1)(functools.partial(side.finish, *side_refs))


def _side_call(body, side, *, name, grid, in_specs, out_specs, out_shape, scratch, args, sem):
    n_out = len(out_shape)
    if side is not None:
        in_specs, args = in_specs + side.specs, list(args) + side.xs
        out_specs, out_shape = out_specs + side.specs, out_shape + side.out_shape
        scratch = side.scratch + scratch
        params = pltpu.CompilerParams(dimension_semantics=("arbitrary",) * len(grid), vmem_limit_bytes=VMEM_LIMIT,
                                      has_side_effects=True)
    else:
        params = _cparams(*sem)
    res = pl.pallas_call(body, name=name, grid=grid, in_specs=in_specs, out_specs=out_specs, out_shape=out_shape,
                         scratch_shapes=scratch, compiler_params=params)(*args)
    return res[:n_out], res[n_out:]


def _attn_fwd(q, k, v, *, name, side=None):
    h, t, dq = q.shape
    dv = v.shape[2]
    tb = min(ATT_BLOCK, t)
    nblk = t // tb
    nt = (((1,), (1,)), ((), ()))

    def compute(refs):
        q_ref, k_ref, v_ref, o_ref, lse_ref = refs
        i = pl.program_id(1)
        qb = q_ref[...]

        def kv_step(j, carry, masked):
            m, l, acc = carry
            r0 = pl.multiple_of(j * tb, tb)
            s = lax.dot_general(qb, k_ref[pl.ds(r0, tb), :], nt, preferred_element_type=F32)
            if masked:
                s = jnp.where(_att_mask(i, j, tb), s, NEG_INF)
            m_new = jnp.maximum(m, jnp.max(s, axis=1, keepdims=True))
            alpha = jnp.exp2(m - m_new)
            p = jnp.exp2(s - m_new)
            l = alpha * l + jnp.sum(p, axis=1, keepdims=True)
            acc = alpha * acc + jnp.dot(p.astype(BF16), v_ref[pl.ds(r0, tb), :], preferred_element_type=F32)
            return m_new, l, acc

        init = (jnp.full((tb, 1), NEG_INF, F32), jnp.zeros((tb, 1), F32), jnp.zeros((tb, dv), F32))
        carry = lax.fori_loop(0, i, functools.partial(kv_step, masked=False), init)
        m, l, acc = kv_step(i, carry, True)
        o_ref[...] = acc / l
        lse_ref[...] = m + jnp.log2(l)

    def body(*refs):
        _with_side(side, 3, 2, refs, pl.program_id(0) * nblk + pl.program_id(1), h * nblk, compute)

    return _side_call(
        body, side, name=name, grid=(h, nblk),
        in_specs=[pl.BlockSpec((None, tb, dq), lambda hh, i: (hh, i, 0)),
                  pl.BlockSpec((None, t, dq), lambda hh, i: (hh, 0, 0)),
                  pl.BlockSpec((None, t, dv), lambda hh, i: (hh, 0, 0))],
        out_specs=[pl.BlockSpec((None, tb, dv), lambda hh, i: (hh, i, 0)),
                   pl.BlockSpec((None, tb, 1), lambda hh, i: (hh, i, 0))],
        out_shape=[jax.ShapeDtypeStruct((h, t, dv), F32), jax.ShapeDtypeStruct((h, t, 1), F32)],
        scratch=[], args=[q, k, v], sem=("parallel", "parallel"))


def _attn_bwd(q, k, v, o, lse, do, *, name, side=None):
    h, t, dq_w = q.shape
    dv_w = v.shape[2]
    tb = min(ATT_BLOCK, t)
    nblk = t // tb
    nt = (((1,), (1,)), ((), ()))
    tn = (((0,), (0,)), ((), ()))

    def compute(refs):
        q_ref, k_ref, v_ref, o_ref, lse_ref, do_ref, dq_ref, dk_ref, dv_ref, delta_ref = refs
        j = pl.program_id(1)

        @pl.when(j == 0)
        def _():
            dq_ref[...] = jnp.zeros_like(dq_ref)

            def dstep(i, c):
                r0 = pl.multiple_of(i * tb, tb)
                delta_ref[pl.ds(r0, tb), :] = jnp.sum(do_ref[pl.ds(r0, tb), :].astype(F32) * o_ref[pl.ds(r0, tb), :],
                                                      axis=1, keepdims=True)
                return c

            lax.fori_loop(0, nblk, dstep, 0)

        kb, vb = k_ref[...], v_ref[...]

        def q_step(i, carry, masked):
            dk, dv = carry
            r0 = pl.multiple_of(i * tb, tb)
            qb, dob = q_ref[pl.ds(r0, tb), :], do_ref[pl.ds(r0, tb), :]
            s = lax.dot_general(qb, kb, nt, preferred_element_type=F32)
            if masked:
                s = jnp.where(_att_mask(i, j, tb), s, NEG_INF)
            p = jnp.exp2(s - lse_ref[pl.ds(r0, tb), :])
            dv = dv + lax.dot_general(p.astype(BF16), dob, tn, preferred_element_type=F32)
            dp = lax.dot_general(dob, vb, nt, preferred_element_type=F32)
            ds = (p * (dp - delta_ref[pl.ds(r0, tb), :]) * LN2).astype(BF16)
            dk = dk + lax.dot_general(ds, qb, tn, preferred_element_type=F32)
            dq_ref[pl.ds(r0, tb), :] += jnp.dot(ds, kb, preferred_element_type=F32)
            return dk, dv

        carry = q_step(j, (jnp.zeros((tb, dq_w), F32), jnp.zeros((tb, dv_w), F32)), True)
        dk, dv = lax.fori_loop(j + 1, nblk, functools.partial(q_step, masked=False), carry)
        dk_ref[...] = dk
        dv_ref[...] = dv

    def body(*refs):
        _with_side(side, 6, 3, refs, pl.program_id(0) * nblk + pl.program_id(1), h * nblk, compute)

    whole = lambda w: pl.BlockSpec((None, t, w), lambda hh, j: (hh, 0, 0))
    blockj = lambda w: pl.BlockSpec((None, tb, w), lambda hh, j: (hh, j, 0))
    return _side_call(
        body, side, name=name, grid=(h, nblk),
        in_specs=[whole(dq_w), blockj(dq_w), blockj(dv_w), whole(dv_w), whole(1), whole(dv_w)],
        out_specs=[whole(dq_w), blockj(dq_w), blockj(dv_w)],
        out_shape=[jax.ShapeDtypeStruct((h, t, dq_w), F32), jax.ShapeDtypeStruct((h, t, dq_w), F32),
                   jax.ShapeDtypeStruct((h, t, dv_w), F32)],
        scratch=[pltpu.VMEM((t, 1), F32)], args=[q, k, v, o, lse, do], sem=("parallel", "arbitrary"))


class _Exchange:
    def __init__(self, xs, gather, two_level=False):
        assert gather or not two_level
        self.xs, self.gather, self.n, self.two_level = list(xs), gather, len(xs), two_level
        shapes = [tuple(x.shape) if gather else tuple(x.shape[1:]) for x in xs]
        self.out_shape = [jax.ShapeDtypeStruct((N_DEV,) + shp, x.dtype) for shp, x in zip(shapes, xs)]
        self.specs = [pl.BlockSpec(memory_space=pl.ANY)] * self.n
        self.scratch = [pltpu.SemaphoreType.DMA((self.n, N_DEV - 1)), pltpu.SemaphoreType.DMA((self.n, N_DEV - 1)),
                        pltpu.SemaphoreType.DMA((self.n,))]

    def copies(self, x_refs, y_refs, send_sems, recv_sems, local_sems):
        mx, my, mc = lax.axis_index("x"), lax.axis_index("y"), lax.axis_index("c")
        me = 4 * mx + 2 * my + mc
        out = [pltpu.make_async_copy(x_refs[i] if self.gather else x_refs[i].at[me], y_refs[i].at[me], local_sems.at[i])
               for i in range(self.n)]
        for k in range(1, N_DEV):
            px = 1 - mx if k & 4 else mx
            py = 1 - my if k & 2 else my
            pc = 1 - mc if k & 1 else mc
            for i in range(self.n):
                out.append(pltpu.make_async_remote_copy(
                    src_ref=x_refs[i] if self.gather else x_refs[i].at[4 * px + 2 * py + pc], dst_ref=y_refs[i].at[me],
                    send_sem=send_sems.at[i, k - 1], recv_sem=recv_sems.at[i, k - 1],
                    device_id=(px, py, pc), device_id_type=pl.DeviceIdType.MESH))
        return out


    def _two_level(self, x_refs, y_refs, send_sems, recv_sems, local_sems):
        mx, my, mc = lax.axis_index("x"), lax.axis_index("y"), lax.axis_index("c")
        sib = (mx, my, 1 - mc)
        chips = [(1 - mx, my), (mx, 1 - my), (1 - mx, 1 - my)]
        idx = lambda px, py, pc: 4 * px + 2 * py + pc
        me = idx(mx, my, mc)

        def rc(i, k, src, block, to):
            return pltpu.make_async_remote_copy(
                src_ref=src, dst_ref=y_refs[i].at[block], send_sem=send_sems.at[i, k], recv_sem=recv_sems.at[i, k],
                device_id=to, device_id_type=pl.DeviceIdType.MESH)

        rng = range(self.n)
        over_ici = [(j, chip, i) for j, chip in enumerate(chips) for i in rng]
        return dict(
            local=lambda: [pltpu.make_async_copy(x_refs[i], y_refs[i].at[me], local_sems.at[i]) for i in rng],
            own=lambda: [rc(i, 0, x_refs[i], me, sib) for i in rng]
            + [rc(i, 1 + j, x_refs[i], me, (*chip, mc)) for j, chip, i in over_ici],
            relay=lambda: [rc(i, 4 + j, y_refs[i].at[idx(*chip, mc)], idx(*chip, mc), sib) for j, chip, i in over_ici],
            landed=lambda: [rc(i, 1 + j, x_refs[i], idx(*chip, mc), sib) for j, chip, i in over_ici],
            last=lambda: [rc(i, 0, x_refs[i], idx(*sib), sib) for i in rng]
            + [rc(i, 4 + j, x_refs[i], idx(*chip, 1 - mc), sib) for j, chip, i in over_ici])

    def start(self, *refs):
        if not self.two_level:
            for cp in self.copies(*refs):
                cp.start()
            return
        plan = self._two_level(*refs)
        for cp in plan['local']() + plan['own']():
            cp.start()

    def relay(self, *refs):
        if not self.two_level:
            return
        plan = self._two_level(*refs)
        for arrived, cp in zip(plan['landed'](), plan['relay']()):
            arrived.wait_recv()
            cp.start()

    def finish(self, *refs):
        if not self.two_level:
            for cp in self.copies(*refs):
                cp.wait()
            return
        plan = self._two_level(*refs)
        for cp in plan['last']():
            cp.wait_recv()
        for cp in plan['own']() + plan['relay']():
            cp.wait_send()
        for cp in plan['local']():
            cp.wait()


def _exchange(xs, *, gather, name, two_level=False):
    ex = _Exchange(xs, gather, two_level)
    n = ex.n

    def body(*refs):
        refs = (refs[:n], refs[n:2 * n]) + tuple(refs[2 * n:])
        ex.start(*refs)
        ex.relay(*refs)
        ex.finish(*refs)

    return pl.pallas_call(
        body, name=name, out_shape=ex.out_shape, in_specs=ex.specs, out_specs=ex.specs, scratch_shapes=ex.scratch,
        compiler_params=pltpu.CompilerParams(has_side_effects=True),
    )(*ex.xs)


def _adamw(w, gs, m, v, *, name, tm=256):
    nl = len(gs)
    parts = gs[0].ndim == 3
    c = w.shape[1]
    r = w.shape[0] // nl
    tm = _pick_rows(r, tm)
    nrow = r // tm

    def body(*refs):
        w_ref, g_refs, (m_ref, v_ref, go_ref, d_ref, mo_ref, vo_ref) = refs[0], refs[1:1 + nl], refs[1 + nl:]

        def update(g_ref):
            if parts:
                gv = g_ref[0].astype(F32)
                for k in range(1, N_DEV):
                    gv = gv + g_ref[k].astype(F32)
            else:
                gv = g_ref[...]
            mn = ADAM_B1 * m_ref[...] + (1.0 - ADAM_B1) * gv
            vn = ADAM_B2 * v_ref[...] + (1.0 - ADAM_B2) * jnp.square(gv)
            m_hat = mn / (1.0 - ADAM_B1 ** ADAM_STEP)
            v_hat = vn / (1.0 - ADAM_B2 ** ADAM_STEP)
            go_ref[...] = gv
            d_ref[...] = -ADAM_LR * (m_hat / (jnp.sqrt(v_hat) + ADAM_EPS) + ADAM_WD * w_ref[...])
            mo_ref[...] = mn
            vo_ref[...] = vn

        if nl == 1:
            update(g_refs[0])
        else:
            for layer, g_ref in enumerate(g_refs):
                pl.when(pl.program_id(0) == layer)(functools.partial(update, g_ref))

    spec = pl.BlockSpec((tm, c), lambda l, i: (l * nrow + i, 0))

    def gspec(layer):
        row = lambda l, i: jnp.where(l == layer, i, 0)
        if parts:
            return pl.BlockSpec((N_DEV, tm, c), lambda l, i: (0, row(l, i), 0))
        return pl.BlockSpec((tm, c), lambda l, i: (row(l, i), 0))

    return pl.pallas_call(
        body, name=name, grid=(nl, nrow), in_specs=[spec] + [gspec(k) for k in range(nl)] + [spec, spec],
        out_specs=[spec] * 4, out_shape=[jax.ShapeDtypeStruct(w.shape, F32)] * 4,
        compiler_params=_cparams("arbitrary", "arbitrary"),
    )(w, *gs, m, v)


def _sum_parts(x, *, name):
    def body(x_ref, o_ref):
        acc = x_ref[0]
        for k in range(1, N_DEV):
            acc = acc + x_ref[k]
        o_ref[...] = acc

    return pl.pallas_call(body, name=name, out_shape=jax.ShapeDtypeStruct(x.shape[1:], x.dtype))(x)


def _permute_in(a):
    pad = jnp.zeros(a.shape[:-1] + (IN_PAD - IN_WIDTH,), a.dtype)
    return jnp.concatenate([a[..., lo:hi] for lo, hi in _IN_SEGMENTS] + [pad], axis=-1)


def _unpermute_in(a):
    out, pos = {}, 0
    for lo, hi in _IN_SEGMENTS:
        out[lo] = a[..., pos:pos + hi - lo]
        pos += hi - lo
    return jnp.concatenate([out[lo] for lo in sorted(out)], axis=-1)


def _full_from_gathered(g, kind):
    if kind == 'row':
        return g.reshape((-1,) + g.shape[2:])
    if g.shape[-1] % V7X_LANES == 0:
        return jnp.moveaxis(g, 0, -2).reshape(g.shape[1:-1] + (-1,))
    return jnp.concatenate([g[d] for d in range(N_DEV)], axis=-1)


def _contrib_from_full(g, kind):
    if kind == 'row':
        return g.reshape((N_DEV, -1) + g.shape[1:])
    ns = g.shape[-1] // N_DEV
    if ns % V7X_LANES == 0:
        return jnp.moveaxis(g.reshape(g.shape[:-1] + (N_DEV, ns)), -2, 0)
    return jnp.stack([g[..., d * ns:(d + 1) * ns] for d in range(N_DEV)])


def _in_runs(ns):
    runs, pos = [], 0
    for lo, hi in _IN_SEGMENTS:
        for d in range(lo // ns, (hi - 1) // ns + 1):
            a, b = max(lo, d * ns), min(hi, (d + 1) * ns)
            runs.append((d, a - d * ns, b - d * ns, pos))
            pos += b - a
    return runs


def _w_in_from_gathered(g):
    pieces = [g[d][:, a:b] for d, a, b, _ in _in_runs(g.shape[2])]
    pad = jnp.zeros((g.shape[1], IN_PAD - IN_WIDTH), g.dtype)
    return jnp.concatenate(pieces + [pad], axis=1)


def _w_in_contrib(gp):
    ns = IN_WIDTH // N_DEV
    per_dev = [[] for _ in range(N_DEV)]
    for d, a, b, pos in sorted(_in_runs(ns), key=lambda r: (r[0], r[1])):
        per_dev[d].append(gp[:, pos:pos + b - a])
    return jnp.stack([jnp.concatenate(p, axis=1) for p in per_dev])


def _as2d(a, lead=0):
    return a.reshape(a.shape[:lead] + (-1, a.shape[-1]))


def _chan_cols(re, im):
    lead = re.shape[:-1]
    nb = S5_CH // SCAN_LANES
    return jnp.stack([re.reshape(lead + (nb, SCAN_LANES)), im.reshape(lead + (nb, SCAN_LANES))],
                     axis=-2).reshape(lead + (2 * S5_CH,))


def _s5_tables(lam_re, lam_im, log_dt, b_re, b_im, c_re, c_im):
    dt = jnp.exp(log_dt)[:, None]
    mag = jnp.exp(lam_re * dt)
    a_re = mag * jnp.cos(lam_im * dt)
    a_im = mag * jnp.sin(lam_im * dt)
    den = lam_re * lam_re + lam_im * lam_im
    f_re = ((a_re - 1.0) * lam_re + a_im * lam_im) / den
    f_im = (a_im * lam_re - (a_re - 1.0) * lam_im) / den
    bb_re = f_re[..., None] * b_re - f_im[..., None] * b_im
    bb_im = f_re[..., None] * b_im + f_im[..., None] * b_re
    gb = S5_GROUPS // S5_BD
    eye = jnp.eye(gb, dtype=F32)
    blocks = lambda a: a.reshape((S5_BD, gb) + a.shape[1:])

    def cols(re, im):
        shp = (S5_BD, S5_WIDTH // S5_BD, -1, SCAN_LANES)
        return jnp.stack([re.reshape(shp), im.reshape(shp)], axis=-2).reshape(S5_BD, S5_WIDTH // S5_BD, -1)

    flat = lambda a: a.reshape(S5_BD, S5_WIDTH // S5_BD, -1)
    wb_c = cols(flat(jnp.einsum('kgpc,gh->kgchp', blocks(bb_re), eye)), flat(jnp.einsum('kgpc,gh->kgchp', blocks(bb_im), eye)))
    wc_c = cols(flat(jnp.einsum('kgcp,gh->khcgp', blocks(c_re), eye)), -flat(jnp.einsum('kgcp,gh->khcgp', blocks(c_im), eye)))
    a_row = _chan_cols(a_re.reshape(1, S5_CH), a_im.reshape(1, S5_CH))
    return wb_c, wc_c.transpose(0, 2, 1), a_row


def _scan_tables(a_row, conj, seg_len):
    nb = S5_CH // SCAN_LANES
    a = a_row.reshape(nb, 2, SCAN_LANES)
    base = (a[:, 0], -a[:, 1] if conj else a[:, 1])
    mul = lambda x, y: (x[0] * y[0] - x[1] * y[1], x[0] * y[1] + x[1] * y[0])
    seg, sq, e = None, base, seg_len
    while e:
        if e & 1:
            seg = sq if seg is None else mul(seg, sq)
        sq, e = mul(sq, sq), e >> 1
    seg2 = mul(seg, seg)
    rows = [base, seg, seg2, mul(seg2, seg2)]
    lay = lambda z: jnp.stack([z[0], z[1]], axis=1).reshape(-1)
    return jnp.stack([lay(z) for z in rows] + [jnp.zeros((2 * S5_CH,), F32)] * (V7X_SUBLANES - len(rows)))


def _rope_tables(t):
    half = QK_ROPE // 2
    inv_freq = 1.0 / (ROPE_THETA ** (jnp.arange(0, QK_ROPE, 2, dtype=F32) / QK_ROPE))
    ang = jnp.arange(t, dtype=F32)[:, None] * inv_freq[None, :]
    cos, sin = jnp.cos(ang), jnp.sin(ang)
    zero = jnp.zeros_like(sin)

    def lay(nope, width, first, second):
        head = jnp.concatenate([jnp.full((t, nope), 1.0 if first is cos else 0.0, F32), first, second], axis=1)
        reps = width // head.shape[1]
        out = jnp.tile(head, (1, reps))
        return jnp.pad(out, ((0, 0), (0, width - out.shape[1])))

    hq = MLA_HEADS * (QK_NOPE + QK_ROPE)
    q_tabs = (lay(QK_NOPE, hq, cos, cos), lay(QK_NOPE, hq, -sin, zero), lay(QK_NOPE, hq, zero, sin))
    k_tabs = (lay(0, V7X_LANES, cos, cos)[:, :V7X_LANES] * (jnp.arange(V7X_LANES) < QK_ROPE),
              lay(0, V7X_LANES, -sin, zero) * (jnp.arange(V7X_LANES) < QK_ROPE),
              lay(0, V7X_LANES, zero, sin) * (jnp.arange(V7X_LANES) < QK_ROPE))
    return q_tabs, k_tabs


def _sgu_tables(w_s, b_s):
    pos = jnp.arange(SGU_CHUNK) // CHUNK
    mask = pos[None, :] <= pos[:, None]
    wm = jnp.where(mask[None], w_s, 0.0).reshape(SGU_GROUPS * SGU_CHUNK, SGU_CHUNK)
    bias = jnp.repeat(b_s.T, SGU_WIDTH // SGU_GROUPS, axis=1)
    return wm, bias


def _row(v):
    return v.reshape(1, -1)


_S5_PARAMS = ('s5_lambda_re', 's5_lambda_im', 's5_log_dt', 's5_b_re', 's5_b_im', 's5_c_re', 's5_c_im')


def _derived_tables(p, t):
    (wb, wc, a_row), s5_pull = jax.vjp(jax.vmap(_s5_tables), *[p[n] for n in _S5_PARAMS])
    (wm, bias), sgu_pull = jax.vjp(jax.vmap(_sgu_tables), p['sgu_w_s'], p['sgu_b_s'])
    tab_f = jax.vmap(lambda a: _scan_tables(a, False, t // V7X_SUBLANES))(a_row)
    tab_b = jax.vmap(lambda a: _scan_tables(a, True, t // V7X_SUBLANES))(a_row)
    wb, wc = wb.astype(BF16), wc.astype(BF16)
    per_layer = [dict(s5_wb=wb[l], s5_wc=wc[l], s5_tab_fwd=tab_f[l], s5_tab_bwd=tab_b[l], sgu_wm=wm[l], sgu_bias=bias[l])
                 for l in range(len(wm))]

    def pull(grads):
        stacked = lambda k: jnp.stack([g[k] for g in grads])
        out = dict(zip(_S5_PARAMS, s5_pull((stacked('s5_wb'), stacked('s5_wc'), stacked('s5_a')))))
        out['sgu_w_s'], out['sgu_b_s'] = sgu_pull((stacked('sgu_wm'), stacked('sgu_bias')))
        return out

    return per_layer, pull


def _layer_fwd(x, ada, w, rope_tabs, tag, side=None, after_attn=None):
    s = {'x': x}
    q_tabs, k_tabs = rope_tabs
    sc1, gt1, sc2, gt2 = _row(1.0 + ada[1]), _row(1.0 + ada[2]), _row(1.0 + ada[4]), _row(1.0 + ada[5])
    s.update(sc1=sc1, gt1=gt1, sc2=sc2, gt2=gt2)
    (h,) = _rowcall(_modulate_fn, [x], [sc1, _row(ada[0])], [(D_MODEL, BF16)], tm=512, name=f"mod1_{tag}")
    proj = _mm(h, w['w_in_p'], bias=w['b_in_p'], name=f"proj_{tag}", tn=1792)
    s.update(h=h, proj=proj)

    u_view = (proj, S5_WIDTH, P_S5 // S5_WIDTH)
    u_seg = _seg_order(proj[:, P_S5:P_S5 + S5_WIDTH]).astype(BF16)
    hs, ylin = _s5_fwd(u_seg, w['s5_wb'], w['s5_wc'], w['s5_tab_fwd'], name=f"s5_fwd_{tag}")
    ylin = _time_order(ylin)
    s5_full = [_row(w['s5_d']), w['s5_w_glu'], _row(w['s5_b_glu'])]
    (y_s5,) = _rowcall(_s5_post_fn, [ylin, u_view], s5_full, [(S5_WIDTH, BF16)], tm=256, name=f"s5_post_{tag}")
    s.update(u_seg=u_seg, hs=hs, ylin=ylin, y_s5=y_s5)

    mla_rows = [(proj, MLA_BLK, P_MLA // MLA_BLK), *q_tabs, *k_tabs]
    mla_full = [_row(w['mla_q_norm']), w['mla_w_q_up'], _row(w['mla_kv_norm']), w['mla_w_kv_up']]
    hq, hkv = MLA_HEADS * (QK_NOPE + QK_ROPE), MLA_HEADS * (QK_NOPE + V_HEAD)
    q_r, kv, kpe_r = _rowcall(_mla_pre_fn, mla_rows, mla_full, [(hq, BF16), (hkv, BF16), (V7X_LANES, BF16)],
                              tm=256, name=f"mla_pre_{tag}")
    t = x.shape[0]
    qh = q_r.reshape(t, MLA_HEADS, -1).transpose(1, 0, 2)
    kv3 = kv.reshape(t, MLA_HEADS, -1).transpose(1, 0, 2)
    kh = jnp.concatenate([kv3[:, :, :QK_NOPE], jnp.broadcast_to(kpe_r[None, :, :QK_ROPE], (MLA_HEADS, t, QK_ROPE))], axis=2)
    vh = kv3[:, :, QK_NOPE:]
    (o, lse), side_out = _attn_fwd(qh, kh, vh, name=f"attn_fwd_{tag}", side=side)
    if after_attn is not None:
        after_attn(side_out)
    y_mla = o.transpose(1, 0, 2).reshape(t, -1).astype(BF16)
    s.update(qh=qh, kh=kh, vh=vh, o=o, lse=lse, y_mla=y_mla)

    sgu_rows = [(proj, SGU_WIDTH, P_USGU // SGU_WIDTH), (proj, SGU_WIDTH, P_VSGU // SGU_WIDTH)]
    sgu_full = [_row(w['sgu_ln_g']), _row(w['sgu_ln_b']), w['sgu_wm'], w['sgu_bias']]
    (y_sgu,) = _rowcall(_sgu_fn, sgu_rows, sgu_full, [(SGU_WIDTH, BF16)], tm=SGU_CHUNK, name=f"sgu_{tag}")
    s.update(sgu_full=sgu_full, y_sgu=y_sgu)

    wbr = w['w_branch'].reshape(-1, D_MODEL)
    gate_rows = [(proj, D_MODEL, b) for b in range(3)]
    (merged,) = _rowcall(_merge_fn, [y_s5, y_mla, y_sgu] + gate_rows, [wbr], [(D_MODEL, BF16)], tm=256, name=f"merge_{tag}")
    ymix = _mm(merged, w['w_out'], name=f"wout_{tag}")
    (x1,) = _rowcall(_ln_res_fn, [x, ymix], [gt1, _row(w['ln1_g']), _row(w['ln1_b'])], [(D_MODEL, F32)], tm=256,
                     name=f"ln1_{tag}")
    s.update(merged=merged, ymix=ymix, x1=x1)

    (h2,) = _rowcall(_modulate_fn, [x1], [sc2, _row(ada[3])], [(D_MODEL, BF16)], tm=512, name=f"mod2_{tag}")
    ab = _mm(h2, w['ffn_w_in'], name=f"ffn_in_{tag}", tn=1408)
    (act,) = _rowcall(_swiglu_fn, [(ab, FF_HIDDEN, 0), (ab, FF_HIDDEN, 1)], [], [(FF_HIDDEN, BF16)], tm=256,
                      name=f"swiglu_{tag}")
    f = _mm(act, w['ffn_w_out'], name=f"ffn_out_{tag}", tk=2816)
    (x2,) = _rowcall(_ln_res_fn, [x1, f], [gt2, _row(w['ln2_g']), _row(w['ln2_b'])], [(D_MODEL, F32)], tm=256,
                     name=f"ln2_{tag}")
    s.update(h2=h2, ab=ab, act=act, f=f)
    return x2, s, side_out


def _mod_bwd_fn(x, dh, dxa, scale_row):
    return (dxa + dh * scale_row, jnp.sum(dh * x, axis=0, keepdims=True), jnp.sum(dh, axis=0, keepdims=True))


def _layer_bwd(dx2, s, w, rope_tabs, tag, make_side=None):
    g = {}
    q_tabs, k_tabs = rope_tabs
    t = dx2.shape[0]
    ln_full = lambda gt, a, b: [gt, _row(w[a]), _row(w[b])]

    dx1_a, df, dgt2, g['ln2_g'], g['ln2_b'] = _rowcall_vjp(
        _ln_res_fn, [s['x1'], s['f']], ln_full(s['gt2'], 'ln2_g', 'ln2_b'), [dx2], [0, 1], [0, 1, 2],
        tm=256, name=f"ln2_bwd_{tag}", row_dtypes=[F32, BF16])
    dact = _mm(df, w['ffn_w_out'], tb=True, name=f"ffn_out_dx_{tag}", tn=1408)
    g['ffn_w_out'] = _mm(s['act'], df, ta=True, name=f"ffn_out_dw_{tag}", tm=1408)
    def swiglu_bwd_fn(a, b, d):
        _, pull = jax.vjp(_swiglu_fn, a, b)
        return (jnp.concatenate(pull((d,)), axis=1),)

    (dab,) = _rowcall(swiglu_bwd_fn, [(s['ab'], FF_HIDDEN, 0), (s['ab'], FF_HIDDEN, 1), dact], [], [(2 * FF_HIDDEN, BF16)],
                      tm=256, name=f"swiglu_bwd_{tag}")
    dh2 = _mm(dab, w['ffn_w_in'], tb=True, name=f"ffn_in_dx_{tag}", tk=1408)
    g['ffn_w_in'] = _mm(s['h2'], dab, ta=True, name=f"ffn_in_dw_{tag}", tn=1408)
    dx1, dsc2, dsh2 = _rowcall(_mod_bwd_fn, [s['x1'], dh2, dx1_a], [s['sc2']], [(D_MODEL, F32)],
                               [((1, D_MODEL), F32)] * 2, tm=256, name=f"mod2_bwd_{tag}")

    dx_a, dymix, dgt1, g['ln1_g'], g['ln1_b'] = _rowcall_vjp(
        _ln_res_fn, [s['x'], s['ymix']], ln_full(s['gt1'], 'ln1_g', 'ln1_b'), [dx1], [0, 1], [0, 1, 2],
        tm=256, name=f"ln1_bwd_{tag}", row_dtypes=[F32, BF16])
    dmerged = _mm(dymix, w['w_out'], tb=True, name=f"wout_dx_{tag}")
    g['w_out'] = _mm(s['merged'], dymix, ta=True, name=f"wout_dw_{tag}")

    proj = s['proj']
    wbr = w['w_branch'].reshape(-1, D_MODEL)
    gate_rows = [(proj, D_MODEL, b) for b in range(3)]
    dy_s5, dy_mla, dy_sgu, dl0, dl1, dl2, dwbr = _rowcall_vjp(
        _merge_fn, [s['y_s5'], s['y_mla'], s['y_sgu']] + gate_rows, [wbr], [dmerged], [0, 1, 2, 3, 4, 5], [0],
        tm=256, name=f"merge_bwd_{tag}")
    g['w_branch'] = dwbr.reshape(w['w_branch'].shape)

    sgu_rows = [(proj, SGU_WIDTH, P_USGU // SGU_WIDTH), (proj, SGU_WIDTH, P_VSGU // SGU_WIDTH)]
    du_sgu, dv_sgu, dlg, dlb, dwm, dbias = _rowcall_vjp(
        _sgu_fn, sgu_rows, s['sgu_full'], [dy_sgu], [0, 1], [0, 1, 2, 3], tm=SGU_CHUNK, name=f"sgu_bwd_{tag}")
    g['sgu_ln_g'], g['sgu_ln_b'] = dlg.reshape(-1), dlb.reshape(-1)
    g['sgu_wm'], g['sgu_bias'] = dwm, dbias

    do = dy_mla.reshape(t, MLA_HEADS, V_HEAD).transpose(1, 0, 2).astype(BF16)
    (dqh, dkh, dvh), side_out = _attn_bwd(s['qh'], s['kh'], s['vh'], s['o'], s['lse'], do, name=f"attn_bwd_{tag}",
                                          side=make_side(g) if make_side is not None else None)
    dq_r = dqh.transpose(1, 0, 2).reshape(t, -1)
    dkv = jnp.concatenate([dkh[:, :, :QK_NOPE], dvh], axis=2).transpose(1, 0, 2).reshape(t, -1)
    dkpe = jnp.pad(jnp.sum(dkh[:, :, QK_NOPE:], axis=0), ((0, 0), (0, V7X_LANES - QK_ROPE)))
    mla_rows = [(proj, MLA_BLK, P_MLA // MLA_BLK), *q_tabs, *k_tabs]
    mla_full = [_row(w['mla_q_norm']), w['mla_w_q_up'], _row(w['mla_kv_norm']), w['mla_w_kv_up']]
    dmla, dqn, g['mla_w_q_up'], dkvn, g['mla_w_kv_up'] = _rowcall_vjp(
        _mla_pre_fn, mla_rows, mla_full, [dq_r, dkv, dkpe], [0], [0, 1, 2, 3], tm=256, name=f"mla_pre_bwd_{tag}")
    g['mla_q_norm'], g['mla_kv_norm'] = dqn.reshape(-1), dkvn.reshape(-1)

    s5_full = [_row(w['s5_d']), w['s5_w_glu'], _row(w['s5_b_glu'])]
    dylin, du_a, dd, g['s5_w_glu'], dbg = _rowcall_vjp(
        _s5_post_fn, [s['ylin'], (proj, S5_WIDTH, P_S5 // S5_WIDTH)], s5_full, [dy_s5], [0, 1], [0, 1, 2], tm=256,
        name=f"s5_post_bwd_{tag}", row_dtypes=[BF16, F32])
    g['s5_d'], g['s5_b_glu'] = dd.reshape(-1), dbg.reshape(-1)
    du_b, g['s5_wb'], g['s5_wc'], da_part = _s5_bwd(_seg_order(dylin), s['u_seg'], s['hs'], w['s5_wb'], w['s5_wc'],
                                                     w['s5_tab_bwd'], name=f"s5_bwd_{tag}")
    du_b = _time_order(du_b)
    g['s5_a'] = jnp.sum(da_part, axis=0, keepdims=True)

    def dproj_fn(g0, g1, g2, ua, ub, us, vs, ml):
        d = jnp.concatenate([g0, g1, g2, ua + ub, us, vs, ml], axis=1)
        return d, jnp.sum(d, axis=0, keepdims=True)

    dproj, db_in = _rowcall(dproj_fn, [dl0, dl1, dl2, du_a, du_b, du_sgu, dv_sgu, dmla], [], [(IN_PAD, BF16)],
                            [((1, IN_PAD), F32)], tm=256, name=f"dproj_{tag}")
    dh = _mm(dproj, w['w_in_p'], tb=True, name=f"proj_dx_{tag}", tk=1792)
    g['w_in_p'] = _mm(s['h'], dproj, ta=True, name=f"proj_dw_{tag}", tn=896)
    g['b_in'] = _unpermute_in(db_in).reshape(-1)
    dx, dsc1, dsh1 = _rowcall(_mod_bwd_fn, [s['x'], dh, dx_a], [s['sc1']], [(D_MODEL, F32)], [((1, D_MODEL), F32)] * 2,
                              tm=256, name=f"mod1_bwd_{tag}")
    d_ada = jnp.concatenate([dsh1, dsc1, dgt1, dsh2, dsc2, dgt2], axis=0)
    return dx, d_ada, g, side_out


def _loss_fn(y, target):
    err = y - target
    return (err / D_MODEL, 0.5 * jnp.sum(jnp.sum(err * err, axis=1, keepdims=True), axis=0, keepdims=True) / D_MODEL)


def _step(p):
    me = 4 * lax.axis_index("x") + 2 * lax.axis_index("y") + lax.axis_index("c")
    x = p['x'][0]
    t = x.shape[0]
    rope_tabs = _rope_tables(t)

    (c_all,) = _exchange([jnp.broadcast_to(p['c'], (V7X_SUBLANES, D_MODEL))], gather=True, name="gather_c")
    c_all = c_all[:, 0, :]
    (c_act,) = _rowcall(lambda cc: (cc * _sigmoid(cc),), [c_all], [], [(D_MODEL, F32)], tm=N_DEV, name="c_silu")
    ncol = p['w_ada'].shape[2]
    b_ada_loc = lax.dynamic_slice_in_dim(p['b_ada'], me * ncol, ncol, axis=1)
    ada_cols = jnp.concatenate([_mm(c_act, p['w_ada'][l], bias=b_ada_loc[l:l + 1], name=f"ada_{l}") for l in range(DEPTH)])
    (ada_all,) = _exchange([ada_cols], gather=True, name="gather_ada")
    ada_all = ada_all.reshape(N_DEV, DEPTH, N_DEV, ncol)
    ada = lax.dynamic_index_in_dim(ada_all, me, axis=2, keepdims=False)
    ada = ada.transpose(1, 0, 2).reshape(DEPTH, 6, D_MODEL)

    mixer_w, ffn_w = SHARDED[:-2], SHARDED[-2:]

    def shards(l, group):
        return [p[n][l].astype(BF16) for n, _ in group]

    def contribs(g, group):
        return [(_w_in_contrib(g['w_in_p']) if n == 'w_in' else _contrib_from_full(g[n], kind)).astype(BF16)
                for n, kind in group]

    tables, pull_tables = _derived_tables(p, t)

    def mixer_weights(l, gathered):
        w = {n: _full_from_gathered(g, kind) for (n, kind), g in zip(mixer_w[1:], gathered[1:])}
        w['w_in_p'] = _w_in_from_gathered(gathered[0])
        w['b_in_p'] = _row(_permute_in(p['b_in'][l]))
        w.update(tables[l])
        for n in SMALL:
            if n != 'b_ada' and n != 'b_in':
                w[n] = p[n][l]
        return w

    saved, layers = [], []
    gathered = _exchange(shards(0, mixer_w), gather=True, name="gather_w_0", two_level=True)
    for l in range(DEPTH):
        w = mixer_weights(l, gathered)
        side = _Exchange(shards(l, ffn_w) + (shards(l + 1, mixer_w) if l + 1 < DEPTH else []), True, two_level=True)

        def add_ffn(res, w=w):
            for (n, kind), g in zip(ffn_w, res):
                w[n] = _full_from_gathered(g, kind)

        x, s, res = _layer_fwd(x, ada[l], w, rope_tabs, f"l{l}", side, add_ffn)
        gathered = res[len(ffn_w):]
        layers.append(w)
        saved.append(s)
    dy, loss_loc = _rowcall(_loss_fn, [x, p['loss_target'][0]], [], [(D_MODEL, F32)], [((1, 1), F32)], tm=256, name="loss")
    loss = lax.psum(loss_loc[0, 0], ("x", "y", "c"))

    d_ada, grads, landed = [None] * DEPTH, [None] * DEPTH, [[None, None] for _ in range(DEPTH)]
    dx, pending = dy, []
    for l in reversed(range(DEPTH)):
        make_side = lambda g, pending=pending: _Exchange(contribs(g, ffn_w) + pending, False)
        dx, d_ada[l], grads[l], res = _layer_bwd(dx, saved[l], layers[l], rope_tabs, f"l{l}", make_side)
        landed[l][1] = res[:len(ffn_w)]
        if l + 1 < DEPTH:
            landed[l + 1][0] = res[len(ffn_w):]
        pending = contribs(grads[l], mixer_w)
    landed[0][0] = _exchange(pending, gather=False, name="scatter_g_0")
    landed = [list(a) + list(b) for a, b in landed]
    d_ada = jnp.stack(d_ada).reshape(DEPTH, 6 * D_MODEL)

    (d_ada_all,) = _exchange([d_ada], gather=True, name="gather_dada")
    d_ada_cols = lax.dynamic_slice_in_dim(d_ada_all, me * ncol, ncol, axis=2)
    pad_b = ((0, V7X_LANES - N_DEV), (0, 0))
    c_act_p = jnp.pad(c_act, pad_b)
    g_w_ada = jnp.stack([_mm(c_act_p, jnp.pad(d_ada_cols[:, l], pad_b), ta=True, name=f"ada_dw_{l}") for l in range(DEPTH)])

    out = {}
    kinds = ('grad_', 'delta_', 'new_m_', 'new_v_')
    for i, (n, _) in enumerate(SHARDED):
        res = _adamw(_as2d(p[n]), [_as2d(landed[l][i], lead=1) for l in range(DEPTH)], _as2d(p['m_' + n]),
                     _as2d(p['v_' + n]), name=f"adamw_{n}")
        for kind, r in zip(kinds, res):
            out[kind + n] = r.reshape(p[n].shape)

    res = _adamw(_as2d(p['w_ada']), [_as2d(g_w_ada)], _as2d(p['m_w_ada']), _as2d(p['v_w_ada']), name="adamw_ada")
    for kind, r in zip(kinds, res):
        out[kind + 'w_ada'] = r.reshape(p['w_ada'].shape)

    small_g = pull_tables(grads)
    small_g.update({n: jnp.stack([grads[l][n] for l in range(DEPTH)]) for n in SMALL if n not in small_g and n != 'b_ada'})
    small_g['b_ada'] = d_ada
    n_small = sum(int(np.prod(p[n].shape)) for n in SMALL)
    n_pad = -(-n_small // SMALL_PAD) * SMALL_PAD
    flat = jnp.concatenate([small_g[n].reshape(-1) for n in SMALL])
    flat = jnp.pad(flat, (0, n_pad - n_small)).reshape(N_DEV, n_pad // N_DEV // 1024, 1024)
    (landed_small,) = _exchange([flat], gather=False, name="scatter_small")
    (g_all,) = _exchange([_sum_parts(landed_small, name="sum_small")], gather=True, name="gather_small")
    g_all = g_all.reshape(-1)
    off = 0
    for n in SMALL:
        size = int(np.prod(p[n].shape))
        g_n = g_all[off:off + size].reshape(p[n].shape)
        off += size
        res = _adamw(_as2d(p[n]), [_as2d(g_n)], _as2d(p['m_' + n]), _as2d(p['v_' + n]), name=f"adamw_{n}")
        for kind, r in zip(kinds, res):
            out[kind + n] = r.reshape(p[n].shape)

    outs = [loss, dx[None]]
    for kind in ('grad_', 'delta_', 'new_m_', 'new_v_'):
        outs += [out[kind + n] for n in WNAMES]
    return tuple(outs)


def kernel(x, c, w_ada, b_ada, w_in, b_in, s5_lambda_re, s5_lambda_im, s5_log_dt, s5_b_re, s5_b_im, s5_c_re, s5_c_im, s5_d, s5_w_glu, s5_b_glu, mla_q_norm, mla_w_q_up, mla_kv_norm, mla_w_kv_up, sgu_ln_g, sgu_ln_b, sgu_w_s, sgu_b_s, w_branch, w_out, ln1_g, ln1_b, ffn_w_in, ffn_w_out, ln2_g, ln2_b, loss_target, m_w_ada, m_b_ada, m_w_in, m_b_in, m_s5_lambda_re, m_s5_lambda_im, m_s5_log_dt, m_s5_b_re, m_s5_b_im, m_s5_c_re, m_s5_c_im, m_s5_d, m_s5_w_glu, m_s5_b_glu, m_mla_q_norm, m_mla_w_q_up, m_mla_kv_norm, m_mla_w_kv_up, m_sgu_ln_g, m_sgu_ln_b, m_sgu_w_s, m_sgu_b_s, m_w_branch, m_w_out, m_ln1_g, m_ln1_b, m_ffn_w_in, m_ffn_w_out, m_ln2_g, m_ln2_b, v_w_ada, v_b_ada, v_w_in, v_b_in, v_s5_lambda_re, v_s5_lambda_im, v_s5_log_dt, v_s5_b_re, v_s5_b_im, v_s5_c_re, v_s5_c_im, v_s5_d, v_s5_w_glu, v_s5_b_glu, v_mla_q_norm, v_mla_w_q_up, v_mla_kv_norm, v_mla_w_kv_up, v_sgu_ln_g, v_sgu_ln_b, v_sgu_w_s, v_sgu_b_s, v_w_branch, v_w_out, v_ln1_g, v_ln1_b, v_ffn_w_in, v_ffn_w_out, v_ln2_g, v_ln2_b):
    return _step(dict(locals()))
```

```python
import functools
import math

import numpy as np
import jax
import jax.numpy as jnp
from jax import lax
from jax.experimental import pallas as pl
from jax.experimental.pallas import tpu as pltpu

F32 = jnp.float32
BF16 = jnp.bfloat16

N_DEV = 8
D_MODEL = 1024
DEPTH = 4
CHUNK = 64
S5_WIDTH = 512
S5_GROUP = 16
S5_GROUPS = 32
S5_STATE = 64
MLA_HEADS = 8
QK_NOPE = 64
QK_ROPE = 32
V_HEAD = 64
Q_LORA = 384
KV_LORA = 256
ROPE_THETA = 10000.0
SGU_WIDTH = 512
SGU_GROUPS = 4
SGU_CHUNK = 128
FF_HIDDEN = 2816
DEEPNORM_ALPHA = (2 * DEPTH) ** 0.25
LN_EPS = 1e-5
RMS_EPS = 1e-6
NEG_INF = -1e30
ADAM_LR = 0.001
ADAM_B1 = 0.9
ADAM_B2 = 0.999
ADAM_EPS = 1e-08
ADAM_WD = 0.01
ADAM_STEP = 10

IN_WIDTH = 5280
IN_PAD = 5376
_O_S5, _O_CQ, _O_CKV, _O_KPE, _O_USGU, _O_VSGU, _O_GATE = 0, 512, 896, 1152, 1184, 1696, 2208
_IN_SEGMENTS = ((_O_GATE, IN_WIDTH), (_O_S5, _O_CQ), (_O_USGU, _O_VSGU), (_O_VSGU, _O_GATE), (_O_CQ, _O_USGU))
P_GATE, P_S5, P_USGU, P_VSGU, P_MLA = 0, 3072, 3584, 4096, 4608
MLA_BLK = 768

V7X_LANES = 128
V7X_SUBLANES = 8
VMEM_LIMIT = 56 * 1024 * 1024
ATT_BLOCK = 512
SCAN_LANES = 128
S5_CH = S5_GROUPS * S5_STATE
S5_BD = 4

WNAMES = ['w_ada', 'b_ada', 'w_in', 'b_in', 's5_lambda_re', 's5_lambda_im', 's5_log_dt', 's5_b_re', 's5_b_im',
          's5_c_re', 's5_c_im', 's5_d', 's5_w_glu', 's5_b_glu', 'mla_q_norm', 'mla_w_q_up', 'mla_kv_norm',
          'mla_w_kv_up', 'sgu_ln_g', 'sgu_ln_b', 'sgu_w_s', 'sgu_b_s', 'w_branch', 'w_out', 'ln1_g', 'ln1_b',
          'ffn_w_in', 'ffn_w_out', 'ln2_g', 'ln2_b']
SHARDED = (('w_in', 'col'), ('s5_w_glu', 'row'), ('mla_w_q_up', 'col'), ('mla_w_kv_up', 'col'),
           ('w_branch', 'col3'), ('w_out', 'row'), ('ffn_w_in', 'col'), ('ffn_w_out', 'row'))
SMALL = [n for n in WNAMES if n != 'w_ada' and n not in dict(SHARDED)]
SMALL_PAD = N_DEV * V7X_SUBLANES * 1024


def _cparams(*sem):
    return pltpu.CompilerParams(dimension_semantics=sem, vmem_limit_bytes=VMEM_LIMIT)


def _pick(n, target):
    if n <= target:
        return n
    best = None
    for d in range(V7X_LANES, target + 1, V7X_LANES):
        if n % d == 0:
            best = d
    assert best is not None, (n, target)
    return best


def _pick_rows(n, target):
    if n <= target:
        return n
    best = None
    for d in range(V7X_SUBLANES, target + 1, V7X_SUBLANES):
        if n % d == 0:
            best = d
    assert best is not None, (n, target)
    return best


@jax.custom_vjp
def _bdot(a, b):
    return jnp.dot(a.astype(BF16), b.astype(BF16), preferred_element_type=F32)


def _bdot_fwd(a, b):
    return _bdot(a, b), (a, b)


def _bdot_bwd(res, g):
    a, b = res
    gb = g.astype(BF16)
    da = lax.dot_general(gb, b.astype(BF16), (((1,), (1,)), ((), ())), preferred_element_type=F32)
    db = lax.dot_general(a.astype(BF16), gb, (((0,), (0,)), ((), ())), preferred_element_type=F32)
    return da.astype(a.dtype), db.astype(b.dtype)


_bdot.defvjp(_bdot_fwd, _bdot_bwd)


@functools.partial(jax.custom_vjp, nondiff_argnums=(1,))
def _lane_roll(x, shift):
    return pltpu.roll(x, shift % x.shape[1], 1)


def _lane_roll_fwd(x, shift):
    return _lane_roll(x, shift), None


def _lane_roll_bwd(shift, _, g):
    return (_lane_roll(g, -shift),)


_lane_roll.defvjp(_lane_roll_fwd, _lane_roll_bwd)


def _sigmoid(x):
    return 1.0 / (1.0 + jnp.exp(-x))


def _gelu(x):
    return 0.5 * x * (1.0 + jnp.tanh(math.sqrt(2.0 / math.pi) * (x + 0.044715 * (x * x * x))))


def _layer_norm(x, g, b):
    mu = jnp.mean(x, axis=-1, keepdims=True)
    var = jnp.mean(jnp.square(x - mu), axis=-1, keepdims=True)
    return (x - mu) * lax.rsqrt(var + LN_EPS) * g + b


def _rms_norm(x, g):
    return x * lax.rsqrt(jnp.mean(x * x, axis=-1, keepdims=True) + RMS_EPS) * g


def _rope(x, c, s1, s2):
    half = QK_ROPE // 2
    return x * c + _lane_roll(x, -half) * s1 + _lane_roll(x, half) * s2


def _modulate_fn(x, scale_row, shift_row):
    return (x * scale_row + shift_row,)


def _ln_res_fn(x, y, gate_row, g, b):
    return (_layer_norm(DEEPNORM_ALPHA * x + gate_row * y, g, b),)


def _s5_post_fn(ylin, u, d, w_glu, b_glu):
    z = _gelu(ylin + d * u)
    return (z * _sigmoid(_bdot(z, w_glu) + b_glu),)


def _mla_pre_fn(blk, cq_t, sq1, sq2, ck_t, sk1, sk2, q_norm, w_q, kv_norm, w_k, w_v):
    cq, ckv, kpe = blk[:, :Q_LORA], blk[:, Q_LORA:Q_LORA + KV_LORA], blk[:, Q_LORA + KV_LORA:]
    q = _rope(_bdot(_rms_norm(cq, q_norm), w_q), cq_t, sq1, sq2) * Q_SCALE
    ckv_n = _rms_norm(ckv, kv_norm)
    kpe_r = _lane_roll(_rope(kpe, ck_t, sk1, sk2), QK_NOPE)
    k = _bdot(ckv_n, w_k) + jnp.concatenate([kpe_r] * MLA_HEADS, axis=1)
    return q, k, _bdot(ckv_n, w_v)


def _pad_heads(w, width):
    w3 = w.reshape(w.shape[0], MLA_HEADS, width)
    return jnp.pad(w3, ((0, 0), (0, 0), (0, V7X_LANES - width))).reshape(w.shape[0], MLA_HEADS * V7X_LANES)


def _mla_weights(w_q_up, w_kv_up):
    kv3 = w_kv_up.reshape(w_kv_up.shape[0], MLA_HEADS, QK_NOPE + V_HEAD)
    w_k = _pad_heads(kv3[:, :, :QK_NOPE].reshape(w_kv_up.shape[0], -1), QK_NOPE)
    return _pad_heads(w_q_up, QK_NOPE + QK_ROPE), w_k, kv3[:, :, QK_NOPE:].reshape(w_kv_up.shape[0], -1)


def _mla_weight_grads(dw_q, dw_k, dw_v):
    unpad = lambda a, width: a.reshape(a.shape[0], MLA_HEADS, V7X_LANES)[:, :, :width]
    dkv = jnp.concatenate([unpad(dw_k, QK_NOPE), dw_v.reshape(dw_v.shape[0], MLA_HEADS, V_HEAD)], axis=2)
    return unpad(dw_q, QK_NOPE + QK_ROPE).reshape(dw_q.shape[0], -1), dkv.reshape(dw_k.shape[0], -1)


def _sgu_fn(u, v, g, b, wm, bias):
    vn = _layer_norm(_gelu(v), g, b)
    w = SGU_CHUNK
    parts = [_bdot(wm[k * w:(k + 1) * w, :], vn[:, k * w:(k + 1) * w]) for k in range(SGU_GROUPS)]
    return (_gelu(u) * (jnp.concatenate(parts, axis=1) + bias),)


def _merge_fn(y0, y1, y2, l0, l1, l2, wb):
    n = S5_WIDTH
    return (_sigmoid(l0) * _bdot(y0, wb[:n]) + _sigmoid(l1) * _bdot(y1, wb[n:2 * n])
            + _sigmoid(l2) * _bdot(y2, wb[2 * n:]),)


def _swiglu_fn(a, b):
    return (a * _sigmoid(a) * b,)


def _rowcall(fn, rows, fulls, out_rows, out_reds=(), *, tm, name):
    rows = [r if isinstance(r, tuple) else (r, r.shape[1], 0) for r in rows]
    t = rows[0][0].shape[0]
    tm = _pick_rows(t, tm)
    n_in, n_or, n_red = len(rows) + len(fulls), len(out_rows), len(out_reds)

    def body(*refs):
        vals = fn(*[r[...] for r in refs[:n_in]])
        assert len(vals) == n_or + n_red, (name, len(vals))
        for ref, v in zip(refs[n_in:n_in + n_or], vals[:n_or]):
            ref[...] = v.astype(ref.dtype)
        if n_red:
            red_refs = refs[n_in + n_or:]

            @pl.when(pl.program_id(0) == 0)
            def _():
                for ref in red_refs:
                    ref[...] = jnp.zeros_like(ref)

            for ref, v in zip(red_refs, vals[n_or:]):
                ref[...] += v.astype(ref.dtype)

    in_specs = [pl.BlockSpec((tm, w), functools.partial(lambda i, blk: (i, blk), blk=blk)) for _, w, blk in rows]
    in_specs += [pl.BlockSpec(f.shape, lambda i: (0, 0)) for f in fulls]
    out_specs = [pl.BlockSpec((tm, c), lambda i: (i, 0)) for c, _ in out_rows]
    out_specs += [pl.BlockSpec(s, lambda i: (0, 0)) for s, _ in out_reds]
    out_shape = [jax.ShapeDtypeStruct((t, c), dt) for c, dt in out_rows]
    out_shape += [jax.ShapeDtypeStruct(s, dt) for s, dt in out_reds]
    return pl.pallas_call(
        body, name=name, grid=(t // tm,), in_specs=in_specs, out_specs=out_specs, out_shape=out_shape,
        compiler_params=_cparams("arbitrary" if n_red else "parallel"),
    )(*[r[0] for r in rows], *fulls)


def _rowcall_vjp(fn, rows, fulls, cots, diff_rows, diff_fulls, *, tm, name, row_dtypes=None):
    rows_n = [r if isinstance(r, tuple) else (r, r.shape[1], 0) for r in rows]
    n_r, n_c = len(rows), len(cots)
    row_dtypes = row_dtypes or [F32] * len(diff_rows)

    def fn2(*vals):
        r = [v.astype(F32) for v in vals[:n_r]]
        ct = vals[n_r:n_r + n_c]
        f = [v.astype(F32) for v in vals[n_r + n_c:]]

        def g(*dargs):
            rr, ff = list(r), list(f)
            for k, idx in enumerate(diff_rows):
                rr[idx] = dargs[k]
            for k, idx in enumerate(diff_fulls):
                ff[idx] = dargs[len(diff_rows) + k]
            return fn(*rr, *ff)

        prim = [r[i] for i in diff_rows] + [f[i] for i in diff_fulls]
        outs, pull = jax.vjp(g, *prim)
        return pull(tuple(c.astype(o.dtype) for c, o in zip(ct, outs)))

    out_rows = [(rows_n[i][1], dt) for i, dt in zip(diff_rows, row_dtypes)]
    out_reds = [(fulls[i].shape, F32) for i in diff_fulls]
    return _rowcall(fn2, list(rows) + list(cots), fulls, out_rows, out_reds, tm=tm, name=name)


def _mm(a, b, *, ta=False, tb=False, bias=None, out_dtype=F32, name, tm=1024, tn=1024, tk=1024):
    (k_a, m) = a.shape if ta else a.shape[::-1]
    (n, k_b) = b.shape if tb else b.shape[::-1]
    assert k_a == k_b, (name, a.shape, b.shape)
    tm, tn, tk = _pick(m, tm) if m % V7X_LANES == 0 else m, _pick(n, tn), _pick(k_a, tk) if k_a % V7X_LANES == 0 else k_a
    nk = k_a // tk
    a_spec = pl.BlockSpec((tk, tm), lambda i, j, k: (k, i)) if ta else pl.BlockSpec((tm, tk), lambda i, j, k: (i, k))
    b_spec = pl.BlockSpec((tn, tk), lambda i, j, k: (j, k)) if tb else pl.BlockSpec((tk, tn), lambda i, j, k: (k, j))
    dims = (((0,) if ta else (1,), (1,) if tb else (0,)), ((), ()))
    has_bias = bias is not None

    def body(*refs):
        a_ref, b_ref = refs[0], refs[1]
        part = lax.dot_general(a_ref[...].astype(BF16), b_ref[...].astype(BF16), dims, preferred_element_type=F32)
        if nk == 1:
            o_ref = refs[-1]
            o_ref[...] = (part + refs[2][...] if has_bias else part).astype(o_ref.dtype)
            return
        o_ref, acc_ref = refs[-2], refs[-1]
        k = pl.program_id(2)

        @pl.when(k == 0)
        def _():
            acc_ref[...] = part

        @pl.when(k > 0)
        def _():
            acc_ref[...] += part

        @pl.when(k == nk - 1)
        def _():
            r = acc_ref[...]
            if has_bias:
                r = r + refs[2][...]
            o_ref[...] = r.astype(o_ref.dtype)

    in_specs = [a_spec, b_spec] + ([pl.BlockSpec((1, tn), lambda i, j, k: (0, j))] if has_bias else [])
    return pl.pallas_call(
        body, name=name, grid=(m // tm, n // tn, nk), in_specs=in_specs,
        out_specs=pl.BlockSpec((tm, tn), lambda i, j, k: (i, j)),
        out_shape=jax.ShapeDtypeStruct((m, n), out_dtype),
        scratch_shapes=[pltpu.VMEM((tm, tn), F32)] if nk > 1 else [],
        compiler_params=_cparams("parallel", "parallel", "arbitrary"),
    )(a, b, *([bias] if has_bias else []))


def _seg_order(a):
    t, c = a.shape
    return a.reshape(V7X_SUBLANES, t // V7X_SUBLANES, c).transpose(1, 0, 2).reshape(t, c)


def _time_order(a):
    t, c = a.shape
    return a.reshape(t // V7X_SUBLANES, V7X_SUBLANES, c).transpose(1, 0, 2).reshape(t, c)


def _scan_in_place(o_ref, tab_ref, reverse, h_ref=None, da_ref=None):
    ntile = o_ref.shape[0] // V7X_SUBLANES
    with_da = h_ref is not None
    ln = SCAN_LANES
    rows8 = V7X_SUBLANES
    subs = range(o_ref.shape[1] // (2 * ln))
    col = lambda s, part: slice((2 * s + part) * ln, (2 * s + part + 1) * ln)
    cmul = lambda p, q: (p[0] * q[0] - p[1] * q[1], p[0] * q[1] + p[1] * q[0])
    cadd = lambda p, q: (p[0] + q[0], p[1] + q[1])

    def run():
        row = lax.broadcasted_iota(jnp.int32, (rows8, ln), 0)
        bcast = lambda k, s: tuple(jnp.broadcast_to(tab_ref[k:k + 1, col(s, part)], (rows8, ln)) for part in (0, 1))
        a = [bcast(0, s) for s in subs]
        zero = [(jnp.zeros((rows8, ln), F32), jnp.zeros((rows8, ln), F32)) for _ in subs]

        def tile(ref, i):
            r0 = pl.multiple_of(i * rows8, rows8)
            return [(ref[pl.ds(r0, rows8), col(s, 0)], ref[pl.ds(r0, rows8), col(s, 1)]) for s in subs]

        def put(i, ys):
            r0 = pl.multiple_of(i * rows8, rows8)
            for s, (yr, yi) in enumerate(ys):
                o_ref[pl.ds(r0, rows8), col(s, 0)] = yr
                o_ref[pl.ds(r0, rows8), col(s, 1)] = yi

        def shifted(z, step, up):
            shift, keep = (rows8 - step, row < rows8 - step) if up else (step, row >= step)
            return tuple(jnp.where(keep, pltpu.roll(v, shift, 0), 0.0) for v in z)

        def pass1(n, carry):
            i = (ntile - 1 - n) if reverse else n
            ys = [cadd(x, cmul(a[s], carry[s])) for s, x in enumerate(tile(o_ref, i))]
            put(i, ys)
            return ys

        ends = lax.fori_loop(0, ntile, pass1, zero, unroll=4)
        incoming = []
        for s in subs:
            f = ends[s]
            for k, step in zip((1, 2, 3), (1, 2, 4)):
                f = cadd(f, cmul(bcast(k, s), shifted(f, step, reverse)))
            incoming.append(shifted(f, 1, reverse))
        if with_da:
            h0 = [shifted(h, 1, False) for h in tile(h_ref, ntile - 1)]

        def pass2(n, carry):
            i = (ntile - 1 - n) if reverse else n
            power = carry[0]
            ys = [cadd(y, cmul(power[s], incoming[s])) for s, y in enumerate(tile(o_ref, i))]
            put(i, ys)
            new_power = [cmul(power[s], a[s]) for s in subs]
            if not with_da:
                return (new_power,)
            prev = tile(h_ref, jnp.maximum(i - 1, 0))
            acc = []
            for s in subs:
                hpr, hpi = jnp.where(i > 0, prev[s][0], h0[s][0]), jnp.where(i > 0, prev[s][1], h0[s][1])
                (yr, yi), (sr, si) = ys[s], carry[1][s]
                acc.append((sr + yr * hpr + yi * hpi, si + yi * hpr - yr * hpi))
            return (new_power, acc)

        out = lax.fori_loop(0, ntile, pass2, (list(a),) + ((zero,) if with_da else ()), unroll=4)
        if with_da:
            for s in subs:
                da_ref[:, col(s, 0)] = out[1][s][0]
                da_ref[:, col(s, 1)] = out[1][s][1]

    run()


SCAN_BLOCK = 4 * SCAN_LANES
_S5_SUB = 2 * S5_CH // S5_BD // SCAN_BLOCK


def _s5_specs(t):
    ku, blk = S5_WIDTH // S5_BD, SCAN_BLOCK
    return dict(
        u=pl.BlockSpec((t, ku), lambda j: (0, j // _S5_SUB)),
        hs=pl.BlockSpec((t, blk), lambda j: (0, j)),
        wb=pl.BlockSpec((None, ku, blk), lambda j: (j // _S5_SUB, 0, j % _S5_SUB)),
        wc=pl.BlockSpec((None, blk, ku), lambda j: (j // _S5_SUB, j % _S5_SUB, 0)),
        tab=pl.BlockSpec((V7X_SUBLANES, blk), lambda j: (0, j)))


def _s5_fwd(u, wb, wc, tab, *, name):
    t = u.shape[0]
    sp = _s5_specs(t)

    def body(u_ref, wb_ref, wc_ref, tab_ref, hs_ref, y_ref):
        hs_ref[...] = jnp.dot(u_ref[...], wb_ref[...], preferred_element_type=F32)
        _scan_in_place(hs_ref, tab_ref, False)
        y = jnp.dot(hs_ref[...].astype(BF16), wc_ref[...], preferred_element_type=F32)
        first = pl.program_id(0) % _S5_SUB == 0

        @pl.when(first)
        def _():
            y_ref[...] = y

        @pl.when(jnp.logical_not(first))
        def _():
            y_ref[...] += y

    return pl.pallas_call(
        body, name=name, grid=(2 * S5_CH // SCAN_BLOCK,),
        in_specs=[sp['u'], sp['wb'], sp['wc'], sp['tab']], out_specs=[sp['hs'], sp['u']],
        out_shape=[jax.ShapeDtypeStruct((t, 2 * S5_CH), F32), jax.ShapeDtypeStruct((t, S5_WIDTH), F32)],
        compiler_params=_cparams("arbitrary"),
    )(u, wb, wc, tab)


def _s5_bwd(dy, u, hs, wb, wc, tab, *, name):
    t = u.shape[0]
    sp = _s5_specs(t)
    nt, tn = (((1,), (1,)), ((), ())), (((0,), (0,)), ((), ()))

    def body(dy_ref, u_ref, hs_ref, wb_ref, wc_ref, tab_ref, du_ref, dwb_ref, dwc_ref, da_ref, g_ref):
        g_ref[...] = lax.dot_general(dy_ref[...], wc_ref[...], nt, preferred_element_type=F32)
        dwc_ref[...] = lax.dot_general(hs_ref[...].astype(BF16), dy_ref[...], tn, preferred_element_type=F32)
        _scan_in_place(g_ref, tab_ref, True, hs_ref, da_ref)
        gb = g_ref[...].astype(BF16)
        dwb_ref[...] = lax.dot_general(u_ref[...], gb, tn, preferred_element_type=F32)
        du = lax.dot_general(gb, wb_ref[...], nt, preferred_element_type=F32)
        first = pl.program_id(0) % _S5_SUB == 0

        @pl.when(first)
        def _():
            du_ref[...] = du

        @pl.when(jnp.logical_not(first))
        def _():
            du_ref[...] += du

    return pl.pallas_call(
        body, name=name, grid=(2 * S5_CH // SCAN_BLOCK,),
        in_specs=[sp['u'], sp['u'], sp['hs'], sp['wb'], sp['wc'], sp['tab']],
        out_specs=[sp['u'], sp['wb'], sp['wc'], sp['tab']],
        out_shape=[jax.ShapeDtypeStruct((t, S5_WIDTH), F32), jax.ShapeDtypeStruct(wb.shape, F32),
                   jax.ShapeDtypeStruct(wc.shape, F32), jax.ShapeDtypeStruct((V7X_SUBLANES, 2 * S5_CH), F32)],
        scratch_shapes=[pltpu.VMEM((t, SCAN_BLOCK), F32)],
        compiler_params=_cparams("arbitrary"),
    )(dy, u, hs, wb, wc, tab)


ATT_SCALE = (QK_NOPE + QK_ROPE) ** -0.5
Q_SCALE = ATT_SCALE * math.log2(math.e)
LN2 = math.log(2.0)


def _att_mask(qi, kj, tb):
    qc = (qi * tb + lax.broadcasted_iota(jnp.int32, (tb, tb), 0)) // CHUNK
    kc = (kj * tb + lax.broadcasted_iota(jnp.int32, (tb, tb), 1)) // CHUNK
    return kc <= qc


def _with_side(side, n_main_in, n_main_out, refs, step, nsteps, compute):
    if side is None:
        compute(refs)
        return
    n = side.n
    main = refs[:n_main_in] + refs[n_main_in + n:n_main_in + n + n_main_out] + refs[n_main_in + 2 * n + n_main_out + 3:]
    x_refs = refs[n_main_in:n_main_in + n]
    y_refs = refs[n_main_in + n + n_main_out:n_main_in + 2 * n + n_main_out]
    side_refs = (x_refs, y_refs) + tuple(refs[n_main_in + 2 * n + n_main_out:n_main_in + 2 * n + n_main_out + 3])
    pl.when(step == 0)(functools.partial(side.start, *side_refs))
    compute(main)
    pl.when(step == (7 * nsteps) // 8)(functools.partial(side.relay, *side_refs))
    pl.when(step == nsteps - 1)(functools.partial(side.finish, *side_refs))


def _side_call(body, side, *, name, grid, in_specs, out_specs, out_shape, scratch, args, sem):
    n_out = len(out_shape)
    if side is not None:
        in_specs, args = in_specs + side.specs, list(args) + side.xs
        out_specs, out_shape = out_specs + side.specs, out_shape + side.out_shape
        scratch = side.scratch + scratch
        params = pltpu.CompilerParams(dimension_semantics=("arbitrary",) * len(grid), vmem_limit_bytes=VMEM_LIMIT,
                                      has_side_effects=True)
    else:
        params = _cparams(*sem)
    res = pl.pallas_call(body, name=name, grid=grid, in_specs=in_specs, out_specs=out_specs, out_shape=out_shape,
                         scratch_shapes=scratch, compiler_params=params)(*args)
    return res[:n_out], res[n_out:]


HEAD_PAIRS = MLA_HEADS // 2


def _attn_fwd(q, k, v, *, name, side=None):
    t = q.shape[0]
    hw = V7X_LANES
    tb = min(ATT_BLOCK, t)
    nblk = t // tb
    nt = (((1,), (1,)), ((), ()))

    def compute(refs):
        q_ref, k_ref, v_ref, o_ref, lse_ref = refs
        i = pl.program_id(1)
        qs = [q_ref[:, a * hw:(a + 1) * hw] for a in range(2)]

        def kv_step(j, carry, masked):
            r0 = pl.multiple_of(j * tb, tb)
            vb = v_ref[pl.ds(r0, tb), :]
            out = []
            for a in range(2):
                m, l, acc = carry[a]
                s = lax.dot_general(qs[a], k_ref[pl.ds(r0, tb), a * hw:(a + 1) * hw], nt, preferred_element_type=F32)
                if masked:
                    s = jnp.where(_att_mask(i, j, tb), s, NEG_INF)
                m_new = jnp.maximum(m, jnp.max(s, axis=1, keepdims=True))
                alpha = jnp.exp2(m - m_new)
                p = jnp.exp2(s - m_new)
                l = alpha * l + jnp.sum(p, axis=1, keepdims=True)
                out.append((m_new, l, alpha * acc + jnp.dot(p.astype(BF16), vb, preferred_element_type=F32)))
            return out

        init = [(jnp.full((tb, 1), NEG_INF, F32), jnp.zeros((tb, 1), F32), jnp.zeros((tb, hw), F32)) for _ in range(2)]
        carry = lax.fori_loop(0, i, functools.partial(kv_step, masked=False), init)
        (m0, l0, acc0), (m1, l1, acc1) = kv_step(i, carry, True)
        first = lax.broadcasted_iota(jnp.int32, (tb, hw), 1) < V_HEAD
        o_ref[...] = jnp.where(first, acc0 / l0, acc1 / l1)
        lse_ref[0] = m0 + jnp.log2(l0)
        lse_ref[1] = m1 + jnp.log2(l1)

    def body(*refs):
        _with_side(side, 3, 2, refs, pl.program_id(0) * nblk + pl.program_id(1), HEAD_PAIRS * nblk, compute)

    return _side_call(
        body, side, name=name, grid=(HEAD_PAIRS, nblk),
        in_specs=[pl.BlockSpec((tb, 2 * hw), lambda hp, i: (i, hp)), pl.BlockSpec((t, 2 * hw), lambda hp, i: (0, hp)),
                  pl.BlockSpec((t, hw), lambda hp, i: (0, hp))],
        out_specs=[pl.BlockSpec((tb, hw), lambda hp, i: (i, hp)), pl.BlockSpec((2, tb, 1), lambda hp, i: (hp, i, 0))],
        out_shape=[jax.ShapeDtypeStruct((t, MLA_HEADS * V_HEAD), F32), jax.ShapeDtypeStruct((MLA_HEADS, t, 1), F32)],
        scratch=[], args=[q, k, v], sem=("parallel", "parallel"))


def _attn_bwd(q, k, v, o, lse, do, *, name, side=None):
    t = q.shape[0]
    hw = V7X_LANES
    tb = min(ATT_BLOCK, t)
    nblk = t // tb
    nt = (((1,), (1,)), ((), ()))
    tn = (((0,), (0,)), ((), ()))

    def compute(refs):
        q_ref, k_ref, v_ref, o_ref, lse_ref, do_ref, dq_ref, dk_ref, dv_ref, delta_ref = refs
        j = pl.program_id(1)
        first = lax.broadcasted_iota(jnp.int32, (tb, hw), 1) < V_HEAD
        mine = [first, jnp.logical_not(first)]

        @pl.when(j == 0)
        def _():
            dq_ref[...] = jnp.zeros_like(dq_ref)

            def dstep(i, c):
                r0 = pl.multiple_of(i * tb, tb)
                prod = do_ref[pl.ds(r0, tb), :] * o_ref[pl.ds(r0, tb), :]
                for a in range(2):
                    delta_ref[a, pl.ds(r0, tb), :] = jnp.sum(jnp.where(mine[a], prod, 0.0), axis=1, keepdims=True)
                return c

            lax.fori_loop(0, nblk, dstep, 0)

        kb, vb = k_ref[...], v_ref[...]

        def q_step(i, carry, masked):
            dks, dv = carry
            r0 = pl.multiple_of(i * tb, tb)
            dob = do_ref[pl.ds(r0, tb), :].astype(BF16)
            new_dks = []
            for a in range(2):
                qa, ka = q_ref[pl.ds(r0, tb), a * hw:(a + 1) * hw], kb[:, a * hw:(a + 1) * hw]
                s = lax.dot_general(qa, ka, nt, preferred_element_type=F32)
                if masked:
                    s = jnp.where(_att_mask(i, j, tb), s, NEG_INF)
                p = jnp.exp2(s - lse_ref[a, pl.ds(r0, tb), :])
                doa = jnp.where(mine[a], dob, jnp.zeros_like(dob))
                dv = dv + lax.dot_general(p.astype(BF16), doa, tn, preferred_element_type=F32)
                dp = lax.dot_general(doa, vb, nt, preferred_element_type=F32)
                ds = (p * (dp - delta_ref[a, pl.ds(r0, tb), :]) * LN2).astype(BF16)
                new_dks.append(dks[a] + lax.dot_general(ds, qa, tn, preferred_element_type=F32))
                dq_ref[pl.ds(r0, tb), a * hw:(a + 1) * hw] += jnp.dot(ds, ka, preferred_element_type=F32)
            return new_dks, dv

        zero = jnp.zeros((tb, hw), F32)
        carry = q_step(j, ([zero, zero], zero), True)
        dks, dv = lax.fori_loop(j + 1, nblk, functools.partial(q_step, masked=False), carry)
        for a in range(2):
            dk_ref[:, a * hw:(a + 1) * hw] = dks[a]
        dv_ref[...] = dv

    def body(*refs):
        _with_side(side, 6, 3, refs, pl.program_id(0) * nblk + pl.program_id(1), HEAD_PAIRS * nblk, compute)

    whole = lambda w: pl.BlockSpec((t, w), lambda hp, j: (0, hp))
    blockj = lambda w: pl.BlockSpec((tb, w), lambda hp, j: (j, hp))
    return _side_call(
        body, side, name=name, grid=(HEAD_PAIRS, nblk),
        in_specs=[whole(2 * hw), blockj(2 * hw), blockj(hw), whole(hw), pl.BlockSpec((2, t, 1), lambda hp, j: (hp, 0, 0)),
                  whole(hw)],
        out_specs=[whole(2 * hw), blockj(2 * hw), blockj(hw)],
        out_shape=[jax.ShapeDtypeStruct(q.shape, F32), jax.ShapeDtypeStruct(k.shape, F32), jax.ShapeDtypeStruct(v.shape, F32)],
        scratch=[pltpu.VMEM((2, t, 1), F32)], args=[q, k, v, o, lse, do], sem=("parallel", "arbitrary"))


class _Exchange:
    def __init__(self, xs, gather, two_level=False):
        assert gather or not two_level
        self.xs, self.gather, self.n, self.two_level = list(xs), gather, len(xs), two_level
        shapes = [tuple(x.shape) if gather else tuple(x.shape[1:]) for x in xs]
        self.out_shape = [jax.ShapeDtypeStruct((N_DEV,) + shp, x.dtype) for shp, x in zip(shapes, xs)]
        self.specs = [pl.BlockSpec(memory_space=pl.ANY)] * self.n
        self.scratch = [pltpu.SemaphoreType.DMA((self.n, N_DEV - 1)), pltpu.SemaphoreType.DMA((self.n, N_DEV - 1)),
                        pltpu.SemaphoreType.DMA((self.n,))]

    def copies(self, x_refs, y_refs, send_sems, recv_sems, local_sems):
        mx, my, mc = lax.axis_index("x"), lax.axis_index("y"), lax.axis_index("c")
        me = 4 * mx + 2 * my + mc
        out = [pltpu.make_async_copy(x_refs[i] if self.gather else x_refs[i].at[me], y_refs[i].at[me], local_sems.at[i])
               for i in range(self.n)]
        for k in range(1, N_DEV):
            px = 1 - mx if k & 4 else mx
            py = 1 - my if k & 2 else my
            pc = 1 - mc if k & 1 else mc
            for i in range(self.n):
                out.append(pltpu.make_async_remote_copy(
                    src_ref=x_refs[i] if self.gather else x_refs[i].at[4 * px + 2 * py + pc], dst_ref=y_refs[i].at[me],
                    send_sem=send_sems.at[i, k - 1], recv_sem=recv_sems.at[i, k - 1],
                    device_id=(px, py, pc), device_id_type=pl.DeviceIdType.MESH))
        return out


    def _two_level(self, x_refs, y_refs, send_sems, recv_sems, local_sems):
        mx, my, mc = lax.axis_index("x"), lax.axis_index("y"), lax.axis_index("c")
        sib = (mx, my, 1 - mc)
        chips = [(1 - mx, my), (mx, 1 - my), (1 - mx, 1 - my)]
        idx = lambda px, py, pc: 4 * px + 2 * py + pc
        me = idx(mx, my, mc)

        def rc(i, k, src, block, to):
            return pltpu.make_async_remote_copy(
                src_ref=src, dst_ref=y_refs[i].at[block], send_sem=send_sems.at[i, k], recv_sem=recv_sems.at[i, k],
                device_id=to, device_id_type=pl.DeviceIdType.MESH)

        rng = range(self.n)
        over_ici = [(j, chip, i) for j, chip in enumerate(chips) for i in rng]
        return dict(
            local=lambda: [pltpu.make_async_copy(x_refs[i], y_refs[i].at[me], local_sems.at[i]) for i in rng],
            own=lambda: [rc(i, 0, x_refs[i], me, sib) for i in rng]
            + [rc(i, 1 + j, x_refs[i], me, (*chip, mc)) for j, chip, i in over_ici],
            relay=lambda: [rc(i, 4 + j, y_refs[i].at[idx(*chip, mc)], idx(*chip, mc), sib) for j, chip, i in over_ici],
            landed=lambda: [rc(i, 1 + j, x_refs[i], idx(*chip, mc), sib) for j, chip, i in over_ici],
            last=lambda: [rc(i, 0, x_refs[i], idx(*sib), sib) for i in rng]
            + [rc(i, 4 + j, x_refs[i], idx(*chip, 1 - mc), sib) for j, chip, i in over_ici])

    def start(self, *refs):
        if not self.two_level:
            for cp in self.copies(*refs):
                cp.start()
            return
        plan = self._two_level(*refs)
        for cp in plan['local']() + plan['own']():
            cp.start()

    def relay(self, *refs):
        if not self.two_level:
            return
        plan = self._two_level(*refs)
        for arrived, cp in zip(plan['landed'](), plan['relay']()):
            arrived.wait_recv()
            cp.start()

    def finish(self, *refs):
        if not self.two_level:
            for cp in self.copies(*refs):
                cp.wait()
            return
        plan = self._two_level(*refs)
        for cp in plan['last']():
            cp.wait_recv()
        for cp in plan['own']() + plan['relay']():
            cp.wait_send()
        for cp in plan['local']():
            cp.wait()


def _exchange(xs, *, gather, name, two_level=False):
    ex = _Exchange(xs, gather, two_level)
    n = ex.n

    def body(*refs):
        refs = (refs[:n], refs[n:2 * n]) + tuple(refs[2 * n:])
        ex.start(*refs)
        ex.relay(*refs)
        ex.finish(*refs)

    return pl.pallas_call(
        body, name=name, out_shape=ex.out_shape, in_specs=ex.specs, out_specs=ex.specs, scratch_shapes=ex.scratch,
        compiler_params=pltpu.CompilerParams(has_side_effects=True),
    )(*ex.xs)


def _adamw(w, gs, m, v, *, name, tm=256):
    nl = len(gs)
    parts = gs[0].ndim == 3
    c = w.shape[1]
    r = w.shape[0] // nl
    tm = _pick_rows(r, tm)
    nrow = r // tm

    def body(*refs):
        w_ref, g_refs, (m_ref, v_ref, go_ref, d_ref, mo_ref, vo_ref) = refs[0], refs[1:1 + nl], refs[1 + nl:]

        def update(g_ref):
            if parts:
                gv = g_ref[0].astype(F32)
                for k in range(1, N_DEV):
                    gv = gv + g_ref[k].astype(F32)
            else:
                gv = g_ref[...]
            mn = ADAM_B1 * m_ref[...] + (1.0 - ADAM_B1) * gv
            vn = ADAM_B2 * v_ref[...] + (1.0 - ADAM_B2) * jnp.square(gv)
            m_hat = mn / (1.0 - ADAM_B1 ** ADAM_STEP)
            v_hat = vn / (1.0 - ADAM_B2 ** ADAM_STEP)
            go_ref[...] = gv
            d_ref[...] = -ADAM_LR * (m_hat / (jnp.sqrt(v_hat) + ADAM_EPS) + ADAM_WD * w_ref[...])
            mo_ref[...] = mn
            vo_ref[...] = vn

        if nl == 1:
            update(g_refs[0])
        else:
            for layer, g_ref in enumerate(g_refs):
                pl.when(pl.program_id(0) == layer)(functools.partial(update, g_ref))

    spec = pl.BlockSpec((tm, c), lambda l, i: (l * nrow + i, 0))

    def gspec(layer):
        row = lambda l, i: jnp.where(l == layer, i, 0)
        if parts:
            return pl.BlockSpec((N_DEV, tm, c), lambda l, i: (0, row(l, i), 0))
        return pl.BlockSpec((tm, c), lambda l, i: (row(l, i), 0))

    return pl.pallas_call(
        body, name=name, grid=(nl, nrow), in_specs=[spec] + [gspec(k) for k in range(nl)] + [spec, spec],
        out_specs=[spec] * 4, out_shape=[jax.ShapeDtypeStruct(w.shape, F32)] * 4,
        compiler_params=_cparams("arbitrary", "arbitrary"),
    )(w, *gs, m, v)


def _sum_parts(x, *, name):
    def body(x_ref, o_ref):
        acc = x_ref[0]
        for k in range(1, N_DEV):
            acc = acc + x_ref[k]
        o_ref[...] = acc

    return pl.pallas_call(body, name=name, out_shape=jax.ShapeDtypeStruct(x.shape[1:], x.dtype))(x)


def _permute_in(a):
    pad = jnp.zeros(a.shape[:-1] + (IN_PAD - IN_WIDTH,), a.dtype)
    return jnp.concatenate([a[..., lo:hi] for lo, hi in _IN_SEGMENTS] + [pad], axis=-1)


def _unpermute_in(a):
    out, pos = {}, 0
    for lo, hi in _IN_SEGMENTS:
        out[lo] = a[..., pos:pos + hi - lo]
        pos += hi - lo
    return jnp.concatenate([out[lo] for lo in sorted(out)], axis=-1)


def _full_from_gathered(g, kind):
    if kind == 'row':
        return g.reshape((-1,) + g.shape[2:])
    if g.shape[-1] % V7X_LANES == 0:
        return jnp.moveaxis(g, 0, -2).reshape(g.shape[1:-1] + (-1,))
    return jnp.concatenate([g[d] for d in range(N_DEV)], axis=-1)


def _contrib_from_full(g, kind):
    if kind == 'row':
        return g.reshape((N_DEV, -1) + g.shape[1:])
    ns = g.shape[-1] // N_DEV
    if ns % V7X_LANES == 0:
        return jnp.moveaxis(g.reshape(g.shape[:-1] + (N_DEV, ns)), -2, 0)
    return jnp.stack([g[..., d * ns:(d + 1) * ns] for d in range(N_DEV)])


def _in_runs(ns):
    runs, pos = [], 0
    for lo, hi in _IN_SEGMENTS:
        for d in range(lo // ns, (hi - 1) // ns + 1):
            a, b = max(lo, d * ns), min(hi, (d + 1) * ns)
            runs.append((d, a - d * ns, b - d * ns, pos))
            pos += b - a
    return runs


def _w_in_from_gathered(g):
    pieces = [g[d][:, a:b] for d, a, b, _ in _in_runs(g.shape[2])]
    pad = jnp.zeros((g.shape[1], IN_PAD - IN_WIDTH), g.dtype)
    return jnp.concatenate(pieces + [pad], axis=1)


def _w_in_contrib(gp):
    ns = IN_WIDTH // N_DEV
    per_dev = [[] for _ in range(N_DEV)]
    for d, a, b, pos in sorted(_in_runs(ns), key=lambda r: (r[0], r[1])):
        per_dev[d].append(gp[:, pos:pos + b - a])
    return jnp.stack([jnp.concatenate(p, axis=1) for p in per_dev])


def _as2d(a, lead=0):
    return a.reshape(a.shape[:lead] + (-1, a.shape[-1]))


def _chan_cols(re, im):
    lead = re.shape[:-1]
    nb = S5_CH // SCAN_LANES
    return jnp.stack([re.reshape(lead + (nb, SCAN_LANES)), im.reshape(lead + (nb, SCAN_LANES))],
                     axis=-2).reshape(lead + (2 * S5_CH,))


def _s5_tables(lam_re, lam_im, log_dt, b_re, b_im, c_re, c_im):
    dt = jnp.exp(log_dt)[:, None]
    mag = jnp.exp(lam_re * dt)
    a_re = mag * jnp.cos(lam_im * dt)
    a_im = mag * jnp.sin(lam_im * dt)
    den = lam_re * lam_re + lam_im * lam_im
    f_re = ((a_re - 1.0) * lam_re + a_im * lam_im) / den
    f_im = (a_im * lam_re - (a_re - 1.0) * lam_im) / den
    bb_re = f_re[..., None] * b_re - f_im[..., None] * b_im
    bb_im = f_re[..., None] * b_im + f_im[..., None] * b_re
    gb = S5_GROUPS // S5_BD
    eye = jnp.eye(gb, dtype=F32)
    blocks = lambda a: a.reshape((S5_BD, gb) + a.shape[1:])

    def cols(re, im):
        shp = (S5_BD, S5_WIDTH // S5_BD, -1, SCAN_LANES)
        return jnp.stack([re.reshape(shp), im.reshape(shp)], axis=-2).reshape(S5_BD, S5_WIDTH // S5_BD, -1)

    flat = lambda a: a.reshape(S5_BD, S5_WIDTH // S5_BD, -1)
    wb_c = cols(flat(jnp.einsum('kgpc,gh->kgchp', blocks(bb_re), eye)), flat(jnp.einsum('kgpc,gh->kgchp', blocks(bb_im), eye)))
    wc_c = cols(flat(jnp.einsum('kgcp,gh->khcgp', blocks(c_re), eye)), -flat(jnp.einsum('kgcp,gh->khcgp', blocks(c_im), eye)))
    a_row = _chan_cols(a_re.reshape(1, S5_CH), a_im.reshape(1, S5_CH))
    return wb_c, wc_c.transpose(0, 2, 1), a_row


def _scan_tables(a_row, conj, seg_len):
    nb = S5_CH // SCAN_LANES
    a = a_row.reshape(nb, 2, SCAN_LANES)
    base = (a[:, 0], -a[:, 1] if conj else a[:, 1])
    mul = lambda x, y: (x[0] * y[0] - x[1] * y[1], x[0] * y[1] + x[1] * y[0])
    seg, sq, e = None, base, seg_len
    while e:
        if e & 1:
            seg = sq if seg is None else mul(seg, sq)
        sq, e = mul(sq, sq), e >> 1
    seg2 = mul(seg, seg)
    rows = [base, seg, seg2, mul(seg2, seg2)]
    lay = lambda z: jnp.stack([z[0], z[1]], axis=1).reshape(-1)
    return jnp.stack([lay(z) for z in rows] + [jnp.zeros((2 * S5_CH,), F32)] * (V7X_SUBLANES - len(rows)))


def _rope_tables(t):
    half = QK_ROPE // 2
    inv_freq = 1.0 / (ROPE_THETA ** (jnp.arange(0, QK_ROPE, 2, dtype=F32) / QK_ROPE))
    ang = jnp.arange(t, dtype=F32)[:, None] * inv_freq[None, :]
    cos, sin = jnp.cos(ang), jnp.sin(ang)
    zero = jnp.zeros_like(sin)

    def lay(nope, width, first, second, pad=0):
        head = jnp.concatenate([jnp.full((t, nope), 1.0 if first is cos else 0.0, F32), first, second,
                                jnp.zeros((t, pad), F32)], axis=1)
        reps = width // head.shape[1]
        out = jnp.tile(head, (1, reps))
        return jnp.pad(out, ((0, 0), (0, width - out.shape[1])))

    hq, pad = MLA_HEADS * V7X_LANES, V7X_LANES - QK_NOPE - QK_ROPE
    q_tabs = (lay(QK_NOPE, hq, cos, cos, pad), lay(QK_NOPE, hq, -sin, zero, pad), lay(QK_NOPE, hq, zero, sin, pad))
    k_tabs = (lay(0, V7X_LANES, cos, cos)[:, :V7X_LANES] * (jnp.arange(V7X_LANES) < QK_ROPE),
              lay(0, V7X_LANES, -sin, zero) * (jnp.arange(V7X_LANES) < QK_ROPE),
              lay(0, V7X_LANES, zero, sin) * (jnp.arange(V7X_LANES) < QK_ROPE))
    return q_tabs, k_tabs


def _sgu_tables(w_s, b_s):
    pos = jnp.arange(SGU_CHUNK) // CHUNK
    mask = pos[None, :] <= pos[:, None]
    wm = jnp.where(mask[None], w_s, 0.0).reshape(SGU_GROUPS * SGU_CHUNK, SGU_CHUNK)
    bias = jnp.repeat(b_s.T, SGU_WIDTH // SGU_GROUPS, axis=1)
    return wm, bias


def _row(v):
    return v.reshape(1, -1)


_S5_PARAMS = ('s5_lambda_re', 's5_lambda_im', 's5_log_dt', 's5_b_re', 's5_b_im', 's5_c_re', 's5_c_im')


def _derived_tables(p, t):
    (wb, wc, a_row), s5_pull = jax.vjp(jax.vmap(_s5_tables), *[p[n] for n in _S5_PARAMS])
    (wm, bias), sgu_pull = jax.vjp(jax.vmap(_sgu_tables), p['sgu_w_s'], p['sgu_b_s'])
    tab_f = jax.vmap(lambda a: _scan_tables(a, False, t // V7X_SUBLANES))(a_row)
    tab_b = jax.vmap(lambda a: _scan_tables(a, True, t // V7X_SUBLANES))(a_row)
    wb, wc = wb.astype(BF16), wc.astype(BF16)
    per_layer = [dict(s5_wb=wb[l], s5_wc=wc[l], s5_tab_fwd=tab_f[l], s5_tab_bwd=tab_b[l], sgu_wm=wm[l], sgu_bias=bias[l])
                 for l in range(len(wm))]

    def pull(grads):
        stacked = lambda k: jnp.stack([g[k] for g in grads])
        out = dict(zip(_S5_PARAMS, s5_pull((stacked('s5_wb'), stacked('s5_wc'), stacked('s5_a')))))
        out['sgu_w_s'], out['sgu_b_s'] = sgu_pull((stacked('sgu_wm'), stacked('sgu_bias')))
        return out

    return per_layer, pull


def _layer_fwd(x, ada, w, rope_tabs, tag, side=None, after_attn=None):
    s = {'x': x}
    q_tabs, k_tabs = rope_tabs
    sc1, gt1, sc2, gt2 = _row(1.0 + ada[1]), _row(1.0 + ada[2]), _row(1.0 + ada[4]), _row(1.0 + ada[5])
    s.update(sc1=sc1, gt1=gt1, sc2=sc2, gt2=gt2)
    (h,) = _rowcall(_modulate_fn, [x], [sc1, _row(ada[0])], [(D_MODEL, BF16)], tm=512, name=f"mod1_{tag}")
    proj = _mm(h, w['w_in_p'], bias=w['b_in_p'], name=f"proj_{tag}", tn=1792)
    s.update(h=h, proj=proj)

    u_view = (proj, S5_WIDTH, P_S5 // S5_WIDTH)
    u_seg = _seg_order(proj[:, P_S5:P_S5 + S5_WIDTH]).astype(BF16)
    hs, ylin = _s5_fwd(u_seg, w['s5_wb'], w['s5_wc'], w['s5_tab_fwd'], name=f"s5_fwd_{tag}")
    ylin = _time_order(ylin)
    s5_full = [_row(w['s5_d']), w['s5_w_glu'], _row(w['s5_b_glu'])]
    (y_s5,) = _rowcall(_s5_post_fn, [ylin, u_view], s5_full, [(S5_WIDTH, BF16)], tm=256, name=f"s5_post_{tag}")
    s.update(u_seg=u_seg, hs=hs, ylin=ylin, y_s5=y_s5)

    mla_rows = [(proj, MLA_BLK, P_MLA // MLA_BLK), *q_tabs, *k_tabs]
    mla_full = [_row(w['mla_q_norm']), w['mla_wq'], _row(w['mla_kv_norm']), w['mla_wk'], w['mla_wv']]
    hq, hv = MLA_HEADS * V7X_LANES, MLA_HEADS * V_HEAD
    q_r, k_r, v_r = _rowcall(_mla_pre_fn, mla_rows, mla_full, [(hq, BF16), (hq, BF16), (hv, BF16)],
                             tm=256, name=f"mla_pre_{tag}")
    (y_mla, lse), side_out = _attn_fwd(q_r, k_r, v_r, name=f"attn_fwd_{tag}", side=side)
    if after_attn is not None:
        after_attn(side_out)
    s.update(q_r=q_r, k_r=k_r, v_r=v_r, lse=lse, y_mla=y_mla, mla_full=mla_full)

    sgu_rows = [(proj, SGU_WIDTH, P_USGU // SGU_WIDTH), (proj, SGU_WIDTH, P_VSGU // SGU_WIDTH)]
    sgu_full = [_row(w['sgu_ln_g']), _row(w['sgu_ln_b']), w['sgu_wm'], w['sgu_bias']]
    (y_sgu,) = _rowcall(_sgu_fn, sgu_rows, sgu_full, [(SGU_WIDTH, BF16)], tm=SGU_CHUNK, name=f"sgu_{tag}")
    s.update(sgu_full=sgu_full, y_sgu=y_sgu)

    wbr = w['w_branch'].reshape(-1, D_MODEL)
    gate_rows = [(proj, D_MODEL, b) for b in range(3)]
    (merged,) = _rowcall(_merge_fn, [y_s5, y_mla, y_sgu] + gate_rows, [wbr], [(D_MODEL, BF16)], tm=256, name=f"merge_{tag}")
    ymix = _mm(merged, w['w_out'], name=f"wout_{tag}")
    (x1,) = _rowcall(_ln_res_fn, [x, ymix], [gt1, _row(w['ln1_g']), _row(w['ln1_b'])], [(D_MODEL, F32)], tm=256,
                     name=f"ln1_{tag}")
    s.update(merged=merged, ymix=ymix, x1=x1)

    (h2,) = _rowcall(_modulate_fn, [x1], [sc2, _row(ada[3])], [(D_MODEL, BF16)], tm=512, name=f"mod2_{tag}")
    ab = _mm(h2, w['ffn_w_in'], name=f"ffn_in_{tag}", tn=1408)
    (act,) = _rowcall(_swiglu_fn, [(ab, FF_HIDDEN, 0), (ab, FF_HIDDEN, 1)], [], [(FF_HIDDEN, BF16)], tm=256,
                      name=f"swiglu_{tag}")
    f = _mm(act, w['ffn_w_out'], name=f"ffn_out_{tag}", tk=2816)
    (x2,) = _rowcall(_ln_res_fn, [x1, f], [gt2, _row(w['ln2_g']), _row(w['ln2_b'])], [(D_MODEL, F32)], tm=256,
                     name=f"ln2_{tag}")
    s.update(h2=h2, ab=ab, act=act, f=f)
    return x2, s, side_out


def _mod_bwd_fn(x, dh, dxa, scale_row):
    return (dxa + dh * scale_row, jnp.sum(dh * x, axis=0, keepdims=True), jnp.sum(dh, axis=0, keepdims=True))


def _layer_bwd(dx2, s, w, rope_tabs, tag, make_side=None):
    g = {}
    q_tabs, k_tabs = rope_tabs
    t = dx2.shape[0]
    ln_full = lambda gt, a, b: [gt, _row(w[a]), _row(w[b])]

    dx1_a, df, dgt2, g['ln2_g'], g['ln2_b'] = _rowcall_vjp(
        _ln_res_fn, [s['x1'], s['f']], ln_full(s['gt2'], 'ln2_g', 'ln2_b'), [dx2], [0, 1], [0, 1, 2],
        tm=256, name=f"ln2_bwd_{tag}", row_dtypes=[F32, BF16])
    dact = _mm(df, w['ffn_w_out'], tb=True, name=f"ffn_out_dx_{tag}", tn=1408)
    g['ffn_w_out'] = _mm(s['act'], df, ta=True, name=f"ffn_out_dw_{tag}", tm=1408)
    def swiglu_bwd_fn(a, b, d):
        _, pull = jax.vjp(_swiglu_fn, a, b)
        return (jnp.concatenate(pull((d,)), axis=1),)

    (dab,) = _rowcall(swiglu_bwd_fn, [(s['ab'], FF_HIDDEN, 0), (s['ab'], FF_HIDDEN, 1), dact], [], [(2 * FF_HIDDEN, BF16)],
                      tm=256, name=f"swiglu_bwd_{tag}")
    dh2 = _mm(dab, w['ffn_w_in'], tb=True, name=f"ffn_in_dx_{tag}", tk=1408)
    g['ffn_w_in'] = _mm(s['h2'], dab, ta=True, name=f"ffn_in_dw_{tag}", tn=1408)
    dx1, dsc2, dsh2 = _rowcall(_mod_bwd_fn, [s['x1'], dh2, dx1_a], [s['sc2']], [(D_MODEL, F32)],
                               [((1, D_MODEL), F32)] * 2, tm=256, name=f"mod2_bwd_{tag}")

    dx_a, dymix, dgt1, g['ln1_g'], g['ln1_b'] = _rowcall_vjp(
        _ln_res_fn, [s['x'], s['ymix']], ln_full(s['gt1'], 'ln1_g', 'ln1_b'), [dx1], [0, 1], [0, 1, 2],
        tm=256, name=f"ln1_bwd_{tag}", row_dtypes=[F32, BF16])
    dmerged = _mm(dymix, w['w_out'], tb=True, name=f"wout_dx_{tag}")
    g['w_out'] = _mm(s['merged'], dymix, ta=True, name=f"wout_dw_{tag}")

    proj = s['proj']
    wbr = w['w_branch'].reshape(-1, D_MODEL)
    gate_rows = [(proj, D_MODEL, b) for b in range(3)]
    dy_s5, dy_mla, dy_sgu, dl0, dl1, dl2, dwbr = _rowcall_vjp(
        _merge_fn, [s['y_s5'], s['y_mla'], s['y_sgu']] + gate_rows, [wbr], [dmerged], [0, 1, 2, 3, 4, 5], [0],
        tm=256, name=f"merge_bwd_{tag}")
    g['w_branch'] = dwbr.reshape(w['w_branch'].shape)

    sgu_rows = [(proj, SGU_WIDTH, P_USGU // SGU_WIDTH), (proj, SGU_WIDTH, P_VSGU // SGU_WIDTH)]
    du_sgu, dv_sgu, dlg, dlb, dwm, dbias = _rowcall_vjp(
        _sgu_fn, sgu_rows, s['sgu_full'], [dy_sgu], [0, 1], [0, 1, 2, 3], tm=SGU_CHUNK, name=f"sgu_bwd_{tag}")
    g['sgu_ln_g'], g['sgu_ln_b'] = dlg.reshape(-1), dlb.reshape(-1)
    g['sgu_wm'], g['sgu_bias'] = dwm, dbias

    (dq_r, dk_r, dv_r), side_out = _attn_bwd(s['q_r'], s['k_r'], s['v_r'], s['y_mla'], s['lse'], dy_mla,
                                             name=f"attn_bwd_{tag}", side=make_side(g) if make_side is not None else None)
    mla_rows = [(proj, MLA_BLK, P_MLA // MLA_BLK), *q_tabs, *k_tabs]
    dmla, dqn, dwq, dkvn, dwk, dwv = _rowcall_vjp(
        _mla_pre_fn, mla_rows, s['mla_full'], [dq_r, dk_r, dv_r], [0], [0, 1, 2, 3, 4], tm=256, name=f"mla_pre_bwd_{tag}")
    g['mla_w_q_up'], g['mla_w_kv_up'] = _mla_weight_grads(dwq, dwk, dwv)
    g['mla_q_norm'], g['mla_kv_norm'] = dqn.reshape(-1), dkvn.reshape(-1)

    s5_full = [_row(w['s5_d']), w['s5_w_glu'], _row(w['s5_b_glu'])]
    dylin, du_a, dd, g['s5_w_glu'], dbg = _rowcall_vjp(
        _s5_post_fn, [s['ylin'], (proj, S5_WIDTH, P_S5 // S5_WIDTH)], s5_full, [dy_s5], [0, 1], [0, 1, 2], tm=256,
        name=f"s5_post_bwd_{tag}", row_dtypes=[BF16, F32])
    g['s5_d'], g['s5_b_glu'] = dd.reshape(-1), dbg.reshape(-1)
    du_b, g['s5_wb'], g['s5_wc'], da_part = _s5_bwd(_seg_order(dylin), s['u_seg'], s['hs'], w['s5_wb'], w['s5_wc'],
                                                     w['s5_tab_bwd'], name=f"s5_bwd_{tag}")
    du_b = _time_order(du_b)
    g['s5_a'] = jnp.sum(da_part, axis=0, keepdims=True)

    def dproj_fn(g0, g1, g2, ua, ub, us, vs, ml):
        d = jnp.concatenate([g0, g1, g2, ua + ub, us, vs, ml], axis=1)
        return d, jnp.sum(d, axis=0, keepdims=True)

    dproj, db_in = _rowcall(dproj_fn, [dl0, dl1, dl2, du_a, du_b, du_sgu, dv_sgu, dmla], [], [(IN_PAD, BF16)],
                            [((1, IN_PAD), F32)], tm=256, name=f"dproj_{tag}")
    dh = _mm(dproj, w['w_in_p'], tb=True, name=f"proj_dx_{tag}", tk=1792)
    g['w_in_p'] = _mm(s['h'], dproj, ta=True, name=f"proj_dw_{tag}", tn=896)
    g['b_in'] = _unpermute_in(db_in).reshape(-1)
    dx, dsc1, dsh1 = _rowcall(_mod_bwd_fn, [s['x'], dh, dx_a], [s['sc1']], [(D_MODEL, F32)], [((1, D_MODEL), F32)] * 2,
                              tm=256, name=f"mod1_bwd_{tag}")
    d_ada = jnp.concatenate([dsh1, dsc1, dgt1, dsh2, dsc2, dgt2], axis=0)
    return dx, d_ada, g, side_out


def _loss_fn(y, target):
    err = y - target
    return (err / D_MODEL, 0.5 * jnp.sum(jnp.sum(err * err, axis=1, keepdims=True), axis=0, keepdims=True) / D_MODEL)


def _step(p):
    me = 4 * lax.axis_index("x") + 2 * lax.axis_index("y") + lax.axis_index("c")
    x = p['x'][0]
    t = x.shape[0]
    rope_tabs = _rope_tables(t)

    (c_all,) = _exchange([jnp.broadcast_to(p['c'], (V7X_SUBLANES, D_MODEL))], gather=True, name="gather_c")
    c_all = c_all[:, 0, :]
    (c_act,) = _rowcall(lambda cc: (cc * _sigmoid(cc),), [c_all], [], [(D_MODEL, F32)], tm=N_DEV, name="c_silu")
    ncol = p['w_ada'].shape[2]
    b_ada_loc = lax.dynamic_slice_in_dim(p['b_ada'], me * ncol, ncol, axis=1)
    ada_cols = jnp.concatenate([_mm(c_act, p['w_ada'][l], bias=b_ada_loc[l:l + 1], name=f"ada_{l}") for l in range(DEPTH)])
    (ada_all,) = _exchange([ada_cols], gather=True, name="gather_ada")
    ada_all = ada_all.reshape(N_DEV, DEPTH, N_DEV, ncol)
    ada = lax.dynamic_index_in_dim(ada_all, me, axis=2, keepdims=False)
    ada = ada.transpose(1, 0, 2).reshape(DEPTH, 6, D_MODEL)

    mixer_w, ffn_w = SHARDED[:-2], SHARDED[-2:]

    def shards(l, group):
        return [p[n][l].astype(BF16) for n, _ in group]

    def contribs(g, group):
        return [(_w_in_contrib(g['w_in_p']) if n == 'w_in' else _contrib_from_full(g[n], kind)).astype(BF16)
                for n, kind in group]

    tables, pull_tables = _derived_tables(p, t)

    def mixer_weights(l, gathered):
        w = {n: _full_from_gathered(g, kind) for (n, kind), g in zip(mixer_w[1:], gathered[1:])}
        w['w_in_p'] = _w_in_from_gathered(gathered[0])
        w['b_in_p'] = _row(_permute_in(p['b_in'][l]))
        w['mla_wq'], w['mla_wk'], w['mla_wv'] = _mla_weights(w['mla_w_q_up'], w['mla_w_kv_up'])
        w.update(tables[l])
        for n in SMALL:
            if n != 'b_ada' and n != 'b_in':
                w[n] = p[n][l]
        return w

    saved, layers = [], []
    gathered = _exchange(shards(0, mixer_w), gather=True, name="gather_w_0", two_level=True)
    for l in range(DEPTH):
        w = mixer_weights(l, gathered)
        side = _Exchange(shards(l, ffn_w) + (shards(l + 1, mixer_w) if l + 1 < DEPTH else []), True, two_level=True)

        def add_ffn(res, w=w):
            for (n, kind), g in zip(ffn_w, res):
                w[n] = _full_from_gathered(g, kind)

        x, s, res = _layer_fwd(x, ada[l], w, rope_tabs, f"l{l}", side, add_ffn)
        gathered = res[len(ffn_w):]
        layers.append(w)
        saved.append(s)
    dy, loss_loc = _rowcall(_loss_fn, [x, p['loss_target'][0]], [], [(D_MODEL, F32)], [((1, 1), F32)], tm=256, name="loss")
    loss = lax.psum(loss_loc[0, 0], ("x", "y", "c"))

    d_ada, grads, landed = [None] * DEPTH, [None] * DEPTH, [[None, None] for _ in range(DEPTH)]
    dx, pending = dy, []
    for l in reversed(range(DEPTH)):
        make_side = lambda g, pending=pending: _Exchange(contribs(g, ffn_w) + pending, False)
        dx, d_ada[l], grads[l], res = _layer_bwd(dx, saved[l], layers[l], rope_tabs, f"l{l}", make_side)
        landed[l][1] = res[:len(ffn_w)]
        if l + 1 < DEPTH:
            landed[l + 1][0] = res[len(ffn_w):]
        pending = contribs(grads[l], mixer_w)
    landed[0][0] = _exchange(pending, gather=False, name="scatter_g_0")
    landed = [list(a) + list(b) for a, b in landed]
    d_ada = jnp.stack(d_ada).reshape(DEPTH, 6 * D_MODEL)

    (d_ada_all,) = _exchange([d_ada], gather=True, name="gather_dada")
    d_ada_cols = lax.dynamic_slice_in_dim(d_ada_all, me * ncol, ncol, axis=2)
    pad_b = ((0, V7X_LANES - N_DEV), (0, 0))
    c_act_p = jnp.pad(c_act, pad_b)
    g_w_ada = jnp.stack([_mm(c_act_p, jnp.pad(d_ada_cols[:, l], pad_b), ta=True, name=f"ada_dw_{l}") for l in range(DEPTH)])

    out = {}
    kinds = ('grad_', 'delta_', 'new_m_', 'new_v_')
    for i, (n, _) in enumerate(SHARDED):
        res = _adamw(_as2d(p[n]), [_as2d(landed[l][i], lead=1) for l in range(DEPTH)], _as2d(p['m_' + n]),
                     _as2d(p['v_' + n]), name=f"adamw_{n}")
        for kind, r in zip(kinds, res):
            out[kind + n] = r.reshape(p[n].shape)

    res = _adamw(_as2d(p['w_ada']), [_as2d(g_w_ada)], _as2d(p['m_w_ada']), _as2d(p['v_w_ada']), name="adamw_ada")
    for kind, r in zip(kinds, res):
        out[kind + 'w_ada'] = r.reshape(p['w_ada'].shape)

    small_g = pull_tables(grads)
    small_g.update({n: jnp.stack([grads[l][n] for l in range(DEPTH)]) for n in SMALL if n not in small_g and n != 'b_ada'})
    small_g['b_ada'] = d_ada
    n_small = sum(int(np.prod(p[n].shape)) for n in SMALL)
    n_pad = -(-n_small // SMALL_PAD) * SMALL_PAD
    flat = jnp.concatenate([small_g[n].reshape(-1) for n in SMALL])
    flat = jnp.pad(flat, (0, n_pad - n_small)).reshape(N_DEV, n_pad // N_DEV // 1024, 1024)
    (landed_small,) = _exchange([flat], gather=False, name="scatter_small")
    (g_all,) = _exchange([_sum_parts(landed_small, name="sum_small")], gather=True, name="gather_small")
    g_all = g_all.reshape(-1)
    off = 0
    for n in SMALL:
        size = int(np.prod(p[n].shape))
        g_n = g_all[off:off + size].reshape(p[n].shape)
        off += size
        res = _adamw(_as2d(p[n]), [_as2d(g_n)], _as2d(p['m_' + n]), _as2d(p['v_' + n]), name=f"adamw_{n}")
        for kind, r in zip(kinds, res):
            out[kind + n] = r.reshape(p[n].shape)

    outs = [loss, dx[None]]
    for kind in ('grad_', 'delta_', 'new_m_', 'new_v_'):
        outs += [out[kind + n] for n in WNAMES]
    return tuple(outs)


def kernel(x, c, w_ada, b_ada, w_in, b_in, s5_lambda_re, s5_lambda_im, s5_log_dt, s5_b_re, s5_b_im, s5_c_re, s5_c_im, s5_d, s5_w_glu, s5_b_glu, mla_q_norm, mla_w_q_up, mla_kv_norm, mla_w_kv_up, sgu_ln_g, sgu_ln_b, sgu_w_s, sgu_b_s, w_branch, w_out, ln1_g, ln1_b, ffn_w_in, ffn_w_out, ln2_g, ln2_b, loss_target, m_w_ada, m_b_ada, m_w_in, m_b_in, m_s5_lambda_re, m_s5_lambda_im, m_s5_log_dt, m_s5_b_re, m_s5_b_im, m_s5_c_re, m_s5_c_im, m_s5_d, m_s5_w_glu, m_s5_b_glu, m_mla_q_norm, m_mla_w_q_up, m_mla_kv_norm, m_mla_w_kv_up, m_sgu_ln_g, m_sgu_ln_b, m_sgu_w_s, m_sgu_b_s, m_w_branch, m_w_out, m_ln1_g, m_ln1_b, m_ffn_w_in, m_ffn_w_out, m_ln2_g, m_ln2_b, v_w_ada, v_b_ada, v_w_in, v_b_in, v_s5_lambda_re, v_s5_lambda_im, v_s5_log_dt, v_s5_b_re, v_s5_b_im, v_s5_c_re, v_s5_c_im, v_s5_d, v_s5_w_glu, v_s5_b_glu, v_mla_q_norm, v_mla_w_q_up, v_mla_kv_norm, v_mla_w_kv_up, v_sgu_ln_g, v_sgu_ln_b, v_sgu_w_s, v_sgu_b_s, v_w_branch, v_w_out, v_ln1_g, v_ln1_b, v_ffn_w_in, v_ffn_w_out, v_ln2_g, v_ln2_b):
    return _step(dict(locals()))
```

```python
import functools
import math

import numpy as np
import jax
import jax.numpy as jnp
from jax import lax
from jax.experimental import pallas as pl
from jax.experimental.pallas import tpu as pltpu

F32 = jnp.float32
BF16 = jnp.bfloat16

N_DEV = 8
D_MODEL = 1024
DEPTH = 4
CHUNK = 64
S5_WIDTH = 512
S5_GROUP = 16
S5_GROUPS = 32
S5_STATE = 64
MLA_HEADS = 8
QK_NOPE = 64
QK_ROPE = 32
V_HEAD = 64
Q_LORA = 384
KV_LORA = 256
ROPE_THETA = 10000.0
SGU_WIDTH = 512
SGU_GROUPS = 4
SGU_CHUNK = 128
FF_HIDDEN = 2816
DEEPNORM_ALPHA = (2 * DEPTH) ** 0.25
LN_EPS = 1e-5
RMS_EPS = 1e-6
NEG_INF = -1e30
ADAM_LR = 0.001
ADAM_B1 = 0.9
ADAM_B2 = 0.999
ADAM_EPS = 1e-08
ADAM_WD = 0.01
ADAM_STEP = 10

IN_WIDTH = 5280
IN_PAD = 5376
_O_S5, _O_CQ, _O_CKV, _O_KPE, _O_USGU, _O_VSGU, _O_GATE = 0, 512, 896, 1152, 1184, 1696, 2208
_IN_SEGMENTS = ((_O_GATE, IN_WIDTH), (_O_S5, _O_CQ), (_O_USGU, _O_VSGU), (_O_VSGU, _O_GATE), (_O_CQ, _O_USGU))
P_GATE, P_S5, P_USGU, P_VSGU, P_MLA = 0, 3072, 3584, 4096, 4608
MLA_BLK = 768

V7X_LANES = 128
V7X_SUBLANES = 8
VMEM_LIMIT = 56 * 1024 * 1024
ATT_BLOCK = 512
SCAN_LANES = 128
S5_CH = S5_GROUPS * S5_STATE
S5_BD = 4

WNAMES = ['w_ada', 'b_ada', 'w_in', 'b_in', 's5_lambda_re', 's5_lambda_im', 's5_log_dt', 's5_b_re', 's5_b_im',
          's5_c_re', 's5_c_im', 's5_d', 's5_w_glu', 's5_b_glu', 'mla_q_norm', 'mla_w_q_up', 'mla_kv_norm',
          'mla_w_kv_up', 'sgu_ln_g', 'sgu_ln_b', 'sgu_w_s', 'sgu_b_s', 'w_branch', 'w_out', 'ln1_g', 'ln1_b',
          'ffn_w_in', 'ffn_w_out', 'ln2_g', 'ln2_b']
SHARDED = (('w_in', 'col'), ('s5_w_glu', 'row'), ('mla_w_q_up', 'col'), ('mla_w_kv_up', 'col'),
           ('w_branch', 'col3'), ('w_out', 'row'), ('ffn_w_in', 'col'), ('ffn_w_out', 'row'))
SMALL = [n for n in WNAMES if n != 'w_ada' and n not in dict(SHARDED)]
SMALL_PAD = N_DEV * V7X_SUBLANES * 1024


def _cparams(*sem):
    return pltpu.CompilerParams(dimension_semantics=sem, vmem_limit_bytes=VMEM_LIMIT)


def _pick(n, target):
    if n <= target:
        return n
    best = None
    for d in range(V7X_LANES, target + 1, V7X_LANES):
        if n % d == 0:
            best = d
    assert best is not None, (n, target)
    return best


def _pick_rows(n, target):
    if n <= target:
        return n
    best = None
    for d in range(V7X_SUBLANES, target + 1, V7X_SUBLANES):
        if n % d == 0:
            best = d
    assert best is not None, (n, target)
    return best


@jax.custom_vjp
def _bdot(a, b):
    return jnp.dot(a.astype(BF16), b.astype(BF16), preferred_element_type=F32)


def _bdot_fwd(a, b):
    return _bdot(a, b), (a, b)


def _bdot_bwd(res, g):
    a, b = res
    gb = g.astype(BF16)
    da = lax.dot_general(gb, b.astype(BF16), (((1,), (1,)), ((), ())), preferred_element_type=F32)
    db = lax.dot_general(a.astype(BF16), gb, (((0,), (0,)), ((), ())), preferred_element_type=F32)
    return da.astype(a.dtype), db.astype(b.dtype)


_bdot.defvjp(_bdot_fwd, _bdot_bwd)


@functools.partial(jax.custom_vjp, nondiff_argnums=(1,))
def _lane_roll(x, shift):
    return pltpu.roll(x, shift % x.shape[1], 1)


def _lane_roll_fwd(x, shift):
    return _lane_roll(x, shift), None


def _lane_roll_bwd(shift, _, g):
    return (_lane_roll(g, -shift),)


_lane_roll.defvjp(_lane_roll_fwd, _lane_roll_bwd)


def _sigmoid(x):
    return 1.0 / (1.0 + jnp.exp(-x))


def _gelu(x):
    return 0.5 * x * (1.0 + jnp.tanh(math.sqrt(2.0 / math.pi) * (x + 0.044715 * (x * x * x))))


def _layer_norm(x, g, b):
    mu = jnp.mean(x, axis=-1, keepdims=True)
    var = jnp.mean(jnp.square(x - mu), axis=-1, keepdims=True)
    return (x - mu) * lax.rsqrt(var + LN_EPS) * g + b


def _rms_norm(x, g):
    return x * lax.rsqrt(jnp.mean(x * x, axis=-1, keepdims=True) + RMS_EPS) * g


def _rope(x, c, s1, s2):
    half = QK_ROPE // 2
    return x * c + _lane_roll(x, -half) * s1 + _lane_roll(x, half) * s2


def _modulate_fn(x, scale_row, shift_row):
    return (x * scale_row + shift_row,)


def _ln_res_fn(x, y, gate_row, g, b):
    return (_layer_norm(DEEPNORM_ALPHA * x + gate_row * y, g, b),)


def _s5_post_fn(ylin, u, d, w_glu, b_glu):
    z = _gelu(ylin + d * u)
    return (z * _sigmoid(_bdot(z, w_glu) + b_glu),)


def _mla_pre_fn(blk, cq_t, sq1, sq2, ck_t, sk1, sk2, q_norm, w_q, kv_norm, w_k, w_v):
    cq, ckv, kpe = blk[:, :Q_LORA], blk[:, Q_LORA:Q_LORA + KV_LORA], blk[:, Q_LORA + KV_LORA:]
    q = _rope(_bdot(_rms_norm(cq, q_norm), w_q), cq_t, sq1, sq2) * Q_SCALE
    ckv_n = _rms_norm(ckv, kv_norm)
    kpe_r = _lane_roll(_rope(kpe, ck_t, sk1, sk2), QK_NOPE)
    k = _bdot(ckv_n, w_k) + jnp.concatenate([kpe_r] * MLA_HEADS, axis=1)
    return q, k, _bdot(ckv_n, w_v)


def _pad_heads(w, width):
    w3 = w.reshape(w.shape[0], MLA_HEADS, width)
    return jnp.pad(w3, ((0, 0), (0, 0), (0, V7X_LANES - width))).reshape(w.shape[0], MLA_HEADS * V7X_LANES)


def _mla_weights(w_q_up, w_kv_up):
    kv3 = w_kv_up.reshape(w_kv_up.shape[0], MLA_HEADS, QK_NOPE + V_HEAD)
    w_k = _pad_heads(kv3[:, :, :QK_NOPE].reshape(w_kv_up.shape[0], -1), QK_NOPE)
    return _pad_heads(w_q_up, QK_NOPE + QK_ROPE), w_k, kv3[:, :, QK_NOPE:].reshape(w_kv_up.shape[0], -1)


def _mla_weight_grads(dw_q, dw_k, dw_v):
    unpad = lambda a, width: a.reshape(a.shape[0], MLA_HEADS, V7X_LANES)[:, :, :width]
    dkv = jnp.concatenate([unpad(dw_k, QK_NOPE), dw_v.reshape(dw_v.shape[0], MLA_HEADS, V_HEAD)], axis=2)
    return unpad(dw_q, QK_NOPE + QK_ROPE).reshape(dw_q.shape[0], -1), dkv.reshape(dw_k.shape[0], -1)


def _sgu_fn(u, v, g, b, wm, bias):
    vn = _layer_norm(_gelu(v), g, b)
    w = SGU_CHUNK
    parts = [_bdot(wm[k * w:(k + 1) * w, :], vn[:, k * w:(k + 1) * w]) for k in range(SGU_GROUPS)]
    return (_gelu(u) * (jnp.concatenate(parts, axis=1) + bias),)


def _merge_fn(y0, y1, y2, l0, l1, l2, wb):
    n = S5_WIDTH
    return (_sigmoid(l0) * _bdot(y0, wb[:n]) + _sigmoid(l1) * _bdot(y1, wb[n:2 * n])
            + _sigmoid(l2) * _bdot(y2, wb[2 * n:]),)


def _swiglu_fn(a, b):
    return (a * _sigmoid(a) * b,)


def _rowcall(fn, rows, fulls, out_rows, out_reds=(), *, tm, name):
    rows = [r if isinstance(r, tuple) else (r, r.shape[1], 0) for r in rows]
    t = rows[0][0].shape[0]
    tm = _pick_rows(t, tm)
    n_in, n_or, n_red = len(rows) + len(fulls), len(out_rows), len(out_reds)

    def body(*refs):
        vals = fn(*[r[...] for r in refs[:n_in]])
        assert len(vals) == n_or + n_red, (name, len(vals))
        for ref, v in zip(refs[n_in:n_in + n_or], vals[:n_or]):
            ref[...] = v.astype(ref.dtype)
        if n_red:
            red_refs = refs[n_in + n_or:]

            @pl.when(pl.program_id(0) == 0)
            def _():
                for ref in red_refs:
                    ref[...] = jnp.zeros_like(ref)

            for ref, v in zip(red_refs, vals[n_or:]):
                ref[...] += v.astype(ref.dtype)

    in_specs = [pl.BlockSpec((tm, w), functools.partial(lambda i, blk: (i, blk), blk=blk)) for _, w, blk in rows]
    in_specs += [pl.BlockSpec(f.shape, lambda i: (0, 0)) for f in fulls]
    out_specs = [pl.BlockSpec((tm, c), lambda i: (i, 0)) for c, _ in out_rows]
    out_specs += [pl.BlockSpec(s, lambda i: (0, 0)) for s, _ in out_reds]
    out_shape = [jax.ShapeDtypeStruct((t, c), dt) for c, dt in out_rows]
    out_shape += [jax.ShapeDtypeStruct(s, dt) for s, dt in out_reds]
    return pl.pallas_call(
        body, name=name, grid=(t // tm,), in_specs=in_specs, out_specs=out_specs, out_shape=out_shape,
        compiler_params=_cparams("arbitrary" if n_red else "parallel"),
    )(*[r[0] for r in rows], *fulls)


def _rowcall_vjp(fn, rows, fulls, cots, diff_rows, diff_fulls, *, tm, name, row_dtypes=None):
    rows_n = [r if isinstance(r, tuple) else (r, r.shape[1], 0) for r in rows]
    n_r, n_c = len(rows), len(cots)
    row_dtypes = row_dtypes or [F32] * len(diff_rows)

    def fn2(*vals):
        r = [v.astype(F32) for v in vals[:n_r]]
        ct = vals[n_r:n_r + n_c]
        f = [v.astype(F32) for v in vals[n_r + n_c:]]

        def g(*dargs):
            rr, ff = list(r), list(f)
            for k, idx in enumerate(diff_rows):
                rr[idx] = dargs[k]
            for k, idx in enumerate(diff_fulls):
                ff[idx] = dargs[len(diff_rows) + k]
            return fn(*rr, *ff)

        prim = [r[i] for i in diff_rows] + [f[i] for i in diff_fulls]
        outs, pull = jax.vjp(g, *prim)
        return pull(tuple(c.astype(o.dtype) for c, o in zip(ct, outs)))

    out_rows = [(rows_n[i][1], dt) for i, dt in zip(diff_rows, row_dtypes)]
    out_reds = [(fulls[i].shape, F32) for i in diff_fulls]
    return _rowcall(fn2, list(rows) + list(cots), fulls, out_rows, out_reds, tm=tm, name=name)


def _mm(a, b, *, ta=False, tb=False, bias=None, out_dtype=F32, name, tm=1024, tn=1024, tk=1024):
    (k_a, m) = a.shape if ta else a.shape[::-1]
    (n, k_b) = b.shape if tb else b.shape[::-1]
    assert k_a == k_b, (name, a.shape, b.shape)
    tm, tn, tk = _pick(m, tm) if m % V7X_LANES == 0 else m, _pick(n, tn), _pick(k_a, tk) if k_a % V7X_LANES == 0 else k_a
    nk = k_a // tk
    a_spec = pl.BlockSpec((tk, tm), lambda i, j, k: (k, i)) if ta else pl.BlockSpec((tm, tk), lambda i, j, k: (i, k))
    b_spec = pl.BlockSpec((tn, tk), lambda i, j, k: (j, k)) if tb else pl.BlockSpec((tk, tn), lambda i, j, k: (k, j))
    dims = (((0,) if ta else (1,), (1,) if tb else (0,)), ((), ()))
    has_bias = bias is not None

    def body(*refs):
        a_ref, b_ref = refs[0], refs[1]
        part = lax.dot_general(a_ref[...].astype(BF16), b_ref[...].astype(BF16), dims, preferred_element_type=F32)
        if nk == 1:
            o_ref = refs[-1]
            o_ref[...] = (part + refs[2][...] if has_bias else part).astype(o_ref.dtype)
            return
        o_ref, acc_ref = refs[-2], refs[-1]
        k = pl.program_id(2)

        @pl.when(k == 0)
        def _():
            acc_ref[...] = part

        @pl.when(k > 0)
        def _():
            acc_ref[...] += part

        @pl.when(k == nk - 1)
        def _():
            r = acc_ref[...]
            if has_bias:
                r = r + refs[2][...]
            o_ref[...] = r.astype(o_ref.dtype)

    in_specs = [a_spec, b_spec] + ([pl.BlockSpec((1, tn), lambda i, j, k: (0, j))] if has_bias else [])
    return pl.pallas_call(
        body, name=name, grid=(m // tm, n // tn, nk), in_specs=in_specs,
        out_specs=pl.BlockSpec((tm, tn), lambda i, j, k: (i, j)),
        out_shape=jax.ShapeDtypeStruct((m, n), out_dtype),
        scratch_shapes=[pltpu.VMEM((tm, tn), F32)] if nk > 1 else [],
        compiler_params=_cparams("parallel", "parallel", "arbitrary"),
    )(a, b, *([bias] if has_bias else []))


def _mm_swiglu(x, w, *, name, tm=512, tn=1408):
    m, k = x.shape
    hdim = w.shape[1] // 2
    tm, tn = _pick(m, tm), _pick(hdim, tn)
    nj = hdim // tn

    def body(x_ref, wa_ref, wb_ref, a_ref, b_ref, act_ref):
        xb = x_ref[...].astype(BF16)
        a = jnp.dot(xb, wa_ref[...].astype(BF16), preferred_element_type=F32)
        b = jnp.dot(xb, wb_ref[...].astype(BF16), preferred_element_type=F32)
        a_ref[...] = a
        b_ref[...] = b
        act_ref[...] = _swiglu_fn(a, b)[0].astype(act_ref.dtype)

    tile = pl.BlockSpec((tm, tn), lambda i, j: (i, j))
    return pl.pallas_call(
        body, name=name, grid=(m // tm, nj),
        in_specs=[pl.BlockSpec((tm, k), lambda i, j: (i, 0)), pl.BlockSpec((k, tn), lambda i, j: (0, j)),
                  pl.BlockSpec((k, tn), lambda i, j: (0, j + nj))],
        out_specs=[tile, tile, tile],
        out_shape=[jax.ShapeDtypeStruct((m, hdim), F32), jax.ShapeDtypeStruct((m, hdim), F32),
                   jax.ShapeDtypeStruct((m, hdim), BF16)],
        compiler_params=_cparams("parallel", "parallel"),
    )(x, w, w)


def _seg_order(a):
    t, c = a.shape
    return a.reshape(V7X_SUBLANES, t // V7X_SUBLANES, c).transpose(1, 0, 2).reshape(t, c)


def _time_order(a):
    t, c = a.shape
    return a.reshape(t // V7X_SUBLANES, V7X_SUBLANES, c).transpose(1, 0, 2).reshape(t, c)


def _scan_in_place(o_ref, tab_ref, reverse, h_ref=None, da_ref=None):
    ntile = o_ref.shape[0] // V7X_SUBLANES
    with_da = h_ref is not None
    ln = SCAN_LANES
    rows8 = V7X_SUBLANES
    subs = range(o_ref.shape[1] // (2 * ln))
    col = lambda s, part: slice((2 * s + part) * ln, (2 * s + part + 1) * ln)
    cmul = lambda p, q: (p[0] * q[0] - p[1] * q[1], p[0] * q[1] + p[1] * q[0])
    cadd = lambda p, q: (p[0] + q[0], p[1] + q[1])

    def run():
        row = lax.broadcasted_iota(jnp.int32, (rows8, ln), 0)
        bcast = lambda k, s: tuple(jnp.broadcast_to(tab_ref[k:k + 1, col(s, part)], (rows8, ln)) for part in (0, 1))
        a = [bcast(0, s) for s in subs]
        zero = [(jnp.zeros((rows8, ln), F32), jnp.zeros((rows8, ln), F32)) for _ in subs]

        def tile(ref, i):
            r0 = pl.multiple_of(i * rows8, rows8)
            return [(ref[pl.ds(r0, rows8), col(s, 0)], ref[pl.ds(r0, rows8), col(s, 1)]) for s in subs]

        def put(i, ys):
            r0 = pl.multiple_of(i * rows8, rows8)
            for s, (yr, yi) in enumerate(ys):
                o_ref[pl.ds(r0, rows8), col(s, 0)] = yr
                o_ref[pl.ds(r0, rows8), col(s, 1)] = yi

        def shifted(z, step, up):
            shift, keep = (rows8 - step, row < rows8 - step) if up else (step, row >= step)
            return tuple(jnp.where(keep, pltpu.roll(v, shift, 0), 0.0) for v in z)

        def pass1(n, carry):
            i = (ntile - 1 - n) if reverse else n
            ys = [cadd(x, cmul(a[s], carry[s])) for s, x in enumerate(tile(o_ref, i))]
            put(i, ys)
            return ys

        ends = lax.fori_loop(0, ntile, pass1, zero, unroll=4)
        incoming = []
        for s in subs:
            f = ends[s]
            for k, step in zip((1, 2, 3), (1, 2, 4)):
                f = cadd(f, cmul(bcast(k, s), shifted(f, step, reverse)))
            incoming.append(shifted(f, 1, reverse))
        if with_da:
            h0 = [shifted(h, 1, False) for h in tile(h_ref, ntile - 1)]

        def pass2(n, carry):
            i = (ntile - 1 - n) if reverse else n
            power = carry[0]
            ys = [cadd(y, cmul(power[s], incoming[s])) for s, y in enumerate(tile(o_ref, i))]
            put(i, ys)
            new_power = [cmul(power[s], a[s]) for s in subs]
            if not with_da:
                return (new_power,)
            prev = tile(h_ref, jnp.maximum(i - 1, 0))
            acc = []
            for s in subs:
                hpr, hpi = jnp.where(i > 0, prev[s][0], h0[s][0]), jnp.where(i > 0, prev[s][1], h0[s][1])
                (yr, yi), (sr, si) = ys[s], carry[1][s]
                acc.append((sr + yr * hpr + yi * hpi, si + yi * hpr - yr * hpi))
            return (new_power, acc)

        out = lax.fori_loop(0, ntile, pass2, (list(a),) + ((zero,) if with_da else ()), unroll=4)
        if with_da:
            for s in subs:
                da_ref[:, col(s, 0)] = out[1][s][0]
                da_ref[:, col(s, 1)] = out[1][s][1]

    run()


SCAN_BLOCK = 4 * SCAN_LANES
_S5_SUB = 2 * S5_CH // S5_BD // SCAN_BLOCK


def _s5_specs(t):
    ku, blk = S5_WIDTH // S5_BD, SCAN_BLOCK
    return dict(
        u=pl.BlockSpec((t, ku), lambda j: (0, j // _S5_SUB)),
        hs=pl.BlockSpec((t, blk), lambda j: (0, j)),
        wb=pl.BlockSpec((None, ku, blk), lambda j: (j // _S5_SUB, 0, j % _S5_SUB)),
        wc=pl.BlockSpec((None, blk, ku), lambda j: (j // _S5_SUB, j % _S5_SUB, 0)),
        tab=pl.BlockSpec((V7X_SUBLANES, blk), lambda j: (0, j)))


def _s5_fwd(u, wb, wc, tab, *, name):
    t = u.shape[0]
    sp = _s5_specs(t)

    def body(u_ref, wb_ref, wc_ref, tab_ref, hs_ref, y_ref):
        hs_ref[...] = jnp.dot(u_ref[...], wb_ref[...], preferred_element_type=F32)
        _scan_in_place(hs_ref, tab_ref, False)
        y = jnp.dot(hs_ref[...].astype(BF16), wc_ref[...], preferred_element_type=F32)
        first = pl.program_id(0) % _S5_SUB == 0

        @pl.when(first)
        def _():
            y_ref[...] = y

        @pl.when(jnp.logical_not(first))
        def _():
            y_ref[...] += y

    return pl.pallas_call(
        body, name=name, grid=(2 * S5_CH // SCAN_BLOCK,),
        in_specs=[sp['u'], sp['wb'], sp['wc'], sp['tab']], out_specs=[sp['hs'], sp['u']],
        out_shape=[jax.ShapeDtypeStruct((t, 2 * S5_CH), F32), jax.ShapeDtypeStruct((t, S5_WIDTH), F32)],
        compiler_params=_cparams("arbitrary"),
    )(u, wb, wc, tab)


def _s5_bwd(dy, u, hs, wb, wc, tab, *, name):
    t = u.shape[0]
    sp = _s5_specs(t)
    nt, tn = (((1,), (1,)), ((), ())), (((0,), (0,)), ((), ()))

    def body(dy_ref, u_ref, hs_ref, wb_ref, wc_ref, tab_ref, du_ref, dwb_ref, dwc_ref, da_ref, g_ref):
        g_ref[...] = lax.dot_general(dy_ref[...], wc_ref[...], nt, preferred_element_type=F32)
        dwc_ref[...] = lax.dot_general(hs_ref[...].astype(BF16), dy_ref[...], tn, preferred_element_type=F32)
        _scan_in_place(g_ref, tab_ref, True, hs_ref, da_ref)
        gb = g_ref[...].astype(BF16)
        dwb_ref[...] = lax.dot_general(u_ref[...], gb, tn, preferred_element_type=F32)
        du = lax.dot_general(gb, wb_ref[...], nt, preferred_element_type=F32)
        first = pl.program_id(0) % _S5_SUB == 0

        @pl.when(first)
        def _():
            du_ref[...] = du

        @pl.when(jnp.logical_not(first))
        def _():
            du_ref[...] += du

    return pl.pallas_call(
        body, name=name, grid=(2 * S5_CH // SCAN_BLOCK,),
        in_specs=[sp['u'], sp['u'], sp['hs'], sp['wb'], sp['wc'], sp['tab']],
        out_specs=[sp['u'], sp['wb'], sp['wc'], sp['tab']],
        out_shape=[jax.ShapeDtypeStruct((t, S5_WIDTH), F32), jax.ShapeDtypeStruct(wb.shape, F32),
                   jax.ShapeDtypeStruct(wc.shape, F32), jax.ShapeDtypeStruct((V7X_SUBLANES, 2 * S5_CH), F32)],
        scratch_shapes=[pltpu.VMEM((t, SCAN_BLOCK), F32)],
        compiler_params=_cparams("arbitrary"),
    )(dy, u, hs, wb, wc, tab)


ATT_SCALE = (QK_NOPE + QK_ROPE) ** -0.5
Q_SCALE = ATT_SCALE * math.log2(math.e)
LN2 = math.log(2.0)


def _att_mask(qi, kj, tb):
    qc = (qi * tb + lax.broadcasted_iota(jnp.int32, (tb, tb), 0)) // CHUNK
    kc = (kj * tb + lax.broadcasted_iota(jnp.int32, (tb, tb), 1)) // CHUNK
    return kc <= qc


def _with_side(side, n_main_in, n_main_out, refs, step, nsteps, compute):
    if side is None:
        compute(refs)
        return
    n = side.n
    main = refs[:n_main_in] + refs[n_main_in + n:n_main_in + n + n_main_out] + refs[n_main_in + 2 * n + n_main_out + 3:]
    x_refs = refs[n_main_in:n_main_in + n]
    y_refs = refs[n_main_in + n + n_main_out:n_main_in + 2 * n + n_main_out]
    side_refs = (x_refs, y_refs) + tuple(refs[n_main_in + 2 * n + n_main_out:n_main_in + 2 * n + n_main_out + 3])
    pl.when(step == 0)(functools.partial(side.start, *side_refs))
    compute(main)
    pl.when(step == (7 * nsteps) // 8)(functools.partial(side.relay, *side_refs))
    pl.when(step == nsteps - 1)(functools.partial(side.finish, *side_refs))


def _side_call(body, side, *, name, grid, in_specs, out_specs, out_shape, scratch, args, sem):
    n_out = len(out_shape)
    if side is not None:
        in_specs, args = in_specs + side.specs, list(args) + side.xs
        out_specs, out_shape = out_specs + side.specs, out_shape + side.out_shape
        scratch = side.scratch + scratch
        params = pltpu.CompilerParams(dimension_semantics=("arbitrary",) * len(grid), vmem_limit_bytes=VMEM_LIMIT,
                                      has_side_effects=True)
    else:
        params = _cparams(*sem)
    res = pl.pallas_call(body, name=name, grid=grid, in_specs=in_specs, out_specs=out_specs, out_shape=out_shape,
                         scratch_shapes=scratch, compiler_params=params)(*args)
    return res[:n_out], res[n_out:]


HEAD_PAIRS = MLA_HEADS // 2


def _attn_fwd(q, k, v, *, name, side=None):
    t = q.shape[0]
    hw = V7X_LANES
    tb = min(ATT_BLOCK, t)
    nblk = t // tb
    nt = (((1,), (1,)), ((), ()))

    def compute(refs):
        q_ref, k_ref, v_ref, o_ref, lse_ref = refs
        i = pl.program_id(1)
        qs = [q_ref[:, a * hw:(a + 1) * hw] for a in range(2)]

        def kv_step(j, carry, masked):
            r0 = pl.multiple_of(j * tb, tb)
            vb = v_ref[pl.ds(r0, tb), :]
            out = []
            for a in range(2):
                m, l, acc = carry[a]
                s = lax.dot_general(qs[a], k_ref[pl.ds(r0, tb), a * hw:(a + 1) * hw], nt, preferred_element_type=F32)
                if masked:
                    s = jnp.where(_att_mask(i, j, tb), s, NEG_INF)
                m_new = jnp.maximum(m, jnp.max(s, axis=1, keepdims=True))
                alpha = jnp.exp2(m - m_new)
                p = jnp.exp2(s - m_new)
                l = alpha * l + jnp.sum(p, axis=1, keepdims=True)
                out.append((m_new, l, alpha * acc + jnp.dot(p.astype(BF16), vb, preferred_element_type=F32)))
            return out

        init = [(jnp.full((tb, 1), NEG_INF, F32), jnp.zeros((tb, 1), F32), jnp.zeros((tb, hw), F32)) for _ in range(2)]
        carry = lax.fori_loop(0, i, functools.partial(kv_step, masked=False), init)
        (m0, l0, acc0), (m1, l1, acc1) = kv_step(i, carry, True)
        first = lax.broadcasted_iota(jnp.int32, (tb, hw), 1) < V_HEAD
        o_ref[...] = jnp.where(first, acc0 / l0, acc1 / l1)
        lse_ref[0] = m0 + jnp.log2(l0)
        lse_ref[1] = m1 + jnp.log2(l1)

    def body(*refs):
        _with_side(side, 3, 2, refs, pl.program_id(0) * nblk + pl.program_id(1), HEAD_PAIRS * nblk, compute)

    return _side_call(
        body, side, name=name, grid=(HEAD_PAIRS, nblk),
        in_specs=[pl.BlockSpec((tb, 2 * hw), lambda hp, i: (i, hp)), pl.BlockSpec((t, 2 * hw), lambda hp, i: (0, hp)),
                  pl.BlockSpec((t, hw), lambda hp, i: (0, hp))],
        out_specs=[pl.BlockSpec((tb, hw), lambda hp, i: (i, hp)), pl.BlockSpec((2, tb, 1), lambda hp, i: (hp, i, 0))],
        out_shape=[jax.ShapeDtypeStruct((t, MLA_HEADS * V_HEAD), F32), jax.ShapeDtypeStruct((MLA_HEADS, t, 1), F32)],
        scratch=[], args=[q, k, v], sem=("parallel", "parallel"))


def _attn_bwd(q, k, v, o, lse, do, *, name, side=None):
    t = q.shape[0]
    hw = V7X_LANES
    tb = min(ATT_BLOCK, t)
    nblk = t // tb
    nt = (((1,), (1,)), ((), ()))
    tn = (((0,), (0,)), ((), ()))

    def compute(refs):
        q_ref, k_ref, v_ref, o_ref, lse_ref, do_ref, dq_ref, dk_ref, dv_ref, delta_ref = refs
        j = pl.program_id(1)
        first = lax.broadcasted_iota(jnp.int32, (tb, hw), 1) < V_HEAD
        mine = [first, jnp.logical_not(first)]

        @pl.when(j == 0)
        def _():
            dq_ref[...] = jnp.zeros_like(dq_ref)

            def dstep(i, c):
                r0 = pl.multiple_of(i * tb, tb)
                prod = do_ref[pl.ds(r0, tb), :] * o_ref[pl.ds(r0, tb), :]
                for a in range(2):
                    delta_ref[a, pl.ds(r0, tb), :] = jnp.sum(jnp.where(mine[a], prod, 0.0), axis=1, keepdims=True)
                return c

            lax.fori_loop(0, nblk, dstep, 0)

        kb, vb = k_ref[...], v_ref[...]

        def q_step(i, carry, masked):
            dks, dv = carry
            r0 = pl.multiple_of(i * tb, tb)
            dob = do_ref[pl.ds(r0, tb), :].astype(BF16)
            new_dks = []
            for a in range(2):
                qa, ka = q_ref[pl.ds(r0, tb), a * hw:(a + 1) * hw], kb[:, a * hw:(a + 1) * hw]
                s = lax.dot_general(qa, ka, nt, preferred_element_type=F32)
                if masked:
                    s = jnp.where(_att_mask(i, j, tb), s, NEG_INF)
                p = jnp.exp2(s - lse_ref[a, pl.ds(r0, tb), :])
                doa = jnp.where(mine[a], dob, jnp.zeros_like(dob))
                dv = dv + lax.dot_general(p.astype(BF16), doa, tn, preferred_element_type=F32)
                dp = lax.dot_general(doa, vb, nt, preferred_element_type=F32)
                ds = (p * (dp - delta_ref[a, pl.ds(r0, tb), :]) * LN2).astype(BF16)
                new_dks.append(dks[a] + lax.dot_general(ds, qa, tn, preferred_element_type=F32))
                dq_ref[pl.ds(r0, tb), a * hw:(a + 1) * hw] += jnp.dot(ds, ka, preferred_element_type=F32)
            return new_dks, dv

        zero = jnp.zeros((tb, hw), F32)
        carry = q_step(j, ([zero, zero], zero), True)
        dks, dv = lax.fori_loop(j + 1, nblk, functools.partial(q_step, masked=False), carry)
        for a in range(2):
            dk_ref[:, a * hw:(a + 1) * hw] = dks[a]
        dv_ref[...] = dv

    def body(*refs):
        _with_side(side, 6, 3, refs, pl.program_id(0) * nblk + pl.program_id(1), HEAD_PAIRS * nblk, compute)

    whole = lambda w: pl.BlockSpec((t, w), lambda hp, j: (0, hp))
    blockj = lambda w: pl.BlockSpec((tb, w), lambda hp, j: (j, hp))
    return _side_call(
        body, side, name=name, grid=(HEAD_PAIRS, nblk),
        in_specs=[whole(2 * hw), blockj(2 * hw), blockj(hw), whole(hw), pl.BlockSpec((2, t, 1), lambda hp, j: (hp, 0, 0)),
                  whole(hw)],
        out_specs=[whole(2 * hw), blockj(2 * hw), blockj(hw)],
        out_shape=[jax.ShapeDtypeStruct(q.shape, F32), jax.ShapeDtypeStruct(k.shape, F32), jax.ShapeDtypeStruct(v.shape, F32)],
        scratch=[pltpu.VMEM((2, t, 1), F32)], args=[q, k, v, o, lse, do], sem=("parallel", "arbitrary"))


class _Exchange:
    def __init__(self, xs, gather, two_level=False):
        assert gather or not two_level
        self.xs, self.gather, self.n, self.two_level = list(xs), gather, len(xs), two_level
        shapes = [tuple(x.shape) if gather else tuple(x.shape[1:]) for x in xs]
        self.out_shape = [jax.ShapeDtypeStruct((N_DEV,) + shp, x.dtype) for shp, x in zip(shapes, xs)]
        self.specs = [pl.BlockSpec(memory_space=pl.ANY)] * self.n
        self.scratch = [pltpu.SemaphoreType.DMA((self.n, N_DEV - 1)), pltpu.SemaphoreType.DMA((self.n, N_DEV - 1)),
                        pltpu.SemaphoreType.DMA((self.n,))]

    def copies(self, x_refs, y_refs, send_sems, recv_sems, local_sems):
        mx, my, mc = lax.axis_index("x"), lax.axis_index("y"), lax.axis_index("c")
        me = 4 * mx + 2 * my + mc
        out = [pltpu.make_async_copy(x_refs[i] if self.gather else x_refs[i].at[me], y_refs[i].at[me], local_sems.at[i])
               for i in range(self.n)]
        for k in range(1, N_DEV):
            px = 1 - mx if k & 4 else mx
            py = 1 - my if k & 2 else my
            pc = 1 - mc if k & 1 else mc
            for i in range(self.n):
                out.append(pltpu.make_async_remote_copy(
                    src_ref=x_refs[i] if self.gather else x_refs[i].at[4 * px + 2 * py + pc], dst_ref=y_refs[i].at[me],
                    send_sem=send_sems.at[i, k - 1], recv_sem=recv_sems.at[i, k - 1],
                    device_id=(px, py, pc), device_id_type=pl.DeviceIdType.MESH))
        return out


    def _two_level(self, x_refs, y_refs, send_sems, recv_sems, local_sems):
        mx, my, mc = lax.axis_index("x"), lax.axis_index("y"), lax.axis_index("c")
        sib = (mx, my, 1 - mc)
        chips = [(1 - mx, my), (mx, 1 - my), (1 - mx, 1 - my)]
        idx = lambda px, py, pc: 4 * px + 2 * py + pc
        me = idx(mx, my, mc)

        def rc(i, k, src, block, to):
            return pltpu.make_async_remote_copy(
                src_ref=src, dst_ref=y_refs[i].at[block], send_sem=send_sems.at[i, k], recv_sem=recv_sems.at[i, k],
                device_id=to, device_id_type=pl.DeviceIdType.MESH)

        rng = range(self.n)
        over_ici = [(j, chip, i) for j, chip in enumerate(chips) for i in rng]
        return dict(
            local=lambda: [pltpu.make_async_copy(x_refs[i], y_refs[i].at[me], local_sems.at[i]) for i in rng],
            own=lambda: [rc(i, 0, x_refs[i], me, sib) for i in rng]
            + [rc(i, 1 + j, x_refs[i], me, (*chip, mc)) for j, chip, i in over_ici],
            relay=lambda: [rc(i, 4 + j, y_refs[i].at[idx(*chip, mc)], idx(*chip, mc), sib) for j, chip, i in over_ici],
            landed=lambda: [rc(i, 1 + j, x_refs[i], idx(*chip, mc), sib) for j, chip, i in over_ici],
            last=lambda: [rc(i, 0, x_refs[i], idx(*sib), sib) for i in rng]
            + [rc(i, 4 + j, x_refs[i], idx(*chip, 1 - mc), sib) for j, chip, i in over_ici])

    def start(self, *refs):
        if not self.two_level:
            for cp in self.copies(*refs):
                cp.start()
            return
        plan = self._two_level(*refs)
        for cp in plan['local']() + plan['own']():
            cp.start()

    def relay(self, *refs):
        if not self.two_level:
            return
        plan = self._two_level(*refs)
        for arrived, cp in zip(plan['landed'](), plan['relay']()):
            arrived.wait_recv()
            cp.start()

    def finish(self, *refs):
        if not self.two_level:
            for cp in self.copies(*refs):
                cp.wait()
            return
        plan = self._two_level(*refs)
        for cp in plan['last']():
            cp.wait_recv()
        for cp in plan['own']() + plan['relay']():
            cp.wait_send()
        for cp in plan['local']():
            cp.wait()


def _exchange(xs, *, gather, name, two_level=False):
    ex = _Exchange(xs, gather, two_level)
    n = ex.n

    def body(*refs):
        refs = (refs[:n], refs[n:2 * n]) + tuple(refs[2 * n:])
        ex.start(*refs)
        ex.relay(*refs)
        ex.finish(*refs)

    return pl.pallas_call(
        body, name=name, out_shape=ex.out_shape, in_specs=ex.specs, out_specs=ex.specs, scratch_shapes=ex.scratch,
        compiler_params=pltpu.CompilerParams(has_side_effects=True),
    )(*ex.xs)


def _adamw(w, gs, m, v, *, name, tm=256):
    nl = len(gs)
    parts = gs[0].ndim == 3
    c = w.shape[1]
    r = w.shape[0] // nl
    tm = _pick_rows(r, tm)
    nrow = r // tm

    def body(*refs):
        w_ref, g_refs, (m_ref, v_ref, go_ref, d_ref, mo_ref, vo_ref) = refs[0], refs[1:1 + nl], refs[1 + nl:]

        def update(g_ref):
            if parts:
                gv = g_ref[0].astype(F32)
                for k in range(1, N_DEV):
                    gv = gv + g_ref[k].astype(F32)
            else:
                gv = g_ref[...]
            mn = ADAM_B1 * m_ref[...] + (1.0 - ADAM_B1) * gv
            vn = ADAM_B2 * v_ref[...] + (1.0 - ADAM_B2) * jnp.square(gv)
            m_hat = mn / (1.0 - ADAM_B1 ** ADAM_STEP)
            v_hat = vn / (1.0 - ADAM_B2 ** ADAM_STEP)
            go_ref[...] = gv
            d_ref[...] = -ADAM_LR * (m_hat / (jnp.sqrt(v_hat) + ADAM_EPS) + ADAM_WD * w_ref[...])
            mo_ref[...] = mn
            vo_ref[...] = vn

        if nl == 1:
            update(g_refs[0])
        else:
            for layer, g_ref in enumerate(g_refs):
                pl.when(pl.program_id(0) == layer)(functools.partial(update, g_ref))

    spec = pl.BlockSpec((tm, c), lambda l, i: (l * nrow + i, 0))

    def gspec(layer):
        row = lambda l, i: jnp.where(l == layer, i, 0)
        if parts:
            return pl.BlockSpec((N_DEV, tm, c), lambda l, i: (0, row(l, i), 0))
        return pl.BlockSpec((tm, c), lambda l, i: (row(l, i), 0))

    return pl.pallas_call(
        body, name=name, grid=(nl, nrow), in_specs=[spec] + [gspec(k) for k in range(nl)] + [spec, spec],
        out_specs=[spec] * 4, out_shape=[jax.ShapeDtypeStruct(w.shape, F32)] * 4,
        compiler_params=_cparams("arbitrary", "arbitrary"),
    )(w, *gs, m, v)


def _sum_parts(x, *, name):
    def body(x_ref, o_ref):
        acc = x_ref[0]
        for k in range(1, N_DEV):
            acc = acc + x_ref[k]
        o_ref[...] = acc

    return pl.pallas_call(body, name=name, out_shape=jax.ShapeDtypeStruct(x.shape[1:], x.dtype))(x)


def _permute_in(a):
    pad = jnp.zeros(a.shape[:-1] + (IN_PAD - IN_WIDTH,), a.dtype)
    return jnp.concatenate([a[..., lo:hi] for lo, hi in _IN_SEGMENTS] + [pad], axis=-1)


def _unpermute_in(a):
    out, pos = {}, 0
    for lo, hi in _IN_SEGMENTS:
        out[lo] = a[..., pos:pos + hi - lo]
        pos += hi - lo
    return jnp.concatenate([out[lo] for lo in sorted(out)], axis=-1)


def _full_from_gathered(g, kind):
    if kind == 'row':
        return g.reshape((-1,) + g.shape[2:])
    if g.shape[-1] % V7X_LANES == 0:
        return jnp.moveaxis(g, 0, -2).reshape(g.shape[1:-1] + (-1,))
    return jnp.concatenate([g[d] for d in range(N_DEV)], axis=-1)


def _contrib_from_full(g, kind):
    if kind == 'row':
        return g.reshape((N_DEV, -1) + g.shape[1:])
    ns = g.shape[-1] // N_DEV
    if ns % V7X_LANES == 0:
        return jnp.moveaxis(g.reshape(g.shape[:-1] + (N_DEV, ns)), -2, 0)
    return jnp.stack([g[..., d * ns:(d + 1) * ns] for d in range(N_DEV)])


def _in_runs(ns):
    runs, pos = [], 0
    for lo, hi in _IN_SEGMENTS:
        for d in range(lo // ns, (hi - 1) // ns + 1):
            a, b = max(lo, d * ns), min(hi, (d + 1) * ns)
            runs.append((d, a - d * ns, b - d * ns, pos))
            pos += b - a
    return runs


def _w_in_from_gathered(g):
    pieces = [g[d][:, a:b] for d, a, b, _ in _in_runs(g.shape[2])]
    pad = jnp.zeros((g.shape[1], IN_PAD - IN_WIDTH), g.dtype)
    return jnp.concatenate(pieces + [pad], axis=1)


def _w_in_contrib(gp):
    ns = IN_WIDTH // N_DEV
    per_dev = [[] for _ in range(N_DEV)]
    for d, a, b, pos in sorted(_in_runs(ns), key=lambda r: (r[0], r[1])):
        per_dev[d].append(gp[:, pos:pos + b - a])
    return jnp.stack([jnp.concatenate(p, axis=1) for p in per_dev])


def _as2d(a, lead=0):
    return a.reshape(a.shape[:lead] + (-1, a.shape[-1]))


def _chan_cols(re, im):
    lead = re.shape[:-1]
    nb = S5_CH // SCAN_LANES
    return jnp.stack([re.reshape(lead + (nb, SCAN_LANES)), im.reshape(lead + (nb, SCAN_LANES))],
                     axis=-2).reshape(lead + (2 * S5_CH,))


def _s5_tables(lam_re, lam_im, log_dt, b_re, b_im, c_re, c_im):
    dt = jnp.exp(log_dt)[:, None]
    mag = jnp.exp(lam_re * dt)
    a_re = mag * jnp.cos(lam_im * dt)
    a_im = mag * jnp.sin(lam_im * dt)
    den = lam_re * lam_re + lam_im * lam_im
    f_re = ((a_re - 1.0) * lam_re + a_im * lam_im) / den
    f_im = (a_im * lam_re - (a_re - 1.0) * lam_im) / den
    bb_re = f_re[..., None] * b_re - f_im[..., None] * b_im
    bb_im = f_re[..., None] * b_im + f_im[..., None] * b_re
    gb = S5_GROUPS // S5_BD
    eye = jnp.eye(gb, dtype=F32)
    blocks = lambda a: a.reshape((S5_BD, gb) + a.shape[1:])

    def cols(re, im):
        shp = (S5_BD, S5_WIDTH // S5_BD, -1, SCAN_LANES)
        return jnp.stack([re.reshape(shp), im.reshape(shp)], axis=-2).reshape(S5_BD, S5_WIDTH // S5_BD, -1)

    flat = lambda a: a.reshape(S5_BD, S5_WIDTH // S5_BD, -1)
    wb_c = cols(flat(jnp.einsum('kgpc,gh->kgchp', blocks(bb_re), eye)), flat(jnp.einsum('kgpc,gh->kgchp', blocks(bb_im), eye)))
    wc_c = cols(flat(jnp.einsum('kgcp,gh->khcgp', blocks(c_re), eye)), -flat(jnp.einsum('kgcp,gh->khcgp', blocks(c_im), eye)))
    a_row = _chan_cols(a_re.reshape(1, S5_CH), a_im.reshape(1, S5_CH))
    return wb_c, wc_c.transpose(0, 2, 1), a_row


def _scan_tables(a_row, conj, seg_len):
    nb = S5_CH // SCAN_LANES
    a = a_row.reshape(nb, 2, SCAN_LANES)
    base = (a[:, 0], -a[:, 1] if conj else a[:, 1])
    mul = lambda x, y: (x[0] * y[0] - x[1] * y[1], x[0] * y[1] + x[1] * y[0])
    seg, sq, e = None, base, seg_len
    while e:
        if e & 1:
            seg = sq if seg is None else mul(seg, sq)
        sq, e = mul(sq, sq), e >> 1
    seg2 = mul(seg, seg)
    rows = [base, seg, seg2, mul(seg2, seg2)]
    lay = lambda z: jnp.stack([z[0], z[1]], axis=1).reshape(-1)
    return jnp.stack([lay(z) for z in rows] + [jnp.zeros((2 * S5_CH,), F32)] * (V7X_SUBLANES - len(rows)))


def _rope_tables(t):
    half = QK_ROPE // 2
    inv_freq = 1.0 / (ROPE_THETA ** (jnp.arange(0, QK_ROPE, 2, dtype=F32) / QK_ROPE))
    ang = jnp.arange(t, dtype=F32)[:, None] * inv_freq[None, :]
    cos, sin = jnp.cos(ang), jnp.sin(ang)
    zero = jnp.zeros_like(sin)

    def lay(nope, width, first, second, pad=0):
        head = jnp.concatenate([jnp.full((t, nope), 1.0 if first is cos else 0.0, F32), first, second,
                                jnp.zeros((t, pad), F32)], axis=1)
        reps = width // head.shape[1]
        out = jnp.tile(head, (1, reps))
        return jnp.pad(out, ((0, 0), (0, width - out.shape[1])))

    hq, pad = MLA_HEADS * V7X_LANES, V7X_LANES - QK_NOPE - QK_ROPE
    q_tabs = (lay(QK_NOPE, hq, cos, cos, pad), lay(QK_NOPE, hq, -sin, zero, pad), lay(QK_NOPE, hq, zero, sin, pad))
    k_tabs = (lay(0, V7X_LANES, cos, cos)[:, :V7X_LANES] * (jnp.arange(V7X_LANES) < QK_ROPE),
              lay(0, V7X_LANES, -sin, zero) * (jnp.arange(V7X_LANES) < QK_ROPE),
              lay(0, V7X_LANES, zero, sin) * (jnp.arange(V7X_LANES) < QK_ROPE))
    return q_tabs, k_tabs


def _sgu_tables(w_s, b_s):
    pos = jnp.arange(SGU_CHUNK) // CHUNK
    mask = pos[None, :] <= pos[:, None]
    wm = jnp.where(mask[None], w_s, 0.0).reshape(SGU_GROUPS * SGU_CHUNK, SGU_CHUNK)
    bias = jnp.repeat(b_s.T, SGU_WIDTH // SGU_GROUPS, axis=1)
    return wm, bias


def _row(v):
    return v.reshape(1, -1)


_S5_PARAMS = ('s5_lambda_re', 's5_lambda_im', 's5_log_dt', 's5_b_re', 's5_b_im', 's5_c_re', 's5_c_im')


def _derived_tables(p, t):
    (wb, wc, a_row), s5_pull = jax.vjp(jax.vmap(_s5_tables), *[p[n] for n in _S5_PARAMS])
    (wm, bias), sgu_pull = jax.vjp(jax.vmap(_sgu_tables), p['sgu_w_s'], p['sgu_b_s'])
    tab_f = jax.vmap(lambda a: _scan_tables(a, False, t // V7X_SUBLANES))(a_row)
    tab_b = jax.vmap(lambda a: _scan_tables(a, True, t // V7X_SUBLANES))(a_row)
    wb, wc = wb.astype(BF16), wc.astype(BF16)
    per_layer = [dict(s5_wb=wb[l], s5_wc=wc[l], s5_tab_fwd=tab_f[l], s5_tab_bwd=tab_b[l], sgu_wm=wm[l], sgu_bias=bias[l])
                 for l in range(len(wm))]

    def pull(grads):
        stacked = lambda k: jnp.stack([g[k] for g in grads])
        out = dict(zip(_S5_PARAMS, s5_pull((stacked('s5_wb'), stacked('s5_wc'), stacked('s5_a')))))
        out['sgu_w_s'], out['sgu_b_s'] = sgu_pull((stacked('sgu_wm'), stacked('sgu_bias')))
        return out

    return per_layer, pull


def _layer_fwd(x, ada, w, rope_tabs, tag, side=None, after_attn=None):
    s = {'x': x}
    q_tabs, k_tabs = rope_tabs
    sc1, gt1, sc2, gt2 = _row(1.0 + ada[1]), _row(1.0 + ada[2]), _row(1.0 + ada[4]), _row(1.0 + ada[5])
    s.update(sc1=sc1, gt1=gt1, sc2=sc2, gt2=gt2)
    (h,) = _rowcall(_modulate_fn, [x], [sc1, _row(ada[0])], [(D_MODEL, BF16)], tm=512, name=f"mod1_{tag}")
    proj = _mm(h, w['w_in_p'], bias=w['b_in_p'], name=f"proj_{tag}", tn=1792)
    s.update(h=h, proj=proj)

    u_view = (proj, S5_WIDTH, P_S5 // S5_WIDTH)
    u_seg = _seg_order(proj[:, P_S5:P_S5 + S5_WIDTH]).astype(BF16)
    hs, ylin = _s5_fwd(u_seg, w['s5_wb'], w['s5_wc'], w['s5_tab_fwd'], name=f"s5_fwd_{tag}")
    ylin = _time_order(ylin)
    s5_full = [_row(w['s5_d']), w['s5_w_glu'], _row(w['s5_b_glu'])]
    (y_s5,) = _rowcall(_s5_post_fn, [ylin, u_view], s5_full, [(S5_WIDTH, BF16)], tm=256, name=f"s5_post_{tag}")
    s.update(u_seg=u_seg, hs=hs, ylin=ylin, y_s5=y_s5)

    mla_rows = [(proj, MLA_BLK, P_MLA // MLA_BLK), *q_tabs, *k_tabs]
    mla_full = [_row(w['mla_q_norm']), w['mla_wq'], _row(w['mla_kv_norm']), w['mla_wk'], w['mla_wv']]
    hq, hv = MLA_HEADS * V7X_LANES, MLA_HEADS * V_HEAD
    q_r, k_r, v_r = _rowcall(_mla_pre_fn, mla_rows, mla_full, [(hq, BF16), (hq, BF16), (hv, BF16)],
                             tm=256, name=f"mla_pre_{tag}")
    (y_mla, lse), side_out = _attn_fwd(q_r, k_r, v_r, name=f"attn_fwd_{tag}", side=side)
    if after_attn is not None:
        after_attn(side_out)
    s.update(q_r=q_r, k_r=k_r, v_r=v_r, lse=lse, y_mla=y_mla, mla_full=mla_full)

    sgu_rows = [(proj, SGU_WIDTH, P_USGU // SGU_WIDTH), (proj, SGU_WIDTH, P_VSGU // SGU_WIDTH)]
    sgu_full = [_row(w['sgu_ln_g']), _row(w['sgu_ln_b']), w['sgu_wm'], w['sgu_bias']]
    (y_sgu,) = _rowcall(_sgu_fn, sgu_rows, sgu_full, [(SGU_WIDTH, BF16)], tm=SGU_CHUNK, name=f"sgu_{tag}")
    s.update(sgu_full=sgu_full, y_sgu=y_sgu)

    wbr = w['w_branch'].reshape(-1, D_MODEL)
    gate_rows = [(proj, D_MODEL, b) for b in range(3)]
    (merged,) = _rowcall(_merge_fn, [y_s5, y_mla, y_sgu] + gate_rows, [wbr], [(D_MODEL, BF16)], tm=256, name=f"merge_{tag}")
    ymix = _mm(merged, w['w_out'], name=f"wout_{tag}")
    (x1,) = _rowcall(_ln_res_fn, [x, ymix], [gt1, _row(w['ln1_g']), _row(w['ln1_b'])], [(D_MODEL, F32)], tm=256,
                     name=f"ln1_{tag}")
    s.update(merged=merged, ymix=ymix, x1=x1)

    (h2,) = _rowcall(_modulate_fn, [x1], [sc2, _row(ada[3])], [(D_MODEL, BF16)], tm=512, name=f"mod2_{tag}")
    ff_a, ff_b, act = _mm_swiglu(h2, w['ffn_w_in'], name=f"ffn_in_{tag}")
    f = _mm(act, w['ffn_w_out'], name=f"ffn_out_{tag}", tk=2816)
    (x2,) = _rowcall(_ln_res_fn, [x1, f], [gt2, _row(w['ln2_g']), _row(w['ln2_b'])], [(D_MODEL, F32)], tm=256,
                     name=f"ln2_{tag}")
    s.update(h2=h2, ff_a=ff_a, ff_b=ff_b, act=act, f=f)
    return x2, s, side_out


def _mod_bwd_fn(x, dh, dxa, scale_row):
    return (dxa + dh * scale_row, jnp.sum(dh * x, axis=0, keepdims=True), jnp.sum(dh, axis=0, keepdims=True))


def _layer_bwd(dx2, s, w, rope_tabs, tag, make_side=None):
    g = {}
    q_tabs, k_tabs = rope_tabs
    t = dx2.shape[0]
    ln_full = lambda gt, a, b: [gt, _row(w[a]), _row(w[b])]

    dx1_a, df, dgt2, g['ln2_g'], g['ln2_b'] = _rowcall_vjp(
        _ln_res_fn, [s['x1'], s['f']], ln_full(s['gt2'], 'ln2_g', 'ln2_b'), [dx2], [0, 1], [0, 1, 2],
        tm=256, name=f"ln2_bwd_{tag}", row_dtypes=[F32, BF16])
    dact = _mm(df, w['ffn_w_out'], tb=True, name=f"ffn_out_dx_{tag}", tn=1408)
    g['ffn_w_out'] = _mm(s['act'], df, ta=True, name=f"ffn_out_dw_{tag}", tm=1408)
    def swiglu_bwd_fn(a, b, d):
        _, pull = jax.vjp(_swiglu_fn, a, b)
        return (jnp.concatenate(pull((d,)), axis=1),)

    (dab,) = _rowcall(swiglu_bwd_fn, [s['ff_a'], s['ff_b'], dact], [], [(2 * FF_HIDDEN, BF16)],
                      tm=256, name=f"swiglu_bwd_{tag}")
    dh2 = _mm(dab, w['ffn_w_in'], tb=True, name=f"ffn_in_dx_{tag}", tk=1408)
    g['ffn_w_in'] = _mm(s['h2'], dab, ta=True, name=f"ffn_in_dw_{tag}", tn=1408)
    dx1, dsc2, dsh2 = _rowcall(_mod_bwd_fn, [s['x1'], dh2, dx1_a], [s['sc2']], [(D_MODEL, F32)],
                               [((1, D_MODEL), F32)] * 2, tm=256, name=f"mod2_bwd_{tag}")

    dx_a, dymix, dgt1, g['ln1_g'], g['ln1_b'] = _rowcall_vjp(
        _ln_res_fn, [s['x'], s['ymix']], ln_full(s['gt1'], 'ln1_g', 'ln1_b'), [dx1], [0, 1], [0, 1, 2],
        tm=256, name=f"ln1_bwd_{tag}", row_dtypes=[F32, BF16])
    dmerged = _mm(dymix, w['w_out'], tb=True, name=f"wout_dx_{tag}")
    g['w_out'] = _mm(s['merged'], dymix, ta=True, name=f"wout_dw_{tag}")

    proj = s['proj']
    wbr = w['w_branch'].reshape(-1, D_MODEL).astype(F32)
    gate_rows = [(proj, D_MODEL, b) for b in range(3)]
    dy_s5, dy_mla, dy_sgu, dl0, dl1, dl2, dwbr = _rowcall_vjp(
        _merge_fn, [s['y_s5'], s['y_mla'], s['y_sgu']] + gate_rows, [wbr], [dmerged], [0, 1, 2, 3, 4, 5], [0],
        tm=256, name=f"merge_bwd_{tag}")
    g['w_branch'] = dwbr.reshape(w['w_branch'].shape)

    sgu_rows = [(proj, SGU_WIDTH, P_USGU // SGU_WIDTH), (proj, SGU_WIDTH, P_VSGU // SGU_WIDTH)]
    du_sgu, dv_sgu, dlg, dlb, dwm, dbias = _rowcall_vjp(
        _sgu_fn, sgu_rows, s['sgu_full'], [dy_sgu], [0, 1], [0, 1, 2, 3], tm=SGU_CHUNK, name=f"sgu_bwd_{tag}")
    g['sgu_ln_g'], g['sgu_ln_b'] = dlg.reshape(-1), dlb.reshape(-1)
    g['sgu_wm'], g['sgu_bias'] = dwm, dbias

    (dq_r, dk_r, dv_r), side_out = _attn_bwd(s['q_r'], s['k_r'], s['v_r'], s['y_mla'], s['lse'], dy_mla,
                                             name=f"attn_bwd_{tag}", side=make_side(g) if make_side is not None else None)
    mla_rows = [(proj, MLA_BLK, P_MLA // MLA_BLK), *q_tabs, *k_tabs]
    dmla, dqn, dwq, dkvn, dwk, dwv = _rowcall_vjp(
        _mla_pre_fn, mla_rows, s['mla_full'], [dq_r, dk_r, dv_r], [0], [0, 1, 2, 3, 4], tm=256, name=f"mla_pre_bwd_{tag}")
    g['mla_w_q_up'], g['mla_w_kv_up'] = _mla_weight_grads(dwq, dwk, dwv)
    g['mla_q_norm'], g['mla_kv_norm'] = dqn.reshape(-1), dkvn.reshape(-1)

    s5_full = [_row(w['s5_d']), w['s5_w_glu'], _row(w['s5_b_glu'])]
    dylin, du_a, dd, g['s5_w_glu'], dbg = _rowcall_vjp(
        _s5_post_fn, [s['ylin'], (proj, S5_WIDTH, P_S5 // S5_WIDTH)], s5_full, [dy_s5], [0, 1], [0, 1, 2], tm=256,
        name=f"s5_post_bwd_{tag}", row_dtypes=[BF16, F32])
    g['s5_d'], g['s5_b_glu'] = dd.reshape(-1), dbg.reshape(-1)
    du_b, g['s5_wb'], g['s5_wc'], da_part = _s5_bwd(_seg_order(dylin), s['u_seg'], s['hs'], w['s5_wb'], w['s5_wc'],
                                                     w['s5_tab_bwd'], name=f"s5_bwd_{tag}")
    du_b = _time_order(du_b)
    g['s5_a'] = jnp.sum(da_part, axis=0, keepdims=True)

    def dproj_fn(g0, g1, g2, ua, ub, us, vs, ml):
        d = jnp.concatenate([g0, g1, g2, ua + ub, us, vs, ml], axis=1)
        return d, jnp.sum(d, axis=0, keepdims=True)

    dproj, db_in = _rowcall(dproj_fn, [dl0, dl1, dl2, du_a, du_b, du_sgu, dv_sgu, dmla], [], [(IN_PAD, BF16)],
                            [((1, IN_PAD), F32)], tm=256, name=f"dproj_{tag}")
    dh = _mm(dproj, w['w_in_p'], tb=True, name=f"proj_dx_{tag}", tk=1792)
    g['w_in_p'] = _mm(s['h'], dproj, ta=True, name=f"proj_dw_{tag}", tn=896)
    g['b_in'] = _unpermute_in(db_in).reshape(-1)
    dx, dsc1, dsh1 = _rowcall(_mod_bwd_fn, [s['x'], dh, dx_a], [s['sc1']], [(D_MODEL, F32)], [((1, D_MODEL), F32)] * 2,
                              tm=256, name=f"mod1_bwd_{tag}")
    d_ada = jnp.concatenate([dsh1, dsc1, dgt1, dsh2, dsc2, dgt2], axis=0)
    return dx, d_ada, g, side_out


def _loss_fn(y, target):
    err = y - target
    return (err / D_MODEL, 0.5 * jnp.sum(jnp.sum(err * err, axis=1, keepdims=True), axis=0, keepdims=True) / D_MODEL)


def _step(p):
    me = 4 * lax.axis_index("x") + 2 * lax.axis_index("y") + lax.axis_index("c")
    x = p['x'][0]
    t = x.shape[0]
    rope_tabs = _rope_tables(t)

    (c_all,) = _exchange([jnp.broadcast_to(p['c'], (V7X_SUBLANES, D_MODEL))], gather=True, name="gather_c")
    c_all = c_all[:, 0, :]
    (c_act,) = _rowcall(lambda cc: (cc * _sigmoid(cc),), [c_all], [], [(D_MODEL, F32)], tm=N_DEV, name="c_silu")
    ncol = p['w_ada'].shape[2]
    b_ada_loc = lax.dynamic_slice_in_dim(p['b_ada'], me * ncol, ncol, axis=1)
    ada_cols = jnp.concatenate([_mm(c_act, p['w_ada'][l], bias=b_ada_loc[l:l + 1], name=f"ada_{l}") for l in range(DEPTH)])
    (ada_all,) = _exchange([ada_cols], gather=True, name="gather_ada")
    ada_all = ada_all.reshape(N_DEV, DEPTH, N_DEV, ncol)
    ada = lax.dynamic_index_in_dim(ada_all, me, axis=2, keepdims=False)
    ada = ada.transpose(1, 0, 2).reshape(DEPTH, 6, D_MODEL)

    mixer_w, ffn_w = SHARDED[:-2], SHARDED[-2:]

    def shards(l, group):
        return [p[n][l].astype(BF16) for n, _ in group]

    def contribs(g, group):
        return [(_w_in_contrib(g['w_in_p']) if n == 'w_in' else _contrib_from_full(g[n], kind)).astype(BF16)
                for n, kind in group]

    tables, pull_tables = _derived_tables(p, t)

    def mixer_weights(l, gathered):
        w = {n: _full_from_gathered(g, kind) for (n, kind), g in zip(mixer_w[1:], gathered[1:])}
        w['w_in_p'] = _w_in_from_gathered(gathered[0])
        w['b_in_p'] = _row(_permute_in(p['b_in'][l]))
        w['mla_wq'], w['mla_wk'], w['mla_wv'] = _mla_weights(w['mla_w_q_up'], w['mla_w_kv_up'])
        w.update(tables[l])
        for n in SMALL:
            if n != 'b_ada' and n != 'b_in':
                w[n] = p[n][l]
        return w

    saved, layers = [], []
    gathered = _exchange(shards(0, mixer_w), gather=True, name="gather_w_0", two_level=True)
    for l in range(DEPTH):
        w = mixer_weights(l, gathered)
        side = _Exchange(shards(l, ffn_w) + (shards(l + 1, mixer_w) if l + 1 < DEPTH else []), True, two_level=True)

        def add_ffn(res, w=w):
            for (n, kind), g in zip(ffn_w, res):
                w[n] = _full_from_gathered(g, kind)

        x, s, res = _layer_fwd(x, ada[l], w, rope_tabs, f"l{l}", side, add_ffn)
        gathered = res[len(ffn_w):]
        layers.append(w)
        saved.append(s)
    dy, loss_loc = _rowcall(_loss_fn, [x, p['loss_target'][0]], [], [(D_MODEL, F32)], [((1, 1), F32)], tm=256, name="loss")
    loss = lax.psum(loss_loc[0, 0], ("x", "y", "c"))

    d_ada, grads, landed = [None] * DEPTH, [None] * DEPTH, [[None, None] for _ in range(DEPTH)]
    dx, pending = dy, []
    for l in reversed(range(DEPTH)):
        make_side = lambda g, pending=pending: _Exchange(contribs(g, ffn_w) + pending, False)
        dx, d_ada[l], grads[l], res = _layer_bwd(dx, saved[l], layers[l], rope_tabs, f"l{l}", make_side)
        landed[l][1] = res[:len(ffn_w)]
        if l + 1 < DEPTH:
            landed[l + 1][0] = res[len(ffn_w):]
        pending = contribs(grads[l], mixer_w)
    landed[0][0] = _exchange(pending, gather=False, name="scatter_g_0")
    landed = [list(a) + list(b) for a, b in landed]
    d_ada = jnp.stack(d_ada).reshape(DEPTH, 6 * D_MODEL)

    (d_ada_all,) = _exchange([d_ada], gather=True, name="gather_dada")
    d_ada_cols = lax.dynamic_slice_in_dim(d_ada_all, me * ncol, ncol, axis=2)
    pad_b = ((0, V7X_LANES - N_DEV), (0, 0))
    c_act_p = jnp.pad(c_act, pad_b)
    g_w_ada = jnp.stack([_mm(c_act_p, jnp.pad(d_ada_cols[:, l], pad_b), ta=True, name=f"ada_dw_{l}") for l in range(DEPTH)])

    out = {}
    kinds = ('grad_', 'delta_', 'new_m_', 'new_v_')
    for i, (n, _) in enumerate(SHARDED):
        res = _adamw(_as2d(p[n]), [_as2d(landed[l][i], lead=1) for l in range(DEPTH)], _as2d(p['m_' + n]),
                     _as2d(p['v_' + n]), name=f"adamw_{n}")
        for kind, r in zip(kinds, res):
            out[kind + n] = r.reshape(p[n].shape)

    res = _adamw(_as2d(p['w_ada']), [_as2d(g_w_ada)], _as2d(p['m_w_ada']), _as2d(p['v_w_ada']), name="adamw_ada")
    for kind, r in zip(kinds, res):
        out[kind + 'w_ada'] = r.reshape(p['w_ada'].shape)

    small_g = pull_tables(grads)
    small_g.update({n: jnp.stack([grads[l][n] for l in range(DEPTH)]) for n in SMALL if n not in small_g and n != 'b_ada'})
    small_g['b_ada'] = d_ada
    n_small = sum(int(np.prod(p[n].shape)) for n in SMALL)
    n_pad = -(-n_small // SMALL_PAD) * SMALL_PAD
    flat = jnp.concatenate([small_g[n].reshape(-1) for n in SMALL])
    flat = jnp.pad(flat, (0, n_pad - n_small)).reshape(N_DEV, n_pad // N_DEV // 1024, 1024)
    (landed_small,) = _exchange([flat], gather=False, name="scatter_small")
    (g_all,) = _exchange([_sum_parts(landed_small, name="sum_small")], gather=True, name="gather_small")
    g_all = g_all.reshape(-1)
    off = 0
    for n in SMALL:
        size = int(np.prod(p[n].shape))
        g_n = g_all[off:off + size].reshape(p[n].shape)
        off += size
        res = _adamw(_as2d(p[n]), [_as2d(g_n)], _as2d(p['m_' + n]), _as2d(p['v_' + n]), name=f"adamw_{n}")
        for kind, r in zip(kinds, res):
            out[kind + n] = r.reshape(p[n].shape)

    outs = [loss, dx[None]]
    for kind in ('grad_', 'delta_', 'new_m_', 'new_v_'):
        outs += [out[kind + n] for n in WNAMES]
    return tuple(outs)


def kernel(x, c, w_ada, b_ada, w_in, b_in, s5_lambda_re, s5_lambda_im, s5_log_dt, s5_b_re, s5_b_im, s5_c_re, s5_c_im, s5_d, s5_w_glu, s5_b_glu, mla_q_norm, mla_w_q_up, mla_kv_norm, mla_w_kv_up, sgu_ln_g, sgu_ln_b, sgu_w_s, sgu_b_s, w_branch, w_out, ln1_g, ln1_b, ffn_w_in, ffn_w_out, ln2_g, ln2_b, loss_target, m_w_ada, m_b_ada, m_w_in, m_b_in, m_s5_lambda_re, m_s5_lambda_im, m_s5_log_dt, m_s5_b_re, m_s5_b_im, m_s5_c_re, m_s5_c_im, m_s5_d, m_s5_w_glu, m_s5_b_glu, m_mla_q_norm, m_mla_w_q_up, m_mla_kv_norm, m_mla_w_kv_up, m_sgu_ln_g, m_sgu_ln_b, m_sgu_w_s, m_sgu_b_s, m_w_branch, m_w_out, m_ln1_g, m_ln1_b, m_ffn_w_in, m_ffn_w_out, m_ln2_g, m_ln2_b, v_w_ada, v_b_ada, v_w_in, v_b_in, v_s5_lambda_re, v_s5_lambda_im, v_s5_log_dt, v_s5_b_re, v_s5_b_im, v_s5_c_re, v_s5_c_im, v_s5_d, v_s5_w_glu, v_s5_b_glu, v_mla_q_norm, v_mla_w_q_up, v_mla_kv_norm, v_mla_w_kv_up, v_sgu_ln_g, v_sgu_ln_b, v_sgu_w_s, v_sgu_b_s, v_w_branch, v_w_out, v_ln1_g, v_ln1_b, v_ffn_w_in, v_ffn_w_out, v_ln2_g, v_ln2_b):
    return _step(dict(locals()))
```

```python
import functools
import math

import numpy as np
import jax
import jax.numpy as jnp
from jax import lax
from jax.experimental import pallas as pl
from jax.experimental.pallas import tpu as pltpu

F32 = jnp.float32
BF16 = jnp.bfloat16

N_DEV = 8
D_MODEL = 1024
DEPTH = 4
CHUNK = 64
S5_WIDTH = 512
S5_GROUP = 16
S5_GROUPS = 32
S5_STATE = 64
MLA_HEADS = 8
QK_NOPE = 64
QK_ROPE = 32
V_HEAD = 64
Q_LORA = 384
KV_LORA = 256
ROPE_THETA = 10000.0
SGU_WIDTH = 512
SGU_GROUPS = 4
SGU_CHUNK = 128
FF_HIDDEN = 2816
DEEPNORM_ALPHA = (2 * DEPTH) ** 0.25
LN_EPS = 1e-5
RMS_EPS = 1e-6
NEG_INF = -1e30
ADAM_LR = 0.001
ADAM_B1 = 0.9
ADAM_B2 = 0.999
ADAM_EPS = 1e-08
ADAM_WD = 0.01
ADAM_STEP = 10

IN_WIDTH = 5280
IN_PAD = 5376
_O_S5, _O_CQ, _O_CKV, _O_KPE, _O_USGU, _O_VSGU, _O_GATE = 0, 512, 896, 1152, 1184, 1696, 2208
_IN_SEGMENTS = ((_O_GATE, IN_WIDTH), (_O_S5, _O_CQ), (_O_USGU, _O_VSGU), (_O_VSGU, _O_GATE), (_O_CQ, _O_USGU))
P_GATE, P_S5, P_USGU, P_VSGU, P_MLA = 0, 3072, 3584, 4096, 4608
MLA_BLK = 768

V7X_LANES = 128
V7X_SUBLANES = 8
VMEM_LIMIT = 56 * 1024 * 1024
ATT_BLOCK = 512
SCAN_LANES = 128
S5_CH = S5_GROUPS * S5_STATE
S5_BD = 4

WNAMES = ['w_ada', 'b_ada', 'w_in', 'b_in', 's5_lambda_re', 's5_lambda_im', 's5_log_dt', 's5_b_re', 's5_b_im',
          's5_c_re', 's5_c_im', 's5_d', 's5_w_glu', 's5_b_glu', 'mla_q_norm', 'mla_w_q_up', 'mla_kv_norm',
          'mla_w_kv_up', 'sgu_ln_g', 'sgu_ln_b', 'sgu_w_s', 'sgu_b_s', 'w_branch', 'w_out', 'ln1_g', 'ln1_b',
          'ffn_w_in', 'ffn_w_out', 'ln2_g', 'ln2_b']
SHARDED = (('w_in', 'col'), ('s5_w_glu', 'row'), ('mla_w_q_up', 'col'), ('mla_w_kv_up', 'col'),
           ('w_branch', 'col3'), ('w_out', 'row'), ('ffn_w_in', 'col'), ('ffn_w_out', 'row'))
SMALL = [n for n in WNAMES if n != 'w_ada' and n not in dict(SHARDED)]
SMALL_PAD = N_DEV * V7X_SUBLANES * 1024


def _cparams(*sem):
    return pltpu.CompilerParams(dimension_semantics=sem, vmem_limit_bytes=VMEM_LIMIT)


def _pick(n, target):
    if n <= target:
        return n
    best = None
    for d in range(V7X_LANES, target + 1, V7X_LANES):
        if n % d == 0:
            best = d
    assert best is not None, (n, target)
    return best


def _pick_rows(n, target):
    if n <= target:
        return n
    best = None
    for d in range(V7X_SUBLANES, target + 1, V7X_SUBLANES):
        if n % d == 0:
            best = d
    assert best is not None, (n, target)
    return best


@jax.custom_vjp
def _bdot(a, b):
    return jnp.dot(a.astype(BF16), b.astype(BF16), preferred_element_type=F32)


def _bdot_fwd(a, b):
    return _bdot(a, b), (a, b)


def _bdot_bwd(res, g):
    a, b = res
    gb = g.astype(BF16)
    da = lax.dot_general(gb, b.astype(BF16), (((1,), (1,)), ((), ())), preferred_element_type=F32)
    db = lax.dot_general(a.astype(BF16), gb, (((0,), (0,)), ((), ())), preferred_element_type=F32)
    return da.astype(a.dtype), db.astype(b.dtype)


_bdot.defvjp(_bdot_fwd, _bdot_bwd)


@functools.partial(jax.custom_vjp, nondiff_argnums=(1,))
def _lane_roll(x, shift):
    return pltpu.roll(x, shift % x.shape[1], 1)


def _lane_roll_fwd(x, shift):
    return _lane_roll(x, shift), None


def _lane_roll_bwd(shift, _, g):
    return (_lane_roll(g, -shift),)


_lane_roll.defvjp(_lane_roll_fwd, _lane_roll_bwd)


def _sigmoid(x):
    return 1.0 / (1.0 + jnp.exp(-x))


def _gelu(x):
    return 0.5 * x * (1.0 + jnp.tanh(math.sqrt(2.0 / math.pi) * (x + 0.044715 * (x * x * x))))


def _layer_norm(x, g, b):
    mu = jnp.mean(x, axis=-1, keepdims=True)
    var = jnp.mean(jnp.square(x - mu), axis=-1, keepdims=True)
    return (x - mu) * lax.rsqrt(var + LN_EPS) * g + b


def _rms_norm(x, g):
    return x * lax.rsqrt(jnp.mean(x * x, axis=-1, keepdims=True) + RMS_EPS) * g


def _rope(x, c, s1, s2):
    half = QK_ROPE // 2
    return x * c + _lane_roll(x, -half) * s1 + _lane_roll(x, half) * s2


def _modulate_fn(x, scale_row, shift_row):
    return (x * scale_row + shift_row,)


def _ln_res_fn(x, y, gate_row, g, b):
    return (_layer_norm(DEEPNORM_ALPHA * x + gate_row * y, g, b),)


def _s5_post_fn(ylin, u, d, w_glu, b_glu):
    z = _gelu(ylin + d * u)
    return (z * _sigmoid(_bdot(z, w_glu) + b_glu),)


def _mla_pre_fn(blk, cq_t, sq1, sq2, ck_t, sk1, sk2, q_norm, w_q, kv_norm, w_k, w_v):
    cq, ckv, kpe = blk[:, :Q_LORA], blk[:, Q_LORA:Q_LORA + KV_LORA], blk[:, Q_LORA + KV_LORA:]
    q = _rope(_bdot(_rms_norm(cq, q_norm), w_q), cq_t, sq1, sq2) * Q_SCALE
    ckv_n = _rms_norm(ckv, kv_norm)
    kpe_r = _lane_roll(_rope(kpe, ck_t, sk1, sk2), QK_NOPE)
    k = _bdot(ckv_n, w_k) + jnp.concatenate([kpe_r] * MLA_HEADS, axis=1)
    return q, k, _bdot(ckv_n, w_v)


def _pad_heads(w, width):
    w3 = w.reshape(w.shape[0], MLA_HEADS, width)
    return jnp.pad(w3, ((0, 0), (0, 0), (0, V7X_LANES - width))).reshape(w.shape[0], MLA_HEADS * V7X_LANES)


def _mla_weights(w_q_up, w_kv_up):
    kv3 = w_kv_up.reshape(w_kv_up.shape[0], MLA_HEADS, QK_NOPE + V_HEAD)
    w_k = _pad_heads(kv3[:, :, :QK_NOPE].reshape(w_kv_up.shape[0], -1), QK_NOPE)
    return _pad_heads(w_q_up, QK_NOPE + QK_ROPE), w_k, kv3[:, :, QK_NOPE:].reshape(w_kv_up.shape[0], -1)


def _mla_weight_grads(dw_q, dw_k, dw_v):
    unpad = lambda a, width: a.reshape(a.shape[0], MLA_HEADS, V7X_LANES)[:, :, :width]
    dkv = jnp.concatenate([unpad(dw_k, QK_NOPE), dw_v.reshape(dw_v.shape[0], MLA_HEADS, V_HEAD)], axis=2)
    return unpad(dw_q, QK_NOPE + QK_ROPE).reshape(dw_q.shape[0], -1), dkv.reshape(dw_k.shape[0], -1)


def _sgu_fn(u, v, g, b, wm, bias):
    vn = _layer_norm(_gelu(v), g, b)
    w = SGU_CHUNK
    parts = [_bdot(wm[k * w:(k + 1) * w, :], vn[:, k * w:(k + 1) * w]) for k in range(SGU_GROUPS)]
    return (_gelu(u) * (jnp.concatenate(parts, axis=1) + bias),)


def _merge_fn(y0, y1, y2, l0, l1, l2, wb):
    n = S5_WIDTH
    return (_sigmoid(l0) * _bdot(y0, wb[:n]) + _sigmoid(l1) * _bdot(y1, wb[n:2 * n])
            + _sigmoid(l2) * _bdot(y2, wb[2 * n:]),)


def _swiglu_fn(a, b):
    return (a * _sigmoid(a) * b,)


def _rowcall(fn, rows, fulls, out_rows, out_reds=(), *, tm, name):
    rows = [r if isinstance(r, tuple) else (r, r.shape[1], 0) for r in rows]
    t = rows[0][0].shape[0]
    tm = _pick_rows(t, tm)
    n_in, n_or, n_red = len(rows) + len(fulls), len(out_rows), len(out_reds)

    def body(*refs):
        vals = fn(*[r[...] for r in refs[:n_in]])
        assert len(vals) == n_or + n_red, (name, len(vals))
        for ref, v in zip(refs[n_in:n_in + n_or], vals[:n_or]):
            ref[...] = v.astype(ref.dtype)
        if n_red:
            red_refs = refs[n_in + n_or:]

            @pl.when(pl.program_id(0) == 0)
            def _():
                for ref in red_refs:
                    ref[...] = jnp.zeros_like(ref)

            for ref, v in zip(red_refs, vals[n_or:]):
                ref[...] += v.astype(ref.dtype)

    in_specs = [pl.BlockSpec((tm, w), functools.partial(lambda i, blk: (i, blk), blk=blk)) for _, w, blk in rows]
    in_specs += [pl.BlockSpec(f.shape, lambda i: (0, 0)) for f in fulls]
    out_specs = [pl.BlockSpec((tm, c), lambda i: (i, 0)) for c, _ in out_rows]
    out_specs += [pl.BlockSpec(s, lambda i: (0, 0)) for s, _ in out_reds]
    out_shape = [jax.ShapeDtypeStruct((t, c), dt) for c, dt in out_rows]
    out_shape += [jax.ShapeDtypeStruct(s, dt) for s, dt in out_reds]
    return pl.pallas_call(
        body, name=name, grid=(t // tm,), in_specs=in_specs, out_specs=out_specs, out_shape=out_shape,
        compiler_params=_cparams("arbitrary" if n_red else "parallel"),
    )(*[r[0] for r in rows], *fulls)


def _rowcall_vjp(fn, rows, fulls, cots, diff_rows, diff_fulls, *, tm, name, row_dtypes=None):
    rows_n = [r if isinstance(r, tuple) else (r, r.shape[1], 0) for r in rows]
    n_r, n_c = len(rows), len(cots)
    row_dtypes = row_dtypes or [F32] * len(diff_rows)

    def fn2(*vals):
        r = [v.astype(F32) for v in vals[:n_r]]
        ct = vals[n_r:n_r + n_c]
        f = [v.astype(F32) for v in vals[n_r + n_c:]]

        def g(*dargs):
            rr, ff = list(r), list(f)
            for k, idx in enumerate(diff_rows):
                rr[idx] = dargs[k]
            for k, idx in enumerate(diff_fulls):
                ff[idx] = dargs[len(diff_rows) + k]
            return fn(*rr, *ff)

        prim = [r[i] for i in diff_rows] + [f[i] for i in diff_fulls]
        outs, pull = jax.vjp(g, *prim)
        return pull(tuple(c.astype(o.dtype) for c, o in zip(ct, outs)))

    out_rows = [(rows_n[i][1], dt) for i, dt in zip(diff_rows, row_dtypes)]
    out_reds = [(fulls[i].shape, F32) for i in diff_fulls]
    return _rowcall(fn2, list(rows) + list(cots), fulls, out_rows, out_reds, tm=tm, name=name)


def _mm(a, b, *, ta=False, tb=False, bias=None, out_dtype=F32, name, tm=1024, tn=1024, tk=1024):
    (k_a, m) = a.shape if ta else a.shape[::-1]
    (n, k_b) = b.shape if tb else b.shape[::-1]
    assert k_a == k_b, (name, a.shape, b.shape)
    tm, tn, tk = _pick(m, tm) if m % V7X_LANES == 0 else m, _pick(n, tn), _pick(k_a, tk) if k_a % V7X_LANES == 0 else k_a
    nk = k_a // tk
    a_spec = pl.BlockSpec((tk, tm), lambda i, j, k: (k, i)) if ta else pl.BlockSpec((tm, tk), lambda i, j, k: (i, k))
    b_spec = pl.BlockSpec((tn, tk), lambda i, j, k: (j, k)) if tb else pl.BlockSpec((tk, tn), lambda i, j, k: (k, j))
    dims = (((0,) if ta else (1,), (1,) if tb else (0,)), ((), ()))
    has_bias = bias is not None

    def body(*refs):
        a_ref, b_ref = refs[0], refs[1]
        part = lax.dot_general(a_ref[...].astype(BF16), b_ref[...].astype(BF16), dims, preferred_element_type=F32)
        if nk == 1:
            o_ref = refs[-1]
            o_ref[...] = (part + refs[2][...] if has_bias else part).astype(o_ref.dtype)
            return
        o_ref, acc_ref = refs[-2], refs[-1]
        k = pl.program_id(2)

        @pl.when(k == 0)
        def _():
            acc_ref[...] = part

        @pl.when(k > 0)
        def _():
            acc_ref[...] += part

        @pl.when(k == nk - 1)
        def _():
            r = acc_ref[...]
            if has_bias:
                r = r + refs[2][...]
            o_ref[...] = r.astype(o_ref.dtype)

    in_specs = [a_spec, b_spec] + ([pl.BlockSpec((1, tn), lambda i, j, k: (0, j))] if has_bias else [])
    return pl.pallas_call(
        body, name=name, grid=(m // tm, n // tn, nk), in_specs=in_specs,
        out_specs=pl.BlockSpec((tm, tn), lambda i, j, k: (i, j)),
        out_shape=jax.ShapeDtypeStruct((m, n), out_dtype),
        scratch_shapes=[pltpu.VMEM((tm, tn), F32)] if nk > 1 else [],
        compiler_params=_cparams("parallel", "parallel", "arbitrary"),
    )(a, b, *([bias] if has_bias else []))


def _mm_swiglu(x, w, *, name, tm=512, tn=1408):
    m, k = x.shape
    hdim = w.shape[1] // 2
    tm, tn = _pick(m, tm), _pick(hdim, tn)
    nj = hdim // tn

    def body(x_ref, wa_ref, wb_ref, a_ref, b_ref, act_ref):
        xb = x_ref[...].astype(BF16)
        a = jnp.dot(xb, wa_ref[...].astype(BF16), preferred_element_type=F32)
        b = jnp.dot(xb, wb_ref[...].astype(BF16), preferred_element_type=F32)
        a_ref[...] = a
        b_ref[...] = b
        act_ref[...] = _swiglu_fn(a, b)[0].astype(act_ref.dtype)

    tile = pl.BlockSpec((tm, tn), lambda i, j: (i, j))
    return pl.pallas_call(
        body, name=name, grid=(m // tm, nj),
        in_specs=[pl.BlockSpec((tm, k), lambda i, j: (i, 0)), pl.BlockSpec((k, tn), lambda i, j: (0, j)),
                  pl.BlockSpec((k, tn), lambda i, j: (0, j + nj))],
        out_specs=[tile, tile, tile],
        out_shape=[jax.ShapeDtypeStruct((m, hdim), F32), jax.ShapeDtypeStruct((m, hdim), F32),
                   jax.ShapeDtypeStruct((m, hdim), BF16)],
        compiler_params=_cparams("parallel", "parallel"),
    )(x, w, w)


def _seg_order(a):
    t, c = a.shape
    return a.reshape(V7X_SUBLANES, t // V7X_SUBLANES, c).transpose(1, 0, 2).reshape(t, c)


def _time_order(a):
    t, c = a.shape
    return a.reshape(t // V7X_SUBLANES, V7X_SUBLANES, c).transpose(1, 0, 2).reshape(t, c)


def _scan_in_place(o_ref, tab_ref, reverse, h_ref=None, da_ref=None):
    ntile = o_ref.shape[0] // V7X_SUBLANES
    with_da = h_ref is not None
    ln = SCAN_LANES
    rows8 = V7X_SUBLANES
    subs = range(o_ref.shape[1] // (2 * ln))
    col = lambda s, part: slice((2 * s + part) * ln, (2 * s + part + 1) * ln)
    cmul = lambda p, q: (p[0] * q[0] - p[1] * q[1], p[0] * q[1] + p[1] * q[0])
    cadd = lambda p, q: (p[0] + q[0], p[1] + q[1])

    def run():
        row = lax.broadcasted_iota(jnp.int32, (rows8, ln), 0)
        bcast = lambda k, s: tuple(jnp.broadcast_to(tab_ref[k:k + 1, col(s, part)], (rows8, ln)) for part in (0, 1))
        a = [bcast(0, s) for s in subs]
        zero = [(jnp.zeros((rows8, ln), F32), jnp.zeros((rows8, ln), F32)) for _ in subs]

        def tile(ref, i):
            r0 = pl.multiple_of(i * rows8, rows8)
            return [(ref[pl.ds(r0, rows8), col(s, 0)], ref[pl.ds(r0, rows8), col(s, 1)]) for s in subs]

        def put(i, ys):
            r0 = pl.multiple_of(i * rows8, rows8)
            for s, (yr, yi) in enumerate(ys):
                o_ref[pl.ds(r0, rows8), col(s, 0)] = yr
                o_ref[pl.ds(r0, rows8), col(s, 1)] = yi

        def shifted(z, step, up):
            shift, keep = (rows8 - step, row < rows8 - step) if up else (step, row >= step)
            return tuple(jnp.where(keep, pltpu.roll(v, shift, 0), 0.0) for v in z)

        def pass1(n, carry):
            i = (ntile - 1 - n) if reverse else n
            ys = [cadd(x, cmul(a[s], carry[s])) for s, x in enumerate(tile(o_ref, i))]
            put(i, ys)
            return ys

        ends = lax.fori_loop(0, ntile, pass1, zero, unroll=4)
        incoming = []
        for s in subs:
            f = ends[s]
            for k, step in zip((1, 2, 3), (1, 2, 4)):
                f = cadd(f, cmul(bcast(k, s), shifted(f, step, reverse)))
            incoming.append(shifted(f, 1, reverse))
        if with_da:
            h0 = [shifted(h, 1, False) for h in tile(h_ref, ntile - 1)]

        def pass2(n, carry):
            i = (ntile - 1 - n) if reverse else n
            power = carry[0]
            ys = [cadd(y, cmul(power[s], incoming[s])) for s, y in enumerate(tile(o_ref, i))]
            put(i, ys)
            new_power = [cmul(power[s], a[s]) for s in subs]
            if not with_da:
                return (new_power,)
            prev = tile(h_ref, jnp.maximum(i - 1, 0))
            acc = []
            for s in subs:
                hpr, hpi = jnp.where(i > 0, prev[s][0], h0[s][0]), jnp.where(i > 0, prev[s][1], h0[s][1])
                (yr, yi), (sr, si) = ys[s], carry[1][s]
                acc.append((sr + yr * hpr + yi * hpi, si + yi * hpr - yr * hpi))
            return (new_power, acc)

        out = lax.fori_loop(0, ntile, pass2, (list(a),) + ((zero,) if with_da else ()), unroll=4)
        if with_da:
            for s in subs:
                da_ref[:, col(s, 0)] = out[1][s][0]
                da_ref[:, col(s, 1)] = out[1][s][1]

    run()


SCAN_BLOCK = 4 * SCAN_LANES
_S5_SUB = 2 * S5_CH // S5_BD // SCAN_BLOCK


def _s5_specs(t):
    ku, blk = S5_WIDTH // S5_BD, SCAN_BLOCK
    return dict(
        u=pl.BlockSpec((t, ku), lambda j: (0, j // _S5_SUB)),
        hs=pl.BlockSpec((t, blk), lambda j: (0, j)),
        wb=pl.BlockSpec((None, ku, blk), lambda j: (j // _S5_SUB, 0, j % _S5_SUB)),
        wc=pl.BlockSpec((None, blk, ku), lambda j: (j // _S5_SUB, j % _S5_SUB, 0)),
        tab=pl.BlockSpec((V7X_SUBLANES, blk), lambda j: (0, j)))


def _s5_fwd(u, wb, wc, tab, *, name):
    t = u.shape[0]
    sp = _s5_specs(t)

    def body(u_ref, wb_ref, wc_ref, tab_ref, hs_ref, y_ref):
        hs_ref[...] = jnp.dot(u_ref[...], wb_ref[...], preferred_element_type=F32)
        _scan_in_place(hs_ref, tab_ref, False)
        y = jnp.dot(hs_ref[...].astype(BF16), wc_ref[...], preferred_element_type=F32)
        first = pl.program_id(0) % _S5_SUB == 0

        @pl.when(first)
        def _():
            y_ref[...] = y

        @pl.when(jnp.logical_not(first))
        def _():
            y_ref[...] += y

    return pl.pallas_call(
        body, name=name, grid=(2 * S5_CH // SCAN_BLOCK,),
        in_specs=[sp['u'], sp['wb'], sp['wc'], sp['tab']], out_specs=[sp['hs'], sp['u']],
        out_shape=[jax.ShapeDtypeStruct((t, 2 * S5_CH), F32), jax.ShapeDtypeStruct((t, S5_WIDTH), F32)],
        compiler_params=_cparams("arbitrary"),
    )(u, wb, wc, tab)


def _s5_bwd(dy, u, hs, wb, wc, tab, *, name, side=None):
    t = u.shape[0]
    sp = _s5_specs(t)
    nt, tn = (((1,), (1,)), ((), ())), (((0,), (0,)), ((), ()))
    nsteps = 2 * S5_CH // SCAN_BLOCK

    def compute(refs):
        dy_ref, u_ref, hs_ref, wb_ref, wc_ref, tab_ref, du_ref, dwb_ref, dwc_ref, da_ref, g_ref = refs
        g_ref[...] = lax.dot_general(dy_ref[...], wc_ref[...], nt, preferred_element_type=F32)
        dwc_ref[...] = lax.dot_general(hs_ref[...].astype(BF16), dy_ref[...], tn, preferred_element_type=F32)
        _scan_in_place(g_ref, tab_ref, True, hs_ref, da_ref)
        gb = g_ref[...].astype(BF16)
        dwb_ref[...] = lax.dot_general(u_ref[...], gb, tn, preferred_element_type=F32)
        du = lax.dot_general(gb, wb_ref[...], nt, preferred_element_type=F32)
        first = pl.program_id(0) % _S5_SUB == 0

        @pl.when(first)
        def _():
            du_ref[...] = du

        @pl.when(jnp.logical_not(first))
        def _():
            du_ref[...] += du

    def body(*refs):
        _with_side(side, 6, 4, refs, pl.program_id(0), nsteps, compute)

    return _side_call(
        body, side, name=name, grid=(nsteps,),
        in_specs=[sp['u'], sp['u'], sp['hs'], sp['wb'], sp['wc'], sp['tab']],
        out_specs=[sp['u'], sp['wb'], sp['wc'], sp['tab']],
        out_shape=[jax.ShapeDtypeStruct((t, S5_WIDTH), F32), jax.ShapeDtypeStruct(wb.shape, F32),
                   jax.ShapeDtypeStruct(wc.shape, F32), jax.ShapeDtypeStruct((V7X_SUBLANES, 2 * S5_CH), F32)],
        scratch=[pltpu.VMEM((t, SCAN_BLOCK), F32)], args=[dy, u, hs, wb, wc, tab], sem=("arbitrary",))


ATT_SCALE = (QK_NOPE + QK_ROPE) ** -0.5
Q_SCALE = ATT_SCALE * math.log2(math.e)
LN2 = math.log(2.0)


def _att_mask(qi, kj, tb):
    qc = (qi * tb + lax.broadcasted_iota(jnp.int32, (tb, tb), 0)) // CHUNK
    kc = (kj * tb + lax.broadcasted_iota(jnp.int32, (tb, tb), 1)) // CHUNK
    return kc <= qc


def _with_side(side, n_main_in, n_main_out, refs, step, nsteps, compute):
    if side is None:
        compute(refs)
        return
    n = side.n
    main = refs[:n_main_in] + refs[n_main_in + n:n_main_in + n + n_main_out] + refs[n_main_in + 2 * n + n_main_out + 3:]
    x_refs = refs[n_main_in:n_main_in + n]
    y_refs = refs[n_main_in + n + n_main_out:n_main_in + 2 * n + n_main_out]
    side_refs = (x_refs, y_refs) + tuple(refs[n_main_in + 2 * n + n_main_out:n_main_in + 2 * n + n_main_out + 3])
    pl.when(step == 0)(functools.partial(side.start, *side_refs))
    compute(main)
    pl.when(step == (7 * nsteps) // 8)(functools.partial(side.relay, *side_refs))
    pl.when(step == nsteps - 1)(functools.partial(side.finish, *side_refs))


def _side_call(body, side, *, name, grid, in_specs, out_specs, out_shape, scratch, args, sem):
    n_out = len(out_shape)
    if side is not None:
        in_specs, args = in_specs + side.specs, list(args) + side.xs
        out_specs, out_shape = out_specs + side.specs, out_shape + side.out_shape
        scratch = side.scratch + scratch
        params = pltpu.CompilerParams(dimension_semantics=("arbitrary",) * len(grid), vmem_limit_bytes=VMEM_LIMIT,
                                      has_side_effects=True)
    else:
        params = _cparams(*sem)
    res = pl.pallas_call(body, name=name, grid=grid, in_specs=in_specs, out_specs=out_specs, out_shape=out_shape,
                         scratch_shapes=scratch, compiler_params=params)(*args)
    return res[:n_out], res[n_out:]


HEAD_PAIRS = MLA_HEADS // 2


def _attn_fwd(q, k, v, *, name, side=None):
    t = q.shape[0]
    hw = V7X_LANES
    tb = min(ATT_BLOCK, t)
    nblk = t // tb
    nt = (((1,), (1,)), ((), ()))

    def compute(refs):
        q_ref, k_ref, v_ref, o_ref, lse_ref = refs
        i = pl.program_id(1)
        qs = [q_ref[:, a * hw:(a + 1) * hw] for a in range(2)]

        def kv_step(j, carry, masked):
            r0 = pl.multiple_of(j * tb, tb)
            vb = v_ref[pl.ds(r0, tb), :]
            out = []
            for a in range(2):
                m, l, acc = carry[a]
                s = lax.dot_general(qs[a], k_ref[pl.ds(r0, tb), a * hw:(a + 1) * hw], nt, preferred_element_type=F32)
                if masked:
                    s = jnp.where(_att_mask(i, j, tb), s, NEG_INF)
                m_new = jnp.maximum(m, jnp.max(s, axis=1, keepdims=True))
                alpha = jnp.exp2(m - m_new)
                p = jnp.exp2(s - m_new)
                l = alpha * l + jnp.sum(p, axis=1, keepdims=True)
                out.append((m_new, l, alpha * acc + jnp.dot(p.astype(BF16), vb, preferred_element_type=F32)))
            return out

        init = [(jnp.full((tb, 1), NEG_INF, F32), jnp.zeros((tb, 1), F32), jnp.zeros((tb, hw), F32)) for _ in range(2)]
        carry = lax.fori_loop(0, i, functools.partial(kv_step, masked=False), init)
        (m0, l0, acc0), (m1, l1, acc1) = kv_step(i, carry, True)
        first = lax.broadcasted_iota(jnp.int32, (tb, hw), 1) < V_HEAD
        o_ref[...] = jnp.where(first, acc0 / l0, acc1 / l1)
        lse_ref[0] = m0 + jnp.log2(l0)
        lse_ref[1] = m1 + jnp.log2(l1)

    def body(*refs):
        _with_side(side, 3, 2, refs, pl.program_id(0) * nblk + pl.program_id(1), HEAD_PAIRS * nblk, compute)

    return _side_call(
        body, side, name=name, grid=(HEAD_PAIRS, nblk),
        in_specs=[pl.BlockSpec((tb, 2 * hw), lambda hp, i: (i, hp)), pl.BlockSpec((t, 2 * hw), lambda hp, i: (0, hp)),
                  pl.BlockSpec((t, hw), lambda hp, i: (0, hp))],
        out_specs=[pl.BlockSpec((tb, hw), lambda hp, i: (i, hp)), pl.BlockSpec((2, tb, 1), lambda hp, i: (hp, i, 0))],
        out_shape=[jax.ShapeDtypeStruct((t, MLA_HEADS * V_HEAD), F32), jax.ShapeDtypeStruct((MLA_HEADS, t, 1), F32)],
        scratch=[], args=[q, k, v], sem=("parallel", "parallel"))


def _attn_bwd(q, k, v, o, lse, do, *, name, side=None):
    t = q.shape[0]
    hw = V7X_LANES
    tb = min(ATT_BLOCK, t)
    nblk = t // tb
    nt = (((1,), (1,)), ((), ()))
    tn = (((0,), (0,)), ((), ()))

    def compute(refs):
        q_ref, k_ref, v_ref, o_ref, lse_ref, do_ref, dq_ref, dk_ref, dv_ref, delta_ref = refs
        j = pl.program_id(1)
        first = lax.broadcasted_iota(jnp.int32, (tb, hw), 1) < V_HEAD
        mine = [first, jnp.logical_not(first)]

        @pl.when(j == 0)
        def _():
            dq_ref[...] = jnp.zeros_like(dq_ref)

            def dstep(i, c):
                r0 = pl.multiple_of(i * tb, tb)
                prod = do_ref[pl.ds(r0, tb), :] * o_ref[pl.ds(r0, tb), :]
                for a in range(2):
                    delta_ref[a, pl.ds(r0, tb), :] = jnp.sum(jnp.where(mine[a], prod, 0.0), axis=1, keepdims=True)
                return c

            lax.fori_loop(0, nblk, dstep, 0)

        kb, vb = k_ref[...], v_ref[...]

        def q_step(i, carry, masked):
            dks, dv = carry
            r0 = pl.multiple_of(i * tb, tb)
            dob = do_ref[pl.ds(r0, tb), :].astype(BF16)
            new_dks = []
            for a in range(2):
                qa, ka = q_ref[pl.ds(r0, tb), a * hw:(a + 1) * hw], kb[:, a * hw:(a + 1) * hw]
                s = lax.dot_general(qa, ka, nt, preferred_element_type=F32)
                if masked:
                    s = jnp.where(_att_mask(i, j, tb), s, NEG_INF)
                p = jnp.exp2(s - lse_ref[a, pl.ds(r0, tb), :])
                doa = jnp.where(mine[a], dob, jnp.zeros_like(dob))
                dv = dv + lax.dot_general(p.astype(BF16), doa, tn, preferred_element_type=F32)
                dp = lax.dot_general(doa, vb, nt, preferred_element_type=F32)
                ds = (p * (dp - delta_ref[a, pl.ds(r0, tb), :]) * LN2).astype(BF16)
                new_dks.append(dks[a] + lax.dot_general(ds, qa, tn, preferred_element_type=F32))
                dq_ref[pl.ds(r0, tb), a * hw:(a + 1) * hw] += jnp.dot(ds, ka, preferred_element_type=F32)
            return new_dks, dv

        zero = jnp.zeros((tb, hw), F32)
        carry = q_step(j, ([zero, zero], zero), True)
        dks, dv = lax.fori_loop(j + 1, nblk, functools.partial(q_step, masked=False), carry)
        for a in range(2):
            dk_ref[:, a * hw:(a + 1) * hw] = dks[a]
        dv_ref[...] = dv

    def body(*refs):
        _with_side(side, 6, 3, refs, pl.program_id(0) * nblk + pl.program_id(1), HEAD_PAIRS * nblk, compute)

    whole = lambda w: pl.BlockSpec((t, w), lambda hp, j: (0, hp))
    blockj = lambda w: pl.BlockSpec((tb, w), lambda hp, j: (j, hp))
    return _side_call(
        body, side, name=name, grid=(HEAD_PAIRS, nblk),
        in_specs=[whole(2 * hw), blockj(2 * hw), blockj(hw), whole(hw), pl.BlockSpec((2, t, 1), lambda hp, j: (hp, 0, 0)),
                  whole(hw)],
        out_specs=[whole(2 * hw), blockj(2 * hw), blockj(hw)],
        out_shape=[jax.ShapeDtypeStruct(q.shape, F32), jax.ShapeDtypeStruct(k.shape, F32), jax.ShapeDtypeStruct(v.shape, F32)],
        scratch=[pltpu.VMEM((2, t, 1), F32)], args=[q, k, v, o, lse, do], sem=("parallel", "arbitrary"))


class _Exchange:
    def __init__(self, xs, gather, two_level=False):
        assert gather or not two_level
        self.xs, self.gather, self.n, self.two_level = list(xs), gather, len(xs), two_level
        shapes = [tuple(x.shape) if gather else tuple(x.shape[1:]) for x in xs]
        self.out_shape = [jax.ShapeDtypeStruct((N_DEV,) + shp, x.dtype) for shp, x in zip(shapes, xs)]
        self.specs = [pl.BlockSpec(memory_space=pl.ANY)] * self.n
        self.scratch = [pltpu.SemaphoreType.DMA((self.n, N_DEV - 1)), pltpu.SemaphoreType.DMA((self.n, N_DEV - 1)),
                        pltpu.SemaphoreType.DMA((self.n,))]

    def copies(self, x_refs, y_refs, send_sems, recv_sems, local_sems):
        mx, my, mc = lax.axis_index("x"), lax.axis_index("y"), lax.axis_index("c")
        me = 4 * mx + 2 * my + mc
        out = [pltpu.make_async_copy(x_refs[i] if self.gather else x_refs[i].at[me], y_refs[i].at[me], local_sems.at[i])
               for i in range(self.n)]
        for k in range(1, N_DEV):
            px = 1 - mx if k & 4 else mx
            py = 1 - my if k & 2 else my
            pc = 1 - mc if k & 1 else mc
            for i in range(self.n):
                out.append(pltpu.make_async_remote_copy(
                    src_ref=x_refs[i] if self.gather else x_refs[i].at[4 * px + 2 * py + pc], dst_ref=y_refs[i].at[me],
                    send_sem=send_sems.at[i, k - 1], recv_sem=recv_sems.at[i, k - 1],
                    device_id=(px, py, pc), device_id_type=pl.DeviceIdType.MESH))
        return out


    def _two_level(self, x_refs, y_refs, send_sems, recv_sems, local_sems):
        mx, my, mc = lax.axis_index("x"), lax.axis_index("y"), lax.axis_index("c")
        sib = (mx, my, 1 - mc)
        chips = [(1 - mx, my), (mx, 1 - my), (1 - mx, 1 - my)]
        idx = lambda px, py, pc: 4 * px + 2 * py + pc
        me = idx(mx, my, mc)

        def rc(i, k, src, block, to):
            return pltpu.make_async_remote_copy(
                src_ref=src, dst_ref=y_refs[i].at[block], send_sem=send_sems.at[i, k], recv_sem=recv_sems.at[i, k],
                device_id=to, device_id_type=pl.DeviceIdType.MESH)

        rng = range(self.n)
        over_ici = [(j, chip, i) for j, chip in enumerate(chips) for i in rng]
        return dict(
            local=lambda: [pltpu.make_async_copy(x_refs[i], y_refs[i].at[me], local_sems.at[i]) for i in rng],
            own=lambda: [rc(i, 0, x_refs[i], me, sib) for i in rng]
            + [rc(i, 1 + j, x_refs[i], me, (*chip, mc)) for j, chip, i in over_ici],
            relay=lambda: [rc(i, 4 + j, y_refs[i].at[idx(*chip, mc)], idx(*chip, mc), sib) for j, chip, i in over_ici],
            landed=lambda: [rc(i, 1 + j, x_refs[i], idx(*chip, mc), sib) for j, chip, i in over_ici],
            last=lambda: [rc(i, 0, x_refs[i], idx(*sib), sib) for i in rng]
            + [rc(i, 4 + j, x_refs[i], idx(*chip, 1 - mc), sib) for j, chip, i in over_ici])

    def start(self, *refs):
        if not self.two_level:
            for cp in self.copies(*refs):
                cp.start()
            return
        plan = self._two_level(*refs)
        for cp in plan['local']() + plan['own']():
            cp.start()

    def relay(self, *refs):
        if not self.two_level:
            return
        plan = self._two_level(*refs)
        for arrived, cp in zip(plan['landed'](), plan['relay']()):
            arrived.wait_recv()
            cp.start()

    def finish(self, *refs):
        if not self.two_level:
            for cp in self.copies(*refs):
                cp.wait()
            return
        plan = self._two_level(*refs)
        for cp in plan['last']():
            cp.wait_recv()
        for cp in plan['own']() + plan['relay']():
            cp.wait_send()
        for cp in plan['local']():
            cp.wait()


def _exchange(xs, *, gather, name, two_level=False):
    ex = _Exchange(xs, gather, two_level)
    n = ex.n

    def body(*refs):
        refs = (refs[:n], refs[n:2 * n]) + tuple(refs[2 * n:])
        ex.start(*refs)
        ex.relay(*refs)
        ex.finish(*refs)

    return pl.pallas_call(
        body, name=name, out_shape=ex.out_shape, in_specs=ex.specs, out_specs=ex.specs, scratch_shapes=ex.scratch,
        compiler_params=pltpu.CompilerParams(has_side_effects=True),
    )(*ex.xs)


def _adamw(w, gs, m, v, *, name, tm=256):
    nl = len(gs)
    parts = gs[0].ndim == 3
    c = w.shape[1]
    r = w.shape[0] // nl
    tm = _pick_rows(r, tm)
    nrow = r // tm

    def body(*refs):
        w_ref, g_refs, (m_ref, v_ref, go_ref, d_ref, mo_ref, vo_ref) = refs[0], refs[1:1 + nl], refs[1 + nl:]

        def update(g_ref):
            if parts:
                gv = g_ref[0].astype(F32)
                for k in range(1, N_DEV):
                    gv = gv + g_ref[k].astype(F32)
            else:
                gv = g_ref[...]
            mn = ADAM_B1 * m_ref[...] + (1.0 - ADAM_B1) * gv
            vn = ADAM_B2 * v_ref[...] + (1.0 - ADAM_B2) * jnp.square(gv)
            m_hat = mn / (1.0 - ADAM_B1 ** ADAM_STEP)
            v_hat = vn / (1.0 - ADAM_B2 ** ADAM_STEP)
            go_ref[...] = gv
            d_ref[...] = -ADAM_LR * (m_hat / (jnp.sqrt(v_hat) + ADAM_EPS) + ADAM_WD * w_ref[...])
            mo_ref[...] = mn
            vo_ref[...] = vn

        if nl == 1:
            update(g_refs[0])
        else:
            for layer, g_ref in enumerate(g_refs):
                pl.when(pl.program_id(0) == layer)(functools.partial(update, g_ref))

    spec = pl.BlockSpec((tm, c), lambda l, i: (l * nrow + i, 0))

    def gspec(layer):
        row = lambda l, i: jnp.where(l == layer, i, 0)
        if parts:
            return pl.BlockSpec((N_DEV, tm, c), lambda l, i: (0, row(l, i), 0))
        return pl.BlockSpec((tm, c), lambda l, i: (row(l, i), 0))

    return pl.pallas_call(
        body, name=name, grid=(nl, nrow), in_specs=[spec] + [gspec(k) for k in range(nl)] + [spec, spec],
        out_specs=[spec] * 4, out_shape=[jax.ShapeDtypeStruct(w.shape, F32)] * 4,
        compiler_params=_cparams("arbitrary", "arbitrary"),
    )(w, *gs, m, v)


def _sum_parts(x, *, name):
    def body(x_ref, o_ref):
        acc = x_ref[0]
        for k in range(1, N_DEV):
            acc = acc + x_ref[k]
        o_ref[...] = acc

    return pl.pallas_call(body, name=name, out_shape=jax.ShapeDtypeStruct(x.shape[1:], x.dtype))(x)


def _permute_in(a):
    pad = jnp.zeros(a.shape[:-1] + (IN_PAD - IN_WIDTH,), a.dtype)
    return jnp.concatenate([a[..., lo:hi] for lo, hi in _IN_SEGMENTS] + [pad], axis=-1)


def _unpermute_in(a):
    out, pos = {}, 0
    for lo, hi in _IN_SEGMENTS:
        out[lo] = a[..., pos:pos + hi - lo]
        pos += hi - lo
    return jnp.concatenate([out[lo] for lo in sorted(out)], axis=-1)


def _full_from_gathered(g, kind):
    if kind == 'row':
        return g.reshape((-1,) + g.shape[2:])
    if g.shape[-1] % V7X_LANES == 0:
        return jnp.moveaxis(g, 0, -2).reshape(g.shape[1:-1] + (-1,))
    return jnp.concatenate([g[d] for d in range(N_DEV)], axis=-1)


def _contrib_from_full(g, kind):
    if kind == 'row':
        return g.reshape((N_DEV, -1) + g.shape[1:])
    ns = g.shape[-1] // N_DEV
    if ns % V7X_LANES == 0:
        return jnp.moveaxis(g.reshape(g.shape[:-1] + (N_DEV, ns)), -2, 0)
    return jnp.stack([g[..., d * ns:(d + 1) * ns] for d in range(N_DEV)])


def _in_runs(ns):
    runs, pos = [], 0
    for lo, hi in _IN_SEGMENTS:
        for d in range(lo // ns, (hi - 1) // ns + 1):
            a, b = max(lo, d * ns), min(hi, (d + 1) * ns)
            runs.append((d, a - d * ns, b - d * ns, pos))
            pos += b - a
    return runs


def _w_in_from_gathered(g):
    pieces = [g[d][:, a:b] for d, a, b, _ in _in_runs(g.shape[2])]
    pad = jnp.zeros((g.shape[1], IN_PAD - IN_WIDTH), g.dtype)
    return jnp.concatenate(pieces + [pad], axis=1)


def _w_in_contrib(gp):
    ns = IN_WIDTH // N_DEV
    per_dev = [[] for _ in range(N_DEV)]
    for d, a, b, pos in sorted(_in_runs(ns), key=lambda r: (r[0], r[1])):
        per_dev[d].append(gp[:, pos:pos + b - a])
    return jnp.stack([jnp.concatenate(p, axis=1) for p in per_dev])


def _as2d(a, lead=0):
    return a.reshape(a.shape[:lead] + (-1, a.shape[-1]))


def _chan_cols(re, im):
    lead = re.shape[:-1]
    nb = S5_CH // SCAN_LANES
    return jnp.stack([re.reshape(lead + (nb, SCAN_LANES)), im.reshape(lead + (nb, SCAN_LANES))],
                     axis=-2).reshape(lead + (2 * S5_CH,))


def _s5_tables(lam_re, lam_im, log_dt, b_re, b_im, c_re, c_im):
    dt = jnp.exp(log_dt)[:, None]
    mag = jnp.exp(lam_re * dt)
    a_re = mag * jnp.cos(lam_im * dt)
    a_im = mag * jnp.sin(lam_im * dt)
    den = lam_re * lam_re + lam_im * lam_im
    f_re = ((a_re - 1.0) * lam_re + a_im * lam_im) / den
    f_im = (a_im * lam_re - (a_re - 1.0) * lam_im) / den
    bb_re = f_re[..., None] * b_re - f_im[..., None] * b_im
    bb_im = f_re[..., None] * b_im + f_im[..., None] * b_re
    gb = S5_GROUPS // S5_BD
    eye = jnp.eye(gb, dtype=F32)
    blocks = lambda a: a.reshape((S5_BD, gb) + a.shape[1:])

    def cols(re, im):
        shp = (S5_BD, S5_WIDTH // S5_BD, -1, SCAN_LANES)
        return jnp.stack([re.reshape(shp), im.reshape(shp)], axis=-2).reshape(S5_BD, S5_WIDTH // S5_BD, -1)

    flat = lambda a: a.reshape(S5_BD, S5_WIDTH // S5_BD, -1)
    wb_c = cols(flat(jnp.einsum('kgpc,gh->kgchp', blocks(bb_re), eye)), flat(jnp.einsum('kgpc,gh->kgchp', blocks(bb_im), eye)))
    wc_c = cols(flat(jnp.einsum('kgcp,gh->khcgp', blocks(c_re), eye)), -flat(jnp.einsum('kgcp,gh->khcgp', blocks(c_im), eye)))
    a_row = _chan_cols(a_re.reshape(1, S5_CH), a_im.reshape(1, S5_CH))
    return wb_c, wc_c.transpose(0, 2, 1), a_row


def _scan_tables(a_row, conj, seg_len):
    nb = S5_CH // SCAN_LANES
    a = a_row.reshape(nb, 2, SCAN_LANES)
    base = (a[:, 0], -a[:, 1] if conj else a[:, 1])
    mul = lambda x, y: (x[0] * y[0] - x[1] * y[1], x[0] * y[1] + x[1] * y[0])
    seg, sq, e = None, base, seg_len
    while e:
        if e & 1:
            seg = sq if seg is None else mul(seg, sq)
        sq, e = mul(sq, sq), e >> 1
    seg2 = mul(seg, seg)
    rows = [base, seg, seg2, mul(seg2, seg2)]
    lay = lambda z: jnp.stack([z[0], z[1]], axis=1).reshape(-1)
    return jnp.stack([lay(z) for z in rows] + [jnp.zeros((2 * S5_CH,), F32)] * (V7X_SUBLANES - len(rows)))


def _rope_tables(t):
    half = QK_ROPE // 2
    inv_freq = 1.0 / (ROPE_THETA ** (jnp.arange(0, QK_ROPE, 2, dtype=F32) / QK_ROPE))
    ang = jnp.arange(t, dtype=F32)[:, None] * inv_freq[None, :]
    cos, sin = jnp.cos(ang), jnp.sin(ang)
    zero = jnp.zeros_like(sin)

    def lay(nope, width, first, second, pad=0):
        head = jnp.concatenate([jnp.full((t, nope), 1.0 if first is cos else 0.0, F32), first, second,
                                jnp.zeros((t, pad), F32)], axis=1)
        reps = width // head.shape[1]
        out = jnp.tile(head, (1, reps))
        return jnp.pad(out, ((0, 0), (0, width - out.shape[1])))

    hq, pad = MLA_HEADS * V7X_LANES, V7X_LANES - QK_NOPE - QK_ROPE
    q_tabs = (lay(QK_NOPE, hq, cos, cos, pad), lay(QK_NOPE, hq, -sin, zero, pad), lay(QK_NOPE, hq, zero, sin, pad))
    k_tabs = (lay(0, V7X_LANES, cos, cos)[:, :V7X_LANES] * (jnp.arange(V7X_LANES) < QK_ROPE),
              lay(0, V7X_LANES, -sin, zero) * (jnp.arange(V7X_LANES) < QK_ROPE),
              lay(0, V7X_LANES, zero, sin) * (jnp.arange(V7X_LANES) < QK_ROPE))
    return q_tabs, k_tabs


def _sgu_tables(w_s, b_s):
    pos = jnp.arange(SGU_CHUNK) // CHUNK
    mask = pos[None, :] <= pos[:, None]
    wm = jnp.where(mask[None], w_s, 0.0).reshape(SGU_GROUPS * SGU_CHUNK, SGU_CHUNK)
    bias = jnp.repeat(b_s.T, SGU_WIDTH // SGU_GROUPS, axis=1)
    return wm, bias


def _row(v):
    return v.reshape(1, -1)


_S5_PARAMS = ('s5_lambda_re', 's5_lambda_im', 's5_log_dt', 's5_b_re', 's5_b_im', 's5_c_re', 's5_c_im')


def _derived_tables(p, t):
    (wb, wc, a_row), s5_pull = jax.vjp(jax.vmap(_s5_tables), *[p[n] for n in _S5_PARAMS])
    (wm, bias), sgu_pull = jax.vjp(jax.vmap(_sgu_tables), p['sgu_w_s'], p['sgu_b_s'])
    tab_f = jax.vmap(lambda a: _scan_tables(a, False, t // V7X_SUBLANES))(a_row)
    tab_b = jax.vmap(lambda a: _scan_tables(a, True, t // V7X_SUBLANES))(a_row)
    wb, wc = wb.astype(BF16), wc.astype(BF16)
    per_layer = [dict(s5_wb=wb[l], s5_wc=wc[l], s5_tab_fwd=tab_f[l], s5_tab_bwd=tab_b[l], sgu_wm=wm[l], sgu_bias=bias[l])
                 for l in range(len(wm))]

    def pull(grads):
        stacked = lambda k: jnp.stack([g[k] for g in grads])
        out = dict(zip(_S5_PARAMS, s5_pull((stacked('s5_wb'), stacked('s5_wc'), stacked('s5_a')))))
        out['sgu_w_s'], out['sgu_b_s'] = sgu_pull((stacked('sgu_wm'), stacked('sgu_bias')))
        return out

    return per_layer, pull


def _layer_fwd(x, ada, w, rope_tabs, tag, side=None, after_attn=None):
    s = {'x': x}
    q_tabs, k_tabs = rope_tabs
    sc1, gt1, sc2, gt2 = _row(1.0 + ada[1]), _row(1.0 + ada[2]), _row(1.0 + ada[4]), _row(1.0 + ada[5])
    s.update(sc1=sc1, gt1=gt1, sc2=sc2, gt2=gt2)
    (h,) = _rowcall(_modulate_fn, [x], [sc1, _row(ada[0])], [(D_MODEL, BF16)], tm=512, name=f"mod1_{tag}")
    proj = _mm(h, w['w_in_p'], bias=w['b_in_p'], name=f"proj_{tag}", tn=1792)
    s.update(h=h, proj=proj)

    u_view = (proj, S5_WIDTH, P_S5 // S5_WIDTH)
    u_seg = _seg_order(proj[:, P_S5:P_S5 + S5_WIDTH]).astype(BF16)
    hs, ylin = _s5_fwd(u_seg, w['s5_wb'], w['s5_wc'], w['s5_tab_fwd'], name=f"s5_fwd_{tag}")
    ylin = _time_order(ylin)
    s5_full = [_row(w['s5_d']), w['s5_w_glu'], _row(w['s5_b_glu'])]
    (y_s5,) = _rowcall(_s5_post_fn, [ylin, u_view], s5_full, [(S5_WIDTH, BF16)], tm=256, name=f"s5_post_{tag}")
    s.update(u_seg=u_seg, hs=hs, ylin=ylin, y_s5=y_s5)

    mla_rows = [(proj, MLA_BLK, P_MLA // MLA_BLK), *q_tabs, *k_tabs]
    mla_full = [_row(w['mla_q_norm']), w['mla_wq'], _row(w['mla_kv_norm']), w['mla_wk'], w['mla_wv']]
    hq, hv = MLA_HEADS * V7X_LANES, MLA_HEADS * V_HEAD
    q_r, k_r, v_r = _rowcall(_mla_pre_fn, mla_rows, mla_full, [(hq, BF16), (hq, BF16), (hv, BF16)],
                             tm=256, name=f"mla_pre_{tag}")
    (y_mla, lse), side_out = _attn_fwd(q_r, k_r, v_r, name=f"attn_fwd_{tag}", side=side)
    if after_attn is not None:
        after_attn(side_out)
    s.update(q_r=q_r, k_r=k_r, v_r=v_r, lse=lse, y_mla=y_mla, mla_full=mla_full)

    sgu_rows = [(proj, SGU_WIDTH, P_USGU // SGU_WIDTH), (proj, SGU_WIDTH, P_VSGU // SGU_WIDTH)]
    sgu_full = [_row(w['sgu_ln_g']), _row(w['sgu_ln_b']), w['sgu_wm'], w['sgu_bias']]
    (y_sgu,) = _rowcall(_sgu_fn, sgu_rows, sgu_full, [(SGU_WIDTH, BF16)], tm=SGU_CHUNK, name=f"sgu_{tag}")
    s.update(sgu_full=sgu_full, y_sgu=y_sgu)

    wbr = w['w_branch'].reshape(-1, D_MODEL)
    gate_rows = [(proj, D_MODEL, b) for b in range(3)]
    (merged,) = _rowcall(_merge_fn, [y_s5, y_mla, y_sgu] + gate_rows, [wbr], [(D_MODEL, BF16)], tm=256, name=f"merge_{tag}")
    ymix = _mm(merged, w['w_out'], name=f"wout_{tag}")
    (x1,) = _rowcall(_ln_res_fn, [x, ymix], [gt1, _row(w['ln1_g']), _row(w['ln1_b'])], [(D_MODEL, F32)], tm=256,
                     name=f"ln1_{tag}")
    s.update(merged=merged, ymix=ymix, x1=x1)

    (h2,) = _rowcall(_modulate_fn, [x1], [sc2, _row(ada[3])], [(D_MODEL, BF16)], tm=512, name=f"mod2_{tag}")
    ff_a, ff_b, act = _mm_swiglu(h2, w['ffn_w_in'], name=f"ffn_in_{tag}")
    f = _mm(act, w['ffn_w_out'], name=f"ffn_out_{tag}", tk=2816)
    (x2,) = _rowcall(_ln_res_fn, [x1, f], [gt2, _row(w['ln2_g']), _row(w['ln2_b'])], [(D_MODEL, F32)], tm=256,
                     name=f"ln2_{tag}")
    s.update(h2=h2, ff_a=ff_a, ff_b=ff_b, act=act, f=f)
    return x2, s, side_out


def _mod_bwd_fn(x, dh, dxa, scale_row):
    return (dxa + dh * scale_row, jnp.sum(dh * x, axis=0, keepdims=True), jnp.sum(dh, axis=0, keepdims=True))


def _layer_bwd(dx2, s, w, rope_tabs, tag, make_side=None, make_side2=None):
    g = {}
    q_tabs, k_tabs = rope_tabs
    t = dx2.shape[0]
    ln_full = lambda gt, a, b: [gt, _row(w[a]), _row(w[b])]

    dx1_a, df, dgt2, g['ln2_g'], g['ln2_b'] = _rowcall_vjp(
        _ln_res_fn, [s['x1'], s['f']], ln_full(s['gt2'], 'ln2_g', 'ln2_b'), [dx2], [0, 1], [0, 1, 2],
        tm=256, name=f"ln2_bwd_{tag}", row_dtypes=[F32, BF16])
    dact = _mm(df, w['ffn_w_out'], tb=True, name=f"ffn_out_dx_{tag}", tn=1408)
    g['ffn_w_out'] = _mm(s['act'], df, ta=True, name=f"ffn_out_dw_{tag}", tm=1408)
    def swiglu_bwd_fn(a, b, d):
        _, pull = jax.vjp(_swiglu_fn, a, b)
        return (jnp.concatenate(pull((d,)), axis=1),)

    (dab,) = _rowcall(swiglu_bwd_fn, [s['ff_a'], s['ff_b'], dact], [], [(2 * FF_HIDDEN, BF16)],
                      tm=256, name=f"swiglu_bwd_{tag}")
    dh2 = _mm(dab, w['ffn_w_in'], tb=True, name=f"ffn_in_dx_{tag}", tk=1408)
    g['ffn_w_in'] = _mm(s['h2'], dab, ta=True, name=f"ffn_in_dw_{tag}", tn=1408)
    dx1, dsc2, dsh2 = _rowcall(_mod_bwd_fn, [s['x1'], dh2, dx1_a], [s['sc2']], [(D_MODEL, F32)],
                               [((1, D_MODEL), F32)] * 2, tm=256, name=f"mod2_bwd_{tag}")

    dx_a, dymix, dgt1, g['ln1_g'], g['ln1_b'] = _rowcall_vjp(
        _ln_res_fn, [s['x'], s['ymix']], ln_full(s['gt1'], 'ln1_g', 'ln1_b'), [dx1], [0, 1], [0, 1, 2],
        tm=256, name=f"ln1_bwd_{tag}", row_dtypes=[F32, BF16])
    dmerged = _mm(dymix, w['w_out'], tb=True, name=f"wout_dx_{tag}")
    g['w_out'] = _mm(s['merged'], dymix, ta=True, name=f"wout_dw_{tag}")

    proj = s['proj']
    wbr = w['w_branch'].reshape(-1, D_MODEL).astype(F32)
    gate_rows = [(proj, D_MODEL, b) for b in range(3)]
    dy_s5, dy_mla, dy_sgu, dl0, dl1, dl2, dwbr = _rowcall_vjp(
        _merge_fn, [s['y_s5'], s['y_mla'], s['y_sgu']] + gate_rows, [wbr], [dmerged], [0, 1, 2, 3, 4, 5], [0],
        tm=256, name=f"merge_bwd_{tag}")
    g['w_branch'] = dwbr.reshape(w['w_branch'].shape)

    sgu_rows = [(proj, SGU_WIDTH, P_USGU // SGU_WIDTH), (proj, SGU_WIDTH, P_VSGU // SGU_WIDTH)]
    du_sgu, dv_sgu, dlg, dlb, dwm, dbias = _rowcall_vjp(
        _sgu_fn, sgu_rows, s['sgu_full'], [dy_sgu], [0, 1], [0, 1, 2, 3], tm=SGU_CHUNK, name=f"sgu_bwd_{tag}")
    g['sgu_ln_g'], g['sgu_ln_b'] = dlg.reshape(-1), dlb.reshape(-1)
    g['sgu_wm'], g['sgu_bias'] = dwm, dbias

    (dq_r, dk_r, dv_r), side_out = _attn_bwd(s['q_r'], s['k_r'], s['v_r'], s['y_mla'], s['lse'], dy_mla,
                                             name=f"attn_bwd_{tag}", side=make_side(g) if make_side is not None else None)
    mla_rows = [(proj, MLA_BLK, P_MLA // MLA_BLK), *q_tabs, *k_tabs]
    dmla, dqn, dwq, dkvn, dwk, dwv = _rowcall_vjp(
        _mla_pre_fn, mla_rows, s['mla_full'], [dq_r, dk_r, dv_r], [0], [0, 1, 2, 3, 4], tm=256, name=f"mla_pre_bwd_{tag}")
    g['mla_w_q_up'], g['mla_w_kv_up'] = _mla_weight_grads(dwq, dwk, dwv)
    g['mla_q_norm'], g['mla_kv_norm'] = dqn.reshape(-1), dkvn.reshape(-1)

    s5_full = [_row(w['s5_d']), w['s5_w_glu'], _row(w['s5_b_glu'])]
    dylin, du_a, dd, g['s5_w_glu'], dbg = _rowcall_vjp(
        _s5_post_fn, [s['ylin'], (proj, S5_WIDTH, P_S5 // S5_WIDTH)], s5_full, [dy_s5], [0, 1], [0, 1, 2], tm=256,
        name=f"s5_post_bwd_{tag}", row_dtypes=[BF16, F32])
    g['s5_d'], g['s5_b_glu'] = dd.reshape(-1), dbg.reshape(-1)
    (du_b, g['s5_wb'], g['s5_wc'], da_part), side2_out = _s5_bwd(
        _seg_order(dylin), s['u_seg'], s['hs'], w['s5_wb'], w['s5_wc'], w['s5_tab_bwd'], name=f"s5_bwd_{tag}",
        side=make_side2(g) if make_side2 is not None else None)
    du_b = _time_order(du_b)
    g['s5_a'] = jnp.sum(da_part, axis=0, keepdims=True)

    def dproj_fn(g0, g1, g2, ua, ub, us, vs, ml):
        d = jnp.concatenate([g0, g1, g2, ua + ub, us, vs, ml], axis=1)
        return d, jnp.sum(d, axis=0, keepdims=True)

    dproj, db_in = _rowcall(dproj_fn, [dl0, dl1, dl2, du_a, du_b, du_sgu, dv_sgu, dmla], [], [(IN_PAD, BF16)],
                            [((1, IN_PAD), F32)], tm=256, name=f"dproj_{tag}")
    dh = _mm(dproj, w['w_in_p'], tb=True, name=f"proj_dx_{tag}", tk=1792)
    g['w_in_p'] = _mm(s['h'], dproj, ta=True, name=f"proj_dw_{tag}", tm=512, tn=1792)
    g['b_in'] = _unpermute_in(db_in).reshape(-1)
    dx, dsc1, dsh1 = _rowcall(_mod_bwd_fn, [s['x'], dh, dx_a], [s['sc1']], [(D_MODEL, F32)], [((1, D_MODEL), F32)] * 2,
                              tm=256, name=f"mod1_bwd_{tag}")
    d_ada = jnp.concatenate([dsh1, dsc1, dgt1, dsh2, dsc2, dgt2], axis=0)
    return dx, d_ada, g, side_out, side2_out


def _loss_fn(y, target):
    err = y - target
    return (err / D_MODEL, 0.5 * jnp.sum(jnp.sum(err * err, axis=1, keepdims=True), axis=0, keepdims=True) / D_MODEL)


def _step(p):
    me = 4 * lax.axis_index("x") + 2 * lax.axis_index("y") + lax.axis_index("c")
    x = p['x'][0]
    t = x.shape[0]
    rope_tabs = _rope_tables(t)

    (c_all,) = _exchange([jnp.broadcast_to(p['c'], (V7X_SUBLANES, D_MODEL))], gather=True, name="gather_c")
    c_all = c_all[:, 0, :]
    (c_act,) = _rowcall(lambda cc: (cc * _sigmoid(cc),), [c_all], [], [(D_MODEL, F32)], tm=N_DEV, name="c_silu")
    ncol = p['w_ada'].shape[2]
    b_ada_loc = lax.dynamic_slice_in_dim(p['b_ada'], me * ncol, ncol, axis=1)
    ada_cols = jnp.concatenate([_mm(c_act, p['w_ada'][l], bias=b_ada_loc[l:l + 1], name=f"ada_{l}") for l in range(DEPTH)])
    (ada_all,) = _exchange([ada_cols], gather=True, name="gather_ada")
    ada_all = ada_all.reshape(N_DEV, DEPTH, N_DEV, ncol)
    ada = lax.dynamic_index_in_dim(ada_all, me, axis=2, keepdims=False)
    ada = ada.transpose(1, 0, 2).reshape(DEPTH, 6, D_MODEL)

    mixer_w, ffn_w = SHARDED[:-2], SHARDED[-2:]

    def shards(l, group):
        return [p[n][l].astype(BF16) for n, _ in group]

    def contribs(g, group):
        return [(_w_in_contrib(g['w_in_p']) if n == 'w_in' else _contrib_from_full(g[n], kind)).astype(BF16)
                for n, kind in group]

    tables, pull_tables = _derived_tables(p, t)

    def mixer_weights(l, gathered):
        w = {n: _full_from_gathered(g, kind) for (n, kind), g in zip(mixer_w[1:], gathered[1:])}
        w['w_in_p'] = _w_in_from_gathered(gathered[0])
        w['b_in_p'] = _row(_permute_in(p['b_in'][l]))
        w['mla_wq'], w['mla_wk'], w['mla_wv'] = _mla_weights(w['mla_w_q_up'], w['mla_w_kv_up'])
        w.update(tables[l])
        for n in SMALL:
            if n != 'b_ada' and n != 'b_in':
                w[n] = p[n][l]
        return w

    saved, layers = [], []
    gathered = _exchange(shards(0, mixer_w), gather=True, name="gather_w_0", two_level=True)
    for l in range(DEPTH):
        w = mixer_weights(l, gathered)
        side = _Exchange(shards(l, ffn_w) + (shards(l + 1, mixer_w) if l + 1 < DEPTH else []), True, two_level=True)

        def add_ffn(res, w=w):
            for (n, kind), g in zip(ffn_w, res):
                w[n] = _full_from_gathered(g, kind)

        x, s, res = _layer_fwd(x, ada[l], w, rope_tabs, f"l{l}", side, add_ffn)
        gathered = res[len(ffn_w):]
        layers.append(w)
        saved.append(s)
    dy, loss_loc = _rowcall(_loss_fn, [x, p['loss_target'][0]], [], [(D_MODEL, F32)], [((1, 1), F32)], tm=256, name="loss")
    loss = lax.psum(loss_loc[0, 0], ("x", "y", "c"))

    w_in_g, other_w = mixer_w[:1], mixer_w[1:]
    d_ada, grads, landed = [None] * DEPTH, [None] * DEPTH, [[None, None, None] for _ in range(DEPTH)]
    dx, pending = dy, []
    for l in reversed(range(DEPTH)):
        make_side = lambda g, pending=pending: _Exchange(contribs(g, ffn_w) + pending, False)
        make_side2 = lambda g: _Exchange(contribs(g, other_w), False)
        dx, d_ada[l], grads[l], res, landed[l][1] = _layer_bwd(dx, saved[l], layers[l], rope_tabs, f"l{l}", make_side,
                                                               make_side2)
        landed[l][2] = res[:len(ffn_w)]
        if l + 1 < DEPTH:
            landed[l + 1][0] = res[len(ffn_w):]
        pending = contribs(grads[l], w_in_g)
    landed[0][0] = _exchange(pending, gather=False, name="scatter_g_0")
    landed = [list(a) + list(b) + list(c) for a, b, c in landed]
    d_ada = jnp.stack(d_ada).reshape(DEPTH, 6 * D_MODEL)

    (d_ada_all,) = _exchange([d_ada], gather=True, name="gather_dada")
    d_ada_cols = lax.dynamic_slice_in_dim(d_ada_all, me * ncol, ncol, axis=2)
    pad_b = ((0, V7X_LANES - N_DEV), (0, 0))
    c_act_p = jnp.pad(c_act, pad_b)
    g_w_ada = jnp.stack([_mm(c_act_p, jnp.pad(d_ada_cols[:, l], pad_b), ta=True, name=f"ada_dw_{l}") for l in range(DEPTH)])

    out = {}
    kinds = ('grad_', 'delta_', 'new_m_', 'new_v_')
    for i, (n, _) in enumerate(SHARDED):
        res = _adamw(_as2d(p[n]), [_as2d(landed[l][i], lead=1) for l in range(DEPTH)], _as2d(p['m_' + n]),
                     _as2d(p['v_' + n]), name=f"adamw_{n}")
        for kind, r in zip(kinds, res):
            out[kind + n] = r.reshape(p[n].shape)

    res = _adamw(_as2d(p['w_ada']), [_as2d(g_w_ada)], _as2d(p['m_w_ada']), _as2d(p['v_w_ada']), name="adamw_ada")
    for kind, r in zip(kinds, res):
        out[kind + 'w_ada'] = r.reshape(p['w_ada'].shape)

    small_g = pull_tables(grads)
    small_g.update({n: jnp.stack([grads[l][n] for l in range(DEPTH)]) for n in SMALL if n not in small_g and n != 'b_ada'})
    small_g['b_ada'] = d_ada
    n_small = sum(int(np.prod(p[n].shape)) for n in SMALL)
    n_pad = -(-n_small // SMALL_PAD) * SMALL_PAD
    flat = jnp.concatenate([small_g[n].reshape(-1) for n in SMALL])
    flat = jnp.pad(flat, (0, n_pad - n_small)).reshape(N_DEV, n_pad // N_DEV // 1024, 1024)
    (landed_small,) = _exchange([flat], gather=False, name="scatter_small")
    (g_all,) = _exchange([_sum_parts(landed_small, name="sum_small")], gather=True, name="gather_small")
    g_all = g_all.reshape(-1)
    off = 0
    for n in SMALL:
        size = int(np.prod(p[n].shape))
        g_n = g_all[off:off + size].reshape(p[n].shape)
        off += size
        res = _adamw(_as2d(p[n]), [_as2d(g_n)], _as2d(p['m_' + n]), _as2d(p['v_' + n]), name=f"adamw_{n}")
        for kind, r in zip(kinds, res):
            out[kind + n] = r.reshape(p[n].shape)

    outs = [loss, dx[None]]
    for kind in ('grad_', 'delta_', 'new_m_', 'new_v_'):
        outs += [out[kind + n] for n in WNAMES]
    return tuple(outs)


def kernel(x, c, w_ada, b_ada, w_in, b_in, s5_lambda_re, s5_lambda_im, s5_log_dt, s5_b_re, s5_b_im, s5_c_re, s5_c_im, s5_d, s5_w_glu, s5_b_glu, mla_q_norm, mla_w_q_up, mla_kv_norm, mla_w_kv_up, sgu_ln_g, sgu_ln_b, sgu_w_s, sgu_b_s, w_branch, w_out, ln1_g, ln1_b, ffn_w_in, ffn_w_out, ln2_g, ln2_b, loss_target, m_w_ada, m_b_ada, m_w_in, m_b_in, m_s5_lambda_re, m_s5_lambda_im, m_s5_log_dt, m_s5_b_re, m_s5_b_im, m_s5_c_re, m_s5_c_im, m_s5_d, m_s5_w_glu, m_s5_b_glu, m_mla_q_norm, m_mla_w_q_up, m_mla_kv_norm, m_mla_w_kv_up, m_sgu_ln_g, m_sgu_ln_b, m_sgu_w_s, m_sgu_b_s, m_w_branch, m_w_out, m_ln1_g, m_ln1_b, m_ffn_w_in, m_ffn_w_out, m_ln2_g, m_ln2_b, v_w_ada, v_b_ada, v_w_in, v_b_in, v_s5_lambda_re, v_s5_lambda_im, v_s5_log_dt, v_s5_b_re, v_s5_b_im, v_s5_c_re, v_s5_c_im, v_s5_d, v_s5_w_glu, v_s5_b_glu, v_mla_q_norm, v_mla_w_q_up, v_mla_kv_norm, v_mla_w_kv_up, v_sgu_ln_g, v_sgu_ln_b, v_sgu_w_s, v_sgu_b_s, v_w_branch, v_w_out, v_ln1_g, v_ln1_b, v_ffn_w_in, v_ffn_w_out, v_ln2_g, v_ln2_b):
    return _step(dict(locals()))
```

```python
import functools
import math

import numpy as np
import jax
import jax.numpy as jnp
from jax import lax
from jax.experimental import pallas as pl
from jax.experimental.pallas import tpu as pltpu

F32 = jnp.float32
BF16 = jnp.bfloat16

N_DEV = 8
D_MODEL = 1024
DEPTH = 4
CHUNK = 64
S5_WIDTH = 512
S5_GROUP = 16
S5_GROUPS = 32
S5_STATE = 64
MLA_HEADS = 8
QK_NOPE = 64
QK_ROPE = 32
V_HEAD = 64
Q_LORA = 384
KV_LORA = 256
ROPE_THETA = 10000.0
SGU_WIDTH = 512
SGU_GROUPS = 4
SGU_CHUNK = 128
FF_HIDDEN = 2816
DEEPNORM_ALPHA = (2 * DEPTH) ** 0.25
LN_EPS = 1e-5
RMS_EPS = 1e-6
NEG_INF = -1e30
ADAM_LR = 0.001
ADAM_B1 = 0.9
ADAM_B2 = 0.999
ADAM_EPS = 1e-08
ADAM_WD = 0.01
ADAM_STEP = 10

IN_WIDTH = 5280
IN_PAD = 5376
_O_S5, _O_CQ, _O_CKV, _O_KPE, _O_USGU, _O_VSGU, _O_GATE = 0, 512, 896, 1152, 1184, 1696, 2208
_IN_SEGMENTS = ((_O_GATE, IN_WIDTH), (_O_S5, _O_CQ), (_O_USGU, _O_VSGU), (_O_VSGU, _O_GATE), (_O_CQ, _O_USGU))
P_GATE, P_S5, P_USGU, P_VSGU, P_MLA = 0, 3072, 3584, 4096, 4608
MLA_BLK = 768

V7X_LANES = 128
V7X_SUBLANES = 8
VMEM_LIMIT = 56 * 1024 * 1024
ATT_BLOCK = 512
SCAN_LANES = 128
S5_CH = S5_GROUPS * S5_STATE
S5_BD = 4

WNAMES = ['w_ada', 'b_ada', 'w_in', 'b_in', 's5_lambda_re', 's5_lambda_im', 's5_log_dt', 's5_b_re', 's5_b_im',
          's5_c_re', 's5_c_im', 's5_d', 's5_w_glu', 's5_b_glu', 'mla_q_norm', 'mla_w_q_up', 'mla_kv_norm',
          'mla_w_kv_up', 'sgu_ln_g', 'sgu_ln_b', 'sgu_w_s', 'sgu_b_s', 'w_branch', 'w_out', 'ln1_g', 'ln1_b',
          'ffn_w_in', 'ffn_w_out', 'ln2_g', 'ln2_b']
SHARDED = (('w_in', 'col'), ('s5_w_glu', 'row'), ('mla_w_q_up', 'col'), ('mla_w_kv_up', 'col'),
           ('w_branch', 'col3'), ('w_out', 'row'), ('ffn_w_in', 'col'), ('ffn_w_out', 'row'))
SMALL = [n for n in WNAMES if n != 'w_ada' and n not in dict(SHARDED)]
SMALL_PAD = N_DEV * V7X_SUBLANES * 1024


def _cparams(*sem):
    return pltpu.CompilerParams(dimension_semantics=sem, vmem_limit_bytes=VMEM_LIMIT)


def _pick(n, target):
    if n <= target:
        return n
    best = None
    for d in range(V7X_LANES, target + 1, V7X_LANES):
        if n % d == 0:
            best = d
    assert best is not None, (n, target)
    return best


def _pick_rows(n, target):
    if n <= target:
        return n
    best = None
    for d in range(V7X_SUBLANES, target + 1, V7X_SUBLANES):
        if n % d == 0:
            best = d
    assert best is not None, (n, target)
    return best


@jax.custom_vjp
def _bdot(a, b):
    return jnp.dot(a.astype(BF16), b.astype(BF16), preferred_element_type=F32)


def _bdot_fwd(a, b):
    return _bdot(a, b), (a, b)


def _bdot_bwd(res, g):
    a, b = res
    gb = g.astype(BF16)
    da = lax.dot_general(gb, b.astype(BF16), (((1,), (1,)), ((), ())), preferred_element_type=F32)
    db = lax.dot_general(a.astype(BF16), gb, (((0,), (0,)), ((), ())), preferred_element_type=F32)
    return da.astype(a.dtype), db.astype(b.dtype)


_bdot.defvjp(_bdot_fwd, _bdot_bwd)


@functools.partial(jax.custom_vjp, nondiff_argnums=(1,))
def _lane_roll(x, shift):
    return pltpu.roll(x, shift % x.shape[1], 1)


def _lane_roll_fwd(x, shift):
    return _lane_roll(x, shift), None


def _lane_roll_bwd(shift, _, g):
    return (_lane_roll(g, -shift),)


_lane_roll.defvjp(_lane_roll_fwd, _lane_roll_bwd)


def _sigmoid(x):
    return 1.0 / (1.0 + jnp.exp(-x))


def _gelu(x):
    return 0.5 * x * (1.0 + jnp.tanh(math.sqrt(2.0 / math.pi) * (x + 0.044715 * (x * x * x))))


def _layer_norm(x, g, b):
    mu = jnp.mean(x, axis=-1, keepdims=True)
    var = jnp.mean(jnp.square(x - mu), axis=-1, keepdims=True)
    return (x - mu) * lax.rsqrt(var + LN_EPS) * g + b


def _rms_norm(x, g):
    return x * lax.rsqrt(jnp.mean(x * x, axis=-1, keepdims=True) + RMS_EPS) * g


def _rope(x, c, s1, s2):
    half = QK_ROPE // 2
    return x * c + _lane_roll(x, -half) * s1 + _lane_roll(x, half) * s2


def _modulate_fn(x, scale_row, shift_row):
    return (x * scale_row + shift_row,)


def _ln_res_fn(x, y, gate_row, g, b):
    return (_layer_norm(DEEPNORM_ALPHA * x + gate_row * y, g, b),)


def _s5_post_fn(ylin, u, d, w_glu, b_glu):
    z = _gelu(ylin + d * u)
    return (z * _sigmoid(_bdot(z, w_glu) + b_glu),)


def _mla_pre_fn(blk, cq_t, sq1, sq2, ck_t, sk1, sk2, q_norm, w_q, kv_norm, w_k, w_v):
    cq, ckv, kpe = blk[:, :Q_LORA], blk[:, Q_LORA:Q_LORA + KV_LORA], blk[:, Q_LORA + KV_LORA:]
    q = _rope(_bdot(_rms_norm(cq, q_norm), w_q), cq_t, sq1, sq2) * Q_SCALE
    ckv_n = _rms_norm(ckv, kv_norm)
    kpe_r = _lane_roll(_rope(kpe, ck_t, sk1, sk2), QK_NOPE)
    k = _bdot(ckv_n, w_k) + jnp.concatenate([kpe_r] * MLA_HEADS, axis=1)
    return q, k, _bdot(ckv_n, w_v)


def _pad_heads(w, width):
    w3 = w.reshape(w.shape[0], MLA_HEADS, width)
    return jnp.pad(w3, ((0, 0), (0, 0), (0, V7X_LANES - width))).reshape(w.shape[0], MLA_HEADS * V7X_LANES)


def _mla_weights(w_q_up, w_kv_up):
    kv3 = w_kv_up.reshape(w_kv_up.shape[0], MLA_HEADS, QK_NOPE + V_HEAD)
    w_k = _pad_heads(kv3[:, :, :QK_NOPE].reshape(w_kv_up.shape[0], -1), QK_NOPE)
    return _pad_heads(w_q_up, QK_NOPE + QK_ROPE), w_k, kv3[:, :, QK_NOPE:].reshape(w_kv_up.shape[0], -1)


def _mla_weight_grads(dw_q, dw_k, dw_v):
    unpad = lambda a, width: a.reshape(a.shape[0], MLA_HEADS, V7X_LANES)[:, :, :width]
    dkv = jnp.concatenate([unpad(dw_k, QK_NOPE), dw_v.reshape(dw_v.shape[0], MLA_HEADS, V_HEAD)], axis=2)
    return unpad(dw_q, QK_NOPE + QK_ROPE).reshape(dw_q.shape[0], -1), dkv.reshape(dw_k.shape[0], -1)


def _sgu_fn(u, v, g, b, wm, bias):
    vn = _layer_norm(_gelu(v), g, b)
    w = SGU_CHUNK
    parts = [_bdot(wm[k * w:(k + 1) * w, :], vn[:, k * w:(k + 1) * w]) for k in range(SGU_GROUPS)]
    return (_gelu(u) * (jnp.concatenate(parts, axis=1) + bias),)


def _merge_fn(y0, y1, y2, l0, l1, l2, wb):
    n = S5_WIDTH
    return (_sigmoid(l0) * _bdot(y0, wb[:n]) + _sigmoid(l1) * _bdot(y1, wb[n:2 * n])
            + _sigmoid(l2) * _bdot(y2, wb[2 * n:]),)


def _swiglu_fn(a, b):
    return (a * _sigmoid(a) * b,)


def _rowcall(fn, rows, fulls, out_rows, out_reds=(), *, tm, name):
    rows = [r if isinstance(r, tuple) else (r, r.shape[1], 0) for r in rows]
    t = rows[0][0].shape[0]
    tm = _pick_rows(t, tm)
    n_in, n_or, n_red = len(rows) + len(fulls), len(out_rows), len(out_reds)

    def body(*refs):
        vals = fn(*[r[...] for r in refs[:n_in]])
        assert len(vals) == n_or + n_red, (name, len(vals))
        for ref, v in zip(refs[n_in:n_in + n_or], vals[:n_or]):
            ref[...] = v.astype(ref.dtype)
        if n_red:
            red_refs = refs[n_in + n_or:]

            @pl.when(pl.program_id(0) == 0)
            def _():
                for ref in red_refs:
                    ref[...] = jnp.zeros_like(ref)

            for ref, v in zip(red_refs, vals[n_or:]):
                ref[...] += v.astype(ref.dtype)

    in_specs = [pl.BlockSpec((tm, w), functools.partial(lambda i, blk: (i, blk), blk=blk)) for _, w, blk in rows]
    in_specs += [pl.BlockSpec(f.shape, lambda i: (0, 0)) for f in fulls]
    out_specs = [pl.BlockSpec((tm, c), lambda i: (i, 0)) for c, _ in out_rows]
    out_specs += [pl.BlockSpec(s, lambda i: (0, 0)) for s, _ in out_reds]
    out_shape = [jax.ShapeDtypeStruct((t, c), dt) for c, dt in out_rows]
    out_shape += [jax.ShapeDtypeStruct(s, dt) for s, dt in out_reds]
    return pl.pallas_call(
        body, name=name, grid=(t // tm,), in_specs=in_specs, out_specs=out_specs, out_shape=out_shape,
        compiler_params=_cparams("arbitrary" if n_red else "parallel"),
    )(*[r[0] for r in rows], *fulls)


def _rowcall_vjp(fn, rows, fulls, cots, diff_rows, diff_fulls, *, tm, name, row_dtypes=None):
    rows_n = [r if isinstance(r, tuple) else (r, r.shape[1], 0) for r in rows]
    n_r, n_c = len(rows), len(cots)
    row_dtypes = row_dtypes or [F32] * len(diff_rows)

    def fn2(*vals):
        r = [v.astype(F32) for v in vals[:n_r]]
        ct = vals[n_r:n_r + n_c]
        f = [v.astype(F32) for v in vals[n_r + n_c:]]

        def g(*dargs):
            rr, ff = list(r), list(f)
            for k, idx in enumerate(diff_rows):
                rr[idx] = dargs[k]
            for k, idx in enumerate(diff_fulls):
                ff[idx] = dargs[len(diff_rows) + k]
            return fn(*rr, *ff)

        prim = [r[i] for i in diff_rows] + [f[i] for i in diff_fulls]
        outs, pull = jax.vjp(g, *prim)
        return pull(tuple(c.astype(o.dtype) for c, o in zip(ct, outs)))

    out_rows = [(rows_n[i][1], dt) for i, dt in zip(diff_rows, row_dtypes)]
    out_reds = [(fulls[i].shape, F32) for i in diff_fulls]
    return _rowcall(fn2, list(rows) + list(cots), fulls, out_rows, out_reds, tm=tm, name=name)


def _mm(a, b, *, ta=False, tb=False, bias=None, out_dtype=F32, name, tm=1024, tn=1024, tk=1024):
    (k_a, m) = a.shape if ta else a.shape[::-1]
    (n, k_b) = b.shape if tb else b.shape[::-1]
    assert k_a == k_b, (name, a.shape, b.shape)
    tm, tn, tk = _pick(m, tm) if m % V7X_LANES == 0 else m, _pick(n, tn), _pick(k_a, tk) if k_a % V7X_LANES == 0 else k_a
    nk = k_a // tk
    a_spec = pl.BlockSpec((tk, tm), lambda i, j, k: (k, i)) if ta else pl.BlockSpec((tm, tk), lambda i, j, k: (i, k))
    b_spec = pl.BlockSpec((tn, tk), lambda i, j, k: (j, k)) if tb else pl.BlockSpec((tk, tn), lambda i, j, k: (k, j))
    dims = (((0,) if ta else (1,), (1,) if tb else (0,)), ((), ()))
    has_bias = bias is not None

    def body(*refs):
        a_ref, b_ref = refs[0], refs[1]
        part = lax.dot_general(a_ref[...].astype(BF16), b_ref[...].astype(BF16), dims, preferred_element_type=F32)
        if nk == 1:
            o_ref = refs[-1]
            o_ref[...] = (part + refs[2][...] if has_bias else part).astype(o_ref.dtype)
            return
        o_ref, acc_ref = refs[-2], refs[-1]
        k = pl.program_id(2)

        @pl.when(k == 0)
        def _():
            acc_ref[...] = part

        @pl.when(k > 0)
        def _():
            acc_ref[...] += part

        @pl.when(k == nk - 1)
        def _():
            r = acc_ref[...]
            if has_bias:
                r = r + refs[2][...]
            o_ref[...] = r.astype(o_ref.dtype)

    in_specs = [a_spec, b_spec] + ([pl.BlockSpec((1, tn), lambda i, j, k: (0, j))] if has_bias else [])
    return pl.pallas_call(
        body, name=name, grid=(m // tm, n // tn, nk), in_specs=in_specs,
        out_specs=pl.BlockSpec((tm, tn), lambda i, j, k: (i, j)),
        out_shape=jax.ShapeDtypeStruct((m, n), out_dtype),
        scratch_shapes=[pltpu.VMEM((tm, tn), F32)] if nk > 1 else [],
        compiler_params=_cparams("parallel", "parallel", "arbitrary"),
    )(a, b, *([bias] if has_bias else []))


def _mm_swiglu(x, w, *, name, tm=512, tn=1408):
    m, k = x.shape
    hdim = w.shape[1] // 2
    tm, tn = _pick(m, tm), _pick(hdim, tn)
    nj = hdim // tn

    def body(x_ref, wa_ref, wb_ref, a_ref, b_ref, act_ref):
        xb = x_ref[...].astype(BF16)
        a = jnp.dot(xb, wa_ref[...].astype(BF16), preferred_element_type=F32)
        b = jnp.dot(xb, wb_ref[...].astype(BF16), preferred_element_type=F32)
        a_ref[...] = a
        b_ref[...] = b
        act_ref[...] = _swiglu_fn(a, b)[0].astype(act_ref.dtype)

    tile = pl.BlockSpec((tm, tn), lambda i, j: (i, j))
    return pl.pallas_call(
        body, name=name, grid=(m // tm, nj),
        in_specs=[pl.BlockSpec((tm, k), lambda i, j: (i, 0)), pl.BlockSpec((k, tn), lambda i, j: (0, j)),
                  pl.BlockSpec((k, tn), lambda i, j: (0, j + nj))],
        out_specs=[tile, tile, tile],
        out_shape=[jax.ShapeDtypeStruct((m, hdim), F32), jax.ShapeDtypeStruct((m, hdim), F32),
                   jax.ShapeDtypeStruct((m, hdim), BF16)],
        compiler_params=_cparams("parallel", "parallel"),
    )(x, w, w)


def _seg_order(a):
    t, c = a.shape
    return a.reshape(V7X_SUBLANES, t // V7X_SUBLANES, c).transpose(1, 0, 2).reshape(t, c)


def _time_order(a):
    t, c = a.shape
    return a.reshape(t // V7X_SUBLANES, V7X_SUBLANES, c).transpose(1, 0, 2).reshape(t, c)


def _scan_in_place(o_ref, tab_ref, reverse, h_ref=None, da_ref=None):
    ntile = o_ref.shape[0] // V7X_SUBLANES
    with_da = h_ref is not None
    ln = SCAN_LANES
    rows8 = V7X_SUBLANES
    subs = range(o_ref.shape[1] // (2 * ln))
    col = lambda s, part: slice((2 * s + part) * ln, (2 * s + part + 1) * ln)
    cmul = lambda p, q: (p[0] * q[0] - p[1] * q[1], p[0] * q[1] + p[1] * q[0])
    cadd = lambda p, q: (p[0] + q[0], p[1] + q[1])

    def run():
        row = lax.broadcasted_iota(jnp.int32, (rows8, ln), 0)
        bcast = lambda k, s: tuple(jnp.broadcast_to(tab_ref[k:k + 1, col(s, part)], (rows8, ln)) for part in (0, 1))
        a = [bcast(0, s) for s in subs]
        zero = [(jnp.zeros((rows8, ln), F32), jnp.zeros((rows8, ln), F32)) for _ in subs]

        def tile(ref, i):
            r0 = pl.multiple_of(i * rows8, rows8)
            return [(ref[pl.ds(r0, rows8), col(s, 0)], ref[pl.ds(r0, rows8), col(s, 1)]) for s in subs]

        def put(i, ys):
            r0 = pl.multiple_of(i * rows8, rows8)
            for s, (yr, yi) in enumerate(ys):
                o_ref[pl.ds(r0, rows8), col(s, 0)] = yr
                o_ref[pl.ds(r0, rows8), col(s, 1)] = yi

        def shifted(z, step, up):
            shift, keep = (rows8 - step, row < rows8 - step) if up else (step, row >= step)
            return tuple(jnp.where(keep, pltpu.roll(v, shift, 0), 0.0) for v in z)

        def pass1(n, carry):
            i = (ntile - 1 - n) if reverse else n
            ys = [cadd(x, cmul(a[s], carry[s])) for s, x in enumerate(tile(o_ref, i))]
            put(i, ys)
            return ys

        ends = lax.fori_loop(0, ntile, pass1, zero, unroll=4)
        incoming = []
        for s in subs:
            f = ends[s]
            for k, step in zip((1, 2, 3), (1, 2, 4)):
                f = cadd(f, cmul(bcast(k, s), shifted(f, step, reverse)))
            incoming.append(shifted(f, 1, reverse))
        if with_da:
            h0 = [shifted(h, 1, False) for h in tile(h_ref, ntile - 1)]

        def pass2(n, carry):
            i = (ntile - 1 - n) if reverse else n
            power = carry[0]
            ys = [cadd(y, cmul(power[s], incoming[s])) for s, y in enumerate(tile(o_ref, i))]
            put(i, ys)
            new_power = [cmul(power[s], a[s]) for s in subs]
            if not with_da:
                return (new_power,)
            prev = tile(h_ref, jnp.maximum(i - 1, 0))
            acc = []
            for s in subs:
                hpr, hpi = jnp.where(i > 0, prev[s][0], h0[s][0]), jnp.where(i > 0, prev[s][1], h0[s][1])
                (yr, yi), (sr, si) = ys[s], carry[1][s]
                acc.append((sr + yr * hpr + yi * hpi, si + yi * hpr - yr * hpi))
            return (new_power, acc)

        out = lax.fori_loop(0, ntile, pass2, (list(a),) + ((zero,) if with_da else ()), unroll=4)
        if with_da:
            for s in subs:
                da_ref[:, col(s, 0)] = out[1][s][0]
                da_ref[:, col(s, 1)] = out[1][s][1]

    run()


SCAN_BLOCK = 4 * SCAN_LANES
_S5_SUB = 2 * S5_CH // S5_BD // SCAN_BLOCK


def _s5_specs(t):
    ku, blk = S5_WIDTH // S5_BD, SCAN_BLOCK
    return dict(
        u=pl.BlockSpec((t, ku), lambda j: (0, j // _S5_SUB)),
        hs=pl.BlockSpec((t, blk), lambda j: (0, j)),
        wb=pl.BlockSpec((None, ku, blk), lambda j: (j // _S5_SUB, 0, j % _S5_SUB)),
        wc=pl.BlockSpec((None, blk, ku), lambda j: (j // _S5_SUB, j % _S5_SUB, 0)),
        tab=pl.BlockSpec((V7X_SUBLANES, blk), lambda j: (0, j)))


def _s5_fwd(u, wb, wc, tab, *, name):
    t = u.shape[0]
    sp = _s5_specs(t)

    def body(u_ref, wb_ref, wc_ref, tab_ref, hs_ref, y_ref):
        hs_ref[...] = jnp.dot(u_ref[...], wb_ref[...], preferred_element_type=F32)
        _scan_in_place(hs_ref, tab_ref, False)
        y = jnp.dot(hs_ref[...].astype(BF16), wc_ref[...], preferred_element_type=F32)
        first = pl.program_id(0) % _S5_SUB == 0

        @pl.when(first)
        def _():
            y_ref[...] = y

        @pl.when(jnp.logical_not(first))
        def _():
            y_ref[...] += y

    return pl.pallas_call(
        body, name=name, grid=(2 * S5_CH // SCAN_BLOCK,),
        in_specs=[sp['u'], sp['wb'], sp['wc'], sp['tab']], out_specs=[sp['hs'], sp['u']],
        out_shape=[jax.ShapeDtypeStruct((t, 2 * S5_CH), F32), jax.ShapeDtypeStruct((t, S5_WIDTH), F32)],
        compiler_params=_cparams("arbitrary"),
    )(u, wb, wc, tab)


def _s5_bwd(dy, u, hs, wb, wc, tab, *, name, side=None):
    t = u.shape[0]
    sp = _s5_specs(t)
    nt, tn = (((1,), (1,)), ((), ())), (((0,), (0,)), ((), ()))
    nsteps = 2 * S5_CH // SCAN_BLOCK

    def compute(refs):
        dy_ref, u_ref, hs_ref, wb_ref, wc_ref, tab_ref, du_ref, dwb_ref, dwc_ref, da_ref, g_ref = refs
        g_ref[...] = lax.dot_general(dy_ref[...], wc_ref[...], nt, preferred_element_type=F32)
        dwc_ref[...] = lax.dot_general(hs_ref[...].astype(BF16), dy_ref[...], tn, preferred_element_type=F32)
        _scan_in_place(g_ref, tab_ref, True, hs_ref, da_ref)
        gb = g_ref[...].astype(BF16)
        dwb_ref[...] = lax.dot_general(u_ref[...], gb, tn, preferred_element_type=F32)
        du = lax.dot_general(gb, wb_ref[...], nt, preferred_element_type=F32)
        first = pl.program_id(0) % _S5_SUB == 0

        @pl.when(first)
        def _():
            du_ref[...] = du

        @pl.when(jnp.logical_not(first))
        def _():
            du_ref[...] += du

    def body(*refs):
        _with_side(side, 6, 4, refs, pl.program_id(0), nsteps, compute)

    return _side_call(
        body, side, name=name, grid=(nsteps,),
        in_specs=[sp['u'], sp['u'], sp['hs'], sp['wb'], sp['wc'], sp['tab']],
        out_specs=[sp['u'], sp['wb'], sp['wc'], sp['tab']],
        out_shape=[jax.ShapeDtypeStruct((t, S5_WIDTH), F32), jax.ShapeDtypeStruct(wb.shape, F32),
                   jax.ShapeDtypeStruct(wc.shape, F32), jax.ShapeDtypeStruct((V7X_SUBLANES, 2 * S5_CH), F32)],
        scratch=[pltpu.VMEM((t, SCAN_BLOCK), F32)], args=[dy, u, hs, wb, wc, tab], sem=("arbitrary",))


ATT_SCALE = (QK_NOPE + QK_ROPE) ** -0.5
Q_SCALE = ATT_SCALE * math.log2(math.e)
LN2 = math.log(2.0)


def _att_mask(qi, kj, tb):
    qc = (qi * tb + lax.broadcasted_iota(jnp.int32, (tb, tb), 0)) // CHUNK
    kc = (kj * tb + lax.broadcasted_iota(jnp.int32, (tb, tb), 1)) // CHUNK
    return kc <= qc


def _with_side(side, n_main_in, n_main_out, refs, step, nsteps, compute):
    if side is None:
        compute(refs)
        return
    n = side.n
    main = refs[:n_main_in] + refs[n_main_in + n:n_main_in + n + n_main_out] + refs[n_main_in + 2 * n + n_main_out + 3:]
    x_refs = refs[n_main_in:n_main_in + n]
    y_refs = refs[n_main_in + n + n_main_out:n_main_in + 2 * n + n_main_out]
    side_refs = (x_refs, y_refs) + tuple(refs[n_main_in + 2 * n + n_main_out:n_main_in + 2 * n + n_main_out + 3])
    pl.when(step == 0)(functools.partial(side.start, *side_refs))
    compute(main)
    pl.when(step == (7 * nsteps) // 8)(functools.partial(side.relay, *side_refs))
    pl.when(step == nsteps - 1)(functools.partial(side.finish, *side_refs))


def _side_call(body, side, *, name, grid, in_specs, out_specs, out_shape, scratch, args, sem):
    n_out = len(out_shape)
    if side is not None:
        in_specs, args = in_specs + side.specs, list(args) + side.xs
        out_specs, out_shape = out_specs + side.specs, out_shape + side.out_shape
        scratch = side.scratch + scratch
        params = pltpu.CompilerParams(dimension_semantics=("arbitrary",) * len(grid), vmem_limit_bytes=VMEM_LIMIT,
                                      has_side_effects=True)
    else:
        params = _cparams(*sem)
    res = pl.pallas_call(body, name=name, grid=grid, in_specs=in_specs, out_specs=out_specs, out_shape=out_shape,
                         scratch_shapes=scratch, compiler_params=params)(*args)
    return res[:n_out], res[n_out:]


HEAD_PAIRS = MLA_HEADS // 2


def _attn_fwd(q, k, v, *, name, side=None):
    t = q.shape[0]
    hw = V7X_LANES
    tb = min(ATT_BLOCK, t)
    nblk = t // tb
    nt = (((1,), (1,)), ((), ()))

    def compute(refs):
        q_ref, k_ref, v_ref, o_ref, lse_ref = refs
        i = pl.program_id(1)
        qs = [q_ref[:, a * hw:(a + 1) * hw] for a in range(2)]

        def kv_step(j, carry, masked):
            r0 = pl.multiple_of(j * tb, tb)
            vb = v_ref[pl.ds(r0, tb), :]
            out = []
            for a in range(2):
                m, l, acc = carry[a]
                s = lax.dot_general(qs[a], k_ref[pl.ds(r0, tb), a * hw:(a + 1) * hw], nt, preferred_element_type=F32)
                if masked:
                    s = jnp.where(_att_mask(i, j, tb), s, NEG_INF)
                m_new = jnp.maximum(m, jnp.max(s, axis=1, keepdims=True))
                alpha = jnp.exp2(m - m_new)
                p = jnp.exp2(s - m_new)
                l = alpha * l + jnp.sum(p, axis=1, keepdims=True)
                out.append((m_new, l, alpha * acc + jnp.dot(p.astype(BF16), vb, preferred_element_type=F32)))
            return out

        init = [(jnp.full((tb, 1), NEG_INF, F32), jnp.zeros((tb, 1), F32), jnp.zeros((tb, hw), F32)) for _ in range(2)]
        carry = lax.fori_loop(0, i, functools.partial(kv_step, masked=False), init)
        (m0, l0, acc0), (m1, l1, acc1) = kv_step(i, carry, True)
        first = lax.broadcasted_iota(jnp.int32, (tb, hw), 1) < V_HEAD
        o_ref[...] = jnp.where(first, acc0 / l0, acc1 / l1)
        lse_ref[0] = m0 + jnp.log2(l0)
        lse_ref[1] = m1 + jnp.log2(l1)

    def body(*refs):
        _with_side(side, 3, 2, refs, pl.program_id(0) * nblk + pl.program_id(1), HEAD_PAIRS * nblk, compute)

    return _side_call(
        body, side, name=name, grid=(HEAD_PAIRS, nblk),
        in_specs=[pl.BlockSpec((tb, 2 * hw), lambda hp, i: (i, hp)), pl.BlockSpec((t, 2 * hw), lambda hp, i: (0, hp)),
                  pl.BlockSpec((t, hw), lambda hp, i: (0, hp))],
        out_specs=[pl.BlockSpec((tb, hw), lambda hp, i: (i, hp)), pl.BlockSpec((2, tb, 1), lambda hp, i: (hp, i, 0))],
        out_shape=[jax.ShapeDtypeStruct((t, MLA_HEADS * V_HEAD), F32), jax.ShapeDtypeStruct((MLA_HEADS, t, 1), F32)],
        scratch=[], args=[q, k, v], sem=("parallel", "parallel"))


def _attn_bwd(q, k, v, o, lse, do, *, name, side=None):
    t = q.shape[0]
    hw = V7X_LANES
    tb = min(ATT_BLOCK, t)
    nblk = t // tb
    nt = (((1,), (1,)), ((), ()))
    tn = (((0,), (0,)), ((), ()))

    def compute(refs):
        q_ref, k_ref, v_ref, o_ref, lse_ref, do_ref, dq_ref, dk_ref, dv_ref, delta_ref = refs
        j = pl.program_id(1)
        first = lax.broadcasted_iota(jnp.int32, (tb, hw), 1) < V_HEAD
        mine = [first, jnp.logical_not(first)]

        @pl.when(j == 0)
        def _():
            dq_ref[...] = jnp.zeros_like(dq_ref)

            def dstep(i, c):
                r0 = pl.multiple_of(i * tb, tb)
                prod = do_ref[pl.ds(r0, tb), :] * o_ref[pl.ds(r0, tb), :]
                for a in range(2):
                    delta_ref[a, pl.ds(r0, tb), :] = jnp.sum(jnp.where(mine[a], prod, 0.0), axis=1, keepdims=True)
                return c

            lax.fori_loop(0, nblk, dstep, 0)

        kb, vb = k_ref[...], v_ref[...]

        def q_step(i, carry, masked):
            dks, dv = carry
            r0 = pl.multiple_of(i * tb, tb)
            dob = do_ref[pl.ds(r0, tb), :].astype(BF16)
            new_dks = []
            for a in range(2):
                qa, ka = q_ref[pl.ds(r0, tb), a * hw:(a + 1) * hw], kb[:, a * hw:(a + 1) * hw]
                s = lax.dot_general(qa, ka, nt, preferred_element_type=F32)
                if masked:
                    s = jnp.where(_att_mask(i, j, tb), s, NEG_INF)
                p = jnp.exp2(s - lse_ref[a, pl.ds(r0, tb), :])
                doa = jnp.where(mine[a], dob, jnp.zeros_like(dob))
                dv = dv + lax.dot_general(p.astype(BF16), doa, tn, preferred_element_type=F32)
                dp = lax.dot_general(doa, vb, nt, preferred_element_type=F32)
                ds = (p * (dp - delta_ref[a, pl.ds(r0, tb), :]) * LN2).astype(BF16)
                new_dks.append(dks[a] + lax.dot_general(ds, qa, tn, preferred_element_type=F32))
                dq_ref[pl.ds(r0, tb), a * hw:(a + 1) * hw] += jnp.dot(ds, ka, preferred_element_type=F32)
            return new_dks, dv

        zero = jnp.zeros((tb, hw), F32)
        carry = q_step(j, ([zero, zero], zero), True)
        dks, dv = lax.fori_loop(j + 1, nblk, functools.partial(q_step, masked=False), carry)
        for a in range(2):
            dk_ref[:, a * hw:(a + 1) * hw] = dks[a]
        dv_ref[...] = dv

    def body(*refs):
        _with_side(side, 6, 3, refs, pl.program_id(0) * nblk + pl.program_id(1), HEAD_PAIRS * nblk, compute)

    whole = lambda w: pl.BlockSpec((t, w), lambda hp, j: (0, hp))
    blockj = lambda w: pl.BlockSpec((tb, w), lambda hp, j: (j, hp))
    return _side_call(
        body, side, name=name, grid=(HEAD_PAIRS, nblk),
        in_specs=[whole(2 * hw), blockj(2 * hw), blockj(hw), whole(hw), pl.BlockSpec((2, t, 1), lambda hp, j: (hp, 0, 0)),
                  whole(hw)],
        out_specs=[whole(2 * hw), blockj(2 * hw), blockj(hw)],
        out_shape=[jax.ShapeDtypeStruct(q.shape, F32), jax.ShapeDtypeStruct(k.shape, F32), jax.ShapeDtypeStruct(v.shape, F32)],
        scratch=[pltpu.VMEM((2, t, 1), F32)], args=[q, k, v, o, lse, do], sem=("parallel", "arbitrary"))


class _Exchange:
    def __init__(self, xs, gather, two_level=False):
        assert gather or not two_level
        self.xs, self.gather, self.n, self.two_level = list(xs), gather, len(xs), two_level
        shapes = [tuple(x.shape) if gather else tuple(x.shape[1:]) for x in xs]
        self.out_shape = [jax.ShapeDtypeStruct((N_DEV,) + shp, x.dtype) for shp, x in zip(shapes, xs)]
        self.specs = [pl.BlockSpec(memory_space=pl.ANY)] * self.n
        self.scratch = [pltpu.SemaphoreType.DMA((self.n, N_DEV - 1)), pltpu.SemaphoreType.DMA((self.n, N_DEV - 1)),
                        pltpu.SemaphoreType.DMA((self.n,))]

    def copies(self, x_refs, y_refs, send_sems, recv_sems, local_sems):
        mx, my, mc = lax.axis_index("x"), lax.axis_index("y"), lax.axis_index("c")
        me = 4 * mx + 2 * my + mc
        out = [pltpu.make_async_copy(x_refs[i] if self.gather else x_refs[i].at[me], y_refs[i].at[me], local_sems.at[i])
               for i in range(self.n)]
        for k in range(1, N_DEV):
            px = 1 - mx if k & 4 else mx
            py = 1 - my if k & 2 else my
            pc = 1 - mc if k & 1 else mc
            for i in range(self.n):
                out.append(pltpu.make_async_remote_copy(
                    src_ref=x_refs[i] if self.gather else x_refs[i].at[4 * px + 2 * py + pc], dst_ref=y_refs[i].at[me],
                    send_sem=send_sems.at[i, k - 1], recv_sem=recv_sems.at[i, k - 1],
                    device_id=(px, py, pc), device_id_type=pl.DeviceIdType.MESH))
        return out


    def _two_level(self, x_refs, y_refs, send_sems, recv_sems, local_sems):
        mx, my, mc = lax.axis_index("x"), lax.axis_index("y"), lax.axis_index("c")
        sib = (mx, my, 1 - mc)
        chips = [(1 - mx, my), (mx, 1 - my), (1 - mx, 1 - my)]
        idx = lambda px, py, pc: 4 * px + 2 * py + pc
        me = idx(mx, my, mc)

        def rc(i, k, src, block, to):
            return pltpu.make_async_remote_copy(
                src_ref=src, dst_ref=y_refs[i].at[block], send_sem=send_sems.at[i, k], recv_sem=recv_sems.at[i, k],
                device_id=to, device_id_type=pl.DeviceIdType.MESH)

        rng = range(self.n)
        over_ici = [(j, chip, i) for j, chip in enumerate(chips) for i in rng]
        return dict(
            local=lambda: [pltpu.make_async_copy(x_refs[i], y_refs[i].at[me], local_sems.at[i]) for i in rng],
            own=lambda: [rc(i, 0, x_refs[i], me, sib) for i in rng]
            + [rc(i, 1 + j, x_refs[i], me, (*chip, mc)) for j, chip, i in over_ici],
            relay=lambda: [rc(i, 4 + j, y_refs[i].at[idx(*chip, mc)], idx(*chip, mc), sib) for j, chip, i in over_ici],
            landed=lambda: [rc(i, 1 + j, x_refs[i], idx(*chip, mc), sib) for j, chip, i in over_ici],
            last=lambda: [rc(i, 0, x_refs[i], idx(*sib), sib) for i in rng]
            + [rc(i, 4 + j, x_refs[i], idx(*chip, 1 - mc), sib) for j, chip, i in over_ici])

    def start(self, *refs):
        if not self.two_level:
            for cp in self.copies(*refs):
                cp.start()
            return
        plan = self._two_level(*refs)
        for cp in plan['local']() + plan['own']():
            cp.start()

    def relay(self, *refs):
        if not self.two_level:
            return
        plan = self._two_level(*refs)
        for arrived, cp in zip(plan['landed'](), plan['relay']()):
            arrived.wait_recv()
            cp.start()

    def finish(self, *refs):
        if not self.two_level:
            for cp in self.copies(*refs):
                cp.wait()
            return
        plan = self._two_level(*refs)
        for cp in plan['last']():
            cp.wait_recv()
        for cp in plan['own']() + plan['relay']():
            cp.wait_send()
        for cp in plan['local']():
            cp.wait()


def _exchange(xs, *, gather, name, two_level=False):
    ex = _Exchange(xs, gather, two_level)
    n = ex.n

    def body(*refs):
        refs = (refs[:n], refs[n:2 * n]) + tuple(refs[2 * n:])
        ex.start(*refs)
        ex.relay(*refs)
        ex.finish(*refs)

    return pl.pallas_call(
        body, name=name, out_shape=ex.out_shape, in_specs=ex.specs, out_specs=ex.specs, scratch_shapes=ex.scratch,
        compiler_params=pltpu.CompilerParams(has_side_effects=True),
    )(*ex.xs)


def _adamw(w, gs, m, v, *, name, tm=256):
    nl = len(gs)
    parts = gs[0].ndim == 3
    c = w.shape[1]
    r = w.shape[0] // nl
    tm = _pick_rows(r, tm)
    nrow = r // tm

    def body(*refs):
        w_ref, g_refs, (m_ref, v_ref, go_ref, d_ref, mo_ref, vo_ref) = refs[0], refs[1:1 + nl], refs[1 + nl:]

        def update(g_ref):
            if parts:
                gv = g_ref[0].astype(F32)
                for k in range(1, N_DEV):
                    gv = gv + g_ref[k].astype(F32)
            else:
                gv = g_ref[...]
            mn = ADAM_B1 * m_ref[...] + (1.0 - ADAM_B1) * gv
            vn = ADAM_B2 * v_ref[...] + (1.0 - ADAM_B2) * jnp.square(gv)
            m_hat = mn / (1.0 - ADAM_B1 ** ADAM_STEP)
            v_hat = vn / (1.0 - ADAM_B2 ** ADAM_STEP)
            go_ref[...] = gv
            d_ref[...] = -ADAM_LR * (m_hat / (jnp.sqrt(v_hat) + ADAM_EPS) + ADAM_WD * w_ref[...])
            mo_ref[...] = mn
            vo_ref[...] = vn

        if nl == 1:
            update(g_refs[0])
        else:
            for layer, g_ref in enumerate(g_refs):
                pl.when(pl.program_id(0) == layer)(functools.partial(update, g_ref))

    spec = pl.BlockSpec((tm, c), lambda l, i: (l * nrow + i, 0))

    def gspec(layer):
        row = lambda l, i: jnp.where(l == layer, i, 0)
        if parts:
            return pl.BlockSpec((N_DEV, tm, c), lambda l, i: (0, row(l, i), 0))
        return pl.BlockSpec((tm, c), lambda l, i: (row(l, i), 0))

    return pl.pallas_call(
        body, name=name, grid=(nl, nrow), in_specs=[spec] + [gspec(k) for k in range(nl)] + [spec, spec],
        out_specs=[spec] * 4, out_shape=[jax.ShapeDtypeStruct(w.shape, F32)] * 4,
        compiler_params=_cparams("arbitrary", "arbitrary"),
    )(w, *gs, m, v)


def _sum_parts(x, *, name):
    def body(x_ref, o_ref):
        acc = x_ref[0]
        for k in range(1, N_DEV):
            acc = acc + x_ref[k]
        o_ref[...] = acc

    return pl.pallas_call(body, name=name, out_shape=jax.ShapeDtypeStruct(x.shape[1:], x.dtype))(x)


def _permute_in(a):
    pad = jnp.zeros(a.shape[:-1] + (IN_PAD - IN_WIDTH,), a.dtype)
    return jnp.concatenate([a[..., lo:hi] for lo, hi in _IN_SEGMENTS] + [pad], axis=-1)


def _unpermute_in(a):
    out, pos = {}, 0
    for lo, hi in _IN_SEGMENTS:
        out[lo] = a[..., pos:pos + hi - lo]
        pos += hi - lo
    return jnp.concatenate([out[lo] for lo in sorted(out)], axis=-1)


def _full_from_gathered(g, kind):
    if kind == 'row':
        return g.reshape((-1,) + g.shape[2:])
    if g.shape[-1] % V7X_LANES == 0:
        return jnp.moveaxis(g, 0, -2).reshape(g.shape[1:-1] + (-1,))
    return jnp.concatenate([g[d] for d in range(N_DEV)], axis=-1)


def _contrib_from_full(g, kind):
    if kind == 'row':
        return g.reshape((N_DEV, -1) + g.shape[1:])
    ns = g.shape[-1] // N_DEV
    if ns % V7X_LANES == 0:
        return jnp.moveaxis(g.reshape(g.shape[:-1] + (N_DEV, ns)), -2, 0)
    return jnp.stack([g[..., d * ns:(d + 1) * ns] for d in range(N_DEV)])


def _in_runs(ns):
    runs, pos = [], 0
    for lo, hi in _IN_SEGMENTS:
        for d in range(lo // ns, (hi - 1) // ns + 1):
            a, b = max(lo, d * ns), min(hi, (d + 1) * ns)
            runs.append((d, a - d * ns, b - d * ns, pos))
            pos += b - a
    return runs


def _w_in_from_gathered(g):
    pieces = [g[d][:, a:b] for d, a, b, _ in _in_runs(g.shape[2])]
    pad = jnp.zeros((g.shape[1], IN_PAD - IN_WIDTH), g.dtype)
    return jnp.concatenate(pieces + [pad], axis=1)


def _w_in_contrib(gp):
    ns = IN_WIDTH // N_DEV
    per_dev = [[] for _ in range(N_DEV)]
    for d, a, b, pos in sorted(_in_runs(ns), key=lambda r: (r[0], r[1])):
        per_dev[d].append(gp[:, pos:pos + b - a])
    return jnp.stack([jnp.concatenate(p, axis=1) for p in per_dev])


def _as2d(a, lead=0):
    return a.reshape(a.shape[:lead] + (-1, a.shape[-1]))


def _chan_cols(re, im):
    lead = re.shape[:-1]
    nb = S5_CH // SCAN_LANES
    return jnp.stack([re.reshape(lead + (nb, SCAN_LANES)), im.reshape(lead + (nb, SCAN_LANES))],
                     axis=-2).reshape(lead + (2 * S5_CH,))


def _s5_tables(lam_re, lam_im, log_dt, b_re, b_im, c_re, c_im):
    dt = jnp.exp(log_dt)[:, None]
    mag = jnp.exp(lam_re * dt)
    a_re = mag * jnp.cos(lam_im * dt)
    a_im = mag * jnp.sin(lam_im * dt)
    den = lam_re * lam_re + lam_im * lam_im
    f_re = ((a_re - 1.0) * lam_re + a_im * lam_im) / den
    f_im = (a_im * lam_re - (a_re - 1.0) * lam_im) / den
    bb_re = f_re[..., None] * b_re - f_im[..., None] * b_im
    bb_im = f_re[..., None] * b_im + f_im[..., None] * b_re
    gb = S5_GROUPS // S5_BD
    eye = jnp.eye(gb, dtype=F32)
    blocks = lambda a: a.reshape((S5_BD, gb) + a.shape[1:])

    def cols(re, im):
        shp = (S5_BD, S5_WIDTH // S5_BD, -1, SCAN_LANES)
        return jnp.stack([re.reshape(shp), im.reshape(shp)], axis=-2).reshape(S5_BD, S5_WIDTH // S5_BD, -1)

    flat = lambda a: a.reshape(S5_BD, S5_WIDTH // S5_BD, -1)
    wb_c = cols(flat(jnp.einsum('kgpc,gh->kgchp', blocks(bb_re), eye)), flat(jnp.einsum('kgpc,gh->kgchp', blocks(bb_im), eye)))
    wc_c = cols(flat(jnp.einsum('kgcp,gh->khcgp', blocks(c_re), eye)), -flat(jnp.einsum('kgcp,gh->khcgp', blocks(c_im), eye)))
    a_row = _chan_cols(a_re.reshape(1, S5_CH), a_im.reshape(1, S5_CH))
    return wb_c, wc_c.transpose(0, 2, 1), a_row


def _scan_tables(a_row, conj, seg_len):
    nb = S5_CH // SCAN_LANES
    a = a_row.reshape(nb, 2, SCAN_LANES)
    base = (a[:, 0], -a[:, 1] if conj else a[:, 1])
    mul = lambda x, y: (x[0] * y[0] - x[1] * y[1], x[0] * y[1] + x[1] * y[0])
    seg, sq, e = None, base, seg_len
    while e:
        if e & 1:
            seg = sq if seg is None else mul(seg, sq)
        sq, e = mul(sq, sq), e >> 1
    seg2 = mul(seg, seg)
    rows = [base, seg, seg2, mul(seg2, seg2)]
    lay = lambda z: jnp.stack([z[0], z[1]], axis=1).reshape(-1)
    return jnp.stack([lay(z) for z in rows] + [jnp.zeros((2 * S5_CH,), F32)] * (V7X_SUBLANES - len(rows)))


def _rope_tables(t):
    half = QK_ROPE // 2
    inv_freq = 1.0 / (ROPE_THETA ** (jnp.arange(0, QK_ROPE, 2, dtype=F32) / QK_ROPE))
    ang = jnp.arange(t, dtype=F32)[:, None] * inv_freq[None, :]
    cos, sin = jnp.cos(ang), jnp.sin(ang)
    zero = jnp.zeros_like(sin)

    def lay(nope, width, first, second, pad=0):
        head = jnp.concatenate([jnp.full((t, nope), 1.0 if first is cos else 0.0, F32), first, second,
                                jnp.zeros((t, pad), F32)], axis=1)
        reps = width // head.shape[1]
        out = jnp.tile(head, (1, reps))
        return jnp.pad(out, ((0, 0), (0, width - out.shape[1])))

    hq, pad = MLA_HEADS * V7X_LANES, V7X_LANES - QK_NOPE - QK_ROPE
    q_tabs = (lay(QK_NOPE, hq, cos, cos, pad), lay(QK_NOPE, hq, -sin, zero, pad), lay(QK_NOPE, hq, zero, sin, pad))
    k_tabs = (lay(0, V7X_LANES, cos, cos)[:, :V7X_LANES] * (jnp.arange(V7X_LANES) < QK_ROPE),
              lay(0, V7X_LANES, -sin, zero) * (jnp.arange(V7X_LANES) < QK_ROPE),
              lay(0, V7X_LANES, zero, sin) * (jnp.arange(V7X_LANES) < QK_ROPE))
    return q_tabs, k_tabs


def _sgu_tables(w_s, b_s):
    pos = jnp.arange(SGU_CHUNK) // CHUNK
    mask = pos[None, :] <= pos[:, None]
    wm = jnp.where(mask[None], w_s, 0.0).reshape(SGU_GROUPS * SGU_CHUNK, SGU_CHUNK)
    bias = jnp.repeat(b_s.T, SGU_WIDTH // SGU_GROUPS, axis=1)
    return wm, bias


def _row(v):
    return v.reshape(1, -1)


_S5_PARAMS = ('s5_lambda_re', 's5_lambda_im', 's5_log_dt', 's5_b_re', 's5_b_im', 's5_c_re', 's5_c_im')


def _derived_tables(p, t):
    (wb, wc, a_row), s5_pull = jax.vjp(jax.vmap(_s5_tables), *[p[n] for n in _S5_PARAMS])
    (wm, bias), sgu_pull = jax.vjp(jax.vmap(_sgu_tables), p['sgu_w_s'], p['sgu_b_s'])
    tab_f = jax.vmap(lambda a: _scan_tables(a, False, t // V7X_SUBLANES))(a_row)
    tab_b = jax.vmap(lambda a: _scan_tables(a, True, t // V7X_SUBLANES))(a_row)
    wb, wc = wb.astype(BF16), wc.astype(BF16)
    per_layer = [dict(s5_wb=wb[l], s5_wc=wc[l], s5_tab_fwd=tab_f[l], s5_tab_bwd=tab_b[l], sgu_wm=wm[l], sgu_bias=bias[l])
                 for l in range(len(wm))]

    def pull(grads):
        stacked = lambda k: jnp.stack([g[k] for g in grads])
        out = dict(zip(_S5_PARAMS, s5_pull((stacked('s5_wb'), stacked('s5_wc'), stacked('s5_a')))))
        out['sgu_w_s'], out['sgu_b_s'] = sgu_pull((stacked('sgu_wm'), stacked('sgu_bias')))
        return out

    return per_layer, pull


def _layer_fwd(x, ada, w, rope_tabs, tag, side=None, after_attn=None):
    s = {'x': x}
    q_tabs, k_tabs = rope_tabs
    sc1, gt1, sc2, gt2 = _row(1.0 + ada[1]), _row(1.0 + ada[2]), _row(1.0 + ada[4]), _row(1.0 + ada[5])
    s.update(sc1=sc1, gt1=gt1, sc2=sc2, gt2=gt2)
    (h,) = _rowcall(_modulate_fn, [x], [sc1, _row(ada[0])], [(D_MODEL, BF16)], tm=512, name=f"mod1_{tag}")
    proj = _mm(h, w['w_in_p'], bias=w['b_in_p'], name=f"proj_{tag}", tn=1792)
    s.update(h=h, proj=proj)

    u_view = (proj, S5_WIDTH, P_S5 // S5_WIDTH)
    u_seg = _seg_order(proj[:, P_S5:P_S5 + S5_WIDTH]).astype(BF16)
    hs, ylin = _s5_fwd(u_seg, w['s5_wb'], w['s5_wc'], w['s5_tab_fwd'], name=f"s5_fwd_{tag}")
    ylin = _time_order(ylin)
    s5_full = [_row(w['s5_d']), w['s5_w_glu'], _row(w['s5_b_glu'])]
    (y_s5,) = _rowcall(_s5_post_fn, [ylin, u_view], s5_full, [(S5_WIDTH, BF16)], tm=256, name=f"s5_post_{tag}")
    s.update(u_seg=u_seg, hs=hs, ylin=ylin, y_s5=y_s5)

    mla_rows = [(proj, MLA_BLK, P_MLA // MLA_BLK), *q_tabs, *k_tabs]
    mla_full = [_row(w['mla_q_norm']), w['mla_wq'], _row(w['mla_kv_norm']), w['mla_wk'], w['mla_wv']]
    hq, hv = MLA_HEADS * V7X_LANES, MLA_HEADS * V_HEAD
    q_r, k_r, v_r = _rowcall(_mla_pre_fn, mla_rows, mla_full, [(hq, BF16), (hq, BF16), (hv, BF16)],
                             tm=256, name=f"mla_pre_{tag}")
    (y_mla, lse), side_out = _attn_fwd(q_r, k_r, v_r, name=f"attn_fwd_{tag}", side=side)
    if after_attn is not None:
        after_attn(side_out)
    s.update(q_r=q_r, k_r=k_r, v_r=v_r, lse=lse, y_mla=y_mla, mla_full=mla_full)

    sgu_rows = [(proj, SGU_WIDTH, P_USGU // SGU_WIDTH), (proj, SGU_WIDTH, P_VSGU // SGU_WIDTH)]
    sgu_full = [_row(w['sgu_ln_g']), _row(w['sgu_ln_b']), w['sgu_wm'], w['sgu_bias']]
    (y_sgu,) = _rowcall(_sgu_fn, sgu_rows, sgu_full, [(SGU_WIDTH, BF16)], tm=SGU_CHUNK, name=f"sgu_{tag}")
    s.update(sgu_full=sgu_full, y_sgu=y_sgu)

    wbr = w['w_branch'].reshape(-1, D_MODEL)
    gate_rows = [(proj, D_MODEL, b) for b in range(3)]
    (merged,) = _rowcall(_merge_fn, [y_s5, y_mla, y_sgu] + gate_rows, [wbr], [(D_MODEL, BF16)], tm=256, name=f"merge_{tag}")
    ymix = _mm(merged, w['w_out'], name=f"wout_{tag}")
    (x1,) = _rowcall(_ln_res_fn, [x, ymix], [gt1, _row(w['ln1_g']), _row(w['ln1_b'])], [(D_MODEL, F32)], tm=512,
                     name=f"ln1_{tag}")
    s.update(merged=merged, ymix=ymix, x1=x1)

    (h2,) = _rowcall(_modulate_fn, [x1], [sc2, _row(ada[3])], [(D_MODEL, BF16)], tm=512, name=f"mod2_{tag}")
    ff_a, ff_b, act = _mm_swiglu(h2, w['ffn_w_in'], name=f"ffn_in_{tag}")
    f = _mm(act, w['ffn_w_out'], name=f"ffn_out_{tag}", tk=2816)
    (x2,) = _rowcall(_ln_res_fn, [x1, f], [gt2, _row(w['ln2_g']), _row(w['ln2_b'])], [(D_MODEL, F32)], tm=512,
                     name=f"ln2_{tag}")
    s.update(h2=h2, ff_a=ff_a, ff_b=ff_b, act=act, f=f)
    return x2, s, side_out


def _mod_bwd_fn(x, dh, dxa, scale_row):
    return (dxa + dh * scale_row, jnp.sum(dh * x, axis=0, keepdims=True), jnp.sum(dh, axis=0, keepdims=True))


def _layer_bwd(dx2, s, w, rope_tabs, tag, make_side=None, make_side2=None):
    g = {}
    q_tabs, k_tabs = rope_tabs
    t = dx2.shape[0]
    ln_full = lambda gt, a, b: [gt, _row(w[a]), _row(w[b])]

    dx1_a, df, dgt2, g['ln2_g'], g['ln2_b'] = _rowcall_vjp(
        _ln_res_fn, [s['x1'], s['f']], ln_full(s['gt2'], 'ln2_g', 'ln2_b'), [dx2], [0, 1], [0, 1, 2],
        tm=512, name=f"ln2_bwd_{tag}", row_dtypes=[F32, BF16])
    dact = _mm(df, w['ffn_w_out'], tb=True, name=f"ffn_out_dx_{tag}", tn=1408)
    g['ffn_w_out'] = _mm(s['act'], df, ta=True, name=f"ffn_out_dw_{tag}", tm=1408)
    def swiglu_bwd_fn(a, b, d):
        _, pull = jax.vjp(_swiglu_fn, a, b)
        return (jnp.concatenate(pull((d,)), axis=1),)

    (dab,) = _rowcall(swiglu_bwd_fn, [s['ff_a'], s['ff_b'], dact], [], [(2 * FF_HIDDEN, BF16)],
                      tm=256, name=f"swiglu_bwd_{tag}")
    dh2 = _mm(dab, w['ffn_w_in'], tb=True, name=f"ffn_in_dx_{tag}", tk=1408)
    g['ffn_w_in'] = _mm(s['h2'], dab, ta=True, name=f"ffn_in_dw_{tag}", tn=1408)
    dx1, dsc2, dsh2 = _rowcall(_mod_bwd_fn, [s['x1'], dh2, dx1_a], [s['sc2']], [(D_MODEL, F32)],
                               [((1, D_MODEL), F32)] * 2, tm=512, name=f"mod2_bwd_{tag}")

    dx_a, dymix, dgt1, g['ln1_g'], g['ln1_b'] = _rowcall_vjp(
        _ln_res_fn, [s['x'], s['ymix']], ln_full(s['gt1'], 'ln1_g', 'ln1_b'), [dx1], [0, 1], [0, 1, 2],
        tm=512, name=f"ln1_bwd_{tag}", row_dtypes=[F32, BF16])
    dmerged = _mm(dymix, w['w_out'], tb=True, name=f"wout_dx_{tag}")
    g['w_out'] = _mm(s['merged'], dymix, ta=True, name=f"wout_dw_{tag}")

    proj = s['proj']
    wbr = w['w_branch'].reshape(-1, D_MODEL).astype(F32)
    gate_rows = [(proj, D_MODEL, b) for b in range(3)]
    dy_s5, dy_mla, dy_sgu, dl0, dl1, dl2, dwbr = _rowcall_vjp(
        _merge_fn, [s['y_s5'], s['y_mla'], s['y_sgu']] + gate_rows, [wbr], [dmerged], [0, 1, 2, 3, 4, 5], [0],
        tm=256, name=f"merge_bwd_{tag}")
    g['w_branch'] = dwbr.reshape(w['w_branch'].shape)

    sgu_rows = [(proj, SGU_WIDTH, P_USGU // SGU_WIDTH), (proj, SGU_WIDTH, P_VSGU // SGU_WIDTH)]
    du_sgu, dv_sgu, dlg, dlb, dwm, dbias = _rowcall_vjp(
        _sgu_fn, sgu_rows, s['sgu_full'], [dy_sgu], [0, 1], [0, 1, 2, 3], tm=SGU_CHUNK, name=f"sgu_bwd_{tag}")
    g['sgu_ln_g'], g['sgu_ln_b'] = dlg.reshape(-1), dlb.reshape(-1)
    g['sgu_wm'], g['sgu_bias'] = dwm, dbias

    (dq_r, dk_r, dv_r), side_out = _attn_bwd(s['q_r'], s['k_r'], s['v_r'], s['y_mla'], s['lse'], dy_mla,
                                             name=f"attn_bwd_{tag}", side=make_side(g) if make_side is not None else None)
    mla_rows = [(proj, MLA_BLK, P_MLA // MLA_BLK), *q_tabs, *k_tabs]
    dmla, dqn, dwq, dkvn, dwk, dwv = _rowcall_vjp(
        _mla_pre_fn, mla_rows, s['mla_full'], [dq_r, dk_r, dv_r], [0], [0, 1, 2, 3, 4], tm=256, name=f"mla_pre_bwd_{tag}")
    g['mla_w_q_up'], g['mla_w_kv_up'] = _mla_weight_grads(dwq, dwk, dwv)
    g['mla_q_norm'], g['mla_kv_norm'] = dqn.reshape(-1), dkvn.reshape(-1)

    s5_full = [_row(w['s5_d']), w['s5_w_glu'], _row(w['s5_b_glu'])]
    dylin, du_a, dd, g['s5_w_glu'], dbg = _rowcall_vjp(
        _s5_post_fn, [s['ylin'], (proj, S5_WIDTH, P_S5 // S5_WIDTH)], s5_full, [dy_s5], [0, 1], [0, 1, 2], tm=256,
        name=f"s5_post_bwd_{tag}", row_dtypes=[BF16, F32])
    g['s5_d'], g['s5_b_glu'] = dd.reshape(-1), dbg.reshape(-1)
    (du_b, g['s5_wb'], g['s5_wc'], da_part), side2_out = _s5_bwd(
        _seg_order(dylin), s['u_seg'], s['hs'], w['s5_wb'], w['s5_wc'], w['s5_tab_bwd'], name=f"s5_bwd_{tag}",
        side=make_side2(g) if make_side2 is not None else None)
    du_b = _time_order(du_b)
    g['s5_a'] = jnp.sum(da_part, axis=0, keepdims=True)

    def dproj_fn(g0, g1, g2, ua, ub, us, vs, ml):
        d = jnp.concatenate([g0, g1, g2, ua + ub, us, vs, ml], axis=1)
        return d, jnp.sum(d, axis=0, keepdims=True)

    dproj, db_in = _rowcall(dproj_fn, [dl0, dl1, dl2, du_a, du_b, du_sgu, dv_sgu, dmla], [], [(IN_PAD, BF16)],
                            [((1, IN_PAD), F32)], tm=256, name=f"dproj_{tag}")
    dh = _mm(dproj, w['w_in_p'], tb=True, name=f"proj_dx_{tag}", tk=1792)
    g['w_in_p'] = _mm(s['h'], dproj, ta=True, name=f"proj_dw_{tag}", tm=512, tn=1792)
    g['b_in'] = _unpermute_in(db_in).reshape(-1)
    dx, dsc1, dsh1 = _rowcall(_mod_bwd_fn, [s['x'], dh, dx_a], [s['sc1']], [(D_MODEL, F32)], [((1, D_MODEL), F32)] * 2,
                              tm=512, name=f"mod1_bwd_{tag}")
    d_ada = jnp.concatenate([dsh1, dsc1, dgt1, dsh2, dsc2, dgt2], axis=0)
    return dx, d_ada, g, side_out, side2_out


def _loss_fn(y, target):
    err = y - target
    return (err / D_MODEL, 0.5 * jnp.sum(jnp.sum(err * err, axis=1, keepdims=True), axis=0, keepdims=True) / D_MODEL)


def _step(p):
    me = 4 * lax.axis_index("x") + 2 * lax.axis_index("y") + lax.axis_index("c")
    x = p['x'][0]
    t = x.shape[0]
    rope_tabs = _rope_tables(t)

    (c_all,) = _exchange([jnp.broadcast_to(p['c'], (V7X_SUBLANES, D_MODEL))], gather=True, name="gather_c")
    c_all = c_all[:, 0, :]
    (c_act,) = _rowcall(lambda cc: (cc * _sigmoid(cc),), [c_all], [], [(D_MODEL, F32)], tm=N_DEV, name="c_silu")
    ncol = p['w_ada'].shape[2]
    b_ada_loc = lax.dynamic_slice_in_dim(p['b_ada'], me * ncol, ncol, axis=1)
    ada_cols = jnp.concatenate([_mm(c_act, p['w_ada'][l], bias=b_ada_loc[l:l + 1], name=f"ada_{l}") for l in range(DEPTH)])
    (ada_all,) = _exchange([ada_cols], gather=True, name="gather_ada")
    ada_all = ada_all.reshape(N_DEV, DEPTH, N_DEV, ncol)
    ada = lax.dynamic_index_in_dim(ada_all, me, axis=2, keepdims=False)
    ada = ada.transpose(1, 0, 2).reshape(DEPTH, 6, D_MODEL)

    mixer_w, ffn_w = SHARDED[:-2], SHARDED[-2:]

    def shards(l, group):
        return [p[n][l].astype(BF16) for n, _ in group]

    def contribs(g, group):
        return [(_w_in_contrib(g['w_in_p']) if n == 'w_in' else _contrib_from_full(g[n], kind)).astype(BF16)
                for n, kind in group]

    tables, pull_tables = _derived_tables(p, t)

    def mixer_weights(l, gathered):
        w = {n: _full_from_gathered(g, kind) for (n, kind), g in zip(mixer_w[1:], gathered[1:])}
        w['w_in_p'] = _w_in_from_gathered(gathered[0])
        w['b_in_p'] = _row(_permute_in(p['b_in'][l]))
        w['mla_wq'], w['mla_wk'], w['mla_wv'] = _mla_weights(w['mla_w_q_up'], w['mla_w_kv_up'])
        w.update(tables[l])
        for n in SMALL:
            if n != 'b_ada' and n != 'b_in':
                w[n] = p[n][l]
        return w

    saved, layers = [], []
    gathered = _exchange(shards(0, mixer_w), gather=True, name="gather_w_0", two_level=True)
    for l in range(DEPTH):
        w = mixer_weights(l, gathered)
        side = _Exchange(shards(l, ffn_w) + (shards(l + 1, mixer_w) if l + 1 < DEPTH else []), True, two_level=True)

        def add_ffn(res, w=w):
            for (n, kind), g in zip(ffn_w, res):
                w[n] = _full_from_gathered(g, kind)

        x, s, res = _layer_fwd(x, ada[l], w, rope_tabs, f"l{l}", side, add_ffn)
        gathered = res[len(ffn_w):]
        layers.append(w)
        saved.append(s)
    dy, loss_loc = _rowcall(_loss_fn, [x, p['loss_target'][0]], [], [(D_MODEL, F32)], [((1, 1), F32)], tm=256, name="loss")
    loss = lax.psum(loss_loc[0, 0], ("x", "y", "c"))

    w_in_g, other_w = mixer_w[:1], mixer_w[1:]
    d_ada, grads, landed = [None] * DEPTH, [None] * DEPTH, [[None, None, None] for _ in range(DEPTH)]
    dx, pending = dy, []
    for l in reversed(range(DEPTH)):
        make_side = lambda g, pending=pending: _Exchange(contribs(g, ffn_w) + pending, False)
        make_side2 = lambda g: _Exchange(contribs(g, other_w), False)
        dx, d_ada[l], grads[l], res, landed[l][1] = _layer_bwd(dx, saved[l], layers[l], rope_tabs, f"l{l}", make_side,
                                                               make_side2)
        landed[l][2] = res[:len(ffn_w)]
        if l + 1 < DEPTH:
            landed[l + 1][0] = res[len(ffn_w):]
        pending = contribs(grads[l], w_in_g)
    landed[0][0] = _exchange(pending, gather=False, name="scatter_g_0")
    landed = [list(a) + list(b) + list(c) for a, b, c in landed]
    d_ada = jnp.stack(d_ada).reshape(DEPTH, 6 * D_MODEL)

    (d_ada_all,) = _exchange([d_ada], gather=True, name="gather_dada")
    d_ada_cols = lax.dynamic_slice_in_dim(d_ada_all, me * ncol, ncol, axis=2)
    pad_b = ((0, V7X_LANES - N_DEV), (0, 0))
    c_act_p = jnp.pad(c_act, pad_b)
    g_w_ada = jnp.stack([_mm(c_act_p, jnp.pad(d_ada_cols[:, l], pad_b), ta=True, name=f"ada_dw_{l}") for l in range(DEPTH)])

    out = {}
    kinds = ('grad_', 'delta_', 'new_m_', 'new_v_')
    for i, (n, _) in enumerate(SHARDED):
        res = _adamw(_as2d(p[n]), [_as2d(landed[l][i], lead=1) for l in range(DEPTH)], _as2d(p['m_' + n]),
                     _as2d(p['v_' + n]), name=f"adamw_{n}")
        for kind, r in zip(kinds, res):
            out[kind + n] = r.reshape(p[n].shape)

    res = _adamw(_as2d(p['w_ada']), [_as2d(g_w_ada)], _as2d(p['m_w_ada']), _as2d(p['v_w_ada']), name="adamw_ada")
    for kind, r in zip(kinds, res):
        out[kind + 'w_ada'] = r.reshape(p['w_ada'].shape)

    small_g = pull_tables(grads)
    small_g.update({n: jnp.stack([grads[l][n] for l in range(DEPTH)]) for n in SMALL if n not in small_g and n != 'b_ada'})
    small_g['b_ada'] = d_ada
    n_small = sum(int(np.prod(p[n].shape)) for n in SMALL)
    n_pad = -(-n_small // SMALL_PAD) * SMALL_PAD
    flat = jnp.concatenate([small_g[n].reshape(-1) for n in SMALL])
    flat = jnp.pad(flat, (0, n_pad - n_small)).reshape(N_DEV, n_pad // N_DEV // 1024, 1024)
    (landed_small,) = _exchange([flat], gather=False, name="scatter_small")
    (g_all,) = _exchange([_sum_parts(landed_small, name="sum_small")], gather=True, name="gather_small")
    g_all = g_all.reshape(-1)
    off = 0
    for n in SMALL:
        size = int(np.prod(p[n].shape))
        g_n = g_all[off:off + size].reshape(p[n].shape)
        off += size
        res = _adamw(_as2d(p[n]), [_as2d(g_n)], _as2d(p['m_' + n]), _as2d(p['v_' + n]), name=f"adamw_{n}")
        for kind, r in zip(kinds, res):
            out[kind + n] = r.reshape(p[n].shape)

    outs = [loss, dx[None]]
    for kind in ('grad_', 'delta_', 'new_m_', 'new_v_'):
        outs += [out[kind + n] for n in WNAMES]
    return tuple(outs)


def kernel(x, c, w_ada, b_ada, w_in, b_in, s5_lambda_re, s5_lambda_im, s5_log_dt, s5_b_re, s5_b_im, s5_c_re, s5_c_im, s5_d, s5_w_glu, s5_b_glu, mla_q_norm, mla_w_q_up, mla_kv_norm, mla_w_kv_up, sgu_ln_g, sgu_ln_b, sgu_w_s, sgu_b_s, w_branch, w_out, ln1_g, ln1_b, ffn_w_in, ffn_w_out, ln2_g, ln2_b, loss_target, m_w_ada, m_b_ada, m_w_in, m_b_in, m_s5_lambda_re, m_s5_lambda_im, m_s5_log_dt, m_s5_b_re, m_s5_b_im, m_s5_c_re, m_s5_c_im, m_s5_d, m_s5_w_glu, m_s5_b_glu, m_mla_q_norm, m_mla_w_q_up, m_mla_kv_norm, m_mla_w_kv_up, m_sgu_ln_g, m_sgu_ln_b, m_sgu_w_s, m_sgu_b_s, m_w_branch, m_w_out, m_ln1_g, m_ln1_b, m_ffn_w_in, m_ffn_w_out, m_ln2_g, m_ln2_b, v_w_ada, v_b_ada, v_w_in, v_b_in, v_s5_lambda_re, v_s5_lambda_im, v_s5_log_dt, v_s5_b_re, v_s5_b_im, v_s5_c_re, v_s5_c_im, v_s5_d, v_s5_w_glu, v_s5_b_glu, v_mla_q_norm, v_mla_w_q_up, v_mla_kv_norm, v_mla_w_kv_up, v_sgu_ln_g, v_sgu_ln_b, v_sgu_w_s, v_sgu_b_s, v_w_branch, v_w_out, v_ln1_g, v_ln1_b, v_ffn_w_in, v_ffn_w_out, v_ln2_g, v_ln2_b):
    return _step(dict(locals()))
```

```python
import functools
import math

import numpy as np
import jax
import jax.numpy as jnp
from jax import lax
from jax.experimental import pallas as pl
from jax.experimental.pallas import tpu as pltpu

F32 = jnp.float32
BF16 = jnp.bfloat16

N_DEV = 8
D_MODEL = 1024
DEPTH = 4
CHUNK = 64
S5_WIDTH = 512
S5_GROUP = 16
S5_GROUPS = 32
S5_STATE = 64
MLA_HEADS = 8
QK_NOPE = 64
QK_ROPE = 32
V_HEAD = 64
Q_LORA = 384
KV_LORA = 256
ROPE_THETA = 10000.0
SGU_WIDTH = 512
SGU_GROUPS = 4
SGU_CHUNK = 128
FF_HIDDEN = 2816
DEEPNORM_ALPHA = (2 * DEPTH) ** 0.25
LN_EPS = 1e-5
RMS_EPS = 1e-6
NEG_INF = -1e30
ADAM_LR = 0.001
ADAM_B1 = 0.9
ADAM_B2 = 0.999
ADAM_EPS = 1e-08
ADAM_WD = 0.01
ADAM_STEP = 10

IN_WIDTH = 5280
IN_PAD = 5376
_O_S5, _O_CQ, _O_CKV, _O_KPE, _O_USGU, _O_VSGU, _O_GATE = 0, 512, 896, 1152, 1184, 1696, 2208
_IN_SEGMENTS = ((_O_GATE, IN_WIDTH), (_O_S5, _O_CQ), (_O_USGU, _O_VSGU), (_O_VSGU, _O_GATE), (_O_CQ, _O_USGU))
P_GATE, P_S5, P_USGU, P_VSGU, P_MLA = 0, 3072, 3584, 4096, 4608
MLA_BLK = 768

V7X_LANES = 128
V7X_SUBLANES = 8
VMEM_LIMIT = 56 * 1024 * 1024
ATT_BLOCK = 512
SCAN_LANES = 128
S5_CH = S5_GROUPS * S5_STATE
S5_BD = 4

WNAMES = ['w_ada', 'b_ada', 'w_in', 'b_in', 's5_lambda_re', 's5_lambda_im', 's5_log_dt', 's5_b_re', 's5_b_im',
          's5_c_re', 's5_c_im', 's5_d', 's5_w_glu', 's5_b_glu', 'mla_q_norm', 'mla_w_q_up', 'mla_kv_norm',
          'mla_w_kv_up', 'sgu_ln_g', 'sgu_ln_b', 'sgu_w_s', 'sgu_b_s', 'w_branch', 'w_out', 'ln1_g', 'ln1_b',
          'ffn_w_in', 'ffn_w_out', 'ln2_g', 'ln2_b']
SHARDED = (('w_in', 'col'), ('s5_w_glu', 'row'), ('mla_w_q_up', 'col'), ('mla_w_kv_up', 'col'),
           ('w_branch', 'col3'), ('w_out', 'row'), ('ffn_w_in', 'col'), ('ffn_w_out', 'row'))
SMALL = [n for n in WNAMES if n != 'w_ada' and n not in dict(SHARDED)]
SMALL_PAD = N_DEV * V7X_SUBLANES * 1024


def _cparams(*sem):
    return pltpu.CompilerParams(dimension_semantics=sem, vmem_limit_bytes=VMEM_LIMIT)


def _pick(n, target):
    if n <= target:
        return n
    best = None
    for d in range(V7X_LANES, target + 1, V7X_LANES):
        if n % d == 0:
            best = d
    assert best is not None, (n, target)
    return best


def _pick_rows(n, target):
    if n <= target:
        return n
    best = None
    for d in range(V7X_SUBLANES, target + 1, V7X_SUBLANES):
        if n % d == 0:
            best = d
    assert best is not None, (n, target)
    return best


@jax.custom_vjp
def _bdot(a, b):
    return jnp.dot(a.astype(BF16), b.astype(BF16), preferred_element_type=F32)


def _bdot_fwd(a, b):
    return _bdot(a, b), (a, b)


def _bdot_bwd(res, g):
    a, b = res
    gb = g.astype(BF16)
    da = lax.dot_general(gb, b.astype(BF16), (((1,), (1,)), ((), ())), preferred_element_type=F32)
    db = lax.dot_general(a.astype(BF16), gb, (((0,), (0,)), ((), ())), preferred_element_type=F32)
    return da.astype(a.dtype), db.astype(b.dtype)


_bdot.defvjp(_bdot_fwd, _bdot_bwd)


@functools.partial(jax.custom_vjp, nondiff_argnums=(1,))
def _lane_roll(x, shift):
    return pltpu.roll(x, shift % x.shape[1], 1)


def _lane_roll_fwd(x, shift):
    return _lane_roll(x, shift), None


def _lane_roll_bwd(shift, _, g):
    return (_lane_roll(g, -shift),)


_lane_roll.defvjp(_lane_roll_fwd, _lane_roll_bwd)


def _sigmoid(x):
    return 1.0 / (1.0 + jnp.exp(-x))


def _gelu(x):
    return 0.5 * x * (1.0 + jnp.tanh(math.sqrt(2.0 / math.pi) * (x + 0.044715 * (x * x * x))))


def _layer_norm(x, g, b):
    mu = jnp.mean(x, axis=-1, keepdims=True)
    var = jnp.mean(jnp.square(x - mu), axis=-1, keepdims=True)
    return (x - mu) * lax.rsqrt(var + LN_EPS) * g + b


def _rms_norm(x, g):
    return x * lax.rsqrt(jnp.mean(x * x, axis=-1, keepdims=True) + RMS_EPS) * g


def _rope(x, c, s1, s2):
    half = QK_ROPE // 2
    return x * c + _lane_roll(x, -half) * s1 + _lane_roll(x, half) * s2


def _modulate_fn(x, scale_row, shift_row):
    return (x * scale_row + shift_row,)


def _ln_res_fn(x, y, gate_row, g, b):
    return (_layer_norm(DEEPNORM_ALPHA * x + gate_row * y, g, b),)


def _s5_post_fn(ylin, u, d, w_glu, b_glu):
    z = _gelu(ylin + d * u)
    return (z * _sigmoid(_bdot(z, w_glu) + b_glu),)


def _mla_pre_fn(blk, cq_t, sq1, sq2, ck_t, sk1, sk2, q_norm, w_q, kv_norm, w_k, w_v):
    cq, ckv, kpe = blk[:, :Q_LORA], blk[:, Q_LORA:Q_LORA + KV_LORA], blk[:, Q_LORA + KV_LORA:]
    q = _rope(_bdot(_rms_norm(cq, q_norm), w_q), cq_t, sq1, sq2) * Q_SCALE
    ckv_n = _rms_norm(ckv, kv_norm)
    kpe_r = _lane_roll(_rope(kpe, ck_t, sk1, sk2), QK_NOPE)
    k = _bdot(ckv_n, w_k) + jnp.concatenate([kpe_r] * MLA_HEADS, axis=1)
    return q, k, _bdot(ckv_n, w_v)


def _pad_heads(w, width):
    w3 = w.reshape(w.shape[0], MLA_HEADS, width)
    return jnp.pad(w3, ((0, 0), (0, 0), (0, V7X_LANES - width))).reshape(w.shape[0], MLA_HEADS * V7X_LANES)


def _mla_weights(w_q_up, w_kv_up):
    kv3 = w_kv_up.reshape(w_kv_up.shape[0], MLA_HEADS, QK_NOPE + V_HEAD)
    w_k = _pad_heads(kv3[:, :, :QK_NOPE].reshape(w_kv_up.shape[0], -1), QK_NOPE)
    return _pad_heads(w_q_up, QK_NOPE + QK_ROPE), w_k, kv3[:, :, QK_NOPE:].reshape(w_kv_up.shape[0], -1)


def _mla_weight_grads(dw_q, dw_k, dw_v):
    unpad = lambda a, width: a.reshape(a.shape[0], MLA_HEADS, V7X_LANES)[:, :, :width]
    dkv = jnp.concatenate([unpad(dw_k, QK_NOPE), dw_v.reshape(dw_v.shape[0], MLA_HEADS, V_HEAD)], axis=2)
    return unpad(dw_q, QK_NOPE + QK_ROPE).reshape(dw_q.shape[0], -1), dkv.reshape(dw_k.shape[0], -1)


def _sgu_fn(u, v, g, b, wm, bias):
    vn = _layer_norm(_gelu(v), g, b)
    w = SGU_CHUNK
    parts = [_bdot(wm[k * w:(k + 1) * w, :], vn[:, k * w:(k + 1) * w]) for k in range(SGU_GROUPS)]
    return (_gelu(u) * (jnp.concatenate(parts, axis=1) + bias),)


def _merge_fn(y0, y1, y2, l0, l1, l2, wb):
    n = S5_WIDTH
    return (_sigmoid(l0) * _bdot(y0, wb[:n]) + _sigmoid(l1) * _bdot(y1, wb[n:2 * n])
            + _sigmoid(l2) * _bdot(y2, wb[2 * n:]),)


def _swiglu_fn(a, b):
    return (a * _sigmoid(a) * b,)


def _rowcall(fn, rows, fulls, out_rows, out_reds=(), *, tm, name):
    rows = [r if isinstance(r, tuple) else (r, r.shape[1], 0) for r in rows]
    t = rows[0][0].shape[0]
    tm = _pick_rows(t, tm)
    n_in, n_or, n_red = len(rows) + len(fulls), len(out_rows), len(out_reds)

    def body(*refs):
        vals = fn(*[r[...] for r in refs[:n_in]])
        assert len(vals) == n_or + n_red, (name, len(vals))
        for ref, v in zip(refs[n_in:n_in + n_or], vals[:n_or]):
            ref[...] = v.astype(ref.dtype)
        if n_red:
            red_refs = refs[n_in + n_or:]

            @pl.when(pl.program_id(0) == 0)
            def _():
                for ref in red_refs:
                    ref[...] = jnp.zeros_like(ref)

            for ref, v in zip(red_refs, vals[n_or:]):
                ref[...] += v.astype(ref.dtype)

    in_specs = [pl.BlockSpec((tm, w), functools.partial(lambda i, blk: (i, blk), blk=blk)) for _, w, blk in rows]
    in_specs += [pl.BlockSpec(f.shape, lambda i: (0, 0)) for f in fulls]
    out_specs = [pl.BlockSpec((tm, c), lambda i: (i, 0)) for c, _ in out_rows]
    out_specs += [pl.BlockSpec(s, lambda i: (0, 0)) for s, _ in out_reds]
    out_shape = [jax.ShapeDtypeStruct((t, c), dt) for c, dt in out_rows]
    out_shape += [jax.ShapeDtypeStruct(s, dt) for s, dt in out_reds]
    return pl.pallas_call(
        body, name=name, grid=(t // tm,), in_specs=in_specs, out_specs=out_specs, out_shape=out_shape,
        compiler_params=_cparams("arbitrary" if n_red else "parallel"),
    )(*[r[0] for r in rows], *fulls)


def _rowcall_vjp(fn, rows, fulls, cots, diff_rows, diff_fulls, *, tm, name, row_dtypes=None):
    rows_n = [r if isinstance(r, tuple) else (r, r.shape[1], 0) for r in rows]
    n_r, n_c = len(rows), len(cots)
    row_dtypes = row_dtypes or [F32] * len(diff_rows)

    def fn2(*vals):
        r = [v.astype(F32) for v in vals[:n_r]]
        ct = vals[n_r:n_r + n_c]
        f = [v.astype(F32) for v in vals[n_r + n_c:]]

        def g(*dargs):
            rr, ff = list(r), list(f)
            for k, idx in enumerate(diff_rows):
                rr[idx] = dargs[k]
            for k, idx in enumerate(diff_fulls):
                ff[idx] = dargs[len(diff_rows) + k]
            return fn(*rr, *ff)

        prim = [r[i] for i in diff_rows] + [f[i] for i in diff_fulls]
        outs, pull = jax.vjp(g, *prim)
        return pull(tuple(c.astype(o.dtype) for c, o in zip(ct, outs)))

    out_rows = [(rows_n[i][1], dt) for i, dt in zip(diff_rows, row_dtypes)]
    out_reds = [(fulls[i].shape, F32) for i in diff_fulls]
    return _rowcall(fn2, list(rows) + list(cots), fulls, out_rows, out_reds, tm=tm, name=name)


def _mm(a, b, *, ta=False, tb=False, bias=None, out_dtype=F32, name, tm=1024, tn=1024, tk=1024):
    (k_a, m) = a.shape if ta else a.shape[::-1]
    (n, k_b) = b.shape if tb else b.shape[::-1]
    assert k_a == k_b, (name, a.shape, b.shape)
    tm, tn, tk = _pick(m, tm) if m % V7X_LANES == 0 else m, _pick(n, tn), _pick(k_a, tk) if k_a % V7X_LANES == 0 else k_a
    nk = k_a // tk
    a_spec = pl.BlockSpec((tk, tm), lambda i, j, k: (k, i)) if ta else pl.BlockSpec((tm, tk), lambda i, j, k: (i, k))
    b_spec = pl.BlockSpec((tn, tk), lambda i, j, k: (j, k)) if tb else pl.BlockSpec((tk, tn), lambda i, j, k: (k, j))
    dims = (((0,) if ta else (1,), (1,) if tb else (0,)), ((), ()))
    has_bias = bias is not None

    def body(*refs):
        a_ref, b_ref = refs[0], refs[1]
        part = lax.dot_general(a_ref[...].astype(BF16), b_ref[...].astype(BF16), dims, preferred_element_type=F32)
        if nk == 1:
            o_ref = refs[-1]
            o_ref[...] = (part + refs[2][...] if has_bias else part).astype(o_ref.dtype)
            return
        o_ref, acc_ref = refs[-2], refs[-1]
        k = pl.program_id(2)

        @pl.when(k == 0)
        def _():
            acc_ref[...] = part

        @pl.when(k > 0)
        def _():
            acc_ref[...] += part

        @pl.when(k == nk - 1)
        def _():
            r = acc_ref[...]
            if has_bias:
                r = r + refs[2][...]
            o_ref[...] = r.astype(o_ref.dtype)

    in_specs = [a_spec, b_spec] + ([pl.BlockSpec((1, tn), lambda i, j, k: (0, j))] if has_bias else [])
    return pl.pallas_call(
        body, name=name, grid=(m // tm, n // tn, nk), in_specs=in_specs,
        out_specs=pl.BlockSpec((tm, tn), lambda i, j, k: (i, j)),
        out_shape=jax.ShapeDtypeStruct((m, n), out_dtype),
        scratch_shapes=[pltpu.VMEM((tm, tn), F32)] if nk > 1 else [],
        compiler_params=_cparams("parallel", "parallel", "arbitrary"),
    )(a, b, *([bias] if has_bias else []))


def _mm_swiglu(x, w, *, name, tm=512, tn=1408):
    m, k = x.shape
    hdim = w.shape[1] // 2
    tm, tn = _pick(m, tm), _pick(hdim, tn)
    nj = hdim // tn

    def body(x_ref, wa_ref, wb_ref, a_ref, b_ref, act_ref):
        xb = x_ref[...].astype(BF16)
        a = jnp.dot(xb, wa_ref[...].astype(BF16), preferred_element_type=F32)
        b = jnp.dot(xb, wb_ref[...].astype(BF16), preferred_element_type=F32)
        a_ref[...] = a
        b_ref[...] = b
        act_ref[...] = _swiglu_fn(a, b)[0].astype(act_ref.dtype)

    tile = pl.BlockSpec((tm, tn), lambda i, j: (i, j))
    return pl.pallas_call(
        body, name=name, grid=(m // tm, nj),
        in_specs=[pl.BlockSpec((tm, k), lambda i, j: (i, 0)), pl.BlockSpec((k, tn), lambda i, j: (0, j)),
                  pl.BlockSpec((k, tn), lambda i, j: (0, j + nj))],
        out_specs=[tile, tile, tile],
        out_shape=[jax.ShapeDtypeStruct((m, hdim), F32), jax.ShapeDtypeStruct((m, hdim), F32),
                   jax.ShapeDtypeStruct((m, hdim), BF16)],
        compiler_params=_cparams("parallel", "parallel"),
    )(x, w, w)


def _seg_order(a):
    t, c = a.shape
    return a.reshape(V7X_SUBLANES, t // V7X_SUBLANES, c).transpose(1, 0, 2).reshape(t, c)


def _time_order(a):
    t, c = a.shape
    return a.reshape(t // V7X_SUBLANES, V7X_SUBLANES, c).transpose(1, 0, 2).reshape(t, c)


def _scan_in_place(o_ref, tab_ref, reverse, h_ref=None, da_ref=None):
    ntile = o_ref.shape[0] // V7X_SUBLANES
    with_da = h_ref is not None
    ln = SCAN_LANES
    rows8 = V7X_SUBLANES
    subs = range(o_ref.shape[1] // (2 * ln))
    col = lambda s, part: slice((2 * s + part) * ln, (2 * s + part + 1) * ln)
    cmul = lambda p, q: (p[0] * q[0] - p[1] * q[1], p[0] * q[1] + p[1] * q[0])
    cadd = lambda p, q: (p[0] + q[0], p[1] + q[1])

    def run():
        row = lax.broadcasted_iota(jnp.int32, (rows8, ln), 0)
        bcast = lambda k, s: tuple(jnp.broadcast_to(tab_ref[k:k + 1, col(s, part)], (rows8, ln)) for part in (0, 1))
        a = [bcast(0, s) for s in subs]
        zero = [(jnp.zeros((rows8, ln), F32), jnp.zeros((rows8, ln), F32)) for _ in subs]

        def tile(ref, i):
            r0 = pl.multiple_of(i * rows8, rows8)
            return [(ref[pl.ds(r0, rows8), col(s, 0)], ref[pl.ds(r0, rows8), col(s, 1)]) for s in subs]

        def put(i, ys):
            r0 = pl.multiple_of(i * rows8, rows8)
            for s, (yr, yi) in enumerate(ys):
                o_ref[pl.ds(r0, rows8), col(s, 0)] = yr
                o_ref[pl.ds(r0, rows8), col(s, 1)] = yi

        def shifted(z, step, up):
            shift, keep = (rows8 - step, row < rows8 - step) if up else (step, row >= step)
            return tuple(jnp.where(keep, pltpu.roll(v, shift, 0), 0.0) for v in z)

        def pass1(n, carry):
            i = (ntile - 1 - n) if reverse else n
            ys = [cadd(x, cmul(a[s], carry[s])) for s, x in enumerate(tile(o_ref, i))]
            put(i, ys)
            return ys

        ends = lax.fori_loop(0, ntile, pass1, zero, unroll=4)
        incoming = []
        for s in subs:
            f = ends[s]
            for k, step in zip((1, 2, 3), (1, 2, 4)):
                f = cadd(f, cmul(bcast(k, s), shifted(f, step, reverse)))
            incoming.append(shifted(f, 1, reverse))
        if with_da:
            h0 = [shifted(h, 1, False) for h in tile(h_ref, ntile - 1)]

        def pass2(n, carry):
            i = (ntile - 1 - n) if reverse else n
            power = carry[0]
            ys = [cadd(y, cmul(power[s], incoming[s])) for s, y in enumerate(tile(o_ref, i))]
            put(i, ys)
            new_power = [cmul(power[s], a[s]) for s in subs]
            if not with_da:
                return (new_power,)
            prev = tile(h_ref, jnp.maximum(i - 1, 0))
            acc = []
            for s in subs:
                hpr, hpi = jnp.where(i > 0, prev[s][0], h0[s][0]), jnp.where(i > 0, prev[s][1], h0[s][1])
                (yr, yi), (sr, si) = ys[s], carry[1][s]
                acc.append((sr + yr * hpr + yi * hpi, si + yi * hpr - yr * hpi))
            return (new_power, acc)

        out = lax.fori_loop(0, ntile, pass2, (list(a),) + ((zero,) if with_da else ()), unroll=4)
        if with_da:
            for s in subs:
                da_ref[:, col(s, 0)] = out[1][s][0]
                da_ref[:, col(s, 1)] = out[1][s][1]

    run()


SCAN_BLOCK = 4 * SCAN_LANES
_S5_SUB = 2 * S5_CH // S5_BD // SCAN_BLOCK


def _s5_specs(t):
    ku, blk = S5_WIDTH // S5_BD, SCAN_BLOCK
    return dict(
        u=pl.BlockSpec((t, ku), lambda j: (0, j // _S5_SUB)),
        hs=pl.BlockSpec((t, blk), lambda j: (0, j)),
        wb=pl.BlockSpec((None, ku, blk), lambda j: (j // _S5_SUB, 0, j % _S5_SUB)),
        wc=pl.BlockSpec((None, blk, ku), lambda j: (j // _S5_SUB, j % _S5_SUB, 0)),
        tab=pl.BlockSpec((V7X_SUBLANES, blk), lambda j: (0, j)))


def _s5_fwd(u, wb, wc, tab, *, name):
    t = u.shape[0]
    sp = _s5_specs(t)

    def body(u_ref, wb_ref, wc_ref, tab_ref, hs_ref, y_ref):
        hs_ref[...] = jnp.dot(u_ref[...], wb_ref[...], preferred_element_type=F32)
        _scan_in_place(hs_ref, tab_ref, False)
        y = jnp.dot(hs_ref[...].astype(BF16), wc_ref[...], preferred_element_type=F32)
        first = pl.program_id(0) % _S5_SUB == 0

        @pl.when(first)
        def _():
            y_ref[...] = y

        @pl.when(jnp.logical_not(first))
        def _():
            y_ref[...] += y

    return pl.pallas_call(
        body, name=name, grid=(2 * S5_CH // SCAN_BLOCK,),
        in_specs=[sp['u'], sp['wb'], sp['wc'], sp['tab']], out_specs=[sp['hs'], sp['u']],
        out_shape=[jax.ShapeDtypeStruct((t, 2 * S5_CH), F32), jax.ShapeDtypeStruct((t, S5_WIDTH), F32)],
        compiler_params=_cparams("arbitrary"),
    )(u, wb, wc, tab)


def _s5_bwd(dy, u, hs, wb, wc, tab, *, name, side=None):
    t = u.shape[0]
    sp = _s5_specs(t)
    nt, tn = (((1,), (1,)), ((), ())), (((0,), (0,)), ((), ()))
    nsteps = 2 * S5_CH // SCAN_BLOCK

    def compute(refs):
        dy_ref, u_ref, hs_ref, wb_ref, wc_ref, tab_ref, du_ref, dwb_ref, dwc_ref, da_ref, g_ref = refs
        g_ref[...] = lax.dot_general(dy_ref[...], wc_ref[...], nt, preferred_element_type=F32)
        dwc_ref[...] = lax.dot_general(hs_ref[...].astype(BF16), dy_ref[...], tn, preferred_element_type=F32)
        _scan_in_place(g_ref, tab_ref, True, hs_ref, da_ref)
        gb = g_ref[...].astype(BF16)
        dwb_ref[...] = lax.dot_general(u_ref[...], gb, tn, preferred_element_type=F32)
        du = lax.dot_general(gb, wb_ref[...], nt, preferred_element_type=F32)
        first = pl.program_id(0) % _S5_SUB == 0

        @pl.when(first)
        def _():
            du_ref[...] = du

        @pl.when(jnp.logical_not(first))
        def _():
            du_ref[...] += du

    def body(*refs):
        _with_side(side, 6, 4, refs, pl.program_id(0), nsteps, compute)

    return _side_call(
        body, side, name=name, grid=(nsteps,),
        in_specs=[sp['u'], sp['u'], sp['hs'], sp['wb'], sp['wc'], sp['tab']],
        out_specs=[sp['u'], sp['wb'], sp['wc'], sp['tab']],
        out_shape=[jax.ShapeDtypeStruct((t, S5_WIDTH), F32), jax.ShapeDtypeStruct(wb.shape, F32),
                   jax.ShapeDtypeStruct(wc.shape, F32), jax.ShapeDtypeStruct((V7X_SUBLANES, 2 * S5_CH), F32)],
        scratch=[pltpu.VMEM((t, SCAN_BLOCK), F32)], args=[dy, u, hs, wb, wc, tab], sem=("arbitrary",))


ATT_SCALE = (QK_NOPE + QK_ROPE) ** -0.5
Q_SCALE = ATT_SCALE * math.log2(math.e)
LN2 = math.log(2.0)


def _att_mask(qi, kj, tb):
    qc = (qi * tb + lax.broadcasted_iota(jnp.int32, (tb, tb), 0)) // CHUNK
    kc = (kj * tb + lax.broadcasted_iota(jnp.int32, (tb, tb), 1)) // CHUNK
    return kc <= qc


def _with_side(side, n_main_in, n_main_out, refs, step, nsteps, compute):
    if side is None:
        compute(refs)
        return
    n = side.n
    main = refs[:n_main_in] + refs[n_main_in + n:n_main_in + n + n_main_out] + refs[n_main_in + 2 * n + n_main_out + 3:]
    x_refs = refs[n_main_in:n_main_in + n]
    y_refs = refs[n_main_in + n + n_main_out:n_main_in + 2 * n + n_main_out]
    side_refs = (x_refs, y_refs) + tuple(refs[n_main_in + 2 * n + n_main_out:n_main_in + 2 * n + n_main_out + 3])
    pl.when(step == 0)(functools.partial(side.start, *side_refs))
    compute(main)
    pl.when(step == (7 * nsteps) // 8)(functools.partial(side.relay, *side_refs))
    pl.when(step == nsteps - 1)(functools.partial(side.finish, *side_refs))


def _side_call(body, side, *, name, grid, in_specs, out_specs, out_shape, scratch, args, sem):
    n_out = len(out_shape)
    if side is not None:
        in_specs, args = in_specs + side.specs, list(args) + side.xs
        out_specs, out_shape = out_specs + side.specs, out_shape + side.out_shape
        scratch = side.scratch + scratch
        params = pltpu.CompilerParams(dimension_semantics=("arbitrary",) * len(grid), vmem_limit_bytes=VMEM_LIMIT,
                                      has_side_effects=True)
    else:
        params = _cparams(*sem)
    res = pl.pallas_call(body, name=name, grid=grid, in_specs=in_specs, out_specs=out_specs, out_shape=out_shape,
                         scratch_shapes=scratch, compiler_params=params)(*args)
    return res[:n_out], res[n_out:]


HEAD_PAIRS = MLA_HEADS // 2


def _attn_fwd(q, k, v, *, name, side=None):
    t = q.shape[0]
    hw = V7X_LANES
    tb = min(ATT_BLOCK, t)
    nblk = t // tb
    nt = (((1,), (1,)), ((), ()))

    def compute(refs):
        q_ref, k_ref, v_ref, o_ref, lse_ref = refs
        i = pl.program_id(1)
        qs = [q_ref[:, a * hw:(a + 1) * hw] for a in range(2)]

        def kv_step(j, carry, masked):
            r0 = pl.multiple_of(j * tb, tb)
            vb = v_ref[pl.ds(r0, tb), :]
            out = []
            for a in range(2):
                m, l, acc = carry[a]
                s = lax.dot_general(qs[a], k_ref[pl.ds(r0, tb), a * hw:(a + 1) * hw], nt, preferred_element_type=F32)
                if masked:
                    s = jnp.where(_att_mask(i, j, tb), s, NEG_INF)
                m_new = jnp.maximum(m, jnp.max(s, axis=1, keepdims=True))
                alpha = jnp.exp2(m - m_new)
                p = jnp.exp2(s - m_new)
                l = alpha * l + jnp.sum(p, axis=1, keepdims=True)
                out.append((m_new, l, alpha * acc + jnp.dot(p.astype(BF16), vb, preferred_element_type=F32)))
            return out

        init = [(jnp.full((tb, 1), NEG_INF, F32), jnp.zeros((tb, 1), F32), jnp.zeros((tb, hw), F32)) for _ in range(2)]
        carry = lax.fori_loop(0, i, functools.partial(kv_step, masked=False), init)
        (m0, l0, acc0), (m1, l1, acc1) = kv_step(i, carry, True)
        first = lax.broadcasted_iota(jnp.int32, (tb, hw), 1) < V_HEAD
        o_ref[...] = jnp.where(first, acc0 / l0, acc1 / l1)
        lse_ref[0] = m0 + jnp.log2(l0)
        lse_ref[1] = m1 + jnp.log2(l1)

    def body(*refs):
        _with_side(side, 3, 2, refs, pl.program_id(0) * nblk + pl.program_id(1), HEAD_PAIRS * nblk, compute)

    return _side_call(
        body, side, name=name, grid=(HEAD_PAIRS, nblk),
        in_specs=[pl.BlockSpec((tb, 2 * hw), lambda hp, i: (i, hp)), pl.BlockSpec((t, 2 * hw), lambda hp, i: (0, hp)),
                  pl.BlockSpec((t, hw), lambda hp, i: (0, hp))],
        out_specs=[pl.BlockSpec((tb, hw), lambda hp, i: (i, hp)), pl.BlockSpec((2, tb, 1), lambda hp, i: (hp, i, 0))],
        out_shape=[jax.ShapeDtypeStruct((t, MLA_HEADS * V_HEAD), F32), jax.ShapeDtypeStruct((MLA_HEADS, t, 1), F32)],
        scratch=[], args=[q, k, v], sem=("parallel", "parallel"))


def _attn_bwd(q, k, v, o, lse, do, *, name, side=None):
    t = q.shape[0]
    hw = V7X_LANES
    tb = min(ATT_BLOCK, t)
    nblk = t // tb
    nt = (((1,), (1,)), ((), ()))
    tn = (((0,), (0,)), ((), ()))

    def compute(refs):
        q_ref, k_ref, v_ref, o_ref, lse_ref, do_ref, dq_ref, dk_ref, dv_ref, delta_ref = refs
        j = pl.program_id(1)
        first = lax.broadcasted_iota(jnp.int32, (tb, hw), 1) < V_HEAD
        mine = [first, jnp.logical_not(first)]

        @pl.when(j == 0)
        def _():
            dq_ref[...] = jnp.zeros_like(dq_ref)

            def dstep(i, c):
                r0 = pl.multiple_of(i * tb, tb)
                prod = do_ref[pl.ds(r0, tb), :] * o_ref[pl.ds(r0, tb), :]
                for a in range(2):
                    delta_ref[a, pl.ds(r0, tb), :] = jnp.sum(jnp.where(mine[a], prod, 0.0), axis=1, keepdims=True)
                return c

            lax.fori_loop(0, nblk, dstep, 0)

        kb, vb = k_ref[...], v_ref[...]

        def q_step(i, carry, masked):
            dks, dv = carry
            r0 = pl.multiple_of(i * tb, tb)
            dob = do_ref[pl.ds(r0, tb), :].astype(BF16)
            new_dks = []
            for a in range(2):
                qa, ka = q_ref[pl.ds(r0, tb), a * hw:(a + 1) * hw], kb[:, a * hw:(a + 1) * hw]
                s = lax.dot_general(qa, ka, nt, preferred_element_type=F32)
                if masked:
                    s = jnp.where(_att_mask(i, j, tb), s, NEG_INF)
                p = jnp.exp2(s - lse_ref[a, pl.ds(r0, tb), :])
                doa = jnp.where(mine[a], dob, jnp.zeros_like(dob))
                dv = dv + lax.dot_general(p.astype(BF16), doa, tn, preferred_element_type=F32)
                dp = lax.dot_general(doa, vb, nt, preferred_element_type=F32)
                ds = (p * (dp - delta_ref[a, pl.ds(r0, tb), :]) * LN2).astype(BF16)
                new_dks.append(dks[a] + lax.dot_general(ds, qa, tn, preferred_element_type=F32))
                dq_ref[pl.ds(r0, tb), a * hw:(a + 1) * hw] += jnp.dot(ds, ka, preferred_element_type=F32)
            return new_dks, dv

        zero = jnp.zeros((tb, hw), F32)
        carry = q_step(j, ([zero, zero], zero), True)
        dks, dv = lax.fori_loop(j + 1, nblk, functools.partial(q_step, masked=False), carry)
        for a in range(2):
            dk_ref[:, a * hw:(a + 1) * hw] = dks[a]
        dv_ref[...] = dv

    def body(*refs):
        _with_side(side, 6, 3, refs, pl.program_id(0) * nblk + pl.program_id(1), HEAD_PAIRS * nblk, compute)

    whole = lambda w: pl.BlockSpec((t, w), lambda hp, j: (0, hp))
    blockj = lambda w: pl.BlockSpec((tb, w), lambda hp, j: (j, hp))
    return _side_call(
        body, side, name=name, grid=(HEAD_PAIRS, nblk),
        in_specs=[whole(2 * hw), blockj(2 * hw), blockj(hw), whole(hw), pl.BlockSpec((2, t, 1), lambda hp, j: (hp, 0, 0)),
                  whole(hw)],
        out_specs=[whole(2 * hw), blockj(2 * hw), blockj(hw)],
        out_shape=[jax.ShapeDtypeStruct(q.shape, F32), jax.ShapeDtypeStruct(k.shape, F32), jax.ShapeDtypeStruct(v.shape, F32)],
        scratch=[pltpu.VMEM((2, t, 1), F32)], args=[q, k, v, o, lse, do], sem=("parallel", "arbitrary"))


class _Exchange:
    def __init__(self, xs, gather, two_level=False):
        assert gather or not two_level
        self.xs, self.gather, self.n, self.two_level = list(xs), gather, len(xs), two_level
        shapes = [tuple(x.shape) if gather else tuple(x.shape[1:]) for x in xs]
        self.out_shape = [jax.ShapeDtypeStruct((N_DEV,) + shp, x.dtype) for shp, x in zip(shapes, xs)]
        self.specs = [pl.BlockSpec(memory_space=pl.ANY)] * self.n
        self.scratch = [pltpu.SemaphoreType.DMA((self.n, N_DEV - 1)), pltpu.SemaphoreType.DMA((self.n, N_DEV - 1)),
                        pltpu.SemaphoreType.DMA((self.n,))]

    def copies(self, x_refs, y_refs, send_sems, recv_sems, local_sems):
        mx, my, mc = lax.axis_index("x"), lax.axis_index("y"), lax.axis_index("c")
        me = 4 * mx + 2 * my + mc
        out = [pltpu.make_async_copy(x_refs[i] if self.gather else x_refs[i].at[me], y_refs[i].at[me], local_sems.at[i])
               for i in range(self.n)]
        for k in range(1, N_DEV):
            px = 1 - mx if k & 4 else mx
            py = 1 - my if k & 2 else my
            pc = 1 - mc if k & 1 else mc
            for i in range(self.n):
                out.append(pltpu.make_async_remote_copy(
                    src_ref=x_refs[i] if self.gather else x_refs[i].at[4 * px + 2 * py + pc], dst_ref=y_refs[i].at[me],
                    send_sem=send_sems.at[i, k - 1], recv_sem=recv_sems.at[i, k - 1],
                    device_id=(px, py, pc), device_id_type=pl.DeviceIdType.MESH))
        return out


    def _two_level(self, x_refs, y_refs, send_sems, recv_sems, local_sems):
        mx, my, mc = lax.axis_index("x"), lax.axis_index("y"), lax.axis_index("c")
        sib = (mx, my, 1 - mc)
        chips = [(1 - mx, my), (mx, 1 - my), (1 - mx, 1 - my)]
        idx = lambda px, py, pc: 4 * px + 2 * py + pc
        me = idx(mx, my, mc)

        def rc(i, k, src, block, to):
            return pltpu.make_async_remote_copy(
                src_ref=src, dst_ref=y_refs[i].at[block], send_sem=send_sems.at[i, k], recv_sem=recv_sems.at[i, k],
                device_id=to, device_id_type=pl.DeviceIdType.MESH)

        rng = range(self.n)
        over_ici = [(j, chip, i) for j, chip in enumerate(chips) for i in rng]
        return dict(
            local=lambda: [pltpu.make_async_copy(x_refs[i], y_refs[i].at[me], local_sems.at[i]) for i in rng],
            own=lambda: [rc(i, 0, x_refs[i], me, sib) for i in rng]
            + [rc(i, 1 + j, x_refs[i], me, (*chip, mc)) for j, chip, i in over_ici],
            relay=lambda: [rc(i, 4 + j, y_refs[i].at[idx(*chip, mc)], idx(*chip, mc), sib) for j, chip, i in over_ici],
            landed=lambda: [rc(i, 1 + j, x_refs[i], idx(*chip, mc), sib) for j, chip, i in over_ici],
            last=lambda: [rc(i, 0, x_refs[i], idx(*sib), sib) for i in rng]
            + [rc(i, 4 + j, x_refs[i], idx(*chip, 1 - mc), sib) for j, chip, i in over_ici])

    def start(self, *refs):
        if not self.two_level:
            for cp in self.copies(*refs):
                cp.start()
            return
        plan = self._two_level(*refs)
        for cp in plan['local']() + plan['own']():
            cp.start()

    def relay(self, *refs):
        if not self.two_level:
            return
        plan = self._two_level(*refs)
        for arrived, cp in zip(plan['landed'](), plan['relay']()):
            arrived.wait_recv()
            cp.start()

    def finish(self, *refs):
        if not self.two_level:
            for cp in self.copies(*refs):
                cp.wait()
            return
        plan = self._two_level(*refs)
        for cp in plan['last']():
            cp.wait_recv()
        for cp in plan['own']() + plan['relay']():
            cp.wait_send()
        for cp in plan['local']():
            cp.wait()


def _exchange(xs, *, gather, name, two_level=False):
    ex = _Exchange(xs, gather, two_level)
    n = ex.n

    def body(*refs):
        refs = (refs[:n], refs[n:2 * n]) + tuple(refs[2 * n:])
        ex.start(*refs)
        ex.relay(*refs)
        ex.finish(*refs)

    return pl.pallas_call(
        body, name=name, out_shape=ex.out_shape, in_specs=ex.specs, out_specs=ex.specs, scratch_shapes=ex.scratch,
        compiler_params=pltpu.CompilerParams(has_side_effects=True),
    )(*ex.xs)


def _adamw(w, gs, m, v, *, name, tm=256):
    nl = len(gs)
    parts = gs[0].ndim == 3
    c = w.shape[1]
    r = w.shape[0] // nl
    tm = _pick_rows(r, tm)
    nrow = r // tm

    def body(*refs):
        w_ref, g_refs, (m_ref, v_ref, go_ref, d_ref, mo_ref, vo_ref) = refs[0], refs[1:1 + nl], refs[1 + nl:]

        def update(g_ref):
            if parts:
                gv = g_ref[0].astype(F32)
                for k in range(1, N_DEV):
                    gv = gv + g_ref[k].astype(F32)
            else:
                gv = g_ref[...]
            mn = ADAM_B1 * m_ref[...] + (1.0 - ADAM_B1) * gv
            vn = ADAM_B2 * v_ref[...] + (1.0 - ADAM_B2) * jnp.square(gv)
            m_hat = mn / (1.0 - ADAM_B1 ** ADAM_STEP)
            v_hat = vn / (1.0 - ADAM_B2 ** ADAM_STEP)
            go_ref[...] = gv
            d_ref[...] = -ADAM_LR * (m_hat / (jnp.sqrt(v_hat) + ADAM_EPS) + ADAM_WD * w_ref[...])
            mo_ref[...] = mn
            vo_ref[...] = vn

        if nl == 1:
            update(g_refs[0])
        else:
            for layer, g_ref in enumerate(g_refs):
                pl.when(pl.program_id(0) == layer)(functools.partial(update, g_ref))

    spec = pl.BlockSpec((tm, c), lambda l, i: (l * nrow + i, 0))

    def gspec(layer):
        row = lambda l, i: jnp.where(l == layer, i, 0)
        if parts:
            return pl.BlockSpec((N_DEV, tm, c), lambda l, i: (0, row(l, i), 0))
        return pl.BlockSpec((tm, c), lambda l, i: (row(l, i), 0))

    return pl.pallas_call(
        body, name=name, grid=(nl, nrow), in_specs=[spec] + [gspec(k) for k in range(nl)] + [spec, spec],
        out_specs=[spec] * 4, out_shape=[jax.ShapeDtypeStruct(w.shape, F32)] * 4,
        compiler_params=_cparams("arbitrary", "arbitrary"),
    )(w, *gs, m, v)


def _sum_parts(x, *, name):
    def body(x_ref, o_ref):
        acc = x_ref[0]
        for k in range(1, N_DEV):
            acc = acc + x_ref[k]
        o_ref[...] = acc

    return pl.pallas_call(body, name=name, out_shape=jax.ShapeDtypeStruct(x.shape[1:], x.dtype))(x)


def _permute_in(a):
    pad = jnp.zeros(a.shape[:-1] + (IN_PAD - IN_WIDTH,), a.dtype)
    return jnp.concatenate([a[..., lo:hi] for lo, hi in _IN_SEGMENTS] + [pad], axis=-1)


def _unpermute_in(a):
    out, pos = {}, 0
    for lo, hi in _IN_SEGMENTS:
        out[lo] = a[..., pos:pos + hi - lo]
        pos += hi - lo
    return jnp.concatenate([out[lo] for lo in sorted(out)], axis=-1)


def _full_from_gathered(g, kind):
    if kind == 'row':
        return g.reshape((-1,) + g.shape[2:])
    if g.shape[-1] % V7X_LANES == 0:
        return jnp.moveaxis(g, 0, -2).reshape(g.shape[1:-1] + (-1,))
    return jnp.concatenate([g[d] for d in range(N_DEV)], axis=-1)


def _contrib_from_full(g, kind):
    if kind == 'row':
        return g.reshape((N_DEV, -1) + g.shape[1:])
    ns = g.shape[-1] // N_DEV
    if ns % V7X_LANES == 0:
        return jnp.moveaxis(g.reshape(g.shape[:-1] + (N_DEV, ns)), -2, 0)
    return jnp.stack([g[..., d * ns:(d + 1) * ns] for d in range(N_DEV)])


def _in_runs(ns):
    runs, pos = [], 0
    for lo, hi in _IN_SEGMENTS:
        for d in range(lo // ns, (hi - 1) // ns + 1):
            a, b = max(lo, d * ns), min(hi, (d + 1) * ns)
            runs.append((d, a - d * ns, b - d * ns, pos))
            pos += b - a
    return runs


def _w_in_from_gathered(g):
    pieces = [g[d][:, a:b] for d, a, b, _ in _in_runs(g.shape[2])]
    pad = jnp.zeros((g.shape[1], IN_PAD - IN_WIDTH), g.dtype)
    return jnp.concatenate(pieces + [pad], axis=1)


def _w_in_contrib(gp):
    ns = IN_WIDTH // N_DEV
    per_dev = [[] for _ in range(N_DEV)]
    for d, a, b, pos in sorted(_in_runs(ns), key=lambda r: (r[0], r[1])):
        per_dev[d].append(gp[:, pos:pos + b - a])
    return jnp.stack([jnp.concatenate(p, axis=1) for p in per_dev])


def _as2d(a, lead=0):
    return a.reshape(a.shape[:lead] + (-1, a.shape[-1]))


def _chan_cols(re, im):
    lead = re.shape[:-1]
    nb = S5_CH // SCAN_LANES
    return jnp.stack([re.reshape(lead + (nb, SCAN_LANES)), im.reshape(lead + (nb, SCAN_LANES))],
                     axis=-2).reshape(lead + (2 * S5_CH,))


def _s5_tables(lam_re, lam_im, log_dt, b_re, b_im, c_re, c_im):
    dt = jnp.exp(log_dt)[:, None]
    mag = jnp.exp(lam_re * dt)
    a_re = mag * jnp.cos(lam_im * dt)
    a_im = mag * jnp.sin(lam_im * dt)
    den = lam_re * lam_re + lam_im * lam_im
    f_re = ((a_re - 1.0) * lam_re + a_im * lam_im) / den
    f_im = (a_im * lam_re - (a_re - 1.0) * lam_im) / den
    bb_re = f_re[..., None] * b_re - f_im[..., None] * b_im
    bb_im = f_re[..., None] * b_im + f_im[..., None] * b_re
    gb = S5_GROUPS // S5_BD
    eye = jnp.eye(gb, dtype=F32)
    blocks = lambda a: a.reshape((S5_BD, gb) + a.shape[1:])

    def cols(re, im):
        shp = (S5_BD, S5_WIDTH // S5_BD, -1, SCAN_LANES)
        return jnp.stack([re.reshape(shp), im.reshape(shp)], axis=-2).reshape(S5_BD, S5_WIDTH // S5_BD, -1)

    flat = lambda a: a.reshape(S5_BD, S5_WIDTH // S5_BD, -1)
    wb_c = cols(flat(jnp.einsum('kgpc,gh->kgchp', blocks(bb_re), eye)), flat(jnp.einsum('kgpc,gh->kgchp', blocks(bb_im), eye)))
    wc_c = cols(flat(jnp.einsum('kgcp,gh->khcgp', blocks(c_re), eye)), -flat(jnp.einsum('kgcp,gh->khcgp', blocks(c_im), eye)))
    a_row = _chan_cols(a_re.reshape(1, S5_CH), a_im.reshape(1, S5_CH))
    return wb_c, wc_c.transpose(0, 2, 1), a_row


def _scan_tables(a_row, conj, seg_len):
    nb = S5_CH // SCAN_LANES
    a = a_row.reshape(nb, 2, SCAN_LANES)
    base = (a[:, 0], -a[:, 1] if conj else a[:, 1])
    mul = lambda x, y: (x[0] * y[0] - x[1] * y[1], x[0] * y[1] + x[1] * y[0])
    seg, sq, e = None, base, seg_len
    while e:
        if e & 1:
            seg = sq if seg is None else mul(seg, sq)
        sq, e = mul(sq, sq), e >> 1
    seg2 = mul(seg, seg)
    rows = [base, seg, seg2, mul(seg2, seg2)]
    lay = lambda z: jnp.stack([z[0], z[1]], axis=1).reshape(-1)
    return jnp.stack([lay(z) for z in rows] + [jnp.zeros((2 * S5_CH,), F32)] * (V7X_SUBLANES - len(rows)))


def _rope_tables(t):
    half = QK_ROPE // 2
    inv_freq = 1.0 / (ROPE_THETA ** (jnp.arange(0, QK_ROPE, 2, dtype=F32) / QK_ROPE))
    ang = jnp.arange(t, dtype=F32)[:, None] * inv_freq[None, :]
    cos, sin = jnp.cos(ang), jnp.sin(ang)
    zero = jnp.zeros_like(sin)

    def lay(nope, width, first, second, pad=0):
        head = jnp.concatenate([jnp.full((t, nope), 1.0 if first is cos else 0.0, F32), first, second,
                                jnp.zeros((t, pad), F32)], axis=1)
        reps = width // head.shape[1]
        out = jnp.tile(head, (1, reps))
        return jnp.pad(out, ((0, 0), (0, width - out.shape[1])))

    hq, pad = MLA_HEADS * V7X_LANES, V7X_LANES - QK_NOPE - QK_ROPE
    q_tabs = (lay(QK_NOPE, hq, cos, cos, pad), lay(QK_NOPE, hq, -sin, zero, pad), lay(QK_NOPE, hq, zero, sin, pad))
    k_tabs = (lay(0, V7X_LANES, cos, cos)[:, :V7X_LANES] * (jnp.arange(V7X_LANES) < QK_ROPE),
              lay(0, V7X_LANES, -sin, zero) * (jnp.arange(V7X_LANES) < QK_ROPE),
              lay(0, V7X_LANES, zero, sin) * (jnp.arange(V7X_LANES) < QK_ROPE))
    return q_tabs, k_tabs


def _sgu_tables(w_s, b_s):
    pos = jnp.arange(SGU_CHUNK) // CHUNK
    mask = pos[None, :] <= pos[:, None]
    wm = jnp.where(mask[None], w_s, 0.0).reshape(SGU_GROUPS * SGU_CHUNK, SGU_CHUNK)
    bias = jnp.repeat(b_s.T, SGU_WIDTH // SGU_GROUPS, axis=1)
    return wm, bias


def _row(v):
    return v.reshape(1, -1)


_S5_PARAMS = ('s5_lambda_re', 's5_lambda_im', 's5_log_dt', 's5_b_re', 's5_b_im', 's5_c_re', 's5_c_im')


def _derived_tables(p, t):
    (wb, wc, a_row), s5_pull = jax.vjp(jax.vmap(_s5_tables), *[p[n] for n in _S5_PARAMS])
    (wm, bias), sgu_pull = jax.vjp(jax.vmap(_sgu_tables), p['sgu_w_s'], p['sgu_b_s'])
    tab_f = jax.vmap(lambda a: _scan_tables(a, False, t // V7X_SUBLANES))(a_row)
    tab_b = jax.vmap(lambda a: _scan_tables(a, True, t // V7X_SUBLANES))(a_row)
    wb, wc = wb.astype(BF16), wc.astype(BF16)
    per_layer = [dict(s5_wb=wb[l], s5_wc=wc[l], s5_tab_fwd=tab_f[l], s5_tab_bwd=tab_b[l], sgu_wm=wm[l], sgu_bias=bias[l])
                 for l in range(len(wm))]

    def pull(grads):
        stacked = lambda k: jnp.stack([g[k] for g in grads])
        out = dict(zip(_S5_PARAMS, s5_pull((stacked('s5_wb'), stacked('s5_wc'), stacked('s5_a')))))
        out['sgu_w_s'], out['sgu_b_s'] = sgu_pull((stacked('sgu_wm'), stacked('sgu_bias')))
        return out

    return per_layer, pull


def _layer_fwd(x, ada, w, rope_tabs, tag, side=None, after_attn=None):
    s = {'x': x}
    q_tabs, k_tabs = rope_tabs
    sc1, gt1, sc2, gt2 = _row(1.0 + ada[1]), _row(1.0 + ada[2]), _row(1.0 + ada[4]), _row(1.0 + ada[5])
    s.update(sc1=sc1, gt1=gt1, sc2=sc2, gt2=gt2)
    (h,) = _rowcall(_modulate_fn, [x], [sc1, _row(ada[0])], [(D_MODEL, BF16)], tm=512, name=f"mod1_{tag}")
    proj = _mm(h, w['w_in_p'], bias=w['b_in_p'], name=f"proj_{tag}", tn=1792)
    s.update(h=h, proj=proj)

    u_view = (proj, S5_WIDTH, P_S5 // S5_WIDTH)
    u_seg = _seg_order(proj[:, P_S5:P_S5 + S5_WIDTH]).astype(BF16)
    hs, ylin = _s5_fwd(u_seg, w['s5_wb'], w['s5_wc'], w['s5_tab_fwd'], name=f"s5_fwd_{tag}")
    ylin = _time_order(ylin)
    s5_full = [_row(w['s5_d']), w['s5_w_glu'], _row(w['s5_b_glu'])]
    (y_s5,) = _rowcall(_s5_post_fn, [ylin, u_view], s5_full, [(S5_WIDTH, BF16)], tm=256, name=f"s5_post_{tag}")
    s.update(u_seg=u_seg, hs=hs, ylin=ylin, y_s5=y_s5)

    mla_rows = [(proj, MLA_BLK, P_MLA // MLA_BLK), *q_tabs, *k_tabs]
    mla_full = [_row(w['mla_q_norm']), w['mla_wq'], _row(w['mla_kv_norm']), w['mla_wk'], w['mla_wv']]
    hq, hv = MLA_HEADS * V7X_LANES, MLA_HEADS * V_HEAD
    q_r, k_r, v_r = _rowcall(_mla_pre_fn, mla_rows, mla_full, [(hq, BF16), (hq, BF16), (hv, BF16)],
                             tm=256, name=f"mla_pre_{tag}")
    (y_mla, lse), side_out = _attn_fwd(q_r, k_r, v_r, name=f"attn_fwd_{tag}", side=side)
    if after_attn is not None:
        after_attn(side_out)
    s.update(q_r=q_r, k_r=k_r, v_r=v_r, lse=lse, y_mla=y_mla, mla_full=mla_full)

    sgu_rows = [(proj, SGU_WIDTH, P_USGU // SGU_WIDTH), (proj, SGU_WIDTH, P_VSGU // SGU_WIDTH)]
    sgu_full = [_row(w['sgu_ln_g']), _row(w['sgu_ln_b']), w['sgu_wm'], w['sgu_bias']]
    (y_sgu,) = _rowcall(_sgu_fn, sgu_rows, sgu_full, [(SGU_WIDTH, BF16)], tm=SGU_CHUNK, name=f"sgu_{tag}")
    s.update(sgu_full=sgu_full, y_sgu=y_sgu)

    wbr = w['w_branch'].reshape(-1, D_MODEL)
    gate_rows = [(proj, D_MODEL, b) for b in range(3)]
    (merged,) = _rowcall(_merge_fn, [y_s5, y_mla, y_sgu] + gate_rows, [wbr], [(D_MODEL, BF16)], tm=256, name=f"merge_{tag}")
    ymix = _mm(merged, w['w_out'], name=f"wout_{tag}")
    (x1,) = _rowcall(_ln_res_fn, [x, ymix], [gt1, _row(w['ln1_g']), _row(w['ln1_b'])], [(D_MODEL, F32)], tm=512,
                     name=f"ln1_{tag}")
    s.update(merged=merged, ymix=ymix, x1=x1)

    (h2,) = _rowcall(_modulate_fn, [x1], [sc2, _row(ada[3])], [(D_MODEL, BF16)], tm=512, name=f"mod2_{tag}")
    ff_a, ff_b, act = _mm_swiglu(h2, w['ffn_w_in'], name=f"ffn_in_{tag}")
    f = _mm(act, w['ffn_w_out'], name=f"ffn_out_{tag}", tk=2816)
    (x2,) = _rowcall(_ln_res_fn, [x1, f], [gt2, _row(w['ln2_g']), _row(w['ln2_b'])], [(D_MODEL, F32)], tm=512,
                     name=f"ln2_{tag}")
    s.update(h2=h2, ff_a=ff_a, ff_b=ff_b, act=act, f=f)
    return x2, s, side_out


def _mod_bwd_fn(x, dh, dxa, scale_row):
    return (dxa + dh * scale_row, jnp.sum(dh * x, axis=0, keepdims=True), jnp.sum(dh, axis=0, keepdims=True))


def _layer_bwd(dx2, s, w, rope_tabs, tag, make_side=None, make_side2=None):
    g = {}
    q_tabs, k_tabs = rope_tabs
    t = dx2.shape[0]
    ln_full = lambda gt, a, b: [gt, _row(w[a]), _row(w[b])]

    dx1_a, df, dgt2, g['ln2_g'], g['ln2_b'] = _rowcall_vjp(
        _ln_res_fn, [s['x1'], s['f']], ln_full(s['gt2'], 'ln2_g', 'ln2_b'), [dx2], [0, 1], [0, 1, 2],
        tm=512, name=f"ln2_bwd_{tag}", row_dtypes=[F32, BF16])
    g['ffn_w_out'] = _mm(s['act'], df, ta=True, name=f"ffn_out_dw_{tag}", tm=1408)

    def swiglu_bwd_fn(a, b, d, w_out):
        dact = lax.dot_general(d, w_out, (((1,), (1,)), ((), ())), preferred_element_type=F32)
        _, pull = jax.vjp(_swiglu_fn, a, b)
        return (jnp.concatenate(pull((dact,)), axis=1),)

    (dab,) = _rowcall(swiglu_bwd_fn, [s['ff_a'], s['ff_b'], df], [w['ffn_w_out']], [(2 * FF_HIDDEN, BF16)],
                      tm=256, name=f"swiglu_bwd_{tag}")
    dh2 = _mm(dab, w['ffn_w_in'], tb=True, name=f"ffn_in_dx_{tag}", tk=1408)
    g['ffn_w_in'] = _mm(s['h2'], dab, ta=True, name=f"ffn_in_dw_{tag}", tn=1408)
    dx1, dsc2, dsh2 = _rowcall(_mod_bwd_fn, [s['x1'], dh2, dx1_a], [s['sc2']], [(D_MODEL, F32)],
                               [((1, D_MODEL), F32)] * 2, tm=512, name=f"mod2_bwd_{tag}")

    dx_a, dymix, dgt1, g['ln1_g'], g['ln1_b'] = _rowcall_vjp(
        _ln_res_fn, [s['x'], s['ymix']], ln_full(s['gt1'], 'ln1_g', 'ln1_b'), [dx1], [0, 1], [0, 1, 2],
        tm=512, name=f"ln1_bwd_{tag}", row_dtypes=[F32, BF16])
    dmerged = _mm(dymix, w['w_out'], tb=True, name=f"wout_dx_{tag}")
    g['w_out'] = _mm(s['merged'], dymix, ta=True, name=f"wout_dw_{tag}")

    proj = s['proj']
    wbr = w['w_branch'].reshape(-1, D_MODEL).astype(F32)
    gate_rows = [(proj, D_MODEL, b) for b in range(3)]
    dy_s5, dy_mla, dy_sgu, dl0, dl1, dl2, dwbr = _rowcall_vjp(
        _merge_fn, [s['y_s5'], s['y_mla'], s['y_sgu']] + gate_rows, [wbr], [dmerged], [0, 1, 2, 3, 4, 5], [0],
        tm=256, name=f"merge_bwd_{tag}")
    g['w_branch'] = dwbr.reshape(w['w_branch'].shape)

    sgu_rows = [(proj, SGU_WIDTH, P_USGU // SGU_WIDTH), (proj, SGU_WIDTH, P_VSGU // SGU_WIDTH)]
    du_sgu, dv_sgu, dlg, dlb, dwm, dbias = _rowcall_vjp(
        _sgu_fn, sgu_rows, s['sgu_full'], [dy_sgu], [0, 1], [0, 1, 2, 3], tm=SGU_CHUNK, name=f"sgu_bwd_{tag}")
    g['sgu_ln_g'], g['sgu_ln_b'] = dlg.reshape(-1), dlb.reshape(-1)
    g['sgu_wm'], g['sgu_bias'] = dwm, dbias

    (dq_r, dk_r, dv_r), side_out = _attn_bwd(s['q_r'], s['k_r'], s['v_r'], s['y_mla'], s['lse'], dy_mla,
                                             name=f"attn_bwd_{tag}", side=make_side(g) if make_side is not None else None)
    mla_rows = [(proj, MLA_BLK, P_MLA // MLA_BLK), *q_tabs, *k_tabs]
    dmla, dqn, dwq, dkvn, dwk, dwv = _rowcall_vjp(
        _mla_pre_fn, mla_rows, s['mla_full'], [dq_r, dk_r, dv_r], [0], [0, 1, 2, 3, 4], tm=256, name=f"mla_pre_bwd_{tag}")
    g['mla_w_q_up'], g['mla_w_kv_up'] = _mla_weight_grads(dwq, dwk, dwv)
    g['mla_q_norm'], g['mla_kv_norm'] = dqn.reshape(-1), dkvn.reshape(-1)

    s5_full = [_row(w['s5_d']), w['s5_w_glu'], _row(w['s5_b_glu'])]
    dylin, du_a, dd, g['s5_w_glu'], dbg = _rowcall_vjp(
        _s5_post_fn, [s['ylin'], (proj, S5_WIDTH, P_S5 // S5_WIDTH)], s5_full, [dy_s5], [0, 1], [0, 1, 2], tm=256,
        name=f"s5_post_bwd_{tag}", row_dtypes=[BF16, F32])
    g['s5_d'], g['s5_b_glu'] = dd.reshape(-1), dbg.reshape(-1)
    (du_b, g['s5_wb'], g['s5_wc'], da_part), side2_out = _s5_bwd(
        _seg_order(dylin), s['u_seg'], s['hs'], w['s5_wb'], w['s5_wc'], w['s5_tab_bwd'], name=f"s5_bwd_{tag}",
        side=make_side2(g) if make_side2 is not None else None)
    du_b = _time_order(du_b)
    g['s5_a'] = jnp.sum(da_part, axis=0, keepdims=True)

    def dproj_fn(g0, g1, g2, ua, ub, us, vs, ml):
        d = jnp.concatenate([g0, g1, g2, ua + ub, us, vs, ml], axis=1)
        return d, jnp.sum(d, axis=0, keepdims=True)

    dproj, db_in = _rowcall(dproj_fn, [dl0, dl1, dl2, du_a, du_b, du_sgu, dv_sgu, dmla], [], [(IN_PAD, BF16)],
                            [((1, IN_PAD), F32)], tm=256, name=f"dproj_{tag}")
    dh = _mm(dproj, w['w_in_p'], tb=True, name=f"proj_dx_{tag}", tk=1792)
    g['w_in_p'] = _mm(s['h'], dproj, ta=True, name=f"proj_dw_{tag}", tm=512, tn=1792)
    g['b_in'] = _unpermute_in(db_in).reshape(-1)
    dx, dsc1, dsh1 = _rowcall(_mod_bwd_fn, [s['x'], dh, dx_a], [s['sc1']], [(D_MODEL, F32)], [((1, D_MODEL), F32)] * 2,
                              tm=512, name=f"mod1_bwd_{tag}")
    d_ada = jnp.concatenate([dsh1, dsc1, dgt1, dsh2, dsc2, dgt2], axis=0)
    return dx, d_ada, g, side_out, side2_out


def _loss_fn(y, target):
    err = y - target
    return (err / D_MODEL, 0.5 * jnp.sum(jnp.sum(err * err, axis=1, keepdims=True), axis=0, keepdims=True) / D_MODEL)


def _step(p):
    me = 4 * lax.axis_index("x") + 2 * lax.axis_index("y") + lax.axis_index("c")
    x = p['x'][0]
    t = x.shape[0]
    rope_tabs = _rope_tables(t)

    (c_all,) = _exchange([jnp.broadcast_to(p['c'], (V7X_SUBLANES, D_MODEL))], gather=True, name="gather_c")
    c_all = c_all[:, 0, :]
    (c_act,) = _rowcall(lambda cc: (cc * _sigmoid(cc),), [c_all], [], [(D_MODEL, F32)], tm=N_DEV, name="c_silu")
    ncol = p['w_ada'].shape[2]
    b_ada_loc = lax.dynamic_slice_in_dim(p['b_ada'], me * ncol, ncol, axis=1)
    ada_cols = jnp.concatenate([_mm(c_act, p['w_ada'][l], bias=b_ada_loc[l:l + 1], name=f"ada_{l}") for l in range(DEPTH)])
    (ada_all,) = _exchange([ada_cols], gather=True, name="gather_ada")
    ada_all = ada_all.reshape(N_DEV, DEPTH, N_DEV, ncol)
    ada = lax.dynamic_index_in_dim(ada_all, me, axis=2, keepdims=False)
    ada = ada.transpose(1, 0, 2).reshape(DEPTH, 6, D_MODEL)

    mixer_w, ffn_w = SHARDED[:-2], SHARDED[-2:]

    def shards(l, group):
        return [p[n][l].astype(BF16) for n, _ in group]

    def contribs(g, group):
        return [(_w_in_contrib(g['w_in_p']) if n == 'w_in' else _contrib_from_full(g[n], kind)).astype(BF16)
                for n, kind in group]

    tables, pull_tables = _derived_tables(p, t)

    def mixer_weights(l, gathered):
        w = {n: _full_from_gathered(g, kind) for (n, kind), g in zip(mixer_w[1:], gathered[1:])}
        w['w_in_p'] = _w_in_from_gathered(gathered[0])
        w['b_in_p'] = _row(_permute_in(p['b_in'][l]))
        w['mla_wq'], w['mla_wk'], w['mla_wv'] = _mla_weights(w['mla_w_q_up'], w['mla_w_kv_up'])
        w.update(tables[l])
        for n in SMALL:
            if n != 'b_ada' and n != 'b_in':
                w[n] = p[n][l]
        return w

    saved, layers = [], []
    gathered = _exchange(shards(0, mixer_w), gather=True, name="gather_w_0", two_level=True)
    for l in range(DEPTH):
        w = mixer_weights(l, gathered)
        side = _Exchange(shards(l, ffn_w) + (shards(l + 1, mixer_w) if l + 1 < DEPTH else []), True, two_level=True)

        def add_ffn(res, w=w):
            for (n, kind), g in zip(ffn_w, res):
                w[n] = _full_from_gathered(g, kind)

        x, s, res = _layer_fwd(x, ada[l], w, rope_tabs, f"l{l}", side, add_ffn)
        gathered = res[len(ffn_w):]
        layers.append(w)
        saved.append(s)
    dy, loss_loc = _rowcall(_loss_fn, [x, p['loss_target'][0]], [], [(D_MODEL, F32)], [((1, 1), F32)], tm=256, name="loss")
    loss = lax.psum(loss_loc[0, 0], ("x", "y", "c"))

    w_in_g, other_w = mixer_w[:1], mixer_w[1:]
    d_ada, grads, landed = [None] * DEPTH, [None] * DEPTH, [[None, None, None] for _ in range(DEPTH)]
    dx, pending = dy, []
    for l in reversed(range(DEPTH)):
        make_side = lambda g, pending=pending: _Exchange(contribs(g, ffn_w) + pending, False)
        make_side2 = lambda g: _Exchange(contribs(g, other_w), False)
        dx, d_ada[l], grads[l], res, landed[l][1] = _layer_bwd(dx, saved[l], layers[l], rope_tabs, f"l{l}", make_side,
                                                               make_side2)
        landed[l][2] = res[:len(ffn_w)]
        if l + 1 < DEPTH:
            landed[l + 1][0] = res[len(ffn_w):]
        pending = contribs(grads[l], w_in_g)
    landed[0][0] = _exchange(pending, gather=False, name="scatter_g_0")
    landed = [list(a) + list(b) + list(c) for a, b, c in landed]
    d_ada = jnp.stack(d_ada).reshape(DEPTH, 6 * D_MODEL)

    (d_ada_all,) = _exchange([d_ada], gather=True, name="gather_dada")
    d_ada_cols = lax.dynamic_slice_in_dim(d_ada_all, me * ncol, ncol, axis=2)
    pad_b = ((0, V7X_LANES - N_DEV), (0, 0))
    c_act_p = jnp.pad(c_act, pad_b)
    g_w_ada = jnp.stack([_mm(c_act_p, jnp.pad(d_ada_cols[:, l], pad_b), ta=True, name=f"ada_dw_{l}") for l in range(DEPTH)])

    out = {}
    kinds = ('grad_', 'delta_', 'new_m_', 'new_v_')
    for i, (n, _) in enumerate(SHARDED):
        res = _adamw(_as2d(p[n]), [_as2d(landed[l][i], lead=1) for l in range(DEPTH)], _as2d(p['m_' + n]),
                     _as2d(p['v_' + n]), name=f"adamw_{n}")
        for kind, r in zip(kinds, res):
            out[kind + n] = r.reshape(p[n].shape)

    res = _adamw(_as2d(p['w_ada']), [_as2d(g_w_ada)], _as2d(p['m_w_ada']), _as2d(p['v_w_ada']), name="adamw_ada")
    for kind, r in zip(kinds, res):
        out[kind + 'w_ada'] = r.reshape(p['w_ada'].shape)

    small_g = pull_tables(grads)
    small_g.update({n: jnp.stack([grads[l][n] for l in range(DEPTH)]) for n in SMALL if n not in small_g and n != 'b_ada'})
    small_g['b_ada'] = d_ada
    n_small = sum(int(np.prod(p[n].shape)) for n in SMALL)
    n_pad = -(-n_small // SMALL_PAD) * SMALL_PAD
    flat = jnp.concatenate([small_g[n].reshape(-1) for n in SMALL])
    flat = jnp.pad(flat, (0, n_pad - n_small)).reshape(N_DEV, n_pad // N_DEV // 1024, 1024)
    (landed_small,) = _exchange([flat], gather=False, name="scatter_small")
    (g_all,) = _exchange([_sum_parts(landed_small, name="sum_small")], gather=True, name="gather_small")
    g_all = g_all.reshape(-1)
    off = 0
    for n in SMALL:
        size = int(np.prod(p[n].shape))
        g_n = g_all[off:off + size].reshape(p[n].shape)
        off += size
        res = _adamw(_as2d(p[n]), [_as2d(g_n)], _as2d(p['m_' + n]), _as2d(p['v_' + n]), name=f"adamw_{n}")
        for kind, r in zip(kinds, res):
            out[kind + n] = r.reshape(p[n].shape)

    outs = [loss, dx[None]]
    for kind in ('grad_', 'delta_', 'new_m_', 'new_v_'):
        outs += [out[kind + n] for n in WNAMES]
    return tuple(outs)


def kernel(x, c, w_ada, b_ada, w_in, b_in, s5_lambda_re, s5_lambda_im, s5_log_dt, s5_b_re, s5_b_im, s5_c_re, s5_c_im, s5_d, s5_w_glu, s5_b_glu, mla_q_norm, mla_w_q_up, mla_kv_norm, mla_w_kv_up, sgu_ln_g, sgu_ln_b, sgu_w_s, sgu_b_s, w_branch, w_out, ln1_g, ln1_b, ffn_w_in, ffn_w_out, ln2_g, ln2_b, loss_target, m_w_ada, m_b_ada, m_w_in, m_b_in, m_s5_lambda_re, m_s5_lambda_im, m_s5_log_dt, m_s5_b_re, m_s5_b_im, m_s5_c_re, m_s5_c_im, m_s5_d, m_s5_w_glu, m_s5_b_glu, m_mla_q_norm, m_mla_w_q_up, m_mla_kv_norm, m_mla_w_kv_up, m_sgu_ln_g, m_sgu_ln_b, m_sgu_w_s, m_sgu_b_s, m_w_branch, m_w_out, m_ln1_g, m_ln1_b, m_ffn_w_in, m_ffn_w_out, m_ln2_g, m_ln2_b, v_w_ada, v_b_ada, v_w_in, v_b_in, v_s5_lambda_re, v_s5_lambda_im, v_s5_log_dt, v_s5_b_re, v_s5_b_im, v_s5_c_re, v_s5_c_im, v_s5_d, v_s5_w_glu, v_s5_b_glu, v_mla_q_norm, v_mla_w_q_up, v_mla_kv_norm, v_mla_w_kv_up, v_sgu_ln_g, v_sgu_ln_b, v_sgu_w_s, v_sgu_b_s, v_w_branch, v_w_out, v_ln1_g, v_ln1_b, v_ffn_w_in, v_ffn_w_out, v_ln2_g, v_ln2_b):
    return _step(dict(locals()))
```

```python
import functools
import math

import numpy as np
import jax
import jax.numpy as jnp
from jax import lax
from jax.experimental import pallas as pl
from jax.experimental.pallas import tpu as pltpu

F32 = jnp.float32
BF16 = jnp.bfloat16

N_DEV = 8
D_MODEL = 1024
DEPTH = 4
CHUNK = 64
S5_WIDTH = 512
S5_GROUP = 16
S5_GROUPS = 32
S5_STATE = 64
MLA_HEADS = 8
QK_NOPE = 64
QK_ROPE = 32
V_HEAD = 64
Q_LORA = 384
KV_LORA = 256
ROPE_THETA = 10000.0
SGU_WIDTH = 512
SGU_GROUPS = 4
SGU_CHUNK = 128
FF_HIDDEN = 2816
DEEPNORM_ALPHA = (2 * DEPTH) ** 0.25
LN_EPS = 1e-5
RMS_EPS = 1e-6
NEG_INF = -1e30
ADAM_LR = 0.001
ADAM_B1 = 0.9
ADAM_B2 = 0.999
ADAM_EPS = 1e-08
ADAM_WD = 0.01
ADAM_STEP = 10

IN_WIDTH = 5280
IN_PAD = 5376
_O_S5, _O_CQ, _O_CKV, _O_KPE, _O_USGU, _O_VSGU, _O_GATE = 0, 512, 896, 1152, 1184, 1696, 2208
_IN_SEGMENTS = ((_O_GATE, IN_WIDTH), (_O_S5, _O_CQ), (_O_USGU, _O_VSGU), (_O_VSGU, _O_GATE), (_O_CQ, _O_USGU))
P_GATE, P_S5, P_USGU, P_VSGU, P_MLA = 0, 3072, 3584, 4096, 4608
MLA_BLK = 768

V7X_LANES = 128
V7X_SUBLANES = 8
VMEM_LIMIT = 56 * 1024 * 1024
ATT_BLOCK = 512
SCAN_LANES = 128
S5_CH = S5_GROUPS * S5_STATE
S5_BD = 4

WNAMES = ['w_ada', 'b_ada', 'w_in', 'b_in', 's5_lambda_re', 's5_lambda_im', 's5_log_dt', 's5_b_re', 's5_b_im',
          's5_c_re', 's5_c_im', 's5_d', 's5_w_glu', 's5_b_glu', 'mla_q_norm', 'mla_w_q_up', 'mla_kv_norm',
          'mla_w_kv_up', 'sgu_ln_g', 'sgu_ln_b', 'sgu_w_s', 'sgu_b_s', 'w_branch', 'w_out', 'ln1_g', 'ln1_b',
          'ffn_w_in', 'ffn_w_out', 'ln2_g', 'ln2_b']
SHARDED = (('w_in', 'col'), ('s5_w_glu', 'row'), ('mla_w_q_up', 'col'), ('mla_w_kv_up', 'col'),
           ('w_branch', 'col3'), ('w_out', 'row'), ('ffn_w_in', 'col'), ('ffn_w_out', 'row'))
SMALL = [n for n in WNAMES if n != 'w_ada' and n not in dict(SHARDED)]
SMALL_PAD = N_DEV * V7X_SUBLANES * 1024


def _cparams(*sem):
    return pltpu.CompilerParams(dimension_semantics=sem, vmem_limit_bytes=VMEM_LIMIT)


def _pick(n, target):
    if n <= target:
        return n
    best = None
    for d in range(V7X_LANES, target + 1, V7X_LANES):
        if n % d == 0:
            best = d
    assert best is not None, (n, target)
    return best


def _pick_rows(n, target):
    if n <= target:
        return n
    best = None
    for d in range(V7X_SUBLANES, target + 1, V7X_SUBLANES):
        if n % d == 0:
            best = d
    assert best is not None, (n, target)
    return best


@jax.custom_vjp
def _bdot(a, b):
    return jnp.dot(a.astype(BF16), b.astype(BF16), preferred_element_type=F32)


def _bdot_fwd(a, b):
    return _bdot(a, b), (a, b)


def _bdot_bwd(res, g):
    a, b = res
    gb = g.astype(BF16)
    da = lax.dot_general(gb, b.astype(BF16), (((1,), (1,)), ((), ())), preferred_element_type=F32)
    db = lax.dot_general(a.astype(BF16), gb, (((0,), (0,)), ((), ())), preferred_element_type=F32)
    return da.astype(a.dtype), db.astype(b.dtype)


_bdot.defvjp(_bdot_fwd, _bdot_bwd)


@functools.partial(jax.custom_vjp, nondiff_argnums=(1,))
def _lane_roll(x, shift):
    return pltpu.roll(x, shift % x.shape[1], 1)


def _lane_roll_fwd(x, shift):
    return _lane_roll(x, shift), None


def _lane_roll_bwd(shift, _, g):
    return (_lane_roll(g, -shift),)


_lane_roll.defvjp(_lane_roll_fwd, _lane_roll_bwd)


def _sigmoid(x):
    return 1.0 / (1.0 + jnp.exp(-x))


def _gelu(x):
    return 0.5 * x * (1.0 + jnp.tanh(math.sqrt(2.0 / math.pi) * (x + 0.044715 * (x * x * x))))


def _layer_norm(x, g, b):
    mu = jnp.mean(x, axis=-1, keepdims=True)
    var = jnp.mean(jnp.square(x - mu), axis=-1, keepdims=True)
    return (x - mu) * lax.rsqrt(var + LN_EPS) * g + b


def _rms_norm(x, g):
    return x * lax.rsqrt(jnp.mean(x * x, axis=-1, keepdims=True) + RMS_EPS) * g


def _rope(x, c, s1, s2):
    half = QK_ROPE // 2
    return x * c + _lane_roll(x, -half) * s1 + _lane_roll(x, half) * s2


def _modulate_fn(x, scale_row, shift_row):
    return (x * scale_row + shift_row,)


def _ln_res_fn(x, y, gate_row, g, b):
    return (_layer_norm(DEEPNORM_ALPHA * x + gate_row * y, g, b),)


def _s5_post_fn(ylin, u, d, w_glu, b_glu):
    z = _gelu(ylin + d * u)
    return (z * _sigmoid(_bdot(z, w_glu) + b_glu),)


def _mla_pre_fn(blk, cq_t, sq1, sq2, ck_t, sk1, sk2, q_norm, w_q, kv_norm, w_k, w_v):
    cq, ckv, kpe = blk[:, :Q_LORA], blk[:, Q_LORA:Q_LORA + KV_LORA], blk[:, Q_LORA + KV_LORA:]
    q = _rope(_bdot(_rms_norm(cq, q_norm), w_q), cq_t, sq1, sq2) * Q_SCALE
    ckv_n = _rms_norm(ckv, kv_norm)
    kpe_r = _lane_roll(_rope(kpe, ck_t, sk1, sk2), QK_NOPE)
    k = _bdot(ckv_n, w_k) + jnp.concatenate([kpe_r] * MLA_HEADS, axis=1)
    return q, k, _bdot(ckv_n, w_v)


def _pad_heads(w, width):
    w3 = w.reshape(w.shape[0], MLA_HEADS, width)
    return jnp.pad(w3, ((0, 0), (0, 0), (0, V7X_LANES - width))).reshape(w.shape[0], MLA_HEADS * V7X_LANES)


def _mla_weights(w_q_up, w_kv_up):
    kv3 = w_kv_up.reshape(w_kv_up.shape[0], MLA_HEADS, QK_NOPE + V_HEAD)
    w_k = _pad_heads(kv3[:, :, :QK_NOPE].reshape(w_kv_up.shape[0], -1), QK_NOPE)
    return _pad_heads(w_q_up, QK_NOPE + QK_ROPE), w_k, kv3[:, :, QK_NOPE:].reshape(w_kv_up.shape[0], -1)


def _mla_weight_grads(dw_q, dw_k, dw_v):
    unpad = lambda a, width: a.reshape(a.shape[0], MLA_HEADS, V7X_LANES)[:, :, :width]
    dkv = jnp.concatenate([unpad(dw_k, QK_NOPE), dw_v.reshape(dw_v.shape[0], MLA_HEADS, V_HEAD)], axis=2)
    return unpad(dw_q, QK_NOPE + QK_ROPE).reshape(dw_q.shape[0], -1), dkv.reshape(dw_k.shape[0], -1)


def _sgu_fn(u, v, g, b, wm, bias):
    vn = _layer_norm(_gelu(v), g, b)
    w = SGU_CHUNK
    parts = [_bdot(wm[k * w:(k + 1) * w, :], vn[:, k * w:(k + 1) * w]) for k in range(SGU_GROUPS)]
    return (_gelu(u) * (jnp.concatenate(parts, axis=1) + bias),)


def _merge_fn(y0, y1, y2, l0, l1, l2, wb):
    n = S5_WIDTH
    return (_sigmoid(l0) * _bdot(y0, wb[:n]) + _sigmoid(l1) * _bdot(y1, wb[n:2 * n])
            + _sigmoid(l2) * _bdot(y2, wb[2 * n:]),)


def _swiglu_fn(a, b):
    return (a * _sigmoid(a) * b,)


def _rowcall(fn, rows, fulls, out_rows, out_reds=(), *, tm, name):
    rows = [r if isinstance(r, tuple) else (r, r.shape[1], 0) for r in rows]
    t = rows[0][0].shape[0]
    tm = _pick_rows(t, tm)
    n_in, n_or, n_red = len(rows) + len(fulls), len(out_rows), len(out_reds)

    def body(*refs):
        vals = fn(*[r[...] for r in refs[:n_in]])
        assert len(vals) == n_or + n_red, (name, len(vals))
        for ref, v in zip(refs[n_in:n_in + n_or], vals[:n_or]):
            ref[...] = v.astype(ref.dtype)
        if n_red:
            red_refs = refs[n_in + n_or:]

            @pl.when(pl.program_id(0) == 0)
            def _():
                for ref in red_refs:
                    ref[...] = jnp.zeros_like(ref)

            for ref, v in zip(red_refs, vals[n_or:]):
                ref[...] += v.astype(ref.dtype)

    in_specs = [pl.BlockSpec((tm, w), functools.partial(lambda i, blk: (i, blk), blk=blk)) for _, w, blk in rows]
    in_specs += [pl.BlockSpec(f.shape, lambda i: (0, 0)) for f in fulls]
    out_specs = [pl.BlockSpec((tm, c), lambda i: (i, 0)) for c, _ in out_rows]
    out_specs += [pl.BlockSpec(s, lambda i: (0, 0)) for s, _ in out_reds]
    out_shape = [jax.ShapeDtypeStruct((t, c), dt) for c, dt in out_rows]
    out_shape += [jax.ShapeDtypeStruct(s, dt) for s, dt in out_reds]
    return pl.pallas_call(
        body, name=name, grid=(t // tm,), in_specs=in_specs, out_specs=out_specs, out_shape=out_shape,
        compiler_params=_cparams("arbitrary" if n_red else "parallel"),
    )(*[r[0] for r in rows], *fulls)


def _rowcall_vjp(fn, rows, fulls, cots, diff_rows, diff_fulls, *, tm, name, row_dtypes=None):
    rows_n = [r if isinstance(r, tuple) else (r, r.shape[1], 0) for r in rows]
    n_r, n_c = len(rows), len(cots)
    row_dtypes = row_dtypes or [F32] * len(diff_rows)

    def fn2(*vals):
        r = [v.astype(F32) for v in vals[:n_r]]
        ct = vals[n_r:n_r + n_c]
        f = [v.astype(F32) for v in vals[n_r + n_c:]]

        def g(*dargs):
            rr, ff = list(r), list(f)
            for k, idx in enumerate(diff_rows):
                rr[idx] = dargs[k]
            for k, idx in enumerate(diff_fulls):
                ff[idx] = dargs[len(diff_rows) + k]
            return fn(*rr, *ff)

        prim = [r[i] for i in diff_rows] + [f[i] for i in diff_fulls]
        outs, pull = jax.vjp(g, *prim)
        return pull(tuple(c.astype(o.dtype) for c, o in zip(ct, outs)))

    out_rows = [(rows_n[i][1], dt) for i, dt in zip(diff_rows, row_dtypes)]
    out_reds = [(fulls[i].shape, F32) for i in diff_fulls]
    return _rowcall(fn2, list(rows) + list(cots), fulls, out_rows, out_reds, tm=tm, name=name)


def _mm(a, b, *, ta=False, tb=False, bias=None, out_dtype=F32, name, tm=1024, tn=1024, tk=1024):
    (k_a, m) = a.shape if ta else a.shape[::-1]
    (n, k_b) = b.shape if tb else b.shape[::-1]
    assert k_a == k_b, (name, a.shape, b.shape)
    tm, tn, tk = _pick(m, tm) if m % V7X_LANES == 0 else m, _pick(n, tn), _pick(k_a, tk) if k_a % V7X_LANES == 0 else k_a
    nk = k_a // tk
    a_spec = pl.BlockSpec((tk, tm), lambda i, j, k: (k, i)) if ta else pl.BlockSpec((tm, tk), lambda i, j, k: (i, k))
    b_spec = pl.BlockSpec((tn, tk), lambda i, j, k: (j, k)) if tb else pl.BlockSpec((tk, tn), lambda i, j, k: (k, j))
    dims = (((0,) if ta else (1,), (1,) if tb else (0,)), ((), ()))
    has_bias = bias is not None

    def body(*refs):
        a_ref, b_ref = refs[0], refs[1]
        part = lax.dot_general(a_ref[...].astype(BF16), b_ref[...].astype(BF16), dims, preferred_element_type=F32)
        if nk == 1:
            o_ref = refs[-1]
            o_ref[...] = (part + refs[2][...] if has_bias else part).astype(o_ref.dtype)
            return
        o_ref, acc_ref = refs[-2], refs[-1]
        k = pl.program_id(2)

        @pl.when(k == 0)
        def _():
            acc_ref[...] = part

        @pl.when(k > 0)
        def _():
            acc_ref[...] += part

        @pl.when(k == nk - 1)
        def _():
            r = acc_ref[...]
            if has_bias:
                r = r + refs[2][...]
            o_ref[...] = r.astype(o_ref.dtype)

    in_specs = [a_spec, b_spec] + ([pl.BlockSpec((1, tn), lambda i, j, k: (0, j))] if has_bias else [])
    return pl.pallas_call(
        body, name=name, grid=(m // tm, n // tn, nk), in_specs=in_specs,
        out_specs=pl.BlockSpec((tm, tn), lambda i, j, k: (i, j)),
        out_shape=jax.ShapeDtypeStruct((m, n), out_dtype),
        scratch_shapes=[pltpu.VMEM((tm, tn), F32)] if nk > 1 else [],
        compiler_params=_cparams("parallel", "parallel", "arbitrary"),
    )(a, b, *([bias] if has_bias else []))


def _mm_swiglu(x, w, *, name, tm=512, tn=1408):
    m, k = x.shape
    hdim = w.shape[1] // 2
    tm, tn = _pick(m, tm), _pick(hdim, tn)
    nj = hdim // tn

    def body(x_ref, wa_ref, wb_ref, a_ref, b_ref, act_ref):
        xb = x_ref[...].astype(BF16)
        a = jnp.dot(xb, wa_ref[...].astype(BF16), preferred_element_type=F32)
        b = jnp.dot(xb, wb_ref[...].astype(BF16), preferred_element_type=F32)
        a_ref[...] = a
        b_ref[...] = b
        act_ref[...] = _swiglu_fn(a, b)[0].astype(act_ref.dtype)

    tile = pl.BlockSpec((tm, tn), lambda i, j: (i, j))
    return pl.pallas_call(
        body, name=name, grid=(m // tm, nj),
        in_specs=[pl.BlockSpec((tm, k), lambda i, j: (i, 0)), pl.BlockSpec((k, tn), lambda i, j: (0, j)),
                  pl.BlockSpec((k, tn), lambda i, j: (0, j + nj))],
        out_specs=[tile, tile, tile],
        out_shape=[jax.ShapeDtypeStruct((m, hdim), F32), jax.ShapeDtypeStruct((m, hdim), F32),
                   jax.ShapeDtypeStruct((m, hdim), BF16)],
        compiler_params=_cparams("parallel", "parallel"),
    )(x, w, w)


def _seg_order(a):
    t, c = a.shape
    return a.reshape(V7X_SUBLANES, t // V7X_SUBLANES, c).transpose(1, 0, 2).reshape(t, c)


def _time_order(a):
    t, c = a.shape
    return a.reshape(t // V7X_SUBLANES, V7X_SUBLANES, c).transpose(1, 0, 2).reshape(t, c)


def _scan_in_place(o_ref, tab_ref, reverse, h_ref=None, da_ref=None):
    ntile = o_ref.shape[0] // V7X_SUBLANES
    with_da = h_ref is not None
    ln = SCAN_LANES
    rows8 = V7X_SUBLANES
    subs = range(o_ref.shape[1] // (2 * ln))
    col = lambda s, part: slice((2 * s + part) * ln, (2 * s + part + 1) * ln)
    cmul = lambda p, q: (p[0] * q[0] - p[1] * q[1], p[0] * q[1] + p[1] * q[0])
    cadd = lambda p, q: (p[0] + q[0], p[1] + q[1])

    def run():
        row = lax.broadcasted_iota(jnp.int32, (rows8, ln), 0)
        bcast = lambda k, s: tuple(jnp.broadcast_to(tab_ref[k:k + 1, col(s, part)], (rows8, ln)) for part in (0, 1))
        a = [bcast(0, s) for s in subs]
        zero = [(jnp.zeros((rows8, ln), F32), jnp.zeros((rows8, ln), F32)) for _ in subs]

        def tile(ref, i):
            r0 = pl.multiple_of(i * rows8, rows8)
            return [(ref[pl.ds(r0, rows8), col(s, 0)], ref[pl.ds(r0, rows8), col(s, 1)]) for s in subs]

        def put(i, ys):
            r0 = pl.multiple_of(i * rows8, rows8)
            for s, (yr, yi) in enumerate(ys):
                o_ref[pl.ds(r0, rows8), col(s, 0)] = yr
                o_ref[pl.ds(r0, rows8), col(s, 1)] = yi

        def shifted(z, step, up):
            shift, keep = (rows8 - step, row < rows8 - step) if up else (step, row >= step)
            return tuple(jnp.where(keep, pltpu.roll(v, shift, 0), 0.0) for v in z)

        def pass1(n, carry):
            i = (ntile - 1 - n) if reverse else n
            ys = [cadd(x, cmul(a[s], carry[s])) for s, x in enumerate(tile(o_ref, i))]
            put(i, ys)
            return ys

        ends = lax.fori_loop(0, ntile, pass1, zero, unroll=4)
        incoming = []
        for s in subs:
            f = ends[s]
            for k, step in zip((1, 2, 3), (1, 2, 4)):
                f = cadd(f, cmul(bcast(k, s), shifted(f, step, reverse)))
            incoming.append(shifted(f, 1, reverse))
        if with_da:
            h0 = [shifted(h, 1, False) for h in tile(h_ref, ntile - 1)]

        def pass2(n, carry):
            i = (ntile - 1 - n) if reverse else n
            power = carry[0]
            ys = [cadd(y, cmul(power[s], incoming[s])) for s, y in enumerate(tile(o_ref, i))]
            put(i, ys)
            new_power = [cmul(power[s], a[s]) for s in subs]
            if not with_da:
                return (new_power,)
            prev = tile(h_ref, jnp.maximum(i - 1, 0))
            acc = []
            for s in subs:
                hpr, hpi = jnp.where(i > 0, prev[s][0], h0[s][0]), jnp.where(i > 0, prev[s][1], h0[s][1])
                (yr, yi), (sr, si) = ys[s], carry[1][s]
                acc.append((sr + yr * hpr + yi * hpi, si + yi * hpr - yr * hpi))
            return (new_power, acc)

        out = lax.fori_loop(0, ntile, pass2, (list(a),) + ((zero,) if with_da else ()), unroll=4)
        if with_da:
            for s in subs:
                da_ref[:, col(s, 0)] = out[1][s][0]
                da_ref[:, col(s, 1)] = out[1][s][1]

    run()


SCAN_BLOCK = 4 * SCAN_LANES
_S5_SUB = 2 * S5_CH // S5_BD // SCAN_BLOCK


def _s5_specs(t):
    ku, blk = S5_WIDTH // S5_BD, SCAN_BLOCK
    return dict(
        u=pl.BlockSpec((t, ku), lambda j: (0, j // _S5_SUB)),
        hs=pl.BlockSpec((t, blk), lambda j: (0, j)),
        wb=pl.BlockSpec((None, ku, blk), lambda j: (j // _S5_SUB, 0, j % _S5_SUB)),
        wc=pl.BlockSpec((None, blk, ku), lambda j: (j // _S5_SUB, j % _S5_SUB, 0)),
        tab=pl.BlockSpec((V7X_SUBLANES, blk), lambda j: (0, j)))


def _s5_fwd(u, wb, wc, tab, *, name):
    t = u.shape[0]
    sp = _s5_specs(t)

    def body(u_ref, wb_ref, wc_ref, tab_ref, hs_ref, y_ref):
        hs_ref[...] = jnp.dot(u_ref[...], wb_ref[...], preferred_element_type=F32)
        _scan_in_place(hs_ref, tab_ref, False)
        y = jnp.dot(hs_ref[...].astype(BF16), wc_ref[...], preferred_element_type=F32)
        first = pl.program_id(0) % _S5_SUB == 0

        @pl.when(first)
        def _():
            y_ref[...] = y

        @pl.when(jnp.logical_not(first))
        def _():
            y_ref[...] += y

    return pl.pallas_call(
        body, name=name, grid=(2 * S5_CH // SCAN_BLOCK,),
        in_specs=[sp['u'], sp['wb'], sp['wc'], sp['tab']], out_specs=[sp['hs'], sp['u']],
        out_shape=[jax.ShapeDtypeStruct((t, 2 * S5_CH), F32), jax.ShapeDtypeStruct((t, S5_WIDTH), F32)],
        compiler_params=_cparams("arbitrary"),
    )(u, wb, wc, tab)


def _s5_bwd(dy, u, hs, wb, wc, tab, *, name, side=None):
    t = u.shape[0]
    sp = _s5_specs(t)
    nt, tn = (((1,), (1,)), ((), ())), (((0,), (0,)), ((), ()))
    nsteps = 2 * S5_CH // SCAN_BLOCK

    def compute(refs):
        dy_ref, u_ref, hs_ref, wb_ref, wc_ref, tab_ref, du_ref, dwb_ref, dwc_ref, da_ref, g_ref = refs
        g_ref[...] = lax.dot_general(dy_ref[...], wc_ref[...], nt, preferred_element_type=F32)
        dwc_ref[...] = lax.dot_general(hs_ref[...].astype(BF16), dy_ref[...], tn, preferred_element_type=F32)
        _scan_in_place(g_ref, tab_ref, True, hs_ref, da_ref)
        gb = g_ref[...].astype(BF16)
        dwb_ref[...] = lax.dot_general(u_ref[...], gb, tn, preferred_element_type=F32)
        du = lax.dot_general(gb, wb_ref[...], nt, preferred_element_type=F32)
        first = pl.program_id(0) % _S5_SUB == 0

        @pl.when(first)
        def _():
            du_ref[...] = du

        @pl.when(jnp.logical_not(first))
        def _():
            du_ref[...] += du

    def body(*refs):
        _with_side(side, 6, 4, refs, pl.program_id(0), nsteps, compute)

    return _side_call(
        body, side, name=name, grid=(nsteps,),
        in_specs=[sp['u'], sp['u'], sp['hs'], sp['wb'], sp['wc'], sp['tab']],
        out_specs=[sp['u'], sp['wb'], sp['wc'], sp['tab']],
        out_shape=[jax.ShapeDtypeStruct((t, S5_WIDTH), F32), jax.ShapeDtypeStruct(wb.shape, F32),
                   jax.ShapeDtypeStruct(wc.shape, F32), jax.ShapeDtypeStruct((V7X_SUBLANES, 2 * S5_CH), F32)],
        scratch=[pltpu.VMEM((t, SCAN_BLOCK), F32)], args=[dy, u, hs, wb, wc, tab], sem=("arbitrary",))


ATT_SCALE = (QK_NOPE + QK_ROPE) ** -0.5
Q_SCALE = ATT_SCALE * math.log2(math.e)
LN2 = math.log(2.0)


def _att_mask(qi, kj, tb):
    qc = (qi * tb + lax.broadcasted_iota(jnp.int32, (tb, tb), 0)) // CHUNK
    kc = (kj * tb + lax.broadcasted_iota(jnp.int32, (tb, tb), 1)) // CHUNK
    return kc <= qc


def _with_side(side, n_main_in, n_main_out, refs, step, nsteps, compute):
    if side is None:
        compute(refs)
        return
    n = side.n
    main = refs[:n_main_in] + refs[n_main_in + n:n_main_in + n + n_main_out] + refs[n_main_in + 2 * n + n_main_out + 3:]
    x_refs = refs[n_main_in:n_main_in + n]
    y_refs = refs[n_main_in + n + n_main_out:n_main_in + 2 * n + n_main_out]
    side_refs = (x_refs, y_refs) + tuple(refs[n_main_in + 2 * n + n_main_out:n_main_in + 2 * n + n_main_out + 3])
    pl.when(step == 0)(functools.partial(side.start, *side_refs))
    compute(main)
    pl.when(step == (7 * nsteps) // 8)(functools.partial(side.relay, *side_refs))
    pl.when(step == nsteps - 1)(functools.partial(side.finish, *side_refs))


def _side_call(body, side, *, name, grid, in_specs, out_specs, out_shape, scratch, args, sem):
    n_out = len(out_shape)
    if side is not None:
        in_specs, args = in_specs + side.specs, list(args) + side.xs
        out_specs, out_shape = out_specs + side.specs, out_shape + side.out_shape
        scratch = side.scratch + scratch
        params = pltpu.CompilerParams(dimension_semantics=("arbitrary",) * len(grid), vmem_limit_bytes=VMEM_LIMIT,
                                      has_side_effects=True)
    else:
        params = _cparams(*sem)
    res = pl.pallas_call(body, name=name, grid=grid, in_specs=in_specs, out_specs=out_specs, out_shape=out_shape,
                         scratch_shapes=scratch, compiler_params=params)(*args)
    return res[:n_out], res[n_out:]


HEAD_PAIRS = MLA_HEADS // 2


def _attn_fwd(q, k, v, *, name, side=None):
    t = q.shape[0]
    hw = V7X_LANES
    tb = min(ATT_BLOCK, t)
    nblk = t // tb
    nt = (((1,), (1,)), ((), ()))

    def compute(refs):
        q_ref, k_ref, v_ref, o_ref, lse_ref = refs
        i = pl.program_id(1)
        qs = [q_ref[:, a * hw:(a + 1) * hw] for a in range(2)]

        def kv_step(j, carry, masked):
            r0 = pl.multiple_of(j * tb, tb)
            vb = v_ref[pl.ds(r0, tb), :]
            out = []
            for a in range(2):
                m, l, acc = carry[a]
                s = lax.dot_general(qs[a], k_ref[pl.ds(r0, tb), a * hw:(a + 1) * hw], nt, preferred_element_type=F32)
                if masked:
                    s = jnp.where(_att_mask(i, j, tb), s, NEG_INF)
                m_new = jnp.maximum(m, jnp.max(s, axis=1, keepdims=True))
                alpha = jnp.exp2(m - m_new)
                p = jnp.exp2(s - m_new)
                l = alpha * l + jnp.sum(p, axis=1, keepdims=True)
                out.append((m_new, l, alpha * acc + jnp.dot(p.astype(BF16), vb, preferred_element_type=F32)))
            return out

        init = [(jnp.full((tb, 1), NEG_INF, F32), jnp.zeros((tb, 1), F32), jnp.zeros((tb, hw), F32)) for _ in range(2)]
        carry = lax.fori_loop(0, i, functools.partial(kv_step, masked=False), init)
        (m0, l0, acc0), (m1, l1, acc1) = kv_step(i, carry, True)
        first = lax.broadcasted_iota(jnp.int32, (tb, hw), 1) < V_HEAD
        o_ref[...] = jnp.where(first, acc0 / l0, acc1 / l1)
        lse_ref[0] = m0 + jnp.log2(l0)
        lse_ref[1] = m1 + jnp.log2(l1)

    def body(*refs):
        _with_side(side, 3, 2, refs, pl.program_id(0) * nblk + pl.program_id(1), HEAD_PAIRS * nblk, compute)

    return _side_call(
        body, side, name=name, grid=(HEAD_PAIRS, nblk),
        in_specs=[pl.BlockSpec((tb, 2 * hw), lambda hp, i: (i, hp)), pl.BlockSpec((t, 2 * hw), lambda hp, i: (0, hp)),
                  pl.BlockSpec((t, hw), lambda hp, i: (0, hp))],
        out_specs=[pl.BlockSpec((tb, hw), lambda hp, i: (i, hp)), pl.BlockSpec((2, tb, 1), lambda hp, i: (hp, i, 0))],
        out_shape=[jax.ShapeDtypeStruct((t, MLA_HEADS * V_HEAD), F32), jax.ShapeDtypeStruct((MLA_HEADS, t, 1), F32)],
        scratch=[], args=[q, k, v], sem=("parallel", "parallel"))


def _attn_bwd(q, k, v, o, lse, do, *, name, side=None):
    t = q.shape[0]
    hw = V7X_LANES
    tb = min(ATT_BLOCK, t)
    nblk = t // tb
    nt = (((1,), (1,)), ((), ()))
    tn = (((0,), (0,)), ((), ()))

    def compute(refs):
        q_ref, k_ref, v_ref, o_ref, lse_ref, do_ref, dq_ref, dk_ref, dv_ref, delta_ref = refs
        j = pl.program_id(1)
        first = lax.broadcasted_iota(jnp.int32, (tb, hw), 1) < V_HEAD
        mine = [first, jnp.logical_not(first)]

        @pl.when(j == 0)
        def _():
            dq_ref[...] = jnp.zeros_like(dq_ref)

            def dstep(i, c):
                r0 = pl.multiple_of(i * tb, tb)
                prod = do_ref[pl.ds(r0, tb), :] * o_ref[pl.ds(r0, tb), :]
                for a in range(2):
                    delta_ref[a, pl.ds(r0, tb), :] = jnp.sum(jnp.where(mine[a], prod, 0.0), axis=1, keepdims=True)
                return c

            lax.fori_loop(0, nblk, dstep, 0)

        kb, vb = k_ref[...], v_ref[...]

        def q_step(i, carry, masked):
            dks, dv = carry
            r0 = pl.multiple_of(i * tb, tb)
            dob = do_ref[pl.ds(r0, tb), :].astype(BF16)
            new_dks = []
            for a in range(2):
                qa, ka = q_ref[pl.ds(r0, tb), a * hw:(a + 1) * hw], kb[:, a * hw:(a + 1) * hw]
                s = lax.dot_general(qa, ka, nt, preferred_element_type=F32)
                if masked:
                    s = jnp.where(_att_mask(i, j, tb), s, NEG_INF)
                p = jnp.exp2(s - lse_ref[a, pl.ds(r0, tb), :])
                doa = jnp.where(mine[a], dob, jnp.zeros_like(dob))
                dv = dv + lax.dot_general(p.astype(BF16), doa, tn, preferred_element_type=F32)
                dp = lax.dot_general(doa, vb, nt, preferred_element_type=F32)
                ds = (p * (dp - delta_ref[a, pl.ds(r0, tb), :]) * LN2).astype(BF16)
                new_dks.append(dks[a] + lax.dot_general(ds, qa, tn, preferred_element_type=F32))
                dq_ref[pl.ds(r0, tb), a * hw:(a + 1) * hw] += jnp.dot(ds, ka, preferred_element_type=F32)
            return new_dks, dv

        zero = jnp.zeros((tb, hw), F32)
        carry = q_step(j, ([zero, zero], zero), True)
        dks, dv = lax.fori_loop(j + 1, nblk, functools.partial(q_step, masked=False), carry)
        for a in range(2):
            dk_ref[:, a * hw:(a + 1) * hw] = dks[a]
        dv_ref[...] = dv

    def body(*refs):
        _with_side(side, 6, 3, refs, pl.program_id(0) * nblk + pl.program_id(1), HEAD_PAIRS * nblk, compute)

    whole = lambda w: pl.BlockSpec((t, w), lambda hp, j: (0, hp))
    blockj = lambda w: pl.BlockSpec((tb, w), lambda hp, j: (j, hp))
    return _side_call(
        body, side, name=name, grid=(HEAD_PAIRS, nblk),
        in_specs=[whole(2 * hw), blockj(2 * hw), blockj(hw), whole(hw), pl.BlockSpec((2, t, 1), lambda hp, j: (hp, 0, 0)),
                  whole(hw)],
        out_specs=[whole(2 * hw), blockj(2 * hw), blockj(hw)],
        out_shape=[jax.ShapeDtypeStruct(q.shape, F32), jax.ShapeDtypeStruct(k.shape, F32), jax.ShapeDtypeStruct(v.shape, F32)],
        scratch=[pltpu.VMEM((2, t, 1), F32)], args=[q, k, v, o, lse, do], sem=("parallel", "arbitrary"))


class _Exchange:
    def __init__(self, xs, gather, two_level=False):
        assert gather or not two_level
        self.xs, self.gather, self.n, self.two_level = list(xs), gather, len(xs), two_level
        shapes = [tuple(x.shape) if gather else tuple(x.shape[1:]) for x in xs]
        self.out_shape = [jax.ShapeDtypeStruct((N_DEV,) + shp, x.dtype) for shp, x in zip(shapes, xs)]
        self.specs = [pl.BlockSpec(memory_space=pl.ANY)] * self.n
        self.scratch = [pltpu.SemaphoreType.DMA((self.n, N_DEV - 1)), pltpu.SemaphoreType.DMA((self.n, N_DEV - 1)),
                        pltpu.SemaphoreType.DMA((self.n,))]

    def copies(self, x_refs, y_refs, send_sems, recv_sems, local_sems):
        mx, my, mc = lax.axis_index("x"), lax.axis_index("y"), lax.axis_index("c")
        me = 4 * mx + 2 * my + mc
        out = [pltpu.make_async_copy(x_refs[i] if self.gather else x_refs[i].at[me], y_refs[i].at[me], local_sems.at[i])
               for i in range(self.n)]
        for k in range(1, N_DEV):
            px = 1 - mx if k & 4 else mx
            py = 1 - my if k & 2 else my
            pc = 1 - mc if k & 1 else mc
            for i in range(self.n):
                out.append(pltpu.make_async_remote_copy(
                    src_ref=x_refs[i] if self.gather else x_refs[i].at[4 * px + 2 * py + pc], dst_ref=y_refs[i].at[me],
                    send_sem=send_sems.at[i, k - 1], recv_sem=recv_sems.at[i, k - 1],
                    device_id=(px, py, pc), device_id_type=pl.DeviceIdType.MESH))
        return out


    def _two_level(self, x_refs, y_refs, send_sems, recv_sems, local_sems):
        mx, my, mc = lax.axis_index("x"), lax.axis_index("y"), lax.axis_index("c")
        sib = (mx, my, 1 - mc)
        chips = [(1 - mx, my), (mx, 1 - my), (1 - mx, 1 - my)]
        idx = lambda px, py, pc: 4 * px + 2 * py + pc
        me = idx(mx, my, mc)

        def rc(i, k, src, block, to):
            return pltpu.make_async_remote_copy(
                src_ref=src, dst_ref=y_refs[i].at[block], send_sem=send_sems.at[i, k], recv_sem=recv_sems.at[i, k],
                device_id=to, device_id_type=pl.DeviceIdType.MESH)

        rng = range(self.n)
        over_ici = [(j, chip, i) for j, chip in enumerate(chips) for i in rng]
        return dict(
            local=lambda: [pltpu.make_async_copy(x_refs[i], y_refs[i].at[me], local_sems.at[i]) for i in rng],
            own=lambda: [rc(i, 0, x_refs[i], me, sib) for i in rng]
            + [rc(i, 1 + j, x_refs[i], me, (*chip, mc)) for j, chip, i in over_ici],
            relay=lambda: [rc(i, 4 + j, y_refs[i].at[idx(*chip, mc)], idx(*chip, mc), sib) for j, chip, i in over_ici],
            landed=lambda: [rc(i, 1 + j, x_refs[i], idx(*chip, mc), sib) for j, chip, i in over_ici],
            last=lambda: [rc(i, 0, x_refs[i], idx(*sib), sib) for i in rng]
            + [rc(i, 4 + j, x_refs[i], idx(*chip, 1 - mc), sib) for j, chip, i in over_ici])

    def start(self, *refs):
        if not self.two_level:
            for cp in self.copies(*refs):
                cp.start()
            return
        plan = self._two_level(*refs)
        for cp in plan['local']() + plan['own']():
            cp.start()

    def relay(self, *refs):
        if not self.two_level:
            return
        plan = self._two_level(*refs)
        for arrived, cp in zip(plan['landed'](), plan['relay']()):
            arrived.wait_recv()
            cp.start()

    def finish(self, *refs):
        if not self.two_level:
            for cp in self.copies(*refs):
                cp.wait()
            return
        plan = self._two_level(*refs)
        for cp in plan['last']():
            cp.wait_recv()
        for cp in plan['own']() + plan['relay']():
            cp.wait_send()
        for cp in plan['local']():
            cp.wait()


def _exchange(xs, *, gather, name, two_level=False):
    ex = _Exchange(xs, gather, two_level)
    n = ex.n

    def body(*refs):
        refs = (refs[:n], refs[n:2 * n]) + tuple(refs[2 * n:])
        ex.start(*refs)
        ex.relay(*refs)
        ex.finish(*refs)

    return pl.pallas_call(
        body, name=name, out_shape=ex.out_shape, in_specs=ex.specs, out_specs=ex.specs, scratch_shapes=ex.scratch,
        compiler_params=pltpu.CompilerParams(has_side_effects=True),
    )(*ex.xs)


def _adamw(w, gs, m, v, *, name, tm=256):
    nl = len(gs)
    parts = gs[0].ndim == 3
    c = w.shape[1]
    r = w.shape[0] // nl
    tm = _pick_rows(r, tm)
    nrow = r // tm

    def body(*refs):
        w_ref, g_refs, (m_ref, v_ref, go_ref, d_ref, mo_ref, vo_ref) = refs[0], refs[1:1 + nl], refs[1 + nl:]

        def update(g_ref):
            if parts:
                gv = g_ref[0].astype(F32)
                for k in range(1, N_DEV):
                    gv = gv + g_ref[k].astype(F32)
            else:
                gv = g_ref[...]
            mn = ADAM_B1 * m_ref[...] + (1.0 - ADAM_B1) * gv
            vn = ADAM_B2 * v_ref[...] + (1.0 - ADAM_B2) * jnp.square(gv)
            m_hat = mn / (1.0 - ADAM_B1 ** ADAM_STEP)
            v_hat = vn / (1.0 - ADAM_B2 ** ADAM_STEP)
            go_ref[...] = gv
            d_ref[...] = -ADAM_LR * (m_hat / (jnp.sqrt(v_hat) + ADAM_EPS) + ADAM_WD * w_ref[...])
            mo_ref[...] = mn
            vo_ref[...] = vn

        if nl == 1:
            update(g_refs[0])
        else:
            for layer, g_ref in enumerate(g_refs):
                pl.when(pl.program_id(0) == layer)(functools.partial(update, g_ref))

    spec = pl.BlockSpec((tm, c), lambda l, i: (l * nrow + i, 0))

    def gspec(layer):
        row = lambda l, i: jnp.where(l == layer, i, 0)
        if parts:
            return pl.BlockSpec((N_DEV, tm, c), lambda l, i: (0, row(l, i), 0))
        return pl.BlockSpec((tm, c), lambda l, i: (row(l, i), 0))

    return pl.pallas_call(
        body, name=name, grid=(nl, nrow), in_specs=[spec] + [gspec(k) for k in range(nl)] + [spec, spec],
        out_specs=[spec] * 4, out_shape=[jax.ShapeDtypeStruct(w.shape, F32)] * 4,
        compiler_params=_cparams("arbitrary", "arbitrary"),
    )(w, *gs, m, v)


def _sum_parts(x, *, name):
    def body(x_ref, o_ref):
        acc = x_ref[0]
        for k in range(1, N_DEV):
            acc = acc + x_ref[k]
        o_ref[...] = acc

    return pl.pallas_call(body, name=name, out_shape=jax.ShapeDtypeStruct(x.shape[1:], x.dtype))(x)


def _permute_in(a):
    pad = jnp.zeros(a.shape[:-1] + (IN_PAD - IN_WIDTH,), a.dtype)
    return jnp.concatenate([a[..., lo:hi] for lo, hi in _IN_SEGMENTS] + [pad], axis=-1)


def _unpermute_in(a):
    out, pos = {}, 0
    for lo, hi in _IN_SEGMENTS:
        out[lo] = a[..., pos:pos + hi - lo]
        pos += hi - lo
    return jnp.concatenate([out[lo] for lo in sorted(out)], axis=-1)


def _full_from_gathered(g, kind):
    if kind == 'row':
        return g.reshape((-1,) + g.shape[2:])
    if g.shape[-1] % V7X_LANES == 0:
        return jnp.moveaxis(g, 0, -2).reshape(g.shape[1:-1] + (-1,))
    return jnp.concatenate([g[d] for d in range(N_DEV)], axis=-1)


def _contrib_from_full(g, kind):
    if kind == 'row':
        return g.reshape((N_DEV, -1) + g.shape[1:])
    ns = g.shape[-1] // N_DEV
    if ns % V7X_LANES == 0:
        return jnp.moveaxis(g.reshape(g.shape[:-1] + (N_DEV, ns)), -2, 0)
    return jnp.stack([g[..., d * ns:(d + 1) * ns] for d in range(N_DEV)])


def _in_runs(ns):
    runs, pos = [], 0
    for lo, hi in _IN_SEGMENTS:
        for d in range(lo // ns, (hi - 1) // ns + 1):
            a, b = max(lo, d * ns), min(hi, (d + 1) * ns)
            runs.append((d, a - d * ns, b - d * ns, pos))
            pos += b - a
    return runs


def _w_in_from_gathered(g):
    pieces = [g[d][:, a:b] for d, a, b, _ in _in_runs(g.shape[2])]
    pad = jnp.zeros((g.shape[1], IN_PAD - IN_WIDTH), g.dtype)
    return jnp.concatenate(pieces + [pad], axis=1)


def _w_in_contrib(gp):
    ns = IN_WIDTH // N_DEV
    per_dev = [[] for _ in range(N_DEV)]
    for d, a, b, pos in sorted(_in_runs(ns), key=lambda r: (r[0], r[1])):
        per_dev[d].append(gp[:, pos:pos + b - a])
    return jnp.stack([jnp.concatenate(p, axis=1) for p in per_dev])


def _as2d(a, lead=0):
    return a.reshape(a.shape[:lead] + (-1, a.shape[-1]))


def _chan_cols(re, im):
    lead = re.shape[:-1]
    nb = S5_CH // SCAN_LANES
    return jnp.stack([re.reshape(lead + (nb, SCAN_LANES)), im.reshape(lead + (nb, SCAN_LANES))],
                     axis=-2).reshape(lead + (2 * S5_CH,))


def _s5_tables(lam_re, lam_im, log_dt, b_re, b_im, c_re, c_im):
    dt = jnp.exp(log_dt)[:, None]
    mag = jnp.exp(lam_re * dt)
    a_re = mag * jnp.cos(lam_im * dt)
    a_im = mag * jnp.sin(lam_im * dt)
    den = lam_re * lam_re + lam_im * lam_im
    f_re = ((a_re - 1.0) * lam_re + a_im * lam_im) / den
    f_im = (a_im * lam_re - (a_re - 1.0) * lam_im) / den
    bb_re = f_re[..., None] * b_re - f_im[..., None] * b_im
    bb_im = f_re[..., None] * b_im + f_im[..., None] * b_re
    gb = S5_GROUPS // S5_BD
    eye = jnp.eye(gb, dtype=F32)
    blocks = lambda a: a.reshape((S5_BD, gb) + a.shape[1:])

    def cols(re, im):
        shp = (S5_BD, S5_WIDTH // S5_BD, -1, SCAN_LANES)
        return jnp.stack([re.reshape(shp), im.reshape(shp)], axis=-2).reshape(S5_BD, S5_WIDTH // S5_BD, -1)

    flat = lambda a: a.reshape(S5_BD, S5_WIDTH // S5_BD, -1)
    wb_c = cols(flat(jnp.einsum('kgpc,gh->kgchp', blocks(bb_re), eye)), flat(jnp.einsum('kgpc,gh->kgchp', blocks(bb_im), eye)))
    wc_c = cols(flat(jnp.einsum('kgcp,gh->khcgp', blocks(c_re), eye)), -flat(jnp.einsum('kgcp,gh->khcgp', blocks(c_im), eye)))
    a_row = _chan_cols(a_re.reshape(1, S5_CH), a_im.reshape(1, S5_CH))
    return wb_c, wc_c.transpose(0, 2, 1), a_row


def _scan_tables(a_row, conj, seg_len):
    nb = S5_CH // SCAN_LANES
    a = a_row.reshape(nb, 2, SCAN_LANES)
    base = (a[:, 0], -a[:, 1] if conj else a[:, 1])
    mul = lambda x, y: (x[0] * y[0] - x[1] * y[1], x[0] * y[1] + x[1] * y[0])
    seg, sq, e = None, base, seg_len
    while e:
        if e & 1:
            seg = sq if seg is None else mul(seg, sq)
        sq, e = mul(sq, sq), e >> 1
    seg2 = mul(seg, seg)
    rows = [base, seg, seg2, mul(seg2, seg2)]
    lay = lambda z: jnp.stack([z[0], z[1]], axis=1).reshape(-1)
    return jnp.stack([lay(z) for z in rows] + [jnp.zeros((2 * S5_CH,), F32)] * (V7X_SUBLANES - len(rows)))


def _rope_tables(t):
    half = QK_ROPE // 2
    inv_freq = 1.0 / (ROPE_THETA ** (jnp.arange(0, QK_ROPE, 2, dtype=F32) / QK_ROPE))
    ang = jnp.arange(t, dtype=F32)[:, None] * inv_freq[None, :]
    cos, sin = jnp.cos(ang), jnp.sin(ang)
    zero = jnp.zeros_like(sin)

    def lay(nope, width, first, second, pad=0):
        head = jnp.concatenate([jnp.full((t, nope), 1.0 if first is cos else 0.0, F32), first, second,
                                jnp.zeros((t, pad), F32)], axis=1)
        reps = width // head.shape[1]
        out = jnp.tile(head, (1, reps))
        return jnp.pad(out, ((0, 0), (0, width - out.shape[1])))

    hq, pad = MLA_HEADS * V7X_LANES, V7X_LANES - QK_NOPE - QK_ROPE
    q_tabs = (lay(QK_NOPE, hq, cos, cos, pad), lay(QK_NOPE, hq, -sin, zero, pad), lay(QK_NOPE, hq, zero, sin, pad))
    k_tabs = (lay(0, V7X_LANES, cos, cos)[:, :V7X_LANES] * (jnp.arange(V7X_LANES) < QK_ROPE),
              lay(0, V7X_LANES, -sin, zero) * (jnp.arange(V7X_LANES) < QK_ROPE),
              lay(0, V7X_LANES, zero, sin) * (jnp.arange(V7X_LANES) < QK_ROPE))
    return q_tabs, k_tabs


def _sgu_tables(w_s, b_s):
    pos = jnp.arange(SGU_CHUNK) // CHUNK
    mask = pos[None, :] <= pos[:, None]
    wm = jnp.where(mask[None], w_s, 0.0).reshape(SGU_GROUPS * SGU_CHUNK, SGU_CHUNK)
    bias = jnp.repeat(b_s.T, SGU_WIDTH // SGU_GROUPS, axis=1)
    return wm, bias


def _row(v):
    return v.reshape(1, -1)


_S5_PARAMS = ('s5_lambda_re', 's5_lambda_im', 's5_log_dt', 's5_b_re', 's5_b_im', 's5_c_re', 's5_c_im')


def _derived_tables(p, t):
    (wb, wc, a_row), s5_pull = jax.vjp(jax.vmap(_s5_tables), *[p[n] for n in _S5_PARAMS])
    (wm, bias), sgu_pull = jax.vjp(jax.vmap(_sgu_tables), p['sgu_w_s'], p['sgu_b_s'])
    tab_f = jax.vmap(lambda a: _scan_tables(a, False, t // V7X_SUBLANES))(a_row)
    tab_b = jax.vmap(lambda a: _scan_tables(a, True, t // V7X_SUBLANES))(a_row)
    wb, wc = wb.astype(BF16), wc.astype(BF16)
    per_layer = [dict(s5_wb=wb[l], s5_wc=wc[l], s5_tab_fwd=tab_f[l], s5_tab_bwd=tab_b[l], sgu_wm=wm[l], sgu_bias=bias[l])
                 for l in range(len(wm))]

    def pull(grads):
        stacked = lambda k: jnp.stack([g[k] for g in grads])
        out = dict(zip(_S5_PARAMS, s5_pull((stacked('s5_wb'), stacked('s5_wc'), stacked('s5_a')))))
        out['sgu_w_s'], out['sgu_b_s'] = sgu_pull((stacked('sgu_wm'), stacked('sgu_bias')))
        return out

    return per_layer, pull


def _layer_fwd(x, ada, w, rope_tabs, tag, side=None, after_attn=None):
    s = {'x': x}
    q_tabs, k_tabs = rope_tabs
    sc1, gt1, sc2, gt2 = _row(1.0 + ada[1]), _row(1.0 + ada[2]), _row(1.0 + ada[4]), _row(1.0 + ada[5])
    s.update(sc1=sc1, gt1=gt1, sc2=sc2, gt2=gt2)
    (h,) = _rowcall(_modulate_fn, [x], [sc1, _row(ada[0])], [(D_MODEL, BF16)], tm=512, name=f"mod1_{tag}")
    proj = _mm(h, w['w_in_p'], bias=w['b_in_p'], name=f"proj_{tag}", tn=1792)
    s.update(h=h, proj=proj)

    u_view = (proj, S5_WIDTH, P_S5 // S5_WIDTH)
    u_seg = _seg_order(proj[:, P_S5:P_S5 + S5_WIDTH]).astype(BF16)
    hs, ylin = _s5_fwd(u_seg, w['s5_wb'], w['s5_wc'], w['s5_tab_fwd'], name=f"s5_fwd_{tag}")
    ylin = _time_order(ylin)
    s5_full = [_row(w['s5_d']), w['s5_w_glu'], _row(w['s5_b_glu'])]
    (y_s5,) = _rowcall(_s5_post_fn, [ylin, u_view], s5_full, [(S5_WIDTH, BF16)], tm=512, name=f"s5_post_{tag}")
    s.update(u_seg=u_seg, hs=hs, ylin=ylin, y_s5=y_s5)

    mla_rows = [(proj, MLA_BLK, P_MLA // MLA_BLK), *q_tabs, *k_tabs]
    mla_full = [_row(w['mla_q_norm']), w['mla_wq'], _row(w['mla_kv_norm']), w['mla_wk'], w['mla_wv']]
    hq, hv = MLA_HEADS * V7X_LANES, MLA_HEADS * V_HEAD
    q_r, k_r, v_r = _rowcall(_mla_pre_fn, mla_rows, mla_full, [(hq, BF16), (hq, BF16), (hv, BF16)],
                             tm=512, name=f"mla_pre_{tag}")
    (y_mla, lse), side_out = _attn_fwd(q_r, k_r, v_r, name=f"attn_fwd_{tag}", side=side)
    if after_attn is not None:
        after_attn(side_out)
    s.update(q_r=q_r, k_r=k_r, v_r=v_r, lse=lse, y_mla=y_mla, mla_full=mla_full)

    sgu_rows = [(proj, SGU_WIDTH, P_USGU // SGU_WIDTH), (proj, SGU_WIDTH, P_VSGU // SGU_WIDTH)]
    sgu_full = [_row(w['sgu_ln_g']), _row(w['sgu_ln_b']), w['sgu_wm'], w['sgu_bias']]
    (y_sgu,) = _rowcall(_sgu_fn, sgu_rows, sgu_full, [(SGU_WIDTH, BF16)], tm=SGU_CHUNK, name=f"sgu_{tag}")
    s.update(sgu_full=sgu_full, y_sgu=y_sgu)

    wbr = w['w_branch'].reshape(-1, D_MODEL)
    gate_rows = [(proj, D_MODEL, b) for b in range(3)]
    (merged,) = _rowcall(_merge_fn, [y_s5, y_mla, y_sgu] + gate_rows, [wbr], [(D_MODEL, BF16)], tm=512, name=f"merge_{tag}")
    ymix = _mm(merged, w['w_out'], name=f"wout_{tag}")
    (x1,) = _rowcall(_ln_res_fn, [x, ymix], [gt1, _row(w['ln1_g']), _row(w['ln1_b'])], [(D_MODEL, F32)], tm=512,
                     name=f"ln1_{tag}")
    s.update(merged=merged, ymix=ymix, x1=x1)

    (h2,) = _rowcall(_modulate_fn, [x1], [sc2, _row(ada[3])], [(D_MODEL, BF16)], tm=512, name=f"mod2_{tag}")
    ff_a, ff_b, act = _mm_swiglu(h2, w['ffn_w_in'], name=f"ffn_in_{tag}")
    f = _mm(act, w['ffn_w_out'], name=f"ffn_out_{tag}", tk=2816)
    (x2,) = _rowcall(_ln_res_fn, [x1, f], [gt2, _row(w['ln2_g']), _row(w['ln2_b'])], [(D_MODEL, F32)], tm=512,
                     name=f"ln2_{tag}")
    s.update(h2=h2, ff_a=ff_a, ff_b=ff_b, act=act, f=f)
    return x2, s, side_out


def _mod_bwd_fn(x, dh, dxa, scale_row):
    return (dxa + dh * scale_row, jnp.sum(dh * x, axis=0, keepdims=True), jnp.sum(dh, axis=0, keepdims=True))


def _layer_bwd(dx2, s, w, rope_tabs, tag, make_side=None, make_side2=None):
    g = {}
    q_tabs, k_tabs = rope_tabs
    t = dx2.shape[0]
    ln_full = lambda gt, a, b: [gt, _row(w[a]), _row(w[b])]

    dx1_a, df, dgt2, g['ln2_g'], g['ln2_b'] = _rowcall_vjp(
        _ln_res_fn, [s['x1'], s['f']], ln_full(s['gt2'], 'ln2_g', 'ln2_b'), [dx2], [0, 1], [0, 1, 2],
        tm=512, name=f"ln2_bwd_{tag}", row_dtypes=[F32, BF16])
    g['ffn_w_out'] = _mm(s['act'], df, ta=True, name=f"ffn_out_dw_{tag}", tm=1408)

    def swiglu_bwd_fn(a, b, d, w_out):
        dact = lax.dot_general(d, w_out, (((1,), (1,)), ((), ())), preferred_element_type=F32)
        _, pull = jax.vjp(_swiglu_fn, a, b)
        return (jnp.concatenate(pull((dact,)), axis=1),)

    (dab,) = _rowcall(swiglu_bwd_fn, [s['ff_a'], s['ff_b'], df], [w['ffn_w_out']], [(2 * FF_HIDDEN, BF16)],
                      tm=256, name=f"swiglu_bwd_{tag}")
    dh2 = _mm(dab, w['ffn_w_in'], tb=True, name=f"ffn_in_dx_{tag}", tk=1408)
    g['ffn_w_in'] = _mm(s['h2'], dab, ta=True, name=f"ffn_in_dw_{tag}", tn=1408)
    dx1, dsc2, dsh2 = _rowcall(_mod_bwd_fn, [s['x1'], dh2, dx1_a], [s['sc2']], [(D_MODEL, F32)],
                               [((1, D_MODEL), F32)] * 2, tm=512, name=f"mod2_bwd_{tag}")

    dx_a, dymix, dgt1, g['ln1_g'], g['ln1_b'] = _rowcall_vjp(
        _ln_res_fn, [s['x'], s['ymix']], ln_full(s['gt1'], 'ln1_g', 'ln1_b'), [dx1], [0, 1], [0, 1, 2],
        tm=512, name=f"ln1_bwd_{tag}", row_dtypes=[F32, BF16])
    dmerged = _mm(dymix, w['w_out'], tb=True, name=f"wout_dx_{tag}")
    g['w_out'] = _mm(s['merged'], dymix, ta=True, name=f"wout_dw_{tag}")

    proj = s['proj']
    wbr = w['w_branch'].reshape(-1, D_MODEL).astype(F32)
    gate_rows = [(proj, D_MODEL, b) for b in range(3)]
    dy_s5, dy_mla, dy_sgu, dl0, dl1, dl2, dwbr = _rowcall_vjp(
        _merge_fn, [s['y_s5'], s['y_mla'], s['y_sgu']] + gate_rows, [wbr], [dmerged], [0, 1, 2, 3, 4, 5], [0],
        tm=256, name=f"merge_bwd_{tag}")
    g['w_branch'] = dwbr.reshape(w['w_branch'].shape)

    sgu_rows = [(proj, SGU_WIDTH, P_USGU // SGU_WIDTH), (proj, SGU_WIDTH, P_VSGU // SGU_WIDTH)]
    du_sgu, dv_sgu, dlg, dlb, dwm, dbias = _rowcall_vjp(
        _sgu_fn, sgu_rows, s['sgu_full'], [dy_sgu], [0, 1], [0, 1, 2, 3], tm=SGU_CHUNK, name=f"sgu_bwd_{tag}")
    g['sgu_ln_g'], g['sgu_ln_b'] = dlg.reshape(-1), dlb.reshape(-1)
    g['sgu_wm'], g['sgu_bias'] = dwm, dbias

    (dq_r, dk_r, dv_r), side_out = _attn_bwd(s['q_r'], s['k_r'], s['v_r'], s['y_mla'], s['lse'], dy_mla,
                                             name=f"attn_bwd_{tag}", side=make_side(g) if make_side is not None else None)
    mla_rows = [(proj, MLA_BLK, P_MLA // MLA_BLK), *q_tabs, *k_tabs]
    dmla, dqn, dwq, dkvn, dwk, dwv = _rowcall_vjp(
        _mla_pre_fn, mla_rows, s['mla_full'], [dq_r, dk_r, dv_r], [0], [0, 1, 2, 3, 4], tm=256, name=f"mla_pre_bwd_{tag}")
    g['mla_w_q_up'], g['mla_w_kv_up'] = _mla_weight_grads(dwq, dwk, dwv)
    g['mla_q_norm'], g['mla_kv_norm'] = dqn.reshape(-1), dkvn.reshape(-1)

    s5_full = [_row(w['s5_d']), w['s5_w_glu'], _row(w['s5_b_glu'])]
    dylin, du_a, dd, g['s5_w_glu'], dbg = _rowcall_vjp(
        _s5_post_fn, [s['ylin'], (proj, S5_WIDTH, P_S5 // S5_WIDTH)], s5_full, [dy_s5], [0, 1], [0, 1, 2], tm=256,
        name=f"s5_post_bwd_{tag}", row_dtypes=[BF16, F32])
    g['s5_d'], g['s5_b_glu'] = dd.reshape(-1), dbg.reshape(-1)
    (du_b, g['s5_wb'], g['s5_wc'], da_part), side2_out = _s5_bwd(
        _seg_order(dylin), s['u_seg'], s['hs'], w['s5_wb'], w['s5_wc'], w['s5_tab_bwd'], name=f"s5_bwd_{tag}",
        side=make_side2(g) if make_side2 is not None else None)
    du_b = _time_order(du_b)
    g['s5_a'] = jnp.sum(da_part, axis=0, keepdims=True)

    def dproj_fn(g0, g1, g2, ua, ub, us, vs, ml):
        d = jnp.concatenate([g0, g1, g2, ua + ub, us, vs, ml], axis=1)
        return d, jnp.sum(d, axis=0, keepdims=True)

    dproj, db_in = _rowcall(dproj_fn, [dl0, dl1, dl2, du_a, du_b, du_sgu, dv_sgu, dmla], [], [(IN_PAD, BF16)],
                            [((1, IN_PAD), F32)], tm=256, name=f"dproj_{tag}")
    dh = _mm(dproj, w['w_in_p'], tb=True, name=f"proj_dx_{tag}", tk=1792)
    g['w_in_p'] = _mm(s['h'], dproj, ta=True, name=f"proj_dw_{tag}", tm=512, tn=1792)
    g['b_in'] = _unpermute_in(db_in).reshape(-1)
    dx, dsc1, dsh1 = _rowcall(_mod_bwd_fn, [s['x'], dh, dx_a], [s['sc1']], [(D_MODEL, F32)], [((1, D_MODEL), F32)] * 2,
                              tm=512, name=f"mod1_bwd_{tag}")
    d_ada = jnp.concatenate([dsh1, dsc1, dgt1, dsh2, dsc2, dgt2], axis=0)
    return dx, d_ada, g, side_out, side2_out


def _loss_fn(y, target):
    err = y - target
    return (err / D_MODEL, 0.5 * jnp.sum(jnp.sum(err * err, axis=1, keepdims=True), axis=0, keepdims=True) / D_MODEL)


def _step(p):
    me = 4 * lax.axis_index("x") + 2 * lax.axis_index("y") + lax.axis_index("c")
    x = p['x'][0]
    t = x.shape[0]
    rope_tabs = _rope_tables(t)

    (c_all,) = _exchange([jnp.broadcast_to(p['c'], (V7X_SUBLANES, D_MODEL))], gather=True, name="gather_c")
    c_all = c_all[:, 0, :]
    (c_act,) = _rowcall(lambda cc: (cc * _sigmoid(cc),), [c_all], [], [(D_MODEL, F32)], tm=N_DEV, name="c_silu")
    ncol = p['w_ada'].shape[2]
    b_ada_loc = lax.dynamic_slice_in_dim(p['b_ada'], me * ncol, ncol, axis=1)
    ada_cols = jnp.concatenate([_mm(c_act, p['w_ada'][l], bias=b_ada_loc[l:l + 1], name=f"ada_{l}") for l in range(DEPTH)])
    (ada_all,) = _exchange([ada_cols], gather=True, name="gather_ada")
    ada_all = ada_all.reshape(N_DEV, DEPTH, N_DEV, ncol)
    ada = lax.dynamic_index_in_dim(ada_all, me, axis=2, keepdims=False)
    ada = ada.transpose(1, 0, 2).reshape(DEPTH, 6, D_MODEL)

    mixer_w, ffn_w = SHARDED[:-2], SHARDED[-2:]

    def shards(l, group):
        return [p[n][l].astype(BF16) for n, _ in group]

    def contribs(g, group):
        return [(_w_in_contrib(g['w_in_p']) if n == 'w_in' else _contrib_from_full(g[n], kind)).astype(BF16)
                for n, kind in group]

    tables, pull_tables = _derived_tables(p, t)

    def mixer_weights(l, gathered):
        w = {n: _full_from_gathered(g, kind) for (n, kind), g in zip(mixer_w[1:], gathered[1:])}
        w['w_in_p'] = _w_in_from_gathered(gathered[0])
        w['b_in_p'] = _row(_permute_in(p['b_in'][l]))
        w['mla_wq'], w['mla_wk'], w['mla_wv'] = _mla_weights(w['mla_w_q_up'], w['mla_w_kv_up'])
        w.update(tables[l])
        for n in SMALL:
            if n != 'b_ada' and n != 'b_in':
                w[n] = p[n][l]
        return w

    saved, layers = [], []
    gathered = _exchange(shards(0, mixer_w), gather=True, name="gather_w_0", two_level=True)
    for l in range(DEPTH):
        w = mixer_weights(l, gathered)
        side = _Exchange(shards(l, ffn_w) + (shards(l + 1, mixer_w) if l + 1 < DEPTH else []), True, two_level=True)

        def add_ffn(res, w=w):
            for (n, kind), g in zip(ffn_w, res):
                w[n] = _full_from_gathered(g, kind)

        x, s, res = _layer_fwd(x, ada[l], w, rope_tabs, f"l{l}", side, add_ffn)
        gathered = res[len(ffn_w):]
        layers.append(w)
        saved.append(s)
    dy, loss_loc = _rowcall(_loss_fn, [x, p['loss_target'][0]], [], [(D_MODEL, F32)], [((1, 1), F32)], tm=256, name="loss")
    loss = lax.psum(loss_loc[0, 0], ("x", "y", "c"))

    w_in_g, other_w = mixer_w[:1], mixer_w[1:]
    d_ada, grads, landed = [None] * DEPTH, [None] * DEPTH, [[None, None, None] for _ in range(DEPTH)]
    dx, pending = dy, []
    for l in reversed(range(DEPTH)):
        make_side = lambda g, pending=pending: _Exchange(contribs(g, ffn_w) + pending, False)
        make_side2 = lambda g: _Exchange(contribs(g, other_w), False)
        dx, d_ada[l], grads[l], res, landed[l][1] = _layer_bwd(dx, saved[l], layers[l], rope_tabs, f"l{l}", make_side,
                                                               make_side2)
        landed[l][2] = res[:len(ffn_w)]
        if l + 1 < DEPTH:
            landed[l + 1][0] = res[len(ffn_w):]
        pending = contribs(grads[l], w_in_g)
    landed[0][0] = _exchange(pending, gather=False, name="scatter_g_0")
    landed = [list(a) + list(b) + list(c) for a, b, c in landed]
    d_ada = jnp.stack(d_ada).reshape(DEPTH, 6 * D_MODEL)

    (d_ada_all,) = _exchange([d_ada], gather=True, name="gather_dada")
    d_ada_cols = lax.dynamic_slice_in_dim(d_ada_all, me * ncol, ncol, axis=2)
    pad_b = ((0, V7X_LANES - N_DEV), (0, 0))
    c_act_p = jnp.pad(c_act, pad_b)
    g_w_ada = jnp.stack([_mm(c_act_p, jnp.pad(d_ada_cols[:, l], pad_b), ta=True, name=f"ada_dw_{l}") for l in range(DEPTH)])

    out = {}
    kinds = ('grad_', 'delta_', 'new_m_', 'new_v_')
    for i, (n, _) in enumerate(SHARDED):
        res = _adamw(_as2d(p[n]), [_as2d(landed[l][i], lead=1) for l in range(DEPTH)], _as2d(p['m_' + n]),
                     _as2d(p['v_' + n]), name=f"adamw_{n}")
        for kind, r in zip(kinds, res):
            out[kind + n] = r.reshape(p[n].shape)

    res = _adamw(_as2d(p['w_ada']), [_as2d(g_w_ada)], _as2d(p['m_w_ada']), _as2d(p['v_w_ada']), name="adamw_ada")
    for kind, r in zip(kinds, res):
        out[kind + 'w_ada'] = r.reshape(p['w_ada'].shape)

    small_g = pull_tables(grads)
    small_g.update({n: jnp.stack([grads[l][n] for l in range(DEPTH)]) for n in SMALL if n not in small_g and n != 'b_ada'})
    small_g['b_ada'] = d_ada
    n_small = sum(int(np.prod(p[n].shape)) for n in SMALL)
    n_pad = -(-n_small // SMALL_PAD) * SMALL_PAD
    flat = jnp.concatenate([small_g[n].reshape(-1) for n in SMALL])
    flat = jnp.pad(flat, (0, n_pad - n_small)).reshape(N_DEV, n_pad // N_DEV // 1024, 1024)
    (landed_small,) = _exchange([flat], gather=False, name="scatter_small")
    (g_all,) = _exchange([_sum_parts(landed_small, name="sum_small")], gather=True, name="gather_small")
    g_all = g_all.reshape(-1)
    off = 0
    for n in SMALL:
        size = int(np.prod(p[n].shape))
        g_n = g_all[off:off + size].reshape(p[n].shape)
        off += size
        res = _adamw(_as2d(p[n]), [_as2d(g_n)], _as2d(p['m_' + n]), _as2d(p['v_' + n]), name=f"adamw_{n}")
        for kind, r in zip(kinds, res):
            out[kind + n] = r.reshape(p[n].shape)

    outs = [loss, dx[None]]
    for kind in ('grad_', 'delta_', 'new_m_', 'new_v_'):
        outs += [out[kind + n] for n in WNAMES]
    return tuple(outs)


def kernel(x, c, w_ada, b_ada, w_in, b_in, s5_lambda_re, s5_lambda_im, s5_log_dt, s5_b_re, s5_b_im, s5_c_re, s5_c_im, s5_d, s5_w_glu, s5_b_glu, mla_q_norm, mla_w_q_up, mla_kv_norm, mla_w_kv_up, sgu_ln_g, sgu_ln_b, sgu_w_s, sgu_b_s, w_branch, w_out, ln1_g, ln1_b, ffn_w_in, ffn_w_out, ln2_g, ln2_b, loss_target, m_w_ada, m_b_ada, m_w_in, m_b_in, m_s5_lambda_re, m_s5_lambda_im, m_s5_log_dt, m_s5_b_re, m_s5_b_im, m_s5_c_re, m_s5_c_im, m_s5_d, m_s5_w_glu, m_s5_b_glu, m_mla_q_norm, m_mla_w_q_up, m_mla_kv_norm, m_mla_w_kv_up, m_sgu_ln_g, m_sgu_ln_b, m_sgu_w_s, m_sgu_b_s, m_w_branch, m_w_out, m_ln1_g, m_ln1_b, m_ffn_w_in, m_ffn_w_out, m_ln2_g, m_ln2_b, v_w_ada, v_b_ada, v_w_in, v_b_in, v_s5_lambda_re, v_s5_lambda_im, v_s5_log_dt, v_s5_b_re, v_s5_b_im, v_s5_c_re, v_s5_c_im, v_s5_d, v_s5_w_glu, v_s5_b_glu, v_mla_q_norm, v_mla_w_q_up, v_mla_kv_norm, v_mla_w_kv_up, v_sgu_ln_g, v_sgu_ln_b, v_sgu_w_s, v_sgu_b_s, v_w_branch, v_w_out, v_ln1_g, v_ln1_b, v_ffn_w_in, v_ffn_w_out, v_ln2_g, v_ln2_b):
    return _step(dict(locals()))
```
